```python
import math
import jax, jax.numpy as jnp
from jax import lax
import numpy as np

D_MODEL = 1024
BATCH = 8
SEQ = 8192
DEPTH = 1

ATTN_HEADS = 16
ATTN_KV_HEADS = 4
HEAD_DIM = 64
ATTN_WIDTH = ATTN_HEADS * HEAD_DIM
KV_WIDTH = ATTN_KV_HEADS * HEAD_DIM
WINDOW = 128
BLOCK = 128
REL_BUCKETS = 32
REL_MAX_DIST = 128
SSM_EXPAND = 2
SSM_WIDTH = SSM_EXPAND * D_MODEL
SSM_HEAD_DIM = 64
SSM_HEADS = SSM_WIDTH // SSM_HEAD_DIM
SSM_GROUPS = 4
SSM_HEADS_PER_GROUP = SSM_HEADS // SSM_GROUPS
SSM_STATE = 128
CONV_WIDTH = 4
CHUNK = 128
XBC_WIDTH = SSM_WIDTH + 2 * SSM_GROUPS * SSM_STATE
DT_MIN = 0.001
DT_MAX = 0.1
N_BRANCH = 2
IN_WIDTH = ATTN_WIDTH + 2 * KV_WIDTH + ATTN_WIDTH + SSM_WIDTH + XBC_WIDTH + SSM_HEADS + N_BRANCH * D_MODEL
EPS = 1e-6

kernel_name = "hybrid_swa_sink_ssd_gated_merge"


def rms_norm(x, w, eps=EPS):
    xf = x.astype(jnp.float32)
    xf = xf * lax.rsqrt(jnp.mean(xf * xf, axis=-1, keepdims=True) + eps)
    return (xf * w.astype(jnp.float32)).astype(x.dtype)


def t5_causal_bucket(dist):
    n = jnp.maximum(dist, 0)
    max_exact = REL_BUCKETS // 2
    nf = jnp.maximum(n, 1).astype(jnp.float32)
    large = max_exact + (jnp.log(nf / max_exact) / math.log(REL_MAX_DIST / max_exact)
                         * (REL_BUCKETS - max_exact)).astype(jnp.int32)
    large = jnp.minimum(large, REL_BUCKETS - 1)
    return jnp.where(n < max_exact, n, large)


def sliding_window_attention(q, k, v, q_norm_w, k_norm_w, rel_bias, sinks):
    b, s = q.shape[0], q.shape[1]
    nb = s // BLOCK
    grp = ATTN_HEADS // ATTN_KV_HEADS
    q = rms_norm(q, q_norm_w)
    k = rms_norm(k, k_norm_w)
    qb = q.reshape(b, nb, BLOCK, ATTN_KV_HEADS, grp, HEAD_DIM)

    def banded(t):
        tb = t.reshape(b, nb, BLOCK, ATTN_KV_HEADS, HEAD_DIM)
        prev = jnp.concatenate([jnp.zeros_like(tb[:, :1]), tb[:, :-1]], axis=1)
        return jnp.concatenate([prev, tb], axis=2)

    kb, vb = banded(k), banded(v)
    scores = jnp.einsum("bnqhgd,bnkhd->bnhgqk", qb, kb).astype(jnp.float32) * (HEAD_DIM ** -0.5)

    qi = jnp.arange(BLOCK)[:, None]
    kj = jnp.arange(2 * BLOCK)[None, :]
    dist = qi + BLOCK - kj
    key_pos = jnp.arange(nb)[:, None, None] * BLOCK - BLOCK + kj[None]
    mask = (dist >= 0) & (dist < WINDOW) & (key_pos >= 0)
    bias = rel_bias.astype(jnp.float32)[t5_causal_bucket(dist)]
    bias = bias.reshape(BLOCK, 2 * BLOCK, ATTN_KV_HEADS, grp).transpose(2, 3, 0, 1)
    scores = jnp.where(mask[None, :, None, None], scores + bias[None, None], -jnp.inf)

    sink = sinks.astype(jnp.float32).reshape(ATTN_KV_HEADS, grp)[None, None, :, :, None, None]
    m = jnp.maximum(scores.max(axis=-1, keepdims=True), sink)
    p = jnp.exp(scores - m)
    probs = p / (p.sum(axis=-1, keepdims=True) + jnp.exp(sink - m))
    out = jnp.einsum("bnhgqk,bnkhd->bnqhgd", probs.astype(v.dtype), vb)
    return out.reshape(b, s, ATTN_WIDTH)


def causal_depthwise_conv(x, w, bias):
    out = lax.conv_general_dilated(
        x, w[:, None, :].astype(x.dtype), window_strides=(1,), padding=[(CONV_WIDTH - 1, 0)],
        dimension_numbers=("NWC", "WIO", "NWC"), feature_group_count=x.shape[-1])
    return out + bias


def ssd_chunked(xs, dt, a, bm, cm):
    b, s = xs.shape[0], xs.shape[1]
    nc = s // CHUNK
    G, R, P, N = SSM_GROUPS, SSM_HEADS_PER_GROUP, SSM_HEAD_DIM, SSM_STATE
    xs = xs.reshape(b, nc, CHUNK, G, R, P).astype(jnp.float32)
    dt = dt.reshape(b, nc, CHUNK, G, R)
    bm = bm.reshape(b, nc, CHUNK, G, N).astype(jnp.float32)
    cm = cm.reshape(b, nc, CHUNK, G, N).astype(jnp.float32)
    a_cum = jnp.cumsum(dt * a, axis=2)
    xdt = xs * dt[..., None]
    causal = jnp.tril(jnp.ones((CHUNK, CHUNK), dtype=bool))[:, :, None, None]
    seg = a_cum[:, :, :, None] - a_cum[:, :, None, :]
    decay = jnp.exp(jnp.where(causal, seg, -jnp.inf))
    cb = jnp.einsum("bclgn,bcsgn->bclsg", cm, bm)
    y_diag = jnp.einsum("bclsg,bclsgr,bcsgrp->bclgrp", cb, decay, xdt)
    decay_to_end = jnp.exp(a_cum[:, :, -1:] - a_cum)
    states = jnp.einsum("bclgn,bclgr,bclgrp->bcgrpn", bm, decay_to_end, xdt)
    chunk_decay = jnp.exp(a_cum[:, :, -1])

    def step(h, inp):
        st, dec = inp
        return h * dec[..., None, None] + st, h

    h0 = jnp.zeros((b, G, R, P, N), jnp.float32)
    _, h_prev = lax.scan(step, h0, (jnp.moveaxis(states, 1, 0), jnp.moveaxis(chunk_decay, 1, 0)))
    h_prev = jnp.moveaxis(h_prev, 0, 1)
    y_off = jnp.einsum("bclgn,bcgrpn,bclgr->bclgrp", cm, h_prev, jnp.exp(a_cum))
    return (y_diag + y_off).reshape(b, s, G, R, P)


def hybrid_layer(x, c, w_ada, b_ada, norm_w, w_in, q_norm_w, k_norm_w, rel_bias, sinks,
                 conv_w, conv_b, dt_bias, a_log, d_skip, ssm_norm_w,
                 w_attn_proj, w_ssm_proj, w_out):
    b, s, _ = x.shape
    G, R, P, N = SSM_GROUPS, SSM_HEADS_PER_GROUP, SSM_HEAD_DIM, SSM_STATE
    mod = jax.nn.silu(c) @ w_ada + b_ada
    shift, scale, gate = jnp.split(mod, 3, axis=-1)
    h = rms_norm(x, norm_w) * (1 + scale[:, None]) + shift[:, None]
    proj = h @ w_in
    widths = [ATTN_WIDTH, KV_WIDTH, KV_WIDTH, ATTN_WIDTH, SSM_WIDTH, XBC_WIDTH, SSM_HEADS, D_MODEL, D_MODEL]
    q, k, v, z_a, z_m, xbc, dt_raw, g_a, g_b = jnp.split(proj, list(np.cumsum(widths)[:-1]), axis=-1)

    y_a = sliding_window_attention(q.reshape(b, s, ATTN_HEADS, HEAD_DIM),
                                   k.reshape(b, s, ATTN_KV_HEADS, HEAD_DIM),
                                   v.reshape(b, s, ATTN_KV_HEADS, HEAD_DIM),
                                   q_norm_w, k_norm_w, rel_bias, sinks)
    y_a = (y_a * jax.nn.silu(z_a)) @ w_attn_proj

    xbc = jax.nn.silu(causal_depthwise_conv(xbc, conv_w, conv_b))
    xs, bm, cm = jnp.split(xbc, [SSM_WIDTH, SSM_WIDTH + G * N], axis=-1)
    xs = xs.reshape(b, s, G, R, P)
    dt = jax.nn.softplus((dt_raw + dt_bias).astype(jnp.float32)).reshape(b, s, G, R)
    a = -jnp.exp(a_log.astype(jnp.float32)).reshape(G, R)
    y = ssd_chunked(xs, dt, a, bm.reshape(b, s, G, N), cm.reshape(b, s, G, N))
    y = y + d_skip.astype(jnp.float32).reshape(G, R)[:, :, None] * xs.astype(jnp.float32)
    y = y.reshape(b, s, SSM_WIDTH).astype(x.dtype) * jax.nn.silu(z_m)
    y = rms_norm(y.reshape(b, s, G, SSM_WIDTH // G), ssm_norm_w.reshape(G, SSM_WIDTH // G))
    y_b = y.reshape(b, s, SSM_WIDTH) @ w_ssm_proj

    merged = jax.nn.sigmoid(g_a) * y_a + jax.nn.sigmoid(g_b) * y_b
    return x + gate[:, None] * (merged @ w_out)


def _fwd_setup_inputs(seed: int = 0) -> dict:
    key = jax.random.key(seed)
    ks = jax.random.split(key, 20)
    nrm = jax.random.normal
    L, D = DEPTH, D_MODEL
    dt0 = jnp.exp(jax.random.uniform(ks[12], (L, SSM_HEADS)) * (math.log(DT_MAX) - math.log(DT_MIN)) + math.log(DT_MIN))
    return {
        "x": nrm(ks[0], (BATCH, SEQ, D), jnp.float32),
        "c": nrm(ks[1], (BATCH, D), jnp.float32),
        "w_ada": nrm(ks[2], (L, D, 3 * D)) * (0.5 * D ** -0.5),
        "b_ada": 0.01 * nrm(ks[3], (L, 3 * D)),
        "norm_w": 1 + 0.02 * nrm(ks[4], (L, D)),
        "w_in": nrm(ks[5], (L, D, IN_WIDTH)) * D ** -0.5,
        "q_norm_w": 1 + 0.02 * nrm(ks[6], (L, HEAD_DIM)),
        "k_norm_w": 1 + 0.02 * nrm(ks[7], (L, HEAD_DIM)),
        "rel_bias": 0.5 * nrm(ks[8], (REL_BUCKETS, ATTN_HEADS)),
        "sinks": 0.5 * nrm(ks[9], (L, ATTN_HEADS)),
        "conv_w": nrm(ks[10], (L, CONV_WIDTH, XBC_WIDTH)) * CONV_WIDTH ** -0.5,
        "conv_b": 0.01 * nrm(ks[11], (L, XBC_WIDTH)),
        "dt_bias": dt0 + jnp.log(-jnp.expm1(-dt0)),
        "a_log": jnp.log(jax.random.uniform(ks[13], (L, SSM_HEADS), minval=1.0, maxval=16.0)),
        "d_skip": 1 + 0.1 * nrm(ks[14], (L, SSM_HEADS)),
        "ssm_norm_w": 1 + 0.02 * nrm(ks[15], (L, SSM_WIDTH)),
        "w_attn_proj": nrm(ks[16], (L, ATTN_WIDTH, D)) * ATTN_WIDTH ** -0.5,
        "w_ssm_proj": nrm(ks[17], (L, SSM_WIDTH, D)) * SSM_WIDTH ** -0.5,
        "w_out": nrm(ks[18], (L, D, D)) * D ** -0.5,
    }


def _fwd_reference(x, c, w_ada, b_ada, norm_w, w_in, q_norm_w, k_norm_w, rel_bias, sinks,
              conv_w, conv_b, dt_bias, a_log, d_skip, ssm_norm_w,
              w_attn_proj, w_ssm_proj, w_out):
    for i in range(DEPTH):
        x = hybrid_layer(x, c, w_ada[i], b_ada[i], norm_w[i], w_in[i], q_norm_w[i], k_norm_w[i],
                         rel_bias, sinks[i], conv_w[i], conv_b[i], dt_bias[i], a_log[i], d_skip[i],
                         ssm_norm_w[i], w_attn_proj[i], w_ssm_proj[i], w_out[i])
    return x


import jax as _jax
import jax.numpy as _jnp

TWIN_FORMAT = 'train_step'
FWD_PARAMS = ['x', 'c', 'w_ada', 'b_ada', 'norm_w', 'w_in', 'q_norm_w', 'k_norm_w', 'rel_bias', 'sinks', 'conv_w', 'conv_b', 'dt_bias', 'a_log', 'd_skip', 'ssm_norm_w', 'w_attn_proj', 'w_ssm_proj', 'w_out']
TWIN_WEIGHTS = ['w_ada', 'b_ada', 'norm_w', 'w_in', 'q_norm_w', 'k_norm_w', 'rel_bias', 'sinks', 'conv_w', 'conv_b', 'dt_bias', 'a_log', 'd_skip', 'ssm_norm_w', 'w_attn_proj', 'w_ssm_proj', 'w_out']
TWIN_DIFF_INPUT = 'x'
TWIN_INPUTS = ['x', 'c', 'w_ada', 'b_ada', 'norm_w', 'w_in', 'q_norm_w', 'k_norm_w', 'rel_bias', 'sinks', 'conv_w', 'conv_b', 'dt_bias', 'a_log', 'd_skip', 'ssm_norm_w', 'w_attn_proj', 'w_ssm_proj', 'w_out', 'loss_target', 'm_w_ada', 'm_b_ada', 'm_norm_w', 'm_w_in', 'm_q_norm_w', 'm_k_norm_w', 'm_rel_bias', 'm_sinks', 'm_conv_w', 'm_conv_b', 'm_dt_bias', 'm_a_log', 'm_d_skip', 'm_ssm_norm_w', 'm_w_attn_proj', 'm_w_ssm_proj', 'm_w_out', 'v_w_ada', 'v_b_ada', 'v_norm_w', 'v_w_in', 'v_q_norm_w', 'v_k_norm_w', 'v_rel_bias', 'v_sinks', 'v_conv_w', 'v_conv_b', 'v_dt_bias', 'v_a_log', 'v_d_skip', 'v_ssm_norm_w', 'v_w_attn_proj', 'v_w_ssm_proj', 'v_w_out']
TWIN_OUTPUTS = ['loss', 'grad_x', 'grad_w_ada', 'grad_b_ada', 'grad_norm_w', 'grad_w_in', 'grad_q_norm_w', 'grad_k_norm_w', 'grad_rel_bias', 'grad_sinks', 'grad_conv_w', 'grad_conv_b', 'grad_dt_bias', 'grad_a_log', 'grad_d_skip', 'grad_ssm_norm_w', 'grad_w_attn_proj', 'grad_w_ssm_proj', 'grad_w_out', 'delta_w_ada', 'delta_b_ada', 'delta_norm_w', 'delta_w_in', 'delta_q_norm_w', 'delta_k_norm_w', 'delta_rel_bias', 'delta_sinks', 'delta_conv_w', 'delta_conv_b', 'delta_dt_bias', 'delta_a_log', 'delta_d_skip', 'delta_ssm_norm_w', 'delta_w_attn_proj', 'delta_w_ssm_proj', 'delta_w_out', 'new_m_w_ada', 'new_m_b_ada', 'new_m_norm_w', 'new_m_w_in', 'new_m_q_norm_w', 'new_m_k_norm_w', 'new_m_rel_bias', 'new_m_sinks', 'new_m_conv_w', 'new_m_conv_b', 'new_m_dt_bias', 'new_m_a_log', 'new_m_d_skip', 'new_m_ssm_norm_w', 'new_m_w_attn_proj', 'new_m_w_ssm_proj', 'new_m_w_out', 'new_v_w_ada', 'new_v_b_ada', 'new_v_norm_w', 'new_v_w_in', 'new_v_q_norm_w', 'new_v_k_norm_w', 'new_v_rel_bias', 'new_v_sinks', 'new_v_conv_w', 'new_v_conv_b', 'new_v_dt_bias', 'new_v_a_log', 'new_v_d_skip', 'new_v_ssm_norm_w', 'new_v_w_attn_proj', 'new_v_w_ssm_proj', 'new_v_w_out']
TWIN_LEAF_KINDS = {'loss': 'loss', 'grad_x': 'grad_x', 'grad_w_ada': 'grad_w', 'grad_b_ada': 'grad_w', 'grad_norm_w': 'grad_w', 'grad_w_in': 'grad_w', 'grad_q_norm_w': 'grad_w', 'grad_k_norm_w': 'grad_w', 'grad_rel_bias': 'grad_w', 'grad_sinks': 'grad_w', 'grad_conv_w': 'grad_w', 'grad_conv_b': 'grad_w', 'grad_dt_bias': 'grad_w', 'grad_a_log': 'grad_w', 'grad_d_skip': 'grad_w', 'grad_ssm_norm_w': 'grad_w', 'grad_w_attn_proj': 'grad_w', 'grad_w_ssm_proj': 'grad_w', 'grad_w_out': 'grad_w', 'delta_w_ada': 'delta_w', 'delta_b_ada': 'delta_w', 'delta_norm_w': 'delta_w', 'delta_w_in': 'delta_w', 'delta_q_norm_w': 'delta_w', 'delta_k_norm_w': 'delta_w', 'delta_rel_bias': 'delta_w', 'delta_sinks': 'delta_w', 'delta_conv_w': 'delta_w', 'delta_conv_b': 'delta_w', 'delta_dt_bias': 'delta_w', 'delta_a_log': 'delta_w', 'delta_d_skip': 'delta_w', 'delta_ssm_norm_w': 'delta_w', 'delta_w_attn_proj': 'delta_w', 'delta_w_ssm_proj': 'delta_w', 'delta_w_out': 'delta_w', 'new_m_w_ada': 'new_m', 'new_m_b_ada': 'new_m', 'new_m_norm_w': 'new_m', 'new_m_w_in': 'new_m', 'new_m_q_norm_w': 'new_m', 'new_m_k_norm_w': 'new_m', 'new_m_rel_bias': 'new_m', 'new_m_sinks': 'new_m', 'new_m_conv_w': 'new_m', 'new_m_conv_b': 'new_m', 'new_m_dt_bias': 'new_m', 'new_m_a_log': 'new_m', 'new_m_d_skip': 'new_m', 'new_m_ssm_norm_w': 'new_m', 'new_m_w_attn_proj': 'new_m', 'new_m_w_ssm_proj': 'new_m', 'new_m_w_out': 'new_m', 'new_v_w_ada': 'new_v', 'new_v_b_ada': 'new_v', 'new_v_norm_w': 'new_v', 'new_v_w_in': 'new_v', 'new_v_q_norm_w': 'new_v', 'new_v_k_norm_w': 'new_v', 'new_v_rel_bias': 'new_v', 'new_v_sinks': 'new_v', 'new_v_conv_w': 'new_v', 'new_v_conv_b': 'new_v', 'new_v_dt_bias': 'new_v', 'new_v_a_log': 'new_v', 'new_v_d_skip': 'new_v', 'new_v_ssm_norm_w': 'new_v', 'new_v_w_attn_proj': 'new_v', 'new_v_w_ssm_proj': 'new_v', 'new_v_w_out': 'new_v'}


def _forward(args):
    return _fwd_reference(*[args[k] for k in FWD_PARAMS])


def _output_shape():
    def fwd():
        inp = _fwd_setup_inputs(0)
        return _fwd_reference(*[inp[k] for k in FWD_PARAMS])
    out = _jax.eval_shape(fwd)
    return out.shape, out.dtype

N_MICROBATCH = 1
ADAM_LR = 0.001
ADAM_B1 = 0.9
ADAM_B2 = 0.999
ADAM_EPS = 1e-08
ADAM_WD = 0.01
ADAM_STEP = 10
PER_EXAMPLE_BATCH_AXIS = {'x': 0, 'c': 0, 'loss_target': 0}
SHARED_INPUTS = []
_WEIGHT_DTYPES = {'w_ada': _jnp.float32, 'b_ada': _jnp.float32, 'norm_w': _jnp.float32, 'w_in': _jnp.float32, 'q_norm_w': _jnp.float32, 'k_norm_w': _jnp.float32, 'rel_bias': _jnp.float32, 'sinks': _jnp.float32, 'conv_w': _jnp.float32, 'conv_b': _jnp.float32, 'dt_bias': _jnp.float32, 'a_log': _jnp.float32, 'd_skip': _jnp.float32, 'ssm_norm_w': _jnp.float32, 'w_attn_proj': _jnp.float32, 'w_ssm_proj': _jnp.float32, 'w_out': _jnp.float32}
MOMENT_SCALE = {'w_ada': 7.388586e-01, 'b_ada': 1.632251e+00, 'norm_w': 2.320678e-01, 'w_in': 4.333790e-02, 'q_norm_w': 1.803428e-01, 'k_norm_w': 1.806376e-01, 'rel_bias': 1.260647e-02, 'sinks': 3.256386e-02, 'conv_w': 6.152173e-02, 'conv_b': 1.464482e-01, 'dt_bias': 1.180146e-01, 'a_log': 6.997130e-01, 'd_skip': 4.263980e-01, 'ssm_norm_w': 1.359256e+00, 'w_attn_proj': 2.494353e-02, 'w_ssm_proj': 1.341638e-01, 'w_out': 1.162864e-01}


def _to_microbatches(a, axis):
    t = _jnp.moveaxis(a, axis, 0)
    t = t.reshape((N_MICROBATCH, t.shape[0] // N_MICROBATCH) + t.shape[1:])
    return _jnp.moveaxis(t, 1, axis + 1)


def setup_inputs(seed: int = 0) -> dict:
    inp = _fwd_setup_inputs(seed)
    key = _jax.random.fold_in(_jax.random.key(seed), 7919)
    shape, _ = _output_shape()
    out = dict(inp)
    out["loss_target"] = _jax.random.normal(_jax.random.fold_in(key, 0), shape, _jnp.float32)
    for i, name in enumerate(TWIN_WEIGHTS):
        w = inp[name].astype(_jnp.float32)
        if MOMENT_SCALE is None:
            s = _jnp.sqrt(_jnp.mean(_jnp.square(w)) + 1e-30)
        else:
            s = MOMENT_SCALE[name]
        km, kv = _jax.random.split(_jax.random.fold_in(key, i + 1))
        out[name] = w
        out["m_" + name] = s * _jax.random.normal(km, w.shape, _jnp.float32)
        out["v_" + name] = (s * s) * _jax.random.uniform(kv, w.shape, _jnp.float32, 0.5, 1.5)
    if N_MICROBATCH > 1:
        for name, axis in PER_EXAMPLE_BATCH_AXIS.items():
            out[name] = _to_microbatches(out[name], axis)
    return {'x': out['x'], 'c': out['c'], 'w_ada': out['w_ada'], 'b_ada': out['b_ada'], 'norm_w': out['norm_w'], 'w_in': out['w_in'], 'q_norm_w': out['q_norm_w'], 'k_norm_w': out['k_norm_w'], 'rel_bias': out['rel_bias'], 'sinks': out['sinks'], 'conv_w': out['conv_w'], 'conv_b': out['conv_b'], 'dt_bias': out['dt_bias'], 'a_log': out['a_log'], 'd_skip': out['d_skip'], 'ssm_norm_w': out['ssm_norm_w'], 'w_attn_proj': out['w_attn_proj'], 'w_ssm_proj': out['w_ssm_proj'], 'w_out': out['w_out'], 'loss_target': out['loss_target'], 'm_w_ada': out['m_w_ada'], 'm_b_ada': out['m_b_ada'], 'm_norm_w': out['m_norm_w'], 'm_w_in': out['m_w_in'], 'm_q_norm_w': out['m_q_norm_w'], 'm_k_norm_w': out['m_k_norm_w'], 'm_rel_bias': out['m_rel_bias'], 'm_sinks': out['m_sinks'], 'm_conv_w': out['m_conv_w'], 'm_conv_b': out['m_conv_b'], 'm_dt_bias': out['m_dt_bias'], 'm_a_log': out['m_a_log'], 'm_d_skip': out['m_d_skip'], 'm_ssm_norm_w': out['m_ssm_norm_w'], 'm_w_attn_proj': out['m_w_attn_proj'], 'm_w_ssm_proj': out['m_w_ssm_proj'], 'm_w_out': out['m_w_out'], 'v_w_ada': out['v_w_ada'], 'v_b_ada': out['v_b_ada'], 'v_norm_w': out['v_norm_w'], 'v_w_in': out['v_w_in'], 'v_q_norm_w': out['v_q_norm_w'], 'v_k_norm_w': out['v_k_norm_w'], 'v_rel_bias': out['v_rel_bias'], 'v_sinks': out['v_sinks'], 'v_conv_w': out['v_conv_w'], 'v_conv_b': out['v_conv_b'], 'v_dt_bias': out['v_dt_bias'], 'v_a_log': out['v_a_log'], 'v_d_skip': out['v_d_skip'], 'v_ssm_norm_w': out['v_ssm_norm_w'], 'v_w_attn_proj': out['v_w_attn_proj'], 'v_w_ssm_proj': out['v_w_ssm_proj'], 'v_w_out': out['v_w_out']}


def _loss(weights, diff, rest, loss_target):
    with _jax.named_scope("forward"):
        args = {**rest, TWIN_DIFF_INPUT: diff, **{k: w.astype(_WEIGHT_DTYPES[k]) for k, w in weights.items()}}
        y = _forward(args)
    with _jax.named_scope("loss_head"):
        err = _jnp.square(y.astype(_jnp.float32) - loss_target)
        return 0.5 * _jnp.sum(_jnp.mean(err, axis=-1)) if err.ndim else 0.5 * err


def _adamw(w, g, m, v):
    m = ADAM_B1 * m + (1.0 - ADAM_B1) * g
    v = ADAM_B2 * v + (1.0 - ADAM_B2) * _jnp.square(g)
    m_hat = m / (1.0 - ADAM_B1 ** ADAM_STEP)
    v_hat = v / (1.0 - ADAM_B2 ** ADAM_STEP)
    delta = -ADAM_LR * (m_hat / (_jnp.sqrt(v_hat) + ADAM_EPS) + ADAM_WD * w)
    return delta, m, v


def reference(x, c, w_ada, b_ada, norm_w, w_in, q_norm_w, k_norm_w, rel_bias, sinks, conv_w, conv_b, dt_bias, a_log, d_skip, ssm_norm_w, w_attn_proj, w_ssm_proj, w_out, loss_target, m_w_ada, m_b_ada, m_norm_w, m_w_in, m_q_norm_w, m_k_norm_w, m_rel_bias, m_sinks, m_conv_w, m_conv_b, m_dt_bias, m_a_log, m_d_skip, m_ssm_norm_w, m_w_attn_proj, m_w_ssm_proj, m_w_out, v_w_ada, v_b_ada, v_norm_w, v_w_in, v_q_norm_w, v_k_norm_w, v_rel_bias, v_sinks, v_conv_w, v_conv_b, v_dt_bias, v_a_log, v_d_skip, v_ssm_norm_w, v_w_attn_proj, v_w_ssm_proj, v_w_out):
    given = dict(x=x, c=c, w_ada=w_ada, b_ada=b_ada, norm_w=norm_w, w_in=w_in, q_norm_w=q_norm_w, k_norm_w=k_norm_w, rel_bias=rel_bias, sinks=sinks, conv_w=conv_w, conv_b=conv_b, dt_bias=dt_bias, a_log=a_log, d_skip=d_skip, ssm_norm_w=ssm_norm_w, w_attn_proj=w_attn_proj, w_ssm_proj=w_ssm_proj, w_out=w_out, loss_target=loss_target, m_w_ada=m_w_ada, m_b_ada=m_b_ada, m_norm_w=m_norm_w, m_w_in=m_w_in, m_q_norm_w=m_q_norm_w, m_k_norm_w=m_k_norm_w, m_rel_bias=m_rel_bias, m_sinks=m_sinks, m_conv_w=m_conv_w, m_conv_b=m_conv_b, m_dt_bias=m_dt_bias, m_a_log=m_a_log, m_d_skip=m_d_skip, m_ssm_norm_w=m_ssm_norm_w, m_w_attn_proj=m_w_attn_proj, m_w_ssm_proj=m_w_ssm_proj, m_w_out=m_w_out, v_w_ada=v_w_ada, v_b_ada=v_b_ada, v_norm_w=v_norm_w, v_w_in=v_w_in, v_q_norm_w=v_q_norm_w, v_k_norm_w=v_k_norm_w, v_rel_bias=v_rel_bias, v_sinks=v_sinks, v_conv_w=v_conv_w, v_conv_b=v_conv_b, v_dt_bias=v_dt_bias, v_a_log=v_a_log, v_d_skip=v_d_skip, v_ssm_norm_w=v_ssm_norm_w, v_w_attn_proj=v_w_attn_proj, v_w_ssm_proj=v_w_ssm_proj, v_w_out=v_w_out)
    weights = {n: given[n] for n in TWIN_WEIGHTS}
    shared = {n: given[n] for n in SHARED_INPUTS}
    per_example = {n: given[n] for n in ['x', 'c']}
    grad_fn = _jax.value_and_grad(_loss, argnums=(0, 1))

    def one_microbatch(ex, loss_target):
        ex = dict(ex)
        diff = ex.pop(TWIN_DIFF_INPUT)
        return grad_fn(weights, diff, {**shared, **ex}, loss_target)

    if N_MICROBATCH == 1:
        loss, (grad_w, grad_x) = one_microbatch(per_example, given["loss_target"])
    else:
        def body(carry, xs):
            loss_sum, grad_sum = carry
            l_k, (gw_k, gx_k) = one_microbatch(xs[0], xs[1])
            with _jax.named_scope("update"):
                return (loss_sum + l_k, _jax.tree.map(_jnp.add, grad_sum, gw_k)), gx_k

        init = (_jnp.zeros((), _jnp.float32), _jax.tree.map(_jnp.zeros_like, weights))
        (loss, grad_w), grad_x = _jax.lax.scan(body, init, (per_example, given["loss_target"]))
    with _jax.named_scope("update"):
        delta_w, new_m, new_v = {}, {}, {}
        for n in TWIN_WEIGHTS:
            delta_w[n], new_m[n], new_v[n] = _adamw(weights[n], grad_w[n], given["m_" + n], given["v_" + n])
    return (loss, grad_x, *[grad_w[n] for n in TWIN_WEIGHTS], *[delta_w[n] for n in TWIN_WEIGHTS],
            *[new_m[n] for n in TWIN_WEIGHTS], *[new_v[n] for n in TWIN_WEIGHTS])
```

```python
import functools
import math

import numpy as np
import jax
import jax.numpy as jnp
from jax import lax
from jax.experimental import pallas as pl
from jax.experimental.pallas import tpu as pltpu

F32 = jnp.float32
BF16 = jnp.bfloat16
MESH = pl.DeviceIdType.MESH

D = 1024
HQ, HKV, GRP, DH = 16, 4, 4, 64
BLK = 128
NBUCKET, MAXDIST = 32, 128
SSM_W, SH, SG, SR, SP, SN = 2048, 32, 4, 8, 64, 128
CONV_K = 4
XBC = SSM_W + 2 * SG * SN
IN_W = 9760
EPS = 1e-6
NEG = -1e30
SCALE = DH ** -0.5

C_Q, C_ZA, C_GA, C_GB, C_ZM, C_XBC, C_K, C_V, C_DT = 0, 1024, 2048, 3072, 4096, 6144, 9216, 9472, 9728
NP = 9984
TN = 768

R_IN, R_AT, R_SS, R_OU = 2440, 256, 512, 256
ROWS = 3488
HALF = ROWS // 2

ADAM_LR, ADAM_B1, ADAM_B2, ADAM_EPS, ADAM_WD, ADAM_STEP = 0.001, 0.9, 0.999, 1e-08, 0.01, 10

VMEM_LIMIT = 56 * 1024 * 1024


def _cp(sem=None):
    if sem is None:
        return pltpu.CompilerParams(vmem_limit_bytes=VMEM_LIMIT)
    return pltpu.CompilerParams(dimension_semantics=sem, vmem_limit_bytes=VMEM_LIMIT)


def _sig(x):
    return 1.0 / (1.0 + jnp.exp(-x))


def _dot(a, b):
    return jnp.dot(a, b, preferred_element_type=F32)


def _dot_nt(a, b):
    return lax.dot_general(a, b, (((1,), (1,)), ((), ())), preferred_element_type=F32)


def _dot_tn(a, b):
    return lax.dot_general(a, b, (((0,), (0,)), ((), ())), preferred_element_type=F32)


def _rsum(x):
    return jnp.sum(x, axis=-1, keepdims=True)


def _csum(x):
    return jnp.sum(x, axis=0, keepdims=True)


def _asum(x):
    return _csum(_rsum(x))


def _full(shape):
    nd = len(shape)
    return pl.BlockSpec(shape, lambda *_: (0,) * nd)


def ada_mod(c_all, w_ada_sh, b_ada_sh):
    def body(c_ref, w_ref, b_ref, o_ref):
        cv = c_ref[...]
        s = cv * _sig(cv)
        o_ref[...] = jnp.dot(s, w_ref[...], preferred_element_type=F32,
                             precision=lax.Precision.HIGHEST) + b_ref[...]

    n = w_ada_sh.shape[1]
    return pl.pallas_call(body, name="ada_mod", out_shape=jax.ShapeDtypeStruct((8, n), F32),
                          compiler_params=_cp())(c_all, w_ada_sh, b_ada_sh)


def ada_grad(c_all, dmod_sh):
    def body(c_ref, d_ref, o_ref):
        cv = c_ref[...]
        s = cv * _sig(cv)
        o_ref[...] = lax.dot_general(s, d_ref[...], (((0,), (0,)), ((), ())), preferred_element_type=F32,
                                     precision=lax.Precision.HIGHEST)

    n = dmod_sh.shape[1]
    return pl.pallas_call(body, name="ada_grad", out_shape=jax.ShapeDtypeStruct((D, n), F32),
                          compiler_params=_cp())(c_all, dmod_sh)


def norm_proj(x, norm_w, scale, shift, wcat):
    t = x.shape[0]
    tm = min(t, 1024)

    def body(x_ref, nw_ref, sc_ref, sh_ref, w_ref, p_ref, h_ref, hs):
        @pl.when(pl.program_id(1) == 0)
        def _():
            xv = x_ref[...]
            r = lax.rsqrt(jnp.mean(xv * xv, axis=-1, keepdims=True) + EPS)
            h = (xv * r) * nw_ref[...]
            h = h * (1.0 + sc_ref[...]) + sh_ref[...]
            hb = h.astype(BF16)
            hs[...] = hb
            h_ref[...] = hb

        p_ref[...] = _dot(hs[...], w_ref[...])

    vec = pl.BlockSpec((1, D), lambda i, j: (0, 0))
    return pl.pallas_call(
        body, name="norm_proj", grid=(t // tm, NP // TN),
        in_specs=[pl.BlockSpec((tm, D), lambda i, j: (i, 0)), vec, vec, vec,
                  pl.BlockSpec((D, TN), lambda i, j: (0, j))],
        out_specs=[pl.BlockSpec((tm, TN), lambda i, j: (i, j)), pl.BlockSpec((tm, D), lambda i, j: (i, 0))],
        out_shape=[jax.ShapeDtypeStruct((t, NP), F32), jax.ShapeDtypeStruct((t, D), BF16)],
        scratch_shapes=[pltpu.VMEM((tm, D), BF16)],
        compiler_params=_cp(("parallel", "arbitrary")),
    )(x, norm_w, scale, shift, wcat)


def _bucket_table():
    qi = jnp.arange(BLK)[:, None]
    kj = jnp.arange(2 * BLK)[None, :]
    dist = qi + BLK - kj
    n = jnp.maximum(dist, 0)
    max_exact = NBUCKET // 2
    nf = jnp.maximum(n, 1).astype(F32)
    large = max_exact + (jnp.log(nf / max_exact) / math.log(MAXDIST / max_exact)
                         * (NBUCKET - max_exact)).astype(jnp.int32)
    large = jnp.minimum(large, NBUCKET - 1)
    bucket = jnp.where(n < max_exact, n, large).astype(jnp.int32)
    valid = ((dist >= 0) & (dist < BLK)).astype(jnp.int32)
    return bucket, valid


def bias_expand(rel_bias, bucket, valid):
    def body(rb_ref, bk_ref, va_ref, o_ref):
        hd = pl.program_id(0)
        bk = bk_ref[...]

        def step(b, acc):
            return jnp.where(bk == b, rb_ref[b, hd], acc)

        acc = lax.fori_loop(0, NBUCKET, step, jnp.zeros((BLK, 2 * BLK), F32))
        o_ref[0] = jnp.where(va_ref[...] > 0, acc, NEG)

    return pl.pallas_call(
        body, name="bias_expand", grid=(HQ,),
        in_specs=[pl.BlockSpec(memory_space=pltpu.SMEM), _full((BLK, 2 * BLK)), _full((BLK, 2 * BLK))],
        out_specs=pl.BlockSpec((1, BLK, 2 * BLK), lambda h: (h, 0, 0)),
        out_shape=jax.ShapeDtypeStruct((HQ, BLK, 2 * BLK), F32),
        compiler_params=_cp(("arbitrary",)),
    )(rel_bias, bucket, valid)


def bias_reduce(dacc, bucket):
    def body(d_ref, bk_ref, o_ref):
        bk = bk_ref[...]
        lane = lax.broadcasted_iota(jnp.int32, (1, 128), 1)
        for hd in range(HQ):
            dv = d_ref[hd]

            def step(b, row):
                s = _asum(jnp.where(bk == b, dv, 0.0))
                return jnp.where(lane == b, s, row)

            o_ref[hd:hd + 1, :] = lax.fori_loop(0, NBUCKET, step, jnp.zeros((1, 128), F32))

    return pl.pallas_call(body, name="bias_reduce", out_shape=jax.ShapeDtypeStruct((HQ, 128), F32),
                          compiler_params=_cp())(dacc, bucket)


def _hnorm(v, w):
    r = lax.rsqrt(jnp.mean(v * v, axis=-1, keepdims=True) + EPS)
    return v * r * w, r


def attn_fwd(proj, biasm, q_norm_w, k_norm_w, sinks_p):
    t = proj.shape[0]
    nb = t // BLK

    def body(q_ref, kc_ref, kp_ref, vc_ref, vp_ref, bm_ref, qw_ref, kw_ref, sk_ref, o_ref, lse_ref):
        n = pl.program_id(0)
        k2 = jnp.concatenate([kp_ref[...], kc_ref[...]], axis=0)
        v2 = jnp.concatenate([vp_ref[...], vc_ref[...]], axis=0)
        col = lax.broadcasted_iota(jnp.int32, (BLK, 2 * BLK), 1)
        kill = jnp.logical_and(n == 0, col < BLK)
        lane = lax.broadcasted_iota(jnp.int32, (BLK, 128), 1)
        lse_acc = jnp.zeros((BLK, 128), F32)
        qw = qw_ref[...]
        kw = kw_ref[...]
        for j in range(HKV):
            kn = _hnorm(k2[:, DH * j:DH * (j + 1)], kw)[0].astype(BF16)
            vj = v2[:, DH * j:DH * (j + 1)].astype(BF16)
            for g in range(GRP):
                hd = GRP * j + g
                qn = _hnorm(q_ref[:, DH * hd:DH * (hd + 1)], qw)[0].astype(BF16)
                s = _dot_nt(qn, kn) * SCALE + bm_ref[hd]
                s = jnp.where(kill, NEG, s)
                sk = sk_ref[:, hd:hd + 1]
                m = jnp.maximum(jnp.max(s, axis=-1, keepdims=True), sk)
                p = jnp.exp(s - m)
                den = _rsum(p) + jnp.exp(sk - m)
                pr = p / den
                o_ref[:, DH * hd:DH * (hd + 1)] = _dot(pr.astype(BF16), vj)
                lse_acc = jnp.where(lane == hd, m + jnp.log(den), lse_acc)
        lse_ref[...] = lse_acc

    kblk, vblk = C_K // 256, C_V // 256
    prev = lambda n: jnp.maximum(n - 1, 0)
    return pl.pallas_call(
        body, name="attn_fwd", grid=(nb,),
        in_specs=[pl.BlockSpec((BLK, D), lambda n: (n, 0)),
                  pl.BlockSpec((BLK, 256), lambda n: (n, kblk)),
                  pl.BlockSpec((BLK, 256), lambda n: (prev(n), kblk)),
                  pl.BlockSpec((BLK, 256), lambda n: (n, vblk)),
                  pl.BlockSpec((BLK, 256), lambda n: (prev(n), vblk)),
                  _full((HQ, BLK, 2 * BLK)), _full((1, DH)), _full((1, DH)), _full((1, 128))],
        out_specs=[pl.BlockSpec((BLK, D), lambda n: (n, 0)), pl.BlockSpec((BLK, 128), lambda n: (n, 0))],
        out_shape=[jax.ShapeDtypeStruct((t, D), F32), jax.ShapeDtypeStruct((t, 128), F32)],
        compiler_params=_cp(("parallel",)),
    )(proj, proj, proj, proj, proj, biasm, q_norm_w, k_norm_w, sinks_p)


def attn_bwd(proj, ao, dao, lse, biasm, q_norm_w, k_norm_w, sinks_p):
    t = proj.shape[0]
    nb = t // BLK

    def body(qc_ref, qn_ref, kc_ref, kp_ref, vc_ref, vp_ref, oc_ref, on_ref, dc_ref, dn_ref, lc_ref, ln_ref,
             bm_ref, qw_ref, kw_ref, sk_ref, dq_ref, dk_ref, dv_ref, dqw_ref, dkw_ref, dsk_ref, dacc_ref):
        n = pl.program_id(0)

        @pl.when(n == 0)
        def _():
            dqw_ref[...] = jnp.zeros_like(dqw_ref)
            dkw_ref[...] = jnp.zeros_like(dkw_ref)
            dsk_ref[...] = jnp.zeros_like(dsk_ref)
            dacc_ref[...] = jnp.zeros_like(dacc_ref)

        has_next = n < nb - 1
        col = lax.broadcasted_iota(jnp.int32, (BLK, 2 * BLK), 1)
        kill = jnp.logical_and(n == 0, col < BLK)
        lane = lax.broadcasted_iota(jnp.int32, (1, 128), 1)
        qw = qw_ref[...]
        kw = kw_ref[...]
        dqw = jnp.zeros((1, DH), F32)
        dkw = jnp.zeros((1, DH), F32)
        dsk = jnp.zeros((1, 128), F32)
        for j in range(HKV):
            sl = slice(DH * j, DH * (j + 1))
            kc = kc_ref[:, sl]
            knc, rkc = _hnorm(kc, kw)
            khat = kc * rkc
            knp = _hnorm(kp_ref[:, sl], kw)[0]
            kncb = knc.astype(BF16)
            kn2 = jnp.concatenate([knp.astype(BF16), kncb], axis=0)
            vcb = vc_ref[:, sl].astype(BF16)
            v2 = jnp.concatenate([vp_ref[:, sl].astype(BF16), vcb], axis=0)
            dkn = jnp.zeros((BLK, DH), F32)
            dvj = jnp.zeros((BLK, DH), F32)
            for g in range(GRP):
                hd = GRP * j + g
                hs = slice(DH * hd, DH * (hd + 1))
                bm = bm_ref[hd]
                sk = sk_ref[:, hd:hd + 1]
                qh = qc_ref[:, hs]
                qn, rq = _hnorm(qh, qw)
                qnb = qn.astype(BF16)
                s = jnp.where(kill, NEG, _dot_nt(qnb, kn2) * SCALE + bm)
                ls = lc_ref[:, hd:hd + 1]
                p = jnp.exp(s - ls)
                doh = dc_ref[:, hs]
                dob = doh.astype(BF16)
                delta = _rsum(doh * oc_ref[:, hs])
                ds = p * (_dot_nt(dob, v2) - delta)
                dacc_ref[hd] += ds
                dsk = dsk + jnp.where(lane == hd, _asum(-jnp.exp(sk - ls) * delta), 0.0)
                dsb = ds.astype(BF16)
                dqn = _dot(dsb, kn2) * SCALE
                qhat = qh * rq
                dqhat = dqn * qw
                dq_ref[:, hs] = (rq * (dqhat - qhat * jnp.mean(dqhat * qhat, axis=-1, keepdims=True))).astype(BF16)
                dqw = dqw + _csum(dqn * qhat)
                dkn = dkn + _dot_tn(dsb[:, BLK:], qnb) * SCALE
                dvj = dvj + _dot_tn(p[:, BLK:].astype(BF16), dob)
                qn2 = _hnorm(qn_ref[:, hs], qw)[0].astype(BF16)
                s2 = _dot_nt(qn2, kncb) * SCALE + bm[:, :BLK]
                p2 = jnp.where(has_next, jnp.exp(s2 - ln_ref[:, hd:hd + 1]), 0.0)
                do2 = dn_ref[:, hs]
                do2b = do2.astype(BF16)
                delta2 = _rsum(do2 * on_ref[:, hs])
                ds2 = (p2 * (_dot_nt(do2b, vcb) - delta2)).astype(BF16)
                dkn = dkn + _dot_tn(ds2, qn2) * SCALE
                dvj = dvj + _dot_tn(p2.astype(BF16), do2b)
            dkhat = dkn * kw
            dk_ref[:, sl] = (rkc * (dkhat - khat * jnp.mean(dkhat * khat, axis=-1, keepdims=True))).astype(BF16)
            dkw = dkw + _csum(dkn * khat)
            dv_ref[:, sl] = dvj.astype(BF16)
        dqw_ref[...] += dqw
        dkw_ref[...] += dkw
        dsk_ref[...] += dsk

    kblk, vblk = C_K // 256, C_V // 256
    prev = lambda n: jnp.maximum(n - 1, 0)
    nxt = lambda n: jnp.minimum(n + 1, nb - 1)
    row = lambda w: pl.BlockSpec((BLK, w), lambda n: (n, 0))
    rown = lambda w: pl.BlockSpec((BLK, w), lambda n: (nxt(n), 0))
    return pl.pallas_call(
        body, name="attn_bwd", grid=(nb,),
        in_specs=[row(D), rown(D),
                  pl.BlockSpec((BLK, 256), lambda n: (n, kblk)), pl.BlockSpec((BLK, 256), lambda n: (prev(n), kblk)),
                  pl.BlockSpec((BLK, 256), lambda n: (n, vblk)), pl.BlockSpec((BLK, 256), lambda n: (prev(n), vblk)),
                  row(D), rown(D), row(D), rown(D), row(128), rown(128),
                  _full((HQ, BLK, 2 * BLK)), _full((1, DH)), _full((1, DH)), _full((1, 128))],
        out_specs=[row(D), row(256), row(256), _full((1, DH)), _full((1, DH)), _full((1, 128)),
                   _full((HQ, BLK, 2 * BLK))],
        out_shape=[jax.ShapeDtypeStruct((t, D), BF16), jax.ShapeDtypeStruct((t, 256), BF16),
                   jax.ShapeDtypeStruct((t, 256), BF16), jax.ShapeDtypeStruct((1, DH), F32),
                   jax.ShapeDtypeStruct((1, DH), F32), jax.ShapeDtypeStruct((1, 128), F32),
                   jax.ShapeDtypeStruct((HQ, BLK, 2 * BLK), F32)],
        compiler_params=_cp(("arbitrary",)),
    )(proj, proj, proj, proj, proj, proj, ao, ao, dao, dao, lse, lse, biasm, q_norm_w, k_norm_w, sinks_p)


CONV_TM = 256


def conv_fwd(proj, conv_w, conv_b):
    t = proj.shape[0]
    tm = min(t, CONV_TM)
    cblk = C_XBC // XBC

    def body(x_ref, xp_ref, w_ref, b_ref, o_ref, xe):
        i = pl.program_id(0)
        xe[0:8, :] = jnp.where(i == 0, 0.0, xp_ref[...])
        xe[8:8 + tm, :] = x_ref[...]
        acc = jnp.broadcast_to(b_ref[...], (tm, XBC))
        for j in range(CONV_K):
            acc = acc + w_ref[j:j + 1, :] * xe[5 + j:5 + j + tm, :]
        o_ref[...] = acc * _sig(acc)

    return pl.pallas_call(
        body, name="conv_fwd", grid=(t // tm,),
        in_specs=[pl.BlockSpec((tm, XBC), lambda i: (i, cblk)),
                  pl.BlockSpec((8, XBC), lambda i: (jnp.maximum(i * (tm // 8) - 1, 0), cblk)),
                  _full((CONV_K, XBC)), _full((1, XBC))],
        out_specs=pl.BlockSpec((tm, XBC), lambda i: (i, 0)),
        out_shape=jax.ShapeDtypeStruct((t, XBC), F32),
        scratch_shapes=[pltpu.VMEM((tm + 8, XBC), F32)],
        compiler_params=_cp(("parallel",)),
    )(proj, proj, conv_w, conv_b)


def conv_bwd(proj, dact, conv_w, conv_b):
    t = proj.shape[0]
    tm = min(t, CONV_TM)
    nt = t // tm
    cblk = C_XBC // XBC

    def body(x_ref, xp_ref, xn_ref, d_ref, dn_ref, w_ref, b_ref, dx_ref, dw_ref, db_ref, xe, de):
        i = pl.program_id(0)

        @pl.when(i == 0)
        def _():
            dw_ref[...] = jnp.zeros_like(dw_ref)
            db_ref[...] = jnp.zeros_like(db_ref)

        xe[0:8, :] = jnp.where(i == 0, 0.0, xp_ref[...])
        xe[8:8 + tm, :] = x_ref[...]
        xe[8 + tm:16 + tm, :] = xn_ref[...]
        pre = jnp.broadcast_to(b_ref[...], (tm + 8, XBC))
        for j in range(CONV_K):
            pre = pre + w_ref[j:j + 1, :] * xe[5 + j:5 + j + tm + 8, :]
        sg = _sig(pre)
        dsilu = sg * (1.0 + pre * (1.0 - sg))
        dpre_c = d_ref[...] * dsilu[0:tm]
        dpre_n = jnp.where(i == nt - 1, 0.0, dn_ref[...] * dsilu[tm:tm + 8])
        de[0:tm, :] = dpre_c
        de[tm:tm + 8, :] = dpre_n
        dx = jnp.zeros((tm, XBC), F32)
        for j in range(CONV_K):
            dx = dx + w_ref[j:j + 1, :] * de[3 - j:3 - j + tm, :]
            dw_ref[j:j + 1, :] += _csum(dpre_c * xe[5 + j:5 + j + tm, :])
        dx_ref[...] = dx.astype(BF16)
        db_ref[...] += _csum(dpre_c)

    r8 = tm // 8
    return pl.pallas_call(
        body, name="conv_bwd", grid=(nt,),
        in_specs=[pl.BlockSpec((tm, XBC), lambda i: (i, cblk)),
                  pl.BlockSpec((8, XBC), lambda i: (jnp.maximum(i * r8 - 1, 0), cblk)),
                  pl.BlockSpec((8, XBC), lambda i: (jnp.minimum((i + 1) * r8, nt * r8 - 1), cblk)),
                  pl.BlockSpec((tm, XBC), lambda i: (i, 0)),
                  pl.BlockSpec((8, XBC), lambda i: (jnp.minimum((i + 1) * r8, nt * r8 - 1), 0)),
                  _full((CONV_K, XBC)), _full((1, XBC))],
        out_specs=[pl.BlockSpec((tm, XBC), lambda i: (i, 0)), _full((CONV_K, XBC)), _full((1, XBC))],
        out_shape=[jax.ShapeDtypeStruct((t, XBC), BF16), jax.ShapeDtypeStruct((CONV_K, XBC), F32),
                   jax.ShapeDtypeStruct((1, XBC), F32)],
        scratch_shapes=[pltpu.VMEM((tm + 16, XBC), F32), pltpu.VMEM((tm + 8, XBC), F32)],
        compiler_params=_cp(("arbitrary",)),
    )(proj, proj, proj, dact, dact, conv_w, conv_b)


def _split3(x):
    h = x.astype(BF16)
    r = x - h.astype(F32)
    m = r.astype(BF16)
    lo = (r - m.astype(F32)).astype(BF16)
    return h, m, lo


def _tri_mm(tri, x):
    h, m, lo = _split3(x)
    return _dot(tri, h) + _dot(tri, m) + _dot(tri, lo)


def _softplus(x):
    return jnp.maximum(x, 0.0) + jnp.log1p(jnp.exp(-jnp.abs(x)))


def _chunk_decays(dt_raw, dtb, alog):
    dtv = _softplus(dt_raw + dtb)
    a = -jnp.exp(alog)
    ri = lax.broadcasted_iota(jnp.int32, (BLK, BLK), 0)
    ci = lax.broadcasted_iota(jnp.int32, (BLK, BLK), 1)
    causal = ri >= ci
    acum = _tri_mm(causal.astype(BF16), dtv * a)
    return dtv, a, causal, acum, acum.T


def ssd_fwd(act, proj, dtb_p, alog_p, dsk_p):
    t = act.shape[0]
    nc = t // BLK

    def body(xs_ref, b_ref, c_ref, dt_ref, dtb_ref, al_ref, dk_ref, y_ref, sp_ref, st):
        c = pl.program_id(0)

        @pl.when(c == 0)
        def _():
            st[...] = jnp.zeros_like(st)

        sp_ref[0] = st[...]
        dtv, a, causal, acum, acum_t = _chunk_decays(dt_ref[...], dtb_ref[...], al_ref[...])
        exp_a = jnp.exp(acum)
        alast = acum[BLK - 1:BLK, :]
        dte = jnp.exp(alast - acum)
        cd = jnp.exp(alast)
        for g in range(SG):
            bb = b_ref[:, SN * g:SN * (g + 1)].astype(BF16)
            cb = c_ref[:, SN * g:SN * (g + 1)].astype(BF16)
            gm = _dot_nt(cb, bb)
            for r in range(SR):
                hd = SR * g + r
                hs = slice(SP * hd, SP * (hd + 1))
                lam = jnp.exp(jnp.where(causal, acum[:, hd:hd + 1] - acum_t[hd:hd + 1, :], NEG))
                xh = xs_ref[:, hs]
                xdt = xh * dtv[:, hd:hd + 1]
                yd = _dot((gm * lam).astype(BF16), xdt.astype(BF16))
                sh = st[hs, :]
                yo = _dot_nt(cb, sh.astype(BF16)) * exp_a[:, hd:hd + 1]
                y_ref[:, hs] = yd + yo + dk_ref[:, hd:hd + 1] * xh
                xe = (xdt * dte[:, hd:hd + 1]).astype(BF16)
                st[hs, :] = cd[:, hd:hd + 1] * sh + _dot_tn(xe, bb)

    vec = _full((1, 128))
    return pl.pallas_call(
        body, name="ssd_fwd", grid=(nc,),
        in_specs=[pl.BlockSpec((BLK, SSM_W), lambda c: (c, 0)),
                  pl.BlockSpec((BLK, SG * SN), lambda c: (c, SSM_W // (SG * SN))),
                  pl.BlockSpec((BLK, SG * SN), lambda c: (c, SSM_W // (SG * SN) + 1)),
                  pl.BlockSpec((BLK, 128), lambda c: (c, C_DT // 128)), vec, vec, vec],
        out_specs=[pl.BlockSpec((BLK, SSM_W), lambda c: (c, 0)), pl.BlockSpec((1, SSM_W, SN), lambda c: (c, 0, 0))],
        out_shape=[jax.ShapeDtypeStruct((t, SSM_W), F32), jax.ShapeDtypeStruct((nc, SSM_W, SN), F32)],
        scratch_shapes=[pltpu.VMEM((SSM_W, SN), F32)],
        compiler_params=_cp(("arbitrary",)),
    )(act, act, act, proj, dtb_p, alog_p, dsk_p)


def ssd_bwd(act, proj, dy, sprev, dtb_p, alog_p, dsk_p):
    t = act.shape[0]
    nc = t // BLK

    def body(xs_ref, b_ref, c_ref, dt_ref, dy_ref, sp_ref, dtb_ref, al_ref, dk_ref,
             da_ref, ddt_ref, ddtb_ref, dal_ref, ddk_ref, dst):
        i = pl.program_id(0)

        @pl.when(i == 0)
        def _():
            dst[...] = jnp.zeros_like(dst)
            ddtb_ref[...] = jnp.zeros_like(ddtb_ref)
            dal_ref[...] = jnp.zeros_like(dal_ref)
            ddk_ref[...] = jnp.zeros_like(ddk_ref)

        dt_raw = dt_ref[...]
        dtb = dtb_ref[...]
        dtv, a, causal, acum, acum_t = _chunk_decays(dt_raw, dtb, al_ref[...])
        exp_a = jnp.exp(acum)
        alast = acum[BLK - 1:BLK, :]
        dte = jnp.exp(alast - acum)
        cd = jnp.exp(alast)
        lane_c = lax.broadcasted_iota(jnp.int32, (BLK, 128), 1)
        sub_r = lax.broadcasted_iota(jnp.int32, (128, BLK), 0)
        lane_1 = lax.broadcasted_iota(jnp.int32, (1, 128), 1)
        da_col = jnp.zeros((BLK, 128), F32)
        da_row = jnp.zeros((128, BLK), F32)
        da_last = jnp.zeros((1, 128), F32)
        ddt = jnp.zeros((BLK, 128), F32)
        ddk = jnp.zeros((1, 128), F32)
        for g in range(SG):
            gs = slice(SN * g, SN * (g + 1))
            bf = b_ref[:, gs]
            cf = c_ref[:, gs]
            bb = bf.astype(BF16)
            cb = cf.astype(BF16)
            gm = _dot_nt(cb, bb)
            dgm = jnp.zeros((BLK, BLK), F32)
            dbg = jnp.zeros((BLK, SN), F32)
            dcg = jnp.zeros((BLK, SN), F32)
            for r in range(SR):
                hd = SR * g + r
                hs = slice(SP * hd, SP * (hd + 1))
                lam = jnp.exp(jnp.where(causal, acum[:, hd:hd + 1] - acum_t[hd:hd + 1, :], NEG))
                xh = xs_ref[:, hs]
                dth = dtv[:, hd:hd + 1]
                xdt = xh * dth
                xdtb = xdt.astype(BF16)
                dyh = dy_ref[:, hs]
                dyb = dyh.astype(BF16)
                eh = dte[:, hd:hd + 1]
                xa = exp_a[:, hd:hd + 1]
                cdh = cd[:, hd:hd + 1]
                sph = sp_ref[0, hs, :]
                dsh = dst[hs, :]
                dshb = dsh.astype(BF16)
                dm = _dot_nt(dyb, xdtb)
                gl = gm * lam
                w = dm * gl
                da_col = da_col + jnp.where(lane_c == hd, _rsum(w), 0.0)
                da_row = da_row + jnp.where(sub_r == hd, _csum(w), 0.0)
                dgm = dgm + dm * lam
                dxdt = _dot_tn(gl.astype(BF16), dyb)
                yoff = _dot_nt(cb, sph.astype(BF16)) * xa
                dxs = _dot_nt(bb, dshb) * eh
                zl = _rsum(xdt * dxs)
                da_col = da_col + jnp.where(lane_c == hd, _rsum(dyh * yoff) - zl, 0.0)
                da_last = da_last + jnp.where(lane_1 == hd, _csum(zl) + cdh * _asum(dsh * sph), 0.0)
                dxdt = dxdt + dxs
                dya = (dyh * xa).astype(BF16)
                dcg = dcg + _dot(dya, sph.astype(BF16))
                dbg = dbg + _dot((xdt * eh).astype(BF16), dshb)
                dst[hs, :] = cdh * dsh + _dot_tn(dya, cb)
                dkh = dk_ref[:, hd:hd + 1]
                da_ref[:, hs] = dxdt * dth + dkh * dyh
                ddt = ddt + jnp.where(lane_c == hd, _rsum(dxdt * xh), 0.0)
                ddk = ddk + jnp.where(lane_1 == hd, _asum(dyh * xh), 0.0)
            dgb = dgm.astype(BF16)
            dcg = dcg + _dot(dgb, bb)
            dbg = dbg + _dot_tn(dgb, cb)
            da_ref[:, SSM_W + SN * g:SSM_W + SN * (g + 1)] = dbg
            da_ref[:, SSM_W + SG * SN + SN * g:SSM_W + SG * SN + SN * (g + 1)] = dcg
        ri = lax.broadcasted_iota(jnp.int32, (BLK, BLK), 0)
        ci = lax.broadcasted_iota(jnp.int32, (BLK, BLK), 1)
        row_i = lax.broadcasted_iota(jnp.int32, (BLK, 128), 0)
        dacum = da_col - da_row.T + jnp.where(row_i == BLK - 1, da_last, 0.0)
        dda = _tri_mm((ri <= ci).astype(BF16), dacum)
        ddt = ddt + dda * a
        dal_ref[...] += _csum(dda * dtv) * a
        ddt_raw = jnp.where(lane_c < SH, ddt * _sig(dt_raw + dtb), 0.0)
        ddt_ref[...] = ddt_raw.astype(BF16)
        ddtb_ref[...] += _csum(ddt_raw)
        ddk_ref[...] += ddk

    rev = lambda i: nc - 1 - i
    vec = _full((1, 128))
    return pl.pallas_call(
        body, name="ssd_bwd", grid=(nc,),
        in_specs=[pl.BlockSpec((BLK, SSM_W), lambda i: (rev(i), 0)),
                  pl.BlockSpec((BLK, SG * SN), lambda i: (rev(i), SSM_W // (SG * SN))),
                  pl.BlockSpec((BLK, SG * SN), lambda i: (rev(i), SSM_W // (SG * SN) + 1)),
                  pl.BlockSpec((BLK, 128), lambda i: (rev(i), C_DT // 128)),
                  pl.BlockSpec((BLK, SSM_W), lambda i: (rev(i), 0)),
                  pl.BlockSpec((1, SSM_W, SN), lambda i: (rev(i), 0, 0)), vec, vec, vec],
        out_specs=[pl.BlockSpec((BLK, XBC), lambda i: (rev(i), 0)), pl.BlockSpec((BLK, 128), lambda i: (rev(i), 0)),
                   vec, vec, vec],
        out_shape=[jax.ShapeDtypeStruct((t, XBC), F32), jax.ShapeDtypeStruct((t, 128), BF16),
                   jax.ShapeDtypeStruct((1, 128), F32), jax.ShapeDtypeStruct((1, 128), F32),
                   jax.ShapeDtypeStruct((1, 128), F32)],
        scratch_shapes=[pltpu.VMEM((SSM_W, SN), F32)],
        compiler_params=_cp(("arbitrary",)),
    )(act, act, act, proj, dy, sprev, dtb_p, alog_p, dsk_p)


TAIL_TM = 128


def _dsilu(z, s):
    return s * (1.0 + z * (1.0 - s))


def tail(proj, ao, yss, x, target, gate, ssm_nw, w_at, w_ss, w_ou):
    t = x.shape[0]
    tm = min(t, TAIL_TM)
    gw = SSM_W // SG

    def body(ao_ref, za_ref, ga_ref, gb_ref, zm_ref, ys_ref, x_ref, tg_ref, gt_ref, nw_ref, wa_ref, ws_ref, wo_ref,
             loss_ref, dy_ref, dao_ref, dza_ref, dga_ref, dgb_ref, dys_ref, dzm_ref,
             ua_ref, yn_ref, mg_ref, dya_ref, dyb_ref, do_ref, dgt_ref, dnw_ref):
        i = pl.program_id(0)

        @pl.when(i == 0)
        def _():
            loss_ref[...] = jnp.zeros_like(loss_ref)
            dgt_ref[...] = jnp.zeros_like(dgt_ref)
            dnw_ref[...] = jnp.zeros_like(dnw_ref)

        ao = ao_ref[...]
        za = za_ref[...]
        sa = _sig(za)
        sila = za * sa
        ua = (ao * sila).astype(BF16)
        ya = _dot(ua, wa_ref[...])
        zm = zm_ref[...]
        sm = _sig(zm)
        silm = zm * sm
        ys = ys_ref[...]
        u = ys * silm
        nw = nw_ref[...]
        rs, uns = [], []
        for g in range(SG):
            ug = u[:, gw * g:gw * (g + 1)]
            r = lax.rsqrt(jnp.mean(ug * ug, axis=-1, keepdims=True) + EPS)
            rs.append(r)
            uns.append(ug * r)
        un = jnp.concatenate(uns, axis=1)
        yn = (un * nw).astype(BF16)
        yb = _dot(yn, ws_ref[...])
        sga = _sig(ga_ref[...])
        sgb = _sig(gb_ref[...])
        mg = (sga * ya + sgb * yb).astype(BF16)
        o = _dot(mg, wo_ref[...])
        gt = gt_ref[...]
        err = (x_ref[...] + gt * o) - tg_ref[...]
        lane = lax.broadcasted_iota(jnp.int32, (1, 128), 1)
        loss_ref[...] += jnp.where(lane == 0, 0.5 * _asum(_rsum(err * err) / D), 0.0)
        dy = err * (1.0 / D)
        dy_ref[...] = dy
        dgt_ref[...] += _csum(dy * o)
        do = (dy * gt).astype(BF16)
        dmg = _dot_nt(do, wo_ref[...])
        dga_ref[...] = (dmg * ya * sga * (1.0 - sga)).astype(BF16)
        dgb_ref[...] = (dmg * yb * sgb * (1.0 - sgb)).astype(BF16)
        dya = (dmg * sga).astype(BF16)
        dyb = (dmg * sgb).astype(BF16)
        dua = _dot_nt(dya, wa_ref[...])
        dao_ref[...] = dua * sila
        dza_ref[...] = (dua * ao * _dsilu(za, sa)).astype(BF16)
        dyn = _dot_nt(dyb, ws_ref[...])
        dnw_ref[...] += _csum(dyn * un)
        dun = dyn * nw
        dus = []
        for g in range(SG):
            gs = slice(gw * g, gw * (g + 1))
            dus.append(rs[g] * (dun[:, gs] - uns[g] * jnp.mean(dun[:, gs] * uns[g], axis=-1, keepdims=True)))
        du = jnp.concatenate(dus, axis=1)
        dys_ref[...] = du * silm
        dzm_ref[...] = (du * ys * _dsilu(zm, sm)).astype(BF16)
        ua_ref[...] = ua
        yn_ref[...] = yn
        mg_ref[...] = mg
        dya_ref[...] = dya
        dyb_ref[...] = dyb
        do_ref[...] = do

    row = lambda w: pl.BlockSpec((tm, w), lambda i: (i, 0))
    pcol = lambda w, c0: pl.BlockSpec((tm, w), lambda i: (i, c0 // w))
    sd = lambda w, dt: jax.ShapeDtypeStruct((t, w), dt)
    return pl.pallas_call(
        body, name="tail", grid=(t // tm,),
        in_specs=[row(D), pcol(D, C_ZA), pcol(D, C_GA), pcol(D, C_GB), pcol(SSM_W, C_ZM), row(SSM_W), row(D), row(D),
                  _full((1, D)), _full((1, SSM_W)), _full((D, D)), _full((SSM_W, D)), _full((D, D))],
        out_specs=[_full((1, 128)), row(D), row(D), row(D), row(D), row(D), row(SSM_W), row(SSM_W),
                   row(D), row(SSM_W), row(D), row(D), row(D), row(D), _full((1, D)), _full((1, SSM_W))],
        out_shape=[jax.ShapeDtypeStruct((1, 128), F32), sd(D, F32), sd(D, F32), sd(D, BF16), sd(D, BF16), sd(D, BF16),
                   sd(SSM_W, F32), sd(SSM_W, BF16), sd(D, BF16), sd(SSM_W, BF16), sd(D, BF16), sd(D, BF16),
                   sd(D, BF16), sd(D, BF16), jax.ShapeDtypeStruct((1, D), F32), jax.ShapeDtypeStruct((1, SSM_W), F32)],
        compiler_params=_cp(("arbitrary",)),
    )(ao, proj, proj, proj, proj, yss, x, target, gate, ssm_nw, w_at, w_ss, w_ou)


def dproj_bwd(dproj, wcat, x, dy, norm_w, scale):
    t = x.shape[0]
    tm = min(t, 512)
    nk = NP // TN
    nt = t // tm

    def body(dp_ref, w_ref, x_ref, dy_ref, nw_ref, sc_ref, gx_ref, dnw_ref, dsc_ref, dsh_ref, acc, dwe_ref):
        i = pl.program_id(0)
        k = pl.program_id(1)

        @pl.when(jnp.logical_and(i == 0, k == 0))
        def _():
            dwe_ref[...] = jnp.zeros_like(dwe_ref)
            dsh_ref[...] = jnp.zeros_like(dsh_ref)
            dnw_ref[...] = jnp.zeros_like(dnw_ref)
            dsc_ref[...] = jnp.zeros_like(dsc_ref)

        part = _dot_nt(dp_ref[...], w_ref[...])

        @pl.when(k == 0)
        def _():
            acc[...] = part

        @pl.when(k > 0)
        def _():
            acc[...] += part

        @pl.when(k == nk - 1)
        def _():
            dh = acc[...]
            xv = x_ref[...]
            r = lax.rsqrt(jnp.mean(xv * xv, axis=-1, keepdims=True) + EPS)
            xn = xv * r
            weff = nw_ref[...] * (1.0 + sc_ref[...])
            dxn = dh * weff
            gx_ref[...] = dy_ref[...] + r * (dxn - xn * jnp.mean(dxn * xn, axis=-1, keepdims=True))
            dwe_ref[...] += _csum(dh * xn)
            dsh_ref[...] += _csum(dh)

        @pl.when(jnp.logical_and(i == nt - 1, k == nk - 1))
        def _():
            dwe = dwe_ref[...]
            dnw_ref[...] = dwe * (1.0 + sc_ref[...])
            dsc_ref[...] = dwe * nw_ref[...]

    vec = pl.BlockSpec((1, D), lambda i, k: (0, 0))
    row = pl.BlockSpec((tm, D), lambda i, k: (i, 0))
    return pl.pallas_call(
        body, name="dproj_bwd", grid=(nt, nk),
        in_specs=[pl.BlockSpec((tm, TN), lambda i, k: (i, k)), pl.BlockSpec((D, TN), lambda i, k: (0, k)),
                  row, row, vec, vec],
        out_specs=[row, vec, vec, vec],
        out_shape=[jax.ShapeDtypeStruct((t, D), F32), jax.ShapeDtypeStruct((1, D), F32),
                   jax.ShapeDtypeStruct((1, D), F32), jax.ShapeDtypeStruct((1, D), F32)],
        scratch_shapes=[pltpu.VMEM((tm, D), F32), pltpu.VMEM((1, D), F32)],
        compiler_params=_cp(("arbitrary", "arbitrary")),
    )(dproj, wcat, x, dy, norm_w, scale)


def xty(a, b, name, bm=512, bn=768):
    t, m = a.shape
    n = b.shape[1]
    tk = min(t, 512)
    bm = min(bm, m)
    bn = min(bn, n)
    nk = t // tk

    def body(a_ref, b_ref, o_ref):
        part = _dot_tn(a_ref[...], b_ref[...])

        @pl.when(pl.program_id(2) == 0)
        def _():
            o_ref[...] = part

        @pl.when(pl.program_id(2) > 0)
        def _():
            o_ref[...] += part

    return pl.pallas_call(
        body, name=name, grid=(m // bm, n // bn, nk),
        in_specs=[pl.BlockSpec((tk, bm), lambda i, j, k: (k, i)), pl.BlockSpec((tk, bn), lambda i, j, k: (k, j))],
        out_specs=pl.BlockSpec((bm, bn), lambda i, j, k: (i, j)),
        out_shape=jax.ShapeDtypeStruct((m, n), F32),
        compiler_params=_cp(("parallel", "parallel", "arbitrary")),
    )(a, b)


def add_n(arrs, name):
    r = arrs[0].shape[0]
    tr = 872 if r % 872 == 0 else r
    k = len(arrs)

    def body(*refs):
        acc = refs[0][...]
        for q in range(1, k):
            acc = acc + refs[q][...]
        refs[k][...] = acc

    spec = pl.BlockSpec((tr, 1024), lambda i: (i, 0))
    return pl.pallas_call(body, name=name, grid=(r // tr,), in_specs=[spec] * k, out_specs=spec,
                          out_shape=jax.ShapeDtypeStruct((r, 1024), F32), compiler_params=_cp(("parallel",)))(*arrs)


def sum_devices(g):
    r = g.shape[1]

    def body(g_ref, o_ref):
        acc = g_ref[0]
        for d in range(1, 8):
            acc = acc + g_ref[d]
        o_ref[...] = acc

    return pl.pallas_call(body, name="sum_devices", out_shape=jax.ShapeDtypeStruct((r, 1024), F32),
                          compiler_params=_cp())(g)


def adamw(w, g, m, v, name):
    r, c = w.shape
    tr = r
    for cand in (256, 128, 64, 32, 16, 8):
        if r % cand == 0 and r > cand:
            tr = cand
            break

    def body(w_ref, g_ref, m_ref, v_ref, d_ref, nm_ref, nv_ref):
        gv = g_ref[...]
        mn = ADAM_B1 * m_ref[...] + (1.0 - ADAM_B1) * gv
        vn = ADAM_B2 * v_ref[...] + (1.0 - ADAM_B2) * (gv * gv)
        m_hat = mn / (1.0 - ADAM_B1 ** ADAM_STEP)
        v_hat = vn / (1.0 - ADAM_B2 ** ADAM_STEP)
        d_ref[...] = -ADAM_LR * (m_hat / (jnp.sqrt(v_hat) + ADAM_EPS) + ADAM_WD * w_ref[...])
        nm_ref[...] = mn
        nv_ref[...] = vn

    spec = pl.BlockSpec((tr, c), lambda i: (i, 0))
    sd = jax.ShapeDtypeStruct((r, c), F32)
    return pl.pallas_call(body, name=name, grid=(r // tr,), in_specs=[spec] * 4, out_specs=[spec] * 3,
                          out_shape=[sd, sd, sd], compiler_params=_cp(("parallel",)))(w, g, m, v)


ANY = pl.BlockSpec(memory_space=pl.ANY)
VM = pl.BlockSpec(memory_space=pltpu.VMEM)
OTHER_CHIPS = ((1, 0), (0, 1), (1, 1))


def _pos():
    return lax.axis_index("x"), lax.axis_index("y"), lax.axis_index("c")


def _flip(v, bit):
    return 1 - v if bit else v


def _rcopy(src, dst, ssem, rsem, peer):
    return pltpu.make_async_remote_copy(src_ref=src, dst_ref=dst, send_sem=ssem, recv_sem=rsem,
                                        device_id=peer, device_id_type=MESH)


def allgather_small(p, name):
    r = p.shape[0]

    def body(in_ref, out_ref, ssem, rsem, lsem):
        x, y, c = _pos()
        me = 4 * x + 2 * y + c
        loc = pltpu.make_async_copy(in_ref, out_ref.at[me], lsem)
        loc.start()
        sends = []
        peers = []
        for k in range(1, 8):
            px, py, pc = _flip(x, (k >> 2) & 1), _flip(y, (k >> 1) & 1), _flip(c, k & 1)
            peers.append((px, py, pc))
            cp = _rcopy(in_ref, out_ref.at[me], ssem.at[k - 1], rsem.at[k - 1], (px, py, pc))
            cp.start()
            sends.append(cp)
        for k in range(1, 8):
            px, py, pc = peers[k - 1]
            _rcopy(in_ref, out_ref.at[4 * px + 2 * py + pc], ssem.at[k - 1], rsem.at[k - 1], (px, py, pc)).wait_recv()
        for cp in sends:
            cp.wait_send()
        loc.wait()

    return pl.pallas_call(
        body, name=name, out_shape=jax.ShapeDtypeStruct((8, r, 1024), F32),
        in_specs=[VM], out_specs=VM,
        scratch_shapes=[pltpu.SemaphoreType.DMA((7,)), pltpu.SemaphoreType.DMA((7,)), pltpu.SemaphoreType.DMA],
    )(p)


def gather_weights(wpack, mod_sh):
    def body(w_ref, m_ref, wg_ref, mo_ref, ssem, rsem, lsem):
        x, y, c = _pos()
        chip = 2 * x + y
        mine = pl.ds(pl.multiple_of(c * HALF, 16), HALF)
        other = pl.ds(pl.multiple_of((1 - c) * HALF, 16), HALF)
        sib = (x, y, 1 - c)
        loc_w = pltpu.make_async_copy(w_ref, wg_ref.at[chip], lsem.at[0])
        loc_m = pltpu.make_async_copy(m_ref, mo_ref.at[chip], lsem.at[1])
        loc_w.start()
        loc_m.start()
        sends = []
        for k, (fx, fy) in enumerate(OTHER_CHIPS):
            peer = (_flip(x, fx), _flip(y, fy), c)
            cw = _rcopy(w_ref.at[mine], wg_ref.at[chip, mine], ssem.at[k], rsem.at[k], peer)
            cm = _rcopy(m_ref, mo_ref.at[chip], ssem.at[6 + k], rsem.at[6 + k], peer)
            cw.start()
            cm.start()
            sends += [cw, cm]
        for k, (fx, fy) in enumerate(OTHER_CHIPS):
            px, py = _flip(x, fx), _flip(y, fy)
            got = wg_ref.at[2 * px + py, mine]
            _rcopy(w_ref.at[mine], got, ssem.at[k], rsem.at[k], (px, py, c)).wait_recv()
            fw = _rcopy(got, got, ssem.at[3 + k], rsem.at[3 + k], sib)
            fw.start()
            sends.append(fw)
        for k, (fx, fy) in enumerate(OTHER_CHIPS):
            px, py = _flip(x, fx), _flip(y, fy)
            land = wg_ref.at[2 * px + py, other]
            _rcopy(land, land, ssem.at[3 + k], rsem.at[3 + k], sib).wait_recv()
            _rcopy(m_ref, mo_ref.at[2 * px + py], ssem.at[6 + k], rsem.at[6 + k], (px, py, c)).wait_recv()
        for cp in sends:
            cp.wait_send()
        loc_w.wait()
        loc_m.wait()

    return pl.pallas_call(
        body, name="gather_weights",
        out_shape=[jax.ShapeDtypeStruct((4, ROWS, 1024), BF16), jax.ShapeDtypeStruct((4, 8, 768), F32)],
        in_specs=[ANY, VM], out_specs=[ANY, VM],
        scratch_shapes=[pltpu.SemaphoreType.DMA((9,)), pltpu.SemaphoreType.DMA((9,)), pltpu.SemaphoreType.DMA((2,))],
    )(wpack, mod_sh)


def pair_exchange(gpack):
    def body(g_ref, r_ref, ssem, rsem):
        x, y, c = _pos()
        other = pl.ds(pl.multiple_of((1 - c) * HALF, 8), HALF)
        cp = _rcopy(g_ref.at[:, other, :], r_ref, ssem, rsem, (x, y, 1 - c))
        cp.start()
        cp.wait()

    return pl.pallas_call(
        body, name="pair_exchange", out_shape=jax.ShapeDtypeStruct((4, HALF, 1024), F32),
        in_specs=[ANY], out_specs=ANY,
        scratch_shapes=[pltpu.SemaphoreType.DMA, pltpu.SemaphoreType.DMA],
    )(gpack)


def chip_exchange(part):
    def body(p_ref, own_ref, r_ref, ssem, rsem, lsem):
        x, y, c = _pos()
        chip = 2 * x + y
        loc = pltpu.make_async_copy(p_ref.at[chip], own_ref, lsem)
        loc.start()
        sends = []
        for k, (fx, fy) in enumerate(OTHER_CHIPS):
            px, py = _flip(x, fx), _flip(y, fy)
            cp = _rcopy(p_ref.at[2 * px + py], r_ref.at[k], ssem.at[k], rsem.at[k], (px, py, c))
            cp.start()
            sends.append(cp)
        for k, (fx, fy) in enumerate(OTHER_CHIPS):
            px, py = _flip(x, fx), _flip(y, fy)
            _rcopy(p_ref.at[chip], r_ref.at[k], ssem.at[k], rsem.at[k], (px, py, c)).wait_recv()
        for cp in sends:
            cp.wait_send()
        loc.wait()

    return pl.pallas_call(
        body, name="chip_exchange",
        out_shape=[jax.ShapeDtypeStruct((HALF, 1024), F32), jax.ShapeDtypeStruct((3, HALF, 1024), F32)],
        in_specs=[ANY], out_specs=[ANY, ANY],
        scratch_shapes=[pltpu.SemaphoreType.DMA((3,)), pltpu.SemaphoreType.DMA((3,)), pltpu.SemaphoreType.DMA],
    )(part)


def pair_allgather(red):
    def body(r_ref, o_ref, ssem, rsem, lsem):
        x, y, c = _pos()
        mine = pl.ds(pl.multiple_of(c * HALF, 8), HALF)
        other = pl.ds(pl.multiple_of((1 - c) * HALF, 8), HALF)
        loc = pltpu.make_async_copy(r_ref, o_ref.at[mine], lsem)
        loc.start()
        cp = _rcopy(r_ref, o_ref.at[mine], ssem, rsem, (x, y, 1 - c))
        cp.start()
        _rcopy(r_ref, o_ref.at[other], ssem, rsem, (x, y, 1 - c)).wait_recv()
        cp.wait_send()
        loc.wait()

    return pl.pallas_call(
        body, name="pair_allgather", out_shape=jax.ShapeDtypeStruct((ROWS, 1024), F32),
        in_specs=[ANY], out_specs=ANY,
        scratch_shapes=[pltpu.SemaphoreType.DMA, pltpu.SemaphoreType.DMA, pltpu.SemaphoreType.DMA],
    )(red)


def _row(v, width=1024):
    v = v.reshape(-1)
    n = -(-v.shape[0] // width) * width
    return jnp.pad(v, (0, n - v.shape[0])).reshape(-1, width)


def _slots(vs):
    row = [jnp.pad(v.reshape(-1), (0, 128 - v.size)) for v in vs]
    row += [jnp.zeros((128,), F32)] * (8 - len(row))
    return jnp.concatenate(row).reshape(1, 1024)


def _pack_small(b_ada, norm_w, conv_b, ssm_norm_w, q_norm_w, k_norm_w, sinks, dt_bias, a_log, d_skip, rel_bias,
                extra=None):
    misc = [q_norm_w, k_norm_w, sinks, dt_bias, a_log, d_skip] + ([] if extra is None else [extra])
    rows = [_row(b_ada), _row(norm_w), _row(conv_b), _row(ssm_norm_w), _slots(misc), _row(rel_bias)]
    rows.append(jnp.zeros((5, 1024), F32))
    return jnp.concatenate(rows, axis=0)


def _unpack_small(p):
    misc = p[9]
    return dict(b_ada=p[0:3].reshape(1, 3072), norm_w=p[3:4], conv_b=p[4:7].reshape(1, 3072),
                ssm_norm_w=p[7:9].reshape(1, 2048), q_norm_w=misc[None, 0:64], k_norm_w=misc[None, 128:192],
                sinks=misc[None, 256:272], dt_bias=misc[None, 384:416], a_log=misc[None, 512:544],
                d_skip=misc[None, 640:672], rel_bias=p[10, :512].reshape(32, 16), extra=misc[768])


SMALL = ("b_ada", "norm_w", "conv_b", "ssm_norm_w", "q_norm_w", "k_norm_w", "sinks", "dt_bias", "a_log", "d_skip",
         "rel_bias")
WEIGHTS = ("w_ada", "b_ada", "norm_w", "w_in", "q_norm_w", "k_norm_w", "rel_bias", "sinks", "conv_w", "conv_b",
           "dt_bias", "a_log", "d_skip", "ssm_norm_w", "w_attn_proj", "w_ssm_proj", "w_out")
IN_COLS = ((0, 1024, C_Q), (1024, 256, C_K), (1280, 256, C_V), (1536, 1024, C_ZA), (2560, 2048, C_ZM),
           (4608, 3072, C_XBC), (7680, 32, C_DT), (7712, 1024, C_GA), (8736, 1024, C_GB))


def _to_cat(w_full):
    by_new = sorted(IN_COLS, key=lambda e: e[2])
    parts, pos = [], 0
    for o, n, cnew in by_new:
        assert cnew == pos
        parts.append(w_full[:, o:o + n])
        pos += n
    parts.append(jnp.zeros((w_full.shape[0], NP - pos), w_full.dtype))
    return jnp.concatenate(parts, axis=1)


def _from_cat(w_cat):
    return jnp.concatenate([w_cat[:, cnew:cnew + n] for o, n, cnew in IN_COLS], axis=1)


def kernel(x, c, w_ada, b_ada, norm_w, w_in, q_norm_w, k_norm_w, rel_bias, sinks, conv_w, conv_b, dt_bias, a_log, d_skip, ssm_norm_w, w_attn_proj, w_ssm_proj, w_out, loss_target, m_w_ada, m_b_ada, m_norm_w, m_w_in, m_q_norm_w, m_k_norm_w, m_rel_bias, m_sinks, m_conv_w, m_conv_b, m_dt_bias, m_a_log, m_d_skip, m_ssm_norm_w, m_w_attn_proj, m_w_ssm_proj, m_w_out, v_w_ada, v_b_ada, v_norm_w, v_w_in, v_q_norm_w, v_k_norm_w, v_rel_bias, v_sinks, v_conv_w, v_conv_b, v_dt_bias, v_a_log, v_d_skip, v_ssm_norm_w, v_w_attn_proj, v_w_ssm_proj, v_w_out):
    args = dict(locals())
    xi, yi, ci = lax.axis_index("x"), lax.axis_index("y"), lax.axis_index("c")
    chip = 2 * xi + yi
    me = 4 * xi + 2 * yi + ci
    x2 = x[0]
    tgt = loss_target[0]

    pay = jnp.concatenate([c, conv_w[0].reshape(3, 1024), jnp.zeros((4, 1024), F32)], axis=0)
    g0 = allgather_small(pay, "gather_cond")
    c_all = g0[:, 0, :]
    conv_w_full = g0[0::2, 1:4, :].reshape(4, CONV_K, 768).transpose(1, 0, 2).reshape(CONV_K, XBC)

    b_ada_sh = lax.dynamic_slice(b_ada, (0, chip * 768), (1, 768))
    mod_sh = ada_mod(c_all, w_ada[0], b_ada_sh)

    wpack = jnp.concatenate([w_in[0].astype(BF16).reshape(R_IN, 1024), w_attn_proj[0].astype(BF16),
                             w_ssm_proj[0].astype(BF16), w_out[0].astype(BF16),
                             jnp.zeros((ROWS - R_IN - R_AT - R_SS - R_OU, 1024), BF16)], axis=0)
    wg, modg = gather_weights(wpack, mod_sh)
    mod = lax.dynamic_slice(modg, (0, me, 0), (4, 1, 768)).reshape(1, 3 * D)
    shift, scale, gate = mod[:, :D], mod[:, D:2 * D], mod[:, 2 * D:]
    w_in_full = wg[:, :R_IN].reshape(4, D, IN_W // 4).transpose(1, 0, 2).reshape(D, IN_W)
    wcat = _to_cat(w_in_full)
    o1 = R_IN
    w_at = wg[:, o1:o1 + R_AT].reshape(D, D)
    w_ss = wg[:, o1 + R_AT:o1 + R_AT + R_SS].reshape(SSM_W, D)
    w_ou = wg[:, o1 + R_AT + R_SS:o1 + R_AT + R_SS + R_OU].reshape(D, D)

    pad128 = lambda v: jnp.pad(v, ((0, 0), (0, 128 - v.shape[1])))
    sinks_p, dtb_p, alog_p, dsk_p = pad128(sinks), pad128(dt_bias), pad128(a_log), pad128(d_skip)
    bucket, valid = _bucket_table()

    proj, h = norm_proj(x2, norm_w, scale, shift, wcat)
    biasm = bias_expand(rel_bias, bucket, valid)
    ao, lse = attn_fwd(proj, biasm, q_norm_w, k_norm_w, sinks_p)
    act = conv_fwd(proj, conv_w_full, conv_b)
    yss, sprev = ssd_fwd(act, proj, dtb_p, alog_p, dsk_p)

    (loss_p, dy, dao, dza, dga, dgb, dyss, dzm, ua, yn, mg, dya, dyb, dout, dgate, dssm_nw) = tail(
        proj, ao, yss, x2, tgt, gate, ssm_norm_w, w_at, w_ss, w_ou)

    dq, dk, dv, dqw, dkw, dsk, dacc = attn_bwd(proj, ao, dao, lse, biasm, q_norm_w, k_norm_w, sinks_p)
    drb = bias_reduce(dacc, bucket)[:, :NBUCKET].T
    dact, ddt, ddtb, dalog, ddskip = ssd_bwd(act, proj, dyss, sprev, dtb_p, alog_p, dsk_p)
    dxbc, dconv_w, dconv_b = conv_bwd(proj, dact, conv_w_full, conv_b)

    t = x2.shape[0]
    dproj = jnp.concatenate([dq, dza, dga, dgb, dzm, dxbc, dk, dv, ddt, jnp.zeros((t, NP - C_DT - 128), BF16)], axis=1)
    grad_x, dnorm_w, dscale, dshift = dproj_bwd(dproj, wcat, x2, dy, norm_w, scale)
    dwcat = xty(h, dproj, "dw_in", bm=512, bn=TN)
    dw_at = xty(ua, dya, "dw_attn", bm=512, bn=512)
    dw_ss = xty(yn, dyb, "dw_ssm", bm=512, bn=512)
    dw_ou = xty(mg, dout, "dw_out", bm=512, bn=512)

    g_in = _from_cat(dwcat).reshape(D, 4, IN_W // 4).transpose(1, 0, 2).reshape(4, R_IN, 1024)
    gpack = jnp.concatenate([g_in, dw_at.reshape(4, R_AT, 1024), dw_ss.reshape(4, R_SS, 1024),
                             dw_ou.reshape(4, R_OU, 1024),
                             jnp.zeros((4, ROWS - R_IN - R_AT - R_SS - R_OU, 1024), F32)], axis=1)
    from_sib = pair_exchange(gpack)
    my_half = lax.dynamic_slice_in_dim(gpack, ci * HALF, HALF, axis=1)
    part = add_n([my_half.reshape(4 * HALF, 1024), from_sib.reshape(4 * HALF, 1024)], "pair_sum")
    own, others = chip_exchange(part.reshape(4, HALF, 1024))
    red = add_n([own, others[0], others[1], others[2]], "chip_sum")
    g_shard = pair_allgather(red)

    dmod = jnp.concatenate([dshift, dscale, dgate], axis=1)
    gsmall = jnp.concatenate([
        _pack_small(dmod, dnorm_w, dconv_b, dssm_nw, dqw, dkw, dsk[:, :HQ], ddtb[:, :SH], dalog[:, :SH],
                    ddskip[:, :SH], drb, extra=loss_p[:, :1]),
        dconv_w.reshape(12, 1024), jnp.zeros((4, 1024), F32)], axis=0)
    gall = allgather_small(gsmall, "gather_small_grads")
    ssum = sum_devices(gall)
    gs = _unpack_small(ssum[:16])
    loss = gs["extra"]
    dconv_w_sh = lax.dynamic_slice(ssum[16:28].reshape(CONV_K, XBC), (0, chip * 768), (CONV_K, 768))
    dmod_all = gall[:, 0:3, :].reshape(8, 3 * D)
    dw_ada = ada_grad(c_all, lax.dynamic_slice(dmod_all, (0, chip * 768), (8, 768)))

    grads = dict(gs)
    grads["w_ada"] = dw_ada
    grads["w_in"] = g_shard[:R_IN].reshape(D, IN_W // 4)
    grads["w_attn_proj"] = g_shard[o1:o1 + R_AT]
    grads["w_ssm_proj"] = g_shard[o1 + R_AT:o1 + R_AT + R_SS]
    grads["w_out"] = g_shard[o1 + R_AT + R_SS:o1 + R_AT + R_SS + R_OU]
    grads["conv_w"] = dconv_w_sh

    delta, new_m, new_v = {}, {}, {}
    for n in ("w_ada", "w_in", "conv_w", "w_attn_proj", "w_ssm_proj", "w_out"):
        delta[n], new_m[n], new_v[n] = adamw(args[n][0], grads[n], args["m_" + n][0], args["v_" + n][0], "adamw_" + n)
    ws = _pack_small(*[args[n] for n in SMALL])
    ms = _pack_small(*[args["m_" + n] for n in SMALL])
    vs = _pack_small(*[args["v_" + n] for n in SMALL])
    d_s, m_s, v_s = adamw(ws, ssum[:16], ms, vs, "adamw_small")
    d_s, m_s, v_s = _unpack_small(d_s), _unpack_small(m_s), _unpack_small(v_s)
    for n in SMALL:
        delta[n], new_m[n], new_v[n] = d_s[n], m_s[n], v_s[n]

    def shaped(n, a):
        return a.reshape(args[n].shape)

    outs = [loss, grad_x[None]]
    for table in (grads, delta, new_m, new_v):
        outs += [shaped(n, table[n]) for n in WEIGHTS]
    return tuple(outs)
```

```python
import functools
import math

import numpy as np
import jax
import jax.numpy as jnp
from jax import lax
from jax.experimental import pallas as pl
from jax.experimental.pallas import tpu as pltpu

F32 = jnp.float32
BF16 = jnp.bfloat16
MESH = pl.DeviceIdType.MESH

D = 1024
HQ, HKV, GRP, DH = 16, 4, 4, 64
BLK = 128
NBUCKET, MAXDIST = 32, 128
SSM_W, SH, SG, SR, SP, SN = 2048, 32, 4, 8, 64, 128
CONV_K = 4
XBC = SSM_W + 2 * SG * SN
IN_W = 9760
EPS = 1e-6
NEG = -1e30
SCALE = DH ** -0.5

C_Q, C_ZA, C_GA, C_GB, C_ZM, C_XBC, C_K, C_V, C_DT = 0, 1024, 2048, 3072, 4096, 6144, 9216, 9472, 9728
NP = 9984
TN = 768

R_IN, R_AT, R_SS, R_OU = 2440, 256, 512, 256
ROWS = 3584
HALF = ROWS // 2

ADAM_LR, ADAM_B1, ADAM_B2, ADAM_EPS, ADAM_WD, ADAM_STEP = 0.001, 0.9, 0.999, 1e-08, 0.01, 10

VMEM_LIMIT = 56 * 1024 * 1024


def _cp(sem=None):
    if sem is None:
        return pltpu.CompilerParams(vmem_limit_bytes=VMEM_LIMIT)
    return pltpu.CompilerParams(dimension_semantics=sem, vmem_limit_bytes=VMEM_LIMIT)


def _sig(x):
    return 1.0 / (1.0 + jnp.exp(-x))


def _dot(a, b):
    return jnp.dot(a, b, preferred_element_type=F32)


def _dot_nt(a, b):
    return lax.dot_general(a, b, (((1,), (1,)), ((), ())), preferred_element_type=F32)


def _dot_tn(a, b):
    return lax.dot_general(a, b, (((0,), (0,)), ((), ())), preferred_element_type=F32)


def _rsum(x):
    return jnp.sum(x, axis=-1, keepdims=True)


def _csum(x):
    return jnp.sum(x, axis=0, keepdims=True)


def _asum(x):
    return _csum(_rsum(x))


def _full(shape):
    nd = len(shape)
    return pl.BlockSpec(shape, lambda *_: (0,) * nd)


def ada_mod(c_all, w_ada_sh, b_ada_sh):
    def body(c_ref, w_ref, b_ref, o_ref):
        cv = c_ref[...]
        s = cv * _sig(cv)
        o_ref[...] = jnp.dot(s, w_ref[...], preferred_element_type=F32,
                             precision=lax.Precision.HIGHEST) + b_ref[...]

    n = w_ada_sh.shape[1]
    return pl.pallas_call(body, name="ada_mod", out_shape=jax.ShapeDtypeStruct((8, n), F32),
                          compiler_params=_cp())(c_all, w_ada_sh, b_ada_sh)


def ada_grad(c_all, dmod_sh):
    def body(c_ref, d_ref, o_ref):
        cv = c_ref[...]
        s = cv * _sig(cv)
        o_ref[...] = lax.dot_general(s, d_ref[...], (((0,), (0,)), ((), ())), preferred_element_type=F32,
                                     precision=lax.Precision.HIGHEST)

    n = dmod_sh.shape[1]
    return pl.pallas_call(body, name="ada_grad", out_shape=jax.ShapeDtypeStruct((D, n), F32),
                          compiler_params=_cp())(c_all, dmod_sh)


def norm_proj(x, norm_w, scale, shift, wcat):
    t = x.shape[0]
    tm = min(t, 1024)

    def body(x_ref, nw_ref, sc_ref, sh_ref, w_ref, p_ref, h_ref, hs):
        @pl.when(pl.program_id(1) == 0)
        def _():
            xv = x_ref[...]
            r = lax.rsqrt(jnp.mean(xv * xv, axis=-1, keepdims=True) + EPS)
            h = (xv * r) * nw_ref[...]
            h = h * (1.0 + sc_ref[...]) + sh_ref[...]
            hb = h.astype(BF16)
            hs[...] = hb
            h_ref[...] = hb

        p_ref[...] = _dot(hs[...], w_ref[...])

    vec = pl.BlockSpec((1, D), lambda i, j: (0, 0))
    return pl.pallas_call(
        body, name="norm_proj", grid=(t // tm, NP // TN),
        in_specs=[pl.BlockSpec((tm, D), lambda i, j: (i, 0)), vec, vec, vec,
                  pl.BlockSpec((D, TN), lambda i, j: (0, j))],
        out_specs=[pl.BlockSpec((tm, TN), lambda i, j: (i, j)), pl.BlockSpec((tm, D), lambda i, j: (i, 0))],
        out_shape=[jax.ShapeDtypeStruct((t, NP), F32), jax.ShapeDtypeStruct((t, D), BF16)],
        scratch_shapes=[pltpu.VMEM((tm, D), BF16)],
        compiler_params=_cp(("parallel", "arbitrary")),
    )(x, norm_w, scale, shift, wcat)


def _bucket_table():
    qi = jnp.arange(BLK)[:, None]
    kj = jnp.arange(2 * BLK)[None, :]
    dist = qi + BLK - kj
    n = jnp.maximum(dist, 0)
    max_exact = NBUCKET // 2
    nf = jnp.maximum(n, 1).astype(F32)
    large = max_exact + (jnp.log(nf / max_exact) / math.log(MAXDIST / max_exact)
                         * (NBUCKET - max_exact)).astype(jnp.int32)
    large = jnp.minimum(large, NBUCKET - 1)
    bucket = jnp.where(n < max_exact, n, large).astype(jnp.int32)
    valid = (dist >= 0) & (dist < BLK)
    return jnp.where(valid, bucket, -1)


def bias_expand(rel_bias, sinks, bucket):
    def body(rb_ref, sk_ref, bk_ref, o_ref):
        hd = pl.program_id(0)
        bk = bk_ref[...]
        col = lax.broadcasted_iota(jnp.int32, (BLK, 2 * BLK), 1)

        def step(b, acc):
            return jnp.where(bk == b, rb_ref[b, hd], acc)

        acc = lax.fori_loop(0, NBUCKET, step, jnp.full((BLK, 2 * BLK), NEG, F32))
        acc = jnp.where(col == 0, sk_ref[0, hd], acc)
        o_ref[1, 0] = acc
        o_ref[0, 0] = jnp.where(jnp.logical_and(col > 0, col < BLK), NEG, acc)

    smem = pl.BlockSpec(memory_space=pltpu.SMEM)
    return pl.pallas_call(
        body, name="bias_expand", grid=(HQ,),
        in_specs=[smem, smem, _full((BLK, 2 * BLK))],
        out_specs=pl.BlockSpec((2, 1, BLK, 2 * BLK), lambda h: (0, h, 0, 0)),
        out_shape=jax.ShapeDtypeStruct((2, HQ, BLK, 2 * BLK), F32),
        compiler_params=_cp(("arbitrary",)),
    )(rel_bias, sinks, bucket)


def bias_reduce(dacc, bucket):
    def body(d_ref, bk_ref, o_ref):
        bk = bk_ref[...]
        lane = lax.broadcasted_iota(jnp.int32, (1, 128), 1)
        col = lax.broadcasted_iota(jnp.int32, (BLK, 2 * BLK), 1)
        for hd in range(HQ):
            dv = d_ref[hd]

            def step(b, row):
                s = _asum(jnp.where(bk == b, dv, 0.0))
                return jnp.where(lane == b, s, row)

            row = lax.fori_loop(0, NBUCKET, step, jnp.zeros((1, 128), F32))
            o_ref[hd:hd + 1, :] = jnp.where(lane == NBUCKET, _asum(jnp.where(col == 0, dv, 0.0)), row)

    return pl.pallas_call(body, name="bias_reduce", out_shape=jax.ShapeDtypeStruct((HQ, 128), F32),
                          compiler_params=_cp())(dacc, bucket)


GQ = GRP * BLK


def _stack_heads(x, nh):
    return jnp.concatenate([x[:, DH * h:DH * (h + 1)] for h in range(nh)], axis=0)


def _unstack(xs, nh):
    rows = xs.shape[0] // nh
    return jnp.concatenate([xs[rows * h:rows * (h + 1)] for h in range(nh)], axis=1)


def _rms(x):
    return lax.rsqrt(jnp.mean(x * x, axis=-1, keepdims=True) + EPS)


def _stack_q(q, qw):
    qs = _stack_heads(q, HQ)
    r = _rms(qs)
    qhat = qs * r
    return qhat * qw, qhat, r


def _band_first(shape):
    return (lax.broadcasted_iota(jnp.int32, shape, 0) & (2 * BLK - 1)) == 0


def _stack_kv(kp, kc, vp, vc, kw):
    ks = _stack_heads(jnp.concatenate([kp, kc], axis=0), HKV)
    r = _rms(ks)
    khat = ks * r
    first = _band_first(ks.shape)
    kn = jnp.where(first, 0.0, khat * kw)
    v2 = jnp.where(first, 0.0, _stack_heads(jnp.concatenate([vp, vc], axis=0), HKV)).astype(BF16)
    return kn, khat, r, v2


def _softmax_rows(s):
    p = jnp.exp(s - jnp.max(s, axis=-1, keepdims=True))
    return p * (1.0 / _rsum(p))


def attn_fwd(proj, biasm, q_norm_w, k_norm_w):
    t = proj.shape[0]
    nb = t // BLK

    def body(q_ref, kc_ref, kp_ref, vc_ref, vp_ref, bm_ref, qw_ref, kw_ref, o_ref):
        qn = _stack_q(q_ref[...], qw_ref[...])[0].astype(BF16)
        kn, _, _, v2 = _stack_kv(kp_ref[...], kc_ref[...], vp_ref[...], vc_ref[...], kw_ref[...])
        knb = kn.astype(BF16)
        s = jnp.concatenate([_dot_nt(qn[GQ * j:GQ * (j + 1)], knb[2 * BLK * j:2 * BLK * (j + 1)])
                             for j in range(HKV)], axis=0)
        pr = _softmax_rows(s * SCALE + bm_ref[0].reshape(HQ * BLK, 2 * BLK)).astype(BF16)
        o = jnp.concatenate([_dot(pr[GQ * j:GQ * (j + 1)], v2[2 * BLK * j:2 * BLK * (j + 1)])
                             for j in range(HKV)], axis=0)
        o_ref[...] = _unstack(o, HQ)

    kblk, vblk = C_K // 256, C_V // 256
    prev = lambda n: jnp.maximum(n - 1, 0)
    return pl.pallas_call(
        body, name="attn_fwd", grid=(nb,),
        in_specs=[pl.BlockSpec((BLK, D), lambda n: (n, 0)),
                  pl.BlockSpec((BLK, 256), lambda n: (n, kblk)),
                  pl.BlockSpec((BLK, 256), lambda n: (prev(n), kblk)),
                  pl.BlockSpec((BLK, 256), lambda n: (n, vblk)),
                  pl.BlockSpec((BLK, 256), lambda n: (prev(n), vblk)),
                  pl.BlockSpec((1, HQ, BLK, 2 * BLK), lambda n: (jnp.minimum(n, 1), 0, 0, 0)),
                  _full((1, DH)), _full((1, DH))],
        out_specs=pl.BlockSpec((BLK, D), lambda n: (n, 0)),
        out_shape=jax.ShapeDtypeStruct((t, D), F32),
        compiler_params=_cp(("parallel",)),
    )(proj, proj, proj, proj, proj, biasm, q_norm_w, k_norm_w)


def attn_bwd(proj, dao, biasm, q_norm_w, k_norm_w):
    t = proj.shape[0]
    nb = t // BLK
    kb = 2 * BLK

    def body(q_ref, kc_ref, kp_ref, vc_ref, vp_ref, do_ref, bm_ref, qw_ref, kw_ref,
             dq_ref, dk_ref, dv_ref, dqw_ref, dkw_ref, dacc_ref, ck, cv, pk, pv, nk, nv):
        n = pl.program_id(0)

        @pl.when(n == 0)
        def _():
            for ref in (dqw_ref, dkw_ref, dacc_ref, ck, cv):
                ref[...] = jnp.zeros_like(ref)

        qw = qw_ref[...]
        kw = kw_ref[...]
        kn, khat, rk, v2 = _stack_kv(kp_ref[...], kc_ref[...], vp_ref[...], vc_ref[...], kw)
        grp = lambda a, j: a[GQ * j:GQ * (j + 1)]
        band = lambda a, j: a[kb * j:kb * (j + 1)]

        @pl.when(n < nb)
        def _():
            qn, qhat, rq = _stack_q(q_ref[...], qw)
            qnb = qn.astype(BF16)
            knb = kn.astype(BF16)
            dos = _stack_heads(do_ref[...], HQ).astype(BF16)
            s = jnp.concatenate([_dot_nt(grp(qnb, j), band(knb, j)) for j in range(HKV)], axis=0)
            pr = _softmax_rows(s * SCALE + bm_ref[0].reshape(HQ * BLK, kb))
            dp = jnp.concatenate([_dot_nt(grp(dos, j), band(v2, j)) for j in range(HKV)], axis=0)
            ds = pr * (dp - _rsum(pr * dp))
            dacc_ref[...] += ds.reshape(HQ, BLK, kb)
            dsb = ds.astype(BF16)
            prb = pr.astype(BF16)
            dqn = jnp.concatenate([_dot(grp(dsb, j), band(knb, j)) for j in range(HKV)], axis=0) * SCALE
            dqhat = dqn * qw
            dq = rq * (dqhat - qhat * jnp.mean(dqhat * qhat, axis=-1, keepdims=True))
            dq_ref[...] = _unstack(dq, HQ).astype(BF16)
            dqw_ref[...] += _csum(dqn * qhat)
            first = _band_first((kb, DH))
            for j in range(HKV):
                rows = slice(BLK * j, BLK * (j + 1))
                dkn = jnp.where(first, 0.0, _dot_tn(grp(dsb, j), grp(qnb, j)) * SCALE)
                dvj = jnp.where(first, 0.0, _dot_tn(grp(prb, j), grp(dos, j)))
                pk[rows, :] = dkn[:BLK]
                nk[rows, :] = dkn[BLK:]
                pv[rows, :] = dvj[:BLK]
                nv[rows, :] = dvj[BLK:]

        @pl.when(n == nb)
        def _():
            for ref in (pk, pv, nk, nv):
                ref[...] = jnp.zeros_like(ref)

        khp = jnp.concatenate([khat[kb * j:kb * j + BLK] for j in range(HKV)], axis=0)
        rkp = jnp.concatenate([rk[kb * j:kb * j + BLK] for j in range(HKV)], axis=0)
        dkn = ck[...] + pk[...]
        dkhat = dkn * kw
        dk = rkp * (dkhat - khp * jnp.mean(dkhat * khp, axis=-1, keepdims=True))
        dk_ref[...] = _unstack(dk, HKV).astype(BF16)
        dkw_ref[...] += _csum(dkn * khp)
        dv_ref[...] = _unstack(cv[...] + pv[...], HKV).astype(BF16)
        ck[...] = nk[...]
        cv[...] = nv[...]

    kblk, vblk = C_K // 256, C_V // 256
    cur = lambda n: jnp.minimum(n, nb - 1)
    prev = lambda n: jnp.maximum(n - 1, 0)
    carry = pltpu.VMEM((HKV * BLK, DH), F32)
    return pl.pallas_call(
        body, name="attn_bwd", grid=(nb + 1,),
        in_specs=[pl.BlockSpec((BLK, D), lambda n: (cur(n), 0)),
                  pl.BlockSpec((BLK, 256), lambda n: (cur(n), kblk)), pl.BlockSpec((BLK, 256), lambda n: (prev(n), kblk)),
                  pl.BlockSpec((BLK, 256), lambda n: (cur(n), vblk)), pl.BlockSpec((BLK, 256), lambda n: (prev(n), vblk)),
                  pl.BlockSpec((BLK, D), lambda n: (cur(n), 0)),
                  pl.BlockSpec((1, HQ, BLK, kb), lambda n: (jnp.minimum(n, 1), 0, 0, 0)),
                  _full((1, DH)), _full((1, DH))],
        out_specs=[pl.BlockSpec((BLK, D), lambda n: (cur(n), 0)),
                   pl.BlockSpec((BLK, 256), lambda n: (prev(n), 0)), pl.BlockSpec((BLK, 256), lambda n: (prev(n), 0)),
                   _full((1, DH)), _full((1, DH)), _full((HQ, BLK, kb))],
        out_shape=[jax.ShapeDtypeStruct((t, D), BF16), jax.ShapeDtypeStruct((t, 256), BF16),
                   jax.ShapeDtypeStruct((t, 256), BF16), jax.ShapeDtypeStruct((1, DH), F32),
                   jax.ShapeDtypeStruct((1, DH), F32), jax.ShapeDtypeStruct((HQ, BLK, kb), F32)],
        scratch_shapes=[carry] * 6,
        compiler_params=_cp(("arbitrary",)),
    )(proj, proj, proj, proj, proj, dao, biasm, q_norm_w, k_norm_w)


CONV_TM = 256


def conv_fwd(proj, conv_w, conv_b):
    t = proj.shape[0]
    tm = min(t, CONV_TM)
    cblk = C_XBC // XBC

    def body(x_ref, xp_ref, w_ref, b_ref, o_ref, xe):
        i = pl.program_id(0)
        xe[0:8, :] = jnp.where(i == 0, 0.0, xp_ref[...])
        xe[8:8 + tm, :] = x_ref[...]
        acc = jnp.broadcast_to(b_ref[...], (tm, XBC))
        for j in range(CONV_K):
            acc = acc + w_ref[j:j + 1, :] * xe[5 + j:5 + j + tm, :]
        o_ref[...] = acc * _sig(acc)

    return pl.pallas_call(
        body, name="conv_fwd", grid=(t // tm,),
        in_specs=[pl.BlockSpec((tm, XBC), lambda i: (i, cblk)),
                  pl.BlockSpec((8, XBC), lambda i: (jnp.maximum(i * (tm // 8) - 1, 0), cblk)),
                  _full((CONV_K, XBC)), _full((1, XBC))],
        out_specs=pl.BlockSpec((tm, XBC), lambda i: (i, 0)),
        out_shape=jax.ShapeDtypeStruct((t, XBC), F32),
        scratch_shapes=[pltpu.VMEM((tm + 8, XBC), F32)],
        compiler_params=_cp(("parallel",)),
    )(proj, proj, conv_w, conv_b)


def conv_bwd(proj, dact, conv_w, conv_b):
    t = proj.shape[0]
    tm = min(t, CONV_TM)
    nt = t // tm
    cblk = C_XBC // XBC

    def body(x_ref, xp_ref, xn_ref, d_ref, dn_ref, w_ref, b_ref, dx_ref, dw_ref, db_ref, xe, de):
        i = pl.program_id(0)

        @pl.when(i == 0)
        def _():
            dw_ref[...] = jnp.zeros_like(dw_ref)
            db_ref[...] = jnp.zeros_like(db_ref)

        xe[0:8, :] = jnp.where(i == 0, 0.0, xp_ref[...])
        xe[8:8 + tm, :] = x_ref[...]
        xe[8 + tm:16 + tm, :] = xn_ref[...]
        pre = jnp.broadcast_to(b_ref[...], (tm + 8, XBC))
        for j in range(CONV_K):
            pre = pre + w_ref[j:j + 1, :] * xe[5 + j:5 + j + tm + 8, :]
        sg = _sig(pre)
        dsilu = sg * (1.0 + pre * (1.0 - sg))
        dpre_c = d_ref[...] * dsilu[0:tm]
        dpre_n = jnp.where(i == nt - 1, 0.0, dn_ref[...] * dsilu[tm:tm + 8])
        de[0:tm, :] = dpre_c
        de[tm:tm + 8, :] = dpre_n
        dx = jnp.zeros((tm, XBC), F32)
        for j in range(CONV_K):
            dx = dx + w_ref[j:j + 1, :] * de[3 - j:3 - j + tm, :]
            dw_ref[j:j + 1, :] += _csum(dpre_c * xe[5 + j:5 + j + tm, :])
        dx_ref[...] = dx.astype(BF16)
        db_ref[...] += _csum(dpre_c)

    r8 = tm // 8
    return pl.pallas_call(
        body, name="conv_bwd", grid=(nt,),
        in_specs=[pl.BlockSpec((tm, XBC), lambda i: (i, cblk)),
                  pl.BlockSpec((8, XBC), lambda i: (jnp.maximum(i * r8 - 1, 0), cblk)),
                  pl.BlockSpec((8, XBC), lambda i: (jnp.minimum((i + 1) * r8, nt * r8 - 1), cblk)),
                  pl.BlockSpec((tm, XBC), lambda i: (i, 0)),
                  pl.BlockSpec((8, XBC), lambda i: (jnp.minimum((i + 1) * r8, nt * r8 - 1), 0)),
                  _full((CONV_K, XBC)), _full((1, XBC))],
        out_specs=[pl.BlockSpec((tm, XBC), lambda i: (i, 0)), _full((CONV_K, XBC)), _full((1, XBC))],
        out_shape=[jax.ShapeDtypeStruct((t, XBC), BF16), jax.ShapeDtypeStruct((CONV_K, XBC), F32),
                   jax.ShapeDtypeStruct((1, XBC), F32)],
        scratch_shapes=[pltpu.VMEM((tm + 16, XBC), F32), pltpu.VMEM((tm + 8, XBC), F32)],
        compiler_params=_cp(("arbitrary",)),
    )(proj, proj, proj, dact, dact, conv_w, conv_b)


def _split3(x):
    h = x.astype(BF16)
    r = x - h.astype(F32)
    m = r.astype(BF16)
    lo = (r - m.astype(F32)).astype(BF16)
    return h, m, lo


def _tri_mm(tri, x):
    h, m, lo = _split3(x)
    return _dot(tri, h) + _dot(tri, m) + _dot(tri, lo)


def _softplus(x):
    return jnp.maximum(x, 0.0) + jnp.log1p(jnp.exp(-jnp.abs(x)))


def _chunk_decays(dt_raw, dtb, alog):
    dtv = _softplus(dt_raw + dtb)
    a = -jnp.exp(alog)
    ri = lax.broadcasted_iota(jnp.int32, (BLK, BLK), 0)
    ci = lax.broadcasted_iota(jnp.int32, (BLK, BLK), 1)
    causal = ri >= ci
    acum = _tri_mm(causal.astype(BF16), dtv * a)
    return dtv, a, causal, acum, acum.T


def ssd_fwd(act, proj, dtb_p, alog_p, dsk_p):
    t = act.shape[0]
    nc = t // BLK

    def body(xs_ref, b_ref, c_ref, dt_ref, dtb_ref, al_ref, dk_ref, y_ref, sp_ref, st):
        c = pl.program_id(0)

        @pl.when(c == 0)
        def _():
            st[...] = jnp.zeros_like(st)

        sp_ref[0] = st[...]
        dtv, a, causal, acum, acum_t = _chunk_decays(dt_ref[...], dtb_ref[...], al_ref[...])
        exp_a = jnp.exp(acum)
        alast = acum[BLK - 1:BLK, :]
        dte = jnp.exp(alast - acum)
        cd = jnp.exp(alast)
        ys, sts = [], []
        for g in range(SG):
            bb = b_ref[:, SN * g:SN * (g + 1)].astype(BF16)
            cb = c_ref[:, SN * g:SN * (g + 1)].astype(BF16)
            gm = _dot_nt(cb, bb)
            for r in range(SR):
                hd = SR * g + r
                hs = slice(SP * hd, SP * (hd + 1))
                lam = jnp.exp(jnp.where(causal, acum[:, hd:hd + 1] - acum_t[hd:hd + 1, :], NEG))
                xh = xs_ref[:, hs]
                xdt = xh * dtv[:, hd:hd + 1]
                yd = _dot((gm * lam).astype(BF16), xdt.astype(BF16))
                sh = st[hs, :]
                yo = _dot_nt(cb, sh.astype(BF16)) * exp_a[:, hd:hd + 1]
                ys.append(yd + yo + dk_ref[:, hd:hd + 1] * xh)
                xe = (xdt * dte[:, hd:hd + 1]).astype(BF16)
                sts.append(cd[:, hd:hd + 1] * sh + _dot_tn(xe, bb))
        for hd in range(SH):
            y_ref[:, SP * hd:SP * (hd + 1)] = ys[hd]
            st[SP * hd:SP * (hd + 1), :] = sts[hd]

    vec = _full((1, 128))
    return pl.pallas_call(
        body, name="ssd_fwd", grid=(nc,),
        in_specs=[pl.BlockSpec((BLK, SSM_W), lambda c: (c, 0)),
                  pl.BlockSpec((BLK, SG * SN), lambda c: (c, SSM_W // (SG * SN))),
                  pl.BlockSpec((BLK, SG * SN), lambda c: (c, SSM_W // (SG * SN) + 1)),
                  pl.BlockSpec((BLK, 128), lambda c: (c, C_DT // 128)), vec, vec, vec],
        out_specs=[pl.BlockSpec((BLK, SSM_W), lambda c: (c, 0)), pl.BlockSpec((1, SSM_W, SN), lambda c: (c, 0, 0))],
        out_shape=[jax.ShapeDtypeStruct((t, SSM_W), F32), jax.ShapeDtypeStruct((nc, SSM_W, SN), F32)],
        scratch_shapes=[pltpu.VMEM((SSM_W, SN), F32)],
        compiler_params=_cp(("arbitrary",)),
    )(act, act, act, proj, dtb_p, alog_p, dsk_p)


def ssd_bwd(act, proj, dy, sprev, dtb_p, alog_p, dsk_p):
    t = act.shape[0]
    nc = t // BLK

    def body(xs_ref, b_ref, c_ref, dt_ref, dy_ref, sp_ref, dtb_ref, al_ref, dk_ref,
             da_ref, ddt_ref, ddtb_ref, dal_ref, ddk_ref, dst):
        i = pl.program_id(0)

        @pl.when(i == 0)
        def _():
            dst[...] = jnp.zeros_like(dst)
            ddtb_ref[...] = jnp.zeros_like(ddtb_ref)
            dal_ref[...] = jnp.zeros_like(dal_ref)
            ddk_ref[...] = jnp.zeros_like(ddk_ref)

        dt_raw = dt_ref[...]
        dtb = dtb_ref[...]
        dtv, a, causal, acum, acum_t = _chunk_decays(dt_raw, dtb, al_ref[...])
        exp_a = jnp.exp(acum)
        alast = acum[BLK - 1:BLK, :]
        dte = jnp.exp(alast - acum)
        cd = jnp.exp(alast)
        lane_c = lax.broadcasted_iota(jnp.int32, (BLK, 128), 1)
        sub_r = lax.broadcasted_iota(jnp.int32, (128, BLK), 0)
        lane_1 = lax.broadcasted_iota(jnp.int32, (1, 128), 1)
        da_col = jnp.zeros((BLK, 128), F32)
        da_row = jnp.zeros((128, BLK), F32)
        da_last = jnp.zeros((1, 128), F32)
        ddt = jnp.zeros((BLK, 128), F32)
        ddk = jnp.zeros((1, 128), F32)
        dxs_out, new_dst, dbs, dcs = [], [], [], []
        for g in range(SG):
            gs = slice(SN * g, SN * (g + 1))
            bf = b_ref[:, gs]
            cf = c_ref[:, gs]
            bb = bf.astype(BF16)
            cb = cf.astype(BF16)
            gm = _dot_nt(cb, bb)
            dgm = jnp.zeros((BLK, BLK), F32)
            dbg = jnp.zeros((BLK, SN), F32)
            dcg = jnp.zeros((BLK, SN), F32)
            for r in range(SR):
                hd = SR * g + r
                hs = slice(SP * hd, SP * (hd + 1))
                lam = jnp.exp(jnp.where(causal, acum[:, hd:hd + 1] - acum_t[hd:hd + 1, :], NEG))
                xh = xs_ref[:, hs]
                dth = dtv[:, hd:hd + 1]
                xdt = xh * dth
                xdtb = xdt.astype(BF16)
                dyh = dy_ref[:, hs]
                dyb = dyh.astype(BF16)
                eh = dte[:, hd:hd + 1]
                xa = exp_a[:, hd:hd + 1]
                cdh = cd[:, hd:hd + 1]
                sph = sp_ref[0, hs, :]
                dsh = dst[hs, :]
                dshb = dsh.astype(BF16)
                dm = _dot_nt(dyb, xdtb)
                gl = gm * lam
                w = dm * gl
                da_col = da_col + jnp.where(lane_c == hd, _rsum(w), 0.0)
                da_row = da_row + jnp.where(sub_r == hd, _csum(w), 0.0)
                dgm = dgm + dm * lam
                dxdt = _dot_tn(gl.astype(BF16), dyb)
                yoff = _dot_nt(cb, sph.astype(BF16)) * xa
                dxs = _dot_nt(bb, dshb) * eh
                zl = _rsum(xdt * dxs)
                da_col = da_col + jnp.where(lane_c == hd, _rsum(dyh * yoff) - zl, 0.0)
                da_last = da_last + jnp.where(lane_1 == hd, _csum(zl) + cdh * _asum(dsh * sph), 0.0)
                dxdt = dxdt + dxs
                dya = (dyh * xa).astype(BF16)
                dcg = dcg + _dot(dya, sph.astype(BF16))
                dbg = dbg + _dot((xdt * eh).astype(BF16), dshb)
                new_dst.append(cdh * dsh + _dot_tn(dya, cb))
                dkh = dk_ref[:, hd:hd + 1]
                dxs_out.append(dxdt * dth + dkh * dyh)
                ddt = ddt + jnp.where(lane_c == hd, _rsum(dxdt * xh), 0.0)
                ddk = ddk + jnp.where(lane_1 == hd, _asum(dyh * xh), 0.0)
            dgb = dgm.astype(BF16)
            dcg = dcg + _dot(dgb, bb)
            dbg = dbg + _dot_tn(dgb, cb)
            dbs.append(dbg)
            dcs.append(dcg)
        for hd in range(SH):
            da_ref[:, SP * hd:SP * (hd + 1)] = dxs_out[hd]
            dst[SP * hd:SP * (hd + 1), :] = new_dst[hd]
        for g in range(SG):
            da_ref[:, SSM_W + SN * g:SSM_W + SN * (g + 1)] = dbs[g]
            da_ref[:, SSM_W + SG * SN + SN * g:SSM_W + SG * SN + SN * (g + 1)] = dcs[g]
        ri = lax.broadcasted_iota(jnp.int32, (BLK, BLK), 0)
        ci = lax.broadcasted_iota(jnp.int32, (BLK, BLK), 1)
        row_i = lax.broadcasted_iota(jnp.int32, (BLK, 128), 0)
        dacum = da_col - da_row.T + jnp.where(row_i == BLK - 1, da_last, 0.0)
        dda = _tri_mm((ri <= ci).astype(BF16), dacum)
        ddt = ddt + dda * a
        dal_ref[...] += _csum(dda * dtv) * a
        ddt_raw = jnp.where(lane_c < SH, ddt * _sig(dt_raw + dtb), 0.0)
        ddt_ref[...] = ddt_raw.astype(BF16)
        ddtb_ref[...] += _csum(ddt_raw)
        ddk_ref[...] += ddk

    rev = lambda i: nc - 1 - i
    vec = _full((1, 128))
    return pl.pallas_call(
        body, name="ssd_bwd", grid=(nc,),
        in_specs=[pl.BlockSpec((BLK, SSM_W), lambda i: (rev(i), 0)),
                  pl.BlockSpec((BLK, SG * SN), lambda i: (rev(i), SSM_W // (SG * SN))),
                  pl.BlockSpec((BLK, SG * SN), lambda i: (rev(i), SSM_W // (SG * SN) + 1)),
                  pl.BlockSpec((BLK, 128), lambda i: (rev(i), C_DT // 128)),
                  pl.BlockSpec((BLK, SSM_W), lambda i: (rev(i), 0)),
                  pl.BlockSpec((1, SSM_W, SN), lambda i: (rev(i), 0, 0)), vec, vec, vec],
        out_specs=[pl.BlockSpec((BLK, XBC), lambda i: (rev(i), 0)), pl.BlockSpec((BLK, 128), lambda i: (rev(i), 0)),
                   vec, vec, vec],
        out_shape=[jax.ShapeDtypeStruct((t, XBC), F32), jax.ShapeDtypeStruct((t, 128), BF16),
                   jax.ShapeDtypeStruct((1, 128), F32), jax.ShapeDtypeStruct((1, 128), F32),
                   jax.ShapeDtypeStruct((1, 128), F32)],
        scratch_shapes=[pltpu.VMEM((SSM_W, SN), F32)],
        compiler_params=_cp(("arbitrary",)),
    )(act, act, act, proj, dy, sprev, dtb_p, alog_p, dsk_p)


TAIL_TM = 128


def _dsilu(z, s):
    return s * (1.0 + z * (1.0 - s))


def tail(proj, ao, yss, x, target, gate, ssm_nw, w_at, w_ss, w_ou):
    t = x.shape[0]
    tm = min(t, TAIL_TM)
    gw = SSM_W // SG

    def body(ao_ref, za_ref, ga_ref, gb_ref, zm_ref, ys_ref, x_ref, tg_ref, gt_ref, nw_ref, wa_ref, ws_ref, wo_ref,
             loss_ref, dy_ref, dao_ref, dza_ref, dga_ref, dgb_ref, dys_ref, dzm_ref,
             ua_ref, yn_ref, mg_ref, dya_ref, dyb_ref, do_ref, dgt_ref, dnw_ref):
        i = pl.program_id(0)

        @pl.when(i == 0)
        def _():
            loss_ref[...] = jnp.zeros_like(loss_ref)
            dgt_ref[...] = jnp.zeros_like(dgt_ref)
            dnw_ref[...] = jnp.zeros_like(dnw_ref)

        ao = ao_ref[...]
        za = za_ref[...]
        sa = _sig(za)
        sila = za * sa
        ua = (ao * sila).astype(BF16)
        ya = _dot(ua, wa_ref[...])
        zm = zm_ref[...]
        sm = _sig(zm)
        silm = zm * sm
        ys = ys_ref[...]
        u = ys * silm
        nw = nw_ref[...]
        rs, uns = [], []
        for g in range(SG):
            ug = u[:, gw * g:gw * (g + 1)]
            r = lax.rsqrt(jnp.mean(ug * ug, axis=-1, keepdims=True) + EPS)
            rs.append(r)
            uns.append(ug * r)
        un = jnp.concatenate(uns, axis=1)
        yn = (un * nw).astype(BF16)
        yb = _dot(yn, ws_ref[...])
        sga = _sig(ga_ref[...])
        sgb = _sig(gb_ref[...])
        mg = (sga * ya + sgb * yb).astype(BF16)
        o = _dot(mg, wo_ref[...])
        gt = gt_ref[...]
        err = (x_ref[...] + gt * o) - tg_ref[...]
        lane = lax.broadcasted_iota(jnp.int32, (1, 128), 1)
        loss_ref[...] += jnp.where(lane == 0, 0.5 * _asum(_rsum(err * err) / D), 0.0)
        dy = err * (1.0 / D)
        dy_ref[...] = dy
        dgt_ref[...] += _csum(dy * o)
        do = (dy * gt).astype(BF16)
        dmg = _dot_nt(do, wo_ref[...])
        dga_ref[...] = (dmg * ya * sga * (1.0 - sga)).astype(BF16)
        dgb_ref[...] = (dmg * yb * sgb * (1.0 - sgb)).astype(BF16)
        dya = (dmg * sga).astype(BF16)
        dyb = (dmg * sgb).astype(BF16)
        dua = _dot_nt(dya, wa_ref[...])
        dao_ref[...] = dua * sila
        dza_ref[...] = (dua * ao * _dsilu(za, sa)).astype(BF16)
        dyn = _dot_nt(dyb, ws_ref[...])
        dnw_ref[...] += _csum(dyn * un)
        dun = dyn * nw
        dus = []
        for g in range(SG):
            gs = slice(gw * g, gw * (g + 1))
            dus.append(rs[g] * (dun[:, gs] - uns[g] * jnp.mean(dun[:, gs] * uns[g], axis=-1, keepdims=True)))
        du = jnp.concatenate(dus, axis=1)
        dys_ref[...] = du * silm
        dzm_ref[...] = (du * ys * _dsilu(zm, sm)).astype(BF16)
        ua_ref[...] = ua
        yn_ref[...] = yn
        mg_ref[...] = mg
        dya_ref[...] = dya
        dyb_ref[...] = dyb
        do_ref[...] = do

    row = lambda w: pl.BlockSpec((tm, w), lambda i: (i, 0))
    pcol = lambda w, c0: pl.BlockSpec((tm, w), lambda i: (i, c0 // w))
    sd = lambda w, dt: jax.ShapeDtypeStruct((t, w), dt)
    return pl.pallas_call(
        body, name="tail", grid=(t // tm,),
        in_specs=[row(D), pcol(D, C_ZA), pcol(D, C_GA), pcol(D, C_GB), pcol(SSM_W, C_ZM), row(SSM_W), row(D), row(D),
                  _full((1, D)), _full((1, SSM_W)), _full((D, D)), _full((SSM_W, D)), _full((D, D))],
        out_specs=[_full((1, 128)), row(D), row(D), row(D), row(D), row(D), row(SSM_W), row(SSM_W),
                   row(D), row(SSM_W), row(D), row(D), row(D), row(D), _full((1, D)), _full((1, SSM_W))],
        out_shape=[jax.ShapeDtypeStruct((1, 128), F32), sd(D, F32), sd(D, F32), sd(D, BF16), sd(D, BF16), sd(D, BF16),
                   sd(SSM_W, F32), sd(SSM_W, BF16), sd(D, BF16), sd(SSM_W, BF16), sd(D, BF16), sd(D, BF16),
                   sd(D, BF16), sd(D, BF16), jax.ShapeDtypeStruct((1, D), F32), jax.ShapeDtypeStruct((1, SSM_W), F32)],
        compiler_params=_cp(("arbitrary",)),
    )(ao, proj, proj, proj, proj, yss, x, target, gate, ssm_nw, w_at, w_ss, w_ou)


def dproj_bwd(dproj, wcat, x, dy, norm_w, scale):
    t = x.shape[0]
    tm = min(t, 512)
    nk = NP // TN
    nt = t // tm

    def body(dp_ref, w_ref, x_ref, dy_ref, nw_ref, sc_ref, gx_ref, dnw_ref, dsc_ref, dsh_ref, acc, dwe_ref):
        i = pl.program_id(0)
        k = pl.program_id(1)

        @pl.when(jnp.logical_and(i == 0, k == 0))
        def _():
            dwe_ref[...] = jnp.zeros_like(dwe_ref)
            dsh_ref[...] = jnp.zeros_like(dsh_ref)
            dnw_ref[...] = jnp.zeros_like(dnw_ref)
            dsc_ref[...] = jnp.zeros_like(dsc_ref)

        part = _dot_nt(dp_ref[...], w_ref[...])

        @pl.when(k == 0)
        def _():
            acc[...] = part

        @pl.when(k > 0)
        def _():
            acc[...] += part

        @pl.when(k == nk - 1)
        def _():
            dh = acc[...]
            xv = x_ref[...]
            r = lax.rsqrt(jnp.mean(xv * xv, axis=-1, keepdims=True) + EPS)
            xn = xv * r
            weff = nw_ref[...] * (1.0 + sc_ref[...])
            dxn = dh * weff
            gx_ref[...] = dy_ref[...] + r * (dxn - xn * jnp.mean(dxn * xn, axis=-1, keepdims=True))
            dwe_ref[...] += _csum(dh * xn)
            dsh_ref[...] += _csum(dh)

        @pl.when(jnp.logical_and(i == nt - 1, k == nk - 1))
        def _():
            dwe = dwe_ref[...]
            dnw_ref[...] = dwe * (1.0 + sc_ref[...])
            dsc_ref[...] = dwe * nw_ref[...]

    vec = pl.BlockSpec((1, D), lambda i, k: (0, 0))
    row = pl.BlockSpec((tm, D), lambda i, k: (i, 0))
    return pl.pallas_call(
        body, name="dproj_bwd", grid=(nt, nk),
        in_specs=[pl.BlockSpec((tm, TN), lambda i, k: (i, k)), pl.BlockSpec((D, TN), lambda i, k: (0, k)),
                  row, row, vec, vec],
        out_specs=[row, vec, vec, vec],
        out_shape=[jax.ShapeDtypeStruct((t, D), F32), jax.ShapeDtypeStruct((1, D), F32),
                   jax.ShapeDtypeStruct((1, D), F32), jax.ShapeDtypeStruct((1, D), F32)],
        scratch_shapes=[pltpu.VMEM((tm, D), F32), pltpu.VMEM((1, D), F32)],
        compiler_params=_cp(("arbitrary", "arbitrary")),
    )(dproj, wcat, x, dy, norm_w, scale)


def xty(a, b, name, bm=512, bn=768):
    t, m = a.shape
    n = b.shape[1]
    tk = min(t, 512)
    bm = min(bm, m)
    bn = min(bn, n)
    nk = t // tk

    def body(a_ref, b_ref, o_ref):
        part = _dot_tn(a_ref[...], b_ref[...])

        @pl.when(pl.program_id(2) == 0)
        def _():
            o_ref[...] = part

        @pl.when(pl.program_id(2) > 0)
        def _():
            o_ref[...] += part

    return pl.pallas_call(
        body, name=name, grid=(m // bm, n // bn, nk),
        in_specs=[pl.BlockSpec((tk, bm), lambda i, j, k: (k, i)), pl.BlockSpec((tk, bn), lambda i, j, k: (k, j))],
        out_specs=pl.BlockSpec((bm, bn), lambda i, j, k: (i, j)),
        out_shape=jax.ShapeDtypeStruct((m, n), F32),
        compiler_params=_cp(("parallel", "parallel", "arbitrary")),
    )(a, b)


SUM_TR = 256


def pair_sum(mine, theirs):
    r = mine.shape[0]

    def body(a_ref, b_ref, o_ref, ob_ref):
        s = a_ref[...] + b_ref[...]
        o_ref[...] = s
        ob_ref[...] = s.astype(BF16)

    spec = pl.BlockSpec((SUM_TR, 1024), lambda i: (i, 0))
    return pl.pallas_call(body, name="pair_sum", grid=(r // SUM_TR,), in_specs=[spec, spec], out_specs=[spec, spec],
                          out_shape=[jax.ShapeDtypeStruct((r, 1024), F32), jax.ShapeDtypeStruct((r, 1024), BF16)],
                          compiler_params=_cp(("parallel",)))(mine, theirs)


def chip_sum(own, others):
    r = own.shape[0]

    def body(a_ref, b_ref, o_ref):
        acc = a_ref[...]
        for k in range(3):
            acc = acc + b_ref[k].astype(F32)
        o_ref[...] = acc

    spec = pl.BlockSpec((SUM_TR, 1024), lambda i: (i, 0))
    return pl.pallas_call(body, name="chip_sum", grid=(r // SUM_TR,),
                          in_specs=[spec, pl.BlockSpec((3, SUM_TR, 1024), lambda i: (0, i, 0))], out_specs=spec,
                          out_shape=jax.ShapeDtypeStruct((r, 1024), F32), compiler_params=_cp(("parallel",)))(own, others)


def sum_devices(g):
    r = g.shape[1]

    def body(g_ref, o_ref):
        acc = g_ref[0]
        for d in range(1, 8):
            acc = acc + g_ref[d]
        o_ref[...] = acc

    return pl.pallas_call(body, name="sum_devices", out_shape=jax.ShapeDtypeStruct((r, 1024), F32),
                          compiler_params=_cp())(g)


def adamw(w, g, m, v, name):
    r, c = w.shape
    tr = r
    for cand in (256, 128, 64, 32, 16, 8):
        if r % cand == 0 and r > cand:
            tr = cand
            break

    def body(w_ref, g_ref, m_ref, v_ref, d_ref, nm_ref, nv_ref):
        gv = g_ref[...]
        mn = ADAM_B1 * m_ref[...] + (1.0 - ADAM_B1) * gv
        vn = ADAM_B2 * v_ref[...] + (1.0 - ADAM_B2) * (gv * gv)
        m_hat = mn / (1.0 - ADAM_B1 ** ADAM_STEP)
        v_hat = vn / (1.0 - ADAM_B2 ** ADAM_STEP)
        d_ref[...] = -ADAM_LR * (m_hat / (jnp.sqrt(v_hat) + ADAM_EPS) + ADAM_WD * w_ref[...])
        nm_ref[...] = mn
        nv_ref[...] = vn

    spec = pl.BlockSpec((tr, c), lambda i: (i, 0))
    sd = jax.ShapeDtypeStruct((r, c), F32)
    return pl.pallas_call(body, name=name, grid=(r // tr,), in_specs=[spec] * 4, out_specs=[spec] * 3,
                          out_shape=[sd, sd, sd], compiler_params=_cp(("parallel",)))(w, g, m, v)


ANY = pl.BlockSpec(memory_space=pl.ANY)
VM = pl.BlockSpec(memory_space=pltpu.VMEM)
OTHER_CHIPS = ((1, 0), (0, 1), (1, 1))


def _pos():
    return lax.axis_index("x"), lax.axis_index("y"), lax.axis_index("c")


def _flip(v, bit):
    return 1 - v if bit else v


def _rcopy(src, dst, ssem, rsem, peer):
    return pltpu.make_async_remote_copy(src_ref=src, dst_ref=dst, send_sem=ssem, recv_sem=rsem,
                                        device_id=peer, device_id_type=MESH)


def allgather_small(p, name):
    r = p.shape[0]

    def body(in_ref, out_ref, ssem, rsem, lsem):
        x, y, c = _pos()
        me = 4 * x + 2 * y + c
        loc = pltpu.make_async_copy(in_ref, out_ref.at[me], lsem)
        loc.start()
        sends = []
        peers = []
        for k in range(1, 8):
            px, py, pc = _flip(x, (k >> 2) & 1), _flip(y, (k >> 1) & 1), _flip(c, k & 1)
            peers.append((px, py, pc))
            cp = _rcopy(in_ref, out_ref.at[me], ssem.at[k - 1], rsem.at[k - 1], (px, py, pc))
            cp.start()
            sends.append(cp)
        for k in range(1, 8):
            px, py, pc = peers[k - 1]
            _rcopy(in_ref, out_ref.at[4 * px + 2 * py + pc], ssem.at[k - 1], rsem.at[k - 1], (px, py, pc)).wait_recv()
        for cp in sends:
            cp.wait_send()
        loc.wait()

    return pl.pallas_call(
        body, name=name, out_shape=jax.ShapeDtypeStruct((8, r, 1024), F32),
        in_specs=[VM], out_specs=VM,
        scratch_shapes=[pltpu.SemaphoreType.DMA((7,)), pltpu.SemaphoreType.DMA((7,)), pltpu.SemaphoreType.DMA],
    )(p)


def gather_weights(wpack, mod_sh):
    def body(w_ref, m_ref, wg_ref, mo_ref, ssem, rsem, lsem):
        x, y, c = _pos()
        chip = 2 * x + y
        mine = pl.ds(pl.multiple_of(c * HALF, 16), HALF)
        other = pl.ds(pl.multiple_of((1 - c) * HALF, 16), HALF)
        sib = (x, y, 1 - c)
        loc_w = pltpu.make_async_copy(w_ref, wg_ref.at[chip], lsem.at[0])
        loc_m = pltpu.make_async_copy(m_ref, mo_ref.at[chip], lsem.at[1])
        loc_w.start()
        loc_m.start()
        sends = []
        for k, (fx, fy) in enumerate(OTHER_CHIPS):
            peer = (_flip(x, fx), _flip(y, fy), c)
            cw = _rcopy(w_ref.at[mine], wg_ref.at[chip, mine], ssem.at[k], rsem.at[k], peer)
            cm = _rcopy(m_ref, mo_ref.at[chip], ssem.at[6 + k], rsem.at[6 + k], peer)
            cw.start()
            cm.start()
            sends += [cw, cm]
        for k, (fx, fy) in enumerate(OTHER_CHIPS):
            px, py = _flip(x, fx), _flip(y, fy)
            got = wg_ref.at[2 * px + py, mine]
            _rcopy(w_ref.at[mine], got, ssem.at[k], rsem.at[k], (px, py, c)).wait_recv()
            fw = _rcopy(got, got, ssem.at[3 + k], rsem.at[3 + k], sib)
            fw.start()
            sends.append(fw)
        for k, (fx, fy) in enumerate(OTHER_CHIPS):
            px, py = _flip(x, fx), _flip(y, fy)
            land = wg_ref.at[2 * px + py, other]
            _rcopy(land, land, ssem.at[3 + k], rsem.at[3 + k], sib).wait_recv()
            _rcopy(m_ref, mo_ref.at[2 * px + py], ssem.at[6 + k], rsem.at[6 + k], (px, py, c)).wait_recv()
        for cp in sends:
            cp.wait_send()
        loc_w.wait()
        loc_m.wait()

    return pl.pallas_call(
        body, name="gather_weights",
        out_shape=[jax.ShapeDtypeStruct((4, ROWS, 1024), BF16), jax.ShapeDtypeStruct((4, 8, 768), F32)],
        in_specs=[ANY, VM], out_specs=[ANY, VM],
        scratch_shapes=[pltpu.SemaphoreType.DMA((9,)), pltpu.SemaphoreType.DMA((9,)), pltpu.SemaphoreType.DMA((2,))],
    )(wpack, mod_sh)


def pair_exchange(gpack):
    def body(g_ref, r_ref, ssem, rsem):
        x, y, c = _pos()
        other = pl.ds(pl.multiple_of((1 - c) * HALF, 8), HALF)
        cp = _rcopy(g_ref.at[:, other, :], r_ref, ssem, rsem, (x, y, 1 - c))
        cp.start()
        cp.wait()

    return pl.pallas_call(
        body, name="pair_exchange", out_shape=jax.ShapeDtypeStruct((4, HALF, 1024), F32),
        in_specs=[ANY], out_specs=ANY,
        scratch_shapes=[pltpu.SemaphoreType.DMA, pltpu.SemaphoreType.DMA],
    )(gpack)


def chip_exchange(part, part_b):
    def body(p_ref, pb_ref, own_ref, r_ref, ssem, rsem, lsem):
        x, y, c = _pos()
        chip = 2 * x + y
        loc = pltpu.make_async_copy(p_ref.at[chip], own_ref, lsem)
        loc.start()
        sends = []
        for k, (fx, fy) in enumerate(OTHER_CHIPS):
            px, py = _flip(x, fx), _flip(y, fy)
            cp = _rcopy(pb_ref.at[2 * px + py], r_ref.at[k], ssem.at[k], rsem.at[k], (px, py, c))
            cp.start()
            sends.append(cp)
        for k, (fx, fy) in enumerate(OTHER_CHIPS):
            px, py = _flip(x, fx), _flip(y, fy)
            _rcopy(pb_ref.at[chip], r_ref.at[k], ssem.at[k], rsem.at[k], (px, py, c)).wait_recv()
        for cp in sends:
            cp.wait_send()
        loc.wait()

    return pl.pallas_call(
        body, name="chip_exchange",
        out_shape=[jax.ShapeDtypeStruct((HALF, 1024), F32), jax.ShapeDtypeStruct((3, HALF, 1024), BF16)],
        in_specs=[ANY, ANY], out_specs=[ANY, ANY],
        scratch_shapes=[pltpu.SemaphoreType.DMA((3,)), pltpu.SemaphoreType.DMA((3,)), pltpu.SemaphoreType.DMA],
    )(part, part_b)


def pair_allgather(red):
    def body(r_ref, o_ref, ssem, rsem, lsem):
        x, y, c = _pos()
        mine = pl.ds(pl.multiple_of(c * HALF, 8), HALF)
        other = pl.ds(pl.multiple_of((1 - c) * HALF, 8), HALF)
        loc = pltpu.make_async_copy(r_ref, o_ref.at[mine], lsem)
        loc.start()
        cp = _rcopy(r_ref, o_ref.at[mine], ssem, rsem, (x, y, 1 - c))
        cp.start()
        _rcopy(r_ref, o_ref.at[other], ssem, rsem, (x, y, 1 - c)).wait_recv()
        cp.wait_send()
        loc.wait()

    return pl.pallas_call(
        body, name="pair_allgather", out_shape=jax.ShapeDtypeStruct((ROWS, 1024), F32),
        in_specs=[ANY], out_specs=ANY,
        scratch_shapes=[pltpu.SemaphoreType.DMA, pltpu.SemaphoreType.DMA, pltpu.SemaphoreType.DMA],
    )(red)


def _row(v, width=1024):
    v = v.reshape(-1)
    n = -(-v.shape[0] // width) * width
    return jnp.pad(v, (0, n - v.shape[0])).reshape(-1, width)


def _slots(vs):
    row = [jnp.pad(v.reshape(-1), (0, 128 - v.size)) for v in vs]
    row += [jnp.zeros((128,), F32)] * (8 - len(row))
    return jnp.concatenate(row).reshape(1, 1024)


def _pack_small(b_ada, norm_w, conv_b, ssm_norm_w, q_norm_w, k_norm_w, sinks, dt_bias, a_log, d_skip, rel_bias,
                extra=None):
    misc = [q_norm_w, k_norm_w, sinks, dt_bias, a_log, d_skip] + ([] if extra is None else [extra])
    rows = [_row(b_ada), _row(norm_w), _row(conv_b), _row(ssm_norm_w), _slots(misc), _row(rel_bias)]
    rows.append(jnp.zeros((5, 1024), F32))
    return jnp.concatenate(rows, axis=0)


def _unpack_small(p):
    misc = p[9]
    return dict(b_ada=p[0:3].reshape(1, 3072), norm_w=p[3:4], conv_b=p[4:7].reshape(1, 3072),
                ssm_norm_w=p[7:9].reshape(1, 2048), q_norm_w=misc[None, 0:64], k_norm_w=misc[None, 128:192],
                sinks=misc[None, 256:272], dt_bias=misc[None, 384:416], a_log=misc[None, 512:544],
                d_skip=misc[None, 640:672], rel_bias=p[10, :512].reshape(32, 16), extra=misc[768])


SMALL = ("b_ada", "norm_w", "conv_b", "ssm_norm_w", "q_norm_w", "k_norm_w", "sinks", "dt_bias", "a_log", "d_skip",
         "rel_bias")
WEIGHTS = ("w_ada", "b_ada", "norm_w", "w_in", "q_norm_w", "k_norm_w", "rel_bias", "sinks", "conv_w", "conv_b",
           "dt_bias", "a_log", "d_skip", "ssm_norm_w", "w_attn_proj", "w_ssm_proj", "w_out")
IN_COLS = ((0, 1024, C_Q), (1024, 256, C_K), (1280, 256, C_V), (1536, 1024, C_ZA), (2560, 2048, C_ZM),
           (4608, 3072, C_XBC), (7680, 32, C_DT), (7712, 1024, C_GA), (8736, 1024, C_GB))


def _to_cat(w_full):
    by_new = sorted(IN_COLS, key=lambda e: e[2])
    parts, pos = [], 0
    for o, n, cnew in by_new:
        assert cnew == pos
        parts.append(w_full[:, o:o + n])
        pos += n
    parts.append(jnp.zeros((w_full.shape[0], NP - pos), w_full.dtype))
    return jnp.concatenate(parts, axis=1)


def _from_cat(w_cat):
    return jnp.concatenate([w_cat[:, cnew:cnew + n] for o, n, cnew in IN_COLS], axis=1)


def kernel(x, c, w_ada, b_ada, norm_w, w_in, q_norm_w, k_norm_w, rel_bias, sinks, conv_w, conv_b, dt_bias, a_log, d_skip, ssm_norm_w, w_attn_proj, w_ssm_proj, w_out, loss_target, m_w_ada, m_b_ada, m_norm_w, m_w_in, m_q_norm_w, m_k_norm_w, m_rel_bias, m_sinks, m_conv_w, m_conv_b, m_dt_bias, m_a_log, m_d_skip, m_ssm_norm_w, m_w_attn_proj, m_w_ssm_proj, m_w_out, v_w_ada, v_b_ada, v_norm_w, v_w_in, v_q_norm_w, v_k_norm_w, v_rel_bias, v_sinks, v_conv_w, v_conv_b, v_dt_bias, v_a_log, v_d_skip, v_ssm_norm_w, v_w_attn_proj, v_w_ssm_proj, v_w_out):
    args = dict(locals())
    xi, yi, ci = lax.axis_index("x"), lax.axis_index("y"), lax.axis_index("c")
    chip = 2 * xi + yi
    me = 4 * xi + 2 * yi + ci
    x2 = x[0]
    tgt = loss_target[0]

    pay = jnp.concatenate([c, conv_w[0].reshape(3, 1024), jnp.zeros((4, 1024), F32)], axis=0)
    g0 = allgather_small(pay, "gather_cond")
    c_all = g0[:, 0, :]
    conv_w_full = g0[0::2, 1:4, :].reshape(4, CONV_K, 768).transpose(1, 0, 2).reshape(CONV_K, XBC)

    b_ada_sh = lax.dynamic_slice(b_ada, (0, chip * 768), (1, 768))
    mod_sh = ada_mod(c_all, w_ada[0], b_ada_sh)

    wpack = jnp.concatenate([w_in[0].astype(BF16).reshape(R_IN, 1024), w_attn_proj[0].astype(BF16),
                             w_ssm_proj[0].astype(BF16), w_out[0].astype(BF16),
                             jnp.zeros((ROWS - R_IN - R_AT - R_SS - R_OU, 1024), BF16)], axis=0)
    wg, modg = gather_weights(wpack, mod_sh)
    mod = lax.dynamic_slice(modg, (0, me, 0), (4, 1, 768)).reshape(1, 3 * D)
    shift, scale, gate = mod[:, :D], mod[:, D:2 * D], mod[:, 2 * D:]
    w_in_full = wg[:, :R_IN].reshape(4, D, IN_W // 4).transpose(1, 0, 2).reshape(D, IN_W)
    wcat = _to_cat(w_in_full)
    o1 = R_IN
    w_at = wg[:, o1:o1 + R_AT].reshape(D, D)
    w_ss = wg[:, o1 + R_AT:o1 + R_AT + R_SS].reshape(SSM_W, D)
    w_ou = wg[:, o1 + R_AT + R_SS:o1 + R_AT + R_SS + R_OU].reshape(D, D)

    pad128 = lambda v: jnp.pad(v, ((0, 0), (0, 128 - v.shape[1])))
    dtb_p, alog_p, dsk_p = pad128(dt_bias), pad128(a_log), pad128(d_skip)
    bucket = _bucket_table()

    proj, h = norm_proj(x2, norm_w, scale, shift, wcat)
    biasm = bias_expand(rel_bias, sinks, bucket)
    ao = attn_fwd(proj, biasm, q_norm_w, k_norm_w)
    act = conv_fwd(proj, conv_w_full, conv_b)
    yss, sprev = ssd_fwd(act, proj, dtb_p, alog_p, dsk_p)

    (loss_p, dy, dao, dza, dga, dgb, dyss, dzm, ua, yn, mg, dya, dyb, dout, dgate, dssm_nw) = tail(
        proj, ao, yss, x2, tgt, gate, ssm_norm_w, w_at, w_ss, w_ou)

    dq, dk, dv, dqw, dkw, dacc = attn_bwd(proj, dao, biasm, q_norm_w, k_norm_w)
    dbias = bias_reduce(dacc, bucket)
    drb = dbias[:, :NBUCKET].T
    dsk = dbias[:, NBUCKET].reshape(1, HQ)
    dact, ddt, ddtb, dalog, ddskip = ssd_bwd(act, proj, dyss, sprev, dtb_p, alog_p, dsk_p)
    dxbc, dconv_w, dconv_b = conv_bwd(proj, dact, conv_w_full, conv_b)

    t = x2.shape[0]
    dproj = jnp.concatenate([dq, dza, dga, dgb, dzm, dxbc, dk, dv, ddt, jnp.zeros((t, NP - C_DT - 128), BF16)], axis=1)
    grad_x, dnorm_w, dscale, dshift = dproj_bwd(dproj, wcat, x2, dy, norm_w, scale)
    dwcat = xty(h, dproj, "dw_in", bm=512, bn=TN)
    dw_at = xty(ua, dya, "dw_attn", bm=512, bn=512)
    dw_ss = xty(yn, dyb, "dw_ssm", bm=512, bn=512)
    dw_ou = xty(mg, dout, "dw_out", bm=512, bn=512)

    g_in = _from_cat(dwcat).reshape(D, 4, IN_W // 4).transpose(1, 0, 2).reshape(4, R_IN, 1024)
    gpack = jnp.concatenate([g_in, dw_at.reshape(4, R_AT, 1024), dw_ss.reshape(4, R_SS, 1024),
                             dw_ou.reshape(4, R_OU, 1024),
                             jnp.zeros((4, ROWS - R_IN - R_AT - R_SS - R_OU, 1024), F32)], axis=1)
    from_sib = pair_exchange(gpack)
    my_half = lax.dynamic_slice_in_dim(gpack, ci * HALF, HALF, axis=1)
    part, part_b = pair_sum(my_half.reshape(4 * HALF, 1024), from_sib.reshape(4 * HALF, 1024))
    own, others = chip_exchange(part.reshape(4, HALF, 1024), part_b.reshape(4, HALF, 1024))
    red = chip_sum(own, others)
    g_shard = pair_allgather(red)

    dmod = jnp.concatenate([dshift, dscale, dgate], axis=1)
    gsmall = jnp.concatenate([
        _pack_small(dmod, dnorm_w, dconv_b, dssm_nw, dqw, dkw, dsk[:, :HQ], ddtb[:, :SH], dalog[:, :SH],
                    ddskip[:, :SH], drb, extra=loss_p[:, :1]),
        dconv_w.reshape(12, 1024), jnp.zeros((4, 1024), F32)], axis=0)
    gall = allgather_small(gsmall, "gather_small_grads")
    ssum = sum_devices(gall)
    gs = _unpack_small(ssum[:16])
    loss = gs["extra"]
    dconv_w_sh = lax.dynamic_slice(ssum[16:28].reshape(CONV_K, XBC), (0, chip * 768), (CONV_K, 768))
    dmod_all = gall[:, 0:3, :].reshape(8, 3 * D)
    dw_ada = ada_grad(c_all, lax.dynamic_slice(dmod_all, (0, chip * 768), (8, 768)))

    grads = dict(gs)
    grads["w_ada"] = dw_ada
    grads["w_in"] = g_shard[:R_IN].reshape(D, IN_W // 4)
    grads["w_attn_proj"] = g_shard[o1:o1 + R_AT]
    grads["w_ssm_proj"] = g_shard[o1 + R_AT:o1 + R_AT + R_SS]
    grads["w_out"] = g_shard[o1 + R_AT + R_SS:o1 + R_AT + R_SS + R_OU]
    grads["conv_w"] = dconv_w_sh

    delta, new_m, new_v = {}, {}, {}
    for n in ("w_ada", "w_in", "conv_w", "w_attn_proj", "w_ssm_proj", "w_out"):
        delta[n], new_m[n], new_v[n] = adamw(args[n][0], grads[n], args["m_" + n][0], args["v_" + n][0], "adamw_" + n)
    ws = _pack_small(*[args[n] for n in SMALL])
    ms = _pack_small(*[args["m_" + n] for n in SMALL])
    vs = _pack_small(*[args["v_" + n] for n in SMALL])
    d_s, m_s, v_s = adamw(ws, ssum[:16], ms, vs, "adamw_small")
    d_s, m_s, v_s = _unpack_small(d_s), _unpack_small(m_s), _unpack_small(v_s)
    for n in SMALL:
        delta[n], new_m[n], new_v[n] = d_s[n], m_s[n], v_s[n]

    def shaped(n, a):
        return a.reshape(args[n].shape)

    outs = [loss, grad_x[None]]
    for table in (grads, delta, new_m, new_v):
        outs += [shaped(n, table[n]) for n in WEIGHTS]
    return tuple(outs)
```

```python
import functools
import math

import numpy as np
import jax
import jax.numpy as jnp
from jax import lax
from jax.experimental import pallas as pl
from jax.experimental.pallas import tpu as pltpu

F32 = jnp.float32
BF16 = jnp.bfloat16
MESH = pl.DeviceIdType.MESH

D = 1024
HQ, HKV, GRP, DH = 16, 4, 4, 64
BLK = 128
NBUCKET, MAXDIST = 32, 128
SSM_W, SH, SG, SR, SP, SN = 2048, 32, 4, 8, 64, 128
CONV_K = 4
XBC = SSM_W + 2 * SG * SN
IN_W = 9760
EPS = 1e-6
NEG = -1e30
SCALE = DH ** -0.5

C_Q, C_ZA, C_GA, C_GB, C_ZM, C_XBC, C_K, C_V, C_DT = 0, 1024, 2048, 3072, 4096, 6144, 9216, 9472, 9728
NP = 9984
TN = 768

R_IN, R_AT, R_SS, R_OU = 2440, 256, 512, 256
ROWS = 3584
HALF = ROWS // 2

ADAM_LR, ADAM_B1, ADAM_B2, ADAM_EPS, ADAM_WD, ADAM_STEP = 0.001, 0.9, 0.999, 1e-08, 0.01, 10

VMEM_LIMIT = 56 * 1024 * 1024


def _cp(sem=None):
    if sem is None:
        return pltpu.CompilerParams(vmem_limit_bytes=VMEM_LIMIT)
    return pltpu.CompilerParams(dimension_semantics=sem, vmem_limit_bytes=VMEM_LIMIT)


def _sig(x):
    return 1.0 / (1.0 + jnp.exp(-x))


def _dot(a, b):
    return jnp.dot(a, b, preferred_element_type=F32)


def _dot_nt(a, b):
    return lax.dot_general(a, b, (((1,), (1,)), ((), ())), preferred_element_type=F32)


def _dot_tn(a, b):
    return lax.dot_general(a, b, (((0,), (0,)), ((), ())), preferred_element_type=F32)


def _rsum(x):
    return jnp.sum(x, axis=-1, keepdims=True)


def _csum(x):
    return jnp.sum(x, axis=0, keepdims=True)


def _asum(x):
    return _csum(_rsum(x))


def _full(shape):
    nd = len(shape)
    return pl.BlockSpec(shape, lambda *_: (0,) * nd)


def ada_mod(c_all, w_ada_sh, b_ada_sh):
    def body(c_ref, w_ref, b_ref, o_ref):
        cv = c_ref[...]
        s = cv * _sig(cv)
        o_ref[...] = jnp.dot(s, w_ref[...], preferred_element_type=F32,
                             precision=lax.Precision.HIGHEST) + b_ref[...]

    n = w_ada_sh.shape[1]
    return pl.pallas_call(body, name="ada_mod", out_shape=jax.ShapeDtypeStruct((8, n), F32),
                          compiler_params=_cp())(c_all, w_ada_sh, b_ada_sh)


def ada_grad(c_all, dmod_sh):
    def body(c_ref, d_ref, o_ref):
        cv = c_ref[...]
        s = cv * _sig(cv)
        o_ref[...] = lax.dot_general(s, d_ref[...], (((0,), (0,)), ((), ())), preferred_element_type=F32,
                                     precision=lax.Precision.HIGHEST)

    n = dmod_sh.shape[1]
    return pl.pallas_call(body, name="ada_grad", out_shape=jax.ShapeDtypeStruct((D, n), F32),
                          compiler_params=_cp())(c_all, dmod_sh)


def norm_proj(x, norm_w, scale, shift, wcat):
    t = x.shape[0]
    tm = min(t, 1024)

    def body(x_ref, nw_ref, sc_ref, sh_ref, w_ref, p_ref, h_ref, hs):
        @pl.when(pl.program_id(1) == 0)
        def _():
            xv = x_ref[...]
            r = lax.rsqrt(jnp.mean(xv * xv, axis=-1, keepdims=True) + EPS)
            h = (xv * r) * nw_ref[...]
            h = h * (1.0 + sc_ref[...]) + sh_ref[...]
            hb = h.astype(BF16)
            hs[...] = hb
            h_ref[...] = hb

        p_ref[...] = _dot(hs[...], w_ref[...])

    vec = pl.BlockSpec((1, D), lambda i, j: (0, 0))
    return pl.pallas_call(
        body, name="norm_proj", grid=(t // tm, NP // TN),
        in_specs=[pl.BlockSpec((tm, D), lambda i, j: (i, 0)), vec, vec, vec,
                  pl.BlockSpec((D, TN), lambda i, j: (0, j))],
        out_specs=[pl.BlockSpec((tm, TN), lambda i, j: (i, j)), pl.BlockSpec((tm, D), lambda i, j: (i, 0))],
        out_shape=[jax.ShapeDtypeStruct((t, NP), F32), jax.ShapeDtypeStruct((t, D), BF16)],
        scratch_shapes=[pltpu.VMEM((tm, D), BF16)],
        compiler_params=_cp(("parallel", "arbitrary")),
    )(x, norm_w, scale, shift, wcat)


def _bucket_table():
    qi = jnp.arange(BLK)[:, None]
    kj = jnp.arange(2 * BLK)[None, :]
    dist = qi + BLK - kj
    n = jnp.maximum(dist, 0)
    max_exact = NBUCKET // 2
    nf = jnp.maximum(n, 1).astype(F32)
    large = max_exact + (jnp.log(nf / max_exact) / math.log(MAXDIST / max_exact)
                         * (NBUCKET - max_exact)).astype(jnp.int32)
    large = jnp.minimum(large, NBUCKET - 1)
    bucket = jnp.where(n < max_exact, n, large).astype(jnp.int32)
    valid = (dist >= 0) & (dist < BLK)
    return jnp.where(valid, bucket, -1)


def bias_expand(rel_bias, sinks, bucket):
    def body(rb_ref, sk_ref, bk_ref, o_ref):
        hd = pl.program_id(0)
        bk = bk_ref[...]
        col = lax.broadcasted_iota(jnp.int32, (BLK, 2 * BLK), 1)

        def step(b, acc):
            return jnp.where(bk == b, rb_ref[b, hd], acc)

        acc = lax.fori_loop(0, NBUCKET, step, jnp.full((BLK, 2 * BLK), NEG, F32))
        acc = jnp.where(col == 0, sk_ref[0, hd], acc)
        o_ref[1, 0] = acc
        o_ref[0, 0] = jnp.where(jnp.logical_and(col > 0, col < BLK), NEG, acc)

    smem = pl.BlockSpec(memory_space=pltpu.SMEM)
    return pl.pallas_call(
        body, name="bias_expand", grid=(HQ,),
        in_specs=[smem, smem, _full((BLK, 2 * BLK))],
        out_specs=pl.BlockSpec((2, 1, BLK, 2 * BLK), lambda h: (0, h, 0, 0)),
        out_shape=jax.ShapeDtypeStruct((2, HQ, BLK, 2 * BLK), F32),
        compiler_params=_cp(("arbitrary",)),
    )(rel_bias, sinks, bucket)


def bias_reduce(dacc, bucket):
    def body(d_ref, bk_ref, o_ref):
        bk = bk_ref[...]
        lane = lax.broadcasted_iota(jnp.int32, (1, 128), 1)
        col = lax.broadcasted_iota(jnp.int32, (BLK, 2 * BLK), 1)
        for hd in range(HQ):
            dv = d_ref[hd]

            def step(b, row):
                s = _asum(jnp.where(bk == b, dv, 0.0))
                return jnp.where(lane == b, s, row)

            row = lax.fori_loop(0, NBUCKET, step, jnp.zeros((1, 128), F32))
            o_ref[hd:hd + 1, :] = jnp.where(lane == NBUCKET, _asum(jnp.where(col == 0, dv, 0.0)), row)

    return pl.pallas_call(body, name="bias_reduce", out_shape=jax.ShapeDtypeStruct((HQ, 128), F32),
                          compiler_params=_cp())(dacc, bucket)


GQ = GRP * BLK


def _stack_heads(x, nh):
    return jnp.concatenate([x[:, DH * h:DH * (h + 1)] for h in range(nh)], axis=0)


def _unstack(xs, nh):
    rows = xs.shape[0] // nh
    return jnp.concatenate([xs[rows * h:rows * (h + 1)] for h in range(nh)], axis=1)


def _rms(x):
    return lax.rsqrt(jnp.mean(x * x, axis=-1, keepdims=True) + EPS)


def _stack_q(q, qw):
    qs = _stack_heads(q, HQ)
    r = _rms(qs)
    qhat = qs * r
    return qhat * qw, qhat, r


def _band_first(shape):
    return (lax.broadcasted_iota(jnp.int32, shape, 0) & (2 * BLK - 1)) == 0


def _stack_kv(kp, kc, vp, vc, kw):
    ks = _stack_heads(jnp.concatenate([kp, kc], axis=0), HKV)
    r = _rms(ks)
    khat = ks * r
    first = _band_first(ks.shape)
    kn = jnp.where(first, 0.0, khat * kw)
    v2 = jnp.where(first, 0.0, _stack_heads(jnp.concatenate([vp, vc], axis=0), HKV)).astype(BF16)
    return kn, khat, r, v2


def _softmax_rows(s):
    p = jnp.exp(s - jnp.max(s, axis=-1, keepdims=True))
    return p * (1.0 / _rsum(p))


def attn_fwd(proj, biasm, q_norm_w, k_norm_w):
    t = proj.shape[0]
    nb = t // BLK

    def body(q_ref, kc_ref, kp_ref, vc_ref, vp_ref, bm_ref, qw_ref, kw_ref, o_ref):
        qn = _stack_q(q_ref[...], qw_ref[...])[0].astype(BF16)
        kn, _, _, v2 = _stack_kv(kp_ref[...], kc_ref[...], vp_ref[...], vc_ref[...], kw_ref[...])
        knb = kn.astype(BF16)
        s = jnp.concatenate([_dot_nt(qn[GQ * j:GQ * (j + 1)], knb[2 * BLK * j:2 * BLK * (j + 1)])
                             for j in range(HKV)], axis=0)
        pr = _softmax_rows(s * SCALE + bm_ref[0].reshape(HQ * BLK, 2 * BLK)).astype(BF16)
        o = jnp.concatenate([_dot(pr[GQ * j:GQ * (j + 1)], v2[2 * BLK * j:2 * BLK * (j + 1)])
                             for j in range(HKV)], axis=0)
        o_ref[...] = _unstack(o, HQ)

    kblk, vblk = C_K // 256, C_V // 256
    prev = lambda n: jnp.maximum(n - 1, 0)
    return pl.pallas_call(
        body, name="attn_fwd", grid=(nb,),
        in_specs=[pl.BlockSpec((BLK, D), lambda n: (n, 0)),
                  pl.BlockSpec((BLK, 256), lambda n: (n, kblk)),
                  pl.BlockSpec((BLK, 256), lambda n: (prev(n), kblk)),
                  pl.BlockSpec((BLK, 256), lambda n: (n, vblk)),
                  pl.BlockSpec((BLK, 256), lambda n: (prev(n), vblk)),
                  pl.BlockSpec((1, HQ, BLK, 2 * BLK), lambda n: (jnp.minimum(n, 1), 0, 0, 0)),
                  _full((1, DH)), _full((1, DH))],
        out_specs=pl.BlockSpec((BLK, D), lambda n: (n, 0)),
        out_shape=jax.ShapeDtypeStruct((t, D), F32),
        compiler_params=_cp(("parallel",)),
    )(proj, proj, proj, proj, proj, biasm, q_norm_w, k_norm_w)


def attn_bwd(proj, dao, biasm, q_norm_w, k_norm_w):
    t = proj.shape[0]
    nb = t // BLK
    kb = 2 * BLK

    def body(q_ref, kc_ref, kp_ref, vc_ref, vp_ref, do_ref, bm_ref, qw_ref, kw_ref,
             dq_ref, dk_ref, dv_ref, dqw_ref, dkw_ref, dacc_ref, ck, cv, pk, pv, nk, nv):
        n = pl.program_id(0)

        @pl.when(n == 0)
        def _():
            for ref in (dqw_ref, dkw_ref, dacc_ref, ck, cv):
                ref[...] = jnp.zeros_like(ref)

        qw = qw_ref[...]
        kw = kw_ref[...]
        kn, khat, rk, v2 = _stack_kv(kp_ref[...], kc_ref[...], vp_ref[...], vc_ref[...], kw)
        grp = lambda a, j: a[GQ * j:GQ * (j + 1)]
        band = lambda a, j: a[kb * j:kb * (j + 1)]

        @pl.when(n < nb)
        def _():
            qn, qhat, rq = _stack_q(q_ref[...], qw)
            qnb = qn.astype(BF16)
            knb = kn.astype(BF16)
            dos = _stack_heads(do_ref[...], HQ).astype(BF16)
            s = jnp.concatenate([_dot_nt(grp(qnb, j), band(knb, j)) for j in range(HKV)], axis=0)
            pr = _softmax_rows(s * SCALE + bm_ref[0].reshape(HQ * BLK, kb))
            dp = jnp.concatenate([_dot_nt(grp(dos, j), band(v2, j)) for j in range(HKV)], axis=0)
            ds = pr * (dp - _rsum(pr * dp))
            dacc_ref[...] += ds.reshape(HQ, BLK, kb)
            dsb = ds.astype(BF16)
            prb = pr.astype(BF16)
            dqn = jnp.concatenate([_dot(grp(dsb, j), band(knb, j)) for j in range(HKV)], axis=0) * SCALE
            dqhat = dqn * qw
            dq = rq * (dqhat - qhat * jnp.mean(dqhat * qhat, axis=-1, keepdims=True))
            dq_ref[...] = _unstack(dq, HQ).astype(BF16)
            dqw_ref[...] += _csum(dqn * qhat)
            first = _band_first((kb, DH))
            for j in range(HKV):
                rows = slice(BLK * j, BLK * (j + 1))
                dkn = jnp.where(first, 0.0, _dot_tn(grp(dsb, j), grp(qnb, j)) * SCALE)
                dvj = jnp.where(first, 0.0, _dot_tn(grp(prb, j), grp(dos, j)))
                pk[rows, :] = dkn[:BLK]
                nk[rows, :] = dkn[BLK:]
                pv[rows, :] = dvj[:BLK]
                nv[rows, :] = dvj[BLK:]

        @pl.when(n == nb)
        def _():
            for ref in (pk, pv, nk, nv):
                ref[...] = jnp.zeros_like(ref)

        khp = jnp.concatenate([khat[kb * j:kb * j + BLK] for j in range(HKV)], axis=0)
        rkp = jnp.concatenate([rk[kb * j:kb * j + BLK] for j in range(HKV)], axis=0)
        dkn = ck[...] + pk[...]
        dkhat = dkn * kw
        dk = rkp * (dkhat - khp * jnp.mean(dkhat * khp, axis=-1, keepdims=True))
        dk_ref[...] = _unstack(dk, HKV).astype(BF16)
        dkw_ref[...] += _csum(dkn * khp)
        dv_ref[...] = _unstack(cv[...] + pv[...], HKV).astype(BF16)
        ck[...] = nk[...]
        cv[...] = nv[...]

    kblk, vblk = C_K // 256, C_V // 256
    cur = lambda n: jnp.minimum(n, nb - 1)
    prev = lambda n: jnp.maximum(n - 1, 0)
    carry = pltpu.VMEM((HKV * BLK, DH), F32)
    return pl.pallas_call(
        body, name="attn_bwd", grid=(nb + 1,),
        in_specs=[pl.BlockSpec((BLK, D), lambda n: (cur(n), 0)),
                  pl.BlockSpec((BLK, 256), lambda n: (cur(n), kblk)), pl.BlockSpec((BLK, 256), lambda n: (prev(n), kblk)),
                  pl.BlockSpec((BLK, 256), lambda n: (cur(n), vblk)), pl.BlockSpec((BLK, 256), lambda n: (prev(n), vblk)),
                  pl.BlockSpec((BLK, D), lambda n: (cur(n), 0)),
                  pl.BlockSpec((1, HQ, BLK, kb), lambda n: (jnp.minimum(n, 1), 0, 0, 0)),
                  _full((1, DH)), _full((1, DH))],
        out_specs=[pl.BlockSpec((BLK, D), lambda n: (cur(n), 0)),
                   pl.BlockSpec((BLK, 256), lambda n: (prev(n), 0)), pl.BlockSpec((BLK, 256), lambda n: (prev(n), 0)),
                   _full((1, DH)), _full((1, DH)), _full((HQ, BLK, kb))],
        out_shape=[jax.ShapeDtypeStruct((t, D), BF16), jax.ShapeDtypeStruct((t, 256), BF16),
                   jax.ShapeDtypeStruct((t, 256), BF16), jax.ShapeDtypeStruct((1, DH), F32),
                   jax.ShapeDtypeStruct((1, DH), F32), jax.ShapeDtypeStruct((HQ, BLK, kb), F32)],
        scratch_shapes=[carry] * 6,
        compiler_params=_cp(("arbitrary",)),
    )(proj, proj, proj, proj, proj, dao, biasm, q_norm_w, k_norm_w)


CONV_TM = 256


def conv_fwd(proj, conv_w, conv_b):
    t = proj.shape[0]
    tm = min(t, CONV_TM)
    cblk = C_XBC // XBC

    def body(x_ref, xp_ref, w_ref, b_ref, o_ref, xe):
        i = pl.program_id(0)
        xe[0:8, :] = jnp.where(i == 0, 0.0, xp_ref[...])
        xe[8:8 + tm, :] = x_ref[...]
        acc = jnp.broadcast_to(b_ref[...], (tm, XBC))
        for j in range(CONV_K):
            acc = acc + w_ref[j:j + 1, :] * xe[5 + j:5 + j + tm, :]
        o_ref[...] = acc * _sig(acc)

    return pl.pallas_call(
        body, name="conv_fwd", grid=(t // tm,),
        in_specs=[pl.BlockSpec((tm, XBC), lambda i: (i, cblk)),
                  pl.BlockSpec((8, XBC), lambda i: (jnp.maximum(i * (tm // 8) - 1, 0), cblk)),
                  _full((CONV_K, XBC)), _full((1, XBC))],
        out_specs=pl.BlockSpec((tm, XBC), lambda i: (i, 0)),
        out_shape=jax.ShapeDtypeStruct((t, XBC), F32),
        scratch_shapes=[pltpu.VMEM((tm + 8, XBC), F32)],
        compiler_params=_cp(("parallel",)),
    )(proj, proj, conv_w, conv_b)


def conv_bwd(proj, dact, conv_w, conv_b):
    t = proj.shape[0]
    tm = min(t, CONV_TM)
    nt = t // tm
    cblk = C_XBC // XBC

    def body(x_ref, xp_ref, xn_ref, d_ref, dn_ref, w_ref, b_ref, dx_ref, dw_ref, db_ref, xe, de):
        i = pl.program_id(0)

        @pl.when(i == 0)
        def _():
            dw_ref[...] = jnp.zeros_like(dw_ref)
            db_ref[...] = jnp.zeros_like(db_ref)

        xe[0:8, :] = jnp.where(i == 0, 0.0, xp_ref[...])
        xe[8:8 + tm, :] = x_ref[...]
        xe[8 + tm:16 + tm, :] = xn_ref[...]
        pre = jnp.broadcast_to(b_ref[...], (tm + 8, XBC))
        for j in range(CONV_K):
            pre = pre + w_ref[j:j + 1, :] * xe[5 + j:5 + j + tm + 8, :]
        sg = _sig(pre)
        dsilu = sg * (1.0 + pre * (1.0 - sg))
        dpre_c = d_ref[...] * dsilu[0:tm]
        dpre_n = jnp.where(i == nt - 1, 0.0, dn_ref[...] * dsilu[tm:tm + 8])
        de[0:tm, :] = dpre_c
        de[tm:tm + 8, :] = dpre_n
        dx = jnp.zeros((tm, XBC), F32)
        for j in range(CONV_K):
            dx = dx + w_ref[j:j + 1, :] * de[3 - j:3 - j + tm, :]
            dw_ref[j:j + 1, :] += _csum(dpre_c * xe[5 + j:5 + j + tm, :])
        dx_ref[...] = dx.astype(BF16)
        db_ref[...] += _csum(dpre_c)

    r8 = tm // 8
    return pl.pallas_call(
        body, name="conv_bwd", grid=(nt,),
        in_specs=[pl.BlockSpec((tm, XBC), lambda i: (i, cblk)),
                  pl.BlockSpec((8, XBC), lambda i: (jnp.maximum(i * r8 - 1, 0), cblk)),
                  pl.BlockSpec((8, XBC), lambda i: (jnp.minimum((i + 1) * r8, nt * r8 - 1), cblk)),
                  pl.BlockSpec((tm, XBC), lambda i: (i, 0)),
                  pl.BlockSpec((8, XBC), lambda i: (jnp.minimum((i + 1) * r8, nt * r8 - 1), 0)),
                  _full((CONV_K, XBC)), _full((1, XBC))],
        out_specs=[pl.BlockSpec((tm, XBC), lambda i: (i, 0)), _full((CONV_K, XBC)), _full((1, XBC))],
        out_shape=[jax.ShapeDtypeStruct((t, XBC), BF16), jax.ShapeDtypeStruct((CONV_K, XBC), F32),
                   jax.ShapeDtypeStruct((1, XBC), F32)],
        scratch_shapes=[pltpu.VMEM((tm + 16, XBC), F32), pltpu.VMEM((tm + 8, XBC), F32)],
        compiler_params=_cp(("arbitrary",)),
    )(proj, proj, proj, dact, dact, conv_w, conv_b)


def _split3(x):
    h = x.astype(BF16)
    r = x - h.astype(F32)
    m = r.astype(BF16)
    lo = (r - m.astype(F32)).astype(BF16)
    return h, m, lo


def _tri_mm(tri, x):
    h, m, lo = _split3(x)
    return _dot(tri, h) + _dot(tri, m) + _dot(tri, lo)


def _softplus(x):
    return jnp.maximum(x, 0.0) + jnp.log1p(jnp.exp(-jnp.abs(x)))


def _chunk_decays(dt_raw, dtb, alog):
    dtv = _softplus(dt_raw + dtb)
    a = -jnp.exp(alog)
    ri = lax.broadcasted_iota(jnp.int32, (BLK, BLK), 0)
    ci = lax.broadcasted_iota(jnp.int32, (BLK, BLK), 1)
    causal = ri >= ci
    acum = _tri_mm(causal.astype(BF16), dtv * a)
    return dtv, a, causal, acum, acum.T


NPAIR = SH // 2


def _pairs(x):
    return jnp.stack([x[:, 128 * k:128 * (k + 1)] for k in range(NPAIR)])


def _unpairs(x3):
    return jnp.concatenate([x3[k] for k in range(NPAIR)], axis=1)


def _per_head_cols(m):
    return jnp.stack([jnp.broadcast_to(m[:, h:h + 1], m.shape) for h in range(SH)])


def _pair_lanes(t):
    r = t.reshape(NPAIR, 2, t.shape[1], 128)
    lo = lax.broadcasted_iota(jnp.int32, (1, t.shape[1], 128), 2) < SP
    return jnp.where(lo, r[:, 0], r[:, 1])


class _Chunk:
    pass


def _chunk_common(dt_raw, dtb, alog, dskip):
    cm = _Chunk()
    cm.dtv, cm.a, cm.causal, acum, acum_t = _chunk_decays(dt_raw, dtb, alog)
    cm.acol = _per_head_cols(acum)
    cm.arow = jnp.stack([acum_t[h:h + 1, :] for h in range(SH)])
    cm.lam = jnp.exp(jnp.where(cm.causal[None], cm.acol - cm.arow, NEG))
    apl = _pair_lanes(cm.acol)
    alast = apl[:, BLK - 1:BLK, :]
    cm.dpl = _pair_lanes(_per_head_cols(cm.dtv))
    cm.eapl = jnp.exp(apl)
    cm.epl = jnp.exp(alast - apl)
    cm.cdpl = jnp.exp(alast)
    cm.dskpl = _pair_lanes(_per_head_cols(dskip))
    cm.lo = lax.broadcasted_iota(jnp.int32, (1, BLK, 128), 2) < SP
    return cm


def ssd_fwd(act, proj, dtb_p, alog_p, dsk_p):
    t = act.shape[0]
    nc = t // BLK

    def body(xs_ref, b_ref, c_ref, dt_ref, dtb_ref, al_ref, dk_ref, y_ref, sp_ref, st):
        c = pl.program_id(0)

        @pl.when(c == 0)
        def _():
            st[...] = jnp.zeros_like(st)

        s_t = st[...]
        sp_ref[0] = s_t
        cm = _chunk_common(dt_ref[...], dtb_ref[...], al_ref[...], dk_ref[...])
        gms, cbs, bts = [], [], []
        for g in range(SG):
            bf = b_ref[:, SN * g:SN * (g + 1)]
            cb = c_ref[:, SN * g:SN * (g + 1)].astype(BF16)
            gms.append(_dot_nt(cb, bf.astype(BF16)))
            cbs.append(cb)
            bts.append(bf.T.astype(BF16))
        m = (cm.lam.reshape(SG, SR, BLK, BLK) * jnp.stack(gms)[:, None]).reshape(SH, BLK, BLK).astype(BF16)
        xs16 = _pairs(xs_ref[...])
        xdt16 = xs16 * cm.dpl
        x_lo = jnp.where(cm.lo, xdt16, 0.0).astype(BF16)
        x_hi = jnp.where(cm.lo, 0.0, xdt16).astype(BF16)
        s16 = _pairs(s_t)
        s16b = s16.astype(BF16)
        yd = jnp.stack([_dot(m[2 * k], x_lo[k]) + _dot(m[2 * k + 1], x_hi[k]) for k in range(NPAIR)])
        yo = jnp.stack([_dot(cbs[k // (NPAIR // SG)], s16b[k]) for k in range(NPAIR)])
        y_ref[...] = _unpairs(yd + yo * cm.eapl + cm.dskpl * xs16)
        xe = (xdt16 * cm.epl).astype(BF16)
        st[...] = _unpairs(cm.cdpl * s16 + jnp.stack([_dot(bts[k // (NPAIR // SG)], xe[k]) for k in range(NPAIR)]))

    vec = _full((1, 128))
    return pl.pallas_call(
        body, name="ssd_fwd", grid=(nc,),
        in_specs=[pl.BlockSpec((BLK, SSM_W), lambda c: (c, 0)),
                  pl.BlockSpec((BLK, SG * SN), lambda c: (c, SSM_W // (SG * SN))),
                  pl.BlockSpec((BLK, SG * SN), lambda c: (c, SSM_W // (SG * SN) + 1)),
                  pl.BlockSpec((BLK, 128), lambda c: (c, C_DT // 128)), vec, vec, vec],
        out_specs=[pl.BlockSpec((BLK, SSM_W), lambda c: (c, 0)), pl.BlockSpec((1, SN, SSM_W), lambda c: (c, 0, 0))],
        out_shape=[jax.ShapeDtypeStruct((t, SSM_W), F32), jax.ShapeDtypeStruct((nc, SN, SSM_W), F32)],
        scratch_shapes=[pltpu.VMEM((SN, SSM_W), F32)],
        compiler_params=_cp(("arbitrary",)),
    )(act, act, act, proj, dtb_p, alog_p, dsk_p)


def _head_sums(q):
    r = q.shape[1]
    lo = lax.broadcasted_iota(jnp.int32, (1, r, 128), 2) < SP
    s_lo = jnp.sum(jnp.where(lo, q, 0.0), axis=-1, keepdims=True)
    s_hi = jnp.sum(jnp.where(lo, 0.0, q), axis=-1, keepdims=True)
    lane = lax.broadcasted_iota(jnp.int32, (r, 128), 1)
    out = jnp.zeros((r, 128), F32)
    for k in range(NPAIR):
        out = jnp.where(lane == 2 * k, s_lo[k], jnp.where(lane == 2 * k + 1, s_hi[k], out))
    return out


def ssd_bwd(act, proj, dy, sprev, dtb_p, alog_p, dsk_p):
    t = act.shape[0]
    nc = t // BLK

    def body(xs_ref, b_ref, c_ref, dt_ref, dy_ref, sp_ref, dtb_ref, al_ref, dk_ref,
             da_ref, ddt_ref, ddtb_ref, dal_ref, ddk_ref, dst):
        i = pl.program_id(0)

        @pl.when(i == 0)
        def _():
            dst[...] = jnp.zeros_like(dst)
            ddtb_ref[...] = jnp.zeros_like(ddtb_ref)
            dal_ref[...] = jnp.zeros_like(dal_ref)
            ddk_ref[...] = jnp.zeros_like(ddk_ref)

        dt_raw = dt_ref[...]
        dtb = dtb_ref[...]
        cm = _chunk_common(dt_raw, dtb, al_ref[...], dk_ref[...])
        ri = lax.broadcasted_iota(jnp.int32, (BLK, BLK), 0)
        ci = lax.broadcasted_iota(jnp.int32, (BLK, BLK), 1)
        lam_t = jnp.exp(jnp.where((ri <= ci)[None], cm.arow - cm.acol, NEG))
        bbs, cbs, cts, gms = [], [], [], []
        for g in range(SG):
            bf = b_ref[:, SN * g:SN * (g + 1)]
            cf = c_ref[:, SN * g:SN * (g + 1)]
            bbs.append(bf.astype(BF16))
            cbs.append(cf.astype(BF16))
            cts.append(cf.T.astype(BF16))
            gms.append(_dot_nt(bbs[g], cbs[g]))
        grp = lambda k: k // (NPAIR // SG)
        xs16 = _pairs(xs_ref[...])
        dy16 = _pairs(dy_ref[...])
        sp16 = _pairs(sp_ref[0])
        ds16 = _pairs(dst[...])
        xdt16 = xs16 * cm.dpl
        xdtb = xdt16.astype(BF16)
        dyh = [jnp.where(cm.lo, dy16, 0.0).astype(BF16), jnp.where(cm.lo, 0.0, dy16).astype(BF16)]
        m_t = (lam_t.reshape(SG, SR, BLK, BLK) * jnp.stack(gms)[:, None]).reshape(SH, BLK, BLK).astype(BF16)
        dxdt = jnp.stack([_dot(m_t[2 * k], dyh[0][k]) + _dot(m_t[2 * k + 1], dyh[1][k]) for k in range(NPAIR)])
        dm = jnp.stack([_dot_nt(dyh[h % 2][h // 2], xdtb[h // 2]) for h in range(SH)])
        dgl = (dm * cm.lam).reshape(SG, SR, BLK, BLK)
        dg = jnp.sum(dgl, axis=1).astype(BF16)
        w = (dgl * jnp.stack([_dot_nt(cbs[g], bbs[g]) for g in range(SG)])[:, None]).reshape(SH, BLK, BLK)
        w_rows = jnp.sum(w, axis=2, keepdims=True)
        w_cols = jnp.concatenate([jnp.sum(w, axis=1)] + [jnp.zeros((128 - SH, BLK), F32)], axis=0).T
        lane_c = lax.broadcasted_iota(jnp.int32, (BLK, 128), 1)
        da_cols = -w_cols
        for h in range(SH):
            da_cols = jnp.where(lane_c == h, da_cols + w_rows[h], da_cols)
        ds16b = ds16.astype(BF16)
        sp16b = sp16.astype(BF16)
        dxs = jnp.stack([_dot(bbs[grp(k)], ds16b[k]) for k in range(NPAIR)]) * cm.epl
        dxdt = dxdt + dxs
        dya = (dy16 * cm.eapl).astype(BF16)
        xe = (xdt16 * cm.epl).astype(BF16)
        dcs, dbs = [], []
        for g in range(SG):
            ks = range(g * (NPAIR // SG), (g + 1) * (NPAIR // SG))
            dcs.append(sum(_dot_nt(dya[k], sp16b[k]) for k in ks) + _dot(dg[g], bbs[g]))
            dbs.append(sum(_dot_nt(xe[k], ds16b[k]) for k in ks) + _dot_tn(dg[g], cbs[g]))
        dst[...] = _unpairs(cm.cdpl * ds16 + jnp.stack([_dot(cts[grp(k)], dya[k]) for k in range(NPAIR)]))
        da_ref[...] = jnp.concatenate([_unpairs(dxdt * cm.dpl + cm.dskpl * dy16)] + dbs + dcs, axis=1)
        y_off = jnp.stack([_dot(cbs[grp(k)], sp16b[k]) for k in range(NPAIR)]) * cm.eapl
        da_cols = da_cols + _head_sums(dy16 * y_off - xdt16 * dxs)
        last = _head_sums(jnp.sum(xdt16 * dxs, axis=1, keepdims=True)
                          + cm.cdpl * jnp.sum(ds16 * sp16, axis=1, keepdims=True))
        ddt = _head_sums(dxdt * xs16)
        row_i = lax.broadcasted_iota(jnp.int32, (BLK, 128), 0)
        dacum = da_cols + jnp.where(row_i == BLK - 1, last, 0.0)
        dda = _tri_mm((ri <= ci).astype(BF16), dacum)
        ddt = ddt + dda * cm.a
        dal_ref[...] += _csum(dda * cm.dtv) * cm.a
        ddt_raw = jnp.where(lane_c < SH, ddt * _sig(dt_raw + dtb), 0.0)
        ddt_ref[...] = ddt_raw.astype(BF16)
        ddtb_ref[...] += _csum(ddt_raw)
        ddk_ref[...] += _head_sums(jnp.sum(dy16 * xs16, axis=1, keepdims=True))

    rev = lambda i: nc - 1 - i
    vec = _full((1, 128))
    slab = pl.BlockSpec((BLK, SSM_W), lambda i: (rev(i), 0))
    return pl.pallas_call(
        body, name="ssd_bwd", grid=(nc,),
        in_specs=[slab,
                  pl.BlockSpec((BLK, SG * SN), lambda i: (rev(i), SSM_W // (SG * SN))),
                  pl.BlockSpec((BLK, SG * SN), lambda i: (rev(i), SSM_W // (SG * SN) + 1)),
                  pl.BlockSpec((BLK, 128), lambda i: (rev(i), C_DT // 128)),
                  slab,
                  pl.BlockSpec((1, SN, SSM_W), lambda i: (rev(i), 0, 0)), vec, vec, vec],
        out_specs=[pl.BlockSpec((BLK, XBC), lambda i: (rev(i), 0)), pl.BlockSpec((BLK, 128), lambda i: (rev(i), 0)),
                   vec, vec, vec],
        out_shape=[jax.ShapeDtypeStruct((t, XBC), F32), jax.ShapeDtypeStruct((t, 128), BF16),
                   jax.ShapeDtypeStruct((1, 128), F32), jax.ShapeDtypeStruct((1, 128), F32),
                   jax.ShapeDtypeStruct((1, 128), F32)],
        scratch_shapes=[pltpu.VMEM((SN, SSM_W), F32)],
        compiler_params=_cp(("arbitrary",)),
    )(act, act, act, proj, dy, sprev, dtb_p, alog_p, dsk_p)


TAIL_TM = 128


def _dsilu(z, s):
    return s * (1.0 + z * (1.0 - s))


def tail(proj, ao, yss, x, target, gate, ssm_nw, w_at, w_ss, w_ou):
    t = x.shape[0]
    tm = min(t, TAIL_TM)
    gw = SSM_W // SG

    def body(ao_ref, za_ref, ga_ref, gb_ref, zm_ref, ys_ref, x_ref, tg_ref, gt_ref, nw_ref, wa_ref, ws_ref, wo_ref,
             loss_ref, dy_ref, dao_ref, dza_ref, dga_ref, dgb_ref, dys_ref, dzm_ref,
             ua_ref, yn_ref, mg_ref, dya_ref, dyb_ref, do_ref, dgt_ref, dnw_ref):
        i = pl.program_id(0)

        @pl.when(i == 0)
        def _():
            loss_ref[...] = jnp.zeros_like(loss_ref)
            dgt_ref[...] = jnp.zeros_like(dgt_ref)
            dnw_ref[...] = jnp.zeros_like(dnw_ref)

        ao = ao_ref[...]
        za = za_ref[...]
        sa = _sig(za)
        sila = za * sa
        ua = (ao * sila).astype(BF16)
        ya = _dot(ua, wa_ref[...])
        zm = zm_ref[...]
        sm = _sig(zm)
        silm = zm * sm
        ys = ys_ref[...]
        u = ys * silm
        nw = nw_ref[...]
        rs, uns = [], []
        for g in range(SG):
            ug = u[:, gw * g:gw * (g + 1)]
            r = lax.rsqrt(jnp.mean(ug * ug, axis=-1, keepdims=True) + EPS)
            rs.append(r)
            uns.append(ug * r)
        un = jnp.concatenate(uns, axis=1)
        yn = (un * nw).astype(BF16)
        yb = _dot(yn, ws_ref[...])
        sga = _sig(ga_ref[...])
        sgb = _sig(gb_ref[...])
        mg = (sga * ya + sgb * yb).astype(BF16)
        o = _dot(mg, wo_ref[...])
        gt = gt_ref[...]
        err = (x_ref[...] + gt * o) - tg_ref[...]
        lane = lax.broadcasted_iota(jnp.int32, (1, 128), 1)
        loss_ref[...] += jnp.where(lane == 0, 0.5 * _asum(_rsum(err * err) / D), 0.0)
        dy = err * (1.0 / D)
        dy_ref[...] = dy
        dgt_ref[...] += _csum(dy * o)
        do = (dy * gt).astype(BF16)
        dmg = _dot_nt(do, wo_ref[...])
        dga_ref[...] = (dmg * ya * sga * (1.0 - sga)).astype(BF16)
        dgb_ref[...] = (dmg * yb * sgb * (1.0 - sgb)).astype(BF16)
        dya = (dmg * sga).astype(BF16)
        dyb = (dmg * sgb).astype(BF16)
        dua = _dot_nt(dya, wa_ref[...])
        dao_ref[...] = dua * sila
        dza_ref[...] = (dua * ao * _dsilu(za, sa)).astype(BF16)
        dyn = _dot_nt(dyb, ws_ref[...])
        dnw_ref[...] += _csum(dyn * un)
        dun = dyn * nw
        dus = []
        for g in range(SG):
            gs = slice(gw * g, gw * (g + 1))
            dus.append(rs[g] * (dun[:, gs] - uns[g] * jnp.mean(dun[:, gs] * uns[g], axis=-1, keepdims=True)))
        du = jnp.concatenate(dus, axis=1)
        dys_ref[...] = du * silm
        dzm_ref[...] = (du * ys * _dsilu(zm, sm)).astype(BF16)
        ua_ref[...] = ua
        yn_ref[...] = yn
        mg_ref[...] = mg
        dya_ref[...] = dya
        dyb_ref[...] = dyb
        do_ref[...] = do

    row = lambda w: pl.BlockSpec((tm, w), lambda i: (i, 0))
    pcol = lambda w, c0: pl.BlockSpec((tm, w), lambda i: (i, c0 // w))
    sd = lambda w, dt: jax.ShapeDtypeStruct((t, w), dt)
    return pl.pallas_call(
        body, name="tail", grid=(t // tm,),
        in_specs=[row(D), pcol(D, C_ZA), pcol(D, C_GA), pcol(D, C_GB), pcol(SSM_W, C_ZM), row(SSM_W), row(D), row(D),
                  _full((1, D)), _full((1, SSM_W)), _full((D, D)), _full((SSM_W, D)), _full((D, D))],
        out_specs=[_full((1, 128)), row(D), row(D), row(D), row(D), row(D), row(SSM_W), row(SSM_W),
                   row(D), row(SSM_W), row(D), row(D), row(D), row(D), _full((1, D)), _full((1, SSM_W))],
        out_shape=[jax.ShapeDtypeStruct((1, 128), F32), sd(D, F32), sd(D, F32), sd(D, BF16), sd(D, BF16), sd(D, BF16),
                   sd(SSM_W, F32), sd(SSM_W, BF16), sd(D, BF16), sd(SSM_W, BF16), sd(D, BF16), sd(D, BF16),
                   sd(D, BF16), sd(D, BF16), jax.ShapeDtypeStruct((1, D), F32), jax.ShapeDtypeStruct((1, SSM_W), F32)],
        compiler_params=_cp(("arbitrary",)),
    )(ao, proj, proj, proj, proj, yss, x, target, gate, ssm_nw, w_at, w_ss, w_ou)


def dproj_bwd(dproj, wcat, x, dy, norm_w, scale):
    t = x.shape[0]
    tm = min(t, 512)
    nk = NP // TN
    nt = t // tm

    def body(dp_ref, w_ref, x_ref, dy_ref, nw_ref, sc_ref, gx_ref, dnw_ref, dsc_ref, dsh_ref, acc, dwe_ref):
        i = pl.program_id(0)
        k = pl.program_id(1)

        @pl.when(jnp.logical_and(i == 0, k == 0))
        def _():
            dwe_ref[...] = jnp.zeros_like(dwe_ref)
            dsh_ref[...] = jnp.zeros_like(dsh_ref)
            dnw_ref[...] = jnp.zeros_like(dnw_ref)
            dsc_ref[...] = jnp.zeros_like(dsc_ref)

        part = _dot_nt(dp_ref[...], w_ref[...])

        @pl.when(k == 0)
        def _():
            acc[...] = part

        @pl.when(k > 0)
        def _():
            acc[...] += part

        @pl.when(k == nk - 1)
        def _():
            dh = acc[...]
            xv = x_ref[...]
            r = lax.rsqrt(jnp.mean(xv * xv, axis=-1, keepdims=True) + EPS)
            xn = xv * r
            weff = nw_ref[...] * (1.0 + sc_ref[...])
            dxn = dh * weff
            gx_ref[...] = dy_ref[...] + r * (dxn - xn * jnp.mean(dxn * xn, axis=-1, keepdims=True))
            dwe_ref[...] += _csum(dh * xn)
            dsh_ref[...] += _csum(dh)

        @pl.when(jnp.logical_and(i == nt - 1, k == nk - 1))
        def _():
            dwe = dwe_ref[...]
            dnw_ref[...] = dwe * (1.0 + sc_ref[...])
            dsc_ref[...] = dwe * nw_ref[...]

    vec = pl.BlockSpec((1, D), lambda i, k: (0, 0))
    row = pl.BlockSpec((tm, D), lambda i, k: (i, 0))
    return pl.pallas_call(
        body, name="dproj_bwd", grid=(nt, nk),
        in_specs=[pl.BlockSpec((tm, TN), lambda i, k: (i, k)), pl.BlockSpec((D, TN), lambda i, k: (0, k)),
                  row, row, vec, vec],
        out_specs=[row, vec, vec, vec],
        out_shape=[jax.ShapeDtypeStruct((t, D), F32), jax.ShapeDtypeStruct((1, D), F32),
                   jax.ShapeDtypeStruct((1, D), F32), jax.ShapeDtypeStruct((1, D), F32)],
        scratch_shapes=[pltpu.VMEM((tm, D), F32), pltpu.VMEM((1, D), F32)],
        compiler_params=_cp(("arbitrary", "arbitrary")),
    )(dproj, wcat, x, dy, norm_w, scale)


def xty(a, b, name, bm=512, bn=768):
    t, m = a.shape
    n = b.shape[1]
    tk = min(t, 512)
    bm = min(bm, m)
    bn = min(bn, n)
    nk = t // tk

    def body(a_ref, b_ref, o_ref):
        part = _dot_tn(a_ref[...], b_ref[...])

        @pl.when(pl.program_id(2) == 0)
        def _():
            o_ref[...] = part

        @pl.when(pl.program_id(2) > 0)
        def _():
            o_ref[...] += part

    return pl.pallas_call(
        body, name=name, grid=(m // bm, n // bn, nk),
        in_specs=[pl.BlockSpec((tk, bm), lambda i, j, k: (k, i)), pl.BlockSpec((tk, bn), lambda i, j, k: (k, j))],
        out_specs=pl.BlockSpec((bm, bn), lambda i, j, k: (i, j)),
        out_shape=jax.ShapeDtypeStruct((m, n), F32),
        compiler_params=_cp(("parallel", "parallel", "arbitrary")),
    )(a, b)


SUM_TR = 256


def pair_sum(mine, theirs):
    r = mine.shape[0]

    def body(a_ref, b_ref, o_ref, ob_ref):
        s = a_ref[...] + b_ref[...]
        o_ref[...] = s
        ob_ref[...] = s.astype(BF16)

    spec = pl.BlockSpec((SUM_TR, 1024), lambda i: (i, 0))
    return pl.pallas_call(body, name="pair_sum", grid=(r // SUM_TR,), in_specs=[spec, spec], out_specs=[spec, spec],
                          out_shape=[jax.ShapeDtypeStruct((r, 1024), F32), jax.ShapeDtypeStruct((r, 1024), BF16)],
                          compiler_params=_cp(("parallel",)))(mine, theirs)


def chip_sum(own, others):
    r = own.shape[0]

    def body(a_ref, b_ref, o_ref):
        acc = a_ref[...]
        for k in range(3):
            acc = acc + b_ref[k].astype(F32)
        o_ref[...] = acc

    spec = pl.BlockSpec((SUM_TR, 1024), lambda i: (i, 0))
    return pl.pallas_call(body, name="chip_sum", grid=(r // SUM_TR,),
                          in_specs=[spec, pl.BlockSpec((3, SUM_TR, 1024), lambda i: (0, i, 0))], out_specs=spec,
                          out_shape=jax.ShapeDtypeStruct((r, 1024), F32), compiler_params=_cp(("parallel",)))(own, others)


def sum_devices(g):
    r = g.shape[1]

    def body(g_ref, o_ref):
        acc = g_ref[0]
        for d in range(1, 8):
            acc = acc + g_ref[d]
        o_ref[...] = acc

    return pl.pallas_call(body, name="sum_devices", out_shape=jax.ShapeDtypeStruct((r, 1024), F32),
                          compiler_params=_cp())(g)


def adamw(w, g, m, v, name):
    r, c = w.shape
    tr = r
    for cand in (256, 128, 64, 32, 16, 8):
        if r % cand == 0 and r > cand:
            tr = cand
            break

    def body(w_ref, g_ref, m_ref, v_ref, d_ref, nm_ref, nv_ref):
        gv = g_ref[...]
        mn = ADAM_B1 * m_ref[...] + (1.0 - ADAM_B1) * gv
        vn = ADAM_B2 * v_ref[...] + (1.0 - ADAM_B2) * (gv * gv)
        m_hat = mn / (1.0 - ADAM_B1 ** ADAM_STEP)
        v_hat = vn / (1.0 - ADAM_B2 ** ADAM_STEP)
        d_ref[...] = -ADAM_LR * (m_hat / (jnp.sqrt(v_hat) + ADAM_EPS) + ADAM_WD * w_ref[...])
        nm_ref[...] = mn
        nv_ref[...] = vn

    spec = pl.BlockSpec((tr, c), lambda i: (i, 0))
    sd = jax.ShapeDtypeStruct((r, c), F32)
    return pl.pallas_call(body, name=name, grid=(r // tr,), in_specs=[spec] * 4, out_specs=[spec] * 3,
                          out_shape=[sd, sd, sd], compiler_params=_cp(("parallel",)))(w, g, m, v)


ANY = pl.BlockSpec(memory_space=pl.ANY)
VM = pl.BlockSpec(memory_space=pltpu.VMEM)
OTHER_CHIPS = ((1, 0), (0, 1), (1, 1))


def _pos():
    return lax.axis_index("x"), lax.axis_index("y"), lax.axis_index("c")


def _flip(v, bit):
    return 1 - v if bit else v


def _rcopy(src, dst, ssem, rsem, peer):
    return pltpu.make_async_remote_copy(src_ref=src, dst_ref=dst, send_sem=ssem, recv_sem=rsem,
                                        device_id=peer, device_id_type=MESH)


def allgather_small(p, name):
    r = p.shape[0]

    def body(in_ref, out_ref, ssem, rsem, lsem):
        x, y, c = _pos()
        me = 4 * x + 2 * y + c
        loc = pltpu.make_async_copy(in_ref, out_ref.at[me], lsem)
        loc.start()
        sends = []
        peers = []
        for k in range(1, 8):
            px, py, pc = _flip(x, (k >> 2) & 1), _flip(y, (k >> 1) & 1), _flip(c, k & 1)
            peers.append((px, py, pc))
            cp = _rcopy(in_ref, out_ref.at[me], ssem.at[k - 1], rsem.at[k - 1], (px, py, pc))
            cp.start()
            sends.append(cp)
        for k in range(1, 8):
            px, py, pc = peers[k - 1]
            _rcopy(in_ref, out_ref.at[4 * px + 2 * py + pc], ssem.at[k - 1], rsem.at[k - 1], (px, py, pc)).wait_recv()
        for cp in sends:
            cp.wait_send()
        loc.wait()

    return pl.pallas_call(
        body, name=name, out_shape=jax.ShapeDtypeStruct((8, r, 1024), F32),
        in_specs=[VM], out_specs=VM,
        scratch_shapes=[pltpu.SemaphoreType.DMA((7,)), pltpu.SemaphoreType.DMA((7,)), pltpu.SemaphoreType.DMA],
    )(p)


def gather_weights(wpack, mod_sh):
    def body(w_ref, m_ref, wg_ref, mo_ref, ssem, rsem, lsem):
        x, y, c = _pos()
        chip = 2 * x + y
        mine = pl.ds(pl.multiple_of(c * HALF, 16), HALF)
        other = pl.ds(pl.multiple_of((1 - c) * HALF, 16), HALF)
        sib = (x, y, 1 - c)
        loc_w = pltpu.make_async_copy(w_ref, wg_ref.at[chip], lsem.at[0])
        loc_m = pltpu.make_async_copy(m_ref, mo_ref.at[chip], lsem.at[1])
        loc_w.start()
        loc_m.start()
        sends = []
        for k, (fx, fy) in enumerate(OTHER_CHIPS):
            peer = (_flip(x, fx), _flip(y, fy), c)
            cw = _rcopy(w_ref.at[mine], wg_ref.at[chip, mine], ssem.at[k], rsem.at[k], peer)
            cm = _rcopy(m_ref, mo_ref.at[chip], ssem.at[6 + k], rsem.at[6 + k], peer)
            cw.start()
            cm.start()
            sends += [cw, cm]
        for k, (fx, fy) in enumerate(OTHER_CHIPS):
            px, py = _flip(x, fx), _flip(y, fy)
            got = wg_ref.at[2 * px + py, mine]
            _rcopy(w_ref.at[mine], got, ssem.at[k], rsem.at[k], (px, py, c)).wait_recv()
            fw = _rcopy(got, got, ssem.at[3 + k], rsem.at[3 + k], sib)
            fw.start()
            sends.append(fw)
        for k, (fx, fy) in enumerate(OTHER_CHIPS):
            px, py = _flip(x, fx), _flip(y, fy)
            land = wg_ref.at[2 * px + py, other]
            _rcopy(land, land, ssem.at[3 + k], rsem.at[3 + k], sib).wait_recv()
            _rcopy(m_ref, mo_ref.at[2 * px + py], ssem.at[6 + k], rsem.at[6 + k], (px, py, c)).wait_recv()
        for cp in sends:
            cp.wait_send()
        loc_w.wait()
        loc_m.wait()

    return pl.pallas_call(
        body, name="gather_weights",
        out_shape=[jax.ShapeDtypeStruct((4, ROWS, 1024), BF16), jax.ShapeDtypeStruct((4, 8, 768), F32)],
        in_specs=[ANY, VM], out_specs=[ANY, VM],
        scratch_shapes=[pltpu.SemaphoreType.DMA((9,)), pltpu.SemaphoreType.DMA((9,)), pltpu.SemaphoreType.DMA((2,))],
    )(wpack, mod_sh)


def pair_exchange(gpack):
    def body(g_ref, r_ref, ssem, rsem):
        x, y, c = _pos()
        other = pl.ds(pl.multiple_of((1 - c) * HALF, 8), HALF)
        cp = _rcopy(g_ref.at[:, other, :], r_ref, ssem, rsem, (x, y, 1 - c))
        cp.start()
        cp.wait()

    return pl.pallas_call(
        body, name="pair_exchange", out_shape=jax.ShapeDtypeStruct((4, HALF, 1024), F32),
        in_specs=[ANY], out_specs=ANY,
        scratch_shapes=[pltpu.SemaphoreType.DMA, pltpu.SemaphoreType.DMA],
    )(gpack)


def chip_exchange(part, part_b):
    def body(p_ref, pb_ref, own_ref, r_ref, ssem, rsem, lsem):
        x, y, c = _pos()
        chip = 2 * x + y
        loc = pltpu.make_async_copy(p_ref.at[chip], own_ref, lsem)
        loc.start()
        sends = []
        for k, (fx, fy) in enumerate(OTHER_CHIPS):
            px, py = _flip(x, fx), _flip(y, fy)
            cp = _rcopy(pb_ref.at[2 * px + py], r_ref.at[k], ssem.at[k], rsem.at[k], (px, py, c))
            cp.start()
            sends.append(cp)
        for k, (fx, fy) in enumerate(OTHER_CHIPS):
            px, py = _flip(x, fx), _flip(y, fy)
            _rcopy(pb_ref.at[chip], r_ref.at[k], ssem.at[k], rsem.at[k], (px, py, c)).wait_recv()
        for cp in sends:
            cp.wait_send()
        loc.wait()

    return pl.pallas_call(
        body, name="chip_exchange",
        out_shape=[jax.ShapeDtypeStruct((HALF, 1024), F32), jax.ShapeDtypeStruct((3, HALF, 1024), BF16)],
        in_specs=[ANY, ANY], out_specs=[ANY, ANY],
        scratch_shapes=[pltpu.SemaphoreType.DMA((3,)), pltpu.SemaphoreType.DMA((3,)), pltpu.SemaphoreType.DMA],
    )(part, part_b)


def pair_allgather(red):
    def body(r_ref, o_ref, ssem, rsem, lsem):
        x, y, c = _pos()
        mine = pl.ds(pl.multiple_of(c * HALF, 8), HALF)
        other = pl.ds(pl.multiple_of((1 - c) * HALF, 8), HALF)
        loc = pltpu.make_async_copy(r_ref, o_ref.at[mine], lsem)
        loc.start()
        cp = _rcopy(r_ref, o_ref.at[mine], ssem, rsem, (x, y, 1 - c))
        cp.start()
        _rcopy(r_ref, o_ref.at[other], ssem, rsem, (x, y, 1 - c)).wait_recv()
        cp.wait_send()
        loc.wait()

    return pl.pallas_call(
        body, name="pair_allgather", out_shape=jax.ShapeDtypeStruct((ROWS, 1024), F32),
        in_specs=[ANY], out_specs=ANY,
        scratch_shapes=[pltpu.SemaphoreType.DMA, pltpu.SemaphoreType.DMA, pltpu.SemaphoreType.DMA],
    )(red)


def _row(v, width=1024):
    v = v.reshape(-1)
    n = -(-v.shape[0] // width) * width
    return jnp.pad(v, (0, n - v.shape[0])).reshape(-1, width)


def _slots(vs):
    row = [jnp.pad(v.reshape(-1), (0, 128 - v.size)) for v in vs]
    row += [jnp.zeros((128,), F32)] * (8 - len(row))
    return jnp.concatenate(row).reshape(1, 1024)


def _pack_small(b_ada, norm_w, conv_b, ssm_norm_w, q_norm_w, k_norm_w, sinks, dt_bias, a_log, d_skip, rel_bias,
                extra=None):
    misc = [q_norm_w, k_norm_w, sinks, dt_bias, a_log, d_skip] + ([] if extra is None else [extra])
    rows = [_row(b_ada), _row(norm_w), _row(conv_b), _row(ssm_norm_w), _slots(misc), _row(rel_bias)]
    rows.append(jnp.zeros((5, 1024), F32))
    return jnp.concatenate(rows, axis=0)


def _unpack_small(p):
    misc = p[9]
    return dict(b_ada=p[0:3].reshape(1, 3072), norm_w=p[3:4], conv_b=p[4:7].reshape(1, 3072),
                ssm_norm_w=p[7:9].reshape(1, 2048), q_norm_w=misc[None, 0:64], k_norm_w=misc[None, 128:192],
                sinks=misc[None, 256:272], dt_bias=misc[None, 384:416], a_log=misc[None, 512:544],
                d_skip=misc[None, 640:672], rel_bias=p[10, :512].reshape(32, 16), extra=misc[768])


SMALL = ("b_ada", "norm_w", "conv_b", "ssm_norm_w", "q_norm_w", "k_norm_w", "sinks", "dt_bias", "a_log", "d_skip",
         "rel_bias")
WEIGHTS = ("w_ada", "b_ada", "norm_w", "w_in", "q_norm_w", "k_norm_w", "rel_bias", "sinks", "conv_w", "conv_b",
           "dt_bias", "a_log", "d_skip", "ssm_norm_w", "w_attn_proj", "w_ssm_proj", "w_out")
IN_COLS = ((0, 1024, C_Q), (1024, 256, C_K), (1280, 256, C_V), (1536, 1024, C_ZA), (2560, 2048, C_ZM),
           (4608, 3072, C_XBC), (7680, 32, C_DT), (7712, 1024, C_GA), (8736, 1024, C_GB))


def _to_cat(w_full):
    by_new = sorted(IN_COLS, key=lambda e: e[2])
    parts, pos = [], 0
    for o, n, cnew in by_new:
        assert cnew == pos
        parts.append(w_full[:, o:o + n])
        pos += n
    parts.append(jnp.zeros((w_full.shape[0], NP - pos), w_full.dtype))
    return jnp.concatenate(parts, axis=1)


def _from_cat(w_cat):
    return jnp.concatenate([w_cat[:, cnew:cnew + n] for o, n, cnew in IN_COLS], axis=1)


def kernel(x, c, w_ada, b_ada, norm_w, w_in, q_norm_w, k_norm_w, rel_bias, sinks, conv_w, conv_b, dt_bias, a_log, d_skip, ssm_norm_w, w_attn_proj, w_ssm_proj, w_out, loss_target, m_w_ada, m_b_ada, m_norm_w, m_w_in, m_q_norm_w, m_k_norm_w, m_rel_bias, m_sinks, m_conv_w, m_conv_b, m_dt_bias, m_a_log, m_d_skip, m_ssm_norm_w, m_w_attn_proj, m_w_ssm_proj, m_w_out, v_w_ada, v_b_ada, v_norm_w, v_w_in, v_q_norm_w, v_k_norm_w, v_rel_bias, v_sinks, v_conv_w, v_conv_b, v_dt_bias, v_a_log, v_d_skip, v_ssm_norm_w, v_w_attn_proj, v_w_ssm_proj, v_w_out):
    args = dict(locals())
    xi, yi, ci = lax.axis_index("x"), lax.axis_index("y"), lax.axis_index("c")
    chip = 2 * xi + yi
    me = 4 * xi + 2 * yi + ci
    x2 = x[0]
    tgt = loss_target[0]

    pay = jnp.concatenate([c, conv_w[0].reshape(3, 1024), jnp.zeros((4, 1024), F32)], axis=0)
    g0 = allgather_small(pay, "gather_cond")
    c_all = g0[:, 0, :]
    conv_w_full = g0[0::2, 1:4, :].reshape(4, CONV_K, 768).transpose(1, 0, 2).reshape(CONV_K, XBC)

    b_ada_sh = lax.dynamic_slice(b_ada, (0, chip * 768), (1, 768))
    mod_sh = ada_mod(c_all, w_ada[0], b_ada_sh)

    wpack = jnp.concatenate([w_in[0].astype(BF16).reshape(R_IN, 1024), w_attn_proj[0].astype(BF16),
                             w_ssm_proj[0].astype(BF16), w_out[0].astype(BF16),
                             jnp.zeros((ROWS - R_IN - R_AT - R_SS - R_OU, 1024), BF16)], axis=0)
    wg, modg = gather_weights(wpack, mod_sh)
    mod = lax.dynamic_slice(modg, (0, me, 0), (4, 1, 768)).reshape(1, 3 * D)
    shift, scale, gate = mod[:, :D], mod[:, D:2 * D], mod[:, 2 * D:]
    w_in_full = wg[:, :R_IN].reshape(4, D, IN_W // 4).transpose(1, 0, 2).reshape(D, IN_W)
    wcat = _to_cat(w_in_full)
    o1 = R_IN
    w_at = wg[:, o1:o1 + R_AT].reshape(D, D)
    w_ss = wg[:, o1 + R_AT:o1 + R_AT + R_SS].reshape(SSM_W, D)
    w_ou = wg[:, o1 + R_AT + R_SS:o1 + R_AT + R_SS + R_OU].reshape(D, D)

    pad128 = lambda v: jnp.pad(v, ((0, 0), (0, 128 - v.shape[1])))
    dtb_p, alog_p, dsk_p = pad128(dt_bias), pad128(a_log), pad128(d_skip)
    bucket = _bucket_table()

    proj, h = norm_proj(x2, norm_w, scale, shift, wcat)
    biasm = bias_expand(rel_bias, sinks, bucket)
    ao = attn_fwd(proj, biasm, q_norm_w, k_norm_w)
    act = conv_fwd(proj, conv_w_full, conv_b)
    yss, sprev = ssd_fwd(act, proj, dtb_p, alog_p, dsk_p)

    (loss_p, dy, dao, dza, dga, dgb, dyss, dzm, ua, yn, mg, dya, dyb, dout, dgate, dssm_nw) = tail(
        proj, ao, yss, x2, tgt, gate, ssm_norm_w, w_at, w_ss, w_ou)

    dq, dk, dv, dqw, dkw, dacc = attn_bwd(proj, dao, biasm, q_norm_w, k_norm_w)
    dbias = bias_reduce(dacc, bucket)
    drb = dbias[:, :NBUCKET].T
    dsk = dbias[:, NBUCKET].reshape(1, HQ)
    dact, ddt, ddtb, dalog, ddskip = ssd_bwd(act, proj, dyss, sprev, dtb_p, alog_p, dsk_p)
    dxbc, dconv_w, dconv_b = conv_bwd(proj, dact, conv_w_full, conv_b)

    t = x2.shape[0]
    dproj = jnp.concatenate([dq, dza, dga, dgb, dzm, dxbc, dk, dv, ddt, jnp.zeros((t, NP - C_DT - 128), BF16)], axis=1)
    grad_x, dnorm_w, dscale, dshift = dproj_bwd(dproj, wcat, x2, dy, norm_w, scale)
    dwcat = xty(h, dproj, "dw_in", bm=512, bn=TN)
    dw_at = xty(ua, dya, "dw_attn", bm=512, bn=512)
    dw_ss = xty(yn, dyb, "dw_ssm", bm=512, bn=512)
    dw_ou = xty(mg, dout, "dw_out", bm=512, bn=512)

    g_in = _from_cat(dwcat).reshape(D, 4, IN_W // 4).transpose(1, 0, 2).reshape(4, R_IN, 1024)
    gpack = jnp.concatenate([g_in, dw_at.reshape(4, R_AT, 1024), dw_ss.reshape(4, R_SS, 1024),
                             dw_ou.reshape(4, R_OU, 1024),
                             jnp.zeros((4, ROWS - R_IN - R_AT - R_SS - R_OU, 1024), F32)], axis=1)
    from_sib = pair_exchange(gpack)
    my_half = lax.dynamic_slice_in_dim(gpack, ci * HALF, HALF, axis=1)
    part, part_b = pair_sum(my_half.reshape(4 * HALF, 1024), from_sib.reshape(4 * HALF, 1024))
    own, others = chip_exchange(part.reshape(4, HALF, 1024), part_b.reshape(4, HALF, 1024))
    red = chip_sum(own, others)
    g_shard = pair_allgather(red)

    dmod = jnp.concatenate([dshift, dscale, dgate], axis=1)
    gsmall = jnp.concatenate([
        _pack_small(dmod, dnorm_w, dconv_b, dssm_nw, dqw, dkw, dsk[:, :HQ], ddtb[:, :SH], dalog[:, :SH],
                    ddskip[:, :SH], drb, extra=loss_p[:, :1]),
        dconv_w.reshape(12, 1024), jnp.zeros((4, 1024), F32)], axis=0)
    gall = allgather_small(gsmall, "gather_small_grads")
    ssum = sum_devices(gall)
    gs = _unpack_small(ssum[:16])
    loss = gs["extra"]
    dconv_w_sh = lax.dynamic_slice(ssum[16:28].reshape(CONV_K, XBC), (0, chip * 768), (CONV_K, 768))
    dmod_all = gall[:, 0:3, :].reshape(8, 3 * D)
    dw_ada = ada_grad(c_all, lax.dynamic_slice(dmod_all, (0, chip * 768), (8, 768)))

    grads = dict(gs)
    grads["w_ada"] = dw_ada
    grads["w_in"] = g_shard[:R_IN].reshape(D, IN_W // 4)
    grads["w_attn_proj"] = g_shard[o1:o1 + R_AT]
    grads["w_ssm_proj"] = g_shard[o1 + R_AT:o1 + R_AT + R_SS]
    grads["w_out"] = g_shard[o1 + R_AT + R_SS:o1 + R_AT + R_SS + R_OU]
    grads["conv_w"] = dconv_w_sh

    delta, new_m, new_v = {}, {}, {}
    for n in ("w_ada", "w_in", "conv_w", "w_attn_proj", "w_ssm_proj", "w_out"):
        delta[n], new_m[n], new_v[n] = adamw(args[n][0], grads[n], args["m_" + n][0], args["v_" + n][0], "adamw_" + n)
    ws = _pack_small(*[args[n] for n in SMALL])
    ms = _pack_small(*[args["m_" + n] for n in SMALL])
    vs = _pack_small(*[args["v_" + n] for n in SMALL])
    d_s, m_s, v_s = adamw(ws, ssum[:16], ms, vs, "adamw_small")
    d_s, m_s, v_s = _unpack_small(d_s), _unpack_small(m_s), _unpack_small(v_s)
    for n in SMALL:
        delta[n], new_m[n], new_v[n] = d_s[n], m_s[n], v_s[n]

    def shaped(n, a):
        return a.reshape(args[n].shape)

    outs = [loss, grad_x[None]]
    for table in (grads, delta, new_m, new_v):
        outs += [shaped(n, table[n]) for n in WEIGHTS]
    return tuple(outs)
```

```python
import functools
import math

import numpy as np
import jax
import jax.numpy as jnp
from jax import lax
from jax.experimental import pallas as pl
from jax.experimental.pallas import tpu as pltpu

F32 = jnp.float32
BF16 = jnp.bfloat16
MESH = pl.DeviceIdType.MESH

D = 1024
HQ, HKV, GRP, DH = 16, 4, 4, 64
BLK = 128
NBUCKET, MAXDIST = 32, 128
SSM_W, SH, SG, SR, SP, SN = 2048, 32, 4, 8, 64, 128
CONV_K = 4
XBC = SSM_W + 2 * SG * SN
IN_W = 9760
EPS = 1e-6
NEG = -1e30
SCALE = DH ** -0.5

C_Q, C_ZA, C_GA, C_GB, C_ZM, C_XBC, C_K, C_V, C_DT = 0, 1024, 2048, 3072, 4096, 6144, 9216, 9472, 9728
NP = 9984
TN = 768

R_IN, R_AT, R_SS, R_OU = 2440, 256, 512, 256
ROWS = 3584
HALF = ROWS // 2

ADAM_LR, ADAM_B1, ADAM_B2, ADAM_EPS, ADAM_WD, ADAM_STEP = 0.001, 0.9, 0.999, 1e-08, 0.01, 10

VMEM_LIMIT = 56 * 1024 * 1024


def _cp(sem=None):
    if sem is None:
        return pltpu.CompilerParams(vmem_limit_bytes=VMEM_LIMIT)
    return pltpu.CompilerParams(dimension_semantics=sem, vmem_limit_bytes=VMEM_LIMIT)


def _sig(x):
    return 1.0 / (1.0 + jnp.exp(-x))


def _dot(a, b):
    return jnp.dot(a, b, preferred_element_type=F32)


def _dot_nt(a, b):
    return lax.dot_general(a, b, (((1,), (1,)), ((), ())), preferred_element_type=F32)


def _dot_tn(a, b):
    return lax.dot_general(a, b, (((0,), (0,)), ((), ())), preferred_element_type=F32)


def _rsum(x):
    return jnp.sum(x, axis=-1, keepdims=True)


def _csum(x):
    return jnp.sum(x, axis=0, keepdims=True)


def _asum(x):
    return _csum(_rsum(x))


def _full(shape):
    nd = len(shape)
    return pl.BlockSpec(shape, lambda *_: (0,) * nd)


def ada_mod(c_all, w_ada_sh, b_ada_sh):
    def body(c_ref, w_ref, b_ref, o_ref):
        cv = c_ref[...]
        s = cv * _sig(cv)
        o_ref[...] = jnp.dot(s, w_ref[...], preferred_element_type=F32,
                             precision=lax.Precision.HIGHEST) + b_ref[...]

    n = w_ada_sh.shape[1]
    return pl.pallas_call(body, name="ada_mod", out_shape=jax.ShapeDtypeStruct((8, n), F32),
                          compiler_params=_cp())(c_all, w_ada_sh, b_ada_sh)


def ada_grad(c_all, dmod_sh):
    def body(c_ref, d_ref, o_ref):
        cv = c_ref[...]
        s = cv * _sig(cv)
        o_ref[...] = lax.dot_general(s, d_ref[...], (((0,), (0,)), ((), ())), preferred_element_type=F32,
                                     precision=lax.Precision.HIGHEST)

    n = dmod_sh.shape[1]
    return pl.pallas_call(body, name="ada_grad", out_shape=jax.ShapeDtypeStruct((D, n), F32),
                          compiler_params=_cp())(c_all, dmod_sh)


def norm_proj(x, norm_w, scale, shift, wcat):
    t = x.shape[0]
    tm = min(t, 1024)

    def body(x_ref, nw_ref, sc_ref, sh_ref, w_ref, p_ref, ht_ref, hs):
        @pl.when(pl.program_id(1) == 0)
        def _():
            xv = x_ref[...]
            r = lax.rsqrt(jnp.mean(xv * xv, axis=-1, keepdims=True) + EPS)
            h = (xv * r) * nw_ref[...]
            h = h * (1.0 + sc_ref[...]) + sh_ref[...]
            hs[...] = h.astype(BF16)
            ht_ref[...] = h.T.astype(BF16)

        p_ref[...] = _dot(hs[...], w_ref[...])

    vec = pl.BlockSpec((1, D), lambda i, j: (0, 0))
    return pl.pallas_call(
        body, name="norm_proj", grid=(t // tm, NP // TN),
        in_specs=[pl.BlockSpec((tm, D), lambda i, j: (i, 0)), vec, vec, vec,
                  pl.BlockSpec((D, TN), lambda i, j: (0, j))],
        out_specs=[pl.BlockSpec((tm, TN), lambda i, j: (i, j)), pl.BlockSpec((D, tm), lambda i, j: (0, i))],
        out_shape=[jax.ShapeDtypeStruct((t, NP), F32), jax.ShapeDtypeStruct((D, t), BF16)],
        scratch_shapes=[pltpu.VMEM((tm, D), BF16)],
        compiler_params=_cp(("parallel", "arbitrary")),
    )(x, norm_w, scale, shift, wcat)


def _bucket_table():
    qi = jnp.arange(BLK)[:, None]
    kj = jnp.arange(2 * BLK)[None, :]
    dist = qi + BLK - kj
    n = jnp.maximum(dist, 0)
    max_exact = NBUCKET // 2
    nf = jnp.maximum(n, 1).astype(F32)
    large = max_exact + (jnp.log(nf / max_exact) / math.log(MAXDIST / max_exact)
                         * (NBUCKET - max_exact)).astype(jnp.int32)
    large = jnp.minimum(large, NBUCKET - 1)
    bucket = jnp.where(n < max_exact, n, large).astype(jnp.int32)
    valid = (dist >= 0) & (dist < BLK)
    return jnp.where(valid, bucket, -1)


def bias_expand(rel_bias, sinks, bucket):
    def body(rb_ref, sk_ref, bk_ref, o_ref):
        hd = pl.program_id(0)
        bk = bk_ref[...]
        col = lax.broadcasted_iota(jnp.int32, (BLK, 2 * BLK), 1)

        def step(b, acc):
            return jnp.where(bk == b, rb_ref[b, hd], acc)

        acc = lax.fori_loop(0, NBUCKET, step, jnp.full((BLK, 2 * BLK), NEG, F32))
        acc = jnp.where(col == 0, sk_ref[0, hd], acc)
        o_ref[1, 0] = acc
        o_ref[0, 0] = jnp.where(jnp.logical_and(col > 0, col < BLK), NEG, acc)

    smem = pl.BlockSpec(memory_space=pltpu.SMEM)
    return pl.pallas_call(
        body, name="bias_expand", grid=(HQ,),
        in_specs=[smem, smem, _full((BLK, 2 * BLK))],
        out_specs=pl.BlockSpec((2, 1, BLK, 2 * BLK), lambda h: (0, h, 0, 0)),
        out_shape=jax.ShapeDtypeStruct((2, HQ, BLK, 2 * BLK), F32),
        compiler_params=_cp(("arbitrary",)),
    )(rel_bias, sinks, bucket)


def bias_reduce(dacc, bucket):
    def body(d_ref, bk_ref, o_ref):
        bk = bk_ref[...]
        lane = lax.broadcasted_iota(jnp.int32, (1, 128), 1)
        col = lax.broadcasted_iota(jnp.int32, (BLK, 2 * BLK), 1)
        for hd in range(HQ):
            dv = d_ref[hd]

            def step(b, row):
                s = _asum(jnp.where(bk == b, dv, 0.0))
                return jnp.where(lane == b, s, row)

            row = lax.fori_loop(0, NBUCKET, step, jnp.zeros((1, 128), F32))
            o_ref[hd:hd + 1, :] = jnp.where(lane == NBUCKET, _asum(jnp.where(col == 0, dv, 0.0)), row)

    return pl.pallas_call(body, name="bias_reduce", out_shape=jax.ShapeDtypeStruct((HQ, 128), F32),
                          compiler_params=_cp())(dacc, bucket)


GQ = GRP * BLK


def _stack_heads(x, nh):
    return jnp.concatenate([x[:, DH * h:DH * (h + 1)] for h in range(nh)], axis=0)


def _unstack(xs, nh):
    rows = xs.shape[0] // nh
    return jnp.concatenate([xs[rows * h:rows * (h + 1)] for h in range(nh)], axis=1)


def _rms(x):
    return lax.rsqrt(jnp.mean(x * x, axis=-1, keepdims=True) + EPS)


def _stack_q(q, qw):
    qs = _stack_heads(q, HQ)
    r = _rms(qs)
    qhat = qs * r
    return qhat * qw, qhat, r


def _band_first(shape):
    return (lax.broadcasted_iota(jnp.int32, shape, 0) & (2 * BLK - 1)) == 0


def _stack_kv(kp, kc, vp, vc, kw):
    ks = _stack_heads(jnp.concatenate([kp, kc], axis=0), HKV)
    r = _rms(ks)
    khat = ks * r
    first = _band_first(ks.shape)
    kn = jnp.where(first, 0.0, khat * kw)
    v2 = jnp.where(first, 0.0, _stack_heads(jnp.concatenate([vp, vc], axis=0), HKV)).astype(BF16)
    return kn, khat, r, v2


def _softmax_rows(s):
    p = jnp.exp(s - jnp.max(s, axis=-1, keepdims=True))
    return p * (1.0 / _rsum(p))


def attn_fwd(proj, biasm, q_norm_w, k_norm_w):
    t = proj.shape[0]
    nb = t // BLK

    def body(q_ref, kc_ref, kp_ref, vc_ref, vp_ref, bm_ref, qw_ref, kw_ref, o_ref):
        qn = _stack_q(q_ref[...], qw_ref[...])[0].astype(BF16)
        kn, _, _, v2 = _stack_kv(kp_ref[...], kc_ref[...], vp_ref[...], vc_ref[...], kw_ref[...])
        knb = kn.astype(BF16)
        s = jnp.concatenate([_dot_nt(qn[GQ * j:GQ * (j + 1)], knb[2 * BLK * j:2 * BLK * (j + 1)])
                             for j in range(HKV)], axis=0)
        pr = _softmax_rows(s * SCALE + bm_ref[0].reshape(HQ * BLK, 2 * BLK)).astype(BF16)
        o = jnp.concatenate([_dot(pr[GQ * j:GQ * (j + 1)], v2[2 * BLK * j:2 * BLK * (j + 1)])
                             for j in range(HKV)], axis=0)
        o_ref[...] = _unstack(o, HQ)

    kblk, vblk = C_K // 256, C_V // 256
    prev = lambda n: jnp.maximum(n - 1, 0)
    return pl.pallas_call(
        body, name="attn_fwd", grid=(nb,),
        in_specs=[pl.BlockSpec((BLK, D), lambda n: (n, 0)),
                  pl.BlockSpec((BLK, 256), lambda n: (n, kblk)),
                  pl.BlockSpec((BLK, 256), lambda n: (prev(n), kblk)),
                  pl.BlockSpec((BLK, 256), lambda n: (n, vblk)),
                  pl.BlockSpec((BLK, 256), lambda n: (prev(n), vblk)),
                  pl.BlockSpec((1, HQ, BLK, 2 * BLK), lambda n: (jnp.minimum(n, 1), 0, 0, 0)),
                  _full((1, DH)), _full((1, DH))],
        out_specs=pl.BlockSpec((BLK, D), lambda n: (n, 0)),
        out_shape=jax.ShapeDtypeStruct((t, D), F32),
        compiler_params=_cp(("parallel",)),
    )(proj, proj, proj, proj, proj, biasm, q_norm_w, k_norm_w)


def attn_bwd(proj, dao, biasm, q_norm_w, k_norm_w):
    t = proj.shape[0]
    nb = t // BLK
    kb = 2 * BLK

    def body(q_ref, kc_ref, kp_ref, vc_ref, vp_ref, do_ref, bm_ref, qw_ref, kw_ref,
             dq_ref, dk_ref, dv_ref, dqw_ref, dkw_ref, dacc_ref, ck, cv, pk, pv, nk, nv):
        n = pl.program_id(0)

        @pl.when(n == 0)
        def _():
            for ref in (dqw_ref, dkw_ref, dacc_ref, ck, cv):
                ref[...] = jnp.zeros_like(ref)

        qw = qw_ref[...]
        kw = kw_ref[...]
        kn, khat, rk, v2 = _stack_kv(kp_ref[...], kc_ref[...], vp_ref[...], vc_ref[...], kw)
        grp = lambda a, j: a[GQ * j:GQ * (j + 1)]
        band = lambda a, j: a[kb * j:kb * (j + 1)]

        @pl.when(n < nb)
        def _():
            qn, qhat, rq = _stack_q(q_ref[...], qw)
            qnb = qn.astype(BF16)
            knb = kn.astype(BF16)
            dos = _stack_heads(do_ref[...], HQ).astype(BF16)
            s = jnp.concatenate([_dot_nt(grp(qnb, j), band(knb, j)) for j in range(HKV)], axis=0)
            pr = _softmax_rows(s * SCALE + bm_ref[0].reshape(HQ * BLK, kb))
            dp = jnp.concatenate([_dot_nt(grp(dos, j), band(v2, j)) for j in range(HKV)], axis=0)
            ds = pr * (dp - _rsum(pr * dp))
            dacc_ref[...] += ds.reshape(HQ, BLK, kb)
            dsb = ds.astype(BF16)
            prb = pr.astype(BF16)
            dqn = jnp.concatenate([_dot(grp(dsb, j), band(knb, j)) for j in range(HKV)], axis=0) * SCALE
            dqhat = dqn * qw
            dq = rq * (dqhat - qhat * jnp.mean(dqhat * qhat, axis=-1, keepdims=True))
            dq_ref[...] = _unstack(dq, HQ).astype(BF16)
            dqw_ref[...] += _csum(dqn * qhat)
            first = _band_first((kb, DH))
            for j in range(HKV):
                rows = slice(BLK * j, BLK * (j + 1))
                dkn = jnp.where(first, 0.0, _dot_tn(grp(dsb, j), grp(qnb, j)) * SCALE)
                dvj = jnp.where(first, 0.0, _dot_tn(grp(prb, j), grp(dos, j)))
                pk[rows, :] = dkn[:BLK]
                nk[rows, :] = dkn[BLK:]
                pv[rows, :] = dvj[:BLK]
                nv[rows, :] = dvj[BLK:]

        @pl.when(n == nb)
        def _():
            for ref in (pk, pv, nk, nv):
                ref[...] = jnp.zeros_like(ref)

        khp = jnp.concatenate([khat[kb * j:kb * j + BLK] for j in range(HKV)], axis=0)
        rkp = jnp.concatenate([rk[kb * j:kb * j + BLK] for j in range(HKV)], axis=0)
        dkn = ck[...] + pk[...]
        dkhat = dkn * kw
        dk = rkp * (dkhat - khp * jnp.mean(dkhat * khp, axis=-1, keepdims=True))
        dk_ref[...] = _unstack(dk, HKV).astype(BF16)
        dkw_ref[...] += _csum(dkn * khp)
        dv_ref[...] = _unstack(cv[...] + pv[...], HKV).astype(BF16)
        ck[...] = nk[...]
        cv[...] = nv[...]

    kblk, vblk = C_K // 256, C_V // 256
    cur = lambda n: jnp.minimum(n, nb - 1)
    prev = lambda n: jnp.maximum(n - 1, 0)
    carry = pltpu.VMEM((HKV * BLK, DH), F32)
    return pl.pallas_call(
        body, name="attn_bwd", grid=(nb + 1,),
        in_specs=[pl.BlockSpec((BLK, D), lambda n: (cur(n), 0)),
                  pl.BlockSpec((BLK, 256), lambda n: (cur(n), kblk)), pl.BlockSpec((BLK, 256), lambda n: (prev(n), kblk)),
                  pl.BlockSpec((BLK, 256), lambda n: (cur(n), vblk)), pl.BlockSpec((BLK, 256), lambda n: (prev(n), vblk)),
                  pl.BlockSpec((BLK, D), lambda n: (cur(n), 0)),
                  pl.BlockSpec((1, HQ, BLK, kb), lambda n: (jnp.minimum(n, 1), 0, 0, 0)),
                  _full((1, DH)), _full((1, DH))],
        out_specs=[pl.BlockSpec((BLK, D), lambda n: (cur(n), 0)),
                   pl.BlockSpec((BLK, 256), lambda n: (prev(n), 0)), pl.BlockSpec((BLK, 256), lambda n: (prev(n), 0)),
                   _full((1, DH)), _full((1, DH)), _full((HQ, BLK, kb))],
        out_shape=[jax.ShapeDtypeStruct((t, D), BF16), jax.ShapeDtypeStruct((t, 256), BF16),
                   jax.ShapeDtypeStruct((t, 256), BF16), jax.ShapeDtypeStruct((1, DH), F32),
                   jax.ShapeDtypeStruct((1, DH), F32), jax.ShapeDtypeStruct((HQ, BLK, kb), F32)],
        scratch_shapes=[carry] * 6,
        compiler_params=_cp(("arbitrary",)),
    )(proj, proj, proj, proj, proj, dao, biasm, q_norm_w, k_norm_w)


CONV_TM = 256


def conv_fwd(proj, conv_w, conv_b):
    t = proj.shape[0]
    tm = min(t, CONV_TM)
    cblk = C_XBC // XBC

    def body(x_ref, xp_ref, w_ref, b_ref, o_ref, xe):
        i = pl.program_id(0)
        xe[0:8, :] = jnp.where(i == 0, 0.0, xp_ref[...])
        xe[8:8 + tm, :] = x_ref[...]
        acc = jnp.broadcast_to(b_ref[...], (tm, XBC))
        for j in range(CONV_K):
            acc = acc + w_ref[j:j + 1, :] * xe[5 + j:5 + j + tm, :]
        o_ref[...] = acc * _sig(acc)

    return pl.pallas_call(
        body, name="conv_fwd", grid=(t // tm,),
        in_specs=[pl.BlockSpec((tm, XBC), lambda i: (i, cblk)),
                  pl.BlockSpec((8, XBC), lambda i: (jnp.maximum(i * (tm // 8) - 1, 0), cblk)),
                  _full((CONV_K, XBC)), _full((1, XBC))],
        out_specs=pl.BlockSpec((tm, XBC), lambda i: (i, 0)),
        out_shape=jax.ShapeDtypeStruct((t, XBC), F32),
        scratch_shapes=[pltpu.VMEM((tm + 8, XBC), F32)],
        compiler_params=_cp(("parallel",)),
    )(proj, proj, conv_w, conv_b)


def conv_bwd(proj, dact, conv_w, conv_b):
    t = proj.shape[0]
    tm = min(t, CONV_TM)
    nt = t // tm
    cblk = C_XBC // XBC

    def body(x_ref, xp_ref, xn_ref, d_ref, dn_ref, w_ref, b_ref, dx_ref, dw_ref, db_ref, xe, de):
        i = pl.program_id(0)

        @pl.when(i == 0)
        def _():
            dw_ref[...] = jnp.zeros_like(dw_ref)
            db_ref[...] = jnp.zeros_like(db_ref)

        xe[0:8, :] = jnp.where(i == 0, 0.0, xp_ref[...])
        xe[8:8 + tm, :] = x_ref[...]
        xe[8 + tm:16 + tm, :] = xn_ref[...]
        pre = jnp.broadcast_to(b_ref[...], (tm + 8, XBC))
        for j in range(CONV_K):
            pre = pre + w_ref[j:j + 1, :] * xe[5 + j:5 + j + tm + 8, :]
        sg = _sig(pre)
        dsilu = sg * (1.0 + pre * (1.0 - sg))
        dpre_c = d_ref[...] * dsilu[0:tm]
        dpre_n = jnp.where(i == nt - 1, 0.0, dn_ref[...] * dsilu[tm:tm + 8])
        de[0:tm, :] = dpre_c
        de[tm:tm + 8, :] = dpre_n
        dx = jnp.zeros((tm, XBC), F32)
        for j in range(CONV_K):
            dx = dx + w_ref[j:j + 1, :] * de[3 - j:3 - j + tm, :]
            dw_ref[j:j + 1, :] += _csum(dpre_c * xe[5 + j:5 + j + tm, :])
        dx_ref[...] = dx.astype(BF16)
        db_ref[...] += _csum(dpre_c)

    r8 = tm // 8
    return pl.pallas_call(
        body, name="conv_bwd", grid=(nt,),
        in_specs=[pl.BlockSpec((tm, XBC), lambda i: (i, cblk)),
                  pl.BlockSpec((8, XBC), lambda i: (jnp.maximum(i * r8 - 1, 0), cblk)),
                  pl.BlockSpec((8, XBC), lambda i: (jnp.minimum((i + 1) * r8, nt * r8 - 1), cblk)),
                  pl.BlockSpec((tm, XBC), lambda i: (i, 0)),
                  pl.BlockSpec((8, XBC), lambda i: (jnp.minimum((i + 1) * r8, nt * r8 - 1), 0)),
                  _full((CONV_K, XBC)), _full((1, XBC))],
        out_specs=[pl.BlockSpec((tm, XBC), lambda i: (i, 0)), _full((CONV_K, XBC)), _full((1, XBC))],
        out_shape=[jax.ShapeDtypeStruct((t, XBC), BF16), jax.ShapeDtypeStruct((CONV_K, XBC), F32),
                   jax.ShapeDtypeStruct((1, XBC), F32)],
        scratch_shapes=[pltpu.VMEM((tm + 16, XBC), F32), pltpu.VMEM((tm + 8, XBC), F32)],
        compiler_params=_cp(("arbitrary",)),
    )(proj, proj, proj, dact, dact, conv_w, conv_b)


def _split3(x):
    h = x.astype(BF16)
    r = x - h.astype(F32)
    m = r.astype(BF16)
    lo = (r - m.astype(F32)).astype(BF16)
    return h, m, lo


def _tri_mm(tri, x):
    h, m, lo = _split3(x)
    return _dot(tri, h) + _dot(tri, m) + _dot(tri, lo)


def _softplus(x):
    return jnp.maximum(x, 0.0) + jnp.log1p(jnp.exp(-jnp.abs(x)))


def _chunk_decays(dt_raw, dtb, alog):
    dtv = _softplus(dt_raw + dtb)
    a = -jnp.exp(alog)
    ri = lax.broadcasted_iota(jnp.int32, (BLK, BLK), 0)
    ci = lax.broadcasted_iota(jnp.int32, (BLK, BLK), 1)
    causal = ri >= ci
    acum = _tri_mm(causal.astype(BF16), dtv * a)
    return dtv, a, causal, acum, acum.T


NPAIR = SH // 2


def _pairs(x):
    return jnp.stack([x[:, 128 * k:128 * (k + 1)] for k in range(NPAIR)])


def _unpairs(x3):
    return jnp.concatenate([x3[k] for k in range(NPAIR)], axis=1)


def _per_head_cols(m):
    return jnp.stack([jnp.broadcast_to(m[:, h:h + 1], m.shape) for h in range(SH)])


def _pair_lanes(t):
    r = t.reshape(NPAIR, 2, t.shape[1], 128)
    lo = lax.broadcasted_iota(jnp.int32, (1, t.shape[1], 128), 2) < SP
    return jnp.where(lo, r[:, 0], r[:, 1])


class _Chunk:
    pass


def _chunk_common(dt_raw, dtb, alog, dskip):
    cm = _Chunk()
    cm.dtv, cm.a, cm.causal, acum, acum_t = _chunk_decays(dt_raw, dtb, alog)
    cm.acol = _per_head_cols(acum)
    cm.arow = jnp.stack([acum_t[h:h + 1, :] for h in range(SH)])
    cm.lam = jnp.exp(jnp.where(cm.causal[None], cm.acol - cm.arow, NEG))
    apl = _pair_lanes(cm.acol)
    alast = apl[:, BLK - 1:BLK, :]
    cm.dpl = _pair_lanes(_per_head_cols(cm.dtv))
    cm.eapl = jnp.exp(apl)
    cm.epl = jnp.exp(alast - apl)
    cm.cdpl = jnp.exp(alast)
    cm.dskpl = _pair_lanes(_per_head_cols(dskip))
    cm.lo = lax.broadcasted_iota(jnp.int32, (1, BLK, 128), 2) < SP
    return cm


def ssd_fwd(act, proj, dtb_p, alog_p, dsk_p):
    t = act.shape[0]
    nc = t // BLK

    def body(xs_ref, b_ref, c_ref, dt_ref, dtb_ref, al_ref, dk_ref, y_ref, sp_ref, st):
        c = pl.program_id(0)

        @pl.when(c == 0)
        def _():
            st[...] = jnp.zeros_like(st)

        s_t = st[...]
        sp_ref[0] = s_t
        cm = _chunk_common(dt_ref[...], dtb_ref[...], al_ref[...], dk_ref[...])
        gms, cbs, bts = [], [], []
        for g in range(SG):
            bf = b_ref[:, SN * g:SN * (g + 1)]
            cb = c_ref[:, SN * g:SN * (g + 1)].astype(BF16)
            gms.append(_dot_nt(cb, bf.astype(BF16)))
            cbs.append(cb)
            bts.append(bf.T.astype(BF16))
        m = (cm.lam.reshape(SG, SR, BLK, BLK) * jnp.stack(gms)[:, None]).reshape(SH, BLK, BLK).astype(BF16)
        xs16 = _pairs(xs_ref[...])
        xdt16 = xs16 * cm.dpl
        x_lo = jnp.where(cm.lo, xdt16, 0.0).astype(BF16)
        x_hi = jnp.where(cm.lo, 0.0, xdt16).astype(BF16)
        s16 = _pairs(s_t)
        s16b = s16.astype(BF16)
        yd = jnp.stack([_dot(m[2 * k], x_lo[k]) + _dot(m[2 * k + 1], x_hi[k]) for k in range(NPAIR)])
        yo = jnp.stack([_dot(cbs[k // (NPAIR // SG)], s16b[k]) for k in range(NPAIR)])
        y_ref[...] = _unpairs(yd + yo * cm.eapl + cm.dskpl * xs16)
        xe = (xdt16 * cm.epl).astype(BF16)
        st[...] = _unpairs(cm.cdpl * s16 + jnp.stack([_dot(bts[k // (NPAIR // SG)], xe[k]) for k in range(NPAIR)]))

    vec = _full((1, 128))
    return pl.pallas_call(
        body, name="ssd_fwd", grid=(nc,),
        in_specs=[pl.BlockSpec((BLK, SSM_W), lambda c: (c, 0)),
                  pl.BlockSpec((BLK, SG * SN), lambda c: (c, SSM_W // (SG * SN))),
                  pl.BlockSpec((BLK, SG * SN), lambda c: (c, SSM_W // (SG * SN) + 1)),
                  pl.BlockSpec((BLK, 128), lambda c: (c, C_DT // 128)), vec, vec, vec],
        out_specs=[pl.BlockSpec((BLK, SSM_W), lambda c: (c, 0)), pl.BlockSpec((1, SN, SSM_W), lambda c: (c, 0, 0))],
        out_shape=[jax.ShapeDtypeStruct((t, SSM_W), F32), jax.ShapeDtypeStruct((nc, SN, SSM_W), F32)],
        scratch_shapes=[pltpu.VMEM((SN, SSM_W), F32)],
        compiler_params=_cp(("arbitrary",)),
    )(act, act, act, proj, dtb_p, alog_p, dsk_p)


def _head_sums(q):
    r = q.shape[1]
    lo = lax.broadcasted_iota(jnp.int32, (1, r, 128), 2) < SP
    s_lo = jnp.sum(jnp.where(lo, q, 0.0), axis=-1, keepdims=True)
    s_hi = jnp.sum(jnp.where(lo, 0.0, q), axis=-1, keepdims=True)
    lane = lax.broadcasted_iota(jnp.int32, (r, 128), 1)
    out = jnp.zeros((r, 128), F32)
    for k in range(NPAIR):
        out = jnp.where(lane == 2 * k, s_lo[k], jnp.where(lane == 2 * k + 1, s_hi[k], out))
    return out


def ssd_bwd(act, proj, dy, sprev, dtb_p, alog_p, dsk_p):
    t = act.shape[0]
    nc = t // BLK

    def body(xs_ref, b_ref, c_ref, dt_ref, dy_ref, sp_ref, dtb_ref, al_ref, dk_ref,
             da_ref, ddt_ref, ddtb_ref, dal_ref, ddk_ref, dst):
        i = pl.program_id(0)

        @pl.when(i == 0)
        def _():
            dst[...] = jnp.zeros_like(dst)
            ddtb_ref[...] = jnp.zeros_like(ddtb_ref)
            dal_ref[...] = jnp.zeros_like(dal_ref)
            ddk_ref[...] = jnp.zeros_like(ddk_ref)

        dt_raw = dt_ref[...]
        dtb = dtb_ref[...]
        cm = _chunk_common(dt_raw, dtb, al_ref[...], dk_ref[...])
        ri = lax.broadcasted_iota(jnp.int32, (BLK, BLK), 0)
        ci = lax.broadcasted_iota(jnp.int32, (BLK, BLK), 1)
        lam_t = jnp.exp(jnp.where((ri <= ci)[None], cm.arow - cm.acol, NEG))
        bbs, cbs, cts, gms = [], [], [], []
        for g in range(SG):
            bf = b_ref[:, SN * g:SN * (g + 1)]
            cf = c_ref[:, SN * g:SN * (g + 1)]
            bbs.append(bf.astype(BF16))
            cbs.append(cf.astype(BF16))
            cts.append(cf.T.astype(BF16))
            gms.append(_dot_nt(bbs[g], cbs[g]))
        grp = lambda k: k // (NPAIR // SG)
        xs16 = _pairs(xs_ref[...])
        dy16 = _pairs(dy_ref[...])
        sp16 = _pairs(sp_ref[0])
        ds16 = _pairs(dst[...])
        xdt16 = xs16 * cm.dpl
        xdtb = xdt16.astype(BF16)
        dyh = [jnp.where(cm.lo, dy16, 0.0).astype(BF16), jnp.where(cm.lo, 0.0, dy16).astype(BF16)]
        m_t = (lam_t.reshape(SG, SR, BLK, BLK) * jnp.stack(gms)[:, None]).reshape(SH, BLK, BLK).astype(BF16)
        dxdt = jnp.stack([_dot(m_t[2 * k], dyh[0][k]) + _dot(m_t[2 * k + 1], dyh[1][k]) for k in range(NPAIR)])
        dm = jnp.stack([_dot_nt(dyh[h % 2][h // 2], xdtb[h // 2]) for h in range(SH)])
        dgl = (dm * cm.lam).reshape(SG, SR, BLK, BLK)
        dg = jnp.sum(dgl, axis=1).astype(BF16)
        w = (dgl * jnp.stack([_dot_nt(cbs[g], bbs[g]) for g in range(SG)])[:, None]).reshape(SH, BLK, BLK)
        w_rows = jnp.sum(w, axis=2, keepdims=True)
        w_cols = jnp.concatenate([jnp.sum(w, axis=1)] + [jnp.zeros((128 - SH, BLK), F32)], axis=0).T
        lane_c = lax.broadcasted_iota(jnp.int32, (BLK, 128), 1)
        da_cols = -w_cols
        for h in range(SH):
            da_cols = jnp.where(lane_c == h, da_cols + w_rows[h], da_cols)
        ds16b = ds16.astype(BF16)
        sp16b = sp16.astype(BF16)
        dxs = jnp.stack([_dot(bbs[grp(k)], ds16b[k]) for k in range(NPAIR)]) * cm.epl
        dxdt = dxdt + dxs
        dya = (dy16 * cm.eapl).astype(BF16)
        xe = (xdt16 * cm.epl).astype(BF16)
        dcs, dbs = [], []
        for g in range(SG):
            ks = range(g * (NPAIR // SG), (g + 1) * (NPAIR // SG))
            dcs.append(sum(_dot_nt(dya[k], sp16b[k]) for k in ks) + _dot(dg[g], bbs[g]))
            dbs.append(sum(_dot_nt(xe[k], ds16b[k]) for k in ks) + _dot_tn(dg[g], cbs[g]))
        dst[...] = _unpairs(cm.cdpl * ds16 + jnp.stack([_dot(cts[grp(k)], dya[k]) for k in range(NPAIR)]))
        da_ref[...] = jnp.concatenate([_unpairs(dxdt * cm.dpl + cm.dskpl * dy16)] + dbs + dcs, axis=1)
        y_off = jnp.stack([_dot(cbs[grp(k)], sp16b[k]) for k in range(NPAIR)]) * cm.eapl
        da_cols = da_cols + _head_sums(dy16 * y_off - xdt16 * dxs)
        last = _head_sums(jnp.sum(xdt16 * dxs, axis=1, keepdims=True)
                          + cm.cdpl * jnp.sum(ds16 * sp16, axis=1, keepdims=True))
        ddt = _head_sums(dxdt * xs16)
        row_i = lax.broadcasted_iota(jnp.int32, (BLK, 128), 0)
        dacum = da_cols + jnp.where(row_i == BLK - 1, last, 0.0)
        dda = _tri_mm((ri <= ci).astype(BF16), dacum)
        ddt = ddt + dda * cm.a
        dal_ref[...] += _csum(dda * cm.dtv) * cm.a
        ddt_raw = jnp.where(lane_c < SH, ddt * _sig(dt_raw + dtb), 0.0)
        ddt_ref[...] = ddt_raw.astype(BF16)
        ddtb_ref[...] += _csum(ddt_raw)
        ddk_ref[...] += _head_sums(jnp.sum(dy16 * xs16, axis=1, keepdims=True))

    rev = lambda i: nc - 1 - i
    vec = _full((1, 128))
    slab = pl.BlockSpec((BLK, SSM_W), lambda i: (rev(i), 0))
    return pl.pallas_call(
        body, name="ssd_bwd", grid=(nc,),
        in_specs=[slab,
                  pl.BlockSpec((BLK, SG * SN), lambda i: (rev(i), SSM_W // (SG * SN))),
                  pl.BlockSpec((BLK, SG * SN), lambda i: (rev(i), SSM_W // (SG * SN) + 1)),
                  pl.BlockSpec((BLK, 128), lambda i: (rev(i), C_DT // 128)),
                  slab,
                  pl.BlockSpec((1, SN, SSM_W), lambda i: (rev(i), 0, 0)), vec, vec, vec],
        out_specs=[pl.BlockSpec((BLK, XBC), lambda i: (rev(i), 0)), pl.BlockSpec((BLK, 128), lambda i: (rev(i), 0)),
                   vec, vec, vec],
        out_shape=[jax.ShapeDtypeStruct((t, XBC), F32), jax.ShapeDtypeStruct((t, 128), BF16),
                   jax.ShapeDtypeStruct((1, 128), F32), jax.ShapeDtypeStruct((1, 128), F32),
                   jax.ShapeDtypeStruct((1, 128), F32)],
        scratch_shapes=[pltpu.VMEM((SN, SSM_W), F32)],
        compiler_params=_cp(("arbitrary",)),
    )(act, act, act, proj, dy, sprev, dtb_p, alog_p, dsk_p)


TAIL_TM = 128


def _dsilu(z, s):
    return s * (1.0 + z * (1.0 - s))


def tail(proj, ao, yss, x, target, gate, ssm_nw, w_at, w_ss, w_ou):
    t = x.shape[0]
    tm = min(t, TAIL_TM)
    gw = SSM_W // SG

    def body(ao_ref, za_ref, ga_ref, gb_ref, zm_ref, ys_ref, x_ref, tg_ref, gt_ref, nw_ref, wa_ref, ws_ref, wo_ref,
             loss_ref, dy_ref, dao_ref, dza_ref, dga_ref, dgb_ref, dys_ref, dzm_ref,
             ua_ref, yn_ref, mg_ref, dya_ref, dyb_ref, do_ref, dgt_ref, dnw_ref):
        i = pl.program_id(0)

        @pl.when(i == 0)
        def _():
            loss_ref[...] = jnp.zeros_like(loss_ref)
            dgt_ref[...] = jnp.zeros_like(dgt_ref)
            dnw_ref[...] = jnp.zeros_like(dnw_ref)

        ao = ao_ref[...]
        za = za_ref[...]
        sa = _sig(za)
        sila = za * sa
        ua_f = ao * sila
        ua = ua_f.astype(BF16)
        ya = _dot(ua, wa_ref[...])
        zm = zm_ref[...]
        sm = _sig(zm)
        silm = zm * sm
        ys = ys_ref[...]
        u = ys * silm
        nw = nw_ref[...]
        rs, uns = [], []
        for g in range(SG):
            ug = u[:, gw * g:gw * (g + 1)]
            r = lax.rsqrt(jnp.mean(ug * ug, axis=-1, keepdims=True) + EPS)
            rs.append(r)
            uns.append(ug * r)
        un = jnp.concatenate(uns, axis=1)
        yn_f = un * nw
        yn = yn_f.astype(BF16)
        yb = _dot(yn, ws_ref[...])
        sga = _sig(ga_ref[...])
        sgb = _sig(gb_ref[...])
        mg_f = sga * ya + sgb * yb
        mg = mg_f.astype(BF16)
        o = _dot(mg, wo_ref[...])
        gt = gt_ref[...]
        err = (x_ref[...] + gt * o) - tg_ref[...]
        lane = lax.broadcasted_iota(jnp.int32, (1, 128), 1)
        loss_ref[...] += jnp.where(lane == 0, 0.5 * _asum(_rsum(err * err) / D), 0.0)
        dy = err * (1.0 / D)
        dy_ref[...] = dy
        dgt_ref[...] += _csum(dy * o)
        do = (dy * gt).astype(BF16)
        dmg = _dot_nt(do, wo_ref[...])
        dga_ref[...] = (dmg * ya * sga * (1.0 - sga)).astype(BF16)
        dgb_ref[...] = (dmg * yb * sgb * (1.0 - sgb)).astype(BF16)
        dya = (dmg * sga).astype(BF16)
        dyb = (dmg * sgb).astype(BF16)
        dua = _dot_nt(dya, wa_ref[...])
        dao_ref[...] = dua * sila
        dza_ref[...] = (dua * ao * _dsilu(za, sa)).astype(BF16)
        dyn = _dot_nt(dyb, ws_ref[...])
        dnw_ref[...] += _csum(dyn * un)
        dun = dyn * nw
        dus = []
        for g in range(SG):
            gs = slice(gw * g, gw * (g + 1))
            dus.append(rs[g] * (dun[:, gs] - uns[g] * jnp.mean(dun[:, gs] * uns[g], axis=-1, keepdims=True)))
        du = jnp.concatenate(dus, axis=1)
        dys_ref[...] = du * silm
        dzm_ref[...] = (du * ys * _dsilu(zm, sm)).astype(BF16)
        ua_ref[...] = ua_f.T.astype(BF16)
        yn_ref[...] = yn_f.T.astype(BF16)
        mg_ref[...] = mg_f.T.astype(BF16)
        dya_ref[...] = dya
        dyb_ref[...] = dyb
        do_ref[...] = do

    row = lambda w: pl.BlockSpec((tm, w), lambda i: (i, 0))
    pcol = lambda w, c0: pl.BlockSpec((tm, w), lambda i: (i, c0 // w))
    sd = lambda w, dt: jax.ShapeDtypeStruct((t, w), dt)
    colt = lambda w: pl.BlockSpec((w, tm), lambda i: (0, i))
    sdt = lambda w: jax.ShapeDtypeStruct((w, t), BF16)
    return pl.pallas_call(
        body, name="tail", grid=(t // tm,),
        in_specs=[row(D), pcol(D, C_ZA), pcol(D, C_GA), pcol(D, C_GB), pcol(SSM_W, C_ZM), row(SSM_W), row(D), row(D),
                  _full((1, D)), _full((1, SSM_W)), _full((D, D)), _full((SSM_W, D)), _full((D, D))],
        out_specs=[_full((1, 128)), row(D), row(D), row(D), row(D), row(D), row(SSM_W), row(SSM_W),
                   colt(D), colt(SSM_W), colt(D), row(D), row(D), row(D), _full((1, D)), _full((1, SSM_W))],
        out_shape=[jax.ShapeDtypeStruct((1, 128), F32), sd(D, F32), sd(D, F32), sd(D, BF16), sd(D, BF16), sd(D, BF16),
                   sd(SSM_W, F32), sd(SSM_W, BF16), sdt(D), sdt(SSM_W), sdt(D), sd(D, BF16),
                   sd(D, BF16), sd(D, BF16), jax.ShapeDtypeStruct((1, D), F32), jax.ShapeDtypeStruct((1, SSM_W), F32)],
        compiler_params=_cp(("arbitrary",)),
    )(ao, proj, proj, proj, proj, yss, x, target, gate, ssm_nw, w_at, w_ss, w_ou)


def dproj_bwd(dproj, wcat, x, dy, norm_w, scale):
    t = x.shape[0]
    tm = min(t, 512)
    tk = NP // 6
    nk = NP // tk
    nt = t // tm

    def body(dp_ref, w_ref, x_ref, dy_ref, nw_ref, sc_ref, gx_ref, dnw_ref, dsc_ref, dsh_ref, acc, dwe_ref):
        i = pl.program_id(0)
        k = pl.program_id(1)

        @pl.when(jnp.logical_and(i == 0, k == 0))
        def _():
            dwe_ref[...] = jnp.zeros_like(dwe_ref)
            dsh_ref[...] = jnp.zeros_like(dsh_ref)
            dnw_ref[...] = jnp.zeros_like(dnw_ref)
            dsc_ref[...] = jnp.zeros_like(dsc_ref)

        part = _dot_nt(dp_ref[...], w_ref[...])

        @pl.when(k == 0)
        def _():
            acc[...] = part

        @pl.when(k > 0)
        def _():
            acc[...] += part

        @pl.when(k == nk - 1)
        def _():
            dh = acc[...]
            xv = x_ref[...]
            r = lax.rsqrt(jnp.mean(xv * xv, axis=-1, keepdims=True) + EPS)
            xn = xv * r
            weff = nw_ref[...] * (1.0 + sc_ref[...])
            dxn = dh * weff
            gx_ref[...] = dy_ref[...] + r * (dxn - xn * jnp.mean(dxn * xn, axis=-1, keepdims=True))
            dwe_ref[...] += _csum(dh * xn)
            dsh_ref[...] += _csum(dh)

        @pl.when(jnp.logical_and(i == nt - 1, k == nk - 1))
        def _():
            dwe = dwe_ref[...]
            dnw_ref[...] = dwe * (1.0 + sc_ref[...])
            dsc_ref[...] = dwe * nw_ref[...]

    vec = pl.BlockSpec((1, D), lambda i, k: (0, 0))
    row = pl.BlockSpec((tm, D), lambda i, k: (i, 0))
    return pl.pallas_call(
        body, name="dproj_bwd", grid=(nt, nk),
        in_specs=[pl.BlockSpec((tm, tk), lambda i, k: (i, k)), pl.BlockSpec((D, tk), lambda i, k: (0, k)),
                  row, row, vec, vec],
        out_specs=[row, vec, vec, vec],
        out_shape=[jax.ShapeDtypeStruct((t, D), F32), jax.ShapeDtypeStruct((1, D), F32),
                   jax.ShapeDtypeStruct((1, D), F32), jax.ShapeDtypeStruct((1, D), F32)],
        scratch_shapes=[pltpu.VMEM((tm, D), F32), pltpu.VMEM((1, D), F32)],
        compiler_params=_cp(("arbitrary", "arbitrary")),
    )(dproj, wcat, x, dy, norm_w, scale)


def wgrad(at, b, name, bn):
    m, t = at.shape
    n = b.shape[1]
    tk = min(t, 1024)
    bm = min(m, 1024)

    def body(a_ref, b_ref, o_ref):
        part = _dot(a_ref[...], b_ref[...])

        @pl.when(pl.program_id(2) == 0)
        def _():
            o_ref[...] = part

        @pl.when(pl.program_id(2) > 0)
        def _():
            o_ref[...] += part

    return pl.pallas_call(
        body, name=name, grid=(m // bm, n // bn, t // tk),
        in_specs=[pl.BlockSpec((bm, tk), lambda i, j, k: (i, k)), pl.BlockSpec((tk, bn), lambda i, j, k: (k, j))],
        out_specs=pl.BlockSpec((bm, bn), lambda i, j, k: (i, j)),
        out_shape=jax.ShapeDtypeStruct((m, n), F32),
        compiler_params=_cp(("parallel", "parallel", "arbitrary")),
    )(at, b)


SUM_TR = 256


def pair_sum(gpack, core, theirs):
    nh = HALF // SUM_TR

    def body(core_ref, a_ref, b_ref, o_ref, ob_ref):
        s = a_ref[...] + b_ref[...]
        o_ref[...] = s
        ob_ref[...] = s.astype(BF16)

    spec = pl.BlockSpec((1, SUM_TR, 1024), lambda d, i, c: (d, i, 0))
    return pl.pallas_call(
        body, name="pair_sum",
        out_shape=[jax.ShapeDtypeStruct((4, HALF, 1024), F32), jax.ShapeDtypeStruct((4, HALF, 1024), BF16)],
        grid_spec=pltpu.PrefetchScalarGridSpec(
            num_scalar_prefetch=1, grid=(4, nh),
            in_specs=[pl.BlockSpec((1, SUM_TR, 1024), lambda d, i, c: (d, c[0] * nh + i, 0)), spec],
            out_specs=[spec, spec]),
        compiler_params=_cp(("parallel", "parallel")))(core.reshape(1).astype(jnp.int32), gpack, theirs)


def chip_sum(part, chip, others):
    r = part.shape[1]

    def body(chip_ref, a_ref, b_ref, o_ref):
        acc = a_ref[0]
        for k in range(3):
            acc = acc + b_ref[k].astype(F32)
        o_ref[...] = acc

    return pl.pallas_call(
        body, name="chip_sum", out_shape=jax.ShapeDtypeStruct((r, 1024), F32),
        grid_spec=pltpu.PrefetchScalarGridSpec(
            num_scalar_prefetch=1, grid=(r // SUM_TR,),
            in_specs=[pl.BlockSpec((1, SUM_TR, 1024), lambda i, c: (c[0], i, 0)),
                      pl.BlockSpec((3, SUM_TR, 1024), lambda i, c: (0, i, 0))],
            out_specs=pl.BlockSpec((SUM_TR, 1024), lambda i, c: (i, 0))),
        compiler_params=_cp(("parallel",)))(chip.reshape(1).astype(jnp.int32), part, others)


def sum_devices(g):
    r = g.shape[1]

    def body(g_ref, o_ref):
        acc = g_ref[0]
        for d in range(1, 8):
            acc = acc + g_ref[d]
        o_ref[...] = acc

    return pl.pallas_call(body, name="sum_devices", out_shape=jax.ShapeDtypeStruct((r, 1024), F32),
                          compiler_params=_cp())(g)


def adamw(w, g, m, v, name):
    r, c = w.shape
    tr = r
    for cand in (256, 128, 64, 32, 16, 8):
        if r % cand == 0 and r > cand:
            tr = cand
            break

    def body(w_ref, g_ref, m_ref, v_ref, d_ref, nm_ref, nv_ref):
        gv = g_ref[...]
        mn = ADAM_B1 * m_ref[...] + (1.0 - ADAM_B1) * gv
        vn = ADAM_B2 * v_ref[...] + (1.0 - ADAM_B2) * (gv * gv)
        m_hat = mn / (1.0 - ADAM_B1 ** ADAM_STEP)
        v_hat = vn / (1.0 - ADAM_B2 ** ADAM_STEP)
        d_ref[...] = -ADAM_LR * (m_hat / (jnp.sqrt(v_hat) + ADAM_EPS) + ADAM_WD * w_ref[...])
        nm_ref[...] = mn
        nv_ref[...] = vn

    spec = pl.BlockSpec((tr, c), lambda i: (i, 0))
    sd = jax.ShapeDtypeStruct((r, c), F32)
    return pl.pallas_call(body, name=name, grid=(r // tr,), in_specs=[spec] * 4, out_specs=[spec] * 3,
                          out_shape=[sd, sd, sd], compiler_params=_cp(("parallel",)))(w, g, m, v)


ANY = pl.BlockSpec(memory_space=pl.ANY)
VM = pl.BlockSpec(memory_space=pltpu.VMEM)
OTHER_CHIPS = ((1, 0), (0, 1), (1, 1))


def _pos():
    return lax.axis_index("x"), lax.axis_index("y"), lax.axis_index("c")


def _flip(v, bit):
    return 1 - v if bit else v


def _rcopy(src, dst, ssem, rsem, peer):
    return pltpu.make_async_remote_copy(src_ref=src, dst_ref=dst, send_sem=ssem, recv_sem=rsem,
                                        device_id=peer, device_id_type=MESH)


def allgather_small(p, name):
    r = p.shape[0]

    def body(in_ref, out_ref, ssem, rsem, lsem):
        x, y, c = _pos()
        me = 4 * x + 2 * y + c
        loc = pltpu.make_async_copy(in_ref, out_ref.at[me], lsem)
        loc.start()
        sends = []
        peers = []
        for k in range(1, 8):
            px, py, pc = _flip(x, (k >> 2) & 1), _flip(y, (k >> 1) & 1), _flip(c, k & 1)
            peers.append((px, py, pc))
            cp = _rcopy(in_ref, out_ref.at[me], ssem.at[k - 1], rsem.at[k - 1], (px, py, pc))
            cp.start()
            sends.append(cp)
        for k in range(1, 8):
            px, py, pc = peers[k - 1]
            _rcopy(in_ref, out_ref.at[4 * px + 2 * py + pc], ssem.at[k - 1], rsem.at[k - 1], (px, py, pc)).wait_recv()
        for cp in sends:
            cp.wait_send()
        loc.wait()

    return pl.pallas_call(
        body, name=name, out_shape=jax.ShapeDtypeStruct((8, r, 1024), F32),
        in_specs=[VM], out_specs=VM,
        scratch_shapes=[pltpu.SemaphoreType.DMA((7,)), pltpu.SemaphoreType.DMA((7,)), pltpu.SemaphoreType.DMA],
    )(p)


def gather_weights(wpack, mod_sh):
    def body(w_ref, m_ref, wg_ref, mo_ref, ssem, rsem, lsem):
        x, y, c = _pos()
        chip = 2 * x + y
        mine = pl.ds(pl.multiple_of(c * HALF, 16), HALF)
        other = pl.ds(pl.multiple_of((1 - c) * HALF, 16), HALF)
        sib = (x, y, 1 - c)
        loc_m = pltpu.make_async_copy(m_ref, mo_ref.at[chip], lsem)
        loc_m.start()
        sends = []
        for k, (fx, fy) in enumerate(OTHER_CHIPS):
            peer = (_flip(x, fx), _flip(y, fy), c)
            cw = _rcopy(w_ref.at[mine], wg_ref.at[chip, mine], ssem.at[k], rsem.at[k], peer)
            cm = _rcopy(m_ref, mo_ref.at[chip], ssem.at[6 + k], rsem.at[6 + k], peer)
            cw.start()
            cm.start()
            sends += [cw, cm]
        for k, (fx, fy) in enumerate(OTHER_CHIPS):
            px, py = _flip(x, fx), _flip(y, fy)
            got = wg_ref.at[2 * px + py, mine]
            _rcopy(w_ref.at[mine], got, ssem.at[k], rsem.at[k], (px, py, c)).wait_recv()
            fw = _rcopy(got, got, ssem.at[3 + k], rsem.at[3 + k], sib)
            fw.start()
            sends.append(fw)
        for k, (fx, fy) in enumerate(OTHER_CHIPS):
            px, py = _flip(x, fx), _flip(y, fy)
            land = wg_ref.at[2 * px + py, other]
            _rcopy(land, land, ssem.at[3 + k], rsem.at[3 + k], sib).wait_recv()
            _rcopy(m_ref, mo_ref.at[2 * px + py], ssem.at[6 + k], rsem.at[6 + k], (px, py, c)).wait_recv()
        for cp in sends:
            cp.wait_send()
        loc_m.wait()

    return pl.pallas_call(
        body, name="gather_weights",
        out_shape=[jax.ShapeDtypeStruct((4, ROWS, 1024), BF16), jax.ShapeDtypeStruct((4, 8, 768), F32)],
        in_specs=[ANY, VM], out_specs=[ANY, VM],
        scratch_shapes=[pltpu.SemaphoreType.DMA((9,)), pltpu.SemaphoreType.DMA((9,)), pltpu.SemaphoreType.DMA],
    )(wpack, mod_sh)


def pair_exchange(gpack):
    def body(g_ref, r_ref, ssem, rsem):
        x, y, c = _pos()
        other = pl.ds(pl.multiple_of((1 - c) * HALF, 8), HALF)
        cp = _rcopy(g_ref.at[:, other, :], r_ref, ssem, rsem, (x, y, 1 - c))
        cp.start()
        cp.wait()

    return pl.pallas_call(
        body, name="pair_exchange", out_shape=jax.ShapeDtypeStruct((4, HALF, 1024), F32),
        in_specs=[ANY], out_specs=ANY,
        scratch_shapes=[pltpu.SemaphoreType.DMA, pltpu.SemaphoreType.DMA],
    )(gpack)


def chip_exchange(part_b):
    def body(pb_ref, r_ref, ssem, rsem):
        x, y, c = _pos()
        chip = 2 * x + y
        sends = []
        for k, (fx, fy) in enumerate(OTHER_CHIPS):
            px, py = _flip(x, fx), _flip(y, fy)
            cp = _rcopy(pb_ref.at[2 * px + py], r_ref.at[k], ssem.at[k], rsem.at[k], (px, py, c))
            cp.start()
            sends.append(cp)
        for k, (fx, fy) in enumerate(OTHER_CHIPS):
            px, py = _flip(x, fx), _flip(y, fy)
            _rcopy(pb_ref.at[chip], r_ref.at[k], ssem.at[k], rsem.at[k], (px, py, c)).wait_recv()
        for cp in sends:
            cp.wait_send()

    return pl.pallas_call(
        body, name="chip_exchange", out_shape=jax.ShapeDtypeStruct((3, HALF, 1024), BF16),
        in_specs=[ANY], out_specs=ANY,
        scratch_shapes=[pltpu.SemaphoreType.DMA((3,)), pltpu.SemaphoreType.DMA((3,))],
    )(part_b)


def pair_swap(red):
    def body(r_ref, o_ref, ssem, rsem):
        x, y, c = _pos()
        cp = _rcopy(r_ref, o_ref, ssem, rsem, (x, y, 1 - c))
        cp.start()
        cp.wait()

    return pl.pallas_call(
        body, name="pair_swap", out_shape=jax.ShapeDtypeStruct((HALF, 1024), F32),
        in_specs=[ANY], out_specs=ANY,
        scratch_shapes=[pltpu.SemaphoreType.DMA, pltpu.SemaphoreType.DMA],
    )(red)


def _row(v, width=1024):
    v = v.reshape(-1)
    n = -(-v.shape[0] // width) * width
    return jnp.pad(v, (0, n - v.shape[0])).reshape(-1, width)


def _slots(vs):
    row = [jnp.pad(v.reshape(-1), (0, 128 - v.size)) for v in vs]
    row += [jnp.zeros((128,), F32)] * (8 - len(row))
    return jnp.concatenate(row).reshape(1, 1024)


def _pack_small(b_ada, norm_w, conv_b, ssm_norm_w, q_norm_w, k_norm_w, sinks, dt_bias, a_log, d_skip, rel_bias,
                extra=None):
    misc = [q_norm_w, k_norm_w, sinks, dt_bias, a_log, d_skip] + ([] if extra is None else [extra])
    rows = [_row(b_ada), _row(norm_w), _row(conv_b), _row(ssm_norm_w), _slots(misc), _row(rel_bias)]
    rows.append(jnp.zeros((5, 1024), F32))
    return jnp.concatenate(rows, axis=0)


def _unpack_small(p):
    misc = p[9]
    return dict(b_ada=p[0:3].reshape(1, 3072), norm_w=p[3:4], conv_b=p[4:7].reshape(1, 3072),
                ssm_norm_w=p[7:9].reshape(1, 2048), q_norm_w=misc[None, 0:64], k_norm_w=misc[None, 128:192],
                sinks=misc[None, 256:272], dt_bias=misc[None, 384:416], a_log=misc[None, 512:544],
                d_skip=misc[None, 640:672], rel_bias=p[10, :512].reshape(32, 16), extra=misc[768])


SMALL = ("b_ada", "norm_w", "conv_b", "ssm_norm_w", "q_norm_w", "k_norm_w", "sinks", "dt_bias", "a_log", "d_skip",
         "rel_bias")
WEIGHTS = ("w_ada", "b_ada", "norm_w", "w_in", "q_norm_w", "k_norm_w", "rel_bias", "sinks", "conv_w", "conv_b",
           "dt_bias", "a_log", "d_skip", "ssm_norm_w", "w_attn_proj", "w_ssm_proj", "w_out")
IN_COLS = ((0, 1024, C_Q), (1024, 256, C_K), (1280, 256, C_V), (1536, 1024, C_ZA), (2560, 2048, C_ZM),
           (4608, 3072, C_XBC), (7680, 32, C_DT), (7712, 1024, C_GA), (8736, 1024, C_GB))


def _to_cat(w_full):
    by_new = sorted(IN_COLS, key=lambda e: e[2])
    parts, pos = [], 0
    for o, n, cnew in by_new:
        assert cnew == pos
        parts.append(w_full[:, o:o + n])
        pos += n
    parts.append(jnp.zeros((w_full.shape[0], NP - pos), w_full.dtype))
    return jnp.concatenate(parts, axis=1)


def _from_cat(w_cat):
    return jnp.concatenate([w_cat[:, cnew:cnew + n] for o, n, cnew in IN_COLS], axis=1)


def kernel(x, c, w_ada, b_ada, norm_w, w_in, q_norm_w, k_norm_w, rel_bias, sinks, conv_w, conv_b, dt_bias, a_log, d_skip, ssm_norm_w, w_attn_proj, w_ssm_proj, w_out, loss_target, m_w_ada, m_b_ada, m_norm_w, m_w_in, m_q_norm_w, m_k_norm_w, m_rel_bias, m_sinks, m_conv_w, m_conv_b, m_dt_bias, m_a_log, m_d_skip, m_ssm_norm_w, m_w_attn_proj, m_w_ssm_proj, m_w_out, v_w_ada, v_b_ada, v_norm_w, v_w_in, v_q_norm_w, v_k_norm_w, v_rel_bias, v_sinks, v_conv_w, v_conv_b, v_dt_bias, v_a_log, v_d_skip, v_ssm_norm_w, v_w_attn_proj, v_w_ssm_proj, v_w_out):
    args = dict(locals())
    xi, yi, ci = lax.axis_index("x"), lax.axis_index("y"), lax.axis_index("c")
    chip = 2 * xi + yi
    me = 4 * xi + 2 * yi + ci
    x2 = x[0]
    tgt = loss_target[0]

    pay = jnp.concatenate([c, conv_w[0].reshape(3, 1024), jnp.zeros((4, 1024), F32)], axis=0)
    g0 = allgather_small(pay, "gather_cond")
    c_all = g0[:, 0, :]
    conv_w_full = g0[0::2, 1:4, :].reshape(4, CONV_K, 768).transpose(1, 0, 2).reshape(CONV_K, XBC)

    b_ada_sh = lax.dynamic_slice(b_ada, (0, chip * 768), (1, 768))
    mod_sh = ada_mod(c_all, w_ada[0], b_ada_sh)

    wpack = jnp.concatenate([w_in[0].astype(BF16).reshape(R_IN, 1024), w_attn_proj[0].astype(BF16),
                             w_ssm_proj[0].astype(BF16), w_out[0].astype(BF16),
                             jnp.zeros((ROWS - R_IN - R_AT - R_SS - R_OU, 1024), BF16)], axis=0)
    wg, modg = gather_weights(wpack, mod_sh)
    wg = lax.dynamic_update_slice(wg, wpack[None], (chip, 0, 0))
    mod = lax.dynamic_slice(modg, (0, me, 0), (4, 1, 768)).reshape(1, 3 * D)
    shift, scale, gate = mod[:, :D], mod[:, D:2 * D], mod[:, 2 * D:]
    w_in_full = wg[:, :R_IN].reshape(4, D, IN_W // 4).transpose(1, 0, 2).reshape(D, IN_W)
    wcat = _to_cat(w_in_full)
    o1 = R_IN
    w_at = wg[:, o1:o1 + R_AT].reshape(D, D)
    w_ss = wg[:, o1 + R_AT:o1 + R_AT + R_SS].reshape(SSM_W, D)
    w_ou = wg[:, o1 + R_AT + R_SS:o1 + R_AT + R_SS + R_OU].reshape(D, D)

    pad128 = lambda v: jnp.pad(v, ((0, 0), (0, 128 - v.shape[1])))
    dtb_p, alog_p, dsk_p = pad128(dt_bias), pad128(a_log), pad128(d_skip)
    bucket = _bucket_table()

    proj, h_t = norm_proj(x2, norm_w, scale, shift, wcat)
    biasm = bias_expand(rel_bias, sinks, bucket)
    ao = attn_fwd(proj, biasm, q_norm_w, k_norm_w)
    act = conv_fwd(proj, conv_w_full, conv_b)
    yss, sprev = ssd_fwd(act, proj, dtb_p, alog_p, dsk_p)

    (loss_p, dy, dao, dza, dga, dgb, dyss, dzm, ua_t, yn_t, mg_t, dya, dyb, dout, dgate, dssm_nw) = tail(
        proj, ao, yss, x2, tgt, gate, ssm_norm_w, w_at, w_ss, w_ou)

    dq, dk, dv, dqw, dkw, dacc = attn_bwd(proj, dao, biasm, q_norm_w, k_norm_w)
    dbias = bias_reduce(dacc, bucket)
    drb = dbias[:, :NBUCKET].T
    dsk = dbias[:, NBUCKET].reshape(1, HQ)
    dact, ddt, ddtb, dalog, ddskip = ssd_bwd(act, proj, dyss, sprev, dtb_p, alog_p, dsk_p)
    dxbc, dconv_w, dconv_b = conv_bwd(proj, dact, conv_w_full, conv_b)

    t = x2.shape[0]
    dproj = jnp.concatenate([dq, dza, dga, dgb, dzm, dxbc, dk, dv, ddt, jnp.zeros((t, NP - C_DT - 128), BF16)], axis=1)
    grad_x, dnorm_w, dscale, dshift = dproj_bwd(dproj, wcat, x2, dy, norm_w, scale)
    dwcat = wgrad(h_t, dproj, "dw_in", TN)
    dw_at = wgrad(ua_t, dya, "dw_attn", 512)
    dw_ss = wgrad(yn_t, dyb, "dw_ssm", 512)
    dw_ou = wgrad(mg_t, dout, "dw_out", 512)

    g_in = _from_cat(dwcat).reshape(D, 4, IN_W // 4).transpose(1, 0, 2).reshape(4, R_IN, 1024)
    gpack = jnp.concatenate([g_in, dw_at.reshape(4, R_AT, 1024), dw_ss.reshape(4, R_SS, 1024),
                             dw_ou.reshape(4, R_OU, 1024),
                             jnp.zeros((4, ROWS - R_IN - R_AT - R_SS - R_OU, 1024), F32)], axis=1)
    from_sib = pair_exchange(gpack)
    part, part_b = pair_sum(gpack, ci, from_sib)
    red = chip_sum(part, chip, chip_exchange(part_b))
    recv = pair_swap(red)
    g_shard = jnp.concatenate([jnp.where(ci == 0, red, recv), jnp.where(ci == 0, recv, red)], axis=0)

    dmod = jnp.concatenate([dshift, dscale, dgate], axis=1)
    gsmall = jnp.concatenate([
        _pack_small(dmod, dnorm_w, dconv_b, dssm_nw, dqw, dkw, dsk[:, :HQ], ddtb[:, :SH], dalog[:, :SH],
                    ddskip[:, :SH], drb, extra=loss_p[:, :1]),
        dconv_w.reshape(12, 1024), jnp.zeros((4, 1024), F32)], axis=0)
    gall = allgather_small(gsmall, "gather_small_grads")
    ssum = sum_devices(gall)
    gs = _unpack_small(ssum[:16])
    loss = gs["extra"]
    dconv_w_sh = lax.dynamic_slice(ssum[16:28].reshape(CONV_K, XBC), (0, chip * 768), (CONV_K, 768))
    dmod_all = gall[:, 0:3, :].reshape(8, 3 * D)
    dw_ada = ada_grad(c_all, lax.dynamic_slice(dmod_all, (0, chip * 768), (8, 768)))

    grads = dict(gs)
    grads["w_ada"] = dw_ada
    grads["w_in"] = g_shard[:R_IN].reshape(D, IN_W // 4)
    grads["w_attn_proj"] = g_shard[o1:o1 + R_AT]
    grads["w_ssm_proj"] = g_shard[o1 + R_AT:o1 + R_AT + R_SS]
    grads["w_out"] = g_shard[o1 + R_AT + R_SS:o1 + R_AT + R_SS + R_OU]
    grads["conv_w"] = dconv_w_sh

    delta, new_m, new_v = {}, {}, {}
    for n in ("w_ada", "w_in", "conv_w", "w_attn_proj", "w_ssm_proj", "w_out"):
        delta[n], new_m[n], new_v[n] = adamw(args[n][0], grads[n], args["m_" + n][0], args["v_" + n][0], "adamw_" + n)
    ws = _pack_small(*[args[n] for n in SMALL])
    ms = _pack_small(*[args["m_" + n] for n in SMALL])
    vs = _pack_small(*[args["v_" + n] for n in SMALL])
    d_s, m_s, v_s = adamw(ws, ssum[:16], ms, vs, "adamw_small")
    d_s, m_s, v_s = _unpack_small(d_s), _unpack_small(m_s), _unpack_small(v_s)
    for n in SMALL:
        delta[n], new_m[n], new_v[n] = d_s[n], m_s[n], v_s[n]

    def shaped(n, a):
        return a.reshape(args[n].shape)

    outs = [loss, grad_x[None]]
    for table in (grads, delta, new_m, new_v):
        outs += [shaped(n, table[n]) for n in WEIGHTS]
    return tuple(outs)
```

```python
import functools
import math

import numpy as np
import jax
import jax.numpy as jnp
from jax import lax
from jax.experimental import pallas as pl
from jax.experimental.pallas import tpu as pltpu

F32 = jnp.float32
BF16 = jnp.bfloat16
MESH = pl.DeviceIdType.MESH

D = 1024
HQ, HKV, GRP, DH = 16, 4, 4, 64
BLK = 128
NBUCKET, MAXDIST = 32, 128
SSM_W, SH, SG, SR, SP, SN = 2048, 32, 4, 8, 64, 128
CONV_K = 4
XBC = SSM_W + 2 * SG * SN
IN_W = 9760
EPS = 1e-6
NEG = -1e30
SCALE = DH ** -0.5

C_Q, C_ZA, C_GA, C_GB, C_ZM, C_XBC, C_K, C_V, C_DT = 0, 1024, 2048, 3072, 4096, 6144, 9216, 9472, 9728
NP = 9984
TN = 768

SH_IN = IN_W // 4
R_AT, R_SS, R_OU = 256, 512, 256
HROWS = D // 2

ADAM_LR, ADAM_B1, ADAM_B2, ADAM_EPS, ADAM_WD, ADAM_STEP = 0.001, 0.9, 0.999, 1e-08, 0.01, 10

VMEM_LIMIT = 56 * 1024 * 1024


def _cp(sem=None):
    if sem is None:
        return pltpu.CompilerParams(vmem_limit_bytes=VMEM_LIMIT)
    return pltpu.CompilerParams(dimension_semantics=sem, vmem_limit_bytes=VMEM_LIMIT)


def _sig(x):
    return 1.0 / (1.0 + jnp.exp(-x))


def _dot(a, b):
    return jnp.dot(a, b, preferred_element_type=F32)


def _dot_nt(a, b):
    return lax.dot_general(a, b, (((1,), (1,)), ((), ())), preferred_element_type=F32)


def _dot_tn(a, b):
    return lax.dot_general(a, b, (((0,), (0,)), ((), ())), preferred_element_type=F32)


def _rsum(x):
    return jnp.sum(x, axis=-1, keepdims=True)


def _csum(x):
    return jnp.sum(x, axis=0, keepdims=True)


def _asum(x):
    return _csum(_rsum(x))


def _full(shape):
    nd = len(shape)
    return pl.BlockSpec(shape, lambda *_: (0,) * nd)


def ada_mod(c_all, w_ada_sh, b_ada_sh):
    def body(c_ref, w_ref, b_ref, o_ref):
        cv = c_ref[...]
        s = cv * _sig(cv)
        o_ref[...] = jnp.dot(s, w_ref[...], preferred_element_type=F32,
                             precision=lax.Precision.HIGHEST) + b_ref[...]

    n = w_ada_sh.shape[1]
    return pl.pallas_call(body, name="ada_mod", out_shape=jax.ShapeDtypeStruct((8, n), F32),
                          compiler_params=_cp())(c_all, w_ada_sh, b_ada_sh)


def ada_grad(c_all, dmod_sh):
    def body(c_ref, d_ref, o_ref):
        cv = c_ref[...]
        s = cv * _sig(cv)
        o_ref[...] = lax.dot_general(s, d_ref[...], (((0,), (0,)), ((), ())), preferred_element_type=F32,
                                     precision=lax.Precision.HIGHEST)

    n = dmod_sh.shape[1]
    return pl.pallas_call(body, name="ada_grad", out_shape=jax.ShapeDtypeStruct((D, n), F32),
                          compiler_params=_cp())(c_all, dmod_sh)


def norm_proj(x, norm_w, scale, shift, wcat):
    t = x.shape[0]
    tm = min(t, 1024)

    def body(x_ref, nw_ref, sc_ref, sh_ref, w_ref, p_ref, ht_ref, hs):
        @pl.when(pl.program_id(1) == 0)
        def _():
            xv = x_ref[...]
            r = lax.rsqrt(jnp.mean(xv * xv, axis=-1, keepdims=True) + EPS)
            h = (xv * r) * nw_ref[...]
            h = h * (1.0 + sc_ref[...]) + sh_ref[...]
            hs[...] = h.astype(BF16)
            ht_ref[...] = h.T.astype(BF16)

        p_ref[...] = _dot(hs[...], w_ref[...])

    vec = pl.BlockSpec((1, D), lambda i, j: (0, 0))
    return pl.pallas_call(
        body, name="norm_proj", grid=(t // tm, NP // TN),
        in_specs=[pl.BlockSpec((tm, D), lambda i, j: (i, 0)), vec, vec, vec,
                  pl.BlockSpec((D, TN), lambda i, j: (0, j))],
        out_specs=[pl.BlockSpec((tm, TN), lambda i, j: (i, j)), pl.BlockSpec((D, tm), lambda i, j: (0, i))],
        out_shape=[jax.ShapeDtypeStruct((t, NP), F32), jax.ShapeDtypeStruct((D, t), BF16)],
        scratch_shapes=[pltpu.VMEM((tm, D), BF16)],
        compiler_params=_cp(("parallel", "arbitrary")),
    )(x, norm_w, scale, shift, wcat)


def _bucket_table():
    qi = jnp.arange(BLK)[:, None]
    kj = jnp.arange(2 * BLK)[None, :]
    dist = qi + BLK - kj
    n = jnp.maximum(dist, 0)
    max_exact = NBUCKET // 2
    nf = jnp.maximum(n, 1).astype(F32)
    large = max_exact + (jnp.log(nf / max_exact) / math.log(MAXDIST / max_exact)
                         * (NBUCKET - max_exact)).astype(jnp.int32)
    large = jnp.minimum(large, NBUCKET - 1)
    bucket = jnp.where(n < max_exact, n, large).astype(jnp.int32)
    valid = (dist >= 0) & (dist < BLK)
    return jnp.where(valid, bucket, -1)


def bias_expand(rel_bias, sinks, bucket):
    def body(rb_ref, sk_ref, bk_ref, o_ref):
        hd = pl.program_id(0)
        bk = bk_ref[...]
        col = lax.broadcasted_iota(jnp.int32, (BLK, 2 * BLK), 1)

        def step(b, acc):
            return jnp.where(bk == b, rb_ref[b, hd], acc)

        acc = lax.fori_loop(0, NBUCKET, step, jnp.full((BLK, 2 * BLK), NEG, F32))
        acc = jnp.where(col == 0, sk_ref[0, hd], acc)
        o_ref[1, 0] = acc
        o_ref[0, 0] = jnp.where(jnp.logical_and(col > 0, col < BLK), NEG, acc)

    smem = pl.BlockSpec(memory_space=pltpu.SMEM)
    return pl.pallas_call(
        body, name="bias_expand", grid=(HQ,),
        in_specs=[smem, smem, _full((BLK, 2 * BLK))],
        out_specs=pl.BlockSpec((2, 1, BLK, 2 * BLK), lambda h: (0, h, 0, 0)),
        out_shape=jax.ShapeDtypeStruct((2, HQ, BLK, 2 * BLK), F32),
        compiler_params=_cp(("arbitrary",)),
    )(rel_bias, sinks, bucket)


def bias_reduce(dacc, bucket):
    col = jnp.arange(BLK * 2 * BLK, dtype=jnp.int32) % (2 * BLK)
    lane = jnp.arange(128, dtype=jnp.int32)[None, :]
    member = (bucket.reshape(-1)[:, None] == lane) | ((col[:, None] == 0) & (lane == NBUCKET))

    def body(d_ref, m_ref, o_ref):
        mm = m_ref[...]
        o_ref[...] = sum(_dot(part, mm) for part in _split3(d_ref[...]))

    return pl.pallas_call(body, name="bias_reduce", out_shape=jax.ShapeDtypeStruct((HQ, 128), F32),
                          compiler_params=_cp())(dacc.reshape(HQ, BLK * 2 * BLK), member.astype(BF16))


GQ = GRP * BLK


def _stack_heads(x, nh):
    return jnp.concatenate([x[:, DH * h:DH * (h + 1)] for h in range(nh)], axis=0)


def _unstack(xs, nh):
    rows = xs.shape[0] // nh
    return jnp.concatenate([xs[rows * h:rows * (h + 1)] for h in range(nh)], axis=1)


def _rms(x):
    return lax.rsqrt(jnp.mean(x * x, axis=-1, keepdims=True) + EPS)


def _stack_q(q, qw):
    qs = _stack_heads(q, HQ)
    r = _rms(qs)
    qhat = qs * r
    return qhat * qw, qhat, r


def _band_first(shape):
    return (lax.broadcasted_iota(jnp.int32, shape, 0) & (2 * BLK - 1)) == 0


def _stack_kv(kp, kc, vp, vc, kw):
    ks = _stack_heads(jnp.concatenate([kp, kc], axis=0), HKV)
    r = _rms(ks)
    khat = ks * r
    first = _band_first(ks.shape)
    kn = jnp.where(first, 0.0, khat * kw)
    v2 = jnp.where(first, 0.0, _stack_heads(jnp.concatenate([vp, vc], axis=0), HKV)).astype(BF16)
    return kn, khat, r, v2


def _softmax_rows(s):
    p = jnp.exp(s - jnp.max(s, axis=-1, keepdims=True))
    return p * (1.0 / _rsum(p))


def attn_fwd(proj, biasm, q_norm_w, k_norm_w):
    t = proj.shape[0]
    nb = t // BLK

    def body(q_ref, kc_ref, kp_ref, vc_ref, vp_ref, bm_ref, qw_ref, kw_ref, o_ref):
        qn = _stack_q(q_ref[...], qw_ref[...])[0].astype(BF16)
        kn, _, _, v2 = _stack_kv(kp_ref[...], kc_ref[...], vp_ref[...], vc_ref[...], kw_ref[...])
        knb = kn.astype(BF16)
        s = jnp.concatenate([_dot_nt(qn[GQ * j:GQ * (j + 1)], knb[2 * BLK * j:2 * BLK * (j + 1)])
                             for j in range(HKV)], axis=0)
        pr = _softmax_rows(s * SCALE + bm_ref[0].reshape(HQ * BLK, 2 * BLK)).astype(BF16)
        o = jnp.concatenate([_dot(pr[GQ * j:GQ * (j + 1)], v2[2 * BLK * j:2 * BLK * (j + 1)])
                             for j in range(HKV)], axis=0)
        o_ref[...] = _unstack(o, HQ)

    kblk, vblk = C_K // 256, C_V // 256
    prev = lambda n: jnp.maximum(n - 1, 0)
    return pl.pallas_call(
        body, name="attn_fwd", grid=(nb,),
        in_specs=[pl.BlockSpec((BLK, D), lambda n: (n, 0)),
                  pl.BlockSpec((BLK, 256), lambda n: (n, kblk)),
                  pl.BlockSpec((BLK, 256), lambda n: (prev(n), kblk)),
                  pl.BlockSpec((BLK, 256), lambda n: (n, vblk)),
                  pl.BlockSpec((BLK, 256), lambda n: (prev(n), vblk)),
                  pl.BlockSpec((1, HQ, BLK, 2 * BLK), lambda n: (jnp.minimum(n, 1), 0, 0, 0)),
                  _full((1, DH)), _full((1, DH))],
        out_specs=pl.BlockSpec((BLK, D), lambda n: (n, 0)),
        out_shape=jax.ShapeDtypeStruct((t, D), F32),
        compiler_params=_cp(("parallel",)),
    )(proj, proj, proj, proj, proj, biasm, q_norm_w, k_norm_w)


def attn_bwd(proj, dao, biasm, q_norm_w, k_norm_w):
    t = proj.shape[0]
    nb = t // BLK
    kb = 2 * BLK

    def body(q_ref, kc_ref, kp_ref, vc_ref, vp_ref, do_ref, bm_ref, qw_ref, kw_ref,
             dq_ref, dk_ref, dv_ref, dqw_ref, dkw_ref, dacc_ref, ck, cv, pk, pv, nk, nv):
        n = pl.program_id(0)

        @pl.when(n == 0)
        def _():
            for ref in (dqw_ref, dkw_ref, dacc_ref, ck, cv):
                ref[...] = jnp.zeros_like(ref)

        qw = qw_ref[...]
        kw = kw_ref[...]
        kn, khat, rk, v2 = _stack_kv(kp_ref[...], kc_ref[...], vp_ref[...], vc_ref[...], kw)
        grp = lambda a, j: a[GQ * j:GQ * (j + 1)]
        band = lambda a, j: a[kb * j:kb * (j + 1)]

        @pl.when(n < nb)
        def _():
            qn, qhat, rq = _stack_q(q_ref[...], qw)
            qnb = qn.astype(BF16)
            knb = kn.astype(BF16)
            dos = _stack_heads(do_ref[...], HQ).astype(BF16)
            s = jnp.concatenate([_dot_nt(grp(qnb, j), band(knb, j)) for j in range(HKV)], axis=0)
            pr = _softmax_rows(s * SCALE + bm_ref[0].reshape(HQ * BLK, kb))
            dp = jnp.concatenate([_dot_nt(grp(dos, j), band(v2, j)) for j in range(HKV)], axis=0)
            ds = pr * (dp - _rsum(pr * dp))
            dacc_ref[...] += ds.reshape(HQ, BLK, kb)
            dsb = ds.astype(BF16)
            prb = pr.astype(BF16)
            dqn = jnp.concatenate([_dot(grp(dsb, j), band(knb, j)) for j in range(HKV)], axis=0) * SCALE
            dqhat = dqn * qw
            dq = rq * (dqhat - qhat * jnp.mean(dqhat * qhat, axis=-1, keepdims=True))
            dq_ref[...] = _unstack(dq, HQ).astype(BF16)
            dqw_ref[...] += _csum(dqn * qhat)
            first = _band_first((kb, DH))
            for j in range(HKV):
                rows = slice(BLK * j, BLK * (j + 1))
                dkn = jnp.where(first, 0.0, _dot_tn(grp(dsb, j), grp(qnb, j)) * SCALE)
                dvj = jnp.where(first, 0.0, _dot_tn(grp(prb, j), grp(dos, j)))
                pk[rows, :] = dkn[:BLK]
                nk[rows, :] = dkn[BLK:]
                pv[rows, :] = dvj[:BLK]
                nv[rows, :] = dvj[BLK:]

        @pl.when(n == nb)
        def _():
            for ref in (pk, pv, nk, nv):
                ref[...] = jnp.zeros_like(ref)

        khp = jnp.concatenate([khat[kb * j:kb * j + BLK] for j in range(HKV)], axis=0)
        rkp = jnp.concatenate([rk[kb * j:kb * j + BLK] for j in range(HKV)], axis=0)
        dkn = ck[...] + pk[...]
        dkhat = dkn * kw
        dk = rkp * (dkhat - khp * jnp.mean(dkhat * khp, axis=-1, keepdims=True))
        dk_ref[...] = _unstack(dk, HKV).astype(BF16)
        dkw_ref[...] += _csum(dkn * khp)
        dv_ref[...] = _unstack(cv[...] + pv[...], HKV).astype(BF16)
        ck[...] = nk[...]
        cv[...] = nv[...]

    kblk, vblk = C_K // 256, C_V // 256
    cur = lambda n: jnp.minimum(n, nb - 1)
    prev = lambda n: jnp.maximum(n - 1, 0)
    carry = pltpu.VMEM((HKV * BLK, DH), F32)
    return pl.pallas_call(
        body, name="attn_bwd", grid=(nb + 1,),
        in_specs=[pl.BlockSpec((BLK, D), lambda n: (cur(n), 0)),
                  pl.BlockSpec((BLK, 256), lambda n: (cur(n), kblk)), pl.BlockSpec((BLK, 256), lambda n: (prev(n), kblk)),
                  pl.BlockSpec((BLK, 256), lambda n: (cur(n), vblk)), pl.BlockSpec((BLK, 256), lambda n: (prev(n), vblk)),
                  pl.BlockSpec((BLK, D), lambda n: (cur(n), 0)),
                  pl.BlockSpec((1, HQ, BLK, kb), lambda n: (jnp.minimum(n, 1), 0, 0, 0)),
                  _full((1, DH)), _full((1, DH))],
        out_specs=[pl.BlockSpec((BLK, D), lambda n: (cur(n), 0)),
                   pl.BlockSpec((BLK, 256), lambda n: (prev(n), 0)), pl.BlockSpec((BLK, 256), lambda n: (prev(n), 0)),
                   _full((1, DH)), _full((1, DH)), _full((HQ, BLK, kb))],
        out_shape=[jax.ShapeDtypeStruct((t, D), BF16), jax.ShapeDtypeStruct((t, 256), BF16),
                   jax.ShapeDtypeStruct((t, 256), BF16), jax.ShapeDtypeStruct((1, DH), F32),
                   jax.ShapeDtypeStruct((1, DH), F32), jax.ShapeDtypeStruct((HQ, BLK, kb), F32)],
        scratch_shapes=[carry] * 6,
        compiler_params=_cp(("arbitrary",)),
    )(proj, proj, proj, proj, proj, dao, biasm, q_norm_w, k_norm_w)


CONV_TM = 256


def conv_fwd(proj, conv_w, conv_b):
    t = proj.shape[0]
    tm = min(t, CONV_TM)
    cblk = C_XBC // XBC

    def body(x_ref, xp_ref, w_ref, b_ref, o_ref, xe):
        i = pl.program_id(0)
        xe[0:8, :] = jnp.where(i == 0, 0.0, xp_ref[...])
        xe[8:8 + tm, :] = x_ref[...]
        acc = jnp.broadcast_to(b_ref[...], (tm, XBC))
        for j in range(CONV_K):
            acc = acc + w_ref[j:j + 1, :] * xe[5 + j:5 + j + tm, :]
        o_ref[...] = acc * _sig(acc)

    return pl.pallas_call(
        body, name="conv_fwd", grid=(t // tm,),
        in_specs=[pl.BlockSpec((tm, XBC), lambda i: (i, cblk)),
                  pl.BlockSpec((8, XBC), lambda i: (jnp.maximum(i * (tm // 8) - 1, 0), cblk)),
                  _full((CONV_K, XBC)), _full((1, XBC))],
        out_specs=pl.BlockSpec((tm, XBC), lambda i: (i, 0)),
        out_shape=jax.ShapeDtypeStruct((t, XBC), F32),
        scratch_shapes=[pltpu.VMEM((tm + 8, XBC), F32)],
        compiler_params=_cp(("parallel",)),
    )(proj, proj, conv_w, conv_b)


def conv_bwd(proj, dact, conv_w, conv_b):
    t = proj.shape[0]
    tm = min(t, CONV_TM)
    nt = t // tm
    cblk = C_XBC // XBC

    def body(x_ref, xp_ref, xn_ref, d_ref, dn_ref, w_ref, b_ref, dx_ref, dw_ref, db_ref, xe, de):
        i = pl.program_id(0)

        @pl.when(i == 0)
        def _():
            dw_ref[...] = jnp.zeros_like(dw_ref)
            db_ref[...] = jnp.zeros_like(db_ref)

        xe[0:8, :] = jnp.where(i == 0, 0.0, xp_ref[...])
        xe[8:8 + tm, :] = x_ref[...]
        xe[8 + tm:16 + tm, :] = xn_ref[...]
        pre = jnp.broadcast_to(b_ref[...], (tm + 8, XBC))
        for j in range(CONV_K):
            pre = pre + w_ref[j:j + 1, :] * xe[5 + j:5 + j + tm + 8, :]
        sg = _sig(pre)
        dsilu = sg * (1.0 + pre * (1.0 - sg))
        dpre_c = d_ref[...] * dsilu[0:tm]
        dpre_n = jnp.where(i == nt - 1, 0.0, dn_ref[...] * dsilu[tm:tm + 8])
        de[0:tm, :] = dpre_c
        de[tm:tm + 8, :] = dpre_n
        dx = jnp.zeros((tm, XBC), F32)
        for j in range(CONV_K):
            dx = dx + w_ref[j:j + 1, :] * de[3 - j:3 - j + tm, :]
            dw_ref[j:j + 1, :] += _csum(dpre_c * xe[5 + j:5 + j + tm, :])
        dx_ref[...] = dx.astype(BF16)
        db_ref[...] += _csum(dpre_c)

    r8 = tm // 8
    return pl.pallas_call(
        body, name="conv_bwd", grid=(nt,),
        in_specs=[pl.BlockSpec((tm, XBC), lambda i: (i, cblk)),
                  pl.BlockSpec((8, XBC), lambda i: (jnp.maximum(i * r8 - 1, 0), cblk)),
                  pl.BlockSpec((8, XBC), lambda i: (jnp.minimum((i + 1) * r8, nt * r8 - 1), cblk)),
                  pl.BlockSpec((tm, XBC), lambda i: (i, 0)),
                  pl.BlockSpec((8, XBC), lambda i: (jnp.minimum((i + 1) * r8, nt * r8 - 1), 0)),
                  _full((CONV_K, XBC)), _full((1, XBC))],
        out_specs=[pl.BlockSpec((tm, XBC), lambda i: (i, 0)), _full((CONV_K, XBC)), _full((1, XBC))],
        out_shape=[jax.ShapeDtypeStruct((t, XBC), BF16), jax.ShapeDtypeStruct((CONV_K, XBC), F32),
                   jax.ShapeDtypeStruct((1, XBC), F32)],
        scratch_shapes=[pltpu.VMEM((tm + 16, XBC), F32), pltpu.VMEM((tm + 8, XBC), F32)],
        compiler_params=_cp(("arbitrary",)),
    )(proj, proj, proj, dact, dact, conv_w, conv_b)


def _split3(x):
    h = x.astype(BF16)
    r = x - h.astype(F32)
    m = r.astype(BF16)
    lo = (r - m.astype(F32)).astype(BF16)
    return h, m, lo


def _tri_mm(tri, x):
    h, m, lo = _split3(x)
    return _dot(tri, h) + _dot(tri, m) + _dot(tri, lo)


def _softplus(x):
    return jnp.maximum(x, 0.0) + jnp.log1p(jnp.exp(-jnp.abs(x)))


def _chunk_decays(dt_raw, dtb, alog):
    dtv = _softplus(dt_raw + dtb)
    a = -jnp.exp(alog)
    ri = lax.broadcasted_iota(jnp.int32, (BLK, BLK), 0)
    ci = lax.broadcasted_iota(jnp.int32, (BLK, BLK), 1)
    causal = ri >= ci
    acum = _tri_mm(causal.astype(BF16), dtv * a)
    return dtv, a, causal, acum, acum.T


NPAIR = SH // 2


def _pairs(x):
    return jnp.stack([x[:, 128 * k:128 * (k + 1)] for k in range(NPAIR)])


def _unpairs(x3):
    return jnp.concatenate([x3[k] for k in range(NPAIR)], axis=1)


def _per_head_cols(m):
    return jnp.stack([jnp.broadcast_to(m[:, h:h + 1], m.shape) for h in range(SH)])


def _pair_lanes(t):
    r = t.reshape(NPAIR, 2, t.shape[1], 128)
    lo = lax.broadcasted_iota(jnp.int32, (1, t.shape[1], 128), 2) < SP
    return jnp.where(lo, r[:, 0], r[:, 1])


class _Chunk:
    pass


def _chunk_common(dt_raw, dtb, alog, dskip):
    cm = _Chunk()
    cm.dtv, cm.a, cm.causal, acum, acum_t = _chunk_decays(dt_raw, dtb, alog)
    cm.acol = _per_head_cols(acum)
    cm.arow = jnp.stack([acum_t[h:h + 1, :] for h in range(SH)])
    cm.lam = jnp.exp(jnp.where(cm.causal[None], cm.acol - cm.arow, NEG))
    apl = _pair_lanes(cm.acol)
    alast = apl[:, BLK - 1:BLK, :]
    cm.dpl = _pair_lanes(_per_head_cols(cm.dtv))
    cm.eapl = jnp.exp(apl)
    cm.epl = jnp.exp(alast - apl)
    cm.cdpl = jnp.exp(alast)
    cm.dskpl = _pair_lanes(_per_head_cols(dskip))
    cm.lo = lax.broadcasted_iota(jnp.int32, (1, BLK, 128), 2) < SP
    return cm


def ssd_fwd(act, proj, dtb_p, alog_p, dsk_p):
    t = act.shape[0]
    nc = t // BLK

    def body(xs_ref, b_ref, c_ref, dt_ref, dtb_ref, al_ref, dk_ref, y_ref, sp_ref, st):
        c = pl.program_id(0)

        @pl.when(c == 0)
        def _():
            st[...] = jnp.zeros_like(st)

        s_t = st[...]
        sp_ref[0] = s_t
        cm = _chunk_common(dt_ref[...], dtb_ref[...], al_ref[...], dk_ref[...])
        gms, cbs, bts = [], [], []
        for g in range(SG):
            bf = b_ref[:, SN * g:SN * (g + 1)]
            cb = c_ref[:, SN * g:SN * (g + 1)].astype(BF16)
            gms.append(_dot_nt(cb, bf.astype(BF16)))
            cbs.append(cb)
            bts.append(bf.T.astype(BF16))
        m = (cm.lam.reshape(SG, SR, BLK, BLK) * jnp.stack(gms)[:, None]).reshape(SH, BLK, BLK).astype(BF16)
        xs16 = _pairs(xs_ref[...])
        xdt16 = xs16 * cm.dpl
        x_lo = jnp.where(cm.lo, xdt16, 0.0).astype(BF16)
        x_hi = jnp.where(cm.lo, 0.0, xdt16).astype(BF16)
        s16 = _pairs(s_t)
        s16b = s16.astype(BF16)
        yd = jnp.stack([_dot(m[2 * k], x_lo[k]) + _dot(m[2 * k + 1], x_hi[k]) for k in range(NPAIR)])
        yo = jnp.stack([_dot(cbs[k // (NPAIR // SG)], s16b[k]) for k in range(NPAIR)])
        y_ref[...] = _unpairs(yd + yo * cm.eapl + cm.dskpl * xs16)
        xe = (xdt16 * cm.epl).astype(BF16)
        st[...] = _unpairs(cm.cdpl * s16 + jnp.stack([_dot(bts[k // (NPAIR // SG)], xe[k]) for k in range(NPAIR)]))

    vec = _full((1, 128))
    return pl.pallas_call(
        body, name="ssd_fwd", grid=(nc,),
        in_specs=[pl.BlockSpec((BLK, SSM_W), lambda c: (c, 0)),
                  pl.BlockSpec((BLK, SG * SN), lambda c: (c, SSM_W // (SG * SN))),
                  pl.BlockSpec((BLK, SG * SN), lambda c: (c, SSM_W // (SG * SN) + 1)),
                  pl.BlockSpec((BLK, 128), lambda c: (c, C_DT // 128)), vec, vec, vec],
        out_specs=[pl.BlockSpec((BLK, SSM_W), lambda c: (c, 0)), pl.BlockSpec((1, SN, SSM_W), lambda c: (c, 0, 0))],
        out_shape=[jax.ShapeDtypeStruct((t, SSM_W), F32), jax.ShapeDtypeStruct((nc, SN, SSM_W), F32)],
        scratch_shapes=[pltpu.VMEM((SN, SSM_W), F32)],
        compiler_params=_cp(("arbitrary",)),
    )(act, act, act, proj, dtb_p, alog_p, dsk_p)


def _head_sums(q):
    r = q.shape[1]
    lo = lax.broadcasted_iota(jnp.int32, (1, r, 128), 2) < SP
    s_lo = jnp.sum(jnp.where(lo, q, 0.0), axis=-1, keepdims=True)
    s_hi = jnp.sum(jnp.where(lo, 0.0, q), axis=-1, keepdims=True)
    lane = lax.broadcasted_iota(jnp.int32, (r, 128), 1)
    out = jnp.zeros((r, 128), F32)
    for k in range(NPAIR):
        out = jnp.where(lane == 2 * k, s_lo[k], jnp.where(lane == 2 * k + 1, s_hi[k], out))
    return out


def ssd_bwd(act, proj, dy, sprev, dtb_p, alog_p, dsk_p):
    t = act.shape[0]
    nc = t // BLK

    def body(xs_ref, b_ref, c_ref, dt_ref, dy_ref, sp_ref, dtb_ref, al_ref, dk_ref,
             da_ref, ddt_ref, ddtb_ref, dal_ref, ddk_ref, dst):
        i = pl.program_id(0)

        @pl.when(i == 0)
        def _():
            dst[...] = jnp.zeros_like(dst)
            ddtb_ref[...] = jnp.zeros_like(ddtb_ref)
            dal_ref[...] = jnp.zeros_like(dal_ref)
            ddk_ref[...] = jnp.zeros_like(ddk_ref)

        dt_raw = dt_ref[...]
        dtb = dtb_ref[...]
        cm = _chunk_common(dt_raw, dtb, al_ref[...], dk_ref[...])
        ri = lax.broadcasted_iota(jnp.int32, (BLK, BLK), 0)
        ci = lax.broadcasted_iota(jnp.int32, (BLK, BLK), 1)
        lam_t = jnp.exp(jnp.where((ri <= ci)[None], cm.arow - cm.acol, NEG))
        bbs, cbs, cts, gms = [], [], [], []
        for g in range(SG):
            bf = b_ref[:, SN * g:SN * (g + 1)]
            cf = c_ref[:, SN * g:SN * (g + 1)]
            bbs.append(bf.astype(BF16))
            cbs.append(cf.astype(BF16))
            cts.append(cf.T.astype(BF16))
            gms.append(_dot_nt(bbs[g], cbs[g]))
        grp = lambda k: k // (NPAIR // SG)
        xs16 = _pairs(xs_ref[...])
        dy16 = _pairs(dy_ref[...])
        sp16 = _pairs(sp_ref[0])
        ds16 = _pairs(dst[...])
        xdt16 = xs16 * cm.dpl
        xdtb = xdt16.astype(BF16)
        dyh = [jnp.where(cm.lo, dy16, 0.0).astype(BF16), jnp.where(cm.lo, 0.0, dy16).astype(BF16)]
        m_t = (lam_t.reshape(SG, SR, BLK, BLK) * jnp.stack(gms)[:, None]).reshape(SH, BLK, BLK).astype(BF16)
        dxdt = jnp.stack([_dot(m_t[2 * k], dyh[0][k]) + _dot(m_t[2 * k + 1], dyh[1][k]) for k in range(NPAIR)])
        dm = jnp.stack([_dot_nt(dyh[h % 2][h // 2], xdtb[h // 2]) for h in range(SH)])
        dgl = (dm * cm.lam).reshape(SG, SR, BLK, BLK)
        dg = jnp.sum(dgl, axis=1).astype(BF16)
        w = (dgl * jnp.stack([_dot_nt(cbs[g], bbs[g]) for g in range(SG)])[:, None]).reshape(SH, BLK, BLK)
        w_rows = jnp.sum(w, axis=2, keepdims=True)
        w_cols = jnp.concatenate([jnp.sum(w, axis=1)] + [jnp.zeros((128 - SH, BLK), F32)], axis=0).T
        lane_c = lax.broadcasted_iota(jnp.int32, (BLK, 128), 1)
        da_cols = -w_cols
        for h in range(SH):
            da_cols = jnp.where(lane_c == h, da_cols + w_rows[h], da_cols)
        ds16b = ds16.astype(BF16)
        sp16b = sp16.astype(BF16)
        dxs = jnp.stack([_dot(bbs[grp(k)], ds16b[k]) for k in range(NPAIR)]) * cm.epl
        dxdt = dxdt + dxs
        dya = (dy16 * cm.eapl).astype(BF16)
        xe = (xdt16 * cm.epl).astype(BF16)
        dcs, dbs = [], []
        for g in range(SG):
            ks = range(g * (NPAIR // SG), (g + 1) * (NPAIR // SG))
            dcs.append(sum(_dot_nt(dya[k], sp16b[k]) for k in ks) + _dot(dg[g], bbs[g]))
            dbs.append(sum(_dot_nt(xe[k], ds16b[k]) for k in ks) + _dot_tn(dg[g], cbs[g]))
        dst[...] = _unpairs(cm.cdpl * ds16 + jnp.stack([_dot(cts[grp(k)], dya[k]) for k in range(NPAIR)]))
        da_ref[...] = jnp.concatenate([_unpairs(dxdt * cm.dpl + cm.dskpl * dy16)] + dbs + dcs, axis=1)
        y_off = jnp.stack([_dot(cbs[grp(k)], sp16b[k]) for k in range(NPAIR)]) * cm.eapl
        da_cols = da_cols + _head_sums(dy16 * y_off - xdt16 * dxs)
        last = _head_sums(jnp.sum(xdt16 * dxs, axis=1, keepdims=True)
                          + cm.cdpl * jnp.sum(ds16 * sp16, axis=1, keepdims=True))
        ddt = _head_sums(dxdt * xs16)
        row_i = lax.broadcasted_iota(jnp.int32, (BLK, 128), 0)
        dacum = da_cols + jnp.where(row_i == BLK - 1, last, 0.0)
        dda = _tri_mm((ri <= ci).astype(BF16), dacum)
        ddt = ddt + dda * cm.a
        dal_ref[...] += _csum(dda * cm.dtv) * cm.a
        ddt_raw = jnp.where(lane_c < SH, ddt * _sig(dt_raw + dtb), 0.0)
        ddt_ref[...] = ddt_raw.astype(BF16)
        ddtb_ref[...] += _csum(ddt_raw)
        ddk_ref[...] += _head_sums(jnp.sum(dy16 * xs16, axis=1, keepdims=True))

    rev = lambda i: nc - 1 - i
    vec = _full((1, 128))
    slab = pl.BlockSpec((BLK, SSM_W), lambda i: (rev(i), 0))
    return pl.pallas_call(
        body, name="ssd_bwd", grid=(nc,),
        in_specs=[slab,
                  pl.BlockSpec((BLK, SG * SN), lambda i: (rev(i), SSM_W // (SG * SN))),
                  pl.BlockSpec((BLK, SG * SN), lambda i: (rev(i), SSM_W // (SG * SN) + 1)),
                  pl.BlockSpec((BLK, 128), lambda i: (rev(i), C_DT // 128)),
                  slab,
                  pl.BlockSpec((1, SN, SSM_W), lambda i: (rev(i), 0, 0)), vec, vec, vec],
        out_specs=[pl.BlockSpec((BLK, XBC), lambda i: (rev(i), 0)), pl.BlockSpec((BLK, 128), lambda i: (rev(i), 0)),
                   vec, vec, vec],
        out_shape=[jax.ShapeDtypeStruct((t, XBC), F32), jax.ShapeDtypeStruct((t, 128), BF16),
                   jax.ShapeDtypeStruct((1, 128), F32), jax.ShapeDtypeStruct((1, 128), F32),
                   jax.ShapeDtypeStruct((1, 128), F32)],
        scratch_shapes=[pltpu.VMEM((SN, SSM_W), F32)],
        compiler_params=_cp(("arbitrary",)),
    )(act, act, act, proj, dy, sprev, dtb_p, alog_p, dsk_p)


TAIL_TM = 128


def _dsilu(z, s):
    return s * (1.0 + z * (1.0 - s))


def tail(proj, ao, yss, x, target, gate, ssm_nw, w_at, w_ss, w_ou):
    t = x.shape[0]
    tm = min(t, TAIL_TM)
    gw = SSM_W // SG

    def body(ao_ref, za_ref, ga_ref, gb_ref, zm_ref, ys_ref, x_ref, tg_ref, gt_ref, nw_ref, wa_ref, ws_ref, wo_ref,
             loss_ref, dy_ref, dao_ref, dza_ref, dga_ref, dgb_ref, dys_ref, dzm_ref,
             ua_ref, yn_ref, mg_ref, dya_ref, dyb_ref, do_ref, dgt_ref, dnw_ref):
        i = pl.program_id(0)

        @pl.when(i == 0)
        def _():
            loss_ref[...] = jnp.zeros_like(loss_ref)
            dgt_ref[...] = jnp.zeros_like(dgt_ref)
            dnw_ref[...] = jnp.zeros_like(dnw_ref)

        ao = ao_ref[...]
        za = za_ref[...]
        sa = _sig(za)
        sila = za * sa
        ua_f = ao * sila
        ua = ua_f.astype(BF16)
        ya = _dot(ua, wa_ref[...])
        zm = zm_ref[...]
        sm = _sig(zm)
        silm = zm * sm
        ys = ys_ref[...]
        u = ys * silm
        nw = nw_ref[...]
        rs, uns = [], []
        for g in range(SG):
            ug = u[:, gw * g:gw * (g + 1)]
            r = lax.rsqrt(jnp.mean(ug * ug, axis=-1, keepdims=True) + EPS)
            rs.append(r)
            uns.append(ug * r)
        un = jnp.concatenate(uns, axis=1)
        yn_f = un * nw
        yn = yn_f.astype(BF16)
        yb = _dot(yn, ws_ref[...])
        sga = _sig(ga_ref[...])
        sgb = _sig(gb_ref[...])
        mg_f = sga * ya + sgb * yb
        mg = mg_f.astype(BF16)
        o = _dot(mg, wo_ref[...])
        gt = gt_ref[...]
        err = (x_ref[...] + gt * o) - tg_ref[...]
        lane = lax.broadcasted_iota(jnp.int32, (1, 128), 1)
        loss_ref[...] += jnp.where(lane == 0, 0.5 * _asum(_rsum(err * err) / D), 0.0)
        dy = err * (1.0 / D)
        dy_ref[...] = dy
        dgt_ref[...] += _csum(dy * o)
        do = (dy * gt).astype(BF16)
        dmg = _dot_nt(do, wo_ref[...])
        dga_ref[...] = (dmg * ya * sga * (1.0 - sga)).astype(BF16)
        dgb_ref[...] = (dmg * yb * sgb * (1.0 - sgb)).astype(BF16)
        dya = (dmg * sga).astype(BF16)
        dyb = (dmg * sgb).astype(BF16)
        dua = _dot_nt(dya, wa_ref[...])
        dao_ref[...] = dua * sila
        dza_ref[...] = (dua * ao * _dsilu(za, sa)).astype(BF16)
        dyn = _dot_nt(dyb, ws_ref[...])
        dnw_ref[...] += _csum(dyn * un)
        dun = dyn * nw
        dus = []
        for g in range(SG):
            gs = slice(gw * g, gw * (g + 1))
            dus.append(rs[g] * (dun[:, gs] - uns[g] * jnp.mean(dun[:, gs] * uns[g], axis=-1, keepdims=True)))
        du = jnp.concatenate(dus, axis=1)
        dys_ref[...] = du * silm
        dzm_ref[...] = (du * ys * _dsilu(zm, sm)).astype(BF16)
        ua_ref[...] = ua_f.T.astype(BF16)
        yn_ref[...] = yn_f.T.astype(BF16)
        mg_ref[...] = mg_f.T.astype(BF16)
        dya_ref[...] = dya
        dyb_ref[...] = dyb
        do_ref[...] = do

    row = lambda w: pl.BlockSpec((tm, w), lambda i: (i, 0))
    pcol = lambda w, c0: pl.BlockSpec((tm, w), lambda i: (i, c0 // w))
    sd = lambda w, dt: jax.ShapeDtypeStruct((t, w), dt)
    colt = lambda w: pl.BlockSpec((w, tm), lambda i: (0, i))
    sdt = lambda w: jax.ShapeDtypeStruct((w, t), BF16)
    return pl.pallas_call(
        body, name="tail", grid=(t // tm,),
        in_specs=[row(D), pcol(D, C_ZA), pcol(D, C_GA), pcol(D, C_GB), pcol(SSM_W, C_ZM), row(SSM_W), row(D), row(D),
                  _full((1, D)), _full((1, SSM_W)), _full((D, D)), _full((SSM_W, D)), _full((D, D))],
        out_specs=[_full((1, 128)), row(D), row(D), row(D), row(D), row(D), row(SSM_W), row(SSM_W),
                   colt(D), colt(SSM_W), colt(D), row(D), row(D), row(D), _full((1, D)), _full((1, SSM_W))],
        out_shape=[jax.ShapeDtypeStruct((1, 128), F32), sd(D, F32), sd(D, F32), sd(D, BF16), sd(D, BF16), sd(D, BF16),
                   sd(SSM_W, F32), sd(SSM_W, BF16), sdt(D), sdt(SSM_W), sdt(D), sd(D, BF16),
                   sd(D, BF16), sd(D, BF16), jax.ShapeDtypeStruct((1, D), F32), jax.ShapeDtypeStruct((1, SSM_W), F32)],
        compiler_params=_cp(("arbitrary",)),
    )(ao, proj, proj, proj, proj, yss, x, target, gate, ssm_nw, w_at, w_ss, w_ou)


def dproj_bwd(dproj, wcat, x, dy, norm_w, scale):
    t = x.shape[0]
    tm = min(t, 512)
    tk = NP // 6
    nk = NP // tk
    nt = t // tm

    def body(dp_ref, w_ref, x_ref, dy_ref, nw_ref, sc_ref, gx_ref, dnw_ref, dsc_ref, dsh_ref, acc, dwe_ref):
        i = pl.program_id(0)
        k = pl.program_id(1)

        @pl.when(jnp.logical_and(i == 0, k == 0))
        def _():
            dwe_ref[...] = jnp.zeros_like(dwe_ref)
            dsh_ref[...] = jnp.zeros_like(dsh_ref)
            dnw_ref[...] = jnp.zeros_like(dnw_ref)
            dsc_ref[...] = jnp.zeros_like(dsc_ref)

        part = _dot_nt(dp_ref[...], w_ref[...])

        @pl.when(k == 0)
        def _():
            acc[...] = part

        @pl.when(k > 0)
        def _():
            acc[...] += part

        @pl.when(k == nk - 1)
        def _():
            dh = acc[...]
            xv = x_ref[...]
            r = lax.rsqrt(jnp.mean(xv * xv, axis=-1, keepdims=True) + EPS)
            xn = xv * r
            weff = nw_ref[...] * (1.0 + sc_ref[...])
            dxn = dh * weff
            gx_ref[...] = dy_ref[...] + r * (dxn - xn * jnp.mean(dxn * xn, axis=-1, keepdims=True))
            dwe_ref[...] += _csum(dh * xn)
            dsh_ref[...] += _csum(dh)

        @pl.when(jnp.logical_and(i == nt - 1, k == nk - 1))
        def _():
            dwe = dwe_ref[...]
            dnw_ref[...] = dwe * (1.0 + sc_ref[...])
            dsc_ref[...] = dwe * nw_ref[...]

    vec = pl.BlockSpec((1, D), lambda i, k: (0, 0))
    row = pl.BlockSpec((tm, D), lambda i, k: (i, 0))
    return pl.pallas_call(
        body, name="dproj_bwd", grid=(nt, nk),
        in_specs=[pl.BlockSpec((tm, tk), lambda i, k: (i, k)), pl.BlockSpec((D, tk), lambda i, k: (0, k)),
                  row, row, vec, vec],
        out_specs=[row, vec, vec, vec],
        out_shape=[jax.ShapeDtypeStruct((t, D), F32), jax.ShapeDtypeStruct((1, D), F32),
                   jax.ShapeDtypeStruct((1, D), F32), jax.ShapeDtypeStruct((1, D), F32)],
        scratch_shapes=[pltpu.VMEM((tm, D), F32), pltpu.VMEM((1, D), F32)],
        compiler_params=_cp(("arbitrary", "arbitrary")),
    )(dproj, wcat, x, dy, norm_w, scale)


def wgrad(at, b, name, bn):
    m, t = at.shape
    n = b.shape[1]
    tk = min(t, 1024)
    bm = min(m, 1024)

    def body(a_ref, b_ref, o_ref):
        part = _dot(a_ref[...], b_ref[...])

        @pl.when(pl.program_id(2) == 0)
        def _():
            o_ref[...] = part

        @pl.when(pl.program_id(2) > 0)
        def _():
            o_ref[...] += part

    return pl.pallas_call(
        body, name=name, grid=(m // bm, n // bn, t // tk),
        in_specs=[pl.BlockSpec((bm, tk), lambda i, j, k: (i, k)), pl.BlockSpec((tk, bn), lambda i, j, k: (k, j))],
        out_specs=pl.BlockSpec((bm, bn), lambda i, j, k: (i, j)),
        out_shape=jax.ShapeDtypeStruct((m, n), F32),
        compiler_params=_cp(("parallel", "parallel", "arbitrary")),
    )(at, b)


SUM_TR = 256


def pair_sum(g, core, theirs, name):
    w = g.shape[2]
    nh = HROWS // SUM_TR

    def body(core_ref, a_ref, b_ref, o_ref, ob_ref):
        s = a_ref[...] + b_ref[...]
        o_ref[...] = s
        ob_ref[...] = s.astype(BF16)

    spec = pl.BlockSpec((1, SUM_TR, w), lambda d, i, c: (d, i, 0))
    return pl.pallas_call(
        body, name=name,
        out_shape=[jax.ShapeDtypeStruct((4, HROWS, w), F32), jax.ShapeDtypeStruct((4, HROWS, w), BF16)],
        grid_spec=pltpu.PrefetchScalarGridSpec(
            num_scalar_prefetch=1, grid=(4, nh),
            in_specs=[pl.BlockSpec((1, SUM_TR, w), lambda d, i, c: (d, c[0] * nh + i, 0)), spec],
            out_specs=[spec, spec]),
        compiler_params=_cp(("parallel", "parallel")))(core.reshape(1).astype(jnp.int32), g, theirs)


def chip_sum(part, chip, others, name):
    r, w = part.shape[1:]

    def body(chip_ref, a_ref, b_ref, o_ref):
        acc = a_ref[0]
        for k in range(3):
            acc = acc + b_ref[k].astype(F32)
        o_ref[...] = acc

    return pl.pallas_call(
        body, name=name, out_shape=jax.ShapeDtypeStruct((r, w), F32),
        grid_spec=pltpu.PrefetchScalarGridSpec(
            num_scalar_prefetch=1, grid=(r // SUM_TR,),
            in_specs=[pl.BlockSpec((1, SUM_TR, w), lambda i, c: (c[0], i, 0)),
                      pl.BlockSpec((3, SUM_TR, w), lambda i, c: (0, i, 0))],
            out_specs=pl.BlockSpec((SUM_TR, w), lambda i, c: (i, 0))),
        compiler_params=_cp(("parallel",)))(chip.reshape(1).astype(jnp.int32), part, others)


def sum_devices(g):
    r = g.shape[1]

    def body(g_ref, o_ref):
        acc = g_ref[0]
        for d in range(1, 8):
            acc = acc + g_ref[d]
        o_ref[...] = acc

    return pl.pallas_call(body, name="sum_devices", out_shape=jax.ShapeDtypeStruct((r, 1024), F32),
                          compiler_params=_cp())(g)


def adamw(w, g, m, v, name):
    r, c = w.shape
    tr = r
    for cand in (256, 128, 64, 32, 16, 8):
        if r % cand == 0 and r > cand:
            tr = cand
            break

    def body(w_ref, g_ref, m_ref, v_ref, d_ref, nm_ref, nv_ref):
        gv = g_ref[...]
        mn = ADAM_B1 * m_ref[...] + (1.0 - ADAM_B1) * gv
        vn = ADAM_B2 * v_ref[...] + (1.0 - ADAM_B2) * (gv * gv)
        m_hat = mn / (1.0 - ADAM_B1 ** ADAM_STEP)
        v_hat = vn / (1.0 - ADAM_B2 ** ADAM_STEP)
        d_ref[...] = -ADAM_LR * (m_hat / (jnp.sqrt(v_hat) + ADAM_EPS) + ADAM_WD * w_ref[...])
        nm_ref[...] = mn
        nv_ref[...] = vn

    spec = pl.BlockSpec((tr, c), lambda i: (i, 0))
    sd = jax.ShapeDtypeStruct((r, c), F32)
    return pl.pallas_call(body, name=name, grid=(r // tr,), in_specs=[spec] * 4, out_specs=[spec] * 3,
                          out_shape=[sd, sd, sd], compiler_params=_cp(("parallel",)))(w, g, m, v)


ANY = pl.BlockSpec(memory_space=pl.ANY)
VM = pl.BlockSpec(memory_space=pltpu.VMEM)
OTHER_CHIPS = ((1, 0), (0, 1), (1, 1))


def _pos():
    return lax.axis_index("x"), lax.axis_index("y"), lax.axis_index("c")


def _flip(v, bit):
    return 1 - v if bit else v


def _rcopy(src, dst, ssem, rsem, peer):
    return pltpu.make_async_remote_copy(src_ref=src, dst_ref=dst, send_sem=ssem, recv_sem=rsem,
                                        device_id=peer, device_id_type=MESH)


def allgather_small(p, name):
    r = p.shape[0]

    def body(in_ref, out_ref, ssem, rsem, lsem):
        x, y, c = _pos()
        me = 4 * x + 2 * y + c
        loc = pltpu.make_async_copy(in_ref, out_ref.at[me], lsem)
        loc.start()
        sends = []
        peers = []
        for k in range(1, 8):
            px, py, pc = _flip(x, (k >> 2) & 1), _flip(y, (k >> 1) & 1), _flip(c, k & 1)
            peers.append((px, py, pc))
            cp = _rcopy(in_ref, out_ref.at[me], ssem.at[k - 1], rsem.at[k - 1], (px, py, pc))
            cp.start()
            sends.append(cp)
        for k in range(1, 8):
            px, py, pc = peers[k - 1]
            _rcopy(in_ref, out_ref.at[4 * px + 2 * py + pc], ssem.at[k - 1], rsem.at[k - 1], (px, py, pc)).wait_recv()
        for cp in sends:
            cp.wait_send()
        loc.wait()

    return pl.pallas_call(
        body, name=name, out_shape=jax.ShapeDtypeStruct((8, r, 1024), F32),
        in_specs=[VM], out_specs=VM,
        scratch_shapes=[pltpu.SemaphoreType.DMA((7,)), pltpu.SemaphoreType.DMA((7,)), pltpu.SemaphoreType.DMA],
    )(p)


def gather_weights(w_in_b, w_rest_b, mod_sh):
    def body(wi_ref, wr_ref, m_ref, gi_ref, gr_ref, mo_ref, ssem, rsem, lsem):
        x, y, c = _pos()
        chip = 2 * x + y
        mine = pl.ds(pl.multiple_of(c * HROWS, 16), HROWS)
        other = pl.ds(pl.multiple_of((1 - c) * HROWS, 16), HROWS)
        sib = (x, y, 1 - c)
        pairs = ((wi_ref, gi_ref), (wr_ref, gr_ref))
        loc_m = pltpu.make_async_copy(m_ref, mo_ref.at[chip], lsem)
        loc_m.start()
        sends = []
        for k, (fx, fy) in enumerate(OTHER_CHIPS):
            peer = (_flip(x, fx), _flip(y, fy), c)
            for a, (w_ref, g_ref) in enumerate(pairs):
                cw = _rcopy(w_ref.at[mine], g_ref.at[chip, mine], ssem.at[6 * a + k], rsem.at[6 * a + k], peer)
                cw.start()
                sends.append(cw)
            cm = _rcopy(m_ref, mo_ref.at[chip], ssem.at[12 + k], rsem.at[12 + k], peer)
            cm.start()
            sends.append(cm)
        for k, (fx, fy) in enumerate(OTHER_CHIPS):
            px, py = _flip(x, fx), _flip(y, fy)
            for a, (w_ref, g_ref) in enumerate(pairs):
                got = g_ref.at[2 * px + py, mine]
                _rcopy(w_ref.at[mine], got, ssem.at[6 * a + k], rsem.at[6 * a + k], (px, py, c)).wait_recv()
                fw = _rcopy(got, got, ssem.at[6 * a + 3 + k], rsem.at[6 * a + 3 + k], sib)
                fw.start()
                sends.append(fw)
        for k, (fx, fy) in enumerate(OTHER_CHIPS):
            px, py = _flip(x, fx), _flip(y, fy)
            for a, (w_ref, g_ref) in enumerate(pairs):
                land = g_ref.at[2 * px + py, other]
                _rcopy(land, land, ssem.at[6 * a + 3 + k], rsem.at[6 * a + 3 + k], sib).wait_recv()
            _rcopy(m_ref, mo_ref.at[2 * px + py], ssem.at[12 + k], rsem.at[12 + k], (px, py, c)).wait_recv()
        for cp in sends:
            cp.wait_send()
        loc_m.wait()

    return pl.pallas_call(
        body, name="gather_weights",
        out_shape=[jax.ShapeDtypeStruct((4, D, SH_IN), BF16), jax.ShapeDtypeStruct((4, D, D), BF16),
                   jax.ShapeDtypeStruct((4, 8, 768), F32)],
        in_specs=[ANY, ANY, VM], out_specs=[ANY, ANY, VM],
        scratch_shapes=[pltpu.SemaphoreType.DMA((15,)), pltpu.SemaphoreType.DMA((15,)), pltpu.SemaphoreType.DMA],
    )(w_in_b, w_rest_b, mod_sh)


def pair_exchange(g_in, g_rest):
    def body(gi_ref, gr_ref, ri_ref, rr_ref, ssem, rsem):
        x, y, c = _pos()
        other = pl.ds(pl.multiple_of((1 - c) * HROWS, 8), HROWS)
        cps = [_rcopy(g_ref.at[:, other, :], r_ref, ssem.at[a], rsem.at[a], (x, y, 1 - c))
               for a, (g_ref, r_ref) in enumerate(((gi_ref, ri_ref), (gr_ref, rr_ref)))]
        for cp in cps:
            cp.start()
        for cp in cps:
            cp.wait()

    return pl.pallas_call(
        body, name="pair_exchange",
        out_shape=[jax.ShapeDtypeStruct((4, HROWS, SH_IN), F32), jax.ShapeDtypeStruct((4, HROWS, D), F32)],
        in_specs=[ANY, ANY], out_specs=[ANY, ANY],
        scratch_shapes=[pltpu.SemaphoreType.DMA((2,)), pltpu.SemaphoreType.DMA((2,))],
    )(g_in, g_rest)


def chip_exchange(pb_in, pb_rest):
    def body(pi_ref, pr_ref, ri_ref, rr_ref, ssem, rsem):
        x, y, c = _pos()
        chip = 2 * x + y
        pairs = ((pi_ref, ri_ref), (pr_ref, rr_ref))
        sends = []
        for k, (fx, fy) in enumerate(OTHER_CHIPS):
            px, py = _flip(x, fx), _flip(y, fy)
            for a, (p_ref, r_ref) in enumerate(pairs):
                cp = _rcopy(p_ref.at[2 * px + py], r_ref.at[k], ssem.at[3 * a + k], rsem.at[3 * a + k], (px, py, c))
                cp.start()
                sends.append(cp)
        for k, (fx, fy) in enumerate(OTHER_CHIPS):
            px, py = _flip(x, fx), _flip(y, fy)
            for a, (p_ref, r_ref) in enumerate(pairs):
                _rcopy(p_ref.at[chip], r_ref.at[k], ssem.at[3 * a + k], rsem.at[3 * a + k], (px, py, c)).wait_recv()
        for cp in sends:
            cp.wait_send()

    return pl.pallas_call(
        body, name="chip_exchange",
        out_shape=[jax.ShapeDtypeStruct((3, HROWS, SH_IN), BF16), jax.ShapeDtypeStruct((3, HROWS, D), BF16)],
        in_specs=[ANY, ANY], out_specs=[ANY, ANY],
        scratch_shapes=[pltpu.SemaphoreType.DMA((6,)), pltpu.SemaphoreType.DMA((6,))],
    )(pb_in, pb_rest)


def pair_swap(red_in, red_rest):
    def body(ai_ref, ar_ref, oi_ref, or_ref, ssem, rsem):
        x, y, c = _pos()
        cps = [_rcopy(a_ref, o_ref, ssem.at[a], rsem.at[a], (x, y, 1 - c))
               for a, (a_ref, o_ref) in enumerate(((ai_ref, oi_ref), (ar_ref, or_ref)))]
        for cp in cps:
            cp.start()
        for cp in cps:
            cp.wait()

    return pl.pallas_call(
        body, name="pair_swap",
        out_shape=[jax.ShapeDtypeStruct((HROWS, SH_IN), F32), jax.ShapeDtypeStruct((HROWS, D), F32)],
        in_specs=[ANY, ANY], out_specs=[ANY, ANY],
        scratch_shapes=[pltpu.SemaphoreType.DMA((2,)), pltpu.SemaphoreType.DMA((2,))],
    )(red_in, red_rest)


def _row(v, width=1024):
    v = v.reshape(-1)
    n = -(-v.shape[0] // width) * width
    return jnp.pad(v, (0, n - v.shape[0])).reshape(-1, width)


def _slots(vs):
    row = [jnp.pad(v.reshape(-1), (0, 128 - v.size)) for v in vs]
    row += [jnp.zeros((128,), F32)] * (8 - len(row))
    return jnp.concatenate(row).reshape(1, 1024)


def _pack_small(b_ada, norm_w, conv_b, ssm_norm_w, q_norm_w, k_norm_w, sinks, dt_bias, a_log, d_skip, rel_bias,
                extra=None):
    misc = [q_norm_w, k_norm_w, sinks, dt_bias, a_log, d_skip] + ([] if extra is None else [extra])
    rows = [_row(b_ada), _row(norm_w), _row(conv_b), _row(ssm_norm_w), _slots(misc), _row(rel_bias)]
    rows.append(jnp.zeros((5, 1024), F32))
    return jnp.concatenate(rows, axis=0)


def _unpack_small(p):
    misc = p[9]
    return dict(b_ada=p[0:3].reshape(1, 3072), norm_w=p[3:4], conv_b=p[4:7].reshape(1, 3072),
                ssm_norm_w=p[7:9].reshape(1, 2048), q_norm_w=misc[None, 0:64], k_norm_w=misc[None, 128:192],
                sinks=misc[None, 256:272], dt_bias=misc[None, 384:416], a_log=misc[None, 512:544],
                d_skip=misc[None, 640:672], rel_bias=p[10, :512].reshape(32, 16), extra=misc[768])


SMALL = ("b_ada", "norm_w", "conv_b", "ssm_norm_w", "q_norm_w", "k_norm_w", "sinks", "dt_bias", "a_log", "d_skip",
         "rel_bias")
WEIGHTS = ("w_ada", "b_ada", "norm_w", "w_in", "q_norm_w", "k_norm_w", "rel_bias", "sinks", "conv_w", "conv_b",
           "dt_bias", "a_log", "d_skip", "ssm_norm_w", "w_attn_proj", "w_ssm_proj", "w_out")
IN_COLS = ((0, 1024, C_Q), (1024, 256, C_K), (1280, 256, C_V), (1536, 1024, C_ZA), (2560, 2048, C_ZM),
           (4608, 3072, C_XBC), (7680, 32, C_DT), (7712, 1024, C_GA), (8736, 1024, C_GB))


def _to_cat(shards):
    parts, pos = [], 0
    for o, n, cnew in sorted(IN_COLS, key=lambda e: e[2]):
        assert cnew == pos
        c0 = o
        while c0 < o + n:
            i = c0 // SH_IN
            c1 = min(o + n, (i + 1) * SH_IN)
            parts.append(shards[i][:, c0 - i * SH_IN:c1 - i * SH_IN])
            c0 = c1
        pos += n
    parts.append(jnp.zeros((D, NP - pos), shards.dtype))
    return jnp.concatenate(parts, axis=1)


def _from_cat(w_cat):
    shards = []
    for i in range(4):
        lo, hi = i * SH_IN, (i + 1) * SH_IN
        parts = []
        for o, n, cnew in IN_COLS:
            a, b = max(o, lo), min(o + n, hi)
            if a < b:
                parts.append(w_cat[:, cnew + a - o:cnew + b - o])
        shards.append(jnp.concatenate(parts, axis=1))
    return jnp.stack(shards)


def kernel(x, c, w_ada, b_ada, norm_w, w_in, q_norm_w, k_norm_w, rel_bias, sinks, conv_w, conv_b, dt_bias, a_log, d_skip, ssm_norm_w, w_attn_proj, w_ssm_proj, w_out, loss_target, m_w_ada, m_b_ada, m_norm_w, m_w_in, m_q_norm_w, m_k_norm_w, m_rel_bias, m_sinks, m_conv_w, m_conv_b, m_dt_bias, m_a_log, m_d_skip, m_ssm_norm_w, m_w_attn_proj, m_w_ssm_proj, m_w_out, v_w_ada, v_b_ada, v_norm_w, v_w_in, v_q_norm_w, v_k_norm_w, v_rel_bias, v_sinks, v_conv_w, v_conv_b, v_dt_bias, v_a_log, v_d_skip, v_ssm_norm_w, v_w_attn_proj, v_w_ssm_proj, v_w_out):
    args = dict(locals())
    xi, yi, ci = lax.axis_index("x"), lax.axis_index("y"), lax.axis_index("c")
    chip = 2 * xi + yi
    me = 4 * xi + 2 * yi + ci
    x2 = x[0]
    tgt = loss_target[0]

    pay = jnp.concatenate([c, conv_w[0].reshape(3, 1024), jnp.zeros((4, 1024), F32)], axis=0)
    g0 = allgather_small(pay, "gather_cond")
    c_all = g0[:, 0, :]
    conv_w_full = g0[0::2, 1:4, :].reshape(4, CONV_K, 768).transpose(1, 0, 2).reshape(CONV_K, XBC)

    b_ada_sh = lax.dynamic_slice(b_ada, (0, chip * 768), (1, 768))
    mod_sh = ada_mod(c_all, w_ada[0], b_ada_sh)

    w_in_b = w_in[0].astype(BF16)
    w_rest_b = jnp.concatenate([w_attn_proj[0], w_ssm_proj[0], w_out[0]], axis=0).astype(BF16)
    wg_in, wg_rest, modg = gather_weights(w_in_b, w_rest_b, mod_sh)
    wg_in = lax.dynamic_update_slice(wg_in, w_in_b[None], (chip, 0, 0))
    wg_rest = lax.dynamic_update_slice(wg_rest, w_rest_b[None], (chip, 0, 0))
    mod = lax.dynamic_slice(modg, (0, me, 0), (4, 1, 768)).reshape(1, 3 * D)
    shift, scale, gate = mod[:, :D], mod[:, D:2 * D], mod[:, 2 * D:]
    wcat = _to_cat(wg_in)
    w_at = wg_rest[:, :R_AT].reshape(D, D)
    w_ss = wg_rest[:, R_AT:R_AT + R_SS].reshape(SSM_W, D)
    w_ou = wg_rest[:, R_AT + R_SS:].reshape(D, D)

    pad128 = lambda v: jnp.pad(v, ((0, 0), (0, 128 - v.shape[1])))
    dtb_p, alog_p, dsk_p = pad128(dt_bias), pad128(a_log), pad128(d_skip)
    bucket = _bucket_table()

    proj, h_t = norm_proj(x2, norm_w, scale, shift, wcat)
    biasm = bias_expand(rel_bias, sinks, bucket)
    ao = attn_fwd(proj, biasm, q_norm_w, k_norm_w)
    act = conv_fwd(proj, conv_w_full, conv_b)
    yss, sprev = ssd_fwd(act, proj, dtb_p, alog_p, dsk_p)

    (loss_p, dy, dao, dza, dga, dgb, dyss, dzm, ua_t, yn_t, mg_t, dya, dyb, dout, dgate, dssm_nw) = tail(
        proj, ao, yss, x2, tgt, gate, ssm_norm_w, w_at, w_ss, w_ou)

    dq, dk, dv, dqw, dkw, dacc = attn_bwd(proj, dao, biasm, q_norm_w, k_norm_w)
    dbias = bias_reduce(dacc, bucket)
    drb = dbias[:, :NBUCKET].T
    dsk = dbias[:, NBUCKET].reshape(1, HQ)
    dact, ddt, ddtb, dalog, ddskip = ssd_bwd(act, proj, dyss, sprev, dtb_p, alog_p, dsk_p)
    dxbc, dconv_w, dconv_b = conv_bwd(proj, dact, conv_w_full, conv_b)

    t = x2.shape[0]
    dproj = jnp.concatenate([dq, dza, dga, dgb, dzm, dxbc, dk, dv, ddt, jnp.zeros((t, NP - C_DT - 128), BF16)], axis=1)
    grad_x, dnorm_w, dscale, dshift = dproj_bwd(dproj, wcat, x2, dy, norm_w, scale)
    dwcat = wgrad(h_t, dproj, "dw_in", TN)
    dw_at = wgrad(ua_t, dya, "dw_attn", 512)
    dw_ss = wgrad(yn_t, dyb, "dw_ssm", 512)
    dw_ou = wgrad(mg_t, dout, "dw_out", 512)

    g_in = _from_cat(dwcat)
    g_rest = jnp.concatenate([dw_at.reshape(4, R_AT, D), dw_ss.reshape(4, R_SS, D), dw_ou.reshape(4, R_OU, D)], axis=1)
    sib_in, sib_rest = pair_exchange(g_in, g_rest)
    part_in, pb_in = pair_sum(g_in, ci, sib_in, "pair_sum_in")
    part_rest, pb_rest = pair_sum(g_rest, ci, sib_rest, "pair_sum_rest")
    oth_in, oth_rest = chip_exchange(pb_in, pb_rest)
    red_in = chip_sum(part_in, chip, oth_in, "chip_sum_in")
    red_rest = chip_sum(part_rest, chip, oth_rest, "chip_sum_rest")
    recv_in, recv_rest = pair_swap(red_in, red_rest)
    both = lambda mine, theirs: jnp.concatenate([jnp.where(ci == 0, mine, theirs), jnp.where(ci == 0, theirs, mine)],
                                                axis=0)
    g_shard_in = both(red_in, recv_in)
    g_shard_rest = both(red_rest, recv_rest)

    dmod = jnp.concatenate([dshift, dscale, dgate], axis=1)
    gsmall = jnp.concatenate([
        _pack_small(dmod, dnorm_w, dconv_b, dssm_nw, dqw, dkw, dsk[:, :HQ], ddtb[:, :SH], dalog[:, :SH],
                    ddskip[:, :SH], drb, extra=loss_p[:, :1]),
        dconv_w.reshape(12, 1024), jnp.zeros((4, 1024), F32)], axis=0)
    gall = allgather_small(gsmall, "gather_small_grads")
    ssum = sum_devices(gall)
    gs = _unpack_small(ssum[:16])
    loss = gs["extra"]
    dconv_w_sh = lax.dynamic_slice(ssum[16:28].reshape(CONV_K, XBC), (0, chip * 768), (CONV_K, 768))
    dmod_all = gall[:, 0:3, :].reshape(8, 3 * D)
    dw_ada = ada_grad(c_all, lax.dynamic_slice(dmod_all, (0, chip * 768), (8, 768)))

    grads = dict(gs)
    grads["w_ada"] = dw_ada
    grads["w_in"] = g_shard_in
    grads["w_attn_proj"] = g_shard_rest[:R_AT]
    grads["w_ssm_proj"] = g_shard_rest[R_AT:R_AT + R_SS]
    grads["w_out"] = g_shard_rest[R_AT + R_SS:]
    grads["conv_w"] = dconv_w_sh

    delta, new_m, new_v = {}, {}, {}
    for n in ("w_ada", "w_in", "conv_w", "w_attn_proj", "w_ssm_proj", "w_out"):
        delta[n], new_m[n], new_v[n] = adamw(args[n][0], grads[n], args["m_" + n][0], args["v_" + n][0], "adamw_" + n)
    ws = _pack_small(*[args[n] for n in SMALL])
    ms = _pack_small(*[args["m_" + n] for n in SMALL])
    vs = _pack_small(*[args["v_" + n] for n in SMALL])
    d_s, m_s, v_s = adamw(ws, ssum[:16], ms, vs, "adamw_small")
    d_s, m_s, v_s = _unpack_small(d_s), _unpack_small(m_s), _unpack_small(v_s)
    for n in SMALL:
        delta[n], new_m[n], new_v[n] = d_s[n], m_s[n], v_s[n]

    def shaped(n, a):
        return a.reshape(args[n].shape)

    outs = [loss, grad_x[None]]
    for table in (grads, delta, new_m, new_v):
        outs += [shaped(n, table[n]) for n in WEIGHTS]
    return tuple(outs)
```

```python
import functools
import math

import numpy as np
import jax
import jax.numpy as jnp
from jax import lax
from jax.experimental import pallas as pl
from jax.experimental.pallas import tpu as pltpu

F32 = jnp.float32
BF16 = jnp.bfloat16
MESH = pl.DeviceIdType.MESH

D = 1024
HQ, HKV, GRP, DH = 16, 4, 4, 64
BLK = 128
NBUCKET, MAXDIST = 32, 128
SSM_W, SH, SG, SR, SP, SN = 2048, 32, 4, 8, 64, 128
CONV_K = 4
XBC = SSM_W + 2 * SG * SN
IN_W = 9760
EPS = 1e-6
NEG = -1e30
SCALE = DH ** -0.5

C_Q, C_ZA, C_GA, C_GB, C_ZM, C_XBC, C_K, C_V, C_DT = 0, 1024, 2048, 3072, 4096, 6144, 9216, 9472, 9728
NP = 9984
TN = 768

SH_IN = IN_W // 4
R_AT, R_SS, R_OU = 256, 512, 256
HROWS = D // 2

ADAM_LR, ADAM_B1, ADAM_B2, ADAM_EPS, ADAM_WD, ADAM_STEP = 0.001, 0.9, 0.999, 1e-08, 0.01, 10

VMEM_LIMIT = 56 * 1024 * 1024


def _cp(sem=None):
    if sem is None:
        return pltpu.CompilerParams(vmem_limit_bytes=VMEM_LIMIT)
    return pltpu.CompilerParams(dimension_semantics=sem, vmem_limit_bytes=VMEM_LIMIT)


def _sig(x):
    return 1.0 / (1.0 + jnp.exp(-x))


def _dot(a, b):
    return jnp.dot(a, b, preferred_element_type=F32)


def _dot_nt(a, b):
    return lax.dot_general(a, b, (((1,), (1,)), ((), ())), preferred_element_type=F32)


def _dot_tn(a, b):
    return lax.dot_general(a, b, (((0,), (0,)), ((), ())), preferred_element_type=F32)


def _rsum(x):
    return jnp.sum(x, axis=-1, keepdims=True)


def _csum(x):
    return jnp.sum(x, axis=0, keepdims=True)


def _asum(x):
    return _csum(_rsum(x))


def _full(shape):
    nd = len(shape)
    return pl.BlockSpec(shape, lambda *_: (0,) * nd)


def ada_mod(c_all, w_ada_sh, b_ada_sh):
    def body(c_ref, w_ref, b_ref, o_ref):
        cv = c_ref[...]
        s = cv * _sig(cv)
        o_ref[...] = jnp.dot(s, w_ref[...], preferred_element_type=F32,
                             precision=lax.Precision.HIGHEST) + b_ref[...]

    n = w_ada_sh.shape[1]
    return pl.pallas_call(body, name="ada_mod", out_shape=jax.ShapeDtypeStruct((8, n), F32),
                          compiler_params=_cp())(c_all, w_ada_sh, b_ada_sh)


def ada_grad(c_all, dmod_sh):
    def body(c_ref, d_ref, o_ref):
        cv = c_ref[...]
        s = cv * _sig(cv)
        o_ref[...] = lax.dot_general(s, d_ref[...], (((0,), (0,)), ((), ())), preferred_element_type=F32,
                                     precision=lax.Precision.HIGHEST)

    n = dmod_sh.shape[1]
    return pl.pallas_call(body, name="ada_grad", out_shape=jax.ShapeDtypeStruct((D, n), F32),
                          compiler_params=_cp())(c_all, dmod_sh)


def norm_proj(x, norm_w, scale, shift, wcat):
    t = x.shape[0]
    tm = min(t, 1024)

    def body(x_ref, nw_ref, sc_ref, sh_ref, w_ref, p_ref, dt_ref, ht_ref, hs):
        @pl.when(pl.program_id(1) == 0)
        def _():
            xv = x_ref[...]
            r = lax.rsqrt(jnp.mean(xv * xv, axis=-1, keepdims=True) + EPS)
            h = (xv * r) * nw_ref[...]
            h = h * (1.0 + sc_ref[...]) + sh_ref[...]
            hs[...] = h.astype(BF16)
            ht_ref[...] = h.T.astype(BF16)

        p = _dot(hs[...], w_ref[...])
        p_ref[...] = p.astype(BF16)

        @pl.when(pl.program_id(1) == C_DT // TN)
        def _():
            dt_ref[...] = p[:, C_DT % TN:C_DT % TN + 128]

    vec = pl.BlockSpec((1, D), lambda i, j: (0, 0))
    return pl.pallas_call(
        body, name="norm_proj", grid=(t // tm, NP // TN),
        in_specs=[pl.BlockSpec((tm, D), lambda i, j: (i, 0)), vec, vec, vec,
                  pl.BlockSpec((D, TN), lambda i, j: (0, j))],
        out_specs=[pl.BlockSpec((tm, TN), lambda i, j: (i, j)), pl.BlockSpec((tm, 128), lambda i, j: (i, 0)),
                   pl.BlockSpec((D, tm), lambda i, j: (0, i))],
        out_shape=[jax.ShapeDtypeStruct((t, NP), BF16), jax.ShapeDtypeStruct((t, 128), F32),
                   jax.ShapeDtypeStruct((D, t), BF16)],
        scratch_shapes=[pltpu.VMEM((tm, D), BF16)],
        compiler_params=_cp(("parallel", "arbitrary")),
    )(x, norm_w, scale, shift, wcat)


def _bucket_table():
    qi = jnp.arange(BLK)[:, None]
    kj = jnp.arange(2 * BLK)[None, :]
    dist = qi + BLK - kj
    n = jnp.maximum(dist, 0)
    max_exact = NBUCKET // 2
    nf = jnp.maximum(n, 1).astype(F32)
    large = max_exact + (jnp.log(nf / max_exact) / math.log(MAXDIST / max_exact)
                         * (NBUCKET - max_exact)).astype(jnp.int32)
    large = jnp.minimum(large, NBUCKET - 1)
    bucket = jnp.where(n < max_exact, n, large).astype(jnp.int32)
    valid = (dist >= 0) & (dist < BLK)
    return jnp.where(valid, bucket, -1)


def bias_expand(rel_bias, sinks, bucket):
    def body(rb_ref, sk_ref, bk_ref, o_ref):
        hd = pl.program_id(0)
        bk = bk_ref[...]
        col = lax.broadcasted_iota(jnp.int32, (BLK, 2 * BLK), 1)

        def step(b, acc):
            return jnp.where(bk == b, rb_ref[b, hd], acc)

        acc = lax.fori_loop(0, NBUCKET, step, jnp.full((BLK, 2 * BLK), NEG, F32))
        acc = jnp.where(col == 0, sk_ref[0, hd], acc)
        o_ref[1, 0] = acc
        o_ref[0, 0] = jnp.where(jnp.logical_and(col > 0, col < BLK), NEG, acc)

    smem = pl.BlockSpec(memory_space=pltpu.SMEM)
    return pl.pallas_call(
        body, name="bias_expand", grid=(HQ,),
        in_specs=[smem, smem, _full((BLK, 2 * BLK))],
        out_specs=pl.BlockSpec((2, 1, BLK, 2 * BLK), lambda h: (0, h, 0, 0)),
        out_shape=jax.ShapeDtypeStruct((2, HQ, BLK, 2 * BLK), F32),
        compiler_params=_cp(("arbitrary",)),
    )(rel_bias, sinks, bucket)


def bias_reduce(dacc, bucket):
    col = jnp.arange(BLK * 2 * BLK, dtype=jnp.int32) % (2 * BLK)
    lane = jnp.arange(128, dtype=jnp.int32)[None, :]
    member = (bucket.reshape(-1)[:, None] == lane) | ((col[:, None] == 0) & (lane == NBUCKET))

    def body(d_ref, m_ref, o_ref):
        mm = m_ref[...]
        o_ref[...] = sum(_dot(part, mm) for part in _split3(d_ref[...]))

    return pl.pallas_call(body, name="bias_reduce", out_shape=jax.ShapeDtypeStruct((HQ, 128), F32),
                          compiler_params=_cp())(dacc.reshape(HQ, BLK * 2 * BLK), member.astype(BF16))


GQ = GRP * BLK


def _stack_heads(x, nh):
    return jnp.concatenate([x[:, DH * h:DH * (h + 1)] for h in range(nh)], axis=0)


def _unstack(xs, nh):
    rows = xs.shape[0] // nh
    return jnp.concatenate([xs[rows * h:rows * (h + 1)] for h in range(nh)], axis=1)


def _rms(x):
    return lax.rsqrt(jnp.mean(x * x, axis=-1, keepdims=True) + EPS)


def _stack_q(q, qw):
    qs = _stack_heads(q, HQ)
    r = _rms(qs)
    qhat = qs * r
    return qhat * qw, qhat, r


def _band_first(shape):
    return (lax.broadcasted_iota(jnp.int32, shape, 0) & (2 * BLK - 1)) == 0


def _stack_kv(kp, kc, vp, vc, kw):
    ks = _stack_heads(jnp.concatenate([kp, kc], axis=0), HKV)
    r = _rms(ks)
    khat = ks * r
    first = _band_first(ks.shape)
    kn = jnp.where(first, 0.0, khat * kw)
    v2 = jnp.where(first, 0.0, _stack_heads(jnp.concatenate([vp, vc], axis=0), HKV)).astype(BF16)
    return kn, khat, r, v2


def _softmax_rows(s):
    p = jnp.exp(s - jnp.max(s, axis=-1, keepdims=True))
    return p * (1.0 / _rsum(p))


def attn_fwd(proj, biasm, q_norm_w, k_norm_w):
    t = proj.shape[0]
    nb = t // BLK

    def body(q_ref, kc_ref, kp_ref, vc_ref, vp_ref, bm_ref, qw_ref, kw_ref, o_ref):
        f = lambda ref: ref[...].astype(F32)
        qn = _stack_q(f(q_ref), qw_ref[...])[0].astype(BF16)
        kn, _, _, v2 = _stack_kv(f(kp_ref), f(kc_ref), f(vp_ref), f(vc_ref), kw_ref[...])
        knb = kn.astype(BF16)
        s = jnp.concatenate([_dot_nt(qn[GQ * j:GQ * (j + 1)], knb[2 * BLK * j:2 * BLK * (j + 1)])
                             for j in range(HKV)], axis=0)
        pr = _softmax_rows(s * SCALE + bm_ref[0].reshape(HQ * BLK, 2 * BLK)).astype(BF16)
        o = jnp.concatenate([_dot(pr[GQ * j:GQ * (j + 1)], v2[2 * BLK * j:2 * BLK * (j + 1)])
                             for j in range(HKV)], axis=0)
        o_ref[...] = _unstack(o, HQ).astype(BF16)

    kblk, vblk = C_K // 256, C_V // 256
    prev = lambda n: jnp.maximum(n - 1, 0)
    return pl.pallas_call(
        body, name="attn_fwd", grid=(nb,),
        in_specs=[pl.BlockSpec((BLK, D), lambda n: (n, 0)),
                  pl.BlockSpec((BLK, 256), lambda n: (n, kblk)),
                  pl.BlockSpec((BLK, 256), lambda n: (prev(n), kblk)),
                  pl.BlockSpec((BLK, 256), lambda n: (n, vblk)),
                  pl.BlockSpec((BLK, 256), lambda n: (prev(n), vblk)),
                  pl.BlockSpec((1, HQ, BLK, 2 * BLK), lambda n: (jnp.minimum(n, 1), 0, 0, 0)),
                  _full((1, DH)), _full((1, DH))],
        out_specs=pl.BlockSpec((BLK, D), lambda n: (n, 0)),
        out_shape=jax.ShapeDtypeStruct((t, D), BF16),
        compiler_params=_cp(("parallel",)),
    )(proj, proj, proj, proj, proj, biasm, q_norm_w, k_norm_w)


def attn_bwd(proj, dao, biasm, q_norm_w, k_norm_w):
    t = proj.shape[0]
    nb = t // BLK
    kb = 2 * BLK

    def body(q_ref, kc_ref, kp_ref, vc_ref, vp_ref, do_ref, bm_ref, qw_ref, kw_ref,
             dq_ref, dk_ref, dv_ref, dqw_ref, dkw_ref, dacc_ref, ck, cv, pk, pv, nk, nv):
        n = pl.program_id(0)

        @pl.when(n == 0)
        def _():
            for ref in (dqw_ref, dkw_ref, dacc_ref, ck, cv):
                ref[...] = jnp.zeros_like(ref)

        qw = qw_ref[...]
        kw = kw_ref[...]
        f = lambda ref: ref[...].astype(F32)
        kn, khat, rk, v2 = _stack_kv(f(kp_ref), f(kc_ref), f(vp_ref), f(vc_ref), kw)
        grp = lambda a, j: a[GQ * j:GQ * (j + 1)]
        band = lambda a, j: a[kb * j:kb * (j + 1)]

        @pl.when(n < nb)
        def _():
            qn, qhat, rq = _stack_q(f(q_ref), qw)
            qnb = qn.astype(BF16)
            knb = kn.astype(BF16)
            dos = _stack_heads(f(do_ref), HQ).astype(BF16)
            s = jnp.concatenate([_dot_nt(grp(qnb, j), band(knb, j)) for j in range(HKV)], axis=0)
            pr = _softmax_rows(s * SCALE + bm_ref[0].reshape(HQ * BLK, kb))
            dp = jnp.concatenate([_dot_nt(grp(dos, j), band(v2, j)) for j in range(HKV)], axis=0)
            ds = pr * (dp - _rsum(pr * dp))
            dacc_ref[...] += ds.reshape(HQ, BLK, kb)
            dsb = ds.astype(BF16)
            prb = pr.astype(BF16)
            dqn = jnp.concatenate([_dot(grp(dsb, j), band(knb, j)) for j in range(HKV)], axis=0) * SCALE
            dqhat = dqn * qw
            dq = rq * (dqhat - qhat * jnp.mean(dqhat * qhat, axis=-1, keepdims=True))
            dq_ref[...] = _unstack(dq, HQ).astype(BF16)
            dqw_ref[...] += _csum(dqn * qhat)
            first = _band_first((kb, DH))
            for j in range(HKV):
                rows = slice(BLK * j, BLK * (j + 1))
                dkn = jnp.where(first, 0.0, _dot_tn(grp(dsb, j), grp(qnb, j)) * SCALE)
                dvj = jnp.where(first, 0.0, _dot_tn(grp(prb, j), grp(dos, j)))
                pk[rows, :] = dkn[:BLK]
                nk[rows, :] = dkn[BLK:]
                pv[rows, :] = dvj[:BLK]
                nv[rows, :] = dvj[BLK:]

        @pl.when(n == nb)
        def _():
            for ref in (pk, pv, nk, nv):
                ref[...] = jnp.zeros_like(ref)

        khp = jnp.concatenate([khat[kb * j:kb * j + BLK] for j in range(HKV)], axis=0)
        rkp = jnp.concatenate([rk[kb * j:kb * j + BLK] for j in range(HKV)], axis=0)
        dkn = ck[...] + pk[...]
        dkhat = dkn * kw
        dk = rkp * (dkhat - khp * jnp.mean(dkhat * khp, axis=-1, keepdims=True))
        dk_ref[...] = _unstack(dk, HKV).astype(BF16)
        dkw_ref[...] += _csum(dkn * khp)
        dv_ref[...] = _unstack(cv[...] + pv[...], HKV).astype(BF16)
        ck[...] = nk[...]
        cv[...] = nv[...]

    kblk, vblk = C_K // 256, C_V // 256
    cur = lambda n: jnp.minimum(n, nb - 1)
    prev = lambda n: jnp.maximum(n - 1, 0)
    carry = pltpu.VMEM((HKV * BLK, DH), F32)
    return pl.pallas_call(
        body, name="attn_bwd", grid=(nb + 1,),
        in_specs=[pl.BlockSpec((BLK, D), lambda n: (cur(n), 0)),
                  pl.BlockSpec((BLK, 256), lambda n: (cur(n), kblk)), pl.BlockSpec((BLK, 256), lambda n: (prev(n), kblk)),
                  pl.BlockSpec((BLK, 256), lambda n: (cur(n), vblk)), pl.BlockSpec((BLK, 256), lambda n: (prev(n), vblk)),
                  pl.BlockSpec((BLK, D), lambda n: (cur(n), 0)),
                  pl.BlockSpec((1, HQ, BLK, kb), lambda n: (jnp.minimum(n, 1), 0, 0, 0)),
                  _full((1, DH)), _full((1, DH))],
        out_specs=[pl.BlockSpec((BLK, D), lambda n: (cur(n), 0)),
                   pl.BlockSpec((BLK, 256), lambda n: (prev(n), 0)), pl.BlockSpec((BLK, 256), lambda n: (prev(n), 0)),
                   _full((1, DH)), _full((1, DH)), _full((HQ, BLK, kb))],
        out_shape=[jax.ShapeDtypeStruct((t, D), BF16), jax.ShapeDtypeStruct((t, 256), BF16),
                   jax.ShapeDtypeStruct((t, 256), BF16), jax.ShapeDtypeStruct((1, DH), F32),
                   jax.ShapeDtypeStruct((1, DH), F32), jax.ShapeDtypeStruct((HQ, BLK, kb), F32)],
        scratch_shapes=[carry] * 6,
        compiler_params=_cp(("arbitrary",)),
    )(proj, proj, proj, proj, proj, dao, biasm, q_norm_w, k_norm_w)


CONV_TM, CONV_CW, CONV_RC, HALO = 256, 512, 32, 16


def conv_fwd(proj, conv_w, conv_b):
    t = proj.shape[0]
    tm = min(t, CONV_TM)
    c0 = C_XBC // CONV_CW

    def body(x_ref, xp_ref, w_ref, b_ref, o_ref):
        i = pl.program_id(1)
        w = w_ref[...]
        b = b_ref[...]
        for r in range(tm // CONV_RC):
            lo = r * CONV_RC
            if r == 0:
                head = jnp.where(i == 0, 0.0, xp_ref[...].astype(F32))
                win = jnp.concatenate([head, x_ref[0:CONV_RC, :].astype(F32)], axis=0)
            else:
                win = x_ref[lo - HALO:lo + CONV_RC, :].astype(F32)
            acc = b
            for j in range(CONV_K):
                acc = acc + w[j:j + 1] * win[HALO - 3 + j:HALO - 3 + j + CONV_RC]
            o_ref[lo:lo + CONV_RC, :] = acc * _sig(acc)

    rh = tm // HALO
    return pl.pallas_call(
        body, name="conv_fwd", grid=(XBC // CONV_CW, t // tm),
        in_specs=[pl.BlockSpec((tm, CONV_CW), lambda s, i: (i, c0 + s)),
                  pl.BlockSpec((HALO, CONV_CW), lambda s, i: (jnp.maximum(i * rh - 1, 0), c0 + s)),
                  pl.BlockSpec((CONV_K, CONV_CW), lambda s, i: (0, s)), pl.BlockSpec((1, CONV_CW), lambda s, i: (0, s))],
        out_specs=pl.BlockSpec((tm, CONV_CW), lambda s, i: (i, s)),
        out_shape=jax.ShapeDtypeStruct((t, XBC), F32),
        compiler_params=_cp(("parallel", "parallel")),
    )(proj, proj, conv_w, conv_b)


def conv_bwd(proj, dact, conv_w, conv_b):
    t = proj.shape[0]
    tm = min(t, CONV_TM)
    nt = t // tm
    nr = tm // CONV_RC
    c0 = C_XBC // CONV_CW
    ext = CONV_RC + 8

    def body(x_ref, xp_ref, xn_ref, d_ref, dn_ref, w_ref, b_ref, dx_ref, dw_ref, db_ref):
        i = pl.program_id(1)

        @pl.when(i == 0)
        def _():
            dw_ref[...] = jnp.zeros_like(dw_ref)
            db_ref[...] = jnp.zeros_like(db_ref)

        w = w_ref[...]
        b = b_ref[...]
        dws = [jnp.zeros((1, CONV_CW), F32) for _ in range(CONV_K)]
        db = jnp.zeros((1, CONV_CW), F32)
        for r in range(nr):
            lo = r * CONV_RC
            parts = []
            if r == 0:
                parts.append(jnp.where(i == 0, 0.0, xp_ref[...].astype(F32)))
                parts.append(x_ref[0:CONV_RC + (HALO if nr > 1 else 0), :].astype(F32))
            else:
                parts.append(x_ref[lo - HALO:lo + CONV_RC + (HALO if r < nr - 1 else 0), :].astype(F32))
            if r == nr - 1:
                parts.append(xn_ref[...].astype(F32))
            win = jnp.concatenate(parts, axis=0)
            if r < nr - 1:
                dext = d_ref[lo:lo + ext, :]
            else:
                dext = jnp.concatenate([d_ref[lo:lo + CONV_RC, :], jnp.where(i == nt - 1, 0.0, dn_ref[...])], axis=0)
            pre = b
            for j in range(CONV_K):
                pre = pre + w[j:j + 1] * win[HALO - 3 + j:HALO - 3 + j + ext]
            sg = _sig(pre)
            dpre = dext * (sg * (1.0 + pre * (1.0 - sg)))
            dx = jnp.zeros((CONV_RC, CONV_CW), F32)
            own = dpre[0:CONV_RC]
            for j in range(CONV_K):
                dx = dx + w[j:j + 1] * dpre[3 - j:3 - j + CONV_RC]
                dws[j] = dws[j] + _csum(own * win[HALO - 3 + j:HALO - 3 + j + CONV_RC])
            db = db + _csum(own)
            dx_ref[lo:lo + CONV_RC, :] = dx.astype(BF16)
        dw_ref[...] += jnp.concatenate(dws, axis=0)
        db_ref[...] += db

    rh = tm // HALO
    r8 = tm // 8
    nxt = lambda i, per: jnp.minimum((i + 1) * per, nt * per - 1)
    return pl.pallas_call(
        body, name="conv_bwd", grid=(XBC // CONV_CW, nt),
        in_specs=[pl.BlockSpec((tm, CONV_CW), lambda s, i: (i, c0 + s)),
                  pl.BlockSpec((HALO, CONV_CW), lambda s, i: (jnp.maximum(i * rh - 1, 0), c0 + s)),
                  pl.BlockSpec((HALO, CONV_CW), lambda s, i: (nxt(i, rh), c0 + s)),
                  pl.BlockSpec((tm, CONV_CW), lambda s, i: (i, s)),
                  pl.BlockSpec((8, CONV_CW), lambda s, i: (nxt(i, r8), s)),
                  pl.BlockSpec((CONV_K, CONV_CW), lambda s, i: (0, s)), pl.BlockSpec((1, CONV_CW), lambda s, i: (0, s))],
        out_specs=[pl.BlockSpec((tm, CONV_CW), lambda s, i: (i, s)),
                   pl.BlockSpec((CONV_K, CONV_CW), lambda s, i: (0, s)), pl.BlockSpec((1, CONV_CW), lambda s, i: (0, s))],
        out_shape=[jax.ShapeDtypeStruct((t, XBC), BF16), jax.ShapeDtypeStruct((CONV_K, XBC), F32),
                   jax.ShapeDtypeStruct((1, XBC), F32)],
        compiler_params=_cp(("parallel", "arbitrary")),
    )(proj, proj, proj, dact, dact, conv_w, conv_b)


def _split3(x):
    h = x.astype(BF16)
    r = x - h.astype(F32)
    m = r.astype(BF16)
    lo = (r - m.astype(F32)).astype(BF16)
    return h, m, lo


def _tri_mm(tri, x):
    h, m, lo = _split3(x)
    return _dot(tri, h) + _dot(tri, m) + _dot(tri, lo)


def _softplus(x):
    return jnp.maximum(x, 0.0) + jnp.log1p(jnp.exp(-jnp.abs(x)))


def _chunk_decays(dt_raw, dtb, alog):
    dtv = _softplus(dt_raw + dtb)
    a = -jnp.exp(alog)
    ri = lax.broadcasted_iota(jnp.int32, (BLK, BLK), 0)
    ci = lax.broadcasted_iota(jnp.int32, (BLK, BLK), 1)
    causal = ri >= ci
    acum = _tri_mm(causal.astype(BF16), dtv * a)
    return dtv, a, causal, acum, acum.T


NPAIR = SH // 2


def _pairs(x):
    return jnp.stack([x[:, 128 * k:128 * (k + 1)] for k in range(NPAIR)])


def _unpairs(x3):
    return jnp.concatenate([x3[k] for k in range(NPAIR)], axis=1)


def _per_head_cols(m):
    return jnp.stack([jnp.broadcast_to(m[:, h:h + 1], m.shape) for h in range(SH)])


def _pair_lanes(t):
    r = t.reshape(NPAIR, 2, t.shape[1], 128)
    lo = lax.broadcasted_iota(jnp.int32, (1, t.shape[1], 128), 2) < SP
    return jnp.where(lo, r[:, 0], r[:, 1])


class _Chunk:
    pass


def _chunk_common(dt_raw, dtb, alog, dskip):
    cm = _Chunk()
    cm.dtv, cm.a, cm.causal, acum, acum_t = _chunk_decays(dt_raw, dtb, alog)
    cm.acol = _per_head_cols(acum)
    cm.arow = jnp.stack([acum_t[h:h + 1, :] for h in range(SH)])
    cm.lam = jnp.exp(jnp.where(cm.causal[None], cm.acol - cm.arow, NEG))
    apl = _pair_lanes(cm.acol)
    alast = apl[:, BLK - 1:BLK, :]
    cm.dpl = _pair_lanes(_per_head_cols(cm.dtv))
    cm.eapl = jnp.exp(apl)
    cm.epl = jnp.exp(alast - apl)
    cm.cdpl = jnp.exp(alast)
    cm.dskpl = _pair_lanes(_per_head_cols(dskip))
    cm.lo = lax.broadcasted_iota(jnp.int32, (1, BLK, 128), 2) < SP
    return cm


def ssd_fwd(act, dt_raw, dtb_p, alog_p, dsk_p):
    t = act.shape[0]
    nc = t // BLK

    def body(xs_ref, b_ref, c_ref, dt_ref, dtb_ref, al_ref, dk_ref, y_ref, sp_ref, st):
        c = pl.program_id(0)

        @pl.when(c == 0)
        def _():
            st[...] = jnp.zeros_like(st)

        s_t = st[...]
        sp_ref[0] = s_t
        cm = _chunk_common(dt_ref[...], dtb_ref[...], al_ref[...], dk_ref[...])
        gms, cbs, bts = [], [], []
        for g in range(SG):
            bf = b_ref[:, SN * g:SN * (g + 1)]
            cb = c_ref[:, SN * g:SN * (g + 1)].astype(BF16)
            gms.append(_dot_nt(cb, bf.astype(BF16)))
            cbs.append(cb)
            bts.append(bf.T.astype(BF16))
        m = (cm.lam.reshape(SG, SR, BLK, BLK) * jnp.stack(gms)[:, None]).reshape(SH, BLK, BLK).astype(BF16)
        xs16 = _pairs(xs_ref[...])
        xdt16 = xs16 * cm.dpl
        x_lo = jnp.where(cm.lo, xdt16, 0.0).astype(BF16)
        x_hi = jnp.where(cm.lo, 0.0, xdt16).astype(BF16)
        s16 = _pairs(s_t)
        s16b = s16.astype(BF16)
        yd = jnp.stack([_dot(m[2 * k], x_lo[k]) + _dot(m[2 * k + 1], x_hi[k]) for k in range(NPAIR)])
        yo = jnp.stack([_dot(cbs[k // (NPAIR // SG)], s16b[k]) for k in range(NPAIR)])
        y_ref[...] = _unpairs(yd + yo * cm.eapl + cm.dskpl * xs16).astype(BF16)
        xe = (xdt16 * cm.epl).astype(BF16)
        st[...] = _unpairs(cm.cdpl * s16 + jnp.stack([_dot(bts[k // (NPAIR // SG)], xe[k]) for k in range(NPAIR)]))

    vec = _full((1, 128))
    return pl.pallas_call(
        body, name="ssd_fwd", grid=(nc,),
        in_specs=[pl.BlockSpec((BLK, SSM_W), lambda c: (c, 0)),
                  pl.BlockSpec((BLK, SG * SN), lambda c: (c, SSM_W // (SG * SN))),
                  pl.BlockSpec((BLK, SG * SN), lambda c: (c, SSM_W // (SG * SN) + 1)),
                  pl.BlockSpec((BLK, 128), lambda c: (c, 0)), vec, vec, vec],
        out_specs=[pl.BlockSpec((BLK, SSM_W), lambda c: (c, 0)), pl.BlockSpec((1, SN, SSM_W), lambda c: (c, 0, 0))],
        out_shape=[jax.ShapeDtypeStruct((t, SSM_W), BF16), jax.ShapeDtypeStruct((nc, SN, SSM_W), F32)],
        scratch_shapes=[pltpu.VMEM((SN, SSM_W), F32)],
        compiler_params=_cp(("arbitrary",)),
    )(act, act, act, dt_raw, dtb_p, alog_p, dsk_p)


def _head_sums(q):
    r = q.shape[1]
    lo = lax.broadcasted_iota(jnp.int32, (1, r, 128), 2) < SP
    s_lo = jnp.sum(jnp.where(lo, q, 0.0), axis=-1, keepdims=True)
    s_hi = jnp.sum(jnp.where(lo, 0.0, q), axis=-1, keepdims=True)
    lane = lax.broadcasted_iota(jnp.int32, (r, 128), 1)
    out = jnp.zeros((r, 128), F32)
    for k in range(NPAIR):
        out = jnp.where(lane == 2 * k, s_lo[k], jnp.where(lane == 2 * k + 1, s_hi[k], out))
    return out


def ssd_bwd(act, dt_raw, dy, sprev, dtb_p, alog_p, dsk_p):
    t = act.shape[0]
    nc = t // BLK

    def body(xs_ref, b_ref, c_ref, dt_ref, dy_ref, sp_ref, dtb_ref, al_ref, dk_ref,
             da_ref, ddt_ref, ddtb_ref, dal_ref, ddk_ref, dst):
        i = pl.program_id(0)

        @pl.when(i == 0)
        def _():
            dst[...] = jnp.zeros_like(dst)
            ddtb_ref[...] = jnp.zeros_like(ddtb_ref)
            dal_ref[...] = jnp.zeros_like(dal_ref)
            ddk_ref[...] = jnp.zeros_like(ddk_ref)

        dt_raw = dt_ref[...]
        dtb = dtb_ref[...]
        cm = _chunk_common(dt_raw, dtb, al_ref[...], dk_ref[...])
        ri = lax.broadcasted_iota(jnp.int32, (BLK, BLK), 0)
        ci = lax.broadcasted_iota(jnp.int32, (BLK, BLK), 1)
        lam_t = jnp.exp(jnp.where((ri <= ci)[None], cm.arow - cm.acol, NEG))
        bbs, cbs, cts, gms = [], [], [], []
        for g in range(SG):
            bf = b_ref[:, SN * g:SN * (g + 1)]
            cf = c_ref[:, SN * g:SN * (g + 1)]
            bbs.append(bf.astype(BF16))
            cbs.append(cf.astype(BF16))
            cts.append(cf.T.astype(BF16))
            gms.append(_dot_nt(bbs[g], cbs[g]))
        grp = lambda k: k // (NPAIR // SG)
        xs16 = _pairs(xs_ref[...])
        dy16 = _pairs(dy_ref[...].astype(F32))
        sp16 = _pairs(sp_ref[0])
        ds16 = _pairs(dst[...])
        xdt16 = xs16 * cm.dpl
        xdtb = xdt16.astype(BF16)
        dyh = [jnp.where(cm.lo, dy16, 0.0).astype(BF16), jnp.where(cm.lo, 0.0, dy16).astype(BF16)]
        m_t = (lam_t.reshape(SG, SR, BLK, BLK) * jnp.stack(gms)[:, None]).reshape(SH, BLK, BLK).astype(BF16)
        dxdt = jnp.stack([_dot(m_t[2 * k], dyh[0][k]) + _dot(m_t[2 * k + 1], dyh[1][k]) for k in range(NPAIR)])
        dm = jnp.stack([_dot_nt(dyh[h % 2][h // 2], xdtb[h // 2]) for h in range(SH)])
        dgl = (dm * cm.lam).reshape(SG, SR, BLK, BLK)
        dg = jnp.sum(dgl, axis=1).astype(BF16)
        w = (dgl * jnp.stack([_dot_nt(cbs[g], bbs[g]) for g in range(SG)])[:, None]).reshape(SH, BLK, BLK)
        w_rows = jnp.sum(w, axis=2, keepdims=True)
        w_cols = jnp.concatenate([jnp.sum(w, axis=1)] + [jnp.zeros((128 - SH, BLK), F32)], axis=0).T
        lane_c = lax.broadcasted_iota(jnp.int32, (BLK, 128), 1)
        da_cols = -w_cols
        for h in range(SH):
            da_cols = jnp.where(lane_c == h, da_cols + w_rows[h], da_cols)
        ds16b = ds16.astype(BF16)
        sp16b = sp16.astype(BF16)
        dxs = jnp.stack([_dot(bbs[grp(k)], ds16b[k]) for k in range(NPAIR)]) * cm.epl
        dxdt = dxdt + dxs
        dya = (dy16 * cm.eapl).astype(BF16)
        xe = (xdt16 * cm.epl).astype(BF16)
        dcs, dbs = [], []
        for g in range(SG):
            ks = range(g * (NPAIR // SG), (g + 1) * (NPAIR // SG))
            dcs.append(sum(_dot_nt(dya[k], sp16b[k]) for k in ks) + _dot(dg[g], bbs[g]))
            dbs.append(sum(_dot_nt(xe[k], ds16b[k]) for k in ks) + _dot_tn(dg[g], cbs[g]))
        dst[...] = _unpairs(cm.cdpl * ds16 + jnp.stack([_dot(cts[grp(k)], dya[k]) for k in range(NPAIR)]))
        da_ref[...] = jnp.concatenate([_unpairs(dxdt * cm.dpl + cm.dskpl * dy16)] + dbs + dcs, axis=1)
        y_off = jnp.stack([_dot(cbs[grp(k)], sp16b[k]) for k in range(NPAIR)]) * cm.eapl
        da_cols = da_cols + _head_sums(dy16 * y_off - xdt16 * dxs)
        last = _head_sums(jnp.sum(xdt16 * dxs, axis=1, keepdims=True)
                          + cm.cdpl * jnp.sum(ds16 * sp16, axis=1, keepdims=True))
        ddt = _head_sums(dxdt * xs16)
        row_i = lax.broadcasted_iota(jnp.int32, (BLK, 128), 0)
        dacum = da_cols + jnp.where(row_i == BLK - 1, last, 0.0)
        dda = _tri_mm((ri <= ci).astype(BF16), dacum)
        ddt = ddt + dda * cm.a
        dal_ref[...] += _csum(dda * cm.dtv) * cm.a
        ddt_raw = jnp.where(lane_c < SH, ddt * _sig(dt_raw + dtb), 0.0)
        ddt_ref[...] = ddt_raw.astype(BF16)
        ddtb_ref[...] += _csum(ddt_raw)
        ddk_ref[...] += _head_sums(jnp.sum(dy16 * xs16, axis=1, keepdims=True))

    rev = lambda i: nc - 1 - i
    vec = _full((1, 128))
    slab = pl.BlockSpec((BLK, SSM_W), lambda i: (rev(i), 0))
    return pl.pallas_call(
        body, name="ssd_bwd", grid=(nc,),
        in_specs=[slab,
                  pl.BlockSpec((BLK, SG * SN), lambda i: (rev(i), SSM_W // (SG * SN))),
                  pl.BlockSpec((BLK, SG * SN), lambda i: (rev(i), SSM_W // (SG * SN) + 1)),
                  pl.BlockSpec((BLK, 128), lambda i: (rev(i), 0)),
                  slab,
                  pl.BlockSpec((1, SN, SSM_W), lambda i: (rev(i), 0, 0)), vec, vec, vec],
        out_specs=[pl.BlockSpec((BLK, XBC), lambda i: (rev(i), 0)), pl.BlockSpec((BLK, 128), lambda i: (rev(i), 0)),
                   vec, vec, vec],
        out_shape=[jax.ShapeDtypeStruct((t, XBC), F32), jax.ShapeDtypeStruct((t, 128), BF16),
                   jax.ShapeDtypeStruct((1, 128), F32), jax.ShapeDtypeStruct((1, 128), F32),
                   jax.ShapeDtypeStruct((1, 128), F32)],
        scratch_shapes=[pltpu.VMEM((SN, SSM_W), F32)],
        compiler_params=_cp(("arbitrary",)),
    )(act, act, act, dt_raw, dy, sprev, dtb_p, alog_p, dsk_p)


TAIL_TM = 128


def _dsilu(z, s):
    return s * (1.0 + z * (1.0 - s))


def tail(proj, ao, yss, x, target, gate, ssm_nw, w_at, w_ss, w_ou):
    t = x.shape[0]
    tm = min(t, TAIL_TM)
    gw = SSM_W // SG

    def body(ao_ref, za_ref, ga_ref, gb_ref, zm_ref, ys_ref, x_ref, tg_ref, gt_ref, nw_ref, wa_ref, ws_ref, wo_ref,
             loss_ref, dy_ref, dao_ref, dza_ref, dga_ref, dgb_ref, dys_ref, dzm_ref,
             ua_ref, yn_ref, mg_ref, dya_ref, dyb_ref, do_ref, dgt_ref, dnw_ref):
        i = pl.program_id(0)

        @pl.when(i == 0)
        def _():
            loss_ref[...] = jnp.zeros_like(loss_ref)
            dgt_ref[...] = jnp.zeros_like(dgt_ref)
            dnw_ref[...] = jnp.zeros_like(dnw_ref)

        ao = ao_ref[...].astype(F32)
        za = za_ref[...].astype(F32)
        sa = _sig(za)
        sila = za * sa
        ua_f = ao * sila
        ua = ua_f.astype(BF16)
        ya = _dot(ua, wa_ref[...])
        zm = zm_ref[...].astype(F32)
        sm = _sig(zm)
        silm = zm * sm
        ys = ys_ref[...].astype(F32)
        u = ys * silm
        nw = nw_ref[...]
        rs, uns = [], []
        for g in range(SG):
            ug = u[:, gw * g:gw * (g + 1)]
            r = lax.rsqrt(jnp.mean(ug * ug, axis=-1, keepdims=True) + EPS)
            rs.append(r)
            uns.append(ug * r)
        un = jnp.concatenate(uns, axis=1)
        yn_f = un * nw
        yn = yn_f.astype(BF16)
        yb = _dot(yn, ws_ref[...])
        sga = _sig(ga_ref[...].astype(F32))
        sgb = _sig(gb_ref[...].astype(F32))
        mg_f = sga * ya + sgb * yb
        mg = mg_f.astype(BF16)
        o = _dot(mg, wo_ref[...])
        gt = gt_ref[...]
        err = (x_ref[...] + gt * o) - tg_ref[...]
        lane = lax.broadcasted_iota(jnp.int32, (1, 128), 1)
        loss_ref[...] += jnp.where(lane == 0, 0.5 * _asum(_rsum(err * err) / D), 0.0)
        dy = err * (1.0 / D)
        dy_ref[...] = dy
        dgt_ref[...] += _csum(dy * o)
        do = (dy * gt).astype(BF16)
        dmg = _dot_nt(do, wo_ref[...])
        dga_ref[...] = (dmg * ya * sga * (1.0 - sga)).astype(BF16)
        dgb_ref[...] = (dmg * yb * sgb * (1.0 - sgb)).astype(BF16)
        dya = (dmg * sga).astype(BF16)
        dyb = (dmg * sgb).astype(BF16)
        dua = _dot_nt(dya, wa_ref[...])
        dao_ref[...] = (dua * sila).astype(BF16)
        dza_ref[...] = (dua * ao * _dsilu(za, sa)).astype(BF16)
        dyn = _dot_nt(dyb, ws_ref[...])
        dnw_ref[...] += _csum(dyn * un)
        dun = dyn * nw
        dus = []
        for g in range(SG):
            gs = slice(gw * g, gw * (g + 1))
            dus.append(rs[g] * (dun[:, gs] - uns[g] * jnp.mean(dun[:, gs] * uns[g], axis=-1, keepdims=True)))
        du = jnp.concatenate(dus, axis=1)
        dys_ref[...] = (du * silm).astype(BF16)
        dzm_ref[...] = (du * ys * _dsilu(zm, sm)).astype(BF16)
        ua_ref[...] = ua_f.T.astype(BF16)
        yn_ref[...] = yn_f.T.astype(BF16)
        mg_ref[...] = mg_f.T.astype(BF16)
        dya_ref[...] = dya
        dyb_ref[...] = dyb
        do_ref[...] = do

    row = lambda w: pl.BlockSpec((tm, w), lambda i: (i, 0))
    pcol = lambda w, c0: pl.BlockSpec((tm, w), lambda i: (i, c0 // w))
    sd = lambda w, dt: jax.ShapeDtypeStruct((t, w), dt)
    colt = lambda w: pl.BlockSpec((w, tm), lambda i: (0, i))
    sdt = lambda w: jax.ShapeDtypeStruct((w, t), BF16)
    return pl.pallas_call(
        body, name="tail", grid=(t // tm,),
        in_specs=[row(D), pcol(D, C_ZA), pcol(D, C_GA), pcol(D, C_GB), pcol(SSM_W, C_ZM), row(SSM_W), row(D), row(D),
                  _full((1, D)), _full((1, SSM_W)), _full((D, D)), _full((SSM_W, D)), _full((D, D))],
        out_specs=[_full((1, 128)), row(D), row(D), row(D), row(D), row(D), row(SSM_W), row(SSM_W),
                   colt(D), colt(SSM_W), colt(D), row(D), row(D), row(D), _full((1, D)), _full((1, SSM_W))],
        out_shape=[jax.ShapeDtypeStruct((1, 128), F32), sd(D, F32), sd(D, BF16), sd(D, BF16), sd(D, BF16), sd(D, BF16),
                   sd(SSM_W, BF16), sd(SSM_W, BF16), sdt(D), sdt(SSM_W), sdt(D), sd(D, BF16),
                   sd(D, BF16), sd(D, BF16), jax.ShapeDtypeStruct((1, D), F32), jax.ShapeDtypeStruct((1, SSM_W), F32)],
        compiler_params=_cp(("arbitrary",)),
    )(ao, proj, proj, proj, proj, yss, x, target, gate, ssm_nw, w_at, w_ss, w_ou)


def dproj_bwd(dproj, wcat, x, dy, norm_w, scale):
    t = x.shape[0]
    tm = min(t, 512)
    tk = NP // 6
    nk = NP // tk
    nt = t // tm

    def body(dp_ref, w_ref, x_ref, dy_ref, nw_ref, sc_ref, gx_ref, dnw_ref, dsc_ref, dsh_ref, acc, dwe_ref):
        i = pl.program_id(0)
        k = pl.program_id(1)

        @pl.when(jnp.logical_and(i == 0, k == 0))
        def _():
            dwe_ref[...] = jnp.zeros_like(dwe_ref)
            dsh_ref[...] = jnp.zeros_like(dsh_ref)
            dnw_ref[...] = jnp.zeros_like(dnw_ref)
            dsc_ref[...] = jnp.zeros_like(dsc_ref)

        part = _dot_nt(dp_ref[...], w_ref[...])

        @pl.when(k == 0)
        def _():
            acc[...] = part

        @pl.when(k > 0)
        def _():
            acc[...] += part

        @pl.when(k == nk - 1)
        def _():
            dh = acc[...]
            xv = x_ref[...]
            r = lax.rsqrt(jnp.mean(xv * xv, axis=-1, keepdims=True) + EPS)
            xn = xv * r
            weff = nw_ref[...] * (1.0 + sc_ref[...])
            dxn = dh * weff
            gx_ref[...] = dy_ref[...] + r * (dxn - xn * jnp.mean(dxn * xn, axis=-1, keepdims=True))
            dwe_ref[...] += _csum(dh * xn)
            dsh_ref[...] += _csum(dh)

        @pl.when(jnp.logical_and(i == nt - 1, k == nk - 1))
        def _():
            dwe = dwe_ref[...]
            dnw_ref[...] = dwe * (1.0 + sc_ref[...])
            dsc_ref[...] = dwe * nw_ref[...]

    vec = pl.BlockSpec((1, D), lambda i, k: (0, 0))
    row = pl.BlockSpec((tm, D), lambda i, k: (i, 0))
    return pl.pallas_call(
        body, name="dproj_bwd", grid=(nt, nk),
        in_specs=[pl.BlockSpec((tm, tk), lambda i, k: (i, k)), pl.BlockSpec((D, tk), lambda i, k: (0, k)),
                  row, row, vec, vec],
        out_specs=[row, vec, vec, vec],
        out_shape=[jax.ShapeDtypeStruct((t, D), F32), jax.ShapeDtypeStruct((1, D), F32),
                   jax.ShapeDtypeStruct((1, D), F32), jax.ShapeDtypeStruct((1, D), F32)],
        scratch_shapes=[pltpu.VMEM((tm, D), F32), pltpu.VMEM((1, D), F32)],
        compiler_params=_cp(("arbitrary", "arbitrary")),
    )(dproj, wcat, x, dy, norm_w, scale)


def wgrad(at, b, name, bn):
    m, t = at.shape
    n = b.shape[1]
    tk = min(t, 1024)
    bm = min(m, 1024)

    def body(a_ref, b_ref, o_ref):
        part = _dot(a_ref[...], b_ref[...])

        @pl.when(pl.program_id(2) == 0)
        def _():
            o_ref[...] = part

        @pl.when(pl.program_id(2) > 0)
        def _():
            o_ref[...] += part

    return pl.pallas_call(
        body, name=name, grid=(m // bm, n // bn, t // tk),
        in_specs=[pl.BlockSpec((bm, tk), lambda i, j, k: (i, k)), pl.BlockSpec((tk, bn), lambda i, j, k: (k, j))],
        out_specs=pl.BlockSpec((bm, bn), lambda i, j, k: (i, j)),
        out_shape=jax.ShapeDtypeStruct((m, n), F32),
        compiler_params=_cp(("parallel", "parallel", "arbitrary")),
    )(at, b)


SUM_TR = 256


def pair_sum(g, core, theirs, name):
    w = g.shape[2]
    nh = HROWS // SUM_TR

    def body(core_ref, a_ref, b_ref, o_ref, ob_ref):
        s = a_ref[...] + b_ref[...]
        o_ref[...] = s
        ob_ref[...] = s.astype(BF16)

    spec = pl.BlockSpec((1, SUM_TR, w), lambda d, i, c: (d, i, 0))
    return pl.pallas_call(
        body, name=name,
        out_shape=[jax.ShapeDtypeStruct((4, HROWS, w), F32), jax.ShapeDtypeStruct((4, HROWS, w), BF16)],
        grid_spec=pltpu.PrefetchScalarGridSpec(
            num_scalar_prefetch=1, grid=(4, nh),
            in_specs=[pl.BlockSpec((1, SUM_TR, w), lambda d, i, c: (d, c[0] * nh + i, 0)), spec],
            out_specs=[spec, spec]),
        compiler_params=_cp(("parallel", "parallel")))(core.reshape(1).astype(jnp.int32), g, theirs)


def chip_sum(part, chip, others, name):
    r, w = part.shape[1:]

    def body(chip_ref, a_ref, b_ref, o_ref):
        acc = a_ref[0]
        for k in range(3):
            acc = acc + b_ref[k].astype(F32)
        o_ref[...] = acc

    return pl.pallas_call(
        body, name=name, out_shape=jax.ShapeDtypeStruct((r, w), F32),
        grid_spec=pltpu.PrefetchScalarGridSpec(
            num_scalar_prefetch=1, grid=(r // SUM_TR,),
            in_specs=[pl.BlockSpec((1, SUM_TR, w), lambda i, c: (c[0], i, 0)),
                      pl.BlockSpec((3, SUM_TR, w), lambda i, c: (0, i, 0))],
            out_specs=pl.BlockSpec((SUM_TR, w), lambda i, c: (i, 0))),
        compiler_params=_cp(("parallel",)))(chip.reshape(1).astype(jnp.int32), part, others)


def sum_devices(g):
    r = g.shape[1]

    def body(g_ref, o_ref):
        acc = g_ref[0]
        for d in range(1, 8):
            acc = acc + g_ref[d]
        o_ref[...] = acc

    return pl.pallas_call(body, name="sum_devices", out_shape=jax.ShapeDtypeStruct((r, 1024), F32),
                          compiler_params=_cp())(g)


def adamw(w, g, m, v, name):
    r, c = w.shape
    tr = r
    for cand in (256, 128, 64, 32, 16, 8):
        if r % cand == 0 and r > cand:
            tr = cand
            break

    def body(w_ref, g_ref, m_ref, v_ref, d_ref, nm_ref, nv_ref):
        gv = g_ref[...]
        mn = ADAM_B1 * m_ref[...] + (1.0 - ADAM_B1) * gv
        vn = ADAM_B2 * v_ref[...] + (1.0 - ADAM_B2) * (gv * gv)
        m_hat = mn / (1.0 - ADAM_B1 ** ADAM_STEP)
        v_hat = vn / (1.0 - ADAM_B2 ** ADAM_STEP)
        d_ref[...] = -ADAM_LR * (m_hat / (jnp.sqrt(v_hat) + ADAM_EPS) + ADAM_WD * w_ref[...])
        nm_ref[...] = mn
        nv_ref[...] = vn

    spec = pl.BlockSpec((tr, c), lambda i: (i, 0))
    sd = jax.ShapeDtypeStruct((r, c), F32)
    return pl.pallas_call(body, name=name, grid=(r // tr,), in_specs=[spec] * 4, out_specs=[spec] * 3,
                          out_shape=[sd, sd, sd], compiler_params=_cp(("parallel",)))(w, g, m, v)


ANY = pl.BlockSpec(memory_space=pl.ANY)
VM = pl.BlockSpec(memory_space=pltpu.VMEM)
OTHER_CHIPS = ((1, 0), (0, 1), (1, 1))


def _pos():
    return lax.axis_index("x"), lax.axis_index("y"), lax.axis_index("c")


def _flip(v, bit):
    return 1 - v if bit else v


def _rcopy(src, dst, ssem, rsem, peer):
    return pltpu.make_async_remote_copy(src_ref=src, dst_ref=dst, send_sem=ssem, recv_sem=rsem,
                                        device_id=peer, device_id_type=MESH)


def allgather_small(p, name):
    r = p.shape[0]

    def body(in_ref, out_ref, ssem, rsem, lsem):
        x, y, c = _pos()
        me = 4 * x + 2 * y + c
        loc = pltpu.make_async_copy(in_ref, out_ref.at[me], lsem)
        loc.start()
        sends = []
        peers = []
        for k in range(1, 8):
            px, py, pc = _flip(x, (k >> 2) & 1), _flip(y, (k >> 1) & 1), _flip(c, k & 1)
            peers.append((px, py, pc))
            cp = _rcopy(in_ref, out_ref.at[me], ssem.at[k - 1], rsem.at[k - 1], (px, py, pc))
            cp.start()
            sends.append(cp)
        for k in range(1, 8):
            px, py, pc = peers[k - 1]
            _rcopy(in_ref, out_ref.at[4 * px + 2 * py + pc], ssem.at[k - 1], rsem.at[k - 1], (px, py, pc)).wait_recv()
        for cp in sends:
            cp.wait_send()
        loc.wait()

    return pl.pallas_call(
        body, name=name, out_shape=jax.ShapeDtypeStruct((8, r, 1024), F32),
        in_specs=[VM], out_specs=VM,
        scratch_shapes=[pltpu.SemaphoreType.DMA((7,)), pltpu.SemaphoreType.DMA((7,)), pltpu.SemaphoreType.DMA],
    )(p)


def gather_weights(w_in_b, w_rest_b, mod_sh):
    def body(wi_ref, wr_ref, m_ref, gi_ref, gr_ref, mo_ref, ssem, rsem, lsem):
        x, y, c = _pos()
        chip = 2 * x + y
        mine = pl.ds(pl.multiple_of(c * HROWS, 16), HROWS)
        other = pl.ds(pl.multiple_of((1 - c) * HROWS, 16), HROWS)
        sib = (x, y, 1 - c)
        pairs = ((wi_ref, gi_ref), (wr_ref, gr_ref))
        loc_m = pltpu.make_async_copy(m_ref, mo_ref.at[chip], lsem)
        loc_m.start()
        sends = []
        for k, (fx, fy) in enumerate(OTHER_CHIPS):
            peer = (_flip(x, fx), _flip(y, fy), c)
            for a, (w_ref, g_ref) in enumerate(pairs):
                cw = _rcopy(w_ref.at[mine], g_ref.at[chip, mine], ssem.at[6 * a + k], rsem.at[6 * a + k], peer)
                cw.start()
                sends.append(cw)
            cm = _rcopy(m_ref, mo_ref.at[chip], ssem.at[12 + k], rsem.at[12 + k], peer)
            cm.start()
            sends.append(cm)
        for k, (fx, fy) in enumerate(OTHER_CHIPS):
            px, py = _flip(x, fx), _flip(y, fy)
            for a, (w_ref, g_ref) in enumerate(pairs):
                got = g_ref.at[2 * px + py, mine]
                _rcopy(w_ref.at[mine], got, ssem.at[6 * a + k], rsem.at[6 * a + k], (px, py, c)).wait_recv()
                fw = _rcopy(got, got, ssem.at[6 * a + 3 + k], rsem.at[6 * a + 3 + k], sib)
                fw.start()
                sends.append(fw)
        for k, (fx, fy) in enumerate(OTHER_CHIPS):
            px, py = _flip(x, fx), _flip(y, fy)
            for a, (w_ref, g_ref) in enumerate(pairs):
                land = g_ref.at[2 * px + py, other]
                _rcopy(land, land, ssem.at[6 * a + 3 + k], rsem.at[6 * a + 3 + k], sib).wait_recv()
            _rcopy(m_ref, mo_ref.at[2 * px + py], ssem.at[12 + k], rsem.at[12 + k], (px, py, c)).wait_recv()
        for cp in sends:
            cp.wait_send()
        loc_m.wait()

    return pl.pallas_call(
        body, name="gather_weights",
        out_shape=[jax.ShapeDtypeStruct((4, D, SH_IN), BF16), jax.ShapeDtypeStruct((4, D, D), BF16),
                   jax.ShapeDtypeStruct((4, 8, 768), F32)],
        in_specs=[ANY, ANY, VM], out_specs=[ANY, ANY, VM],
        scratch_shapes=[pltpu.SemaphoreType.DMA((15,)), pltpu.SemaphoreType.DMA((15,)), pltpu.SemaphoreType.DMA],
    )(w_in_b, w_rest_b, mod_sh)


def pair_exchange(g_in, g_rest):
    def body(gi_ref, gr_ref, ri_ref, rr_ref, ssem, rsem):
        x, y, c = _pos()
        other = pl.ds(pl.multiple_of((1 - c) * HROWS, 8), HROWS)
        cps = [_rcopy(g_ref.at[:, other, :], r_ref, ssem.at[a], rsem.at[a], (x, y, 1 - c))
               for a, (g_ref, r_ref) in enumerate(((gi_ref, ri_ref), (gr_ref, rr_ref)))]
        for cp in cps:
            cp.start()
        for cp in cps:
            cp.wait()

    return pl.pallas_call(
        body, name="pair_exchange",
        out_shape=[jax.ShapeDtypeStruct((4, HROWS, SH_IN), F32), jax.ShapeDtypeStruct((4, HROWS, D), F32)],
        in_specs=[ANY, ANY], out_specs=[ANY, ANY],
        scratch_shapes=[pltpu.SemaphoreType.DMA((2,)), pltpu.SemaphoreType.DMA((2,))],
    )(g_in, g_rest)


def chip_exchange(pb_in, pb_rest):
    def body(pi_ref, pr_ref, ri_ref, rr_ref, ssem, rsem):
        x, y, c = _pos()
        chip = 2 * x + y
        pairs = ((pi_ref, ri_ref), (pr_ref, rr_ref))
        sends = []
        for k, (fx, fy) in enumerate(OTHER_CHIPS):
            px, py = _flip(x, fx), _flip(y, fy)
            for a, (p_ref, r_ref) in enumerate(pairs):
                cp = _rcopy(p_ref.at[2 * px + py], r_ref.at[k], ssem.at[3 * a + k], rsem.at[3 * a + k], (px, py, c))
                cp.start()
                sends.append(cp)
        for k, (fx, fy) in enumerate(OTHER_CHIPS):
            px, py = _flip(x, fx), _flip(y, fy)
            for a, (p_ref, r_ref) in enumerate(pairs):
                _rcopy(p_ref.at[chip], r_ref.at[k], ssem.at[3 * a + k], rsem.at[3 * a + k], (px, py, c)).wait_recv()
        for cp in sends:
            cp.wait_send()

    return pl.pallas_call(
        body, name="chip_exchange",
        out_shape=[jax.ShapeDtypeStruct((3, HROWS, SH_IN), BF16), jax.ShapeDtypeStruct((3, HROWS, D), BF16)],
        in_specs=[ANY, ANY], out_specs=[ANY, ANY],
        scratch_shapes=[pltpu.SemaphoreType.DMA((6,)), pltpu.SemaphoreType.DMA((6,))],
    )(pb_in, pb_rest)


def pair_swap(red_in, red_rest):
    def body(ai_ref, ar_ref, oi_ref, or_ref, ssem, rsem):
        x, y, c = _pos()
        cps = [_rcopy(a_ref, o_ref, ssem.at[a], rsem.at[a], (x, y, 1 - c))
               for a, (a_ref, o_ref) in enumerate(((ai_ref, oi_ref), (ar_ref, or_ref)))]
        for cp in cps:
            cp.start()
        for cp in cps:
            cp.wait()

    return pl.pallas_call(
        body, name="pair_swap",
        out_shape=[jax.ShapeDtypeStruct((HROWS, SH_IN), F32), jax.ShapeDtypeStruct((HROWS, D), F32)],
        in_specs=[ANY, ANY], out_specs=[ANY, ANY],
        scratch_shapes=[pltpu.SemaphoreType.DMA((2,)), pltpu.SemaphoreType.DMA((2,))],
    )(red_in, red_rest)


def _row(v, width=1024):
    v = v.reshape(-1)
    n = -(-v.shape[0] // width) * width
    return jnp.pad(v, (0, n - v.shape[0])).reshape(-1, width)


def _slots(vs):
    row = [jnp.pad(v.reshape(-1), (0, 128 - v.size)) for v in vs]
    row += [jnp.zeros((128,), F32)] * (8 - len(row))
    return jnp.concatenate(row).reshape(1, 1024)


def _pack_small(b_ada, norm_w, conv_b, ssm_norm_w, q_norm_w, k_norm_w, sinks, dt_bias, a_log, d_skip, rel_bias,
                extra=None):
    misc = [q_norm_w, k_norm_w, sinks, dt_bias, a_log, d_skip] + ([] if extra is None else [extra])
    rows = [_row(b_ada), _row(norm_w), _row(conv_b), _row(ssm_norm_w), _slots(misc), _row(rel_bias)]
    rows.append(jnp.zeros((5, 1024), F32))
    return jnp.concatenate(rows, axis=0)


def _unpack_small(p):
    misc = p[9]
    return dict(b_ada=p[0:3].reshape(1, 3072), norm_w=p[3:4], conv_b=p[4:7].reshape(1, 3072),
                ssm_norm_w=p[7:9].reshape(1, 2048), q_norm_w=misc[None, 0:64], k_norm_w=misc[None, 128:192],
                sinks=misc[None, 256:272], dt_bias=misc[None, 384:416], a_log=misc[None, 512:544],
                d_skip=misc[None, 640:672], rel_bias=p[10, :512].reshape(32, 16), extra=misc[768])


SMALL = ("b_ada", "norm_w", "conv_b", "ssm_norm_w", "q_norm_w", "k_norm_w", "sinks", "dt_bias", "a_log", "d_skip",
         "rel_bias")
WEIGHTS = ("w_ada", "b_ada", "norm_w", "w_in", "q_norm_w", "k_norm_w", "rel_bias", "sinks", "conv_w", "conv_b",
           "dt_bias", "a_log", "d_skip", "ssm_norm_w", "w_attn_proj", "w_ssm_proj", "w_out")
IN_COLS = ((0, 1024, C_Q), (1024, 256, C_K), (1280, 256, C_V), (1536, 1024, C_ZA), (2560, 2048, C_ZM),
           (4608, 3072, C_XBC), (7680, 32, C_DT), (7712, 1024, C_GA), (8736, 1024, C_GB))


def _to_cat(shards):
    parts, pos = [], 0
    for o, n, cnew in sorted(IN_COLS, key=lambda e: e[2]):
        assert cnew == pos
        c0 = o
        while c0 < o + n:
            i = c0 // SH_IN
            c1 = min(o + n, (i + 1) * SH_IN)
            parts.append(shards[i][:, c0 - i * SH_IN:c1 - i * SH_IN])
            c0 = c1
        pos += n
    parts.append(jnp.zeros((D, NP - pos), shards.dtype))
    return jnp.concatenate(parts, axis=1)


def _from_cat(w_cat):
    shards = []
    for i in range(4):
        lo, hi = i * SH_IN, (i + 1) * SH_IN
        parts = []
        for o, n, cnew in IN_COLS:
            a, b = max(o, lo), min(o + n, hi)
            if a < b:
                parts.append(w_cat[:, cnew + a - o:cnew + b - o])
        shards.append(jnp.concatenate(parts, axis=1))
    return jnp.stack(shards)


def kernel(x, c, w_ada, b_ada, norm_w, w_in, q_norm_w, k_norm_w, rel_bias, sinks, conv_w, conv_b, dt_bias, a_log, d_skip, ssm_norm_w, w_attn_proj, w_ssm_proj, w_out, loss_target, m_w_ada, m_b_ada, m_norm_w, m_w_in, m_q_norm_w, m_k_norm_w, m_rel_bias, m_sinks, m_conv_w, m_conv_b, m_dt_bias, m_a_log, m_d_skip, m_ssm_norm_w, m_w_attn_proj, m_w_ssm_proj, m_w_out, v_w_ada, v_b_ada, v_norm_w, v_w_in, v_q_norm_w, v_k_norm_w, v_rel_bias, v_sinks, v_conv_w, v_conv_b, v_dt_bias, v_a_log, v_d_skip, v_ssm_norm_w, v_w_attn_proj, v_w_ssm_proj, v_w_out):
    args = dict(locals())
    xi, yi, ci = lax.axis_index("x"), lax.axis_index("y"), lax.axis_index("c")
    chip = 2 * xi + yi
    me = 4 * xi + 2 * yi + ci
    x2 = x[0]
    tgt = loss_target[0]

    pay = jnp.concatenate([c, conv_w[0].reshape(3, 1024), jnp.zeros((4, 1024), F32)], axis=0)
    g0 = allgather_small(pay, "gather_cond")
    c_all = g0[:, 0, :]
    conv_w_full = g0[0::2, 1:4, :].reshape(4, CONV_K, 768).transpose(1, 0, 2).reshape(CONV_K, XBC)

    b_ada_sh = lax.dynamic_slice(b_ada, (0, chip * 768), (1, 768))
    mod_sh = ada_mod(c_all, w_ada[0], b_ada_sh)

    w_in_b = w_in[0].astype(BF16)
    w_rest_b = jnp.concatenate([w_attn_proj[0], w_ssm_proj[0], w_out[0]], axis=0).astype(BF16)
    wg_in, wg_rest, modg = gather_weights(w_in_b, w_rest_b, mod_sh)
    wg_in = lax.dynamic_update_slice(wg_in, w_in_b[None], (chip, 0, 0))
    wg_rest = lax.dynamic_update_slice(wg_rest, w_rest_b[None], (chip, 0, 0))
    mod = lax.dynamic_slice(modg, (0, me, 0), (4, 1, 768)).reshape(1, 3 * D)
    shift, scale, gate = mod[:, :D], mod[:, D:2 * D], mod[:, 2 * D:]
    wcat = _to_cat(wg_in)
    w_at = wg_rest[:, :R_AT].reshape(D, D)
    w_ss = wg_rest[:, R_AT:R_AT + R_SS].reshape(SSM_W, D)
    w_ou = wg_rest[:, R_AT + R_SS:].reshape(D, D)

    pad128 = lambda v: jnp.pad(v, ((0, 0), (0, 128 - v.shape[1])))
    dtb_p, alog_p, dsk_p = pad128(dt_bias), pad128(a_log), pad128(d_skip)
    bucket = _bucket_table()

    proj, dt_raw, h_t = norm_proj(x2, norm_w, scale, shift, wcat)
    biasm = bias_expand(rel_bias, sinks, bucket)
    ao = attn_fwd(proj, biasm, q_norm_w, k_norm_w)
    act = conv_fwd(proj, conv_w_full, conv_b)
    yss, sprev = ssd_fwd(act, dt_raw, dtb_p, alog_p, dsk_p)

    (loss_p, dy, dao, dza, dga, dgb, dyss, dzm, ua_t, yn_t, mg_t, dya, dyb, dout, dgate, dssm_nw) = tail(
        proj, ao, yss, x2, tgt, gate, ssm_norm_w, w_at, w_ss, w_ou)

    dq, dk, dv, dqw, dkw, dacc = attn_bwd(proj, dao, biasm, q_norm_w, k_norm_w)
    dbias = bias_reduce(dacc, bucket)
    drb = dbias[:, :NBUCKET].T
    dsk = dbias[:, NBUCKET].reshape(1, HQ)
    dact, ddt, ddtb, dalog, ddskip = ssd_bwd(act, dt_raw, dyss, sprev, dtb_p, alog_p, dsk_p)
    dxbc, dconv_w, dconv_b = conv_bwd(proj, dact, conv_w_full, conv_b)

    t = x2.shape[0]
    dproj = jnp.concatenate([dq, dza, dga, dgb, dzm, dxbc, dk, dv, ddt, jnp.zeros((t, NP - C_DT - 128), BF16)], axis=1)
    grad_x, dnorm_w, dscale, dshift = dproj_bwd(dproj, wcat, x2, dy, norm_w, scale)
    dwcat = wgrad(h_t, dproj, "dw_in", TN)
    dw_at = wgrad(ua_t, dya, "dw_attn", 512)
    dw_ss = wgrad(yn_t, dyb, "dw_ssm", 512)
    dw_ou = wgrad(mg_t, dout, "dw_out", 512)

    g_in = _from_cat(dwcat)
    g_rest = jnp.concatenate([dw_at.reshape(4, R_AT, D), dw_ss.reshape(4, R_SS, D), dw_ou.reshape(4, R_OU, D)], axis=1)
    sib_in, sib_rest = pair_exchange(g_in, g_rest)
    part_in, pb_in = pair_sum(g_in, ci, sib_in, "pair_sum_in")
    part_rest, pb_rest = pair_sum(g_rest, ci, sib_rest, "pair_sum_rest")
    oth_in, oth_rest = chip_exchange(pb_in, pb_rest)
    red_in = chip_sum(part_in, chip, oth_in, "chip_sum_in")
    red_rest = chip_sum(part_rest, chip, oth_rest, "chip_sum_rest")
    recv_in, recv_rest = pair_swap(red_in, red_rest)
    both = lambda mine, theirs: jnp.concatenate([jnp.where(ci == 0, mine, theirs), jnp.where(ci == 0, theirs, mine)],
                                                axis=0)
    g_shard_in = both(red_in, recv_in)
    g_shard_rest = both(red_rest, recv_rest)

    dmod = jnp.concatenate([dshift, dscale, dgate], axis=1)
    gsmall = jnp.concatenate([
        _pack_small(dmod, dnorm_w, dconv_b, dssm_nw, dqw, dkw, dsk[:, :HQ], ddtb[:, :SH], dalog[:, :SH],
                    ddskip[:, :SH], drb, extra=loss_p[:, :1]),
        dconv_w.reshape(12, 1024), jnp.zeros((4, 1024), F32)], axis=0)
    gall = allgather_small(gsmall, "gather_small_grads")
    ssum = sum_devices(gall)
    gs = _unpack_small(ssum[:16])
    loss = gs["extra"]
    dconv_w_sh = lax.dynamic_slice(ssum[16:28].reshape(CONV_K, XBC), (0, chip * 768), (CONV_K, 768))
    dmod_all = gall[:, 0:3, :].reshape(8, 3 * D)
    dw_ada = ada_grad(c_all, lax.dynamic_slice(dmod_all, (0, chip * 768), (8, 768)))

    grads = dict(gs)
    grads["w_ada"] = dw_ada
    grads["w_in"] = g_shard_in
    grads["w_attn_proj"] = g_shard_rest[:R_AT]
    grads["w_ssm_proj"] = g_shard_rest[R_AT:R_AT + R_SS]
    grads["w_out"] = g_shard_rest[R_AT + R_SS:]
    grads["conv_w"] = dconv_w_sh

    delta, new_m, new_v = {}, {}, {}
    for n in ("w_ada", "w_in", "conv_w", "w_attn_proj", "w_ssm_proj", "w_out"):
        delta[n], new_m[n], new_v[n] = adamw(args[n][0], grads[n], args["m_" + n][0], args["v_" + n][0], "adamw_" + n)
    ws = _pack_small(*[args[n] for n in SMALL])
    ms = _pack_small(*[args["m_" + n] for n in SMALL])
    vs = _pack_small(*[args["v_" + n] for n in SMALL])
    d_s, m_s, v_s = adamw(ws, ssum[:16], ms, vs, "adamw_small")
    d_s, m_s, v_s = _unpack_small(d_s), _unpack_small(m_s), _unpack_small(v_s)
    for n in SMALL:
        delta[n], new_m[n], new_v[n] = d_s[n], m_s[n], v_s[n]

    def shaped(n, a):
        return a.reshape(args[n].shape)

    outs = [loss, grad_x[None]]
    for table in (grads, delta, new_m, new_v):
        outs += [shaped(n, table[n]) for n in WEIGHTS]
    return tuple(outs)
```

```python
import functools
import math

import numpy as np
import jax
import jax.numpy as jnp
from jax import lax
from jax.experimental import pallas as pl
from jax.experimental.pallas import tpu as pltpu

F32 = jnp.float32
BF16 = jnp.bfloat16
MESH = pl.DeviceIdType.MESH

D = 1024
HQ, HKV, GRP, DH = 16, 4, 4, 64
BLK = 128
NBUCKET, MAXDIST = 32, 128
SSM_W, SH, SG, SR, SP, SN = 2048, 32, 4, 8, 64, 128
CONV_K = 4
XBC = SSM_W + 2 * SG * SN
IN_W = 9760
EPS = 1e-6
NEG = -1e30
SCALE = DH ** -0.5

C_Q, C_ZA, C_GA, C_GB, C_ZM, C_XBC, C_K, C_V, C_DT = 0, 1024, 2048, 3072, 4096, 6144, 9216, 9472, 9728
NP = 9984
TN = 768

SH_IN = IN_W // 4
R_AT, R_SS, R_OU = 256, 512, 256
HROWS = D // 2

ADAM_LR, ADAM_B1, ADAM_B2, ADAM_EPS, ADAM_WD, ADAM_STEP = 0.001, 0.9, 0.999, 1e-08, 0.01, 10

VMEM_LIMIT = 56 * 1024 * 1024


def _cp(sem=None):
    if sem is None:
        return pltpu.CompilerParams(vmem_limit_bytes=VMEM_LIMIT)
    return pltpu.CompilerParams(dimension_semantics=sem, vmem_limit_bytes=VMEM_LIMIT)


def _sig(x):
    return 0.5 * jnp.tanh(0.5 * x) + 0.5


def _dot(a, b):
    return jnp.dot(a, b, preferred_element_type=F32)


def _dot_nt(a, b):
    return lax.dot_general(a, b, (((1,), (1,)), ((), ())), preferred_element_type=F32)


def _dot_tn(a, b):
    return lax.dot_general(a, b, (((0,), (0,)), ((), ())), preferred_element_type=F32)


def _rsum(x):
    return jnp.sum(x, axis=-1, keepdims=True)


def _csum(x):
    return jnp.sum(x, axis=0, keepdims=True)


def _asum(x):
    return _csum(_rsum(x))


def _full(shape):
    nd = len(shape)
    return pl.BlockSpec(shape, lambda *_: (0,) * nd)


def ada_mod(c_all, w_ada_sh, b_ada_sh):
    def body(c_ref, w_ref, b_ref, o_ref):
        cv = c_ref[...]
        s = cv * _sig(cv)
        o_ref[...] = jnp.dot(s, w_ref[...], preferred_element_type=F32,
                             precision=lax.Precision.HIGHEST) + b_ref[...]

    n = w_ada_sh.shape[1]
    return pl.pallas_call(body, name="ada_mod", out_shape=jax.ShapeDtypeStruct((8, n), F32),
                          compiler_params=_cp())(c_all, w_ada_sh, b_ada_sh)


def ada_grad(c_all, dmod_sh):
    def body(c_ref, d_ref, o_ref):
        cv = c_ref[...]
        s = cv * _sig(cv)
        o_ref[...] = lax.dot_general(s, d_ref[...], (((0,), (0,)), ((), ())), preferred_element_type=F32,
                                     precision=lax.Precision.HIGHEST)

    n = dmod_sh.shape[1]
    return pl.pallas_call(body, name="ada_grad", out_shape=jax.ShapeDtypeStruct((D, n), F32),
                          compiler_params=_cp())(c_all, dmod_sh)


def norm_proj(x, norm_w, scale, shift, wcat):
    t = x.shape[0]
    tm = min(t, 1024)

    def body(x_ref, nw_ref, sc_ref, sh_ref, w_ref, p_ref, dt_ref, ht_ref, hs):
        @pl.when(pl.program_id(1) == 0)
        def _():
            xv = x_ref[...]
            r = lax.rsqrt(jnp.mean(xv * xv, axis=-1, keepdims=True) + EPS)
            h = (xv * r) * nw_ref[...]
            h = h * (1.0 + sc_ref[...]) + sh_ref[...]
            hs[...] = h.astype(BF16)
            ht_ref[...] = h.T.astype(BF16)

        p = _dot(hs[...], w_ref[...])
        p_ref[...] = p.astype(BF16)

        @pl.when(pl.program_id(1) == C_DT // TN)
        def _():
            dt_ref[...] = p[:, C_DT % TN:C_DT % TN + 128]

    vec = pl.BlockSpec((1, D), lambda i, j: (0, 0))
    return pl.pallas_call(
        body, name="norm_proj", grid=(t // tm, NP // TN),
        in_specs=[pl.BlockSpec((tm, D), lambda i, j: (i, 0)), vec, vec, vec,
                  pl.BlockSpec((D, TN), lambda i, j: (0, j))],
        out_specs=[pl.BlockSpec((tm, TN), lambda i, j: (i, j)), pl.BlockSpec((tm, 128), lambda i, j: (i, 0)),
                   pl.BlockSpec((D, tm), lambda i, j: (0, i))],
        out_shape=[jax.ShapeDtypeStruct((t, NP), BF16), jax.ShapeDtypeStruct((t, 128), F32),
                   jax.ShapeDtypeStruct((D, t), BF16)],
        scratch_shapes=[pltpu.VMEM((tm, D), BF16)],
        compiler_params=_cp(("parallel", "arbitrary")),
    )(x, norm_w, scale, shift, wcat)


def _bucket_table():
    qi = jnp.arange(BLK)[:, None]
    kj = jnp.arange(2 * BLK)[None, :]
    dist = qi + BLK - kj
    n = jnp.maximum(dist, 0)
    max_exact = NBUCKET // 2
    nf = jnp.maximum(n, 1).astype(F32)
    large = max_exact + (jnp.log(nf / max_exact) / math.log(MAXDIST / max_exact)
                         * (NBUCKET - max_exact)).astype(jnp.int32)
    large = jnp.minimum(large, NBUCKET - 1)
    bucket = jnp.where(n < max_exact, n, large).astype(jnp.int32)
    valid = (dist >= 0) & (dist < BLK)
    return jnp.where(valid, bucket, -1)


def bias_expand(rel_bias, sinks, bucket):
    def body(rb_ref, sk_ref, bk_ref, o_ref):
        hd = pl.program_id(0)
        bk = bk_ref[...]
        col = lax.broadcasted_iota(jnp.int32, (BLK, 2 * BLK), 1)

        def step(b, acc):
            return jnp.where(bk == b, rb_ref[b, hd], acc)

        acc = lax.fori_loop(0, NBUCKET, step, jnp.full((BLK, 2 * BLK), NEG, F32))
        acc = jnp.where(col == 0, sk_ref[0, hd], acc)
        o_ref[1, 0] = acc
        o_ref[0, 0] = jnp.where(jnp.logical_and(col > 0, col < BLK), NEG, acc)

    smem = pl.BlockSpec(memory_space=pltpu.SMEM)
    return pl.pallas_call(
        body, name="bias_expand", grid=(HQ,),
        in_specs=[smem, smem, _full((BLK, 2 * BLK))],
        out_specs=pl.BlockSpec((2, 1, BLK, 2 * BLK), lambda h: (0, h, 0, 0)),
        out_shape=jax.ShapeDtypeStruct((2, HQ, BLK, 2 * BLK), F32),
        compiler_params=_cp(("arbitrary",)),
    )(rel_bias, sinks, bucket)


def bias_reduce(dacc, bucket):
    col = jnp.arange(BLK * 2 * BLK, dtype=jnp.int32) % (2 * BLK)
    lane = jnp.arange(128, dtype=jnp.int32)[None, :]
    member = (bucket.reshape(-1)[:, None] == lane) | ((col[:, None] == 0) & (lane == NBUCKET))

    def body(d_ref, m_ref, o_ref):
        mm = m_ref[...]
        o_ref[...] = sum(_dot(part, mm) for part in _split3(d_ref[...]))

    return pl.pallas_call(body, name="bias_reduce", out_shape=jax.ShapeDtypeStruct((HQ, 128), F32),
                          compiler_params=_cp())(dacc.reshape(HQ, BLK * 2 * BLK), member.astype(BF16))


GQ = GRP * BLK


def _stack_heads(x, nh):
    return jnp.concatenate([x[:, DH * h:DH * (h + 1)] for h in range(nh)], axis=0)


def _unstack(xs, nh):
    rows = xs.shape[0] // nh
    return jnp.concatenate([xs[rows * h:rows * (h + 1)] for h in range(nh)], axis=1)


def _rms(x):
    return lax.rsqrt(jnp.mean(x * x, axis=-1, keepdims=True) + EPS)


def _stack_q(q, qw):
    qs = _stack_heads(q, HQ)
    r = _rms(qs)
    qhat = qs * r
    return qhat * qw, qhat, r


def _band_first(shape):
    return (lax.broadcasted_iota(jnp.int32, shape, 0) & (2 * BLK - 1)) == 0


def _stack_kv(kp, kc, vp, vc, kw):
    ks = _stack_heads(jnp.concatenate([kp, kc], axis=0), HKV)
    r = _rms(ks)
    khat = ks * r
    first = _band_first(ks.shape)
    kn = jnp.where(first, 0.0, khat * kw)
    v2 = jnp.where(first, 0.0, _stack_heads(jnp.concatenate([vp, vc], axis=0), HKV)).astype(BF16)
    return kn, khat, r, v2


def _softmax_rows(s):
    p = jnp.exp(s - jnp.max(s, axis=-1, keepdims=True))
    return p * (1.0 / _rsum(p))


def attn_fwd(proj, biasm, q_norm_w, k_norm_w):
    t = proj.shape[0]
    nb = t // BLK

    def body(q_ref, kc_ref, kp_ref, vc_ref, vp_ref, bm_ref, qw_ref, kw_ref, o_ref):
        f = lambda ref: ref[...].astype(F32)
        qn = _stack_q(f(q_ref), qw_ref[...])[0].astype(BF16)
        kn, _, _, v2 = _stack_kv(f(kp_ref), f(kc_ref), f(vp_ref), f(vc_ref), kw_ref[...])
        knb = kn.astype(BF16)
        s = jnp.concatenate([_dot_nt(qn[GQ * j:GQ * (j + 1)], knb[2 * BLK * j:2 * BLK * (j + 1)])
                             for j in range(HKV)], axis=0)
        pr = _softmax_rows(s * SCALE + bm_ref[0].reshape(HQ * BLK, 2 * BLK)).astype(BF16)
        o = jnp.concatenate([_dot(pr[GQ * j:GQ * (j + 1)], v2[2 * BLK * j:2 * BLK * (j + 1)])
                             for j in range(HKV)], axis=0)
        o_ref[...] = _unstack(o, HQ).astype(BF16)

    kblk, vblk = C_K // 256, C_V // 256
    prev = lambda n: jnp.maximum(n - 1, 0)
    return pl.pallas_call(
        body, name="attn_fwd", grid=(nb,),
        in_specs=[pl.BlockSpec((BLK, D), lambda n: (n, 0)),
                  pl.BlockSpec((BLK, 256), lambda n: (n, kblk)),
                  pl.BlockSpec((BLK, 256), lambda n: (prev(n), kblk)),
                  pl.BlockSpec((BLK, 256), lambda n: (n, vblk)),
                  pl.BlockSpec((BLK, 256), lambda n: (prev(n), vblk)),
                  pl.BlockSpec((1, HQ, BLK, 2 * BLK), lambda n: (jnp.minimum(n, 1), 0, 0, 0)),
                  _full((1, DH)), _full((1, DH))],
        out_specs=pl.BlockSpec((BLK, D), lambda n: (n, 0)),
        out_shape=jax.ShapeDtypeStruct((t, D), BF16),
        compiler_params=_cp(("parallel",)),
    )(proj, proj, proj, proj, proj, biasm, q_norm_w, k_norm_w)


def attn_bwd(proj, dao, biasm, q_norm_w, k_norm_w):
    t = proj.shape[0]
    nb = t // BLK
    kb = 2 * BLK

    def body(q_ref, kc_ref, kp_ref, vc_ref, vp_ref, do_ref, bm_ref, qw_ref, kw_ref,
             dq_ref, dk_ref, dv_ref, dqw_ref, dkw_ref, dacc_ref, ck, cv, pk, pv, nk, nv):
        n = pl.program_id(0)

        @pl.when(n == 0)
        def _():
            for ref in (dqw_ref, dkw_ref, dacc_ref, ck, cv):
                ref[...] = jnp.zeros_like(ref)

        qw = qw_ref[...]
        kw = kw_ref[...]
        f = lambda ref: ref[...].astype(F32)
        kn, khat, rk, v2 = _stack_kv(f(kp_ref), f(kc_ref), f(vp_ref), f(vc_ref), kw)
        grp = lambda a, j: a[GQ * j:GQ * (j + 1)]
        band = lambda a, j: a[kb * j:kb * (j + 1)]

        @pl.when(n < nb)
        def _():
            qn, qhat, rq = _stack_q(f(q_ref), qw)
            qnb = qn.astype(BF16)
            knb = kn.astype(BF16)
            dos = _stack_heads(f(do_ref), HQ).astype(BF16)
            s = jnp.concatenate([_dot_nt(grp(qnb, j), band(knb, j)) for j in range(HKV)], axis=0)
            pr = _softmax_rows(s * SCALE + bm_ref[0].reshape(HQ * BLK, kb))
            dp = jnp.concatenate([_dot_nt(grp(dos, j), band(v2, j)) for j in range(HKV)], axis=0)
            ds = pr * (dp - _rsum(pr * dp))
            dacc_ref[...] += ds.reshape(HQ, BLK, kb)
            dsb = ds.astype(BF16)
            prb = pr.astype(BF16)
            dqn = jnp.concatenate([_dot(grp(dsb, j), band(knb, j)) for j in range(HKV)], axis=0) * SCALE
            dqhat = dqn * qw
            dq = rq * (dqhat - qhat * jnp.mean(dqhat * qhat, axis=-1, keepdims=True))
            dq_ref[...] = _unstack(dq, HQ).astype(BF16)
            dqw_ref[...] += _csum(dqn * qhat)
            first = _band_first((kb, DH))
            for j in range(HKV):
                rows = slice(BLK * j, BLK * (j + 1))
                dkn = jnp.where(first, 0.0, _dot_tn(grp(dsb, j), grp(qnb, j)) * SCALE)
                dvj = jnp.where(first, 0.0, _dot_tn(grp(prb, j), grp(dos, j)))
                pk[rows, :] = dkn[:BLK]
                nk[rows, :] = dkn[BLK:]
                pv[rows, :] = dvj[:BLK]
                nv[rows, :] = dvj[BLK:]

        @pl.when(n == nb)
        def _():
            for ref in (pk, pv, nk, nv):
                ref[...] = jnp.zeros_like(ref)

        khp = jnp.concatenate([khat[kb * j:kb * j + BLK] for j in range(HKV)], axis=0)
        rkp = jnp.concatenate([rk[kb * j:kb * j + BLK] for j in range(HKV)], axis=0)
        dkn = ck[...] + pk[...]
        dkhat = dkn * kw
        dk = rkp * (dkhat - khp * jnp.mean(dkhat * khp, axis=-1, keepdims=True))
        dk_ref[...] = _unstack(dk, HKV).astype(BF16)
        dkw_ref[...] += _csum(dkn * khp)
        dv_ref[...] = _unstack(cv[...] + pv[...], HKV).astype(BF16)
        ck[...] = nk[...]
        cv[...] = nv[...]

    kblk, vblk = C_K // 256, C_V // 256
    cur = lambda n: jnp.minimum(n, nb - 1)
    prev = lambda n: jnp.maximum(n - 1, 0)
    carry = pltpu.VMEM((HKV * BLK, DH), F32)
    return pl.pallas_call(
        body, name="attn_bwd", grid=(nb + 1,),
        in_specs=[pl.BlockSpec((BLK, D), lambda n: (cur(n), 0)),
                  pl.BlockSpec((BLK, 256), lambda n: (cur(n), kblk)), pl.BlockSpec((BLK, 256), lambda n: (prev(n), kblk)),
                  pl.BlockSpec((BLK, 256), lambda n: (cur(n), vblk)), pl.BlockSpec((BLK, 256), lambda n: (prev(n), vblk)),
                  pl.BlockSpec((BLK, D), lambda n: (cur(n), 0)),
                  pl.BlockSpec((1, HQ, BLK, kb), lambda n: (jnp.minimum(n, 1), 0, 0, 0)),
                  _full((1, DH)), _full((1, DH))],
        out_specs=[pl.BlockSpec((BLK, D), lambda n: (cur(n), 0)),
                   pl.BlockSpec((BLK, 256), lambda n: (prev(n), 0)), pl.BlockSpec((BLK, 256), lambda n: (prev(n), 0)),
                   _full((1, DH)), _full((1, DH)), _full((HQ, BLK, kb))],
        out_shape=[jax.ShapeDtypeStruct((t, D), BF16), jax.ShapeDtypeStruct((t, 256), BF16),
                   jax.ShapeDtypeStruct((t, 256), BF16), jax.ShapeDtypeStruct((1, DH), F32),
                   jax.ShapeDtypeStruct((1, DH), F32), jax.ShapeDtypeStruct((HQ, BLK, kb), F32)],
        scratch_shapes=[carry] * 6,
        compiler_params=_cp(("arbitrary",)),
    )(proj, proj, proj, proj, proj, dao, biasm, q_norm_w, k_norm_w)


CONV_TM, CONV_CW, CONV_RC, HALO = 512, 512, 32, 16


def conv_fwd(proj, conv_w, conv_b):
    t = proj.shape[0]
    tm = min(t, CONV_TM)
    c0 = C_XBC // CONV_CW

    def body(x_ref, xp_ref, w_ref, b_ref, o_ref):
        i = pl.program_id(1)
        w = w_ref[...]
        b = b_ref[...]
        for r in range(tm // CONV_RC):
            lo = r * CONV_RC
            if r == 0:
                head = jnp.where(i == 0, 0.0, xp_ref[...].astype(F32))
                win = jnp.concatenate([head, x_ref[0:CONV_RC, :].astype(F32)], axis=0)
            else:
                win = x_ref[lo - HALO:lo + CONV_RC, :].astype(F32)
            acc = b
            for j in range(CONV_K):
                acc = acc + w[j:j + 1] * win[HALO - 3 + j:HALO - 3 + j + CONV_RC]
            o_ref[lo:lo + CONV_RC, :] = acc * _sig(acc)

    rh = tm // HALO
    return pl.pallas_call(
        body, name="conv_fwd", grid=(XBC // CONV_CW, t // tm),
        in_specs=[pl.BlockSpec((tm, CONV_CW), lambda s, i: (i, c0 + s)),
                  pl.BlockSpec((HALO, CONV_CW), lambda s, i: (jnp.maximum(i * rh - 1, 0), c0 + s)),
                  pl.BlockSpec((CONV_K, CONV_CW), lambda s, i: (0, s)), pl.BlockSpec((1, CONV_CW), lambda s, i: (0, s))],
        out_specs=pl.BlockSpec((tm, CONV_CW), lambda s, i: (i, s)),
        out_shape=jax.ShapeDtypeStruct((t, XBC), F32),
        compiler_params=_cp(("parallel", "parallel")),
    )(proj, proj, conv_w, conv_b)


def conv_bwd(proj, dact, conv_w, conv_b):
    t = proj.shape[0]
    tm = min(t, CONV_TM)
    nt = t // tm
    nr = tm // CONV_RC
    c0 = C_XBC // CONV_CW
    ext = CONV_RC + 8

    def body(x_ref, xp_ref, xn_ref, d_ref, dn_ref, w_ref, b_ref, dx_ref, dw_ref, db_ref):
        i = pl.program_id(1)

        @pl.when(i == 0)
        def _():
            dw_ref[...] = jnp.zeros_like(dw_ref)
            db_ref[...] = jnp.zeros_like(db_ref)

        w = w_ref[...]
        b = b_ref[...]
        dws = [jnp.zeros((1, CONV_CW), F32) for _ in range(CONV_K)]
        db = jnp.zeros((1, CONV_CW), F32)
        for r in range(nr):
            lo = r * CONV_RC
            parts = []
            if r == 0:
                parts.append(jnp.where(i == 0, 0.0, xp_ref[...].astype(F32)))
                parts.append(x_ref[0:CONV_RC + (HALO if nr > 1 else 0), :].astype(F32))
            else:
                parts.append(x_ref[lo - HALO:lo + CONV_RC + (HALO if r < nr - 1 else 0), :].astype(F32))
            if r == nr - 1:
                parts.append(xn_ref[...].astype(F32))
            win = jnp.concatenate(parts, axis=0)
            if r < nr - 1:
                dext = d_ref[lo:lo + ext, :]
            else:
                dext = jnp.concatenate([d_ref[lo:lo + CONV_RC, :], jnp.where(i == nt - 1, 0.0, dn_ref[...])], axis=0)
            pre = b
            for j in range(CONV_K):
                pre = pre + w[j:j + 1] * win[HALO - 3 + j:HALO - 3 + j + ext]
            sg = _sig(pre)
            dpre = dext * (sg * (1.0 + pre * (1.0 - sg)))
            dx = jnp.zeros((CONV_RC, CONV_CW), F32)
            own = dpre[0:CONV_RC]
            for j in range(CONV_K):
                dx = dx + w[j:j + 1] * dpre[3 - j:3 - j + CONV_RC]
                dws[j] = dws[j] + _csum(own * win[HALO - 3 + j:HALO - 3 + j + CONV_RC])
            db = db + _csum(own)
            dx_ref[lo:lo + CONV_RC, :] = dx.astype(BF16)
        dw_ref[...] += jnp.concatenate(dws, axis=0)
        db_ref[...] += db

    rh = tm // HALO
    r8 = tm // 8
    nxt = lambda i, per: jnp.minimum((i + 1) * per, nt * per - 1)
    return pl.pallas_call(
        body, name="conv_bwd", grid=(XBC // CONV_CW, nt),
        in_specs=[pl.BlockSpec((tm, CONV_CW), lambda s, i: (i, c0 + s)),
                  pl.BlockSpec((HALO, CONV_CW), lambda s, i: (jnp.maximum(i * rh - 1, 0), c0 + s)),
                  pl.BlockSpec((HALO, CONV_CW), lambda s, i: (nxt(i, rh), c0 + s)),
                  pl.BlockSpec((tm, CONV_CW), lambda s, i: (i, s)),
                  pl.BlockSpec((8, CONV_CW), lambda s, i: (nxt(i, r8), s)),
                  pl.BlockSpec((CONV_K, CONV_CW), lambda s, i: (0, s)), pl.BlockSpec((1, CONV_CW), lambda s, i: (0, s))],
        out_specs=[pl.BlockSpec((tm, CONV_CW), lambda s, i: (i, s)),
                   pl.BlockSpec((CONV_K, CONV_CW), lambda s, i: (0, s)), pl.BlockSpec((1, CONV_CW), lambda s, i: (0, s))],
        out_shape=[jax.ShapeDtypeStruct((t, XBC), BF16), jax.ShapeDtypeStruct((CONV_K, XBC), F32),
                   jax.ShapeDtypeStruct((1, XBC), F32)],
        compiler_params=_cp(("parallel", "arbitrary")),
    )(proj, proj, proj, dact, dact, conv_w, conv_b)


def _split3(x):
    h = x.astype(BF16)
    r = x - h.astype(F32)
    m = r.astype(BF16)
    lo = (r - m.astype(F32)).astype(BF16)
    return h, m, lo


def _tri_mm(tri, x):
    h, m, lo = _split3(x)
    return _dot(tri, h) + _dot(tri, m) + _dot(tri, lo)


def _softplus(x):
    return jnp.maximum(x, 0.0) + jnp.log1p(jnp.exp(-jnp.abs(x)))


def _chunk_decays(dt_raw, dtb, alog):
    dtv = _softplus(dt_raw + dtb)
    a = -jnp.exp(alog)
    ri = lax.broadcasted_iota(jnp.int32, (BLK, BLK), 0)
    ci = lax.broadcasted_iota(jnp.int32, (BLK, BLK), 1)
    causal = ri >= ci
    acum = _tri_mm(causal.astype(BF16), dtv * a)
    return dtv, a, causal, acum, acum.T


NPAIR = SH // 2


def _pairs(x):
    return jnp.stack([x[:, 128 * k:128 * (k + 1)] for k in range(NPAIR)])


def _unpairs(x3):
    return jnp.concatenate([x3[k] for k in range(NPAIR)], axis=1)


def _per_head_cols(m):
    return jnp.stack([jnp.broadcast_to(m[:, h:h + 1], m.shape) for h in range(SH)])


def _pair_lanes(t):
    r = t.reshape(NPAIR, 2, t.shape[1], 128)
    lo = lax.broadcasted_iota(jnp.int32, (1, t.shape[1], 128), 2) < SP
    return jnp.where(lo, r[:, 0], r[:, 1])


class _Chunk:
    pass


def _chunk_common(dt_raw, dtb, alog, dskip):
    cm = _Chunk()
    cm.dtv, cm.a, cm.causal, acum, acum_t = _chunk_decays(dt_raw, dtb, alog)
    cm.acol = _per_head_cols(acum)
    cm.arow = jnp.stack([acum_t[h:h + 1, :] for h in range(SH)])
    cm.lam = jnp.exp(jnp.where(cm.causal[None], cm.acol - cm.arow, NEG))
    apl = _pair_lanes(cm.acol)
    alast = apl[:, BLK - 1:BLK, :]
    cm.dpl = _pair_lanes(_per_head_cols(cm.dtv))
    cm.eapl = jnp.exp(apl)
    cm.epl = jnp.exp(alast - apl)
    cm.cdpl = jnp.exp(alast)
    cm.dskpl = _pair_lanes(_per_head_cols(dskip))
    cm.lo = lax.broadcasted_iota(jnp.int32, (1, BLK, 128), 2) < SP
    return cm


def ssd_fwd(act, dt_raw, dtb_p, alog_p, dsk_p):
    t = act.shape[0]
    nc = t // BLK

    def body(xs_ref, b_ref, c_ref, dt_ref, dtb_ref, al_ref, dk_ref, y_ref, sp_ref, st):
        c = pl.program_id(0)

        @pl.when(c == 0)
        def _():
            st[...] = jnp.zeros_like(st)

        s_t = st[...]
        sp_ref[0] = s_t
        cm = _chunk_common(dt_ref[...], dtb_ref[...], al_ref[...], dk_ref[...])
        gms, cbs, bts = [], [], []
        for g in range(SG):
            bf = b_ref[:, SN * g:SN * (g + 1)]
            cb = c_ref[:, SN * g:SN * (g + 1)].astype(BF16)
            gms.append(_dot_nt(cb, bf.astype(BF16)))
            cbs.append(cb)
            bts.append(bf.T.astype(BF16))
        m = (cm.lam.reshape(SG, SR, BLK, BLK) * jnp.stack(gms)[:, None]).reshape(SH, BLK, BLK).astype(BF16)
        xs16 = _pairs(xs_ref[...])
        xdt16 = xs16 * cm.dpl
        x_lo = jnp.where(cm.lo, xdt16, 0.0).astype(BF16)
        x_hi = jnp.where(cm.lo, 0.0, xdt16).astype(BF16)
        s16 = _pairs(s_t)
        s16b = s16.astype(BF16)
        yd = jnp.stack([_dot(m[2 * k], x_lo[k]) + _dot(m[2 * k + 1], x_hi[k]) for k in range(NPAIR)])
        yo = jnp.stack([_dot(cbs[k // (NPAIR // SG)], s16b[k]) for k in range(NPAIR)])
        y_ref[...] = _unpairs(yd + yo * cm.eapl + cm.dskpl * xs16).astype(BF16)
        xe = (xdt16 * cm.epl).astype(BF16)
        st[...] = _unpairs(cm.cdpl * s16 + jnp.stack([_dot(bts[k // (NPAIR // SG)], xe[k]) for k in range(NPAIR)]))

    vec = _full((1, 128))
    return pl.pallas_call(
        body, name="ssd_fwd", grid=(nc,),
        in_specs=[pl.BlockSpec((BLK, SSM_W), lambda c: (c, 0)),
                  pl.BlockSpec((BLK, SG * SN), lambda c: (c, SSM_W // (SG * SN))),
                  pl.BlockSpec((BLK, SG * SN), lambda c: (c, SSM_W // (SG * SN) + 1)),
                  pl.BlockSpec((BLK, 128), lambda c: (c, 0)), vec, vec, vec],
        out_specs=[pl.BlockSpec((BLK, SSM_W), lambda c: (c, 0)), pl.BlockSpec((1, SN, SSM_W), lambda c: (c, 0, 0))],
        out_shape=[jax.ShapeDtypeStruct((t, SSM_W), BF16), jax.ShapeDtypeStruct((nc, SN, SSM_W), F32)],
        scratch_shapes=[pltpu.VMEM((SN, SSM_W), F32)],
        compiler_params=_cp(("arbitrary",)),
    )(act, act, act, dt_raw, dtb_p, alog_p, dsk_p)


def _head_sums(q):
    r = q.shape[1]
    lo = lax.broadcasted_iota(jnp.int32, (1, r, 128), 2) < SP
    s_lo = jnp.sum(jnp.where(lo, q, 0.0), axis=-1, keepdims=True)
    s_hi = jnp.sum(jnp.where(lo, 0.0, q), axis=-1, keepdims=True)
    lane = lax.broadcasted_iota(jnp.int32, (r, 128), 1)
    out = jnp.zeros((r, 128), F32)
    for k in range(NPAIR):
        out = jnp.where(lane == 2 * k, s_lo[k], jnp.where(lane == 2 * k + 1, s_hi[k], out))
    return out


def ssd_bwd(act, dt_raw, dy, sprev, dtb_p, alog_p, dsk_p):
    t = act.shape[0]
    nc = t // BLK

    def body(xs_ref, b_ref, c_ref, dt_ref, dy_ref, sp_ref, dtb_ref, al_ref, dk_ref,
             da_ref, ddt_ref, ddtb_ref, dal_ref, ddk_ref, dst):
        i = pl.program_id(0)

        @pl.when(i == 0)
        def _():
            dst[...] = jnp.zeros_like(dst)
            ddtb_ref[...] = jnp.zeros_like(ddtb_ref)
            dal_ref[...] = jnp.zeros_like(dal_ref)
            ddk_ref[...] = jnp.zeros_like(ddk_ref)

        dt_raw = dt_ref[...]
        dtb = dtb_ref[...]
        cm = _chunk_common(dt_raw, dtb, al_ref[...], dk_ref[...])
        ri = lax.broadcasted_iota(jnp.int32, (BLK, BLK), 0)
        ci = lax.broadcasted_iota(jnp.int32, (BLK, BLK), 1)
        lam_t = jnp.exp(jnp.where((ri <= ci)[None], cm.arow - cm.acol, NEG))
        bbs, cbs, cts, gms = [], [], [], []
        for g in range(SG):
            bf = b_ref[:, SN * g:SN * (g + 1)]
            cf = c_ref[:, SN * g:SN * (g + 1)]
            bbs.append(bf.astype(BF16))
            cbs.append(cf.astype(BF16))
            cts.append(cf.T.astype(BF16))
            gms.append(_dot_nt(bbs[g], cbs[g]))
        grp = lambda k: k // (NPAIR // SG)
        xs16 = _pairs(xs_ref[...])
        dy16 = _pairs(dy_ref[...].astype(F32))
        sp16 = _pairs(sp_ref[0])
        ds16 = _pairs(dst[...])
        xdt16 = xs16 * cm.dpl
        xdtb = xdt16.astype(BF16)
        dyh = [jnp.where(cm.lo, dy16, 0.0).astype(BF16), jnp.where(cm.lo, 0.0, dy16).astype(BF16)]
        m_t = (lam_t.reshape(SG, SR, BLK, BLK) * jnp.stack(gms)[:, None]).reshape(SH, BLK, BLK).astype(BF16)
        dxdt = jnp.stack([_dot(m_t[2 * k], dyh[0][k]) + _dot(m_t[2 * k + 1], dyh[1][k]) for k in range(NPAIR)])
        dm = jnp.stack([_dot_nt(dyh[h % 2][h // 2], xdtb[h // 2]) for h in range(SH)])
        dgl = (dm * cm.lam).reshape(SG, SR, BLK, BLK)
        dg = jnp.sum(dgl, axis=1).astype(BF16)
        w = (dgl * jnp.stack([_dot_nt(cbs[g], bbs[g]) for g in range(SG)])[:, None]).reshape(SH, BLK, BLK)
        w_rows = jnp.sum(w, axis=2, keepdims=True)
        w_cols = jnp.concatenate([jnp.sum(w, axis=1)] + [jnp.zeros((128 - SH, BLK), F32)], axis=0).T
        lane_c = lax.broadcasted_iota(jnp.int32, (BLK, 128), 1)
        da_cols = -w_cols
        for h in range(SH):
            da_cols = jnp.where(lane_c == h, da_cols + w_rows[h], da_cols)
        ds16b = ds16.astype(BF16)
        sp16b = sp16.astype(BF16)
        dxs = jnp.stack([_dot(bbs[grp(k)], ds16b[k]) for k in range(NPAIR)]) * cm.epl
        dxdt = dxdt + dxs
        dya = (dy16 * cm.eapl).astype(BF16)
        xe = (xdt16 * cm.epl).astype(BF16)
        dcs, dbs = [], []
        for g in range(SG):
            ks = range(g * (NPAIR // SG), (g + 1) * (NPAIR // SG))
            dcs.append(sum(_dot_nt(dya[k], sp16b[k]) for k in ks) + _dot(dg[g], bbs[g]))
            dbs.append(sum(_dot_nt(xe[k], ds16b[k]) for k in ks) + _dot_tn(dg[g], cbs[g]))
        dst[...] = _unpairs(cm.cdpl * ds16 + jnp.stack([_dot(cts[grp(k)], dya[k]) for k in range(NPAIR)]))
        da_ref[...] = jnp.concatenate([_unpairs(dxdt * cm.dpl + cm.dskpl * dy16)] + dbs + dcs, axis=1)
        y_off = jnp.stack([_dot(cbs[grp(k)], sp16b[k]) for k in range(NPAIR)]) * cm.eapl
        da_cols = da_cols + _head_sums(dy16 * y_off - xdt16 * dxs)
        last = _head_sums(jnp.sum(xdt16 * dxs, axis=1, keepdims=True)
                          + cm.cdpl * jnp.sum(ds16 * sp16, axis=1, keepdims=True))
        ddt = _head_sums(dxdt * xs16)
        row_i = lax.broadcasted_iota(jnp.int32, (BLK, 128), 0)
        dacum = da_cols + jnp.where(row_i == BLK - 1, last, 0.0)
        dda = _tri_mm((ri <= ci).astype(BF16), dacum)
        ddt = ddt + dda * cm.a
        dal_ref[...] += _csum(dda * cm.dtv) * cm.a
        ddt_raw = jnp.where(lane_c < SH, ddt * _sig(dt_raw + dtb), 0.0)
        ddt_ref[...] = ddt_raw.astype(BF16)
        ddtb_ref[...] += _csum(ddt_raw)
        ddk_ref[...] += _head_sums(jnp.sum(dy16 * xs16, axis=1, keepdims=True))

    rev = lambda i: nc - 1 - i
    vec = _full((1, 128))
    slab = pl.BlockSpec((BLK, SSM_W), lambda i: (rev(i), 0))
    return pl.pallas_call(
        body, name="ssd_bwd", grid=(nc,),
        in_specs=[slab,
                  pl.BlockSpec((BLK, SG * SN), lambda i: (rev(i), SSM_W // (SG * SN))),
                  pl.BlockSpec((BLK, SG * SN), lambda i: (rev(i), SSM_W // (SG * SN) + 1)),
                  pl.BlockSpec((BLK, 128), lambda i: (rev(i), 0)),
                  slab,
                  pl.BlockSpec((1, SN, SSM_W), lambda i: (rev(i), 0, 0)), vec, vec, vec],
        out_specs=[pl.BlockSpec((BLK, XBC), lambda i: (rev(i), 0)), pl.BlockSpec((BLK, 128), lambda i: (rev(i), 0)),
                   vec, vec, vec],
        out_shape=[jax.ShapeDtypeStruct((t, XBC), F32), jax.ShapeDtypeStruct((t, 128), BF16),
                   jax.ShapeDtypeStruct((1, 128), F32), jax.ShapeDtypeStruct((1, 128), F32),
                   jax.ShapeDtypeStruct((1, 128), F32)],
        scratch_shapes=[pltpu.VMEM((SN, SSM_W), F32)],
        compiler_params=_cp(("arbitrary",)),
    )(act, act, act, dt_raw, dy, sprev, dtb_p, alog_p, dsk_p)


TAIL_TM = 256


def _dsilu(z, s):
    return s * (1.0 + z * (1.0 - s))


def tail(proj, ao, yss, x, target, gate, ssm_nw, w_at, w_ss, w_ou):
    t = x.shape[0]
    tm = min(t, TAIL_TM)
    gw = SSM_W // SG

    def body(ao_ref, za_ref, ga_ref, gb_ref, zm_ref, ys_ref, x_ref, tg_ref, gt_ref, nw_ref, wa_ref, ws_ref, wo_ref,
             loss_ref, dy_ref, dao_ref, dza_ref, dga_ref, dgb_ref, dys_ref, dzm_ref,
             ua_ref, yn_ref, mg_ref, dya_ref, dyb_ref, do_ref, dgt_ref, dnw_ref):
        i = pl.program_id(0)

        @pl.when(i == 0)
        def _():
            loss_ref[...] = jnp.zeros_like(loss_ref)
            dgt_ref[...] = jnp.zeros_like(dgt_ref)
            dnw_ref[...] = jnp.zeros_like(dnw_ref)

        ao = ao_ref[...].astype(F32)
        za = za_ref[...].astype(F32)
        sa = _sig(za)
        sila = za * sa
        ua_f = ao * sila
        ua = ua_f.astype(BF16)
        ya = _dot(ua, wa_ref[...])
        zm = zm_ref[...].astype(F32)
        sm = _sig(zm)
        silm = zm * sm
        ys = ys_ref[...].astype(F32)
        u = ys * silm
        nw = nw_ref[...]
        rs, uns = [], []
        for g in range(SG):
            ug = u[:, gw * g:gw * (g + 1)]
            r = lax.rsqrt(jnp.mean(ug * ug, axis=-1, keepdims=True) + EPS)
            rs.append(r)
            uns.append(ug * r)
        un = jnp.concatenate(uns, axis=1)
        yn_f = un * nw
        yn = yn_f.astype(BF16)
        yb = _dot(yn, ws_ref[...])
        sga = _sig(ga_ref[...].astype(F32))
        sgb = _sig(gb_ref[...].astype(F32))
        mg_f = sga * ya + sgb * yb
        mg = mg_f.astype(BF16)
        o = _dot(mg, wo_ref[...])
        gt = gt_ref[...]
        err = (x_ref[...] + gt * o) - tg_ref[...]
        lane = lax.broadcasted_iota(jnp.int32, (1, 128), 1)
        loss_ref[...] += jnp.where(lane == 0, 0.5 * _asum(_rsum(err * err) / D), 0.0)
        dy = err * (1.0 / D)
        dy_ref[...] = dy
        dgt_ref[...] += _csum(dy * o)
        do = (dy * gt).astype(BF16)
        dmg = _dot_nt(do, wo_ref[...])
        dga_ref[...] = (dmg * ya * sga * (1.0 - sga)).astype(BF16)
        dgb_ref[...] = (dmg * yb * sgb * (1.0 - sgb)).astype(BF16)
        dya = (dmg * sga).astype(BF16)
        dyb = (dmg * sgb).astype(BF16)
        dua = _dot_nt(dya, wa_ref[...])
        dao_ref[...] = (dua * sila).astype(BF16)
        dza_ref[...] = (dua * ao * _dsilu(za, sa)).astype(BF16)
        dyn = _dot_nt(dyb, ws_ref[...])
        dnw_ref[...] += _csum(dyn * un)
        dun = dyn * nw
        dus = []
        for g in range(SG):
            gs = slice(gw * g, gw * (g + 1))
            dus.append(rs[g] * (dun[:, gs] - uns[g] * jnp.mean(dun[:, gs] * uns[g], axis=-1, keepdims=True)))
        du = jnp.concatenate(dus, axis=1)
        dys_ref[...] = (du * silm).astype(BF16)
        dzm_ref[...] = (du * ys * _dsilu(zm, sm)).astype(BF16)
        ua_ref[...] = ua_f.T.astype(BF16)
        yn_ref[...] = yn_f.T.astype(BF16)
        mg_ref[...] = mg_f.T.astype(BF16)
        dya_ref[...] = dya
        dyb_ref[...] = dyb
        do_ref[...] = do

    row = lambda w: pl.BlockSpec((tm, w), lambda i: (i, 0))
    pcol = lambda w, c0: pl.BlockSpec((tm, w), lambda i: (i, c0 // w))
    sd = lambda w, dt: jax.ShapeDtypeStruct((t, w), dt)
    colt = lambda w: pl.BlockSpec((w, tm), lambda i: (0, i))
    sdt = lambda w: jax.ShapeDtypeStruct((w, t), BF16)
    return pl.pallas_call(
        body, name="tail", grid=(t // tm,),
        in_specs=[row(D), pcol(D, C_ZA), pcol(D, C_GA), pcol(D, C_GB), pcol(SSM_W, C_ZM), row(SSM_W), row(D), row(D),
                  _full((1, D)), _full((1, SSM_W)), _full((D, D)), _full((SSM_W, D)), _full((D, D))],
        out_specs=[_full((1, 128)), row(D), row(D), row(D), row(D), row(D), row(SSM_W), row(SSM_W),
                   colt(D), colt(SSM_W), colt(D), row(D), row(D), row(D), _full((1, D)), _full((1, SSM_W))],
        out_shape=[jax.ShapeDtypeStruct((1, 128), F32), sd(D, F32), sd(D, BF16), sd(D, BF16), sd(D, BF16), sd(D, BF16),
                   sd(SSM_W, BF16), sd(SSM_W, BF16), sdt(D), sdt(SSM_W), sdt(D), sd(D, BF16),
                   sd(D, BF16), sd(D, BF16), jax.ShapeDtypeStruct((1, D), F32), jax.ShapeDtypeStruct((1, SSM_W), F32)],
        compiler_params=_cp(("arbitrary",)),
    )(ao, proj, proj, proj, proj, yss, x, target, gate, ssm_nw, w_at, w_ss, w_ou)


def dproj_bwd(dproj, wcat, x, dy, norm_w, scale):
    t = x.shape[0]
    tm = min(t, 512)
    tk = NP // 6
    nk = NP // tk
    nt = t // tm

    def body(dp_ref, w_ref, x_ref, dy_ref, nw_ref, sc_ref, gx_ref, dnw_ref, dsc_ref, dsh_ref, acc, dwe_ref):
        i = pl.program_id(0)
        k = pl.program_id(1)

        @pl.when(jnp.logical_and(i == 0, k == 0))
        def _():
            dwe_ref[...] = jnp.zeros_like(dwe_ref)
            dsh_ref[...] = jnp.zeros_like(dsh_ref)
            dnw_ref[...] = jnp.zeros_like(dnw_ref)
            dsc_ref[...] = jnp.zeros_like(dsc_ref)

        part = _dot_nt(dp_ref[...], w_ref[...])

        @pl.when(k == 0)
        def _():
            acc[...] = part

        @pl.when(k > 0)
        def _():
            acc[...] += part

        @pl.when(k == nk - 1)
        def _():
            dh = acc[...]
            xv = x_ref[...]
            r = lax.rsqrt(jnp.mean(xv * xv, axis=-1, keepdims=True) + EPS)
            xn = xv * r
            weff = nw_ref[...] * (1.0 + sc_ref[...])
            dxn = dh * weff
            gx_ref[...] = dy_ref[...] + r * (dxn - xn * jnp.mean(dxn * xn, axis=-1, keepdims=True))
            dwe_ref[...] += _csum(dh * xn)
            dsh_ref[...] += _csum(dh)

        @pl.when(jnp.logical_and(i == nt - 1, k == nk - 1))
        def _():
            dwe = dwe_ref[...]
            dnw_ref[...] = dwe * (1.0 + sc_ref[...])
            dsc_ref[...] = dwe * nw_ref[...]

    vec = pl.BlockSpec((1, D), lambda i, k: (0, 0))
    row = pl.BlockSpec((tm, D), lambda i, k: (i, 0))
    return pl.pallas_call(
        body, name="dproj_bwd", grid=(nt, nk),
        in_specs=[pl.BlockSpec((tm, tk), lambda i, k: (i, k)), pl.BlockSpec((D, tk), lambda i, k: (0, k)),
                  row, row, vec, vec],
        out_specs=[row, vec, vec, vec],
        out_shape=[jax.ShapeDtypeStruct((t, D), F32), jax.ShapeDtypeStruct((1, D), F32),
                   jax.ShapeDtypeStruct((1, D), F32), jax.ShapeDtypeStruct((1, D), F32)],
        scratch_shapes=[pltpu.VMEM((tm, D), F32), pltpu.VMEM((1, D), F32)],
        compiler_params=_cp(("arbitrary", "arbitrary")),
    )(dproj, wcat, x, dy, norm_w, scale)


def wgrad(at, b, name, bn):
    m, t = at.shape
    n = b.shape[1]
    tk = min(t, 1024)
    bm = min(m, 1024)

    def body(a_ref, b_ref, o_ref):
        part = _dot(a_ref[...], b_ref[...])

        @pl.when(pl.program_id(2) == 0)
        def _():
            o_ref[...] = part

        @pl.when(pl.program_id(2) > 0)
        def _():
            o_ref[...] += part

    return pl.pallas_call(
        body, name=name, grid=(m // bm, n // bn, t // tk),
        in_specs=[pl.BlockSpec((bm, tk), lambda i, j, k: (i, k)), pl.BlockSpec((tk, bn), lambda i, j, k: (k, j))],
        out_specs=pl.BlockSpec((bm, bn), lambda i, j, k: (i, j)),
        out_shape=jax.ShapeDtypeStruct((m, n), F32),
        compiler_params=_cp(("parallel", "parallel", "arbitrary")),
    )(at, b)


SUM_TR = 256


def pair_sum(g, core, theirs, name):
    w = g.shape[2]
    nh = HROWS // SUM_TR

    def body(core_ref, a_ref, b_ref, o_ref, ob_ref):
        s = a_ref[...] + b_ref[...]
        o_ref[...] = s
        ob_ref[...] = s.astype(BF16)

    spec = pl.BlockSpec((1, SUM_TR, w), lambda d, i, c: (d, i, 0))
    return pl.pallas_call(
        body, name=name,
        out_shape=[jax.ShapeDtypeStruct((4, HROWS, w), F32), jax.ShapeDtypeStruct((4, HROWS, w), BF16)],
        grid_spec=pltpu.PrefetchScalarGridSpec(
            num_scalar_prefetch=1, grid=(4, nh),
            in_specs=[pl.BlockSpec((1, SUM_TR, w), lambda d, i, c: (d, c[0] * nh + i, 0)), spec],
            out_specs=[spec, spec]),
        compiler_params=_cp(("parallel", "parallel")))(core.reshape(1).astype(jnp.int32), g, theirs)


def chip_sum(part, chip, others, name):
    r, w = part.shape[1:]

    def body(chip_ref, a_ref, b_ref, o_ref):
        acc = a_ref[0]
        for k in range(3):
            acc = acc + b_ref[k].astype(F32)
        o_ref[...] = acc

    return pl.pallas_call(
        body, name=name, out_shape=jax.ShapeDtypeStruct((r, w), F32),
        grid_spec=pltpu.PrefetchScalarGridSpec(
            num_scalar_prefetch=1, grid=(r // SUM_TR,),
            in_specs=[pl.BlockSpec((1, SUM_TR, w), lambda i, c: (c[0], i, 0)),
                      pl.BlockSpec((3, SUM_TR, w), lambda i, c: (0, i, 0))],
            out_specs=pl.BlockSpec((SUM_TR, w), lambda i, c: (i, 0))),
        compiler_params=_cp(("parallel",)))(chip.reshape(1).astype(jnp.int32), part, others)


def sum_devices(g):
    r = g.shape[1]

    def body(g_ref, o_ref):
        acc = g_ref[0]
        for d in range(1, 8):
            acc = acc + g_ref[d]
        o_ref[...] = acc

    return pl.pallas_call(body, name="sum_devices", out_shape=jax.ShapeDtypeStruct((r, 1024), F32),
                          compiler_params=_cp())(g)


def adamw(w, g, m, v, name):
    r, c = w.shape
    tr = r
    for cand in (256, 128, 64, 32, 16, 8):
        if r % cand == 0 and r > cand:
            tr = cand
            break

    def body(w_ref, g_ref, m_ref, v_ref, d_ref, nm_ref, nv_ref):
        gv = g_ref[...]
        mn = ADAM_B1 * m_ref[...] + (1.0 - ADAM_B1) * gv
        vn = ADAM_B2 * v_ref[...] + (1.0 - ADAM_B2) * (gv * gv)
        m_hat = mn / (1.0 - ADAM_B1 ** ADAM_STEP)
        v_hat = vn / (1.0 - ADAM_B2 ** ADAM_STEP)
        d_ref[...] = -ADAM_LR * (m_hat / (jnp.sqrt(v_hat) + ADAM_EPS) + ADAM_WD * w_ref[...])
        nm_ref[...] = mn
        nv_ref[...] = vn

    spec = pl.BlockSpec((tr, c), lambda i: (i, 0))
    sd = jax.ShapeDtypeStruct((r, c), F32)
    return pl.pallas_call(body, name=name, grid=(r // tr,), in_specs=[spec] * 4, out_specs=[spec] * 3,
                          out_shape=[sd, sd, sd], compiler_params=_cp(("parallel",)))(w, g, m, v)


ANY = pl.BlockSpec(memory_space=pl.ANY)
VM = pl.BlockSpec(memory_space=pltpu.VMEM)
OTHER_CHIPS = ((1, 0), (0, 1), (1, 1))


def _pos():
    return lax.axis_index("x"), lax.axis_index("y"), lax.axis_index("c")


def _flip(v, bit):
    return 1 - v if bit else v


def _rcopy(src, dst, ssem, rsem, peer):
    return pltpu.make_async_remote_copy(src_ref=src, dst_ref=dst, send_sem=ssem, recv_sem=rsem,
                                        device_id=peer, device_id_type=MESH)


def allgather_small(p, name):
    r = p.shape[0]

    def body(in_ref, out_ref, ssem, rsem, lsem):
        x, y, c = _pos()
        me = 4 * x + 2 * y + c
        loc = pltpu.make_async_copy(in_ref, out_ref.at[me], lsem)
        loc.start()
        sends = []
        peers = []
        for k in range(1, 8):
            px, py, pc = _flip(x, (k >> 2) & 1), _flip(y, (k >> 1) & 1), _flip(c, k & 1)
            peers.append((px, py, pc))
            cp = _rcopy(in_ref, out_ref.at[me], ssem.at[k - 1], rsem.at[k - 1], (px, py, pc))
            cp.start()
            sends.append(cp)
        for k in range(1, 8):
            px, py, pc = peers[k - 1]
            _rcopy(in_ref, out_ref.at[4 * px + 2 * py + pc], ssem.at[k - 1], rsem.at[k - 1], (px, py, pc)).wait_recv()
        for cp in sends:
            cp.wait_send()
        loc.wait()

    return pl.pallas_call(
        body, name=name, out_shape=jax.ShapeDtypeStruct((8, r, 1024), F32),
        in_specs=[VM], out_specs=VM,
        scratch_shapes=[pltpu.SemaphoreType.DMA((7,)), pltpu.SemaphoreType.DMA((7,)), pltpu.SemaphoreType.DMA],
    )(p)


def gather_weights(w_in_b, w_rest_b, mod_sh):
    def body(wi_ref, wr_ref, m_ref, gi_ref, gr_ref, mo_ref, ssem, rsem, lsem):
        x, y, c = _pos()
        chip = 2 * x + y
        mine = pl.ds(pl.multiple_of(c * HROWS, 16), HROWS)
        other = pl.ds(pl.multiple_of((1 - c) * HROWS, 16), HROWS)
        sib = (x, y, 1 - c)
        pairs = ((wi_ref, gi_ref), (wr_ref, gr_ref))
        loc_m = pltpu.make_async_copy(m_ref, mo_ref.at[chip], lsem)
        loc_m.start()
        sends = []
        for k, (fx, fy) in enumerate(OTHER_CHIPS):
            peer = (_flip(x, fx), _flip(y, fy), c)
            for a, (w_ref, g_ref) in enumerate(pairs):
                cw = _rcopy(w_ref.at[mine], g_ref.at[chip, mine], ssem.at[6 * a + k], rsem.at[6 * a + k], peer)
                cw.start()
                sends.append(cw)
            cm = _rcopy(m_ref, mo_ref.at[chip], ssem.at[12 + k], rsem.at[12 + k], peer)
            cm.start()
            sends.append(cm)
        for k, (fx, fy) in enumerate(OTHER_CHIPS):
            px, py = _flip(x, fx), _flip(y, fy)
            for a, (w_ref, g_ref) in enumerate(pairs):
                got = g_ref.at[2 * px + py, mine]
                _rcopy(w_ref.at[mine], got, ssem.at[6 * a + k], rsem.at[6 * a + k], (px, py, c)).wait_recv()
                fw = _rcopy(got, got, ssem.at[6 * a + 3 + k], rsem.at[6 * a + 3 + k], sib)
                fw.start()
                sends.append(fw)
        for k, (fx, fy) in enumerate(OTHER_CHIPS):
            px, py = _flip(x, fx), _flip(y, fy)
            for a, (w_ref, g_ref) in enumerate(pairs):
                land = g_ref.at[2 * px + py, other]
                _rcopy(land, land, ssem.at[6 * a + 3 + k], rsem.at[6 * a + 3 + k], sib).wait_recv()
            _rcopy(m_ref, mo_ref.at[2 * px + py], ssem.at[12 + k], rsem.at[12 + k], (px, py, c)).wait_recv()
        for cp in sends:
            cp.wait_send()
        loc_m.wait()

    return pl.pallas_call(
        body, name="gather_weights",
        out_shape=[jax.ShapeDtypeStruct((4, D, SH_IN), BF16), jax.ShapeDtypeStruct((4, D, D), BF16),
                   jax.ShapeDtypeStruct((4, 8, 768), F32)],
        in_specs=[ANY, ANY, VM], out_specs=[ANY, ANY, VM],
        scratch_shapes=[pltpu.SemaphoreType.DMA((15,)), pltpu.SemaphoreType.DMA((15,)), pltpu.SemaphoreType.DMA],
    )(w_in_b, w_rest_b, mod_sh)


def pair_exchange(g_in, g_rest):
    def body(gi_ref, gr_ref, ri_ref, rr_ref, ssem, rsem):
        x, y, c = _pos()
        other = pl.ds(pl.multiple_of((1 - c) * HROWS, 8), HROWS)
        cps = [_rcopy(g_ref.at[:, other, :], r_ref, ssem.at[a], rsem.at[a], (x, y, 1 - c))
               for a, (g_ref, r_ref) in enumerate(((gi_ref, ri_ref), (gr_ref, rr_ref)))]
        for cp in cps:
            cp.start()
        for cp in cps:
            cp.wait()

    return pl.pallas_call(
        body, name="pair_exchange",
        out_shape=[jax.ShapeDtypeStruct((4, HROWS, SH_IN), F32), jax.ShapeDtypeStruct((4, HROWS, D), F32)],
        in_specs=[ANY, ANY], out_specs=[ANY, ANY],
        scratch_shapes=[pltpu.SemaphoreType.DMA((2,)), pltpu.SemaphoreType.DMA((2,))],
    )(g_in, g_rest)


def chip_exchange(pb_in, pb_rest):
    def body(pi_ref, pr_ref, ri_ref, rr_ref, ssem, rsem):
        x, y, c = _pos()
        chip = 2 * x + y
        pairs = ((pi_ref, ri_ref), (pr_ref, rr_ref))
        sends = []
        for k, (fx, fy) in enumerate(OTHER_CHIPS):
            px, py = _flip(x, fx), _flip(y, fy)
            for a, (p_ref, r_ref) in enumerate(pairs):
                cp = _rcopy(p_ref.at[2 * px + py], r_ref.at[k], ssem.at[3 * a + k], rsem.at[3 * a + k], (px, py, c))
                cp.start()
                sends.append(cp)
        for k, (fx, fy) in enumerate(OTHER_CHIPS):
            px, py = _flip(x, fx), _flip(y, fy)
            for a, (p_ref, r_ref) in enumerate(pairs):
                _rcopy(p_ref.at[chip], r_ref.at[k], ssem.at[3 * a + k], rsem.at[3 * a + k], (px, py, c)).wait_recv()
        for cp in sends:
            cp.wait_send()

    return pl.pallas_call(
        body, name="chip_exchange",
        out_shape=[jax.ShapeDtypeStruct((3, HROWS, SH_IN), BF16), jax.ShapeDtypeStruct((3, HROWS, D), BF16)],
        in_specs=[ANY, ANY], out_specs=[ANY, ANY],
        scratch_shapes=[pltpu.SemaphoreType.DMA((6,)), pltpu.SemaphoreType.DMA((6,))],
    )(pb_in, pb_rest)


def pair_swap(red_in, red_rest):
    def body(ai_ref, ar_ref, oi_ref, or_ref, ssem, rsem):
        x, y, c = _pos()
        cps = [_rcopy(a_ref, o_ref, ssem.at[a], rsem.at[a], (x, y, 1 - c))
               for a, (a_ref, o_ref) in enumerate(((ai_ref, oi_ref), (ar_ref, or_ref)))]
        for cp in cps:
            cp.start()
        for cp in cps:
            cp.wait()

    return pl.pallas_call(
        body, name="pair_swap",
        out_shape=[jax.ShapeDtypeStruct((HROWS, SH_IN), F32), jax.ShapeDtypeStruct((HROWS, D), F32)],
        in_specs=[ANY, ANY], out_specs=[ANY, ANY],
        scratch_shapes=[pltpu.SemaphoreType.DMA((2,)), pltpu.SemaphoreType.DMA((2,))],
    )(red_in, red_rest)


def _row(v, width=1024):
    v = v.reshape(-1)
    n = -(-v.shape[0] // width) * width
    return jnp.pad(v, (0, n - v.shape[0])).reshape(-1, width)


def _slots(vs):
    row = [jnp.pad(v.reshape(-1), (0, 128 - v.size)) for v in vs]
    row += [jnp.zeros((128,), F32)] * (8 - len(row))
    return jnp.concatenate(row).reshape(1, 1024)


def _pack_small(b_ada, norm_w, conv_b, ssm_norm_w, q_norm_w, k_norm_w, sinks, dt_bias, a_log, d_skip, rel_bias,
                extra=None):
    misc = [q_norm_w, k_norm_w, sinks, dt_bias, a_log, d_skip] + ([] if extra is None else [extra])
    rows = [_row(b_ada), _row(norm_w), _row(conv_b), _row(ssm_norm_w), _slots(misc), _row(rel_bias)]
    rows.append(jnp.zeros((5, 1024), F32))
    return jnp.concatenate(rows, axis=0)


def _unpack_small(p):
    misc = p[9]
    return dict(b_ada=p[0:3].reshape(1, 3072), norm_w=p[3:4], conv_b=p[4:7].reshape(1, 3072),
                ssm_norm_w=p[7:9].reshape(1, 2048), q_norm_w=misc[None, 0:64], k_norm_w=misc[None, 128:192],
                sinks=misc[None, 256:272], dt_bias=misc[None, 384:416], a_log=misc[None, 512:544],
                d_skip=misc[None, 640:672], rel_bias=p[10, :512].reshape(32, 16), extra=misc[768])


SMALL = ("b_ada", "norm_w", "conv_b", "ssm_norm_w", "q_norm_w", "k_norm_w", "sinks", "dt_bias", "a_log", "d_skip",
         "rel_bias")
WEIGHTS = ("w_ada", "b_ada", "norm_w", "w_in", "q_norm_w", "k_norm_w", "rel_bias", "sinks", "conv_w", "conv_b",
           "dt_bias", "a_log", "d_skip", "ssm_norm_w", "w_attn_proj", "w_ssm_proj", "w_out")
IN_COLS = ((0, 1024, C_Q), (1024, 256, C_K), (1280, 256, C_V), (1536, 1024, C_ZA), (2560, 2048, C_ZM),
           (4608, 3072, C_XBC), (7680, 32, C_DT), (7712, 1024, C_GA), (8736, 1024, C_GB))


def _to_cat(shards):
    parts, pos = [], 0
    for o, n, cnew in sorted(IN_COLS, key=lambda e: e[2]):
        assert cnew == pos
        c0 = o
        while c0 < o + n:
            i = c0 // SH_IN
            c1 = min(o + n, (i + 1) * SH_IN)
            parts.append(shards[i][:, c0 - i * SH_IN:c1 - i * SH_IN])
            c0 = c1
        pos += n
    parts.append(jnp.zeros((D, NP - pos), shards.dtype))
    return jnp.concatenate(parts, axis=1)


def _from_cat(w_cat):
    shards = []
    for i in range(4):
        lo, hi = i * SH_IN, (i + 1) * SH_IN
        parts = []
        for o, n, cnew in IN_COLS:
            a, b = max(o, lo), min(o + n, hi)
            if a < b:
                parts.append(w_cat[:, cnew + a - o:cnew + b - o])
        shards.append(jnp.concatenate(parts, axis=1))
    return jnp.stack(shards)


def kernel(x, c, w_ada, b_ada, norm_w, w_in, q_norm_w, k_norm_w, rel_bias, sinks, conv_w, conv_b, dt_bias, a_log, d_skip, ssm_norm_w, w_attn_proj, w_ssm_proj, w_out, loss_target, m_w_ada, m_b_ada, m_norm_w, m_w_in, m_q_norm_w, m_k_norm_w, m_rel_bias, m_sinks, m_conv_w, m_conv_b, m_dt_bias, m_a_log, m_d_skip, m_ssm_norm_w, m_w_attn_proj, m_w_ssm_proj, m_w_out, v_w_ada, v_b_ada, v_norm_w, v_w_in, v_q_norm_w, v_k_norm_w, v_rel_bias, v_sinks, v_conv_w, v_conv_b, v_dt_bias, v_a_log, v_d_skip, v_ssm_norm_w, v_w_attn_proj, v_w_ssm_proj, v_w_out):
    args = dict(locals())
    xi, yi, ci = lax.axis_index("x"), lax.axis_index("y"), lax.axis_index("c")
    chip = 2 * xi + yi
    me = 4 * xi + 2 * yi + ci
    x2 = x[0]
    tgt = loss_target[0]

    pay = jnp.concatenate([c, conv_w[0].reshape(3, 1024), jnp.zeros((4, 1024), F32)], axis=0)
    g0 = allgather_small(pay, "gather_cond")
    c_all = g0[:, 0, :]
    conv_w_full = g0[0::2, 1:4, :].reshape(4, CONV_K, 768).transpose(1, 0, 2).reshape(CONV_K, XBC)

    b_ada_sh = lax.dynamic_slice(b_ada, (0, chip * 768), (1, 768))
    mod_sh = ada_mod(c_all, w_ada[0], b_ada_sh)

    w_in_b = w_in[0].astype(BF16)
    w_rest_b = jnp.concatenate([w_attn_proj[0], w_ssm_proj[0], w_out[0]], axis=0).astype(BF16)
    wg_in, wg_rest, modg = gather_weights(w_in_b, w_rest_b, mod_sh)
    wg_in = lax.dynamic_update_slice(wg_in, w_in_b[None], (chip, 0, 0))
    wg_rest = lax.dynamic_update_slice(wg_rest, w_rest_b[None], (chip, 0, 0))
    mod = lax.dynamic_slice(modg, (0, me, 0), (4, 1, 768)).reshape(1, 3 * D)
    shift, scale, gate = mod[:, :D], mod[:, D:2 * D], mod[:, 2 * D:]
    wcat = _to_cat(wg_in)
    w_at = wg_rest[:, :R_AT].reshape(D, D)
    w_ss = wg_rest[:, R_AT:R_AT + R_SS].reshape(SSM_W, D)
    w_ou = wg_rest[:, R_AT + R_SS:].reshape(D, D)

    pad128 = lambda v: jnp.pad(v, ((0, 0), (0, 128 - v.shape[1])))
    dtb_p, alog_p, dsk_p = pad128(dt_bias), pad128(a_log), pad128(d_skip)
    bucket = _bucket_table()

    proj, dt_raw, h_t = norm_proj(x2, norm_w, scale, shift, wcat)
    biasm = bias_expand(rel_bias, sinks, bucket)
    ao = attn_fwd(proj, biasm, q_norm_w, k_norm_w)
    act = conv_fwd(proj, conv_w_full, conv_b)
    yss, sprev = ssd_fwd(act, dt_raw, dtb_p, alog_p, dsk_p)

    (loss_p, dy, dao, dza, dga, dgb, dyss, dzm, ua_t, yn_t, mg_t, dya, dyb, dout, dgate, dssm_nw) = tail(
        proj, ao, yss, x2, tgt, gate, ssm_norm_w, w_at, w_ss, w_ou)

    dq, dk, dv, dqw, dkw, dacc = attn_bwd(proj, dao, biasm, q_norm_w, k_norm_w)
    dbias = bias_reduce(dacc, bucket)
    drb = dbias[:, :NBUCKET].T
    dsk = dbias[:, NBUCKET].reshape(1, HQ)
    dact, ddt, ddtb, dalog, ddskip = ssd_bwd(act, dt_raw, dyss, sprev, dtb_p, alog_p, dsk_p)
    dxbc, dconv_w, dconv_b = conv_bwd(proj, dact, conv_w_full, conv_b)

    t = x2.shape[0]
    dproj = jnp.concatenate([dq, dza, dga, dgb, dzm, dxbc, dk, dv, ddt, jnp.zeros((t, NP - C_DT - 128), BF16)], axis=1)
    grad_x, dnorm_w, dscale, dshift = dproj_bwd(dproj, wcat, x2, dy, norm_w, scale)
    dwcat = wgrad(h_t, dproj, "dw_in", TN)
    dw_at = wgrad(ua_t, dya, "dw_attn", 512)
    dw_ss = wgrad(yn_t, dyb, "dw_ssm", 512)
    dw_ou = wgrad(mg_t, dout, "dw_out", 512)

    g_in = _from_cat(dwcat)
    g_rest = jnp.concatenate([dw_at.reshape(4, R_AT, D), dw_ss.reshape(4, R_SS, D), dw_ou.reshape(4, R_OU, D)], axis=1)
    sib_in, sib_rest = pair_exchange(g_in, g_rest)
    part_in, pb_in = pair_sum(g_in, ci, sib_in, "pair_sum_in")
    part_rest, pb_rest = pair_sum(g_rest, ci, sib_rest, "pair_sum_rest")
    oth_in, oth_rest = chip_exchange(pb_in, pb_rest)
    red_in = chip_sum(part_in, chip, oth_in, "chip_sum_in")
    red_rest = chip_sum(part_rest, chip, oth_rest, "chip_sum_rest")
    recv_in, recv_rest = pair_swap(red_in, red_rest)
    both = lambda mine, theirs: jnp.concatenate([jnp.where(ci == 0, mine, theirs), jnp.where(ci == 0, theirs, mine)],
                                                axis=0)
    g_shard_in = both(red_in, recv_in)
    g_shard_rest = both(red_rest, recv_rest)

    dmod = jnp.concatenate([dshift, dscale, dgate], axis=1)
    gsmall = jnp.concatenate([
        _pack_small(dmod, dnorm_w, dconv_b, dssm_nw, dqw, dkw, dsk[:, :HQ], ddtb[:, :SH], dalog[:, :SH],
                    ddskip[:, :SH], drb, extra=loss_p[:, :1]),
        dconv_w.reshape(12, 1024), jnp.zeros((4, 1024), F32)], axis=0)
    gall = allgather_small(gsmall, "gather_small_grads")
    ssum = sum_devices(gall)
    gs = _unpack_small(ssum[:16])
    loss = gs["extra"]
    dconv_w_sh = lax.dynamic_slice(ssum[16:28].reshape(CONV_K, XBC), (0, chip * 768), (CONV_K, 768))
    dmod_all = gall[:, 0:3, :].reshape(8, 3 * D)
    dw_ada = ada_grad(c_all, lax.dynamic_slice(dmod_all, (0, chip * 768), (8, 768)))

    grads = dict(gs)
    grads["w_ada"] = dw_ada
    grads["w_in"] = g_shard_in
    grads["w_attn_proj"] = g_shard_rest[:R_AT]
    grads["w_ssm_proj"] = g_shard_rest[R_AT:R_AT + R_SS]
    grads["w_out"] = g_shard_rest[R_AT + R_SS:]
    grads["conv_w"] = dconv_w_sh

    delta, new_m, new_v = {}, {}, {}
    for n in ("w_ada", "w_in", "conv_w", "w_attn_proj", "w_ssm_proj", "w_out"):
        delta[n], new_m[n], new_v[n] = adamw(args[n][0], grads[n], args["m_" + n][0], args["v_" + n][0], "adamw_" + n)
    ws = _pack_small(*[args[n] for n in SMALL])
    ms = _pack_small(*[args["m_" + n] for n in SMALL])
    vs = _pack_small(*[args["v_" + n] for n in SMALL])
    d_s, m_s, v_s = adamw(ws, ssum[:16], ms, vs, "adamw_small")
    d_s, m_s, v_s = _unpack_small(d_s), _unpack_small(m_s), _unpack_small(v_s)
    for n in SMALL:
        delta[n], new_m[n], new_v[n] = d_s[n], m_s[n], v_s[n]

    def shaped(n, a):
        return a.reshape(args[n].shape)

    outs = [loss, grad_x[None]]
    for table in (grads, delta, new_m, new_v):
        outs += [shaped(n, table[n]) for n in WEIGHTS]
    return tuple(outs)
```

```python
import functools
import math

import numpy as np
import jax
import jax.numpy as jnp
from jax import lax
from jax.experimental import pallas as pl
from jax.experimental.pallas import tpu as pltpu

F32 = jnp.float32
BF16 = jnp.bfloat16
MESH = pl.DeviceIdType.MESH

D = 1024
HQ, HKV, GRP, DH = 16, 4, 4, 64
BLK = 128
NBUCKET, MAXDIST = 32, 128
SSM_W, SH, SG, SR, SP, SN = 2048, 32, 4, 8, 64, 128
CONV_K = 4
XBC = SSM_W + 2 * SG * SN
IN_W = 9760
EPS = 1e-6
NEG = -1e30
SCALE = DH ** -0.5

C_Q, C_ZA, C_GA, C_GB, C_ZM, C_XBC, C_K, C_V, C_DT = 0, 1024, 2048, 3072, 4096, 6144, 9216, 9472, 9728
NP = 9984
TN = 1664

SH_IN = IN_W // 4
R_AT, R_SS, R_OU = 256, 512, 256
HROWS = D // 2

ADAM_LR, ADAM_B1, ADAM_B2, ADAM_EPS, ADAM_WD, ADAM_STEP = 0.001, 0.9, 0.999, 1e-08, 0.01, 10

VMEM_LIMIT = 56 * 1024 * 1024


def _cp(sem=None):
    if sem is None:
        return pltpu.CompilerParams(vmem_limit_bytes=VMEM_LIMIT)
    return pltpu.CompilerParams(dimension_semantics=sem, vmem_limit_bytes=VMEM_LIMIT)


def _sig(x):
    return 0.5 * jnp.tanh(0.5 * x) + 0.5


def _dot(a, b):
    return jnp.dot(a, b, preferred_element_type=F32)


def _dot_nt(a, b):
    return lax.dot_general(a, b, (((1,), (1,)), ((), ())), preferred_element_type=F32)


def _dot_tn(a, b):
    return lax.dot_general(a, b, (((0,), (0,)), ((), ())), preferred_element_type=F32)


def _rsum(x):
    return jnp.sum(x, axis=-1, keepdims=True)


def _csum(x):
    return jnp.sum(x, axis=0, keepdims=True)


def _asum(x):
    return _csum(_rsum(x))


def _full(shape):
    nd = len(shape)
    return pl.BlockSpec(shape, lambda *_: (0,) * nd)


def ada_mod(c_all, w_ada_sh, b_ada_sh):
    def body(c_ref, w_ref, b_ref, o_ref):
        cv = c_ref[...]
        s = cv * _sig(cv)
        o_ref[...] = jnp.dot(s, w_ref[...], preferred_element_type=F32,
                             precision=lax.Precision.HIGHEST) + b_ref[...]

    n = w_ada_sh.shape[1]
    return pl.pallas_call(body, name="ada_mod", out_shape=jax.ShapeDtypeStruct((8, n), F32),
                          compiler_params=_cp())(c_all, w_ada_sh, b_ada_sh)


def ada_grad(c_all, dmod_sh):
    def body(c_ref, d_ref, o_ref):
        cv = c_ref[...]
        s = cv * _sig(cv)
        o_ref[...] = lax.dot_general(s, d_ref[...], (((0,), (0,)), ((), ())), preferred_element_type=F32,
                                     precision=lax.Precision.HIGHEST)

    n = dmod_sh.shape[1]
    return pl.pallas_call(body, name="ada_grad", out_shape=jax.ShapeDtypeStruct((D, n), F32),
                          compiler_params=_cp())(c_all, dmod_sh)


def norm_proj(x, norm_w, scale, shift, wcat):
    t = x.shape[0]
    tm = min(t, 1024)

    def body(x_ref, nw_ref, sc_ref, sh_ref, w_ref, p_ref, dt_ref, ht_ref, hs):
        @pl.when(pl.program_id(1) == 0)
        def _():
            xv = x_ref[...]
            r = lax.rsqrt(jnp.mean(xv * xv, axis=-1, keepdims=True) + EPS)
            h = (xv * r) * nw_ref[...]
            h = h * (1.0 + sc_ref[...]) + sh_ref[...]
            hs[...] = h.astype(BF16)
            ht_ref[...] = h.T.astype(BF16)

        p = _dot(hs[...], w_ref[...])
        p_ref[...] = p.astype(BF16)

        @pl.when(pl.program_id(1) == C_DT // TN)
        def _():
            dt_ref[...] = p[:, C_DT % TN:C_DT % TN + 128]

    vec = pl.BlockSpec((1, D), lambda i, j: (0, 0))
    return pl.pallas_call(
        body, name="norm_proj", grid=(t // tm, NP // TN),
        in_specs=[pl.BlockSpec((tm, D), lambda i, j: (i, 0)), vec, vec, vec,
                  pl.BlockSpec((D, TN), lambda i, j: (0, j))],
        out_specs=[pl.BlockSpec((tm, TN), lambda i, j: (i, j)), pl.BlockSpec((tm, 128), lambda i, j: (i, 0)),
                   pl.BlockSpec((D, tm), lambda i, j: (0, i))],
        out_shape=[jax.ShapeDtypeStruct((t, NP), BF16), jax.ShapeDtypeStruct((t, 128), F32),
                   jax.ShapeDtypeStruct((D, t), BF16)],
        scratch_shapes=[pltpu.VMEM((tm, D), BF16)],
        compiler_params=_cp(("parallel", "arbitrary")),
    )(x, norm_w, scale, shift, wcat)


def _bucket_table():
    qi = jnp.arange(BLK)[:, None]
    kj = jnp.arange(2 * BLK)[None, :]
    dist = qi + BLK - kj
    n = jnp.maximum(dist, 0)
    max_exact = NBUCKET // 2
    nf = jnp.maximum(n, 1).astype(F32)
    large = max_exact + (jnp.log(nf / max_exact) / math.log(MAXDIST / max_exact)
                         * (NBUCKET - max_exact)).astype(jnp.int32)
    large = jnp.minimum(large, NBUCKET - 1)
    bucket = jnp.where(n < max_exact, n, large).astype(jnp.int32)
    valid = (dist >= 0) & (dist < BLK)
    return jnp.where(valid, bucket, -1)


def bias_expand(rel_bias, sinks, bucket):
    def body(rb_ref, sk_ref, bk_ref, o_ref):
        hd = pl.program_id(0)
        bk = bk_ref[...]
        col = lax.broadcasted_iota(jnp.int32, (BLK, 2 * BLK), 1)

        def step(b, acc):
            return jnp.where(bk == b, rb_ref[b, hd], acc)

        acc = lax.fori_loop(0, NBUCKET, step, jnp.full((BLK, 2 * BLK), NEG, F32))
        acc = jnp.where(col == 0, sk_ref[0, hd], acc)
        o_ref[1, 0] = acc
        o_ref[0, 0] = jnp.where(jnp.logical_and(col > 0, col < BLK), NEG, acc)

    smem = pl.BlockSpec(memory_space=pltpu.SMEM)
    return pl.pallas_call(
        body, name="bias_expand", grid=(HQ,),
        in_specs=[smem, smem, _full((BLK, 2 * BLK))],
        out_specs=pl.BlockSpec((2, 1, BLK, 2 * BLK), lambda h: (0, h, 0, 0)),
        out_shape=jax.ShapeDtypeStruct((2, HQ, BLK, 2 * BLK), F32),
        compiler_params=_cp(("arbitrary",)),
    )(rel_bias, sinks, bucket)


def bias_reduce(dacc, bucket):
    col = jnp.arange(BLK * 2 * BLK, dtype=jnp.int32) % (2 * BLK)
    lane = jnp.arange(128, dtype=jnp.int32)[None, :]
    member = (bucket.reshape(-1)[:, None] == lane) | ((col[:, None] == 0) & (lane == NBUCKET))

    def body(d_ref, m_ref, o_ref):
        mm = m_ref[...]
        o_ref[...] = sum(_dot(part, mm) for part in _split3(d_ref[...]))

    return pl.pallas_call(body, name="bias_reduce", out_shape=jax.ShapeDtypeStruct((HQ, 128), F32),
                          compiler_params=_cp())(dacc.reshape(HQ, BLK * 2 * BLK), member.astype(BF16))


GQ = GRP * BLK


def _stack_heads(x, nh):
    return jnp.concatenate([x[:, DH * h:DH * (h + 1)] for h in range(nh)], axis=0)


def _unstack(xs, nh):
    rows = xs.shape[0] // nh
    return jnp.concatenate([xs[rows * h:rows * (h + 1)] for h in range(nh)], axis=1)


def _rms(x):
    return lax.rsqrt(jnp.mean(x * x, axis=-1, keepdims=True) + EPS)


def _stack_q(q, qw):
    qs = _stack_heads(q, HQ)
    r = _rms(qs)
    qhat = qs * r
    return qhat * qw, qhat, r


def _band_first(shape):
    return (lax.broadcasted_iota(jnp.int32, shape, 0) & (2 * BLK - 1)) == 0


def _stack_kv(kp, kc, vp, vc, kw):
    ks = _stack_heads(jnp.concatenate([kp, kc], axis=0), HKV)
    r = _rms(ks)
    khat = ks * r
    first = _band_first(ks.shape)
    kn = jnp.where(first, 0.0, khat * kw)
    v2 = jnp.where(first, 0.0, _stack_heads(jnp.concatenate([vp, vc], axis=0), HKV)).astype(BF16)
    return kn, khat, r, v2


def _softmax_rows(s):
    p = jnp.exp(s - jnp.max(s, axis=-1, keepdims=True))
    return p * (1.0 / _rsum(p))


def attn_fwd(proj, biasm, q_norm_w, k_norm_w):
    t = proj.shape[0]
    nb = t // BLK

    def body(q_ref, kc_ref, kp_ref, vc_ref, vp_ref, bm_ref, qw_ref, kw_ref, o_ref):
        f = lambda ref: ref[...].astype(F32)
        qn = _stack_q(f(q_ref), qw_ref[...])[0].astype(BF16)
        kn, _, _, v2 = _stack_kv(f(kp_ref), f(kc_ref), f(vp_ref), f(vc_ref), kw_ref[...])
        knb = kn.astype(BF16)
        s = jnp.concatenate([_dot_nt(qn[GQ * j:GQ * (j + 1)], knb[2 * BLK * j:2 * BLK * (j + 1)])
                             for j in range(HKV)], axis=0)
        pr = _softmax_rows(s * SCALE + bm_ref[0].reshape(HQ * BLK, 2 * BLK)).astype(BF16)
        o = jnp.concatenate([_dot(pr[GQ * j:GQ * (j + 1)], v2[2 * BLK * j:2 * BLK * (j + 1)])
                             for j in range(HKV)], axis=0)
        o_ref[...] = _unstack(o, HQ).astype(BF16)

    kblk, vblk = C_K // 256, C_V // 256
    prev = lambda n: jnp.maximum(n - 1, 0)
    return pl.pallas_call(
        body, name="attn_fwd", grid=(nb,),
        in_specs=[pl.BlockSpec((BLK, D), lambda n: (n, 0)),
                  pl.BlockSpec((BLK, 256), lambda n: (n, kblk)),
                  pl.BlockSpec((BLK, 256), lambda n: (prev(n), kblk)),
                  pl.BlockSpec((BLK, 256), lambda n: (n, vblk)),
                  pl.BlockSpec((BLK, 256), lambda n: (prev(n), vblk)),
                  pl.BlockSpec((1, HQ, BLK, 2 * BLK), lambda n: (jnp.minimum(n, 1), 0, 0, 0)),
                  _full((1, DH)), _full((1, DH))],
        out_specs=pl.BlockSpec((BLK, D), lambda n: (n, 0)),
        out_shape=jax.ShapeDtypeStruct((t, D), BF16),
        compiler_params=_cp(("parallel",)),
    )(proj, proj, proj, proj, proj, biasm, q_norm_w, k_norm_w)


def attn_bwd(proj, dao, biasm, q_norm_w, k_norm_w):
    t = proj.shape[0]
    nb = t // BLK
    kb = 2 * BLK

    def body(q_ref, kc_ref, kp_ref, vc_ref, vp_ref, do_ref, bm_ref, qw_ref, kw_ref,
             dq_ref, dk_ref, dv_ref, dqw_ref, dkw_ref, dacc_ref, ck, cv, pk, pv, nk, nv):
        n = pl.program_id(0)

        @pl.when(n == 0)
        def _():
            for ref in (dqw_ref, dkw_ref, dacc_ref, ck, cv):
                ref[...] = jnp.zeros_like(ref)

        qw = qw_ref[...]
        kw = kw_ref[...]
        f = lambda ref: ref[...].astype(F32)
        kn, khat, rk, v2 = _stack_kv(f(kp_ref), f(kc_ref), f(vp_ref), f(vc_ref), kw)
        grp = lambda a, j: a[GQ * j:GQ * (j + 1)]
        band = lambda a, j: a[kb * j:kb * (j + 1)]

        @pl.when(n < nb)
        def _():
            qn, qhat, rq = _stack_q(f(q_ref), qw)
            qnb = qn.astype(BF16)
            knb = kn.astype(BF16)
            dos = _stack_heads(f(do_ref), HQ).astype(BF16)
            s = jnp.concatenate([_dot_nt(grp(qnb, j), band(knb, j)) for j in range(HKV)], axis=0)
            pr = _softmax_rows(s * SCALE + bm_ref[0].reshape(HQ * BLK, kb))
            dp = jnp.concatenate([_dot_nt(grp(dos, j), band(v2, j)) for j in range(HKV)], axis=0)
            ds = pr * (dp - _rsum(pr * dp))
            dacc_ref[...] += ds.reshape(HQ, BLK, kb)
            dsb = ds.astype(BF16)
            prb = pr.astype(BF16)
            dqn = jnp.concatenate([_dot(grp(dsb, j), band(knb, j)) for j in range(HKV)], axis=0) * SCALE
            dqhat = dqn * qw
            dq = rq * (dqhat - qhat * jnp.mean(dqhat * qhat, axis=-1, keepdims=True))
            dq_ref[...] = _unstack(dq, HQ).astype(BF16)
            dqw_ref[...] += _csum(dqn * qhat)
            first = _band_first((kb, DH))
            for j in range(HKV):
                rows = slice(BLK * j, BLK * (j + 1))
                dkn = jnp.where(first, 0.0, _dot_tn(grp(dsb, j), grp(qnb, j)) * SCALE)
                dvj = jnp.where(first, 0.0, _dot_tn(grp(prb, j), grp(dos, j)))
                pk[rows, :] = dkn[:BLK]
                nk[rows, :] = dkn[BLK:]
                pv[rows, :] = dvj[:BLK]
                nv[rows, :] = dvj[BLK:]

        @pl.when(n == nb)
        def _():
            for ref in (pk, pv, nk, nv):
                ref[...] = jnp.zeros_like(ref)

        khp = jnp.concatenate([khat[kb * j:kb * j + BLK] for j in range(HKV)], axis=0)
        rkp = jnp.concatenate([rk[kb * j:kb * j + BLK] for j in range(HKV)], axis=0)
        dkn = ck[...] + pk[...]
        dkhat = dkn * kw
        dk = rkp * (dkhat - khp * jnp.mean(dkhat * khp, axis=-1, keepdims=True))
        dk_ref[...] = _unstack(dk, HKV).astype(BF16)
        dkw_ref[...] += _csum(dkn * khp)
        dv_ref[...] = _unstack(cv[...] + pv[...], HKV).astype(BF16)
        ck[...] = nk[...]
        cv[...] = nv[...]

    kblk, vblk = C_K // 256, C_V // 256
    cur = lambda n: jnp.minimum(n, nb - 1)
    prev = lambda n: jnp.maximum(n - 1, 0)
    carry = pltpu.VMEM((HKV * BLK, DH), F32)
    return pl.pallas_call(
        body, name="attn_bwd", grid=(nb + 1,),
        in_specs=[pl.BlockSpec((BLK, D), lambda n: (cur(n), 0)),
                  pl.BlockSpec((BLK, 256), lambda n: (cur(n), kblk)), pl.BlockSpec((BLK, 256), lambda n: (prev(n), kblk)),
                  pl.BlockSpec((BLK, 256), lambda n: (cur(n), vblk)), pl.BlockSpec((BLK, 256), lambda n: (prev(n), vblk)),
                  pl.BlockSpec((BLK, D), lambda n: (cur(n), 0)),
                  pl.BlockSpec((1, HQ, BLK, kb), lambda n: (jnp.minimum(n, 1), 0, 0, 0)),
                  _full((1, DH)), _full((1, DH))],
        out_specs=[pl.BlockSpec((BLK, D), lambda n: (cur(n), 0)),
                   pl.BlockSpec((BLK, 256), lambda n: (prev(n), 0)), pl.BlockSpec((BLK, 256), lambda n: (prev(n), 0)),
                   _full((1, DH)), _full((1, DH)), _full((HQ, BLK, kb))],
        out_shape=[jax.ShapeDtypeStruct((t, D), BF16), jax.ShapeDtypeStruct((t, 256), BF16),
                   jax.ShapeDtypeStruct((t, 256), BF16), jax.ShapeDtypeStruct((1, DH), F32),
                   jax.ShapeDtypeStruct((1, DH), F32), jax.ShapeDtypeStruct((HQ, BLK, kb), F32)],
        scratch_shapes=[carry] * 6,
        compiler_params=_cp(("arbitrary",)),
    )(proj, proj, proj, proj, proj, dao, biasm, q_norm_w, k_norm_w)


CONV_TM, CONV_CW, CONV_RC, HALO = 512, 512, 32, 16


def conv_fwd(proj, conv_w, conv_b):
    t = proj.shape[0]
    tm = min(t, CONV_TM)
    c0 = C_XBC // CONV_CW

    def body(x_ref, xp_ref, w_ref, b_ref, o_ref, ds_ref):
        i = pl.program_id(1)
        w = w_ref[...]
        b = b_ref[...]
        for r in range(tm // CONV_RC):
            lo = r * CONV_RC
            if r == 0:
                head = jnp.where(i == 0, 0.0, xp_ref[...].astype(F32))
                win = jnp.concatenate([head, x_ref[0:CONV_RC, :].astype(F32)], axis=0)
            else:
                win = x_ref[lo - HALO:lo + CONV_RC, :].astype(F32)
            acc = b
            for j in range(CONV_K):
                acc = acc + w[j:j + 1] * win[HALO - 3 + j:HALO - 3 + j + CONV_RC]
            sg = _sig(acc)
            o_ref[lo:lo + CONV_RC, :] = acc * sg
            ds_ref[lo:lo + CONV_RC, :] = _dsilu(acc, sg).astype(BF16)

    rh = tm // HALO
    tile = pl.BlockSpec((tm, CONV_CW), lambda s, i: (i, s))
    return pl.pallas_call(
        body, name="conv_fwd", grid=(XBC // CONV_CW, t // tm),
        in_specs=[pl.BlockSpec((tm, CONV_CW), lambda s, i: (i, c0 + s)),
                  pl.BlockSpec((HALO, CONV_CW), lambda s, i: (jnp.maximum(i * rh - 1, 0), c0 + s)),
                  pl.BlockSpec((CONV_K, CONV_CW), lambda s, i: (0, s)), pl.BlockSpec((1, CONV_CW), lambda s, i: (0, s))],
        out_specs=[tile, tile],
        out_shape=[jax.ShapeDtypeStruct((t, XBC), F32), jax.ShapeDtypeStruct((t, XBC), BF16)],
        compiler_params=_cp(("parallel", "parallel")),
    )(proj, proj, conv_w, conv_b)


def conv_bwd(proj, dact, dsl, conv_w):
    t = proj.shape[0]
    tm = min(t, CONV_TM)
    nt = t // tm
    nr = tm // CONV_RC
    c0 = C_XBC // CONV_CW
    ext = CONV_RC + 8

    def body(x_ref, xp_ref, d_ref, dn_ref, s_ref, sn_ref, w_ref, dx_ref, dw_ref, db_ref):
        i = pl.program_id(1)

        @pl.when(i == 0)
        def _():
            dw_ref[...] = jnp.zeros_like(dw_ref)
            db_ref[...] = jnp.zeros_like(db_ref)

        w = w_ref[...]
        dws = [jnp.zeros((1, CONV_CW), F32) for _ in range(CONV_K)]
        db = jnp.zeros((1, CONV_CW), F32)
        for r in range(nr):
            lo = r * CONV_RC
            if r == 0:
                head = jnp.where(i == 0, 0.0, xp_ref[...].astype(F32))
                win = jnp.concatenate([head, x_ref[0:CONV_RC, :].astype(F32)], axis=0)
            else:
                win = x_ref[lo - HALO:lo + CONV_RC, :].astype(F32)
            if r < nr - 1:
                dext = d_ref[lo:lo + ext, :]
                sext = s_ref[lo:lo + CONV_RC + HALO, :].astype(F32)[0:ext]
            else:
                dext = jnp.concatenate([d_ref[lo:lo + CONV_RC, :], jnp.where(i == nt - 1, 0.0, dn_ref[...])], axis=0)
                sext = jnp.concatenate([s_ref[lo:lo + CONV_RC, :].astype(F32), sn_ref[...].astype(F32)], axis=0)[0:ext]
            dpre = dext * sext
            dx = jnp.zeros((CONV_RC, CONV_CW), F32)
            own = dpre[0:CONV_RC]
            for j in range(CONV_K):
                dx = dx + w[j:j + 1] * dpre[3 - j:3 - j + CONV_RC]
                dws[j] = dws[j] + _csum(own * win[HALO - 3 + j:HALO - 3 + j + CONV_RC])
            db = db + _csum(own)
            dx_ref[lo:lo + CONV_RC, :] = dx.astype(BF16)
        dw_ref[...] += jnp.concatenate(dws, axis=0)
        db_ref[...] += db

    rh = tm // HALO
    r8 = tm // 8
    nxt = lambda i, per: jnp.minimum((i + 1) * per, nt * per - 1)
    return pl.pallas_call(
        body, name="conv_bwd", grid=(XBC // CONV_CW, nt),
        in_specs=[pl.BlockSpec((tm, CONV_CW), lambda s, i: (i, c0 + s)),
                  pl.BlockSpec((HALO, CONV_CW), lambda s, i: (jnp.maximum(i * rh - 1, 0), c0 + s)),
                  pl.BlockSpec((tm, CONV_CW), lambda s, i: (i, s)),
                  pl.BlockSpec((8, CONV_CW), lambda s, i: (nxt(i, r8), s)),
                  pl.BlockSpec((tm, CONV_CW), lambda s, i: (i, s)),
                  pl.BlockSpec((HALO, CONV_CW), lambda s, i: (nxt(i, rh), s)),
                  pl.BlockSpec((CONV_K, CONV_CW), lambda s, i: (0, s))],
        out_specs=[pl.BlockSpec((tm, CONV_CW), lambda s, i: (i, s)),
                   pl.BlockSpec((CONV_K, CONV_CW), lambda s, i: (0, s)), pl.BlockSpec((1, CONV_CW), lambda s, i: (0, s))],
        out_shape=[jax.ShapeDtypeStruct((t, XBC), BF16), jax.ShapeDtypeStruct((CONV_K, XBC), F32),
                   jax.ShapeDtypeStruct((1, XBC), F32)],
        compiler_params=_cp(("parallel", "arbitrary")),
    )(proj, proj, dact, dact, dsl, dsl, conv_w)


def _split3(x):
    h = x.astype(BF16)
    r = x - h.astype(F32)
    m = r.astype(BF16)
    lo = (r - m.astype(F32)).astype(BF16)
    return h, m, lo


def _tri_mm(tri, x):
    h, m, lo = _split3(x)
    return _dot(tri, h) + _dot(tri, m) + _dot(tri, lo)


def _softplus(x):
    return jnp.maximum(x, 0.0) + jnp.log1p(jnp.exp(-jnp.abs(x)))


def _chunk_decays(dt_raw, dtb, alog):
    dtv = _softplus(dt_raw + dtb)
    a = -jnp.exp(alog)
    ri = lax.broadcasted_iota(jnp.int32, (BLK, BLK), 0)
    ci = lax.broadcasted_iota(jnp.int32, (BLK, BLK), 1)
    causal = ri >= ci
    acum = _tri_mm(causal.astype(BF16), dtv * a)
    return dtv, a, causal, acum, acum.T


NPAIR = SH // 2


def _pairs(x):
    return jnp.stack([x[:, 128 * k:128 * (k + 1)] for k in range(NPAIR)])


def _unpairs(x3):
    return jnp.concatenate([x3[k] for k in range(NPAIR)], axis=1)


def _per_head_cols(m):
    return jnp.stack([jnp.broadcast_to(m[:, h:h + 1], m.shape) for h in range(SH)])


def _pair_lanes(t):
    r = t.reshape(NPAIR, 2, t.shape[1], 128)
    lo = lax.broadcasted_iota(jnp.int32, (1, t.shape[1], 128), 2) < SP
    return jnp.where(lo, r[:, 0], r[:, 1])


class _Chunk:
    pass


def _chunk_common(dt_raw, dtb, alog, dskip):
    cm = _Chunk()
    cm.dtv, cm.a, cm.causal, acum, acum_t = _chunk_decays(dt_raw, dtb, alog)
    cm.acol = _per_head_cols(acum)
    cm.arow = jnp.stack([acum_t[h:h + 1, :] for h in range(SH)])
    cm.lam = jnp.exp(jnp.where(cm.causal[None], cm.acol - cm.arow, NEG))
    apl = _pair_lanes(cm.acol)
    alast = apl[:, BLK - 1:BLK, :]
    cm.dpl = _pair_lanes(_per_head_cols(cm.dtv))
    cm.eapl = jnp.exp(apl)
    cm.epl = jnp.exp(alast - apl)
    cm.cdpl = jnp.exp(alast)
    cm.dskpl = _pair_lanes(_per_head_cols(dskip))
    cm.lo = lax.broadcasted_iota(jnp.int32, (1, BLK, 128), 2) < SP
    return cm


def ssd_fwd(act, dt_raw, dtb_p, alog_p, dsk_p):
    t = act.shape[0]
    nc = t // BLK

    def body(xs_ref, b_ref, c_ref, dt_ref, dtb_ref, al_ref, dk_ref, y_ref, sp_ref, st):
        c = pl.program_id(0)

        @pl.when(c == 0)
        def _():
            st[...] = jnp.zeros_like(st)

        s_t = st[...]
        sp_ref[0] = s_t
        cm = _chunk_common(dt_ref[...], dtb_ref[...], al_ref[...], dk_ref[...])
        gms, cbs, bts = [], [], []
        for g in range(SG):
            bf = b_ref[:, SN * g:SN * (g + 1)]
            cb = c_ref[:, SN * g:SN * (g + 1)].astype(BF16)
            gms.append(_dot_nt(cb, bf.astype(BF16)))
            cbs.append(cb)
            bts.append(bf.T.astype(BF16))
        m = (cm.lam.reshape(SG, SR, BLK, BLK) * jnp.stack(gms)[:, None]).reshape(SH, BLK, BLK).astype(BF16)
        xs16 = _pairs(xs_ref[...])
        xdt16 = xs16 * cm.dpl
        x_lo = jnp.where(cm.lo, xdt16, 0.0).astype(BF16)
        x_hi = jnp.where(cm.lo, 0.0, xdt16).astype(BF16)
        s16 = _pairs(s_t)
        s16b = s16.astype(BF16)
        yd = jnp.stack([_dot(m[2 * k], x_lo[k]) + _dot(m[2 * k + 1], x_hi[k]) for k in range(NPAIR)])
        yo = jnp.stack([_dot(cbs[k // (NPAIR // SG)], s16b[k]) for k in range(NPAIR)])
        y_ref[...] = _unpairs(yd + yo * cm.eapl + cm.dskpl * xs16).astype(BF16)
        xe = (xdt16 * cm.epl).astype(BF16)
        st[...] = _unpairs(cm.cdpl * s16 + jnp.stack([_dot(bts[k // (NPAIR // SG)], xe[k]) for k in range(NPAIR)]))

    vec = _full((1, 128))
    return pl.pallas_call(
        body, name="ssd_fwd", grid=(nc,),
        in_specs=[pl.BlockSpec((BLK, SSM_W), lambda c: (c, 0)),
                  pl.BlockSpec((BLK, SG * SN), lambda c: (c, SSM_W // (SG * SN))),
                  pl.BlockSpec((BLK, SG * SN), lambda c: (c, SSM_W // (SG * SN) + 1)),
                  pl.BlockSpec((BLK, 128), lambda c: (c, 0)), vec, vec, vec],
        out_specs=[pl.BlockSpec((BLK, SSM_W), lambda c: (c, 0)), pl.BlockSpec((1, SN, SSM_W), lambda c: (c, 0, 0))],
        out_shape=[jax.ShapeDtypeStruct((t, SSM_W), BF16), jax.ShapeDtypeStruct((nc, SN, SSM_W), F32)],
        scratch_shapes=[pltpu.VMEM((SN, SSM_W), F32)],
        compiler_params=_cp(("arbitrary",)),
    )(act, act, act, dt_raw, dtb_p, alog_p, dsk_p)


def _head_sums(q):
    r = q.shape[1]
    lo = lax.broadcasted_iota(jnp.int32, (1, r, 128), 2) < SP
    s_lo = jnp.sum(jnp.where(lo, q, 0.0), axis=-1, keepdims=True)
    s_hi = jnp.sum(jnp.where(lo, 0.0, q), axis=-1, keepdims=True)
    lane = lax.broadcasted_iota(jnp.int32, (r, 128), 1)
    out = jnp.zeros((r, 128), F32)
    for k in range(NPAIR):
        out = jnp.where(lane == 2 * k, s_lo[k], jnp.where(lane == 2 * k + 1, s_hi[k], out))
    return out


def ssd_bwd(act, dt_raw, dy, sprev, dtb_p, alog_p, dsk_p):
    t = act.shape[0]
    nc = t // BLK

    def body(xs_ref, b_ref, c_ref, dt_ref, dy_ref, sp_ref, dtb_ref, al_ref, dk_ref,
             da_ref, ddt_ref, ddtb_ref, dal_ref, ddk_ref, dst):
        i = pl.program_id(0)

        @pl.when(i == 0)
        def _():
            dst[...] = jnp.zeros_like(dst)
            ddtb_ref[...] = jnp.zeros_like(ddtb_ref)
            dal_ref[...] = jnp.zeros_like(dal_ref)
            ddk_ref[...] = jnp.zeros_like(ddk_ref)

        dt_raw = dt_ref[...]
        dtb = dtb_ref[...]
        cm = _chunk_common(dt_raw, dtb, al_ref[...], dk_ref[...])
        ri = lax.broadcasted_iota(jnp.int32, (BLK, BLK), 0)
        ci = lax.broadcasted_iota(jnp.int32, (BLK, BLK), 1)
        lam_t = jnp.exp(jnp.where((ri <= ci)[None], cm.arow - cm.acol, NEG))
        bbs, cbs, cts, gms = [], [], [], []
        for g in range(SG):
            bf = b_ref[:, SN * g:SN * (g + 1)]
            cf = c_ref[:, SN * g:SN * (g + 1)]
            bbs.append(bf.astype(BF16))
            cbs.append(cf.astype(BF16))
            cts.append(cf.T.astype(BF16))
            gms.append(_dot_nt(bbs[g], cbs[g]))
        grp = lambda k: k // (NPAIR // SG)
        xs16 = _pairs(xs_ref[...])
        dy16 = _pairs(dy_ref[...].astype(F32))
        sp16 = _pairs(sp_ref[0])
        ds16 = _pairs(dst[...])
        xdt16 = xs16 * cm.dpl
        xdtb = xdt16.astype(BF16)
        dyh = [jnp.where(cm.lo, dy16, 0.0).astype(BF16), jnp.where(cm.lo, 0.0, dy16).astype(BF16)]
        m_t = (lam_t.reshape(SG, SR, BLK, BLK) * jnp.stack(gms)[:, None]).reshape(SH, BLK, BLK).astype(BF16)
        dxdt = jnp.stack([_dot(m_t[2 * k], dyh[0][k]) + _dot(m_t[2 * k + 1], dyh[1][k]) for k in range(NPAIR)])
        dm = jnp.stack([_dot_nt(dyh[h % 2][h // 2], xdtb[h // 2]) for h in range(SH)])
        dgl = (dm * cm.lam).reshape(SG, SR, BLK, BLK)
        dg = jnp.sum(dgl, axis=1).astype(BF16)
        w = (dgl * jnp.stack([_dot_nt(cbs[g], bbs[g]) for g in range(SG)])[:, None]).reshape(SH, BLK, BLK)
        w_rows = jnp.sum(w, axis=2, keepdims=True)
        w_cols = jnp.concatenate([jnp.sum(w, axis=1)] + [jnp.zeros((128 - SH, BLK), F32)], axis=0).T
        lane_c = lax.broadcasted_iota(jnp.int32, (BLK, 128), 1)
        da_cols = -w_cols
        for h in range(SH):
            da_cols = jnp.where(lane_c == h, da_cols + w_rows[h], da_cols)
        ds16b = ds16.astype(BF16)
        sp16b = sp16.astype(BF16)
        dxs = jnp.stack([_dot(bbs[grp(k)], ds16b[k]) for k in range(NPAIR)]) * cm.epl
        dxdt = dxdt + dxs
        dya = (dy16 * cm.eapl).astype(BF16)
        xe = (xdt16 * cm.epl).astype(BF16)
        dcs, dbs = [], []
        for g in range(SG):
            ks = range(g * (NPAIR // SG), (g + 1) * (NPAIR // SG))
            dcs.append(sum(_dot_nt(dya[k], sp16b[k]) for k in ks) + _dot(dg[g], bbs[g]))
            dbs.append(sum(_dot_nt(xe[k], ds16b[k]) for k in ks) + _dot_tn(dg[g], cbs[g]))
        dst[...] = _unpairs(cm.cdpl * ds16 + jnp.stack([_dot(cts[grp(k)], dya[k]) for k in range(NPAIR)]))
        da_ref[...] = jnp.concatenate([_unpairs(dxdt * cm.dpl + cm.dskpl * dy16)] + dbs + dcs, axis=1)
        y_off = jnp.stack([_dot(cbs[grp(k)], sp16b[k]) for k in range(NPAIR)]) * cm.eapl
        da_cols = da_cols + _head_sums(dy16 * y_off - xdt16 * dxs)
        last = _head_sums(jnp.sum(xdt16 * dxs, axis=1, keepdims=True)
                          + cm.cdpl * jnp.sum(ds16 * sp16, axis=1, keepdims=True))
        ddt = _head_sums(dxdt * xs16)
        row_i = lax.broadcasted_iota(jnp.int32, (BLK, 128), 0)
        dacum = da_cols + jnp.where(row_i == BLK - 1, last, 0.0)
        dda = _tri_mm((ri <= ci).astype(BF16), dacum)
        ddt = ddt + dda * cm.a
        dal_ref[...] += _csum(dda * cm.dtv) * cm.a
        ddt_raw = jnp.where(lane_c < SH, ddt * _sig(dt_raw + dtb), 0.0)
        ddt_ref[...] = ddt_raw.astype(BF16)
        ddtb_ref[...] += _csum(ddt_raw)
        ddk_ref[...] += _head_sums(jnp.sum(dy16 * xs16, axis=1, keepdims=True))

    rev = lambda i: nc - 1 - i
    vec = _full((1, 128))
    slab = pl.BlockSpec((BLK, SSM_W), lambda i: (rev(i), 0))
    return pl.pallas_call(
        body, name="ssd_bwd", grid=(nc,),
        in_specs=[slab,
                  pl.BlockSpec((BLK, SG * SN), lambda i: (rev(i), SSM_W // (SG * SN))),
                  pl.BlockSpec((BLK, SG * SN), lambda i: (rev(i), SSM_W // (SG * SN) + 1)),
                  pl.BlockSpec((BLK, 128), lambda i: (rev(i), 0)),
                  slab,
                  pl.BlockSpec((1, SN, SSM_W), lambda i: (rev(i), 0, 0)), vec, vec, vec],
        out_specs=[pl.BlockSpec((BLK, XBC), lambda i: (rev(i), 0)), pl.BlockSpec((BLK, 128), lambda i: (rev(i), 0)),
                   vec, vec, vec],
        out_shape=[jax.ShapeDtypeStruct((t, XBC), F32), jax.ShapeDtypeStruct((t, 128), BF16),
                   jax.ShapeDtypeStruct((1, 128), F32), jax.ShapeDtypeStruct((1, 128), F32),
                   jax.ShapeDtypeStruct((1, 128), F32)],
        scratch_shapes=[pltpu.VMEM((SN, SSM_W), F32)],
        compiler_params=_cp(("arbitrary",)),
    )(act, act, act, dt_raw, dy, sprev, dtb_p, alog_p, dsk_p)


TAIL_TM = 256


def _dsilu(z, s):
    return s * (1.0 + z * (1.0 - s))


def tail(proj, ao, yss, x, target, gate, ssm_nw, w_at, w_ss, w_ou):
    t = x.shape[0]
    tm = min(t, TAIL_TM)
    gw = SSM_W // SG

    def body(ao_ref, za_ref, ga_ref, gb_ref, zm_ref, ys_ref, x_ref, tg_ref, gt_ref, nw_ref, wa_ref, ws_ref, wo_ref,
             loss_ref, dy_ref, dao_ref, dza_ref, dga_ref, dgb_ref, dys_ref, dzm_ref,
             ua_ref, yn_ref, mg_ref, dya_ref, dyb_ref, do_ref, dgt_ref, dnw_ref):
        i = pl.program_id(0)

        @pl.when(i == 0)
        def _():
            loss_ref[...] = jnp.zeros_like(loss_ref)
            dgt_ref[...] = jnp.zeros_like(dgt_ref)
            dnw_ref[...] = jnp.zeros_like(dnw_ref)

        ao = ao_ref[...].astype(F32)
        za = za_ref[...].astype(F32)
        sa = _sig(za)
        sila = za * sa
        ua_f = ao * sila
        ua = ua_f.astype(BF16)
        ya = _dot(ua, wa_ref[...])
        zm = zm_ref[...].astype(F32)
        sm = _sig(zm)
        silm = zm * sm
        ys = ys_ref[...].astype(F32)
        u = ys * silm
        nw = nw_ref[...]
        rs, uns = [], []
        for g in range(SG):
            ug = u[:, gw * g:gw * (g + 1)]
            r = lax.rsqrt(jnp.mean(ug * ug, axis=-1, keepdims=True) + EPS)
            rs.append(r)
            uns.append(ug * r)
        un = jnp.concatenate(uns, axis=1)
        yn_f = un * nw
        yn = yn_f.astype(BF16)
        yb = _dot(yn, ws_ref[...])
        sga = _sig(ga_ref[...].astype(F32))
        sgb = _sig(gb_ref[...].astype(F32))
        mg_f = sga * ya + sgb * yb
        mg = mg_f.astype(BF16)
        o = _dot(mg, wo_ref[...])
        gt = gt_ref[...]
        err = (x_ref[...] + gt * o) - tg_ref[...]
        lane = lax.broadcasted_iota(jnp.int32, (1, 128), 1)
        loss_ref[...] += jnp.where(lane == 0, 0.5 * _asum(_rsum(err * err) / D), 0.0)
        dy = err * (1.0 / D)
        dy_ref[...] = dy
        dgt_ref[...] += _csum(dy * o)
        do = (dy * gt).astype(BF16)
        dmg = _dot_nt(do, wo_ref[...])
        dga_ref[...] = (dmg * ya * sga * (1.0 - sga)).astype(BF16)
        dgb_ref[...] = (dmg * yb * sgb * (1.0 - sgb)).astype(BF16)
        dya = (dmg * sga).astype(BF16)
        dyb = (dmg * sgb).astype(BF16)
        dua = _dot_nt(dya, wa_ref[...])
        dao_ref[...] = (dua * sila).astype(BF16)
        dza_ref[...] = (dua * ao * _dsilu(za, sa)).astype(BF16)
        dyn = _dot_nt(dyb, ws_ref[...])
        dnw_ref[...] += _csum(dyn * un)
        dun = dyn * nw
        dus = []
        for g in range(SG):
            gs = slice(gw * g, gw * (g + 1))
            dus.append(rs[g] * (dun[:, gs] - uns[g] * jnp.mean(dun[:, gs] * uns[g], axis=-1, keepdims=True)))
        du = jnp.concatenate(dus, axis=1)
        dys_ref[...] = (du * silm).astype(BF16)
        dzm_ref[...] = (du * ys * _dsilu(zm, sm)).astype(BF16)
        ua_ref[...] = ua_f.T.astype(BF16)
        yn_ref[...] = yn_f.T.astype(BF16)
        mg_ref[...] = mg_f.T.astype(BF16)
        dya_ref[...] = dya
        dyb_ref[...] = dyb
        do_ref[...] = do

    row = lambda w: pl.BlockSpec((tm, w), lambda i: (i, 0))
    pcol = lambda w, c0: pl.BlockSpec((tm, w), lambda i: (i, c0 // w))
    sd = lambda w, dt: jax.ShapeDtypeStruct((t, w), dt)
    colt = lambda w: pl.BlockSpec((w, tm), lambda i: (0, i))
    sdt = lambda w: jax.ShapeDtypeStruct((w, t), BF16)
    return pl.pallas_call(
        body, name="tail", grid=(t // tm,),
        in_specs=[row(D), pcol(D, C_ZA), pcol(D, C_GA), pcol(D, C_GB), pcol(SSM_W, C_ZM), row(SSM_W), row(D), row(D),
                  _full((1, D)), _full((1, SSM_W)), _full((D, D)), _full((SSM_W, D)), _full((D, D))],
        out_specs=[_full((1, 128)), row(D), row(D), row(D), row(D), row(D), row(SSM_W), row(SSM_W),
                   colt(D), colt(SSM_W), colt(D), row(D), row(D), row(D), _full((1, D)), _full((1, SSM_W))],
        out_shape=[jax.ShapeDtypeStruct((1, 128), F32), sd(D, F32), sd(D, BF16), sd(D, BF16), sd(D, BF16), sd(D, BF16),
                   sd(SSM_W, BF16), sd(SSM_W, BF16), sdt(D), sdt(SSM_W), sdt(D), sd(D, BF16),
                   sd(D, BF16), sd(D, BF16), jax.ShapeDtypeStruct((1, D), F32), jax.ShapeDtypeStruct((1, SSM_W), F32)],
        compiler_params=_cp(("arbitrary",)),
    )(ao, proj, proj, proj, proj, yss, x, target, gate, ssm_nw, w_at, w_ss, w_ou)


def dproj_bwd(dproj, wcat, x, dy, norm_w, scale):
    t = x.shape[0]
    tm = min(t, 512)
    tk = NP // 3
    nk = NP // tk
    nt = t // tm

    def body(dp_ref, w_ref, x_ref, dy_ref, nw_ref, sc_ref, gx_ref, dnw_ref, dsc_ref, dsh_ref, acc, dwe_ref):
        i = pl.program_id(0)
        k = pl.program_id(1)

        @pl.when(jnp.logical_and(i == 0, k == 0))
        def _():
            dwe_ref[...] = jnp.zeros_like(dwe_ref)
            dsh_ref[...] = jnp.zeros_like(dsh_ref)
            dnw_ref[...] = jnp.zeros_like(dnw_ref)
            dsc_ref[...] = jnp.zeros_like(dsc_ref)

        part = _dot_nt(dp_ref[...], w_ref[...])

        @pl.when(k == 0)
        def _():
            acc[...] = part

        @pl.when(k > 0)
        def _():
            acc[...] += part

        @pl.when(k == nk - 1)
        def _():
            dh = acc[...]
            xv = x_ref[...]
            r = lax.rsqrt(jnp.mean(xv * xv, axis=-1, keepdims=True) + EPS)
            xn = xv * r
            weff = nw_ref[...] * (1.0 + sc_ref[...])
            dxn = dh * weff
            gx_ref[...] = dy_ref[...] + r * (dxn - xn * jnp.mean(dxn * xn, axis=-1, keepdims=True))
            dwe_ref[...] += _csum(dh * xn)
            dsh_ref[...] += _csum(dh)

        @pl.when(jnp.logical_and(i == nt - 1, k == nk - 1))
        def _():
            dwe = dwe_ref[...]
            dnw_ref[...] = dwe * (1.0 + sc_ref[...])
            dsc_ref[...] = dwe * nw_ref[...]

    vec = pl.BlockSpec((1, D), lambda i, k: (0, 0))
    row = pl.BlockSpec((tm, D), lambda i, k: (i, 0))
    return pl.pallas_call(
        body, name="dproj_bwd", grid=(nt, nk),
        in_specs=[pl.BlockSpec((tm, tk), lambda i, k: (i, k)), pl.BlockSpec((D, tk), lambda i, k: (0, k)),
                  row, row, vec, vec],
        out_specs=[row, vec, vec, vec],
        out_shape=[jax.ShapeDtypeStruct((t, D), F32), jax.ShapeDtypeStruct((1, D), F32),
                   jax.ShapeDtypeStruct((1, D), F32), jax.ShapeDtypeStruct((1, D), F32)],
        scratch_shapes=[pltpu.VMEM((tm, D), F32), pltpu.VMEM((1, D), F32)],
        compiler_params=_cp(("arbitrary", "arbitrary")),
    )(dproj, wcat, x, dy, norm_w, scale)


def wgrad(at, b, name, bn):
    m, t = at.shape
    n = b.shape[1]
    tk = min(t, 1024)
    bm = min(m, 1024)

    def body(a_ref, b_ref, o_ref):
        part = _dot(a_ref[...], b_ref[...])

        @pl.when(pl.program_id(2) == 0)
        def _():
            o_ref[...] = part

        @pl.when(pl.program_id(2) > 0)
        def _():
            o_ref[...] += part

    return pl.pallas_call(
        body, name=name, grid=(m // bm, n // bn, t // tk),
        in_specs=[pl.BlockSpec((bm, tk), lambda i, j, k: (i, k)), pl.BlockSpec((tk, bn), lambda i, j, k: (k, j))],
        out_specs=pl.BlockSpec((bm, bn), lambda i, j, k: (i, j)),
        out_shape=jax.ShapeDtypeStruct((m, n), F32),
        compiler_params=_cp(("parallel", "parallel", "arbitrary")),
    )(at, b)


SUM_TR = 256


def pair_sum(g, core, theirs, name):
    w = g.shape[2]
    nh = HROWS // SUM_TR

    def body(core_ref, a_ref, b_ref, o_ref, ob_ref):
        s = a_ref[...] + b_ref[...]
        o_ref[...] = s
        ob_ref[...] = s.astype(BF16)

    spec = pl.BlockSpec((1, SUM_TR, w), lambda d, i, c: (d, i, 0))
    return pl.pallas_call(
        body, name=name,
        out_shape=[jax.ShapeDtypeStruct((4, HROWS, w), F32), jax.ShapeDtypeStruct((4, HROWS, w), BF16)],
        grid_spec=pltpu.PrefetchScalarGridSpec(
            num_scalar_prefetch=1, grid=(4, nh),
            in_specs=[pl.BlockSpec((1, SUM_TR, w), lambda d, i, c: (d, c[0] * nh + i, 0)), spec],
            out_specs=[spec, spec]),
        compiler_params=_cp(("parallel", "parallel")))(core.reshape(1).astype(jnp.int32), g, theirs)


def chip_sum(part, chip, others, name):
    r, w = part.shape[1:]

    def body(chip_ref, a_ref, b_ref, o_ref):
        acc = a_ref[0]
        for k in range(3):
            acc = acc + b_ref[k].astype(F32)
        o_ref[...] = acc

    return pl.pallas_call(
        body, name=name, out_shape=jax.ShapeDtypeStruct((r, w), F32),
        grid_spec=pltpu.PrefetchScalarGridSpec(
            num_scalar_prefetch=1, grid=(r // SUM_TR,),
            in_specs=[pl.BlockSpec((1, SUM_TR, w), lambda i, c: (c[0], i, 0)),
                      pl.BlockSpec((3, SUM_TR, w), lambda i, c: (0, i, 0))],
            out_specs=pl.BlockSpec((SUM_TR, w), lambda i, c: (i, 0))),
        compiler_params=_cp(("parallel",)))(chip.reshape(1).astype(jnp.int32), part, others)


def sum_devices(g):
    r = g.shape[1]

    def body(g_ref, o_ref):
        acc = g_ref[0]
        for d in range(1, 8):
            acc = acc + g_ref[d]
        o_ref[...] = acc

    return pl.pallas_call(body, name="sum_devices", out_shape=jax.ShapeDtypeStruct((r, 1024), F32),
                          compiler_params=_cp())(g)


def adamw(w, g, m, v, name):
    r, c = w.shape
    tr = r
    for cand in (256, 128, 64, 32, 16, 8):
        if r % cand == 0 and r > cand:
            tr = cand
            break

    def body(w_ref, g_ref, m_ref, v_ref, d_ref, nm_ref, nv_ref):
        gv = g_ref[...]
        mn = ADAM_B1 * m_ref[...] + (1.0 - ADAM_B1) * gv
        vn = ADAM_B2 * v_ref[...] + (1.0 - ADAM_B2) * (gv * gv)
        m_hat = mn / (1.0 - ADAM_B1 ** ADAM_STEP)
        v_hat = vn / (1.0 - ADAM_B2 ** ADAM_STEP)
        d_ref[...] = -ADAM_LR * (m_hat / (jnp.sqrt(v_hat) + ADAM_EPS) + ADAM_WD * w_ref[...])
        nm_ref[...] = mn
        nv_ref[...] = vn

    spec = pl.BlockSpec((tr, c), lambda i: (i, 0))
    sd = jax.ShapeDtypeStruct((r, c), F32)
    return pl.pallas_call(body, name=name, grid=(r // tr,), in_specs=[spec] * 4, out_specs=[spec] * 3,
                          out_shape=[sd, sd, sd], compiler_params=_cp(("parallel",)))(w, g, m, v)


ANY = pl.BlockSpec(memory_space=pl.ANY)
VM = pl.BlockSpec(memory_space=pltpu.VMEM)
OTHER_CHIPS = ((1, 0), (0, 1), (1, 1))


def _pos():
    return lax.axis_index("x"), lax.axis_index("y"), lax.axis_index("c")


def _flip(v, bit):
    return 1 - v if bit else v


def _rcopy(src, dst, ssem, rsem, peer):
    return pltpu.make_async_remote_copy(src_ref=src, dst_ref=dst, send_sem=ssem, recv_sem=rsem,
                                        device_id=peer, device_id_type=MESH)


def allgather_small(p, name):
    r = p.shape[0]

    def body(in_ref, out_ref, ssem, rsem, lsem):
        x, y, c = _pos()
        me = 4 * x + 2 * y + c
        loc = pltpu.make_async_copy(in_ref, out_ref.at[me], lsem)
        loc.start()
        sends = []
        peers = []
        for k in range(1, 8):
            px, py, pc = _flip(x, (k >> 2) & 1), _flip(y, (k >> 1) & 1), _flip(c, k & 1)
            peers.append((px, py, pc))
            cp = _rcopy(in_ref, out_ref.at[me], ssem.at[k - 1], rsem.at[k - 1], (px, py, pc))
            cp.start()
            sends.append(cp)
        for k in range(1, 8):
            px, py, pc = peers[k - 1]
            _rcopy(in_ref, out_ref.at[4 * px + 2 * py + pc], ssem.at[k - 1], rsem.at[k - 1], (px, py, pc)).wait_recv()
        for cp in sends:
            cp.wait_send()
        loc.wait()

    return pl.pallas_call(
        body, name=name, out_shape=jax.ShapeDtypeStruct((8, r, 1024), F32),
        in_specs=[VM], out_specs=VM,
        scratch_shapes=[pltpu.SemaphoreType.DMA((7,)), pltpu.SemaphoreType.DMA((7,)), pltpu.SemaphoreType.DMA],
    )(p)


def gather_weights(w_in_b, w_rest_b, mod_sh):
    def body(wi_ref, wr_ref, m_ref, gi_ref, gr_ref, mo_ref, ssem, rsem, lsem):
        x, y, c = _pos()
        chip = 2 * x + y
        mine = pl.ds(pl.multiple_of(c * HROWS, 16), HROWS)
        other = pl.ds(pl.multiple_of((1 - c) * HROWS, 16), HROWS)
        sib = (x, y, 1 - c)
        pairs = ((wi_ref, gi_ref), (wr_ref, gr_ref))
        loc_m = pltpu.make_async_copy(m_ref, mo_ref.at[chip], lsem)
        loc_m.start()
        sends = []
        for k, (fx, fy) in enumerate(OTHER_CHIPS):
            peer = (_flip(x, fx), _flip(y, fy), c)
            for a, (w_ref, g_ref) in enumerate(pairs):
                cw = _rcopy(w_ref.at[mine], g_ref.at[chip, mine], ssem.at[6 * a + k], rsem.at[6 * a + k], peer)
                cw.start()
                sends.append(cw)
            cm = _rcopy(m_ref, mo_ref.at[chip], ssem.at[12 + k], rsem.at[12 + k], peer)
            cm.start()
            sends.append(cm)
        for k, (fx, fy) in enumerate(OTHER_CHIPS):
            px, py = _flip(x, fx), _flip(y, fy)
            for a, (w_ref, g_ref) in enumerate(pairs):
                got = g_ref.at[2 * px + py, mine]
                _rcopy(w_ref.at[mine], got, ssem.at[6 * a + k], rsem.at[6 * a + k], (px, py, c)).wait_recv()
                fw = _rcopy(got, got, ssem.at[6 * a + 3 + k], rsem.at[6 * a + 3 + k], sib)
                fw.start()
                sends.append(fw)
        for k, (fx, fy) in enumerate(OTHER_CHIPS):
            px, py = _flip(x, fx), _flip(y, fy)
            for a, (w_ref, g_ref) in enumerate(pairs):
                land = g_ref.at[2 * px + py, other]
                _rcopy(land, land, ssem.at[6 * a + 3 + k], rsem.at[6 * a + 3 + k], sib).wait_recv()
            _rcopy(m_ref, mo_ref.at[2 * px + py], ssem.at[12 + k], rsem.at[12 + k], (px, py, c)).wait_recv()
        for cp in sends:
            cp.wait_send()
        loc_m.wait()

    return pl.pallas_call(
        body, name="gather_weights",
        out_shape=[jax.ShapeDtypeStruct((4, D, SH_IN), BF16), jax.ShapeDtypeStruct((4, D, D), BF16),
                   jax.ShapeDtypeStruct((4, 8, 768), F32)],
        in_specs=[ANY, ANY, VM], out_specs=[ANY, ANY, VM],
        scratch_shapes=[pltpu.SemaphoreType.DMA((15,)), pltpu.SemaphoreType.DMA((15,)), pltpu.SemaphoreType.DMA],
    )(w_in_b, w_rest_b, mod_sh)


def pair_exchange(g_in, g_rest):
    def body(gi_ref, gr_ref, ri_ref, rr_ref, ssem, rsem):
        x, y, c = _pos()
        other = pl.ds(pl.multiple_of((1 - c) * HROWS, 8), HROWS)
        cps = [_rcopy(g_ref.at[:, other, :], r_ref, ssem.at[a], rsem.at[a], (x, y, 1 - c))
               for a, (g_ref, r_ref) in enumerate(((gi_ref, ri_ref), (gr_ref, rr_ref)))]
        for cp in cps:
            cp.start()
        for cp in cps:
            cp.wait()

    return pl.pallas_call(
        body, name="pair_exchange",
        out_shape=[jax.ShapeDtypeStruct((4, HROWS, SH_IN), F32), jax.ShapeDtypeStruct((4, HROWS, D), F32)],
        in_specs=[ANY, ANY], out_specs=[ANY, ANY],
        scratch_shapes=[pltpu.SemaphoreType.DMA((2,)), pltpu.SemaphoreType.DMA((2,))],
    )(g_in, g_rest)


def chip_exchange(pb_in, pb_rest):
    def body(pi_ref, pr_ref, ri_ref, rr_ref, ssem, rsem):
        x, y, c = _pos()
        chip = 2 * x + y
        pairs = ((pi_ref, ri_ref), (pr_ref, rr_ref))
        sends = []
        for k, (fx, fy) in enumerate(OTHER_CHIPS):
            px, py = _flip(x, fx), _flip(y, fy)
            for a, (p_ref, r_ref) in enumerate(pairs):
                cp = _rcopy(p_ref.at[2 * px + py], r_ref.at[k], ssem.at[3 * a + k], rsem.at[3 * a + k], (px, py, c))
                cp.start()
                sends.append(cp)
        for k, (fx, fy) in enumerate(OTHER_CHIPS):
            px, py = _flip(x, fx), _flip(y, fy)
            for a, (p_ref, r_ref) in enumerate(pairs):
                _rcopy(p_ref.at[chip], r_ref.at[k], ssem.at[3 * a + k], rsem.at[3 * a + k], (px, py, c)).wait_recv()
        for cp in sends:
            cp.wait_send()

    return pl.pallas_call(
        body, name="chip_exchange",
        out_shape=[jax.ShapeDtypeStruct((3, HROWS, SH_IN), BF16), jax.ShapeDtypeStruct((3, HROWS, D), BF16)],
        in_specs=[ANY, ANY], out_specs=[ANY, ANY],
        scratch_shapes=[pltpu.SemaphoreType.DMA((6,)), pltpu.SemaphoreType.DMA((6,))],
    )(pb_in, pb_rest)


def pair_swap(red_in, red_rest):
    def body(ai_ref, ar_ref, oi_ref, or_ref, ssem, rsem):
        x, y, c = _pos()
        cps = [_rcopy(a_ref, o_ref, ssem.at[a], rsem.at[a], (x, y, 1 - c))
               for a, (a_ref, o_ref) in enumerate(((ai_ref, oi_ref), (ar_ref, or_ref)))]
        for cp in cps:
            cp.start()
        for cp in cps:
            cp.wait()

    return pl.pallas_call(
        body, name="pair_swap",
        out_shape=[jax.ShapeDtypeStruct((HROWS, SH_IN), F32), jax.ShapeDtypeStruct((HROWS, D), F32)],
        in_specs=[ANY, ANY], out_specs=[ANY, ANY],
        scratch_shapes=[pltpu.SemaphoreType.DMA((2,)), pltpu.SemaphoreType.DMA((2,))],
    )(red_in, red_rest)


def _row(v, width=1024):
    v = v.reshape(-1)
    n = -(-v.shape[0] // width) * width
    return jnp.pad(v, (0, n - v.shape[0])).reshape(-1, width)


def _slots(vs):
    row = [jnp.pad(v.reshape(-1), (0, 128 - v.size)) for v in vs]
    row += [jnp.zeros((128,), F32)] * (8 - len(row))
    return jnp.concatenate(row).reshape(1, 1024)


def _pack_small(b_ada, norm_w, conv_b, ssm_norm_w, q_norm_w, k_norm_w, sinks, dt_bias, a_log, d_skip, rel_bias,
                extra=None):
    misc = [q_norm_w, k_norm_w, sinks, dt_bias, a_log, d_skip] + ([] if extra is None else [extra])
    rows = [_row(b_ada), _row(norm_w), _row(conv_b), _row(ssm_norm_w), _slots(misc), _row(rel_bias)]
    rows.append(jnp.zeros((5, 1024), F32))
    return jnp.concatenate(rows, axis=0)


def _unpack_small(p):
    misc = p[9]
    return dict(b_ada=p[0:3].reshape(1, 3072), norm_w=p[3:4], conv_b=p[4:7].reshape(1, 3072),
                ssm_norm_w=p[7:9].reshape(1, 2048), q_norm_w=misc[None, 0:64], k_norm_w=misc[None, 128:192],
                sinks=misc[None, 256:272], dt_bias=misc[None, 384:416], a_log=misc[None, 512:544],
                d_skip=misc[None, 640:672], rel_bias=p[10, :512].reshape(32, 16), extra=misc[768])


SMALL = ("b_ada", "norm_w", "conv_b", "ssm_norm_w", "q_norm_w", "k_norm_w", "sinks", "dt_bias", "a_log", "d_skip",
         "rel_bias")
WEIGHTS = ("w_ada", "b_ada", "norm_w", "w_in", "q_norm_w", "k_norm_w", "rel_bias", "sinks", "conv_w", "conv_b",
           "dt_bias", "a_log", "d_skip", "ssm_norm_w", "w_attn_proj", "w_ssm_proj", "w_out")
IN_COLS = ((0, 1024, C_Q), (1024, 256, C_K), (1280, 256, C_V), (1536, 1024, C_ZA), (2560, 2048, C_ZM),
           (4608, 3072, C_XBC), (7680, 32, C_DT), (7712, 1024, C_GA), (8736, 1024, C_GB))


def _to_cat(shards):
    parts, pos = [], 0
    for o, n, cnew in sorted(IN_COLS, key=lambda e: e[2]):
        assert cnew == pos
        c0 = o
        while c0 < o + n:
            i = c0 // SH_IN
            c1 = min(o + n, (i + 1) * SH_IN)
            parts.append(shards[i][:, c0 - i * SH_IN:c1 - i * SH_IN])
            c0 = c1
        pos += n
    parts.append(jnp.zeros((D, NP - pos), shards.dtype))
    return jnp.concatenate(parts, axis=1)


def _from_cat(w_cat):
    shards = []
    for i in range(4):
        lo, hi = i * SH_IN, (i + 1) * SH_IN
        parts = []
        for o, n, cnew in IN_COLS:
            a, b = max(o, lo), min(o + n, hi)
            if a < b:
                parts.append(w_cat[:, cnew + a - o:cnew + b - o])
        shards.append(jnp.concatenate(parts, axis=1))
    return jnp.stack(shards)


def kernel(x, c, w_ada, b_ada, norm_w, w_in, q_norm_w, k_norm_w, rel_bias, sinks, conv_w, conv_b, dt_bias, a_log, d_skip, ssm_norm_w, w_attn_proj, w_ssm_proj, w_out, loss_target, m_w_ada, m_b_ada, m_norm_w, m_w_in, m_q_norm_w, m_k_norm_w, m_rel_bias, m_sinks, m_conv_w, m_conv_b, m_dt_bias, m_a_log, m_d_skip, m_ssm_norm_w, m_w_attn_proj, m_w_ssm_proj, m_w_out, v_w_ada, v_b_ada, v_norm_w, v_w_in, v_q_norm_w, v_k_norm_w, v_rel_bias, v_sinks, v_conv_w, v_conv_b, v_dt_bias, v_a_log, v_d_skip, v_ssm_norm_w, v_w_attn_proj, v_w_ssm_proj, v_w_out):
    args = dict(locals())
    xi, yi, ci = lax.axis_index("x"), lax.axis_index("y"), lax.axis_index("c")
    chip = 2 * xi + yi
    me = 4 * xi + 2 * yi + ci
    x2 = x[0]
    tgt = loss_target[0]

    pay = jnp.concatenate([c, conv_w[0].reshape(3, 1024), jnp.zeros((4, 1024), F32)], axis=0)
    g0 = allgather_small(pay, "gather_cond")
    c_all = g0[:, 0, :]
    conv_w_full = g0[0::2, 1:4, :].reshape(4, CONV_K, 768).transpose(1, 0, 2).reshape(CONV_K, XBC)

    b_ada_sh = lax.dynamic_slice(b_ada, (0, chip * 768), (1, 768))
    mod_sh = ada_mod(c_all, w_ada[0], b_ada_sh)

    w_in_b = w_in[0].astype(BF16)
    w_rest_b = jnp.concatenate([w_attn_proj[0], w_ssm_proj[0], w_out[0]], axis=0).astype(BF16)
    wg_in, wg_rest, modg = gather_weights(w_in_b, w_rest_b, mod_sh)
    wg_in = lax.dynamic_update_slice(wg_in, w_in_b[None], (chip, 0, 0))
    wg_rest = lax.dynamic_update_slice(wg_rest, w_rest_b[None], (chip, 0, 0))
    mod = lax.dynamic_slice(modg, (0, me, 0), (4, 1, 768)).reshape(1, 3 * D)
    shift, scale, gate = mod[:, :D], mod[:, D:2 * D], mod[:, 2 * D:]
    wcat = _to_cat(wg_in)
    w_at = wg_rest[:, :R_AT].reshape(D, D)
    w_ss = wg_rest[:, R_AT:R_AT + R_SS].reshape(SSM_W, D)
    w_ou = wg_rest[:, R_AT + R_SS:].reshape(D, D)

    pad128 = lambda v: jnp.pad(v, ((0, 0), (0, 128 - v.shape[1])))
    dtb_p, alog_p, dsk_p = pad128(dt_bias), pad128(a_log), pad128(d_skip)
    bucket = _bucket_table()

    proj, dt_raw, h_t = norm_proj(x2, norm_w, scale, shift, wcat)
    biasm = bias_expand(rel_bias, sinks, bucket)
    ao = attn_fwd(proj, biasm, q_norm_w, k_norm_w)
    act, dsl = conv_fwd(proj, conv_w_full, conv_b)
    yss, sprev = ssd_fwd(act, dt_raw, dtb_p, alog_p, dsk_p)

    (loss_p, dy, dao, dza, dga, dgb, dyss, dzm, ua_t, yn_t, mg_t, dya, dyb, dout, dgate, dssm_nw) = tail(
        proj, ao, yss, x2, tgt, gate, ssm_norm_w, w_at, w_ss, w_ou)

    dq, dk, dv, dqw, dkw, dacc = attn_bwd(proj, dao, biasm, q_norm_w, k_norm_w)
    dbias = bias_reduce(dacc, bucket)
    drb = dbias[:, :NBUCKET].T
    dsk = dbias[:, NBUCKET].reshape(1, HQ)
    dact, ddt, ddtb, dalog, ddskip = ssd_bwd(act, dt_raw, dyss, sprev, dtb_p, alog_p, dsk_p)
    dxbc, dconv_w, dconv_b = conv_bwd(proj, dact, dsl, conv_w_full)

    t = x2.shape[0]
    dproj = jnp.concatenate([dq, dza, dga, dgb, dzm, dxbc, dk, dv, ddt, jnp.zeros((t, NP - C_DT - 128), BF16)], axis=1)
    grad_x, dnorm_w, dscale, dshift = dproj_bwd(dproj, wcat, x2, dy, norm_w, scale)
    dwcat = wgrad(h_t, dproj, "dw_in", TN)
    dw_at = wgrad(ua_t, dya, "dw_attn", 512)
    dw_ss = wgrad(yn_t, dyb, "dw_ssm", 512)
    dw_ou = wgrad(mg_t, dout, "dw_out", 512)

    g_in = _from_cat(dwcat)
    g_rest = jnp.concatenate([dw_at.reshape(4, R_AT, D), dw_ss.reshape(4, R_SS, D), dw_ou.reshape(4, R_OU, D)], axis=1)
    sib_in, sib_rest = pair_exchange(g_in, g_rest)
    part_in, pb_in = pair_sum(g_in, ci, sib_in, "pair_sum_in")
    part_rest, pb_rest = pair_sum(g_rest, ci, sib_rest, "pair_sum_rest")
    oth_in, oth_rest = chip_exchange(pb_in, pb_rest)
    red_in = chip_sum(part_in, chip, oth_in, "chip_sum_in")
    red_rest = chip_sum(part_rest, chip, oth_rest, "chip_sum_rest")
    recv_in, recv_rest = pair_swap(red_in, red_rest)
    both = lambda mine, theirs: jnp.concatenate([jnp.where(ci == 0, mine, theirs), jnp.where(ci == 0, theirs, mine)],
                                                axis=0)
    g_shard_in = both(red_in, recv_in)
    g_shard_rest = both(red_rest, recv_rest)

    dmod = jnp.concatenate([dshift, dscale, dgate], axis=1)
    gsmall = jnp.concatenate([
        _pack_small(dmod, dnorm_w, dconv_b, dssm_nw, dqw, dkw, dsk[:, :HQ], ddtb[:, :SH], dalog[:, :SH],
                    ddskip[:, :SH], drb, extra=loss_p[:, :1]),
        dconv_w.reshape(12, 1024), jnp.zeros((4, 1024), F32)], axis=0)
    gall = allgather_small(gsmall, "gather_small_grads")
    ssum = sum_devices(gall)
    gs = _unpack_small(ssum[:16])
    loss = gs["extra"]
    dconv_w_sh = lax.dynamic_slice(ssum[16:28].reshape(CONV_K, XBC), (0, chip * 768), (CONV_K, 768))
    dmod_all = gall[:, 0:3, :].reshape(8, 3 * D)
    dw_ada = ada_grad(c_all, lax.dynamic_slice(dmod_all, (0, chip * 768), (8, 768)))

    grads = dict(gs)
    grads["w_ada"] = dw_ada
    grads["w_in"] = g_shard_in
    grads["w_attn_proj"] = g_shard_rest[:R_AT]
    grads["w_ssm_proj"] = g_shard_rest[R_AT:R_AT + R_SS]
    grads["w_out"] = g_shard_rest[R_AT + R_SS:]
    grads["conv_w"] = dconv_w_sh

    delta, new_m, new_v = {}, {}, {}
    for n in ("w_ada", "w_in", "conv_w", "w_attn_proj", "w_ssm_proj", "w_out"):
        delta[n], new_m[n], new_v[n] = adamw(args[n][0], grads[n], args["m_" + n][0], args["v_" + n][0], "adamw_" + n)
    ws = _pack_small(*[args[n] for n in SMALL])
    ms = _pack_small(*[args["m_" + n] for n in SMALL])
    vs = _pack_small(*[args["v_" + n] for n in SMALL])
    d_s, m_s, v_s = adamw(ws, ssum[:16], ms, vs, "adamw_small")
    d_s, m_s, v_s = _unpack_small(d_s), _unpack_small(m_s), _unpack_small(v_s)
    for n in SMALL:
        delta[n], new_m[n], new_v[n] = d_s[n], m_s[n], v_s[n]

    def shaped(n, a):
        return a.reshape(args[n].shape)

    outs = [loss, grad_x[None]]
    for table in (grads, delta, new_m, new_v):
        outs += [shaped(n, table[n]) for n in WEIGHTS]
    return tuple(outs)
```

```python
import functools
import math

import numpy as np
import jax
import jax.numpy as jnp
from jax import lax
from jax.experimental import pallas as pl
from jax.experimental.pallas import tpu as pltpu

F32 = jnp.float32
BF16 = jnp.bfloat16
MESH = pl.DeviceIdType.MESH

D = 1024
HQ, HKV, GRP, DH = 16, 4, 4, 64
BLK = 128
NBUCKET, MAXDIST = 32, 128
SSM_W, SH, SG, SR, SP, SN = 2048, 32, 4, 8, 64, 128
CONV_K = 4
XBC = SSM_W + 2 * SG * SN
IN_W = 9760
EPS = 1e-6
NEG = -1e30
SCALE = DH ** -0.5

C_Q, C_ZA, C_GA, C_GB, C_ZM, C_XBC, C_K, C_V, C_DT = 0, 1024, 2048, 3072, 4096, 6144, 9216, 9472, 9728
NP = 9984
TN = 1664

SH_IN = IN_W // 4
R_AT, R_SS, R_OU = 256, 512, 256
HROWS = D // 2

ADAM_LR, ADAM_B1, ADAM_B2, ADAM_EPS, ADAM_WD, ADAM_STEP = 0.001, 0.9, 0.999, 1e-08, 0.01, 10

VMEM_LIMIT = 56 * 1024 * 1024


def _cp(sem=None):
    if sem is None:
        return pltpu.CompilerParams(vmem_limit_bytes=VMEM_LIMIT)
    return pltpu.CompilerParams(dimension_semantics=sem, vmem_limit_bytes=VMEM_LIMIT)


def _sig(x):
    return 0.5 * jnp.tanh(0.5 * x) + 0.5


def _dot(a, b):
    return jnp.dot(a, b, preferred_element_type=F32)


def _dot_nt(a, b):
    return lax.dot_general(a, b, (((1,), (1,)), ((), ())), preferred_element_type=F32)


def _dot_tn(a, b):
    return lax.dot_general(a, b, (((0,), (0,)), ((), ())), preferred_element_type=F32)


def _rsum(x):
    return jnp.sum(x, axis=-1, keepdims=True)


def _csum(x):
    return jnp.sum(x, axis=0, keepdims=True)


def _asum(x):
    return _csum(_rsum(x))


def _full(shape):
    nd = len(shape)
    return pl.BlockSpec(shape, lambda *_: (0,) * nd)


def ada_mod(c_all, w_ada_sh, b_ada_sh):
    def body(c_ref, w_ref, b_ref, o_ref):
        cv = c_ref[...]
        s = cv * _sig(cv)
        o_ref[...] = jnp.dot(s, w_ref[...], preferred_element_type=F32,
                             precision=lax.Precision.HIGHEST) + b_ref[...]

    n = w_ada_sh.shape[1]
    return pl.pallas_call(body, name="ada_mod", out_shape=jax.ShapeDtypeStruct((8, n), F32),
                          compiler_params=_cp())(c_all, w_ada_sh, b_ada_sh)


def ada_grad(c_all, dmod_sh):
    def body(c_ref, d_ref, o_ref):
        cv = c_ref[...]
        s = cv * _sig(cv)
        o_ref[...] = lax.dot_general(s, d_ref[...], (((0,), (0,)), ((), ())), preferred_element_type=F32,
                                     precision=lax.Precision.HIGHEST)

    n = dmod_sh.shape[1]
    return pl.pallas_call(body, name="ada_grad", out_shape=jax.ShapeDtypeStruct((D, n), F32),
                          compiler_params=_cp())(c_all, dmod_sh)


def norm_proj(x, norm_w, scale, shift, wcat):
    t = x.shape[0]
    tm = min(t, 1024)

    def body(x_ref, nw_ref, sc_ref, sh_ref, w_ref, p_ref, dt_ref, ht_ref, hs):
        @pl.when(pl.program_id(1) == 0)
        def _():
            xv = x_ref[...]
            r = lax.rsqrt(jnp.mean(xv * xv, axis=-1, keepdims=True) + EPS)
            h = (xv * r) * nw_ref[...]
            h = h * (1.0 + sc_ref[...]) + sh_ref[...]
            hs[...] = h.astype(BF16)
            ht_ref[...] = h.T.astype(BF16)

        p = _dot(hs[...], w_ref[...])
        p_ref[...] = p.astype(BF16)

        @pl.when(pl.program_id(1) == C_DT // TN)
        def _():
            dt_ref[...] = p[:, C_DT % TN:C_DT % TN + 128]

    vec = pl.BlockSpec((1, D), lambda i, j: (0, 0))
    return pl.pallas_call(
        body, name="norm_proj", grid=(t // tm, NP // TN),
        in_specs=[pl.BlockSpec((tm, D), lambda i, j: (i, 0)), vec, vec, vec,
                  pl.BlockSpec((D, TN), lambda i, j: (0, j))],
        out_specs=[pl.BlockSpec((tm, TN), lambda i, j: (i, j)), pl.BlockSpec((tm, 128), lambda i, j: (i, 0)),
                   pl.BlockSpec((D, tm), lambda i, j: (0, i))],
        out_shape=[jax.ShapeDtypeStruct((t, NP), BF16), jax.ShapeDtypeStruct((t, 128), F32),
                   jax.ShapeDtypeStruct((D, t), BF16)],
        scratch_shapes=[pltpu.VMEM((tm, D), BF16)],
        compiler_params=_cp(("parallel", "arbitrary")),
    )(x, norm_w, scale, shift, wcat)


def _bucket_table():
    qi = jnp.arange(BLK)[:, None]
    kj = jnp.arange(2 * BLK)[None, :]
    dist = qi + BLK - kj
    n = jnp.maximum(dist, 0)
    max_exact = NBUCKET // 2
    nf = jnp.maximum(n, 1).astype(F32)
    large = max_exact + (jnp.log(nf / max_exact) / math.log(MAXDIST / max_exact)
                         * (NBUCKET - max_exact)).astype(jnp.int32)
    large = jnp.minimum(large, NBUCKET - 1)
    bucket = jnp.where(n < max_exact, n, large).astype(jnp.int32)
    valid = (dist >= 0) & (dist < BLK)
    return jnp.where(valid, bucket, -1)


def bias_expand(rel_bias, sinks, bucket):
    def body(rb_ref, sk_ref, bk_ref, o_ref):
        hd = pl.program_id(0)
        bk = bk_ref[...]
        col = lax.broadcasted_iota(jnp.int32, (BLK, 2 * BLK), 1)

        def step(b, acc):
            return jnp.where(bk == b, rb_ref[b, hd], acc)

        acc = lax.fori_loop(0, NBUCKET, step, jnp.full((BLK, 2 * BLK), NEG, F32))
        acc = jnp.where(col == 0, sk_ref[0, hd], acc)
        o_ref[1, 0] = acc
        o_ref[0, 0] = jnp.where(jnp.logical_and(col > 0, col < BLK), NEG, acc)

    smem = pl.BlockSpec(memory_space=pltpu.SMEM)
    return pl.pallas_call(
        body, name="bias_expand", grid=(HQ,),
        in_specs=[smem, smem, _full((BLK, 2 * BLK))],
        out_specs=pl.BlockSpec((2, 1, BLK, 2 * BLK), lambda h: (0, h, 0, 0)),
        out_shape=jax.ShapeDtypeStruct((2, HQ, BLK, 2 * BLK), F32),
        compiler_params=_cp(("arbitrary",)),
    )(rel_bias, sinks, bucket)


def bias_reduce(dacc, bucket):
    col = jnp.arange(BLK * 2 * BLK, dtype=jnp.int32) % (2 * BLK)
    lane = jnp.arange(128, dtype=jnp.int32)[None, :]
    member = (bucket.reshape(-1)[:, None] == lane) | ((col[:, None] == 0) & (lane == NBUCKET))

    def body(d_ref, m_ref, o_ref):
        mm = m_ref[...]
        o_ref[...] = sum(_dot(part, mm) for part in _split3(d_ref[...]))

    return pl.pallas_call(body, name="bias_reduce", out_shape=jax.ShapeDtypeStruct((HQ, 128), F32),
                          compiler_params=_cp())(dacc.reshape(HQ, BLK * 2 * BLK), member.astype(BF16))


GQ = GRP * BLK


def _stack_heads(x, nh):
    return jnp.concatenate([x[:, DH * h:DH * (h + 1)] for h in range(nh)], axis=0)


def _unstack(xs, nh):
    rows = xs.shape[0] // nh
    return jnp.concatenate([xs[rows * h:rows * (h + 1)] for h in range(nh)], axis=1)


def _rms(x):
    return lax.rsqrt(jnp.mean(x * x, axis=-1, keepdims=True) + EPS)


def _stack_q(q, qw):
    qs = _stack_heads(q, HQ)
    r = _rms(qs)
    qhat = qs * r
    return qhat * qw, qhat, r


def _band_first(shape):
    return (lax.broadcasted_iota(jnp.int32, shape, 0) & (2 * BLK - 1)) == 0


def _stack_kv(kp, kc, vp, vc, kw):
    ks = _stack_heads(jnp.concatenate([kp, kc], axis=0), HKV)
    r = _rms(ks)
    khat = ks * r
    first = _band_first(ks.shape)
    kn = jnp.where(first, 0.0, khat * kw)
    v2 = jnp.where(first, 0.0, _stack_heads(jnp.concatenate([vp, vc], axis=0), HKV)).astype(BF16)
    return kn, khat, r, v2


def _softmax_rows(s):
    p = jnp.exp(s - jnp.max(s, axis=-1, keepdims=True))
    return p * (1.0 / _rsum(p))


def attn_fwd(proj, biasm, q_norm_w, k_norm_w):
    t = proj.shape[0]
    nb = t // BLK

    def body(q_ref, kc_ref, kp_ref, vc_ref, vp_ref, bm_ref, qw_ref, kw_ref, o_ref):
        f = lambda ref: ref[...].astype(F32)
        qn = _stack_q(f(q_ref), qw_ref[...])[0].astype(BF16)
        kn, _, _, v2 = _stack_kv(f(kp_ref), f(kc_ref), f(vp_ref), f(vc_ref), kw_ref[...])
        knb = kn.astype(BF16)
        s = jnp.concatenate([_dot_nt(qn[GQ * j:GQ * (j + 1)], knb[2 * BLK * j:2 * BLK * (j + 1)])
                             for j in range(HKV)], axis=0)
        pr = _softmax_rows(s * SCALE + bm_ref[0].reshape(HQ * BLK, 2 * BLK)).astype(BF16)
        o = jnp.concatenate([_dot(pr[GQ * j:GQ * (j + 1)], v2[2 * BLK * j:2 * BLK * (j + 1)])
                             for j in range(HKV)], axis=0)
        o_ref[...] = _unstack(o, HQ).astype(BF16)

    kblk, vblk = C_K // 256, C_V // 256
    prev = lambda n: jnp.maximum(n - 1, 0)
    return pl.pallas_call(
        body, name="attn_fwd", grid=(nb,),
        in_specs=[pl.BlockSpec((BLK, D), lambda n: (n, 0)),
                  pl.BlockSpec((BLK, 256), lambda n: (n, kblk)),
                  pl.BlockSpec((BLK, 256), lambda n: (prev(n), kblk)),
                  pl.BlockSpec((BLK, 256), lambda n: (n, vblk)),
                  pl.BlockSpec((BLK, 256), lambda n: (prev(n), vblk)),
                  pl.BlockSpec((1, HQ, BLK, 2 * BLK), lambda n: (jnp.minimum(n, 1), 0, 0, 0)),
                  _full((1, DH)), _full((1, DH))],
        out_specs=pl.BlockSpec((BLK, D), lambda n: (n, 0)),
        out_shape=jax.ShapeDtypeStruct((t, D), BF16),
        compiler_params=_cp(("parallel",)),
    )(proj, proj, proj, proj, proj, biasm, q_norm_w, k_norm_w)


def attn_bwd(proj, dao, biasm, q_norm_w, k_norm_w):
    t = proj.shape[0]
    nb = t // BLK
    kb = 2 * BLK

    def body(q_ref, kc_ref, kp_ref, vc_ref, vp_ref, do_ref, bm_ref, qw_ref, kw_ref,
             dq_ref, dk_ref, dv_ref, dqw_ref, dkw_ref, dacc_ref, ck, cv, pk, pv, nk, nv):
        n = pl.program_id(0)

        @pl.when(n == 0)
        def _():
            for ref in (dqw_ref, dkw_ref, dacc_ref, ck, cv):
                ref[...] = jnp.zeros_like(ref)

        qw = qw_ref[...]
        kw = kw_ref[...]
        f = lambda ref: ref[...].astype(F32)
        kn, khat, rk, v2 = _stack_kv(f(kp_ref), f(kc_ref), f(vp_ref), f(vc_ref), kw)
        grp = lambda a, j: a[GQ * j:GQ * (j + 1)]
        band = lambda a, j: a[kb * j:kb * (j + 1)]

        @pl.when(n < nb)
        def _():
            qn, qhat, rq = _stack_q(f(q_ref), qw)
            qnb = qn.astype(BF16)
            knb = kn.astype(BF16)
            dos = _stack_heads(f(do_ref), HQ).astype(BF16)
            s = jnp.concatenate([_dot_nt(grp(qnb, j), band(knb, j)) for j in range(HKV)], axis=0)
            pr = _softmax_rows(s * SCALE + bm_ref[0].reshape(HQ * BLK, kb))
            dp = jnp.concatenate([_dot_nt(grp(dos, j), band(v2, j)) for j in range(HKV)], axis=0)
            ds = pr * (dp - _rsum(pr * dp))
            dacc_ref[...] += ds.reshape(HQ, BLK, kb)
            dsb = ds.astype(BF16)
            prb = pr.astype(BF16)
            dqn = jnp.concatenate([_dot(grp(dsb, j), band(knb, j)) for j in range(HKV)], axis=0) * SCALE
            dqhat = dqn * qw
            dq = rq * (dqhat - qhat * jnp.mean(dqhat * qhat, axis=-1, keepdims=True))
            dq_ref[...] = _unstack(dq, HQ).astype(BF16)
            dqw_ref[...] += _csum(dqn * qhat)
            first = _band_first((kb, DH))
            for j in range(HKV):
                rows = slice(BLK * j, BLK * (j + 1))
                dkn = jnp.where(first, 0.0, _dot_tn(grp(dsb, j), grp(qnb, j)) * SCALE)
                dvj = jnp.where(first, 0.0, _dot_tn(grp(prb, j), grp(dos, j)))
                pk[rows, :] = dkn[:BLK]
                nk[rows, :] = dkn[BLK:]
                pv[rows, :] = dvj[:BLK]
                nv[rows, :] = dvj[BLK:]

        @pl.when(n == nb)
        def _():
            for ref in (pk, pv, nk, nv):
                ref[...] = jnp.zeros_like(ref)

        khp = jnp.concatenate([khat[kb * j:kb * j + BLK] for j in range(HKV)], axis=0)
        rkp = jnp.concatenate([rk[kb * j:kb * j + BLK] for j in range(HKV)], axis=0)
        dkn = ck[...] + pk[...]
        dkhat = dkn * kw
        dk = rkp * (dkhat - khp * jnp.mean(dkhat * khp, axis=-1, keepdims=True))
        dk_ref[...] = _unstack(dk, HKV).astype(BF16)
        dkw_ref[...] += _csum(dkn * khp)
        dv_ref[...] = _unstack(cv[...] + pv[...], HKV).astype(BF16)
        ck[...] = nk[...]
        cv[...] = nv[...]

    kblk, vblk = C_K // 256, C_V // 256
    cur = lambda n: jnp.minimum(n, nb - 1)
    prev = lambda n: jnp.maximum(n - 1, 0)
    carry = pltpu.VMEM((HKV * BLK, DH), F32)
    return pl.pallas_call(
        body, name="attn_bwd", grid=(nb + 1,),
        in_specs=[pl.BlockSpec((BLK, D), lambda n: (cur(n), 0)),
                  pl.BlockSpec((BLK, 256), lambda n: (cur(n), kblk)), pl.BlockSpec((BLK, 256), lambda n: (prev(n), kblk)),
                  pl.BlockSpec((BLK, 256), lambda n: (cur(n), vblk)), pl.BlockSpec((BLK, 256), lambda n: (prev(n), vblk)),
                  pl.BlockSpec((BLK, D), lambda n: (cur(n), 0)),
                  pl.BlockSpec((1, HQ, BLK, kb), lambda n: (jnp.minimum(n, 1), 0, 0, 0)),
                  _full((1, DH)), _full((1, DH))],
        out_specs=[pl.BlockSpec((BLK, D), lambda n: (cur(n), 0)),
                   pl.BlockSpec((BLK, 256), lambda n: (prev(n), 0)), pl.BlockSpec((BLK, 256), lambda n: (prev(n), 0)),
                   _full((1, DH)), _full((1, DH)), _full((HQ, BLK, kb))],
        out_shape=[jax.ShapeDtypeStruct((t, D), BF16), jax.ShapeDtypeStruct((t, 256), BF16),
                   jax.ShapeDtypeStruct((t, 256), BF16), jax.ShapeDtypeStruct((1, DH), F32),
                   jax.ShapeDtypeStruct((1, DH), F32), jax.ShapeDtypeStruct((HQ, BLK, kb), F32)],
        scratch_shapes=[carry] * 6,
        compiler_params=_cp(("arbitrary",)),
    )(proj, proj, proj, proj, proj, dao, biasm, q_norm_w, k_norm_w)


CONV_TM, CONV_CW, CONV_RC, HALO = 512, 512, 32, 16


def conv_fwd(proj, conv_w, conv_b):
    t = proj.shape[0]
    tm = min(t, CONV_TM)
    c0 = C_XBC // CONV_CW

    def body(x_ref, xp_ref, w_ref, b_ref, o_ref, ds_ref):
        i = pl.program_id(1)
        w = w_ref[...]
        b = b_ref[...]
        for r in range(tm // CONV_RC):
            lo = r * CONV_RC
            if r == 0:
                head = jnp.where(i == 0, 0.0, xp_ref[...].astype(F32))
                win = jnp.concatenate([head, x_ref[0:CONV_RC, :].astype(F32)], axis=0)
            else:
                win = x_ref[lo - HALO:lo + CONV_RC, :].astype(F32)
            acc = b
            for j in range(CONV_K):
                acc = acc + w[j:j + 1] * win[HALO - 3 + j:HALO - 3 + j + CONV_RC]
            sg = _sig(acc)
            o_ref[lo:lo + CONV_RC, :] = acc * sg
            ds_ref[lo:lo + CONV_RC, :] = _dsilu(acc, sg).astype(BF16)

    rh = tm // HALO
    tile = pl.BlockSpec((tm, CONV_CW), lambda s, i: (i, s))
    return pl.pallas_call(
        body, name="conv_fwd", grid=(XBC // CONV_CW, t // tm),
        in_specs=[pl.BlockSpec((tm, CONV_CW), lambda s, i: (i, c0 + s)),
                  pl.BlockSpec((HALO, CONV_CW), lambda s, i: (jnp.maximum(i * rh - 1, 0), c0 + s)),
                  pl.BlockSpec((CONV_K, CONV_CW), lambda s, i: (0, s)), pl.BlockSpec((1, CONV_CW), lambda s, i: (0, s))],
        out_specs=[tile, tile],
        out_shape=[jax.ShapeDtypeStruct((t, XBC), F32), jax.ShapeDtypeStruct((t, XBC), BF16)],
        compiler_params=_cp(("parallel", "parallel")),
    )(proj, proj, conv_w, conv_b)


def conv_bwd(proj, dact, dsl, conv_w):
    t = proj.shape[0]
    tm = min(t, CONV_TM)
    nt = t // tm
    nr = tm // CONV_RC
    c0 = C_XBC // CONV_CW
    ext = CONV_RC + 8

    def body(x_ref, xp_ref, d_ref, dn_ref, s_ref, sn_ref, w_ref, dx_ref, dw_ref, db_ref):
        i = pl.program_id(1)

        @pl.when(i == 0)
        def _():
            dw_ref[...] = jnp.zeros_like(dw_ref)
            db_ref[...] = jnp.zeros_like(db_ref)

        w = w_ref[...]
        dws = [jnp.zeros((1, CONV_CW), F32) for _ in range(CONV_K)]
        db = jnp.zeros((1, CONV_CW), F32)
        for r in range(nr):
            lo = r * CONV_RC
            if r == 0:
                head = jnp.where(i == 0, 0.0, xp_ref[...].astype(F32))
                win = jnp.concatenate([head, x_ref[0:CONV_RC, :].astype(F32)], axis=0)
            else:
                win = x_ref[lo - HALO:lo + CONV_RC, :].astype(F32)
            if r < nr - 1:
                dext = d_ref[lo:lo + ext, :]
                sext = s_ref[lo:lo + CONV_RC + HALO, :].astype(F32)[0:ext]
            else:
                dext = jnp.concatenate([d_ref[lo:lo + CONV_RC, :], jnp.where(i == nt - 1, 0.0, dn_ref[...])], axis=0)
                sext = jnp.concatenate([s_ref[lo:lo + CONV_RC, :].astype(F32), sn_ref[...].astype(F32)], axis=0)[0:ext]
            dpre = dext * sext
            dx = jnp.zeros((CONV_RC, CONV_CW), F32)
            own = dpre[0:CONV_RC]
            for j in range(CONV_K):
                dx = dx + w[j:j + 1] * dpre[3 - j:3 - j + CONV_RC]
                dws[j] = dws[j] + _csum(own * win[HALO - 3 + j:HALO - 3 + j + CONV_RC])
            db = db + _csum(own)
            dx_ref[lo:lo + CONV_RC, :] = dx.astype(BF16)
        dw_ref[...] += jnp.concatenate(dws, axis=0)
        db_ref[...] += db

    rh = tm // HALO
    r8 = tm // 8
    nxt = lambda i, per: jnp.minimum((i + 1) * per, nt * per - 1)
    return pl.pallas_call(
        body, name="conv_bwd", grid=(XBC // CONV_CW, nt),
        in_specs=[pl.BlockSpec((tm, CONV_CW), lambda s, i: (i, c0 + s)),
                  pl.BlockSpec((HALO, CONV_CW), lambda s, i: (jnp.maximum(i * rh - 1, 0), c0 + s)),
                  pl.BlockSpec((tm, CONV_CW), lambda s, i: (i, s)),
                  pl.BlockSpec((8, CONV_CW), lambda s, i: (nxt(i, r8), s)),
                  pl.BlockSpec((tm, CONV_CW), lambda s, i: (i, s)),
                  pl.BlockSpec((HALO, CONV_CW), lambda s, i: (nxt(i, rh), s)),
                  pl.BlockSpec((CONV_K, CONV_CW), lambda s, i: (0, s))],
        out_specs=[pl.BlockSpec((tm, CONV_CW), lambda s, i: (i, s)),
                   pl.BlockSpec((CONV_K, CONV_CW), lambda s, i: (0, s)), pl.BlockSpec((1, CONV_CW), lambda s, i: (0, s))],
        out_shape=[jax.ShapeDtypeStruct((t, XBC), BF16), jax.ShapeDtypeStruct((CONV_K, XBC), F32),
                   jax.ShapeDtypeStruct((1, XBC), F32)],
        compiler_params=_cp(("parallel", "arbitrary")),
    )(proj, proj, dact, dact, dsl, dsl, conv_w)


def _split3(x):
    h = x.astype(BF16)
    r = x - h.astype(F32)
    m = r.astype(BF16)
    lo = (r - m.astype(F32)).astype(BF16)
    return h, m, lo


def _tri_mm(tri, x):
    h, m, lo = _split3(x)
    return _dot(tri, h) + _dot(tri, m) + _dot(tri, lo)


def _softplus(x):
    return jnp.maximum(x, 0.0) + jnp.log1p(jnp.exp(-jnp.abs(x)))


def _chunk_decays(dt_raw, dtb, alog):
    dtv = _softplus(dt_raw + dtb)
    a = -jnp.exp(alog)
    ri = lax.broadcasted_iota(jnp.int32, (BLK, BLK), 0)
    ci = lax.broadcasted_iota(jnp.int32, (BLK, BLK), 1)
    causal = ri >= ci
    acum = _tri_mm(causal.astype(BF16), dtv * a)
    return dtv, a, causal, acum, acum.T


NPAIR = SH // 2


def _pairs(x):
    return jnp.stack([x[:, 128 * k:128 * (k + 1)] for k in range(NPAIR)])


def _unpairs(x3):
    return jnp.concatenate([x3[k] for k in range(NPAIR)], axis=1)


def _per_head_cols(m):
    return jnp.stack([jnp.broadcast_to(m[:, h:h + 1], m.shape) for h in range(SH)])


def _pair_lanes(t):
    r = t.reshape(NPAIR, 2, t.shape[1], 128)
    lo = lax.broadcasted_iota(jnp.int32, (1, t.shape[1], 128), 2) < SP
    return jnp.where(lo, r[:, 0], r[:, 1])


class _Chunk:
    pass


def _chunk_common(dt_raw, dtb, alog, dskip):
    cm = _Chunk()
    cm.dtv, cm.a, cm.causal, acum, acum_t = _chunk_decays(dt_raw, dtb, alog)
    cm.acol = _per_head_cols(acum)
    cm.arow = jnp.stack([acum_t[h:h + 1, :] for h in range(SH)])
    cm.lam = jnp.exp(jnp.where(cm.causal[None], cm.acol - cm.arow, NEG))
    apl = _pair_lanes(cm.acol)
    alast = apl[:, BLK - 1:BLK, :]
    cm.dpl = _pair_lanes(_per_head_cols(cm.dtv))
    cm.eapl = jnp.exp(apl)
    cm.epl = jnp.exp(alast - apl)
    cm.cdpl = jnp.exp(alast)
    cm.dskpl = _pair_lanes(_per_head_cols(dskip))
    cm.lo = lax.broadcasted_iota(jnp.int32, (1, BLK, 128), 2) < SP
    return cm


def ssd_fwd(act, dt_raw, dtb_p, alog_p, dsk_p):
    t = act.shape[0]
    nc = t // BLK

    def body(xs_ref, b_ref, c_ref, dt_ref, dtb_ref, al_ref, dk_ref, y_ref, sp_ref, st):
        c = pl.program_id(0)

        @pl.when(c == 0)
        def _():
            st[...] = jnp.zeros_like(st)

        s_t = st[...]
        sp_ref[0] = s_t
        cm = _chunk_common(dt_ref[...], dtb_ref[...], al_ref[...], dk_ref[...])
        gms, cbs, bts = [], [], []
        for g in range(SG):
            bf = b_ref[:, SN * g:SN * (g + 1)]
            cb = c_ref[:, SN * g:SN * (g + 1)].astype(BF16)
            gms.append(_dot_nt(cb, bf.astype(BF16)))
            cbs.append(cb)
            bts.append(bf.T.astype(BF16))
        m = (cm.lam.reshape(SG, SR, BLK, BLK) * jnp.stack(gms)[:, None]).reshape(SH, BLK, BLK).astype(BF16)
        xs16 = _pairs(xs_ref[...])
        xdt16 = xs16 * cm.dpl
        x_lo = jnp.where(cm.lo, xdt16, 0.0).astype(BF16)
        x_hi = jnp.where(cm.lo, 0.0, xdt16).astype(BF16)
        s16 = _pairs(s_t)
        s16b = s16.astype(BF16)
        yd = jnp.stack([_dot(m[2 * k], x_lo[k]) + _dot(m[2 * k + 1], x_hi[k]) for k in range(NPAIR)])
        yo = jnp.stack([_dot(cbs[k // (NPAIR // SG)], s16b[k]) for k in range(NPAIR)])
        y_ref[...] = _unpairs(yd + yo * cm.eapl + cm.dskpl * xs16).astype(BF16)
        xe = (xdt16 * cm.epl).astype(BF16)
        st[...] = _unpairs(cm.cdpl * s16 + jnp.stack([_dot(bts[k // (NPAIR // SG)], xe[k]) for k in range(NPAIR)]))

    vec = _full((1, 128))
    return pl.pallas_call(
        body, name="ssd_fwd", grid=(nc,),
        in_specs=[pl.BlockSpec((BLK, SSM_W), lambda c: (c, 0)),
                  pl.BlockSpec((BLK, SG * SN), lambda c: (c, SSM_W // (SG * SN))),
                  pl.BlockSpec((BLK, SG * SN), lambda c: (c, SSM_W // (SG * SN) + 1)),
                  pl.BlockSpec((BLK, 128), lambda c: (c, 0)), vec, vec, vec],
        out_specs=[pl.BlockSpec((BLK, SSM_W), lambda c: (c, 0)), pl.BlockSpec((1, SN, SSM_W), lambda c: (c, 0, 0))],
        out_shape=[jax.ShapeDtypeStruct((t, SSM_W), BF16), jax.ShapeDtypeStruct((nc, SN, SSM_W), F32)],
        scratch_shapes=[pltpu.VMEM((SN, SSM_W), F32)],
        compiler_params=_cp(("arbitrary",)),
    )(act, act, act, dt_raw, dtb_p, alog_p, dsk_p)


def _head_sums(q):
    r = q.shape[1]
    lo = lax.broadcasted_iota(jnp.int32, (1, r, 128), 2) < SP
    s_lo = jnp.sum(jnp.where(lo, q, 0.0), axis=-1, keepdims=True)
    s_hi = jnp.sum(jnp.where(lo, 0.0, q), axis=-1, keepdims=True)
    lane = lax.broadcasted_iota(jnp.int32, (r, 128), 1)
    out = jnp.zeros((r, 128), F32)
    for k in range(NPAIR):
        out = jnp.where(lane == 2 * k, s_lo[k], jnp.where(lane == 2 * k + 1, s_hi[k], out))
    return out


def ssd_bwd(act, dt_raw, dy, sprev, dtb_p, alog_p, dsk_p):
    t = act.shape[0]
    nc = t // BLK

    def body(xs_ref, b_ref, c_ref, dt_ref, dy_ref, sp_ref, dtb_ref, al_ref, dk_ref,
             da_ref, ddt_ref, ddtb_ref, dal_ref, ddk_ref, dst):
        i = pl.program_id(0)

        @pl.when(i == 0)
        def _():
            dst[...] = jnp.zeros_like(dst)
            ddtb_ref[...] = jnp.zeros_like(ddtb_ref)
            dal_ref[...] = jnp.zeros_like(dal_ref)
            ddk_ref[...] = jnp.zeros_like(ddk_ref)

        dt_raw = dt_ref[...]
        dtb = dtb_ref[...]
        cm = _chunk_common(dt_raw, dtb, al_ref[...], dk_ref[...])
        ri = lax.broadcasted_iota(jnp.int32, (BLK, BLK), 0)
        ci = lax.broadcasted_iota(jnp.int32, (BLK, BLK), 1)
        lam_t = jnp.exp(jnp.where((ri <= ci)[None], cm.arow - cm.acol, NEG))
        bbs, cbs, cts, gms = [], [], [], []
        for g in range(SG):
            bf = b_ref[:, SN * g:SN * (g + 1)]
            cf = c_ref[:, SN * g:SN * (g + 1)]
            bbs.append(bf.astype(BF16))
            cbs.append(cf.astype(BF16))
            cts.append(cf.T.astype(BF16))
            gms.append(_dot_nt(bbs[g], cbs[g]))
        grp = lambda k: k // (NPAIR // SG)
        xs16 = _pairs(xs_ref[...])
        dy16 = _pairs(dy_ref[...].astype(F32))
        sp16 = _pairs(sp_ref[0])
        ds16 = _pairs(dst[...])
        xdt16 = xs16 * cm.dpl
        xdtb = xdt16.astype(BF16)
        dyh = [jnp.where(cm.lo, dy16, 0.0).astype(BF16), jnp.where(cm.lo, 0.0, dy16).astype(BF16)]
        m_t = (lam_t.reshape(SG, SR, BLK, BLK) * jnp.stack(gms)[:, None]).reshape(SH, BLK, BLK).astype(BF16)
        dxdt = jnp.stack([_dot(m_t[2 * k], dyh[0][k]) + _dot(m_t[2 * k + 1], dyh[1][k]) for k in range(NPAIR)])
        dm = jnp.stack([_dot_nt(dyh[h % 2][h // 2], xdtb[h // 2]) for h in range(SH)])
        dgl = (dm * cm.lam).reshape(SG, SR, BLK, BLK)
        dg = jnp.sum(dgl, axis=1).astype(BF16)
        w = (dgl * jnp.stack([_dot_nt(cbs[g], bbs[g]) for g in range(SG)])[:, None]).reshape(SH, BLK, BLK)
        w_rows = jnp.sum(w, axis=2, keepdims=True)
        w_cols = jnp.concatenate([jnp.sum(w, axis=1)] + [jnp.zeros((128 - SH, BLK), F32)], axis=0).T
        lane_c = lax.broadcasted_iota(jnp.int32, (BLK, 128), 1)
        da_cols = -w_cols
        for h in range(SH):
            da_cols = jnp.where(lane_c == h, da_cols + w_rows[h], da_cols)
        ds16b = ds16.astype(BF16)
        sp16b = sp16.astype(BF16)
        dxs = jnp.stack([_dot(bbs[grp(k)], ds16b[k]) for k in range(NPAIR)]) * cm.epl
        dxdt = dxdt + dxs
        dya = (dy16 * cm.eapl).astype(BF16)
        xe = (xdt16 * cm.epl).astype(BF16)
        dcs, dbs = [], []
        for g in range(SG):
            ks = range(g * (NPAIR // SG), (g + 1) * (NPAIR // SG))
            dcs.append(sum(_dot_nt(dya[k], sp16b[k]) for k in ks) + _dot(dg[g], bbs[g]))
            dbs.append(sum(_dot_nt(xe[k], ds16b[k]) for k in ks) + _dot_tn(dg[g], cbs[g]))
        dst[...] = _unpairs(cm.cdpl * ds16 + jnp.stack([_dot(cts[grp(k)], dya[k]) for k in range(NPAIR)]))
        da_ref[...] = jnp.concatenate([_unpairs(dxdt * cm.dpl + cm.dskpl * dy16)] + dbs + dcs, axis=1)
        y_off = jnp.stack([_dot(cbs[grp(k)], sp16b[k]) for k in range(NPAIR)]) * cm.eapl
        da_cols = da_cols + _head_sums(dy16 * y_off - xdt16 * dxs)
        last = _head_sums(jnp.sum(xdt16 * dxs, axis=1, keepdims=True)
                          + cm.cdpl * jnp.sum(ds16 * sp16, axis=1, keepdims=True))
        ddt = _head_sums(dxdt * xs16)
        row_i = lax.broadcasted_iota(jnp.int32, (BLK, 128), 0)
        dacum = da_cols + jnp.where(row_i == BLK - 1, last, 0.0)
        dda = _tri_mm((ri <= ci).astype(BF16), dacum)
        ddt = ddt + dda * cm.a
        dal_ref[...] += _csum(dda * cm.dtv) * cm.a
        ddt_raw = jnp.where(lane_c < SH, ddt * _sig(dt_raw + dtb), 0.0)
        ddt_ref[...] = ddt_raw.astype(BF16)
        ddtb_ref[...] += _csum(ddt_raw)
        ddk_ref[...] += _head_sums(jnp.sum(dy16 * xs16, axis=1, keepdims=True))

    rev = lambda i: nc - 1 - i
    vec = _full((1, 128))
    slab = pl.BlockSpec((BLK, SSM_W), lambda i: (rev(i), 0))
    return pl.pallas_call(
        body, name="ssd_bwd", grid=(nc,),
        in_specs=[slab,
                  pl.BlockSpec((BLK, SG * SN), lambda i: (rev(i), SSM_W // (SG * SN))),
                  pl.BlockSpec((BLK, SG * SN), lambda i: (rev(i), SSM_W // (SG * SN) + 1)),
                  pl.BlockSpec((BLK, 128), lambda i: (rev(i), 0)),
                  slab,
                  pl.BlockSpec((1, SN, SSM_W), lambda i: (rev(i), 0, 0)), vec, vec, vec],
        out_specs=[pl.BlockSpec((BLK, XBC), lambda i: (rev(i), 0)), pl.BlockSpec((BLK, 128), lambda i: (rev(i), 0)),
                   vec, vec, vec],
        out_shape=[jax.ShapeDtypeStruct((t, XBC), F32), jax.ShapeDtypeStruct((t, 128), BF16),
                   jax.ShapeDtypeStruct((1, 128), F32), jax.ShapeDtypeStruct((1, 128), F32),
                   jax.ShapeDtypeStruct((1, 128), F32)],
        scratch_shapes=[pltpu.VMEM((SN, SSM_W), F32)],
        compiler_params=_cp(("arbitrary",)),
    )(act, act, act, dt_raw, dy, sprev, dtb_p, alog_p, dsk_p)


TAIL_TM = 256


def _dsilu(z, s):
    return s * (1.0 + z * (1.0 - s))


def tail(proj, ao, yss, x, target, gate, ssm_nw, w_at, w_ss, w_ou):
    t = x.shape[0]
    tm = min(t, TAIL_TM)
    gw = SSM_W // SG

    def body(ao_ref, za_ref, ga_ref, gb_ref, zm_ref, ys_ref, x_ref, tg_ref, gt_ref, nw_ref, wa_ref, ws_ref, wo_ref,
             loss_ref, dy_ref, dao_ref, dza_ref, dga_ref, dgb_ref, dys_ref, dzm_ref,
             ua_ref, yn_ref, mg_ref, dya_ref, dyb_ref, do_ref, dgt_ref, dnw_ref):
        i = pl.program_id(0)

        @pl.when(i == 0)
        def _():
            loss_ref[...] = jnp.zeros_like(loss_ref)
            dgt_ref[...] = jnp.zeros_like(dgt_ref)
            dnw_ref[...] = jnp.zeros_like(dnw_ref)

        ao = ao_ref[...].astype(F32)
        za = za_ref[...].astype(F32)
        sa = _sig(za)
        sila = za * sa
        ua_f = ao * sila
        ua = ua_f.astype(BF16)
        ya = _dot(ua, wa_ref[...])
        zm = zm_ref[...].astype(F32)
        sm = _sig(zm)
        silm = zm * sm
        ys = ys_ref[...].astype(F32)
        u = ys * silm
        nw = nw_ref[...]
        rs, uns = [], []
        for g in range(SG):
            ug = u[:, gw * g:gw * (g + 1)]
            r = lax.rsqrt(jnp.mean(ug * ug, axis=-1, keepdims=True) + EPS)
            rs.append(r)
            uns.append(ug * r)
        un = jnp.concatenate(uns, axis=1)
        yn_f = un * nw
        yn = yn_f.astype(BF16)
        yb = _dot(yn, ws_ref[...])
        sga = _sig(ga_ref[...].astype(F32))
        sgb = _sig(gb_ref[...].astype(F32))
        mg_f = sga * ya + sgb * yb
        mg = mg_f.astype(BF16)
        o = _dot(mg, wo_ref[...])
        gt = gt_ref[...]
        err = (x_ref[...] + gt * o) - tg_ref[...]
        lane = lax.broadcasted_iota(jnp.int32, (1, 128), 1)
        loss_ref[...] += jnp.where(lane == 0, 0.5 * _asum(_rsum(err * err) / D), 0.0)
        dy = err * (1.0 / D)
        dy_ref[...] = dy
        dgt_ref[...] += _csum(dy * o)
        do = (dy * gt).astype(BF16)
        dmg = _dot_nt(do, wo_ref[...])
        dga_ref[...] = (dmg * ya * sga * (1.0 - sga)).astype(BF16)
        dgb_ref[...] = (dmg * yb * sgb * (1.0 - sgb)).astype(BF16)
        dya = (dmg * sga).astype(BF16)
        dyb = (dmg * sgb).astype(BF16)
        dua = _dot_nt(dya, wa_ref[...])
        dao_ref[...] = (dua * sila).astype(BF16)
        dza_ref[...] = (dua * ao * _dsilu(za, sa)).astype(BF16)
        dyn = _dot_nt(dyb, ws_ref[...])
        dnw_ref[...] += _csum(dyn * un)
        dun = dyn * nw
        dus = []
        for g in range(SG):
            gs = slice(gw * g, gw * (g + 1))
            dus.append(rs[g] * (dun[:, gs] - uns[g] * jnp.mean(dun[:, gs] * uns[g], axis=-1, keepdims=True)))
        du = jnp.concatenate(dus, axis=1)
        dys_ref[...] = (du * silm).astype(BF16)
        dzm_ref[...] = (du * ys * _dsilu(zm, sm)).astype(BF16)
        ua_ref[...] = ua_f.T.astype(BF16)
        yn_ref[...] = yn_f.T.astype(BF16)
        mg_ref[...] = mg_f.T.astype(BF16)
        dya_ref[...] = dya
        dyb_ref[...] = dyb
        do_ref[...] = do

    row = lambda w: pl.BlockSpec((tm, w), lambda i: (i, 0))
    pcol = lambda w, c0: pl.BlockSpec((tm, w), lambda i: (i, c0 // w))
    sd = lambda w, dt: jax.ShapeDtypeStruct((t, w), dt)
    colt = lambda w: pl.BlockSpec((w, tm), lambda i: (0, i))
    sdt = lambda w: jax.ShapeDtypeStruct((w, t), BF16)
    return pl.pallas_call(
        body, name="tail", grid=(t // tm,),
        in_specs=[row(D), pcol(D, C_ZA), pcol(D, C_GA), pcol(D, C_GB), pcol(SSM_W, C_ZM), row(SSM_W), row(D), row(D),
                  _full((1, D)), _full((1, SSM_W)), _full((D, D)), _full((SSM_W, D)), _full((D, D))],
        out_specs=[_full((1, 128)), row(D), row(D), row(D), row(D), row(D), row(SSM_W), row(SSM_W),
                   colt(D), colt(SSM_W), colt(D), row(D), row(D), row(D), _full((1, D)), _full((1, SSM_W))],
        out_shape=[jax.ShapeDtypeStruct((1, 128), F32), sd(D, F32), sd(D, BF16), sd(D, BF16), sd(D, BF16), sd(D, BF16),
                   sd(SSM_W, BF16), sd(SSM_W, BF16), sdt(D), sdt(SSM_W), sdt(D), sd(D, BF16),
                   sd(D, BF16), sd(D, BF16), jax.ShapeDtypeStruct((1, D), F32), jax.ShapeDtypeStruct((1, SSM_W), F32)],
        compiler_params=_cp(("arbitrary",)),
    )(ao, proj, proj, proj, proj, yss, x, target, gate, ssm_nw, w_at, w_ss, w_ou)


def dproj_bwd(dproj, wcat, x, dy, norm_w, scale):
    t = x.shape[0]
    tm = min(t, 512)
    tk = NP // 3
    nk = NP // tk
    nt = t // tm

    def body(dp_ref, w_ref, x_ref, dy_ref, nw_ref, sc_ref, gx_ref, dnw_ref, dsc_ref, dsh_ref, acc, dwe_ref):
        i = pl.program_id(0)
        k = pl.program_id(1)

        @pl.when(jnp.logical_and(i == 0, k == 0))
        def _():
            dwe_ref[...] = jnp.zeros_like(dwe_ref)
            dsh_ref[...] = jnp.zeros_like(dsh_ref)
            dnw_ref[...] = jnp.zeros_like(dnw_ref)
            dsc_ref[...] = jnp.zeros_like(dsc_ref)

        part = _dot_nt(dp_ref[...], w_ref[...])

        @pl.when(k == 0)
        def _():
            acc[...] = part

        @pl.when(k > 0)
        def _():
            acc[...] += part

        @pl.when(k == nk - 1)
        def _():
            dh = acc[...]
            xv = x_ref[...]
            r = lax.rsqrt(jnp.mean(xv * xv, axis=-1, keepdims=True) + EPS)
            xn = xv * r
            weff = nw_ref[...] * (1.0 + sc_ref[...])
            dxn = dh * weff
            gx_ref[...] = dy_ref[...] + r * (dxn - xn * jnp.mean(dxn * xn, axis=-1, keepdims=True))
            dwe_ref[...] += _csum(dh * xn)
            dsh_ref[...] += _csum(dh)

        @pl.when(jnp.logical_and(i == nt - 1, k == nk - 1))
        def _():
            dwe = dwe_ref[...]
            dnw_ref[...] = dwe * (1.0 + sc_ref[...])
            dsc_ref[...] = dwe * nw_ref[...]

    vec = pl.BlockSpec((1, D), lambda i, k: (0, 0))
    row = pl.BlockSpec((tm, D), lambda i, k: (i, 0))
    return pl.pallas_call(
        body, name="dproj_bwd", grid=(nt, nk),
        in_specs=[pl.BlockSpec((tm, tk), lambda i, k: (i, k)), pl.BlockSpec((D, tk), lambda i, k: (0, k)),
                  row, row, vec, vec],
        out_specs=[row, vec, vec, vec],
        out_shape=[jax.ShapeDtypeStruct((t, D), F32), jax.ShapeDtypeStruct((1, D), F32),
                   jax.ShapeDtypeStruct((1, D), F32), jax.ShapeDtypeStruct((1, D), F32)],
        scratch_shapes=[pltpu.VMEM((tm, D), F32), pltpu.VMEM((1, D), F32)],
        compiler_params=_cp(("arbitrary", "arbitrary")),
    )(dproj, wcat, x, dy, norm_w, scale)


def wgrad(at, b, name, bn):
    m, t = at.shape
    n = b.shape[1]
    tk = min(t, 1024)
    bm = min(m, 1024)

    def body(a_ref, b_ref, o_ref):
        part = _dot(a_ref[...], b_ref[...])

        @pl.when(pl.program_id(2) == 0)
        def _():
            o_ref[...] = part

        @pl.when(pl.program_id(2) > 0)
        def _():
            o_ref[...] += part

    return pl.pallas_call(
        body, name=name, grid=(m // bm, n // bn, t // tk),
        in_specs=[pl.BlockSpec((bm, tk), lambda i, j, k: (i, k)), pl.BlockSpec((tk, bn), lambda i, j, k: (k, j))],
        out_specs=pl.BlockSpec((bm, bn), lambda i, j, k: (i, j)),
        out_shape=jax.ShapeDtypeStruct((m, n), F32),
        compiler_params=_cp(("parallel", "parallel", "arbitrary")),
    )(at, b)


SUM_TR = 256


def pair_sum(g, core, theirs, name):
    w = g.shape[2]
    nh = HROWS // SUM_TR

    def body(core_ref, a_ref, b_ref, o_ref, ob_ref):
        s = a_ref[...] + b_ref[...]
        o_ref[...] = s
        ob_ref[...] = s.astype(BF16)

    spec = pl.BlockSpec((1, SUM_TR, w), lambda d, i, c: (d, i, 0))
    return pl.pallas_call(
        body, name=name,
        out_shape=[jax.ShapeDtypeStruct((4, HROWS, w), F32), jax.ShapeDtypeStruct((4, HROWS, w), BF16)],
        grid_spec=pltpu.PrefetchScalarGridSpec(
            num_scalar_prefetch=1, grid=(4, nh),
            in_specs=[pl.BlockSpec((1, SUM_TR, w), lambda d, i, c: (d, c[0] * nh + i, 0)), spec],
            out_specs=[spec, spec]),
        compiler_params=_cp(("parallel", "parallel")))(core.reshape(1).astype(jnp.int32), g, theirs)


def chip_sum(part, chip, others, name):
    r, w = part.shape[1:]

    def body(chip_ref, a_ref, b_ref, o_ref):
        acc = a_ref[0]
        for k in range(3):
            acc = acc + b_ref[k].astype(F32)
        o_ref[...] = acc

    return pl.pallas_call(
        body, name=name, out_shape=jax.ShapeDtypeStruct((r, w), F32),
        grid_spec=pltpu.PrefetchScalarGridSpec(
            num_scalar_prefetch=1, grid=(r // SUM_TR,),
            in_specs=[pl.BlockSpec((1, SUM_TR, w), lambda i, c: (c[0], i, 0)),
                      pl.BlockSpec((3, SUM_TR, w), lambda i, c: (0, i, 0))],
            out_specs=pl.BlockSpec((SUM_TR, w), lambda i, c: (i, 0))),
        compiler_params=_cp(("parallel",)))(chip.reshape(1).astype(jnp.int32), part, others)


def sum_devices(g):
    r = g.shape[1]

    def body(g_ref, o_ref):
        acc = g_ref[0]
        for d in range(1, 8):
            acc = acc + g_ref[d]
        o_ref[...] = acc

    return pl.pallas_call(body, name="sum_devices", out_shape=jax.ShapeDtypeStruct((r, 1024), F32),
                          compiler_params=_cp())(g)


def adamw(w, g, m, v, name):
    r, c = w.shape
    tr = r
    for cand in (256, 128, 64, 32, 16, 8):
        if r % cand == 0 and r > cand:
            tr = cand
            break

    def body(w_ref, g_ref, m_ref, v_ref, d_ref, nm_ref, nv_ref):
        gv = g_ref[...]
        mn = ADAM_B1 * m_ref[...] + (1.0 - ADAM_B1) * gv
        vn = ADAM_B2 * v_ref[...] + (1.0 - ADAM_B2) * (gv * gv)
        m_hat = mn / (1.0 - ADAM_B1 ** ADAM_STEP)
        v_hat = vn / (1.0 - ADAM_B2 ** ADAM_STEP)
        d_ref[...] = -ADAM_LR * (m_hat / (jnp.sqrt(v_hat) + ADAM_EPS) + ADAM_WD * w_ref[...])
        nm_ref[...] = mn
        nv_ref[...] = vn

    spec = pl.BlockSpec((tr, c), lambda i: (i, 0))
    sd = jax.ShapeDtypeStruct((r, c), F32)
    return pl.pallas_call(body, name=name, grid=(r // tr,), in_specs=[spec] * 4, out_specs=[spec] * 3,
                          out_shape=[sd, sd, sd], compiler_params=_cp(("parallel",)))(w, g, m, v)


ANY = pl.BlockSpec(memory_space=pl.ANY)
VM = pl.BlockSpec(memory_space=pltpu.VMEM)
OTHER_CHIPS = ((1, 0), (0, 1), (1, 1))


def _pos():
    return lax.axis_index("x"), lax.axis_index("y"), lax.axis_index("c")


def _flip(v, bit):
    return 1 - v if bit else v


def _rcopy(src, dst, ssem, rsem, peer):
    return pltpu.make_async_remote_copy(src_ref=src, dst_ref=dst, send_sem=ssem, recv_sem=rsem,
                                        device_id=peer, device_id_type=MESH)


def allgather_small(p, name):
    r = p.shape[0]

    def body(in_ref, out_ref, ssem, rsem, lsem):
        x, y, c = _pos()
        me = 4 * x + 2 * y + c
        loc = pltpu.make_async_copy(in_ref, out_ref.at[me], lsem)
        loc.start()
        sends = []
        peers = []
        for k in range(1, 8):
            px, py, pc = _flip(x, (k >> 2) & 1), _flip(y, (k >> 1) & 1), _flip(c, k & 1)
            peers.append((px, py, pc))
            cp = _rcopy(in_ref, out_ref.at[me], ssem.at[k - 1], rsem.at[k - 1], (px, py, pc))
            cp.start()
            sends.append(cp)
        for k in range(1, 8):
            px, py, pc = peers[k - 1]
            _rcopy(in_ref, out_ref.at[4 * px + 2 * py + pc], ssem.at[k - 1], rsem.at[k - 1], (px, py, pc)).wait_recv()
        for cp in sends:
            cp.wait_send()
        loc.wait()

    return pl.pallas_call(
        body, name=name, out_shape=jax.ShapeDtypeStruct((8, r, 1024), F32),
        in_specs=[VM], out_specs=VM,
        scratch_shapes=[pltpu.SemaphoreType.DMA((7,)), pltpu.SemaphoreType.DMA((7,)), pltpu.SemaphoreType.DMA],
    )(p)


def gather_weights(w_in_b, w_rest_b, mod_sh):
    def body(wi_ref, wr_ref, m_ref, gi_ref, gr_ref, mo_ref, ssem, rsem, lsem):
        x, y, c = _pos()
        chip = 2 * x + y
        mine = pl.ds(pl.multiple_of(c * HROWS, 16), HROWS)
        other = pl.ds(pl.multiple_of((1 - c) * HROWS, 16), HROWS)
        sib = (x, y, 1 - c)
        pairs = ((wi_ref, gi_ref), (wr_ref, gr_ref))
        loc_m = pltpu.make_async_copy(m_ref, mo_ref.at[chip], lsem)
        loc_m.start()
        sends = []
        for k, (fx, fy) in enumerate(OTHER_CHIPS):
            peer = (_flip(x, fx), _flip(y, fy), c)
            for a, (w_ref, g_ref) in enumerate(pairs):
                cw = _rcopy(w_ref.at[mine], g_ref.at[chip, mine], ssem.at[6 * a + k], rsem.at[6 * a + k], peer)
                cw.start()
                sends.append(cw)
            cm = _rcopy(m_ref, mo_ref.at[chip], ssem.at[12 + k], rsem.at[12 + k], peer)
            cm.start()
            sends.append(cm)
        for k, (fx, fy) in enumerate(OTHER_CHIPS):
            px, py = _flip(x, fx), _flip(y, fy)
            for a, (w_ref, g_ref) in enumerate(pairs):
                got = g_ref.at[2 * px + py, mine]
                _rcopy(w_ref.at[mine], got, ssem.at[6 * a + k], rsem.at[6 * a + k], (px, py, c)).wait_recv()
                fw = _rcopy(got, got, ssem.at[6 * a + 3 + k], rsem.at[6 * a + 3 + k], sib)
                fw.start()
                sends.append(fw)
        for k, (fx, fy) in enumerate(OTHER_CHIPS):
            px, py = _flip(x, fx), _flip(y, fy)
            for a, (w_ref, g_ref) in enumerate(pairs):
                land = g_ref.at[2 * px + py, other]
                _rcopy(land, land, ssem.at[6 * a + 3 + k], rsem.at[6 * a + 3 + k], sib).wait_recv()
            _rcopy(m_ref, mo_ref.at[2 * px + py], ssem.at[12 + k], rsem.at[12 + k], (px, py, c)).wait_recv()
        for cp in sends:
            cp.wait_send()
        loc_m.wait()

    return pl.pallas_call(
        body, name="gather_weights",
        out_shape=[jax.ShapeDtypeStruct((4, D, SH_IN), BF16), jax.ShapeDtypeStruct((4, D, D), BF16),
                   jax.ShapeDtypeStruct((4, 8, 768), F32)],
        in_specs=[ANY, ANY, VM], out_specs=[ANY, ANY, VM],
        scratch_shapes=[pltpu.SemaphoreType.DMA((15,)), pltpu.SemaphoreType.DMA((15,)), pltpu.SemaphoreType.DMA],
    )(w_in_b, w_rest_b, mod_sh)


def pair_exchange(g_in, g_rest):
    def body(gi_ref, gr_ref, ri_ref, rr_ref, ssem, rsem):
        x, y, c = _pos()
        other = pl.ds(pl.multiple_of((1 - c) * HROWS, 8), HROWS)
        cps = [_rcopy(g_ref.at[:, other, :], r_ref, ssem.at[a], rsem.at[a], (x, y, 1 - c))
               for a, (g_ref, r_ref) in enumerate(((gi_ref, ri_ref), (gr_ref, rr_ref)))]
        for cp in cps:
            cp.start()
        for cp in cps:
            cp.wait()

    return pl.pallas_call(
        body, name="pair_exchange",
        out_shape=[jax.ShapeDtypeStruct((4, HROWS, SH_IN), F32), jax.ShapeDtypeStruct((4, HROWS, D), F32)],
        in_specs=[ANY, ANY], out_specs=[ANY, ANY],
        scratch_shapes=[pltpu.SemaphoreType.DMA((2,)), pltpu.SemaphoreType.DMA((2,))],
    )(g_in, g_rest)


HBM = pl.BlockSpec(memory_space=pltpu.HBM)
SEM = pl.BlockSpec(memory_space=pltpu.SEMAPHORE)
DATAFLOW = pltpu.SideEffectType.DATAFLOW_SIDE_EFFECTING


def _chip_copies(pi_ref, pr_ref, li_ref, lr_ref, ssem, rsem):
    x, y, c = _pos()
    copies = []
    for k, (fx, fy) in enumerate(OTHER_CHIPS):
        px, py = _flip(x, fx), _flip(y, fy)
        for a, (p_ref, l_ref) in enumerate(((pi_ref, li_ref), (pr_ref, lr_ref))):
            copies.append(_rcopy(p_ref.at[2 * px + py], l_ref.at[k], ssem.at[3 * a + k], rsem.at[3 * a + k], (px, py, c)))
    return copies


def chip_exchange_start(pb_in, pb_rest):
    def body(pi_ref, pr_ref, li_ref, lr_ref, ssem, rsem, pi_thru, pr_thru, li_thru, lr_thru, token):
        for cp in _chip_copies(pi_ref, pr_ref, li_ref, lr_ref, ssem, rsem):
            cp.start()
        token[...] = jnp.zeros_like(token)

    land_in = lax.empty((3, HROWS, SH_IN), BF16)
    land_rest = lax.empty((3, HROWS, D), BF16)
    hbm = lambda a: pltpu.HBM(a.shape, a.dtype)
    return pl.pallas_call(
        body, name="chip_exchange_start",
        out_shape=(pltpu.SemaphoreType.DMA((6,)), pltpu.SemaphoreType.DMA((6,)), hbm(pb_in), hbm(pb_rest),
                   hbm(land_in), hbm(land_rest), jax.ShapeDtypeStruct((8, 128), F32)),
        in_specs=(HBM, HBM, HBM, HBM), out_specs=(SEM, SEM, HBM, HBM, HBM, HBM, VM),
        input_output_aliases={0: 2, 1: 3, 2: 4, 3: 5},
        compiler_params=pltpu.CompilerParams(has_side_effects=DATAFLOW),
    )(*[pltpu.with_memory_space_constraint(a, pltpu.HBM) for a in (pb_in, pb_rest, land_in, land_rest)])


def chip_exchange_wait(ssem, rsem, pb_in, pb_rest, land_in, land_rest, after):
    def body(pi_ref, pr_ref, li_ref, lr_ref, ssem, rsem, after_ref, pi_dead, pr_dead, li_out, lr_out):
        for cp in _chip_copies(pi_ref, pr_ref, li_ref, lr_ref, ssem, rsem):
            cp.wait_send()
            cp.wait_recv()

    hbm = lambda a: pltpu.HBM(a.shape, a.dtype)
    return pl.pallas_call(
        body, name="chip_exchange_wait", out_shape=(hbm(pb_in), hbm(pb_rest), hbm(land_in), hbm(land_rest)),
        in_specs=(HBM, HBM, HBM, HBM, SEM, SEM, ANY), out_specs=(HBM, HBM, HBM, HBM),
        input_output_aliases={0: 0, 1: 1, 2: 2, 3: 3},
        compiler_params=pltpu.CompilerParams(has_side_effects=DATAFLOW),
    )(pb_in, pb_rest, land_in, land_rest, ssem, rsem, after)[2:]


def pair_swap(red_in, red_rest):
    def body(ai_ref, ar_ref, oi_ref, or_ref, ssem, rsem):
        x, y, c = _pos()
        cps = [_rcopy(a_ref, o_ref, ssem.at[a], rsem.at[a], (x, y, 1 - c))
               for a, (a_ref, o_ref) in enumerate(((ai_ref, oi_ref), (ar_ref, or_ref)))]
        for cp in cps:
            cp.start()
        for cp in cps:
            cp.wait()

    return pl.pallas_call(
        body, name="pair_swap",
        out_shape=[jax.ShapeDtypeStruct((HROWS, SH_IN), F32), jax.ShapeDtypeStruct((HROWS, D), F32)],
        in_specs=[ANY, ANY], out_specs=[ANY, ANY],
        scratch_shapes=[pltpu.SemaphoreType.DMA((2,)), pltpu.SemaphoreType.DMA((2,))],
    )(red_in, red_rest)


def _flat(v, width=1024):
    v = v.reshape(-1)
    n = -(-v.shape[0] // width) * width
    return jnp.pad(v, (0, n - v.shape[0]))


def _rows(parts, rows):
    flat = jnp.concatenate(parts)
    return jnp.pad(flat, (0, rows * 1024 - flat.shape[0])).reshape(rows, 1024)


def _pack_small(b_ada, norm_w, conv_b, ssm_norm_w, q_norm_w, k_norm_w, sinks, dt_bias, a_log, d_skip, rel_bias,
                extra=None, tail=(), rows=16):
    misc = [q_norm_w, k_norm_w, sinks, dt_bias, a_log, d_skip] + ([] if extra is None else [extra])
    parts = [_flat(b_ada), _flat(norm_w), _flat(conv_b), _flat(ssm_norm_w)] + [_flat(v, 128) for v in misc]
    parts.append(jnp.zeros(((8 - len(misc)) * 128,), F32))
    parts.append(_flat(rel_bias))
    parts.append(jnp.zeros((5 * 1024,), F32))
    return _rows(parts + [_flat(v) for v in tail], rows)


def _unpack_small(p):
    misc = p[9]
    return dict(b_ada=p[0:3].reshape(1, 3072), norm_w=p[3:4], conv_b=p[4:7].reshape(1, 3072),
                ssm_norm_w=p[7:9].reshape(1, 2048), q_norm_w=misc[None, 0:64], k_norm_w=misc[None, 128:192],
                sinks=misc[None, 256:272], dt_bias=misc[None, 384:416], a_log=misc[None, 512:544],
                d_skip=misc[None, 640:672], rel_bias=p[10, :512].reshape(32, 16), extra=misc[768])


SMALL = ("b_ada", "norm_w", "conv_b", "ssm_norm_w", "q_norm_w", "k_norm_w", "sinks", "dt_bias", "a_log", "d_skip",
         "rel_bias")
WEIGHTS = ("w_ada", "b_ada", "norm_w", "w_in", "q_norm_w", "k_norm_w", "rel_bias", "sinks", "conv_w", "conv_b",
           "dt_bias", "a_log", "d_skip", "ssm_norm_w", "w_attn_proj", "w_ssm_proj", "w_out")
IN_COLS = ((0, 1024, C_Q), (1024, 256, C_K), (1280, 256, C_V), (1536, 1024, C_ZA), (2560, 2048, C_ZM),
           (4608, 3072, C_XBC), (7680, 32, C_DT), (7712, 1024, C_GA), (8736, 1024, C_GB))


def _to_cat(shards):
    parts, pos = [], 0
    for o, n, cnew in sorted(IN_COLS, key=lambda e: e[2]):
        assert cnew == pos
        c0 = o
        while c0 < o + n:
            i = c0 // SH_IN
            c1 = min(o + n, (i + 1) * SH_IN)
            parts.append(shards[i][:, c0 - i * SH_IN:c1 - i * SH_IN])
            c0 = c1
        pos += n
    parts.append(jnp.zeros((D, NP - pos), shards.dtype))
    return jnp.concatenate(parts, axis=1)


def _from_cat(w_cat):
    shards = []
    for i in range(4):
        lo, hi = i * SH_IN, (i + 1) * SH_IN
        parts = []
        for o, n, cnew in IN_COLS:
            a, b = max(o, lo), min(o + n, hi)
            if a < b:
                parts.append(w_cat[:, cnew + a - o:cnew + b - o])
        shards.append(jnp.concatenate(parts, axis=1))
    return jnp.stack(shards)


def kernel(x, c, w_ada, b_ada, norm_w, w_in, q_norm_w, k_norm_w, rel_bias, sinks, conv_w, conv_b, dt_bias, a_log, d_skip, ssm_norm_w, w_attn_proj, w_ssm_proj, w_out, loss_target, m_w_ada, m_b_ada, m_norm_w, m_w_in, m_q_norm_w, m_k_norm_w, m_rel_bias, m_sinks, m_conv_w, m_conv_b, m_dt_bias, m_a_log, m_d_skip, m_ssm_norm_w, m_w_attn_proj, m_w_ssm_proj, m_w_out, v_w_ada, v_b_ada, v_norm_w, v_w_in, v_q_norm_w, v_k_norm_w, v_rel_bias, v_sinks, v_conv_w, v_conv_b, v_dt_bias, v_a_log, v_d_skip, v_ssm_norm_w, v_w_attn_proj, v_w_ssm_proj, v_w_out):
    args = dict(locals())
    xi, yi, ci = lax.axis_index("x"), lax.axis_index("y"), lax.axis_index("c")
    chip = 2 * xi + yi
    me = 4 * xi + 2 * yi + ci
    x2 = x[0]
    tgt = loss_target[0]

    pay = _rows([c.reshape(-1), conv_w[0].reshape(-1)], 8)
    g0 = allgather_small(pay, "gather_cond")
    c_all = g0[:, 0, :]
    conv_w_full = g0[0::2, 1:4, :].reshape(4, CONV_K, 768).transpose(1, 0, 2).reshape(CONV_K, XBC)

    b_ada_sh = lax.dynamic_slice(b_ada, (0, chip * 768), (1, 768))
    mod_sh = ada_mod(c_all, w_ada[0], b_ada_sh)

    w_in_b = w_in[0].astype(BF16)
    w_rest_b = jnp.concatenate([w_attn_proj[0], w_ssm_proj[0], w_out[0]], axis=0).astype(BF16)
    wg_in, wg_rest, modg = gather_weights(w_in_b, w_rest_b, mod_sh)
    wg_in = lax.dynamic_update_slice(wg_in, w_in_b[None], (chip, 0, 0))
    wg_rest = lax.dynamic_update_slice(wg_rest, w_rest_b[None], (chip, 0, 0))
    mod = lax.dynamic_slice(modg, (0, me, 0), (4, 1, 768)).reshape(1, 3 * D)
    shift, scale, gate = mod[:, :D], mod[:, D:2 * D], mod[:, 2 * D:]
    wcat = _to_cat(wg_in)
    w_at = wg_rest[:, :R_AT].reshape(D, D)
    w_ss = wg_rest[:, R_AT:R_AT + R_SS].reshape(SSM_W, D)
    w_ou = wg_rest[:, R_AT + R_SS:].reshape(D, D)

    pad128 = lambda v: jnp.pad(v, ((0, 0), (0, 128 - v.shape[1])))
    dtb_p, alog_p, dsk_p = pad128(dt_bias), pad128(a_log), pad128(d_skip)
    bucket = _bucket_table()

    proj, dt_raw, h_t = norm_proj(x2, norm_w, scale, shift, wcat)
    biasm = bias_expand(rel_bias, sinks, bucket)
    ao = attn_fwd(proj, biasm, q_norm_w, k_norm_w)
    act, dsl = conv_fwd(proj, conv_w_full, conv_b)
    yss, sprev = ssd_fwd(act, dt_raw, dtb_p, alog_p, dsk_p)

    (loss_p, dy, dao, dza, dga, dgb, dyss, dzm, ua_t, yn_t, mg_t, dya, dyb, dout, dgate, dssm_nw) = tail(
        proj, ao, yss, x2, tgt, gate, ssm_norm_w, w_at, w_ss, w_ou)

    dq, dk, dv, dqw, dkw, dacc = attn_bwd(proj, dao, biasm, q_norm_w, k_norm_w)
    dbias = bias_reduce(dacc, bucket)
    drb = dbias[:, :NBUCKET].T
    dsk = dbias[:, NBUCKET].reshape(1, HQ)
    dact, ddt, ddtb, dalog, ddskip = ssd_bwd(act, dt_raw, dyss, sprev, dtb_p, alog_p, dsk_p)
    dxbc, dconv_w, dconv_b = conv_bwd(proj, dact, dsl, conv_w_full)

    t = x2.shape[0]
    dproj = jnp.concatenate([dq, dza, dga, dgb, dzm, dxbc, dk, dv, ddt, jnp.zeros((t, NP - C_DT - 128), BF16)], axis=1)
    dwcat = wgrad(h_t, dproj, "dw_in", TN)
    dw_at = wgrad(ua_t, dya, "dw_attn", 512)
    dw_ss = wgrad(yn_t, dyb, "dw_ssm", 512)
    dw_ou = wgrad(mg_t, dout, "dw_out", 512)

    g_in = _from_cat(dwcat)
    g_rest = jnp.concatenate([dw_at.reshape(4, R_AT, D), dw_ss.reshape(4, R_SS, D), dw_ou.reshape(4, R_OU, D)], axis=1)
    sib_in, sib_rest = pair_exchange(g_in, g_rest)
    part_in, pb_in = pair_sum(g_in, ci, sib_in, "pair_sum_in")
    part_rest, pb_rest = pair_sum(g_rest, ci, sib_rest, "pair_sum_rest")
    ssem, rsem, pb_in, pb_rest, land_in, land_rest, token = chip_exchange_start(pb_in, pb_rest)
    grad_x, dnorm_w, dscale, dshift = dproj_bwd(dproj, wcat, x2, dy, norm_w, scale + token[:1, :1])
    oth_in, oth_rest = chip_exchange_wait(ssem, rsem, pb_in, pb_rest, land_in, land_rest, dshift)
    red_in = chip_sum(part_in, chip, oth_in, "chip_sum_in")
    red_rest = chip_sum(part_rest, chip, oth_rest, "chip_sum_rest")
    recv_in, recv_rest = pair_swap(red_in, red_rest)
    both = lambda mine, theirs: jnp.concatenate([jnp.where(ci == 0, mine, theirs), jnp.where(ci == 0, theirs, mine)],
                                                axis=0)
    g_shard_in = both(red_in, recv_in)
    g_shard_rest = both(red_rest, recv_rest)

    dmod = jnp.concatenate([dshift, dscale, dgate], axis=1)
    gsmall = _pack_small(dmod, dnorm_w, dconv_b, dssm_nw, dqw, dkw, dsk[:, :HQ], ddtb[:, :SH], dalog[:, :SH],
                         ddskip[:, :SH], drb, extra=loss_p[:, :1], tail=(dconv_w,), rows=32)
    gall = allgather_small(gsmall, "gather_small_grads")
    ssum = sum_devices(gall)
    gs = _unpack_small(ssum[:16])
    loss = gs["extra"]
    dconv_w_sh = lax.dynamic_slice(ssum[16:28].reshape(CONV_K, XBC), (0, chip * 768), (CONV_K, 768))
    dmod_all = gall[:, 0:3, :].reshape(8, 3 * D)
    dw_ada = ada_grad(c_all, lax.dynamic_slice(dmod_all, (0, chip * 768), (8, 768)))

    grads = dict(gs)
    grads["w_ada"] = dw_ada
    grads["w_in"] = g_shard_in
    grads["w_attn_proj"] = g_shard_rest[:R_AT]
    grads["w_ssm_proj"] = g_shard_rest[R_AT:R_AT + R_SS]
    grads["w_out"] = g_shard_rest[R_AT + R_SS:]
    grads["conv_w"] = dconv_w_sh

    delta, new_m, new_v = {}, {}, {}
    for n in ("w_ada", "w_in", "conv_w", "w_attn_proj", "w_ssm_proj", "w_out"):
        delta[n], new_m[n], new_v[n] = adamw(args[n][0], grads[n], args["m_" + n][0], args["v_" + n][0], "adamw_" + n)
    ws = _pack_small(*[args[n] for n in SMALL])
    ms = _pack_small(*[args["m_" + n] for n in SMALL])
    vs = _pack_small(*[args["v_" + n] for n in SMALL])
    d_s, m_s, v_s = adamw(ws, ssum[:16], ms, vs, "adamw_small")
    d_s, m_s, v_s = _unpack_small(d_s), _unpack_small(m_s), _unpack_small(v_s)
    for n in SMALL:
        delta[n], new_m[n], new_v[n] = d_s[n], m_s[n], v_s[n]

    def shaped(n, a):
        return a.reshape(args[n].shape)

    outs = [loss, grad_x[None]]
    for table in (grads, delta, new_m, new_v):
        outs += [shaped(n, table[n]) for n in WEIGHTS]
    return tuple(outs)
```

```python
import functools
import math

import numpy as np
import jax
import jax.numpy as jnp
from jax import lax
from jax.experimental import pallas as pl
from jax.experimental.pallas import tpu as pltpu

F32 = jnp.float32
BF16 = jnp.bfloat16
MESH = pl.DeviceIdType.MESH

D = 1024
HQ, HKV, GRP, DH = 16, 4, 4, 64
BLK = 128
NBUCKET, MAXDIST = 32, 128
SSM_W, SH, SG, SR, SP, SN = 2048, 32, 4, 8, 64, 128
CONV_K = 4
XBC = SSM_W + 2 * SG * SN
IN_W = 9760
EPS = 1e-6
NEG = -1e30
SCALE = DH ** -0.5

C_Q, C_ZA, C_GA, C_GB, C_ZM, C_XBC, C_K, C_V, C_DT = 0, 1024, 2048, 3072, 4096, 6144, 9216, 9472, 9728
NP = 9984
TN = 1664
W_MID = C_XBC - C_ZA

SH_IN = IN_W // 4
R_AT, R_SS, R_OU = 256, 512, 256
HROWS = D // 2

ADAM_LR, ADAM_B1, ADAM_B2, ADAM_EPS, ADAM_WD, ADAM_STEP = 0.001, 0.9, 0.999, 1e-08, 0.01, 10

VMEM_LIMIT = 56 * 1024 * 1024


def _cp(sem=None):
    if sem is None:
        return pltpu.CompilerParams(vmem_limit_bytes=VMEM_LIMIT)
    return pltpu.CompilerParams(dimension_semantics=sem, vmem_limit_bytes=VMEM_LIMIT)


def _sig(x):
    return 0.5 * jnp.tanh(0.5 * x) + 0.5


def _dot(a, b):
    return jnp.dot(a, b, preferred_element_type=F32)


def _dot_nt(a, b):
    return lax.dot_general(a, b, (((1,), (1,)), ((), ())), preferred_element_type=F32)


def _dot_tn(a, b):
    return lax.dot_general(a, b, (((0,), (0,)), ((), ())), preferred_element_type=F32)


def _rsum(x):
    return jnp.sum(x, axis=-1, keepdims=True)


def _csum(x):
    return jnp.sum(x, axis=0, keepdims=True)


def _asum(x):
    return _csum(_rsum(x))


def _full(shape):
    nd = len(shape)
    return pl.BlockSpec(shape, lambda *_: (0,) * nd)


def ada_mod(c_all, w_ada_sh, b_ada_sh):
    def body(c_ref, w_ref, b_ref, o_ref):
        cv = c_ref[...]
        s = cv * _sig(cv)
        o_ref[...] = jnp.dot(s, w_ref[...], preferred_element_type=F32,
                             precision=lax.Precision.HIGHEST) + b_ref[...]

    n = w_ada_sh.shape[1]
    return pl.pallas_call(body, name="ada_mod", out_shape=jax.ShapeDtypeStruct((8, n), F32),
                          compiler_params=_cp())(c_all, w_ada_sh, b_ada_sh)


def ada_grad(c_all, dmod_sh):
    def body(c_ref, d_ref, o_ref):
        cv = c_ref[...]
        s = cv * _sig(cv)
        o_ref[...] = lax.dot_general(s, d_ref[...], (((0,), (0,)), ((), ())), preferred_element_type=F32,
                                     precision=lax.Precision.HIGHEST)

    n = dmod_sh.shape[1]
    return pl.pallas_call(body, name="ada_grad", out_shape=jax.ShapeDtypeStruct((D, n), F32),
                          compiler_params=_cp())(c_all, dmod_sh)


def norm_proj(x, norm_w, scale, shift, wcat):
    t = x.shape[0]
    tm = min(t, 1024)

    def body(x_ref, nw_ref, sc_ref, sh_ref, w_ref, p_ref, dt_ref, ht_ref, hs):
        @pl.when(pl.program_id(1) == 0)
        def _():
            xv = x_ref[...]
            r = lax.rsqrt(jnp.mean(xv * xv, axis=-1, keepdims=True) + EPS)
            h = (xv * r) * nw_ref[...]
            h = h * (1.0 + sc_ref[...]) + sh_ref[...]
            hs[...] = h.astype(BF16)
            ht_ref[...] = h.T.astype(BF16)

        p = _dot(hs[...], w_ref[...])
        p_ref[...] = p.astype(BF16)

        @pl.when(pl.program_id(1) == C_DT // TN)
        def _():
            dt_ref[...] = p[:, C_DT % TN:C_DT % TN + 128]

    vec = pl.BlockSpec((1, D), lambda i, j: (0, 0))
    return pl.pallas_call(
        body, name="norm_proj", grid=(t // tm, NP // TN),
        in_specs=[pl.BlockSpec((tm, D), lambda i, j: (i, 0)), vec, vec, vec,
                  pl.BlockSpec((D, TN), lambda i, j: (0, j))],
        out_specs=[pl.BlockSpec((tm, TN), lambda i, j: (i, j)), pl.BlockSpec((tm, 128), lambda i, j: (i, 0)),
                   pl.BlockSpec((D, tm), lambda i, j: (0, i))],
        out_shape=[jax.ShapeDtypeStruct((t, NP), BF16), jax.ShapeDtypeStruct((t, 128), F32),
                   jax.ShapeDtypeStruct((D, t), BF16)],
        scratch_shapes=[pltpu.VMEM((tm, D), BF16)],
        compiler_params=_cp(("parallel", "arbitrary")),
    )(x, norm_w, scale, shift, wcat)


def _bucket_table():
    qi = jnp.arange(BLK)[:, None]
    kj = jnp.arange(2 * BLK)[None, :]
    dist = qi + BLK - kj
    n = jnp.maximum(dist, 0)
    max_exact = NBUCKET // 2
    nf = jnp.maximum(n, 1).astype(F32)
    large = max_exact + (jnp.log(nf / max_exact) / math.log(MAXDIST / max_exact)
                         * (NBUCKET - max_exact)).astype(jnp.int32)
    large = jnp.minimum(large, NBUCKET - 1)
    bucket = jnp.where(n < max_exact, n, large).astype(jnp.int32)
    valid = (dist >= 0) & (dist < BLK)
    return jnp.where(valid, bucket, -1)


def bias_expand(rel_bias, sinks, bucket):
    def body(rb_ref, sk_ref, bk_ref, o_ref):
        hd = pl.program_id(0)
        bk = bk_ref[...]
        col = lax.broadcasted_iota(jnp.int32, (BLK, 2 * BLK), 1)

        def step(b, acc):
            return jnp.where(bk == b, rb_ref[b, hd], acc)

        acc = lax.fori_loop(0, NBUCKET, step, jnp.full((BLK, 2 * BLK), NEG, F32))
        acc = jnp.where(col == 0, sk_ref[0, hd], acc)
        o_ref[1, 0] = acc
        o_ref[0, 0] = jnp.where(jnp.logical_and(col > 0, col < BLK), NEG, acc)

    smem = pl.BlockSpec(memory_space=pltpu.SMEM)
    return pl.pallas_call(
        body, name="bias_expand", grid=(HQ,),
        in_specs=[smem, smem, _full((BLK, 2 * BLK))],
        out_specs=pl.BlockSpec((2, 1, BLK, 2 * BLK), lambda h: (0, h, 0, 0)),
        out_shape=jax.ShapeDtypeStruct((2, HQ, BLK, 2 * BLK), F32),
        compiler_params=_cp(("arbitrary",)),
    )(rel_bias, sinks, bucket)


def bias_reduce(dacc, bucket):
    col = jnp.arange(BLK * 2 * BLK, dtype=jnp.int32) % (2 * BLK)
    lane = jnp.arange(128, dtype=jnp.int32)[None, :]
    member = (bucket.reshape(-1)[:, None] == lane) | ((col[:, None] == 0) & (lane == NBUCKET))

    def body(d_ref, m_ref, o_ref):
        mm = m_ref[...]
        o_ref[...] = sum(_dot(part, mm) for part in _split3(d_ref[...]))

    return pl.pallas_call(body, name="bias_reduce", out_shape=jax.ShapeDtypeStruct((HQ, 128), F32),
                          compiler_params=_cp())(dacc.reshape(HQ, BLK * 2 * BLK), member.astype(BF16))


GQ = GRP * BLK


def _stack_heads(x, nh):
    return jnp.concatenate([x[:, DH * h:DH * (h + 1)] for h in range(nh)], axis=0)


def _unstack(xs, nh):
    rows = xs.shape[0] // nh
    return jnp.concatenate([xs[rows * h:rows * (h + 1)] for h in range(nh)], axis=1)


def _rms(x):
    return lax.rsqrt(jnp.mean(x * x, axis=-1, keepdims=True) + EPS)


def _stack_q(q, qw):
    qs = _stack_heads(q, HQ)
    r = _rms(qs)
    qhat = qs * r
    return qhat * qw, qhat, r


def _band_first(shape):
    return (lax.broadcasted_iota(jnp.int32, shape, 0) & (2 * BLK - 1)) == 0


def _stack_kv(kp, kc, vp, vc, kw):
    ks = _stack_heads(jnp.concatenate([kp, kc], axis=0), HKV)
    r = _rms(ks)
    khat = ks * r
    first = _band_first(ks.shape)
    kn = jnp.where(first, 0.0, khat * kw)
    v2 = jnp.where(first, 0.0, _stack_heads(jnp.concatenate([vp, vc], axis=0), HKV)).astype(BF16)
    return kn, khat, r, v2


def _softmax_rows(s):
    p = jnp.exp(s - jnp.max(s, axis=-1, keepdims=True))
    return p * (1.0 / _rsum(p))


def attn_fwd(proj, biasm, q_norm_w, k_norm_w):
    t = proj.shape[0]
    nb = t // BLK

    def body(q_ref, kc_ref, kp_ref, vc_ref, vp_ref, bm_ref, qw_ref, kw_ref, o_ref):
        f = lambda ref: ref[...].astype(F32)
        qn = _stack_q(f(q_ref), qw_ref[...])[0].astype(BF16)
        kn, _, _, v2 = _stack_kv(f(kp_ref), f(kc_ref), f(vp_ref), f(vc_ref), kw_ref[...])
        knb = kn.astype(BF16)
        s = jnp.concatenate([_dot_nt(qn[GQ * j:GQ * (j + 1)], knb[2 * BLK * j:2 * BLK * (j + 1)])
                             for j in range(HKV)], axis=0)
        pr = _softmax_rows(s * SCALE + bm_ref[0].reshape(HQ * BLK, 2 * BLK)).astype(BF16)
        o = jnp.concatenate([_dot(pr[GQ * j:GQ * (j + 1)], v2[2 * BLK * j:2 * BLK * (j + 1)])
                             for j in range(HKV)], axis=0)
        o_ref[...] = _unstack(o, HQ).astype(BF16)

    kblk, vblk = C_K // 256, C_V // 256
    prev = lambda n: jnp.maximum(n - 1, 0)
    return pl.pallas_call(
        body, name="attn_fwd", grid=(nb,),
        in_specs=[pl.BlockSpec((BLK, D), lambda n: (n, 0)),
                  pl.BlockSpec((BLK, 256), lambda n: (n, kblk)),
                  pl.BlockSpec((BLK, 256), lambda n: (prev(n), kblk)),
                  pl.BlockSpec((BLK, 256), lambda n: (n, vblk)),
                  pl.BlockSpec((BLK, 256), lambda n: (prev(n), vblk)),
                  pl.BlockSpec((1, HQ, BLK, 2 * BLK), lambda n: (jnp.minimum(n, 1), 0, 0, 0)),
                  _full((1, DH)), _full((1, DH))],
        out_specs=pl.BlockSpec((BLK, D), lambda n: (n, 0)),
        out_shape=jax.ShapeDtypeStruct((t, D), BF16),
        compiler_params=_cp(("parallel",)),
    )(proj, proj, proj, proj, proj, biasm, q_norm_w, k_norm_w)


def attn_bwd(proj, dao, biasm, q_norm_w, k_norm_w):
    t = proj.shape[0]
    nb = t // BLK
    kb = 2 * BLK

    def body(q_ref, kc_ref, kp_ref, vc_ref, vp_ref, do_ref, bm_ref, qw_ref, kw_ref,
             dq_ref, dkv_ref, dqw_ref, dkw_ref, dacc_ref, ck, cv, pk, pv, nk, nv):
        n = pl.program_id(0)

        @pl.when(n == 0)
        def _():
            for ref in (dqw_ref, dkw_ref, dacc_ref, ck, cv):
                ref[...] = jnp.zeros_like(ref)

        qw = qw_ref[...]
        kw = kw_ref[...]
        f = lambda ref: ref[...].astype(F32)
        kn, khat, rk, v2 = _stack_kv(f(kp_ref), f(kc_ref), f(vp_ref), f(vc_ref), kw)
        grp = lambda a, j: a[GQ * j:GQ * (j + 1)]
        band = lambda a, j: a[kb * j:kb * (j + 1)]

        @pl.when(n < nb)
        def _():
            qn, qhat, rq = _stack_q(f(q_ref), qw)
            qnb = qn.astype(BF16)
            knb = kn.astype(BF16)
            dos = _stack_heads(f(do_ref), HQ).astype(BF16)
            s = jnp.concatenate([_dot_nt(grp(qnb, j), band(knb, j)) for j in range(HKV)], axis=0)
            pr = _softmax_rows(s * SCALE + bm_ref[0].reshape(HQ * BLK, kb))
            dp = jnp.concatenate([_dot_nt(grp(dos, j), band(v2, j)) for j in range(HKV)], axis=0)
            ds = pr * (dp - _rsum(pr * dp))
            dacc_ref[...] += ds.reshape(HQ, BLK, kb)
            dsb = ds.astype(BF16)
            prb = pr.astype(BF16)
            dqn = jnp.concatenate([_dot(grp(dsb, j), band(knb, j)) for j in range(HKV)], axis=0) * SCALE
            dqhat = dqn * qw
            dq = rq * (dqhat - qhat * jnp.mean(dqhat * qhat, axis=-1, keepdims=True))
            dq_ref[...] = _unstack(dq, HQ).astype(BF16)
            dqw_ref[...] += _csum(dqn * qhat)
            first = _band_first((kb, DH))
            for j in range(HKV):
                rows = slice(BLK * j, BLK * (j + 1))
                dkn = jnp.where(first, 0.0, _dot_tn(grp(dsb, j), grp(qnb, j)) * SCALE)
                dvj = jnp.where(first, 0.0, _dot_tn(grp(prb, j), grp(dos, j)))
                pk[rows, :] = dkn[:BLK]
                nk[rows, :] = dkn[BLK:]
                pv[rows, :] = dvj[:BLK]
                nv[rows, :] = dvj[BLK:]

        @pl.when(n == nb)
        def _():
            for ref in (pk, pv, nk, nv):
                ref[...] = jnp.zeros_like(ref)

        khp = jnp.concatenate([khat[kb * j:kb * j + BLK] for j in range(HKV)], axis=0)
        rkp = jnp.concatenate([rk[kb * j:kb * j + BLK] for j in range(HKV)], axis=0)
        dkn = ck[...] + pk[...]
        dkhat = dkn * kw
        dk = rkp * (dkhat - khp * jnp.mean(dkhat * khp, axis=-1, keepdims=True))
        dkw_ref[...] += _csum(dkn * khp)
        dkv_ref[...] = jnp.concatenate([_unstack(dk, HKV), _unstack(cv[...] + pv[...], HKV)], axis=1).astype(BF16)
        ck[...] = nk[...]
        cv[...] = nv[...]

    kblk, vblk = C_K // 256, C_V // 256
    cur = lambda n: jnp.minimum(n, nb - 1)
    prev = lambda n: jnp.maximum(n - 1, 0)
    carry = pltpu.VMEM((HKV * BLK, DH), F32)
    return pl.pallas_call(
        body, name="attn_bwd", grid=(nb + 1,),
        in_specs=[pl.BlockSpec((BLK, D), lambda n: (cur(n), 0)),
                  pl.BlockSpec((BLK, 256), lambda n: (cur(n), kblk)), pl.BlockSpec((BLK, 256), lambda n: (prev(n), kblk)),
                  pl.BlockSpec((BLK, 256), lambda n: (cur(n), vblk)), pl.BlockSpec((BLK, 256), lambda n: (prev(n), vblk)),
                  pl.BlockSpec((BLK, D), lambda n: (cur(n), 0)),
                  pl.BlockSpec((1, HQ, BLK, kb), lambda n: (jnp.minimum(n, 1), 0, 0, 0)),
                  _full((1, DH)), _full((1, DH))],
        out_specs=[pl.BlockSpec((BLK, D), lambda n: (cur(n), 0)),
                   pl.BlockSpec((BLK, 512), lambda n: (prev(n), 0)),
                   _full((1, DH)), _full((1, DH)), _full((HQ, BLK, kb))],
        out_shape=[jax.ShapeDtypeStruct((t, D), BF16), jax.ShapeDtypeStruct((t, 512), BF16),
                   jax.ShapeDtypeStruct((1, DH), F32),
                   jax.ShapeDtypeStruct((1, DH), F32), jax.ShapeDtypeStruct((HQ, BLK, kb), F32)],
        scratch_shapes=[carry] * 6,
        compiler_params=_cp(("arbitrary",)),
    )(proj, proj, proj, proj, proj, dao, biasm, q_norm_w, k_norm_w)


CONV_TM, CONV_CW, CONV_RC, HALO = 512, 512, 32, 16


def conv_fwd(proj, conv_w, conv_b):
    t = proj.shape[0]
    tm = min(t, CONV_TM)
    c0 = C_XBC // CONV_CW

    def body(x_ref, xp_ref, w_ref, b_ref, o_ref, ds_ref):
        i = pl.program_id(1)
        w = w_ref[...]
        b = b_ref[...]
        for r in range(tm // CONV_RC):
            lo = r * CONV_RC
            if r == 0:
                head = jnp.where(i == 0, 0.0, xp_ref[...].astype(F32))
                win = jnp.concatenate([head, x_ref[0:CONV_RC, :].astype(F32)], axis=0)
            else:
                win = x_ref[lo - HALO:lo + CONV_RC, :].astype(F32)
            acc = b
            for j in range(CONV_K):
                acc = acc + w[j:j + 1] * win[HALO - 3 + j:HALO - 3 + j + CONV_RC]
            sg = _sig(acc)
            o_ref[lo:lo + CONV_RC, :] = acc * sg
            ds_ref[lo:lo + CONV_RC, :] = _dsilu(acc, sg).astype(BF16)

    rh = tm // HALO
    tile = pl.BlockSpec((tm, CONV_CW), lambda s, i: (i, s))
    return pl.pallas_call(
        body, name="conv_fwd", grid=(XBC // CONV_CW, t // tm),
        in_specs=[pl.BlockSpec((tm, CONV_CW), lambda s, i: (i, c0 + s)),
                  pl.BlockSpec((HALO, CONV_CW), lambda s, i: (jnp.maximum(i * rh - 1, 0), c0 + s)),
                  pl.BlockSpec((CONV_K, CONV_CW), lambda s, i: (0, s)), pl.BlockSpec((1, CONV_CW), lambda s, i: (0, s))],
        out_specs=[tile, tile],
        out_shape=[jax.ShapeDtypeStruct((t, XBC), F32), jax.ShapeDtypeStruct((t, XBC), BF16)],
        compiler_params=_cp(("parallel", "parallel")),
    )(proj, proj, conv_w, conv_b)


def conv_bwd(proj, dact, dsl, conv_w):
    t = proj.shape[0]
    tm = min(t, CONV_TM)
    nt = t // tm
    nr = tm // CONV_RC
    c0 = C_XBC // CONV_CW
    ext = CONV_RC + 8

    def body(x_ref, xp_ref, d_ref, dn_ref, s_ref, sn_ref, w_ref, dx_ref, dw_ref, db_ref):
        i = pl.program_id(1)

        @pl.when(i == 0)
        def _():
            dw_ref[...] = jnp.zeros_like(dw_ref)
            db_ref[...] = jnp.zeros_like(db_ref)

        w = w_ref[...]
        dws = [jnp.zeros((1, CONV_CW), F32) for _ in range(CONV_K)]
        db = jnp.zeros((1, CONV_CW), F32)
        for r in range(nr):
            lo = r * CONV_RC
            if r == 0:
                head = jnp.where(i == 0, 0.0, xp_ref[...].astype(F32))
                win = jnp.concatenate([head, x_ref[0:CONV_RC, :].astype(F32)], axis=0)
            else:
                win = x_ref[lo - HALO:lo + CONV_RC, :].astype(F32)
            if r < nr - 1:
                dext = d_ref[lo:lo + ext, :]
                sext = s_ref[lo:lo + CONV_RC + HALO, :].astype(F32)[0:ext]
            else:
                dext = jnp.concatenate([d_ref[lo:lo + CONV_RC, :], jnp.where(i == nt - 1, 0.0, dn_ref[...])], axis=0)
                sext = jnp.concatenate([s_ref[lo:lo + CONV_RC, :].astype(F32), sn_ref[...].astype(F32)], axis=0)[0:ext]
            dpre = dext * sext
            dx = jnp.zeros((CONV_RC, CONV_CW), F32)
            own = dpre[0:CONV_RC]
            for j in range(CONV_K):
                dx = dx + w[j:j + 1] * dpre[3 - j:3 - j + CONV_RC]
                dws[j] = dws[j] + _csum(own * win[HALO - 3 + j:HALO - 3 + j + CONV_RC])
            db = db + _csum(own)
            dx_ref[lo:lo + CONV_RC, :] = dx.astype(BF16)
        dw_ref[...] += jnp.concatenate(dws, axis=0)
        db_ref[...] += db

    rh = tm // HALO
    r8 = tm // 8
    nxt = lambda i, per: jnp.minimum((i + 1) * per, nt * per - 1)
    return pl.pallas_call(
        body, name="conv_bwd", grid=(XBC // CONV_CW, nt),
        in_specs=[pl.BlockSpec((tm, CONV_CW), lambda s, i: (i, c0 + s)),
                  pl.BlockSpec((HALO, CONV_CW), lambda s, i: (jnp.maximum(i * rh - 1, 0), c0 + s)),
                  pl.BlockSpec((tm, CONV_CW), lambda s, i: (i, s)),
                  pl.BlockSpec((8, CONV_CW), lambda s, i: (nxt(i, r8), s)),
                  pl.BlockSpec((tm, CONV_CW), lambda s, i: (i, s)),
                  pl.BlockSpec((HALO, CONV_CW), lambda s, i: (nxt(i, rh), s)),
                  pl.BlockSpec((CONV_K, CONV_CW), lambda s, i: (0, s))],
        out_specs=[pl.BlockSpec((tm, CONV_CW), lambda s, i: (i, s)),
                   pl.BlockSpec((CONV_K, CONV_CW), lambda s, i: (0, s)), pl.BlockSpec((1, CONV_CW), lambda s, i: (0, s))],
        out_shape=[jax.ShapeDtypeStruct((t, XBC), BF16), jax.ShapeDtypeStruct((CONV_K, XBC), F32),
                   jax.ShapeDtypeStruct((1, XBC), F32)],
        compiler_params=_cp(("parallel", "arbitrary")),
    )(proj, proj, dact, dact, dsl, dsl, conv_w)


def _split3(x):
    h = x.astype(BF16)
    r = x - h.astype(F32)
    m = r.astype(BF16)
    lo = (r - m.astype(F32)).astype(BF16)
    return h, m, lo


def _tri_mm(tri, x):
    h, m, lo = _split3(x)
    return _dot(tri, h) + _dot(tri, m) + _dot(tri, lo)


def _softplus(x):
    return jnp.maximum(x, 0.0) + jnp.log1p(jnp.exp(-jnp.abs(x)))


def _chunk_decays(dt_raw, dtb, alog):
    dtv = _softplus(dt_raw + dtb)
    a = -jnp.exp(alog)
    ri = lax.broadcasted_iota(jnp.int32, (BLK, BLK), 0)
    ci = lax.broadcasted_iota(jnp.int32, (BLK, BLK), 1)
    causal = ri >= ci
    acum = _tri_mm(causal.astype(BF16), dtv * a)
    return dtv, a, causal, acum, acum.T


NPAIR = SH // 2


def _pairs(x):
    return jnp.stack([x[:, 128 * k:128 * (k + 1)] for k in range(NPAIR)])


def _unpairs(x3):
    return jnp.concatenate([x3[k] for k in range(NPAIR)], axis=1)


def _per_head_cols(m):
    return jnp.stack([jnp.broadcast_to(m[:, h:h + 1], m.shape) for h in range(SH)])


def _pair_lanes(t):
    r = t.reshape(NPAIR, 2, t.shape[1], 128)
    lo = lax.broadcasted_iota(jnp.int32, (1, t.shape[1], 128), 2) < SP
    return jnp.where(lo, r[:, 0], r[:, 1])


class _Chunk:
    pass


def _chunk_common(dt_raw, dtb, alog, dskip):
    cm = _Chunk()
    cm.dtv, cm.a, cm.causal, acum, acum_t = _chunk_decays(dt_raw, dtb, alog)
    cm.acol = _per_head_cols(acum)
    cm.arow = jnp.stack([acum_t[h:h + 1, :] for h in range(SH)])
    cm.lam = jnp.exp(jnp.where(cm.causal[None], cm.acol - cm.arow, NEG))
    apl = _pair_lanes(cm.acol)
    alast = apl[:, BLK - 1:BLK, :]
    cm.dpl = _pair_lanes(_per_head_cols(cm.dtv))
    cm.eapl = jnp.exp(apl)
    cm.epl = jnp.exp(alast - apl)
    cm.cdpl = jnp.exp(alast)
    cm.dskpl = _pair_lanes(_per_head_cols(dskip))
    cm.lo = lax.broadcasted_iota(jnp.int32, (1, BLK, 128), 2) < SP
    return cm


def ssd_fwd(act, dt_raw, dtb_p, alog_p, dsk_p):
    t = act.shape[0]
    nc = t // BLK

    def body(xs_ref, b_ref, c_ref, dt_ref, dtb_ref, al_ref, dk_ref, y_ref, sp_ref, st):
        c = pl.program_id(0)

        @pl.when(c == 0)
        def _():
            st[...] = jnp.zeros_like(st)

        s_t = st[...]
        sp_ref[0] = s_t
        cm = _chunk_common(dt_ref[...], dtb_ref[...], al_ref[...], dk_ref[...])
        gms, cbs, bts = [], [], []
        for g in range(SG):
            bf = b_ref[:, SN * g:SN * (g + 1)]
            cb = c_ref[:, SN * g:SN * (g + 1)].astype(BF16)
            gms.append(_dot_nt(cb, bf.astype(BF16)))
            cbs.append(cb)
            bts.append(bf.T.astype(BF16))
        m = (cm.lam.reshape(SG, SR, BLK, BLK) * jnp.stack(gms)[:, None]).reshape(SH, BLK, BLK).astype(BF16)
        xs16 = _pairs(xs_ref[...])
        xdt16 = xs16 * cm.dpl
        x_lo = jnp.where(cm.lo, xdt16, 0.0).astype(BF16)
        x_hi = jnp.where(cm.lo, 0.0, xdt16).astype(BF16)
        s16 = _pairs(s_t)
        s16b = s16.astype(BF16)
        yd = jnp.stack([_dot(m[2 * k], x_lo[k]) + _dot(m[2 * k + 1], x_hi[k]) for k in range(NPAIR)])
        yo = jnp.stack([_dot(cbs[k // (NPAIR // SG)], s16b[k]) for k in range(NPAIR)])
        y_ref[...] = _unpairs(yd + yo * cm.eapl + cm.dskpl * xs16).astype(BF16)
        xe = (xdt16 * cm.epl).astype(BF16)
        st[...] = _unpairs(cm.cdpl * s16 + jnp.stack([_dot(bts[k // (NPAIR // SG)], xe[k]) for k in range(NPAIR)]))

    vec = _full((1, 128))
    return pl.pallas_call(
        body, name="ssd_fwd", grid=(nc,),
        in_specs=[pl.BlockSpec((BLK, SSM_W), lambda c: (c, 0)),
                  pl.BlockSpec((BLK, SG * SN), lambda c: (c, SSM_W // (SG * SN))),
                  pl.BlockSpec((BLK, SG * SN), lambda c: (c, SSM_W // (SG * SN) + 1)),
                  pl.BlockSpec((BLK, 128), lambda c: (c, 0)), vec, vec, vec],
        out_specs=[pl.BlockSpec((BLK, SSM_W), lambda c: (c, 0)), pl.BlockSpec((1, SN, SSM_W), lambda c: (c, 0, 0))],
        out_shape=[jax.ShapeDtypeStruct((t, SSM_W), BF16), jax.ShapeDtypeStruct((nc, SN, SSM_W), F32)],
        scratch_shapes=[pltpu.VMEM((SN, SSM_W), F32)],
        compiler_params=_cp(("arbitrary",)),
    )(act, act, act, dt_raw, dtb_p, alog_p, dsk_p)


def _head_sums(q):
    r = q.shape[1]
    lo = lax.broadcasted_iota(jnp.int32, (1, r, 128), 2) < SP
    s_lo = jnp.sum(jnp.where(lo, q, 0.0), axis=-1, keepdims=True)
    s_hi = jnp.sum(jnp.where(lo, 0.0, q), axis=-1, keepdims=True)
    lane = lax.broadcasted_iota(jnp.int32, (r, 128), 1)
    out = jnp.zeros((r, 128), F32)
    for k in range(NPAIR):
        out = jnp.where(lane == 2 * k, s_lo[k], jnp.where(lane == 2 * k + 1, s_hi[k], out))
    return out


def ssd_bwd(act, dt_raw, dy, sprev, dtb_p, alog_p, dsk_p):
    t = act.shape[0]
    nc = t // BLK

    def body(xs_ref, b_ref, c_ref, dt_ref, dy_ref, sp_ref, dtb_ref, al_ref, dk_ref,
             da_ref, ddt_ref, ddtb_ref, dal_ref, ddk_ref, dst):
        i = pl.program_id(0)

        @pl.when(i == 0)
        def _():
            dst[...] = jnp.zeros_like(dst)
            ddtb_ref[...] = jnp.zeros_like(ddtb_ref)
            dal_ref[...] = jnp.zeros_like(dal_ref)
            ddk_ref[...] = jnp.zeros_like(ddk_ref)

        dt_raw = dt_ref[...]
        dtb = dtb_ref[...]
        cm = _chunk_common(dt_raw, dtb, al_ref[...], dk_ref[...])
        ri = lax.broadcasted_iota(jnp.int32, (BLK, BLK), 0)
        ci = lax.broadcasted_iota(jnp.int32, (BLK, BLK), 1)
        lam_t = jnp.exp(jnp.where((ri <= ci)[None], cm.arow - cm.acol, NEG))
        bbs, cbs, cts, gms = [], [], [], []
        for g in range(SG):
            bf = b_ref[:, SN * g:SN * (g + 1)]
            cf = c_ref[:, SN * g:SN * (g + 1)]
            bbs.append(bf.astype(BF16))
            cbs.append(cf.astype(BF16))
            cts.append(cf.T.astype(BF16))
            gms.append(_dot_nt(bbs[g], cbs[g]))
        grp = lambda k: k // (NPAIR // SG)
        xs16 = _pairs(xs_ref[...])
        dy16 = _pairs(dy_ref[...].astype(F32))
        sp16 = _pairs(sp_ref[0])
        ds16 = _pairs(dst[...])
        xdt16 = xs16 * cm.dpl
        xdtb = xdt16.astype(BF16)
        dyh = [jnp.where(cm.lo, dy16, 0.0).astype(BF16), jnp.where(cm.lo, 0.0, dy16).astype(BF16)]
        m_t = (lam_t.reshape(SG, SR, BLK, BLK) * jnp.stack(gms)[:, None]).reshape(SH, BLK, BLK).astype(BF16)
        dxdt = jnp.stack([_dot(m_t[2 * k], dyh[0][k]) + _dot(m_t[2 * k + 1], dyh[1][k]) for k in range(NPAIR)])
        dm = jnp.stack([_dot_nt(dyh[h % 2][h // 2], xdtb[h // 2]) for h in range(SH)])
        dgl = (dm * cm.lam).reshape(SG, SR, BLK, BLK)
        dg = jnp.sum(dgl, axis=1).astype(BF16)
        w = (dgl * jnp.stack([_dot_nt(cbs[g], bbs[g]) for g in range(SG)])[:, None]).reshape(SH, BLK, BLK)
        w_rows = jnp.sum(w, axis=2, keepdims=True)
        w_cols = jnp.concatenate([jnp.sum(w, axis=1)] + [jnp.zeros((128 - SH, BLK), F32)], axis=0).T
        lane_c = lax.broadcasted_iota(jnp.int32, (BLK, 128), 1)
        da_cols = -w_cols
        for h in range(SH):
            da_cols = jnp.where(lane_c == h, da_cols + w_rows[h], da_cols)
        ds16b = ds16.astype(BF16)
        sp16b = sp16.astype(BF16)
        dxs = jnp.stack([_dot(bbs[grp(k)], ds16b[k]) for k in range(NPAIR)]) * cm.epl
        dxdt = dxdt + dxs
        dya = (dy16 * cm.eapl).astype(BF16)
        xe = (xdt16 * cm.epl).astype(BF16)
        dcs, dbs = [], []
        for g in range(SG):
            ks = range(g * (NPAIR // SG), (g + 1) * (NPAIR // SG))
            dcs.append(sum(_dot_nt(dya[k], sp16b[k]) for k in ks) + _dot(dg[g], bbs[g]))
            dbs.append(sum(_dot_nt(xe[k], ds16b[k]) for k in ks) + _dot_tn(dg[g], cbs[g]))
        dst[...] = _unpairs(cm.cdpl * ds16 + jnp.stack([_dot(cts[grp(k)], dya[k]) for k in range(NPAIR)]))
        da_ref[...] = jnp.concatenate([_unpairs(dxdt * cm.dpl + cm.dskpl * dy16)] + dbs + dcs, axis=1)
        y_off = jnp.stack([_dot(cbs[grp(k)], sp16b[k]) for k in range(NPAIR)]) * cm.eapl
        da_cols = da_cols + _head_sums(dy16 * y_off - xdt16 * dxs)
        last = _head_sums(jnp.sum(xdt16 * dxs, axis=1, keepdims=True)
                          + cm.cdpl * jnp.sum(ds16 * sp16, axis=1, keepdims=True))
        ddt = _head_sums(dxdt * xs16)
        row_i = lax.broadcasted_iota(jnp.int32, (BLK, 128), 0)
        dacum = da_cols + jnp.where(row_i == BLK - 1, last, 0.0)
        dda = _tri_mm((ri <= ci).astype(BF16), dacum)
        ddt = ddt + dda * cm.a
        dal_ref[...] += _csum(dda * cm.dtv) * cm.a
        ddt_raw = jnp.where(lane_c < SH, ddt * _sig(dt_raw + dtb), 0.0)
        ddt_ref[...] = ddt_raw.astype(BF16)
        ddtb_ref[...] += _csum(ddt_raw)
        ddk_ref[...] += _head_sums(jnp.sum(dy16 * xs16, axis=1, keepdims=True))

    rev = lambda i: nc - 1 - i
    vec = _full((1, 128))
    slab = pl.BlockSpec((BLK, SSM_W), lambda i: (rev(i), 0))
    return pl.pallas_call(
        body, name="ssd_bwd", grid=(nc,),
        in_specs=[slab,
                  pl.BlockSpec((BLK, SG * SN), lambda i: (rev(i), SSM_W // (SG * SN))),
                  pl.BlockSpec((BLK, SG * SN), lambda i: (rev(i), SSM_W // (SG * SN) + 1)),
                  pl.BlockSpec((BLK, 128), lambda i: (rev(i), 0)),
                  slab,
                  pl.BlockSpec((1, SN, SSM_W), lambda i: (rev(i), 0, 0)), vec, vec, vec],
        out_specs=[pl.BlockSpec((BLK, XBC), lambda i: (rev(i), 0)), pl.BlockSpec((BLK, 128), lambda i: (rev(i), 0)),
                   vec, vec, vec],
        out_shape=[jax.ShapeDtypeStruct((t, XBC), F32), jax.ShapeDtypeStruct((t, 128), BF16),
                   jax.ShapeDtypeStruct((1, 128), F32), jax.ShapeDtypeStruct((1, 128), F32),
                   jax.ShapeDtypeStruct((1, 128), F32)],
        scratch_shapes=[pltpu.VMEM((SN, SSM_W), F32)],
        compiler_params=_cp(("arbitrary",)),
    )(act, act, act, dt_raw, dy, sprev, dtb_p, alog_p, dsk_p)


TAIL_TM = 256


def _dsilu(z, s):
    return s * (1.0 + z * (1.0 - s))


def tail(proj, ao, yss, x, target, gate, ssm_nw, w_at, w_ss, w_ou):
    t = x.shape[0]
    tm = min(t, TAIL_TM)
    gw = SSM_W // SG

    def body(ao_ref, za_ref, ga_ref, gb_ref, zm_ref, ys_ref, x_ref, tg_ref, gt_ref, nw_ref, wa_ref, ws_ref, wo_ref,
             loss_ref, dy_ref, dao_ref, dmid_ref, dys_ref,
             ua_ref, yn_ref, mg_ref, dya_ref, dyb_ref, do_ref, dgt_ref, dnw_ref):
        i = pl.program_id(0)

        @pl.when(i == 0)
        def _():
            loss_ref[...] = jnp.zeros_like(loss_ref)
            dgt_ref[...] = jnp.zeros_like(dgt_ref)
            dnw_ref[...] = jnp.zeros_like(dnw_ref)

        ao = ao_ref[...].astype(F32)
        za = za_ref[...].astype(F32)
        sa = _sig(za)
        sila = za * sa
        ua_f = ao * sila
        ua = ua_f.astype(BF16)
        ya = _dot(ua, wa_ref[...])
        zm = zm_ref[...].astype(F32)
        sm = _sig(zm)
        silm = zm * sm
        ys = ys_ref[...].astype(F32)
        u = ys * silm
        nw = nw_ref[...]
        rs, uns = [], []
        for g in range(SG):
            ug = u[:, gw * g:gw * (g + 1)]
            r = lax.rsqrt(jnp.mean(ug * ug, axis=-1, keepdims=True) + EPS)
            rs.append(r)
            uns.append(ug * r)
        un = jnp.concatenate(uns, axis=1)
        yn_f = un * nw
        yn = yn_f.astype(BF16)
        yb = _dot(yn, ws_ref[...])
        sga = _sig(ga_ref[...].astype(F32))
        sgb = _sig(gb_ref[...].astype(F32))
        mg_f = sga * ya + sgb * yb
        mg = mg_f.astype(BF16)
        o = _dot(mg, wo_ref[...])
        gt = gt_ref[...]
        err = (x_ref[...] + gt * o) - tg_ref[...]
        lane = lax.broadcasted_iota(jnp.int32, (1, 128), 1)
        loss_ref[...] += jnp.where(lane == 0, 0.5 * _asum(_rsum(err * err) / D), 0.0)
        dy = err * (1.0 / D)
        dy_ref[...] = dy
        dgt_ref[...] += _csum(dy * o)
        do = (dy * gt).astype(BF16)
        dmg = _dot_nt(do, wo_ref[...])
        dmid_ref[:, C_GA - C_ZA:C_GB - C_ZA] = (dmg * ya * sga * (1.0 - sga)).astype(BF16)
        dmid_ref[:, C_GB - C_ZA:C_ZM - C_ZA] = (dmg * yb * sgb * (1.0 - sgb)).astype(BF16)
        dya = (dmg * sga).astype(BF16)
        dyb = (dmg * sgb).astype(BF16)
        dua = _dot_nt(dya, wa_ref[...])
        dao_ref[...] = (dua * sila).astype(BF16)
        dmid_ref[:, 0:C_GA - C_ZA] = (dua * ao * _dsilu(za, sa)).astype(BF16)
        dyn = _dot_nt(dyb, ws_ref[...])
        dnw_ref[...] += _csum(dyn * un)
        dun = dyn * nw
        dus = []
        for g in range(SG):
            gs = slice(gw * g, gw * (g + 1))
            dus.append(rs[g] * (dun[:, gs] - uns[g] * jnp.mean(dun[:, gs] * uns[g], axis=-1, keepdims=True)))
        du = jnp.concatenate(dus, axis=1)
        dys_ref[...] = (du * silm).astype(BF16)
        dmid_ref[:, C_ZM - C_ZA:] = (du * ys * _dsilu(zm, sm)).astype(BF16)
        ua_ref[...] = ua_f.T.astype(BF16)
        yn_ref[...] = yn_f.T.astype(BF16)
        mg_ref[...] = mg_f.T.astype(BF16)
        dya_ref[...] = dya
        dyb_ref[...] = dyb
        do_ref[...] = do

    row = lambda w: pl.BlockSpec((tm, w), lambda i: (i, 0))
    pcol = lambda w, c0: pl.BlockSpec((tm, w), lambda i: (i, c0 // w))
    sd = lambda w, dt: jax.ShapeDtypeStruct((t, w), dt)
    colt = lambda w: pl.BlockSpec((w, tm), lambda i: (0, i))
    sdt = lambda w: jax.ShapeDtypeStruct((w, t), BF16)
    return pl.pallas_call(
        body, name="tail", grid=(t // tm,),
        in_specs=[row(D), pcol(D, C_ZA), pcol(D, C_GA), pcol(D, C_GB), pcol(SSM_W, C_ZM), row(SSM_W), row(D), row(D),
                  _full((1, D)), _full((1, SSM_W)), _full((D, D)), _full((SSM_W, D)), _full((D, D))],
        out_specs=[_full((1, 128)), row(D), row(D), row(W_MID), row(SSM_W),
                   colt(D), colt(SSM_W), colt(D), row(D), row(D), row(D), _full((1, D)), _full((1, SSM_W))],
        out_shape=[jax.ShapeDtypeStruct((1, 128), F32), sd(D, F32), sd(D, BF16), sd(W_MID, BF16),
                   sd(SSM_W, BF16), sdt(D), sdt(SSM_W), sdt(D), sd(D, BF16),
                   sd(D, BF16), sd(D, BF16), jax.ShapeDtypeStruct((1, D), F32), jax.ShapeDtypeStruct((1, SSM_W), F32)],
        compiler_params=_cp(("arbitrary",)),
    )(ao, proj, proj, proj, proj, yss, x, target, gate, ssm_nw, w_at, w_ss, w_ou)


DPIECES = ((D, ((D, C_Q),)),
           (W_MID, ((D, C_ZA), (D, C_GA), (D, C_GB), (SSM_W, C_ZM))),
           (XBC, ((XBC, C_XBC),)),
           (512, ((512, C_K),)),
           (128, ((128, C_DT),)))


def dproj_bwd(pieces, wcat, x, dy, norm_w, scale):
    t = x.shape[0]
    tm = min(t, 256)
    nt = t // tm
    wblocks = [blk for _, subs in DPIECES for blk in subs]
    npc, nwb = len(DPIECES), len(wblocks)

    def body(*refs):
        p_refs, w_refs = refs[:npc], refs[npc:npc + nwb]
        x_ref, dy_ref, nw_ref, sc_ref, gx_ref, dnw_ref, dsc_ref, dsh_ref, dwe_ref = refs[npc + nwb:]
        i = pl.program_id(0)

        @pl.when(i == 0)
        def _():
            for ref in (dwe_ref, dsh_ref, dnw_ref, dsc_ref):
                ref[...] = jnp.zeros_like(ref)

        dh, wi = None, 0
        for p_ref, (_, subs) in zip(p_refs, DPIECES):
            loc = 0
            for w, _ in subs:
                part = _dot_nt(p_ref[:, loc:loc + w], w_refs[wi][...])
                dh = part if dh is None else dh + part
                loc += w
                wi += 1
        xv = x_ref[...]
        r = lax.rsqrt(jnp.mean(xv * xv, axis=-1, keepdims=True) + EPS)
        xn = xv * r
        weff = nw_ref[...] * (1.0 + sc_ref[...])
        dxn = dh * weff
        gx_ref[...] = dy_ref[...] + r * (dxn - xn * jnp.mean(dxn * xn, axis=-1, keepdims=True))
        dwe_ref[...] += _csum(dh * xn)
        dsh_ref[...] += _csum(dh)

        @pl.when(i == nt - 1)
        def _():
            dwe = dwe_ref[...]
            dnw_ref[...] = dwe * (1.0 + sc_ref[...])
            dsc_ref[...] = dwe * nw_ref[...]

    vec = pl.BlockSpec((1, D), lambda i: (0, 0))
    row = pl.BlockSpec((tm, D), lambda i: (i, 0))
    return pl.pallas_call(
        body, name="dproj_bwd", grid=(nt,),
        in_specs=[pl.BlockSpec((tm, pw), lambda i: (i, 0)) for pw, _ in DPIECES]
        + [pl.BlockSpec((D, w), functools.partial(lambda i, b: (0, b), b=off // w), pipeline_mode=pl.Buffered(1))
           for w, off in wblocks]
        + [row, row, vec, vec],
        out_specs=[row, vec, vec, vec],
        out_shape=[jax.ShapeDtypeStruct((t, D), F32), jax.ShapeDtypeStruct((1, D), F32),
                   jax.ShapeDtypeStruct((1, D), F32), jax.ShapeDtypeStruct((1, D), F32)],
        scratch_shapes=[pltpu.VMEM((1, D), F32)],
        compiler_params=_cp(("arbitrary",)),
    )(*pieces, *([wcat] * nwb), x, dy, norm_w, scale)


def wgrad(at, b, name, bn):
    m, t = at.shape
    n = b.shape[1]
    tk = min(t, 1024)
    bm = min(m, 1024)

    def body(a_ref, b_ref, o_ref):
        part = _dot(a_ref[...], b_ref[...])

        @pl.when(pl.program_id(2) == 0)
        def _():
            o_ref[...] = part

        @pl.when(pl.program_id(2) > 0)
        def _():
            o_ref[...] += part

    return pl.pallas_call(
        body, name=name, grid=(m // bm, n // bn, t // tk),
        in_specs=[pl.BlockSpec((bm, tk), lambda i, j, k: (i, k)), pl.BlockSpec((tk, bn), lambda i, j, k: (k, j))],
        out_specs=pl.BlockSpec((bm, bn), lambda i, j, k: (i, j)),
        out_shape=jax.ShapeDtypeStruct((m, n), F32),
        compiler_params=_cp(("parallel", "parallel", "arbitrary")),
    )(at, b)


SUM_TR = 256


def pair_sum(g, core, theirs, name):
    w = g.shape[2]
    nh = HROWS // SUM_TR

    def body(core_ref, a_ref, b_ref, o_ref, ob_ref):
        s = a_ref[...] + b_ref[...]
        o_ref[...] = s
        ob_ref[...] = s.astype(BF16)

    spec = pl.BlockSpec((1, SUM_TR, w), lambda d, i, c: (d, i, 0))
    return pl.pallas_call(
        body, name=name,
        out_shape=[jax.ShapeDtypeStruct((4, HROWS, w), F32), jax.ShapeDtypeStruct((4, HROWS, w), BF16)],
        grid_spec=pltpu.PrefetchScalarGridSpec(
            num_scalar_prefetch=1, grid=(4, nh),
            in_specs=[pl.BlockSpec((1, SUM_TR, w), lambda d, i, c: (d, c[0] * nh + i, 0)), spec],
            out_specs=[spec, spec]),
        compiler_params=_cp(("parallel", "parallel")))(core.reshape(1).astype(jnp.int32), g, theirs)


def chip_sum(part, chip, others, name):
    r, w = part.shape[1:]

    def body(chip_ref, a_ref, b_ref, o_ref):
        acc = a_ref[0]
        for k in range(3):
            acc = acc + b_ref[k].astype(F32)
        o_ref[...] = acc

    return pl.pallas_call(
        body, name=name, out_shape=jax.ShapeDtypeStruct((r, w), F32),
        grid_spec=pltpu.PrefetchScalarGridSpec(
            num_scalar_prefetch=1, grid=(r // SUM_TR,),
            in_specs=[pl.BlockSpec((1, SUM_TR, w), lambda i, c: (c[0], i, 0)),
                      pl.BlockSpec((3, SUM_TR, w), lambda i, c: (0, i, 0))],
            out_specs=pl.BlockSpec((SUM_TR, w), lambda i, c: (i, 0))),
        compiler_params=_cp(("parallel",)))(chip.reshape(1).astype(jnp.int32), part, others)


def sum_devices(g):
    r = g.shape[1]

    def body(g_ref, o_ref):
        acc = g_ref[0]
        for d in range(1, 8):
            acc = acc + g_ref[d]
        o_ref[...] = acc

    return pl.pallas_call(body, name="sum_devices", out_shape=jax.ShapeDtypeStruct((r, 1024), F32),
                          compiler_params=_cp())(g)


def adamw(w, g, m, v, name):
    r, c = w.shape
    tr = r
    for cand in (256, 128, 64, 32, 16, 8):
        if r % cand == 0 and r > cand:
            tr = cand
            break

    def body(w_ref, g_ref, m_ref, v_ref, d_ref, nm_ref, nv_ref):
        gv = g_ref[...]
        mn = ADAM_B1 * m_ref[...] + (1.0 - ADAM_B1) * gv
        vn = ADAM_B2 * v_ref[...] + (1.0 - ADAM_B2) * (gv * gv)
        m_hat = mn / (1.0 - ADAM_B1 ** ADAM_STEP)
        v_hat = vn / (1.0 - ADAM_B2 ** ADAM_STEP)
        d_ref[...] = -ADAM_LR * (m_hat / (jnp.sqrt(v_hat) + ADAM_EPS) + ADAM_WD * w_ref[...])
        nm_ref[...] = mn
        nv_ref[...] = vn

    spec = pl.BlockSpec((tr, c), lambda i: (i, 0))
    sd = jax.ShapeDtypeStruct((r, c), F32)
    return pl.pallas_call(body, name=name, grid=(r // tr,), in_specs=[spec] * 4, out_specs=[spec] * 3,
                          out_shape=[sd, sd, sd], compiler_params=_cp(("parallel",)))(w, g, m, v)


ANY = pl.BlockSpec(memory_space=pl.ANY)
VM = pl.BlockSpec(memory_space=pltpu.VMEM)
OTHER_CHIPS = ((1, 0), (0, 1), (1, 1))


def _pos():
    return lax.axis_index("x"), lax.axis_index("y"), lax.axis_index("c")


def _flip(v, bit):
    return 1 - v if bit else v


def _rcopy(src, dst, ssem, rsem, peer):
    return pltpu.make_async_remote_copy(src_ref=src, dst_ref=dst, send_sem=ssem, recv_sem=rsem,
                                        device_id=peer, device_id_type=MESH)


def allgather_small(p, name):
    r = p.shape[0]

    def body(in_ref, out_ref, ssem, rsem, lsem):
        x, y, c = _pos()
        me = 4 * x + 2 * y + c
        loc = pltpu.make_async_copy(in_ref, out_ref.at[me], lsem)
        loc.start()
        sends = []
        peers = []
        for k in range(1, 8):
            px, py, pc = _flip(x, (k >> 2) & 1), _flip(y, (k >> 1) & 1), _flip(c, k & 1)
            peers.append((px, py, pc))
            cp = _rcopy(in_ref, out_ref.at[me], ssem.at[k - 1], rsem.at[k - 1], (px, py, pc))
            cp.start()
            sends.append(cp)
        for k in range(1, 8):
            px, py, pc = peers[k - 1]
            _rcopy(in_ref, out_ref.at[4 * px + 2 * py + pc], ssem.at[k - 1], rsem.at[k - 1], (px, py, pc)).wait_recv()
        for cp in sends:
            cp.wait_send()
        loc.wait()

    return pl.pallas_call(
        body, name=name, out_shape=jax.ShapeDtypeStruct((8, r, 1024), F32),
        in_specs=[VM], out_specs=VM,
        scratch_shapes=[pltpu.SemaphoreType.DMA((7,)), pltpu.SemaphoreType.DMA((7,)), pltpu.SemaphoreType.DMA],
    )(p)


def gather_weights(w_in_b, w_rest_b, mod_sh):
    def body(wi_ref, wr_ref, m_ref, gi_ref, gr_ref, mo_ref, ssem, rsem, lsem):
        x, y, c = _pos()
        chip = 2 * x + y
        mine = pl.ds(pl.multiple_of(c * HROWS, 16), HROWS)
        other = pl.ds(pl.multiple_of((1 - c) * HROWS, 16), HROWS)
        sib = (x, y, 1 - c)
        pairs = ((wi_ref, gi_ref), (wr_ref, gr_ref))
        loc_m = pltpu.make_async_copy(m_ref, mo_ref.at[chip], lsem)
        loc_m.start()
        sends = []
        for k, (fx, fy) in enumerate(OTHER_CHIPS):
            peer = (_flip(x, fx), _flip(y, fy), c)
            for a, (w_ref, g_ref) in enumerate(pairs):
                cw = _rcopy(w_ref.at[mine], g_ref.at[chip, mine], ssem.at[6 * a + k], rsem.at[6 * a + k], peer)
                cw.start()
                sends.append(cw)
            cm = _rcopy(m_ref, mo_ref.at[chip], ssem.at[12 + k], rsem.at[12 + k], peer)
            cm.start()
            sends.append(cm)
        for k, (fx, fy) in enumerate(OTHER_CHIPS):
            px, py = _flip(x, fx), _flip(y, fy)
            for a, (w_ref, g_ref) in enumerate(pairs):
                got = g_ref.at[2 * px + py, mine]
                _rcopy(w_ref.at[mine], got, ssem.at[6 * a + k], rsem.at[6 * a + k], (px, py, c)).wait_recv()
                fw = _rcopy(got, got, ssem.at[6 * a + 3 + k], rsem.at[6 * a + 3 + k], sib)
                fw.start()
                sends.append(fw)
        for k, (fx, fy) in enumerate(OTHER_CHIPS):
            px, py = _flip(x, fx), _flip(y, fy)
            for a, (w_ref, g_ref) in enumerate(pairs):
                land = g_ref.at[2 * px + py, other]
                _rcopy(land, land, ssem.at[6 * a + 3 + k], rsem.at[6 * a + 3 + k], sib).wait_recv()
            _rcopy(m_ref, mo_ref.at[2 * px + py], ssem.at[12 + k], rsem.at[12 + k], (px, py, c)).wait_recv()
        for cp in sends:
            cp.wait_send()
        loc_m.wait()

    return pl.pallas_call(
        body, name="gather_weights",
        out_shape=[jax.ShapeDtypeStruct((4, D, SH_IN), BF16), jax.ShapeDtypeStruct((4, D, D), BF16),
                   jax.ShapeDtypeStruct((4, 8, 768), F32)],
        in_specs=[ANY, ANY, VM], out_specs=[ANY, ANY, VM],
        scratch_shapes=[pltpu.SemaphoreType.DMA((15,)), pltpu.SemaphoreType.DMA((15,)), pltpu.SemaphoreType.DMA],
    )(w_in_b, w_rest_b, mod_sh)


def pair_exchange(g_in, g_rest):
    def body(gi_ref, gr_ref, ri_ref, rr_ref, ssem, rsem):
        x, y, c = _pos()
        other = pl.ds(pl.multiple_of((1 - c) * HROWS, 8), HROWS)
        cps = [_rcopy(g_ref.at[:, other, :], r_ref, ssem.at[a], rsem.at[a], (x, y, 1 - c))
               for a, (g_ref, r_ref) in enumerate(((gi_ref, ri_ref), (gr_ref, rr_ref)))]
        for cp in cps:
            cp.start()
        for cp in cps:
            cp.wait()

    return pl.pallas_call(
        body, name="pair_exchange",
        out_shape=[jax.ShapeDtypeStruct((4, HROWS, SH_IN), F32), jax.ShapeDtypeStruct((4, HROWS, D), F32)],
        in_specs=[ANY, ANY], out_specs=[ANY, ANY],
        scratch_shapes=[pltpu.SemaphoreType.DMA((2,)), pltpu.SemaphoreType.DMA((2,))],
    )(g_in, g_rest)


HBM = pl.BlockSpec(memory_space=pltpu.HBM)
SEM = pl.BlockSpec(memory_space=pltpu.SEMAPHORE)
DATAFLOW = pltpu.SideEffectType.DATAFLOW_SIDE_EFFECTING


def _chip_copies(pi_ref, pr_ref, li_ref, lr_ref, ssem, rsem):
    x, y, c = _pos()
    copies = []
    for k, (fx, fy) in enumerate(OTHER_CHIPS):
        px, py = _flip(x, fx), _flip(y, fy)
        for a, (p_ref, l_ref) in enumerate(((pi_ref, li_ref), (pr_ref, lr_ref))):
            copies.append(_rcopy(p_ref.at[2 * px + py], l_ref.at[k], ssem.at[3 * a + k], rsem.at[3 * a + k], (px, py, c)))
    return copies


def chip_exchange_start(pb_in, pb_rest):
    def body(pi_ref, pr_ref, li_ref, lr_ref, ssem, rsem, pi_thru, pr_thru, li_thru, lr_thru, token):
        for cp in _chip_copies(pi_ref, pr_ref, li_ref, lr_ref, ssem, rsem):
            cp.start()
        token[...] = jnp.zeros_like(token)

    land_in = lax.empty((3, HROWS, SH_IN), BF16)
    land_rest = lax.empty((3, HROWS, D), BF16)
    hbm = lambda a: pltpu.HBM(a.shape, a.dtype)
    return pl.pallas_call(
        body, name="chip_exchange_start",
        out_shape=(pltpu.SemaphoreType.DMA((6,)), pltpu.SemaphoreType.DMA((6,)), hbm(pb_in), hbm(pb_rest),
                   hbm(land_in), hbm(land_rest), jax.ShapeDtypeStruct((8, 128), F32)),
        in_specs=(HBM, HBM, HBM, HBM), out_specs=(SEM, SEM, HBM, HBM, HBM, HBM, VM),
        input_output_aliases={0: 2, 1: 3, 2: 4, 3: 5},
        compiler_params=pltpu.CompilerParams(has_side_effects=DATAFLOW),
    )(*[pltpu.with_memory_space_constraint(a, pltpu.HBM) for a in (pb_in, pb_rest, land_in, land_rest)])


def chip_exchange_wait(ssem, rsem, pb_in, pb_rest, land_in, land_rest, after):
    def body(pi_ref, pr_ref, li_ref, lr_ref, ssem, rsem, after_ref, pi_dead, pr_dead, li_out, lr_out):
        for cp in _chip_copies(pi_ref, pr_ref, li_ref, lr_ref, ssem, rsem):
            cp.wait_send()
            cp.wait_recv()

    hbm = lambda a: pltpu.HBM(a.shape, a.dtype)
    return pl.pallas_call(
        body, name="chip_exchange_wait", out_shape=(hbm(pb_in), hbm(pb_rest), hbm(land_in), hbm(land_rest)),
        in_specs=(HBM, HBM, HBM, HBM, SEM, SEM, ANY), out_specs=(HBM, HBM, HBM, HBM),
        input_output_aliases={0: 0, 1: 1, 2: 2, 3: 3},
        compiler_params=pltpu.CompilerParams(has_side_effects=DATAFLOW),
    )(pb_in, pb_rest, land_in, land_rest, ssem, rsem, after)[2:]


def pair_swap(red_in, red_rest):
    def body(ai_ref, ar_ref, oi_ref, or_ref, ssem, rsem):
        x, y, c = _pos()
        cps = [_rcopy(a_ref, o_ref, ssem.at[a], rsem.at[a], (x, y, 1 - c))
               for a, (a_ref, o_ref) in enumerate(((ai_ref, oi_ref), (ar_ref, or_ref)))]
        for cp in cps:
            cp.start()
        for cp in cps:
            cp.wait()

    return pl.pallas_call(
        body, name="pair_swap",
        out_shape=[jax.ShapeDtypeStruct((HROWS, SH_IN), F32), jax.ShapeDtypeStruct((HROWS, D), F32)],
        in_specs=[ANY, ANY], out_specs=[ANY, ANY],
        scratch_shapes=[pltpu.SemaphoreType.DMA((2,)), pltpu.SemaphoreType.DMA((2,))],
    )(red_in, red_rest)


def _flat(v, width=1024):
    v = v.reshape(-1)
    n = -(-v.shape[0] // width) * width
    return jnp.pad(v, (0, n - v.shape[0]))


def _rows(parts, rows):
    flat = jnp.concatenate(parts)
    return jnp.pad(flat, (0, rows * 1024 - flat.shape[0])).reshape(rows, 1024)


def _pack_small(b_ada, norm_w, conv_b, ssm_norm_w, q_norm_w, k_norm_w, sinks, dt_bias, a_log, d_skip, rel_bias,
                extra=None, tail=(), rows=16):
    misc = [q_norm_w, k_norm_w, sinks, dt_bias, a_log, d_skip] + ([] if extra is None else [extra])
    parts = [_flat(b_ada), _flat(norm_w), _flat(conv_b), _flat(ssm_norm_w)] + [_flat(v, 128) for v in misc]
    parts.append(jnp.zeros(((8 - len(misc)) * 128,), F32))
    parts.append(_flat(rel_bias))
    parts.append(jnp.zeros((5 * 1024,), F32))
    return _rows(parts + [_flat(v) for v in tail], rows)


def _unpack_small(p):
    misc = p[9]
    return dict(b_ada=p[0:3].reshape(1, 3072), norm_w=p[3:4], conv_b=p[4:7].reshape(1, 3072),
                ssm_norm_w=p[7:9].reshape(1, 2048), q_norm_w=misc[None, 0:64], k_norm_w=misc[None, 128:192],
                sinks=misc[None, 256:272], dt_bias=misc[None, 384:416], a_log=misc[None, 512:544],
                d_skip=misc[None, 640:672], rel_bias=p[10, :512].reshape(32, 16), extra=misc[768])


SMALL = ("b_ada", "norm_w", "conv_b", "ssm_norm_w", "q_norm_w", "k_norm_w", "sinks", "dt_bias", "a_log", "d_skip",
         "rel_bias")
WEIGHTS = ("w_ada", "b_ada", "norm_w", "w_in", "q_norm_w", "k_norm_w", "rel_bias", "sinks", "conv_w", "conv_b",
           "dt_bias", "a_log", "d_skip", "ssm_norm_w", "w_attn_proj", "w_ssm_proj", "w_out")
IN_COLS = ((0, 1024, C_Q), (1024, 256, C_K), (1280, 256, C_V), (1536, 1024, C_ZA), (2560, 2048, C_ZM),
           (4608, 3072, C_XBC), (7680, 32, C_DT), (7712, 1024, C_GA), (8736, 1024, C_GB))


def _to_cat(shards):
    parts, pos = [], 0
    for o, n, cnew in sorted(IN_COLS, key=lambda e: e[2]):
        assert cnew == pos
        c0 = o
        while c0 < o + n:
            i = c0 // SH_IN
            c1 = min(o + n, (i + 1) * SH_IN)
            parts.append(shards[i][:, c0 - i * SH_IN:c1 - i * SH_IN])
            c0 = c1
        pos += n
    parts.append(jnp.zeros((D, NP - pos), shards.dtype))
    return jnp.concatenate(parts, axis=1)


def _from_cat(dw_pieces):
    starts = [subs[0][1] for _, subs in DPIECES]

    def cols(c0, c1):
        p = max(q for q in range(len(starts)) if starts[q] <= c0)
        return dw_pieces[p][:, c0 - starts[p]:c1 - starts[p]]

    shards = []
    for i in range(4):
        lo, hi = i * SH_IN, (i + 1) * SH_IN
        parts = []
        for o, n, cnew in IN_COLS:
            a, b = max(o, lo), min(o + n, hi)
            if a < b:
                parts.append(cols(cnew + a - o, cnew + b - o))
        shards.append(jnp.concatenate(parts, axis=1))
    return jnp.stack(shards)


def kernel(x, c, w_ada, b_ada, norm_w, w_in, q_norm_w, k_norm_w, rel_bias, sinks, conv_w, conv_b, dt_bias, a_log, d_skip, ssm_norm_w, w_attn_proj, w_ssm_proj, w_out, loss_target, m_w_ada, m_b_ada, m_norm_w, m_w_in, m_q_norm_w, m_k_norm_w, m_rel_bias, m_sinks, m_conv_w, m_conv_b, m_dt_bias, m_a_log, m_d_skip, m_ssm_norm_w, m_w_attn_proj, m_w_ssm_proj, m_w_out, v_w_ada, v_b_ada, v_norm_w, v_w_in, v_q_norm_w, v_k_norm_w, v_rel_bias, v_sinks, v_conv_w, v_conv_b, v_dt_bias, v_a_log, v_d_skip, v_ssm_norm_w, v_w_attn_proj, v_w_ssm_proj, v_w_out):
    args = dict(locals())
    xi, yi, ci = lax.axis_index("x"), lax.axis_index("y"), lax.axis_index("c")
    chip = 2 * xi + yi
    me = 4 * xi + 2 * yi + ci
    x2 = x[0]
    tgt = loss_target[0]

    pay = _rows([c.reshape(-1), conv_w[0].reshape(-1)], 8)
    g0 = allgather_small(pay, "gather_cond")
    c_all = g0[:, 0, :]
    conv_w_full = g0[0::2, 1:4, :].reshape(4, CONV_K, 768).transpose(1, 0, 2).reshape(CONV_K, XBC)

    b_ada_sh = lax.dynamic_slice(b_ada, (0, chip * 768), (1, 768))
    mod_sh = ada_mod(c_all, w_ada[0], b_ada_sh)

    w_in_b = w_in[0].astype(BF16)
    w_rest_b = jnp.concatenate([w_attn_proj[0], w_ssm_proj[0], w_out[0]], axis=0).astype(BF16)
    wg_in, wg_rest, modg = gather_weights(w_in_b, w_rest_b, mod_sh)
    wg_in = lax.dynamic_update_slice(wg_in, w_in_b[None], (chip, 0, 0))
    wg_rest = lax.dynamic_update_slice(wg_rest, w_rest_b[None], (chip, 0, 0))
    mod = lax.dynamic_slice(modg, (0, me, 0), (4, 1, 768)).reshape(1, 3 * D)
    shift, scale, gate = mod[:, :D], mod[:, D:2 * D], mod[:, 2 * D:]
    wcat = _to_cat(wg_in)
    w_at = wg_rest[:, :R_AT].reshape(D, D)
    w_ss = wg_rest[:, R_AT:R_AT + R_SS].reshape(SSM_W, D)
    w_ou = wg_rest[:, R_AT + R_SS:].reshape(D, D)

    pad128 = lambda v: jnp.pad(v, ((0, 0), (0, 128 - v.shape[1])))
    dtb_p, alog_p, dsk_p = pad128(dt_bias), pad128(a_log), pad128(d_skip)
    bucket = _bucket_table()

    proj, dt_raw, h_t = norm_proj(x2, norm_w, scale, shift, wcat)
    biasm = bias_expand(rel_bias, sinks, bucket)
    ao = attn_fwd(proj, biasm, q_norm_w, k_norm_w)
    act, dsl = conv_fwd(proj, conv_w_full, conv_b)
    yss, sprev = ssd_fwd(act, dt_raw, dtb_p, alog_p, dsk_p)

    (loss_p, dy, dao, dmid, dyss, ua_t, yn_t, mg_t, dya, dyb, dout, dgate, dssm_nw) = tail(
        proj, ao, yss, x2, tgt, gate, ssm_norm_w, w_at, w_ss, w_ou)

    dq, dkv, dqw, dkw, dacc = attn_bwd(proj, dao, biasm, q_norm_w, k_norm_w)
    dbias = bias_reduce(dacc, bucket)
    drb = dbias[:, :NBUCKET].T
    dsk = dbias[:, NBUCKET].reshape(1, HQ)
    dact, ddt, ddtb, dalog, ddskip = ssd_bwd(act, dt_raw, dyss, sprev, dtb_p, alog_p, dsk_p)
    dxbc, dconv_w, dconv_b = conv_bwd(proj, dact, dsl, conv_w_full)

    dproj = (dq, dmid, dxbc, dkv, ddt)
    dwcat = [wgrad(h_t, piece, "dw_in_%d" % p, min(piece.shape[1], 1024)) for p, piece in enumerate(dproj)]
    dw_at = wgrad(ua_t, dya, "dw_attn", 512)
    dw_ss = wgrad(yn_t, dyb, "dw_ssm", 512)
    dw_ou = wgrad(mg_t, dout, "dw_out", 512)

    g_in = _from_cat(dwcat)
    g_rest = jnp.concatenate([dw_at.reshape(4, R_AT, D), dw_ss.reshape(4, R_SS, D), dw_ou.reshape(4, R_OU, D)], axis=1)
    sib_in, sib_rest = pair_exchange(g_in, g_rest)
    part_in, pb_in = pair_sum(g_in, ci, sib_in, "pair_sum_in")
    part_rest, pb_rest = pair_sum(g_rest, ci, sib_rest, "pair_sum_rest")
    ssem, rsem, pb_in, pb_rest, land_in, land_rest, token = chip_exchange_start(pb_in, pb_rest)
    grad_x, dnorm_w, dscale, dshift = dproj_bwd(dproj, wcat, x2, dy, norm_w, scale + token[:1, :1])
    oth_in, oth_rest = chip_exchange_wait(ssem, rsem, pb_in, pb_rest, land_in, land_rest, dshift)
    red_in = chip_sum(part_in, chip, oth_in, "chip_sum_in")
    red_rest = chip_sum(part_rest, chip, oth_rest, "chip_sum_rest")
    recv_in, recv_rest = pair_swap(red_in, red_rest)
    both = lambda mine, theirs: jnp.concatenate([jnp.where(ci == 0, mine, theirs), jnp.where(ci == 0, theirs, mine)],
                                                axis=0)
    g_shard_in = both(red_in, recv_in)
    g_shard_rest = both(red_rest, recv_rest)

    dmod = jnp.concatenate([dshift, dscale, dgate], axis=1)
    gsmall = _pack_small(dmod, dnorm_w, dconv_b, dssm_nw, dqw, dkw, dsk[:, :HQ], ddtb[:, :SH], dalog[:, :SH],
                         ddskip[:, :SH], drb, extra=loss_p[:, :1], tail=(dconv_w,), rows=32)
    gall = allgather_small(gsmall, "gather_small_grads")
    ssum = sum_devices(gall)
    gs = _unpack_small(ssum[:16])
    loss = gs["extra"]
    dconv_w_sh = lax.dynamic_slice(ssum[16:28].reshape(CONV_K, XBC), (0, chip * 768), (CONV_K, 768))
    dmod_all = gall[:, 0:3, :].reshape(8, 3 * D)
    dw_ada = ada_grad(c_all, lax.dynamic_slice(dmod_all, (0, chip * 768), (8, 768)))

    grads = dict(gs)
    grads["w_ada"] = dw_ada
    grads["w_in"] = g_shard_in
    grads["w_attn_proj"] = g_shard_rest[:R_AT]
    grads["w_ssm_proj"] = g_shard_rest[R_AT:R_AT + R_SS]
    grads["w_out"] = g_shard_rest[R_AT + R_SS:]
    grads["conv_w"] = dconv_w_sh

    delta, new_m, new_v = {}, {}, {}
    for n in ("w_ada", "w_in", "conv_w", "w_attn_proj", "w_ssm_proj", "w_out"):
        delta[n], new_m[n], new_v[n] = adamw(args[n][0], grads[n], args["m_" + n][0], args["v_" + n][0], "adamw_" + n)
    ws = _pack_small(*[args[n] for n in SMALL])
    ms = _pack_small(*[args["m_" + n] for n in SMALL])
    vs = _pack_small(*[args["v_" + n] for n in SMALL])
    d_s, m_s, v_s = adamw(ws, ssum[:16], ms, vs, "adamw_small")
    d_s, m_s, v_s = _unpack_small(d_s), _unpack_small(m_s), _unpack_small(v_s)
    for n in SMALL:
        delta[n], new_m[n], new_v[n] = d_s[n], m_s[n], v_s[n]

    def shaped(n, a):
        return a.reshape(args[n].shape)

    outs = [loss, grad_x[None]]
    for table in (grads, delta, new_m, new_v):
        outs += [shaped(n, table[n]) for n in WEIGHTS]
    return tuple(outs)
```

```python
import functools
import math

import numpy as np
import jax
import jax.numpy as jnp
from jax import lax
from jax.experimental import pallas as pl
from jax.experimental.pallas import tpu as pltpu

F32 = jnp.float32
BF16 = jnp.bfloat16
MESH = pl.DeviceIdType.MESH

D = 1024
HQ, HKV, GRP, DH = 16, 4, 4, 64
BLK = 128
NBUCKET, MAXDIST = 32, 128
SSM_W, SH, SG, SR, SP, SN = 2048, 32, 4, 8, 64, 128
CONV_K = 4
XBC = SSM_W + 2 * SG * SN
IN_W = 9760
EPS = 1e-6
NEG = -1e30
SCALE = DH ** -0.5

C_Q, C_ZA, C_GA, C_GB, C_ZM, C_XBC, C_K, C_V, C_DT = 0, 1024, 2048, 3072, 4096, 6144, 9216, 9472, 9728
NP = 9984
TN = 1664
W_MID = C_XBC - C_ZA

SH_IN = IN_W // 4
R_AT, R_SS, R_OU = 256, 512, 256
HROWS = D // 2

ADAM_LR, ADAM_B1, ADAM_B2, ADAM_EPS, ADAM_WD, ADAM_STEP = 0.001, 0.9, 0.999, 1e-08, 0.01, 10

VMEM_LIMIT = 56 * 1024 * 1024


def _cp(sem=None):
    if sem is None:
        return pltpu.CompilerParams(vmem_limit_bytes=VMEM_LIMIT)
    return pltpu.CompilerParams(dimension_semantics=sem, vmem_limit_bytes=VMEM_LIMIT)


def _sig(x):
    return 0.5 * jnp.tanh(0.5 * x) + 0.5


def _dot(a, b):
    return jnp.dot(a, b, preferred_element_type=F32)


def _dot_nt(a, b):
    return lax.dot_general(a, b, (((1,), (1,)), ((), ())), preferred_element_type=F32)


def _dot_tn(a, b):
    return lax.dot_general(a, b, (((0,), (0,)), ((), ())), preferred_element_type=F32)


def _rsum(x):
    return jnp.sum(x, axis=-1, keepdims=True)


def _csum(x):
    return jnp.sum(x, axis=0, keepdims=True)


def _asum(x):
    return _csum(_rsum(x))


def _full(shape):
    nd = len(shape)
    return pl.BlockSpec(shape, lambda *_: (0,) * nd)


def ada_mod(c_all, w_ada_sh, b_ada_sh):
    def body(c_ref, w_ref, b_ref, o_ref):
        cv = c_ref[...]
        s = cv * _sig(cv)
        o_ref[...] = jnp.dot(s, w_ref[...], preferred_element_type=F32,
                             precision=lax.Precision.HIGHEST) + b_ref[...]

    n = w_ada_sh.shape[1]
    return pl.pallas_call(body, name="ada_mod", out_shape=jax.ShapeDtypeStruct((8, n), F32),
                          compiler_params=_cp())(c_all, w_ada_sh, b_ada_sh)


def ada_grad(c_all, dmod_sh):
    def body(c_ref, d_ref, o_ref):
        cv = c_ref[...]
        s = cv * _sig(cv)
        o_ref[...] = lax.dot_general(s, d_ref[...], (((0,), (0,)), ((), ())), preferred_element_type=F32,
                                     precision=lax.Precision.HIGHEST)

    n = dmod_sh.shape[1]
    return pl.pallas_call(body, name="ada_grad", out_shape=jax.ShapeDtypeStruct((D, n), F32),
                          compiler_params=_cp())(c_all, dmod_sh)


def norm_proj(x, norm_w, scale, shift, wcat):
    t = x.shape[0]
    tm = min(t, 1024)

    def body(x_ref, nw_ref, sc_ref, sh_ref, w_ref, p_ref, dt_ref, ht_ref, hs):
        @pl.when(pl.program_id(1) == 0)
        def _():
            xv = x_ref[...]
            r = lax.rsqrt(jnp.mean(xv * xv, axis=-1, keepdims=True) + EPS)
            h = (xv * r) * nw_ref[...]
            h = h * (1.0 + sc_ref[...]) + sh_ref[...]
            hs[...] = h.astype(BF16)
            ht_ref[...] = h.T.astype(BF16)

        p = _dot(hs[...], w_ref[...])
        p_ref[...] = p.astype(BF16)

        @pl.when(pl.program_id(1) == C_DT // TN)
        def _():
            dt_ref[...] = p[:, C_DT % TN:C_DT % TN + 128]

    vec = pl.BlockSpec((1, D), lambda i, j: (0, 0))
    return pl.pallas_call(
        body, name="norm_proj", grid=(t // tm, NP // TN),
        in_specs=[pl.BlockSpec((tm, D), lambda i, j: (i, 0)), vec, vec, vec,
                  pl.BlockSpec((D, TN), lambda i, j: (0, j))],
        out_specs=[pl.BlockSpec((tm, TN), lambda i, j: (i, j)), pl.BlockSpec((tm, 128), lambda i, j: (i, 0)),
                   pl.BlockSpec((D, tm), lambda i, j: (0, i))],
        out_shape=[jax.ShapeDtypeStruct((t, NP), BF16), jax.ShapeDtypeStruct((t, 128), F32),
                   jax.ShapeDtypeStruct((D, t), BF16)],
        scratch_shapes=[pltpu.VMEM((tm, D), BF16)],
        compiler_params=_cp(("parallel", "arbitrary")),
    )(x, norm_w, scale, shift, wcat)


def _bucket_table():
    qi = jnp.arange(BLK)[:, None]
    kj = jnp.arange(2 * BLK)[None, :]
    dist = qi + BLK - kj
    n = jnp.maximum(dist, 0)
    max_exact = NBUCKET // 2
    nf = jnp.maximum(n, 1).astype(F32)
    large = max_exact + (jnp.log(nf / max_exact) / math.log(MAXDIST / max_exact)
                         * (NBUCKET - max_exact)).astype(jnp.int32)
    large = jnp.minimum(large, NBUCKET - 1)
    bucket = jnp.where(n < max_exact, n, large).astype(jnp.int32)
    valid = (dist >= 0) & (dist < BLK)
    return jnp.where(valid, bucket, -1)


def bias_expand(rel_bias, sinks, bucket):
    def body(rb_ref, sk_ref, bk_ref, o_ref):
        hd = pl.program_id(0)
        bk = bk_ref[...]
        col = lax.broadcasted_iota(jnp.int32, (BLK, 2 * BLK), 1)

        def step(b, acc):
            return jnp.where(bk == b, rb_ref[b, hd], acc)

        acc = lax.fori_loop(0, NBUCKET, step, jnp.full((BLK, 2 * BLK), NEG, F32))
        acc = jnp.where(col == 0, sk_ref[0, hd], acc)
        o_ref[1, 0] = acc
        o_ref[0, 0] = jnp.where(jnp.logical_and(col > 0, col < BLK), NEG, acc)

    smem = pl.BlockSpec(memory_space=pltpu.SMEM)
    return pl.pallas_call(
        body, name="bias_expand", grid=(HQ,),
        in_specs=[smem, smem, _full((BLK, 2 * BLK))],
        out_specs=pl.BlockSpec((2, 1, BLK, 2 * BLK), lambda h: (0, h, 0, 0)),
        out_shape=jax.ShapeDtypeStruct((2, HQ, BLK, 2 * BLK), F32),
        compiler_params=_cp(("arbitrary",)),
    )(rel_bias, sinks, bucket)


def bias_reduce(dacc, bucket):
    col = jnp.arange(BLK * 2 * BLK, dtype=jnp.int32) % (2 * BLK)
    lane = jnp.arange(128, dtype=jnp.int32)[None, :]
    member = (bucket.reshape(-1)[:, None] == lane) | ((col[:, None] == 0) & (lane == NBUCKET))

    def body(d_ref, m_ref, o_ref):
        mm = m_ref[...]
        o_ref[...] = sum(_dot(part, mm) for part in _split3(d_ref[...]))

    return pl.pallas_call(body, name="bias_reduce", out_shape=jax.ShapeDtypeStruct((HQ, 128), F32),
                          compiler_params=_cp())(dacc.reshape(HQ, BLK * 2 * BLK), member.astype(BF16))


GQ = GRP * BLK


def _stack_heads(x, nh):
    return jnp.concatenate([x[:, DH * h:DH * (h + 1)] for h in range(nh)], axis=0)


def _unstack(xs, nh):
    rows = xs.shape[0] // nh
    return jnp.concatenate([xs[rows * h:rows * (h + 1)] for h in range(nh)], axis=1)


def _rms(x):
    return lax.rsqrt(jnp.mean(x * x, axis=-1, keepdims=True) + EPS)


def _stack_q(q, qw):
    qs = _stack_heads(q, HQ)
    r = _rms(qs)
    qhat = qs * r
    return qhat * qw, qhat, r


def _band_first(shape):
    return (lax.broadcasted_iota(jnp.int32, shape, 0) & (2 * BLK - 1)) == 0


def _stack_kv(kp, kc, vp, vc, kw):
    ks = _stack_heads(jnp.concatenate([kp, kc], axis=0), HKV)
    r = _rms(ks)
    khat = ks * r
    first = _band_first(ks.shape)
    kn = jnp.where(first, 0.0, khat * kw)
    v2 = jnp.where(first, 0.0, _stack_heads(jnp.concatenate([vp, vc], axis=0), HKV)).astype(BF16)
    return kn, khat, r, v2


def _softmax_rows(s):
    p = jnp.exp(s - jnp.max(s, axis=-1, keepdims=True))
    return p * (1.0 / _rsum(p))


def attn_fwd(proj, biasm, q_norm_w, k_norm_w):
    t = proj.shape[0]
    nb = t // BLK

    def body(q_ref, kc_ref, kp_ref, vc_ref, vp_ref, bm_ref, qw_ref, kw_ref, o_ref):
        f = lambda ref: ref[...].astype(F32)
        qn = _stack_q(f(q_ref), qw_ref[...])[0].astype(BF16)
        kn, _, _, v2 = _stack_kv(f(kp_ref), f(kc_ref), f(vp_ref), f(vc_ref), kw_ref[...])
        knb = kn.astype(BF16)
        s = jnp.concatenate([_dot_nt(qn[GQ * j:GQ * (j + 1)], knb[2 * BLK * j:2 * BLK * (j + 1)])
                             for j in range(HKV)], axis=0)
        pr = _softmax_rows(s * SCALE + bm_ref[0].reshape(HQ * BLK, 2 * BLK)).astype(BF16)
        o = jnp.concatenate([_dot(pr[GQ * j:GQ * (j + 1)], v2[2 * BLK * j:2 * BLK * (j + 1)])
                             for j in range(HKV)], axis=0)
        o_ref[...] = _unstack(o, HQ).astype(BF16)

    kblk, vblk = C_K // 256, C_V // 256
    prev = lambda n: jnp.maximum(n - 1, 0)
    return pl.pallas_call(
        body, name="attn_fwd", grid=(nb,),
        in_specs=[pl.BlockSpec((BLK, D), lambda n: (n, 0)),
                  pl.BlockSpec((BLK, 256), lambda n: (n, kblk)),
                  pl.BlockSpec((BLK, 256), lambda n: (prev(n), kblk)),
                  pl.BlockSpec((BLK, 256), lambda n: (n, vblk)),
                  pl.BlockSpec((BLK, 256), lambda n: (prev(n), vblk)),
                  pl.BlockSpec((1, HQ, BLK, 2 * BLK), lambda n: (jnp.minimum(n, 1), 0, 0, 0)),
                  _full((1, DH)), _full((1, DH))],
        out_specs=pl.BlockSpec((BLK, D), lambda n: (n, 0)),
        out_shape=jax.ShapeDtypeStruct((t, D), BF16),
        compiler_params=_cp(("parallel",)),
    )(proj, proj, proj, proj, proj, biasm, q_norm_w, k_norm_w)


def attn_bwd(proj, dao, biasm, q_norm_w, k_norm_w):
    t = proj.shape[0]
    nb = t // BLK
    kb = 2 * BLK

    def body(q_ref, kc_ref, kp_ref, vc_ref, vp_ref, do_ref, bm_ref, qw_ref, kw_ref,
             dq_ref, dkv_ref, dqw_ref, dkw_ref, dacc_ref, ck, cv, pk, pv, nk, nv):
        n = pl.program_id(0)

        @pl.when(n == 0)
        def _():
            for ref in (dqw_ref, dkw_ref, dacc_ref, ck, cv):
                ref[...] = jnp.zeros_like(ref)

        qw = qw_ref[...]
        kw = kw_ref[...]
        f = lambda ref: ref[...].astype(F32)
        kn, khat, rk, v2 = _stack_kv(f(kp_ref), f(kc_ref), f(vp_ref), f(vc_ref), kw)
        grp = lambda a, j: a[GQ * j:GQ * (j + 1)]
        band = lambda a, j: a[kb * j:kb * (j + 1)]

        @pl.when(n < nb)
        def _():
            qn, qhat, rq = _stack_q(f(q_ref), qw)
            qnb = qn.astype(BF16)
            knb = kn.astype(BF16)
            dos = _stack_heads(f(do_ref), HQ).astype(BF16)
            s = jnp.concatenate([_dot_nt(grp(qnb, j), band(knb, j)) for j in range(HKV)], axis=0)
            pr = _softmax_rows(s * SCALE + bm_ref[0].reshape(HQ * BLK, kb))
            dp = jnp.concatenate([_dot_nt(grp(dos, j), band(v2, j)) for j in range(HKV)], axis=0)
            ds = pr * (dp - _rsum(pr * dp))
            dacc_ref[...] += ds.reshape(HQ, BLK, kb)
            dsb = ds.astype(BF16)
            prb = pr.astype(BF16)
            dqn = jnp.concatenate([_dot(grp(dsb, j), band(knb, j)) for j in range(HKV)], axis=0) * SCALE
            dqhat = dqn * qw
            dq = rq * (dqhat - qhat * jnp.mean(dqhat * qhat, axis=-1, keepdims=True))
            dq_ref[...] = _unstack(dq, HQ).astype(BF16)
            dqw_ref[...] += _csum(dqn * qhat)
            first = _band_first((kb, DH))
            for j in range(HKV):
                rows = slice(BLK * j, BLK * (j + 1))
                dkn = jnp.where(first, 0.0, _dot_tn(grp(dsb, j), grp(qnb, j)) * SCALE)
                dvj = jnp.where(first, 0.0, _dot_tn(grp(prb, j), grp(dos, j)))
                pk[rows, :] = dkn[:BLK]
                nk[rows, :] = dkn[BLK:]
                pv[rows, :] = dvj[:BLK]
                nv[rows, :] = dvj[BLK:]

        @pl.when(n == nb)
        def _():
            for ref in (pk, pv, nk, nv):
                ref[...] = jnp.zeros_like(ref)

        khp = jnp.concatenate([khat[kb * j:kb * j + BLK] for j in range(HKV)], axis=0)
        rkp = jnp.concatenate([rk[kb * j:kb * j + BLK] for j in range(HKV)], axis=0)
        dkn = ck[...] + pk[...]
        dkhat = dkn * kw
        dk = rkp * (dkhat - khp * jnp.mean(dkhat * khp, axis=-1, keepdims=True))
        dkw_ref[...] += _csum(dkn * khp)
        dkv_ref[...] = jnp.concatenate([_unstack(dk, HKV), _unstack(cv[...] + pv[...], HKV)], axis=1).astype(BF16)
        ck[...] = nk[...]
        cv[...] = nv[...]

    kblk, vblk = C_K // 256, C_V // 256
    cur = lambda n: jnp.minimum(n, nb - 1)
    prev = lambda n: jnp.maximum(n - 1, 0)
    carry = pltpu.VMEM((HKV * BLK, DH), F32)
    return pl.pallas_call(
        body, name="attn_bwd", grid=(nb + 1,),
        in_specs=[pl.BlockSpec((BLK, D), lambda n: (cur(n), 0)),
                  pl.BlockSpec((BLK, 256), lambda n: (cur(n), kblk)), pl.BlockSpec((BLK, 256), lambda n: (prev(n), kblk)),
                  pl.BlockSpec((BLK, 256), lambda n: (cur(n), vblk)), pl.BlockSpec((BLK, 256), lambda n: (prev(n), vblk)),
                  pl.BlockSpec((BLK, D), lambda n: (cur(n), 0)),
                  pl.BlockSpec((1, HQ, BLK, kb), lambda n: (jnp.minimum(n, 1), 0, 0, 0)),
                  _full((1, DH)), _full((1, DH))],
        out_specs=[pl.BlockSpec((BLK, D), lambda n: (cur(n), 0)),
                   pl.BlockSpec((BLK, 512), lambda n: (prev(n), 0)),
                   _full((1, DH)), _full((1, DH)), _full((HQ, BLK, kb))],
        out_shape=[jax.ShapeDtypeStruct((t, D), BF16), jax.ShapeDtypeStruct((t, 512), BF16),
                   jax.ShapeDtypeStruct((1, DH), F32),
                   jax.ShapeDtypeStruct((1, DH), F32), jax.ShapeDtypeStruct((HQ, BLK, kb), F32)],
        scratch_shapes=[carry] * 6,
        compiler_params=_cp(("arbitrary",)),
    )(proj, proj, proj, proj, proj, dao, biasm, q_norm_w, k_norm_w)


CONV_TM, CONV_CW, CONV_RC, HALO = 512, 512, 32, 16


def conv_fwd(proj, conv_w, conv_b):
    t = proj.shape[0]
    tm = min(t, CONV_TM)
    c0 = C_XBC // CONV_CW

    def body(x_ref, xp_ref, w_ref, b_ref, o_ref, ds_ref):
        i = pl.program_id(1)
        w = w_ref[...]
        b = b_ref[...]
        for r in range(tm // CONV_RC):
            lo = r * CONV_RC
            if r == 0:
                head = jnp.where(i == 0, 0.0, xp_ref[...].astype(F32))
                win = jnp.concatenate([head, x_ref[0:CONV_RC, :].astype(F32)], axis=0)
            else:
                win = x_ref[lo - HALO:lo + CONV_RC, :].astype(F32)
            acc = b
            for j in range(CONV_K):
                acc = acc + w[j:j + 1] * win[HALO - 3 + j:HALO - 3 + j + CONV_RC]
            sg = _sig(acc)
            o_ref[lo:lo + CONV_RC, :] = acc * sg
            ds_ref[lo:lo + CONV_RC, :] = _dsilu(acc, sg).astype(BF16)

    rh = tm // HALO
    tile = pl.BlockSpec((tm, CONV_CW), lambda s, i: (i, s))
    return pl.pallas_call(
        body, name="conv_fwd", grid=(XBC // CONV_CW, t // tm),
        in_specs=[pl.BlockSpec((tm, CONV_CW), lambda s, i: (i, c0 + s)),
                  pl.BlockSpec((HALO, CONV_CW), lambda s, i: (jnp.maximum(i * rh - 1, 0), c0 + s)),
                  pl.BlockSpec((CONV_K, CONV_CW), lambda s, i: (0, s)), pl.BlockSpec((1, CONV_CW), lambda s, i: (0, s))],
        out_specs=[tile, tile],
        out_shape=[jax.ShapeDtypeStruct((t, XBC), F32), jax.ShapeDtypeStruct((t, XBC), BF16)],
        compiler_params=_cp(("parallel", "parallel")),
    )(proj, proj, conv_w, conv_b)


def conv_bwd(proj, dact, dsl, conv_w):
    t = proj.shape[0]
    tm = min(t, CONV_TM)
    nt = t // tm
    nr = tm // CONV_RC
    c0 = C_XBC // CONV_CW
    ext = CONV_RC + 8

    def body(x_ref, xp_ref, d_ref, dn_ref, s_ref, sn_ref, w_ref, dx_ref, dw_ref, db_ref):
        i = pl.program_id(1)

        @pl.when(i == 0)
        def _():
            dw_ref[...] = jnp.zeros_like(dw_ref)
            db_ref[...] = jnp.zeros_like(db_ref)

        w = w_ref[...]
        dws = [jnp.zeros((1, CONV_CW), F32) for _ in range(CONV_K)]
        db = jnp.zeros((1, CONV_CW), F32)
        for r in range(nr):
            lo = r * CONV_RC
            if r == 0:
                head = jnp.where(i == 0, 0.0, xp_ref[...].astype(F32))
                win = jnp.concatenate([head, x_ref[0:CONV_RC, :].astype(F32)], axis=0)
            else:
                win = x_ref[lo - HALO:lo + CONV_RC, :].astype(F32)
            if r < nr - 1:
                dext = d_ref[lo:lo + ext, :]
                sext = s_ref[lo:lo + CONV_RC + HALO, :].astype(F32)[0:ext]
            else:
                dext = jnp.concatenate([d_ref[lo:lo + CONV_RC, :], jnp.where(i == nt - 1, 0.0, dn_ref[...])], axis=0)
                sext = jnp.concatenate([s_ref[lo:lo + CONV_RC, :].astype(F32), sn_ref[...].astype(F32)], axis=0)[0:ext]
            dpre = dext * sext
            dx = jnp.zeros((CONV_RC, CONV_CW), F32)
            own = dpre[0:CONV_RC]
            for j in range(CONV_K):
                dx = dx + w[j:j + 1] * dpre[3 - j:3 - j + CONV_RC]
                dws[j] = dws[j] + _csum(own * win[HALO - 3 + j:HALO - 3 + j + CONV_RC])
            db = db + _csum(own)
            dx_ref[lo:lo + CONV_RC, :] = dx.astype(BF16)
        dw_ref[...] += jnp.concatenate(dws, axis=0)
        db_ref[...] += db

    rh = tm // HALO
    r8 = tm // 8
    nxt = lambda i, per: jnp.minimum((i + 1) * per, nt * per - 1)
    return pl.pallas_call(
        body, name="conv_bwd", grid=(XBC // CONV_CW, nt),
        in_specs=[pl.BlockSpec((tm, CONV_CW), lambda s, i: (i, c0 + s)),
                  pl.BlockSpec((HALO, CONV_CW), lambda s, i: (jnp.maximum(i * rh - 1, 0), c0 + s)),
                  pl.BlockSpec((tm, CONV_CW), lambda s, i: (i, s)),
                  pl.BlockSpec((8, CONV_CW), lambda s, i: (nxt(i, r8), s)),
                  pl.BlockSpec((tm, CONV_CW), lambda s, i: (i, s)),
                  pl.BlockSpec((HALO, CONV_CW), lambda s, i: (nxt(i, rh), s)),
                  pl.BlockSpec((CONV_K, CONV_CW), lambda s, i: (0, s))],
        out_specs=[pl.BlockSpec((tm, CONV_CW), lambda s, i: (i, s)),
                   pl.BlockSpec((CONV_K, CONV_CW), lambda s, i: (0, s)), pl.BlockSpec((1, CONV_CW), lambda s, i: (0, s))],
        out_shape=[jax.ShapeDtypeStruct((t, XBC), BF16), jax.ShapeDtypeStruct((CONV_K, XBC), F32),
                   jax.ShapeDtypeStruct((1, XBC), F32)],
        compiler_params=_cp(("parallel", "arbitrary")),
    )(proj, proj, dact, dact, dsl, dsl, conv_w)


def _split3(x):
    h = x.astype(BF16)
    r = x - h.astype(F32)
    m = r.astype(BF16)
    lo = (r - m.astype(F32)).astype(BF16)
    return h, m, lo


def _tri_mm(tri, x):
    h, m, lo = _split3(x)
    return _dot(tri, h) + _dot(tri, m) + _dot(tri, lo)


def _softplus(x):
    return jnp.maximum(x, 0.0) + jnp.log1p(jnp.exp(-jnp.abs(x)))


def _chunk_decays(dt_raw, dtb, alog):
    dtv = _softplus(dt_raw + dtb)
    a = -jnp.exp(alog)
    ri = lax.broadcasted_iota(jnp.int32, (BLK, BLK), 0)
    ci = lax.broadcasted_iota(jnp.int32, (BLK, BLK), 1)
    causal = ri >= ci
    acum = _tri_mm(causal.astype(BF16), dtv * a)
    return dtv, a, causal, acum, acum.T


NPAIR = SH // 2


def _pairs(x):
    return jnp.stack([x[:, 128 * k:128 * (k + 1)] for k in range(NPAIR)])


def _unpairs(x3):
    return jnp.concatenate([x3[k] for k in range(NPAIR)], axis=1)


def _per_head_cols(m):
    return jnp.stack([jnp.broadcast_to(m[:, h:h + 1], m.shape) for h in range(SH)])


def _pair_lanes(t):
    r = t.reshape(NPAIR, 2, t.shape[1], 128)
    lo = lax.broadcasted_iota(jnp.int32, (1, t.shape[1], 128), 2) < SP
    return jnp.where(lo, r[:, 0], r[:, 1])


class _Chunk:
    pass


def _chunk_common(dt_raw, dtb, alog, dskip):
    cm = _Chunk()
    cm.dtv, cm.a, cm.causal, acum, acum_t = _chunk_decays(dt_raw, dtb, alog)
    cm.acol = _per_head_cols(acum)
    cm.arow = jnp.stack([acum_t[h:h + 1, :] for h in range(SH)])
    cm.lam = jnp.exp(jnp.where(cm.causal[None], cm.acol - cm.arow, NEG))
    apl = _pair_lanes(cm.acol)
    alast = apl[:, BLK - 1:BLK, :]
    cm.dpl = _pair_lanes(_per_head_cols(cm.dtv))
    cm.eapl = jnp.exp(apl)
    cm.epl = jnp.exp(alast - apl)
    cm.cdpl = jnp.exp(alast)
    cm.dskpl = _pair_lanes(_per_head_cols(dskip))
    cm.lo = lax.broadcasted_iota(jnp.int32, (1, BLK, 128), 2) < SP
    return cm


def ssd_fwd(act, dt_raw, dtb_p, alog_p, dsk_p):
    t = act.shape[0]
    nc = t // BLK

    def body(xs_ref, b_ref, c_ref, dt_ref, dtb_ref, al_ref, dk_ref, y_ref, sp_ref, st):
        c = pl.program_id(0)

        @pl.when(c == 0)
        def _():
            st[...] = jnp.zeros_like(st)

        s_t = st[...]
        sp_ref[0] = s_t
        cm = _chunk_common(dt_ref[...], dtb_ref[...], al_ref[...], dk_ref[...])
        gms, cbs, bts = [], [], []
        for g in range(SG):
            bf = b_ref[:, SN * g:SN * (g + 1)]
            cb = c_ref[:, SN * g:SN * (g + 1)].astype(BF16)
            gms.append(_dot_nt(cb, bf.astype(BF16)))
            cbs.append(cb)
            bts.append(bf.T.astype(BF16))
        m = (cm.lam.reshape(SG, SR, BLK, BLK) * jnp.stack(gms)[:, None]).reshape(SH, BLK, BLK).astype(BF16)
        xs16 = _pairs(xs_ref[...])
        xdt16 = xs16 * cm.dpl
        x_lo = jnp.where(cm.lo, xdt16, 0.0).astype(BF16)
        x_hi = jnp.where(cm.lo, 0.0, xdt16).astype(BF16)
        s16 = _pairs(s_t)
        s16b = s16.astype(BF16)
        yd = jnp.stack([_dot(m[2 * k], x_lo[k]) + _dot(m[2 * k + 1], x_hi[k]) for k in range(NPAIR)])
        yo = jnp.stack([_dot(cbs[k // (NPAIR // SG)], s16b[k]) for k in range(NPAIR)])
        y_ref[...] = _unpairs(yd + yo * cm.eapl + cm.dskpl * xs16).astype(BF16)
        xe = (xdt16 * cm.epl).astype(BF16)
        st[...] = _unpairs(cm.cdpl * s16 + jnp.stack([_dot(bts[k // (NPAIR // SG)], xe[k]) for k in range(NPAIR)]))

    vec = _full((1, 128))
    return pl.pallas_call(
        body, name="ssd_fwd", grid=(nc,),
        in_specs=[pl.BlockSpec((BLK, SSM_W), lambda c: (c, 0)),
                  pl.BlockSpec((BLK, SG * SN), lambda c: (c, SSM_W // (SG * SN))),
                  pl.BlockSpec((BLK, SG * SN), lambda c: (c, SSM_W // (SG * SN) + 1)),
                  pl.BlockSpec((BLK, 128), lambda c: (c, 0)), vec, vec, vec],
        out_specs=[pl.BlockSpec((BLK, SSM_W), lambda c: (c, 0)), pl.BlockSpec((1, SN, SSM_W), lambda c: (c, 0, 0))],
        out_shape=[jax.ShapeDtypeStruct((t, SSM_W), BF16), jax.ShapeDtypeStruct((nc, SN, SSM_W), F32)],
        scratch_shapes=[pltpu.VMEM((SN, SSM_W), F32)],
        compiler_params=_cp(("arbitrary",)),
    )(act, act, act, dt_raw, dtb_p, alog_p, dsk_p)


def _head_sums(q):
    r = q.shape[1]
    lo = lax.broadcasted_iota(jnp.int32, (1, r, 128), 2) < SP
    s_lo = jnp.sum(jnp.where(lo, q, 0.0), axis=-1, keepdims=True)
    s_hi = jnp.sum(jnp.where(lo, 0.0, q), axis=-1, keepdims=True)
    lane = lax.broadcasted_iota(jnp.int32, (r, 128), 1)
    out = jnp.zeros((r, 128), F32)
    for k in range(NPAIR):
        out = jnp.where(lane == 2 * k, s_lo[k], jnp.where(lane == 2 * k + 1, s_hi[k], out))
    return out


def ssd_bwd(act, dt_raw, dy, sprev, dtb_p, alog_p, dsk_p):
    t = act.shape[0]
    nc = t // BLK

    def body(xs_ref, b_ref, c_ref, dt_ref, dy_ref, sp_ref, dtb_ref, al_ref, dk_ref,
             da_ref, ddt_ref, ddtb_ref, dal_ref, ddk_ref, dst):
        i = pl.program_id(0)

        @pl.when(i == 0)
        def _():
            dst[...] = jnp.zeros_like(dst)
            ddtb_ref[...] = jnp.zeros_like(ddtb_ref)
            dal_ref[...] = jnp.zeros_like(dal_ref)
            ddk_ref[...] = jnp.zeros_like(ddk_ref)

        dt_raw = dt_ref[...]
        dtb = dtb_ref[...]
        cm = _chunk_common(dt_raw, dtb, al_ref[...], dk_ref[...])
        ri = lax.broadcasted_iota(jnp.int32, (BLK, BLK), 0)
        ci = lax.broadcasted_iota(jnp.int32, (BLK, BLK), 1)
        lam_t = jnp.exp(jnp.where((ri <= ci)[None], cm.arow - cm.acol, NEG))
        bbs, cbs, cts, gms = [], [], [], []
        for g in range(SG):
            bf = b_ref[:, SN * g:SN * (g + 1)]
            cf = c_ref[:, SN * g:SN * (g + 1)]
            bbs.append(bf.astype(BF16))
            cbs.append(cf.astype(BF16))
            cts.append(cf.T.astype(BF16))
            gms.append(_dot_nt(bbs[g], cbs[g]))
        grp = lambda k: k // (NPAIR // SG)
        xs16 = _pairs(xs_ref[...])
        dy16 = _pairs(dy_ref[...].astype(F32))
        sp16 = _pairs(sp_ref[0])
        ds16 = _pairs(dst[...])
        xdt16 = xs16 * cm.dpl
        xdtb = xdt16.astype(BF16)
        dyh = [jnp.where(cm.lo, dy16, 0.0).astype(BF16), jnp.where(cm.lo, 0.0, dy16).astype(BF16)]
        m_t = (lam_t.reshape(SG, SR, BLK, BLK) * jnp.stack(gms)[:, None]).reshape(SH, BLK, BLK).astype(BF16)
        dxdt = jnp.stack([_dot(m_t[2 * k], dyh[0][k]) + _dot(m_t[2 * k + 1], dyh[1][k]) for k in range(NPAIR)])
        dm = jnp.stack([_dot_nt(dyh[h % 2][h // 2], xdtb[h // 2]) for h in range(SH)])
        dgl = (dm * cm.lam).reshape(SG, SR, BLK, BLK)
        dg = jnp.sum(dgl, axis=1).astype(BF16)
        w = (dgl * jnp.stack([_dot_nt(cbs[g], bbs[g]) for g in range(SG)])[:, None]).reshape(SH, BLK, BLK)
        w_rows = jnp.sum(w, axis=2, keepdims=True)
        w_cols = jnp.concatenate([jnp.sum(w, axis=1)] + [jnp.zeros((128 - SH, BLK), F32)], axis=0).T
        lane_c = lax.broadcasted_iota(jnp.int32, (BLK, 128), 1)
        da_cols = -w_cols
        for h in range(SH):
            da_cols = jnp.where(lane_c == h, da_cols + w_rows[h], da_cols)
        ds16b = ds16.astype(BF16)
        sp16b = sp16.astype(BF16)
        dxs = jnp.stack([_dot(bbs[grp(k)], ds16b[k]) for k in range(NPAIR)]) * cm.epl
        dxdt = dxdt + dxs
        dya = (dy16 * cm.eapl).astype(BF16)
        xe = (xdt16 * cm.epl).astype(BF16)
        dcs, dbs = [], []
        for g in range(SG):
            ks = range(g * (NPAIR // SG), (g + 1) * (NPAIR // SG))
            dcs.append(sum(_dot_nt(dya[k], sp16b[k]) for k in ks) + _dot(dg[g], bbs[g]))
            dbs.append(sum(_dot_nt(xe[k], ds16b[k]) for k in ks) + _dot_tn(dg[g], cbs[g]))
        dst[...] = _unpairs(cm.cdpl * ds16 + jnp.stack([_dot(cts[grp(k)], dya[k]) for k in range(NPAIR)]))
        da_ref[...] = jnp.concatenate([_unpairs(dxdt * cm.dpl + cm.dskpl * dy16)] + dbs + dcs, axis=1)
        y_off = jnp.stack([_dot(cbs[grp(k)], sp16b[k]) for k in range(NPAIR)]) * cm.eapl
        da_cols = da_cols + _head_sums(dy16 * y_off - xdt16 * dxs)
        last = _head_sums(jnp.sum(xdt16 * dxs, axis=1, keepdims=True)
                          + cm.cdpl * jnp.sum(ds16 * sp16, axis=1, keepdims=True))
        ddt = _head_sums(dxdt * xs16)
        row_i = lax.broadcasted_iota(jnp.int32, (BLK, 128), 0)
        dacum = da_cols + jnp.where(row_i == BLK - 1, last, 0.0)
        dda = _tri_mm((ri <= ci).astype(BF16), dacum)
        ddt = ddt + dda * cm.a
        dal_ref[...] += _csum(dda * cm.dtv) * cm.a
        ddt_raw = jnp.where(lane_c < SH, ddt * _sig(dt_raw + dtb), 0.0)
        ddt_ref[...] = ddt_raw.astype(BF16)
        ddtb_ref[...] += _csum(ddt_raw)
        ddk_ref[...] += _head_sums(jnp.sum(dy16 * xs16, axis=1, keepdims=True))

    rev = lambda i: nc - 1 - i
    vec = _full((1, 128))
    slab = pl.BlockSpec((BLK, SSM_W), lambda i: (rev(i), 0))
    return pl.pallas_call(
        body, name="ssd_bwd", grid=(nc,),
        in_specs=[slab,
                  pl.BlockSpec((BLK, SG * SN), lambda i: (rev(i), SSM_W // (SG * SN))),
                  pl.BlockSpec((BLK, SG * SN), lambda i: (rev(i), SSM_W // (SG * SN) + 1)),
                  pl.BlockSpec((BLK, 128), lambda i: (rev(i), 0)),
                  slab,
                  pl.BlockSpec((1, SN, SSM_W), lambda i: (rev(i), 0, 0)), vec, vec, vec],
        out_specs=[pl.BlockSpec((BLK, XBC), lambda i: (rev(i), 0)), pl.BlockSpec((BLK, 128), lambda i: (rev(i), 0)),
                   vec, vec, vec],
        out_shape=[jax.ShapeDtypeStruct((t, XBC), F32), jax.ShapeDtypeStruct((t, 128), BF16),
                   jax.ShapeDtypeStruct((1, 128), F32), jax.ShapeDtypeStruct((1, 128), F32),
                   jax.ShapeDtypeStruct((1, 128), F32)],
        scratch_shapes=[pltpu.VMEM((SN, SSM_W), F32)],
        compiler_params=_cp(("arbitrary",)),
    )(act, act, act, dt_raw, dy, sprev, dtb_p, alog_p, dsk_p)


TAIL_TM = 256


def _dsilu(z, s):
    return s * (1.0 + z * (1.0 - s))


def tail(proj, ao, yss, x, target, gate, ssm_nw, w_at, w_ss, w_ou):
    t = x.shape[0]
    tm = min(t, TAIL_TM)
    gw = SSM_W // SG

    def body(ao_ref, za_ref, ga_ref, gb_ref, zm_ref, ys_ref, x_ref, tg_ref, gt_ref, nw_ref, wa_ref, ws_ref, wo_ref,
             loss_ref, dy_ref, dao_ref, dmid_ref, dys_ref,
             ua_ref, yn_ref, mg_ref, dya_ref, dyb_ref, do_ref, dgt_ref, dnw_ref):
        i = pl.program_id(0)

        @pl.when(i == 0)
        def _():
            loss_ref[...] = jnp.zeros_like(loss_ref)
            dgt_ref[...] = jnp.zeros_like(dgt_ref)
            dnw_ref[...] = jnp.zeros_like(dnw_ref)

        ao = ao_ref[...].astype(F32)
        za = za_ref[...].astype(F32)
        sa = _sig(za)
        sila = za * sa
        ua_f = ao * sila
        ua = ua_f.astype(BF16)
        ya = _dot(ua, wa_ref[...])
        zm = zm_ref[...].astype(F32)
        sm = _sig(zm)
        silm = zm * sm
        ys = ys_ref[...].astype(F32)
        u = ys * silm
        nw = nw_ref[...]
        rs, uns = [], []
        for g in range(SG):
            ug = u[:, gw * g:gw * (g + 1)]
            r = lax.rsqrt(jnp.mean(ug * ug, axis=-1, keepdims=True) + EPS)
            rs.append(r)
            uns.append(ug * r)
        un = jnp.concatenate(uns, axis=1)
        yn_f = un * nw
        yn = yn_f.astype(BF16)
        yb = _dot(yn, ws_ref[...])
        sga = _sig(ga_ref[...].astype(F32))
        sgb = _sig(gb_ref[...].astype(F32))
        mg_f = sga * ya + sgb * yb
        mg = mg_f.astype(BF16)
        o = _dot(mg, wo_ref[...])
        gt = gt_ref[...]
        err = (x_ref[...] + gt * o) - tg_ref[...]
        lane = lax.broadcasted_iota(jnp.int32, (1, 128), 1)
        loss_ref[...] += jnp.where(lane == 0, 0.5 * _asum(_rsum(err * err) / D), 0.0)
        dy = err * (1.0 / D)
        dy_ref[...] = dy
        dgt_ref[...] += _csum(dy * o)
        do = (dy * gt).astype(BF16)
        dmg = _dot_nt(do, wo_ref[...])
        dmid_ref[:, C_GA - C_ZA:C_GB - C_ZA] = (dmg * ya * sga * (1.0 - sga)).astype(BF16)
        dmid_ref[:, C_GB - C_ZA:C_ZM - C_ZA] = (dmg * yb * sgb * (1.0 - sgb)).astype(BF16)
        dya = (dmg * sga).astype(BF16)
        dyb = (dmg * sgb).astype(BF16)
        dua = _dot_nt(dya, wa_ref[...])
        dao_ref[...] = (dua * sila).astype(BF16)
        dmid_ref[:, 0:C_GA - C_ZA] = (dua * ao * _dsilu(za, sa)).astype(BF16)
        dyn = _dot_nt(dyb, ws_ref[...])
        dnw_ref[...] += _csum(dyn * un)
        dun = dyn * nw
        dus = []
        for g in range(SG):
            gs = slice(gw * g, gw * (g + 1))
            dus.append(rs[g] * (dun[:, gs] - uns[g] * jnp.mean(dun[:, gs] * uns[g], axis=-1, keepdims=True)))
        du = jnp.concatenate(dus, axis=1)
        dys_ref[...] = (du * silm).astype(BF16)
        dmid_ref[:, C_ZM - C_ZA:] = (du * ys * _dsilu(zm, sm)).astype(BF16)
        ua_ref[...] = ua_f.T.astype(BF16)
        yn_ref[...] = yn_f.T.astype(BF16)
        mg_ref[...] = mg_f.T.astype(BF16)
        dya_ref[...] = dya
        dyb_ref[...] = dyb
        do_ref[...] = do

    row = lambda w: pl.BlockSpec((tm, w), lambda i: (i, 0))
    pcol = lambda w, c0: pl.BlockSpec((tm, w), lambda i: (i, c0 // w))
    sd = lambda w, dt: jax.ShapeDtypeStruct((t, w), dt)
    colt = lambda w: pl.BlockSpec((w, tm), lambda i: (0, i))
    sdt = lambda w: jax.ShapeDtypeStruct((w, t), BF16)
    return pl.pallas_call(
        body, name="tail", grid=(t // tm,),
        in_specs=[row(D), pcol(D, C_ZA), pcol(D, C_GA), pcol(D, C_GB), pcol(SSM_W, C_ZM), row(SSM_W), row(D), row(D),
                  _full((1, D)), _full((1, SSM_W)), _full((D, D)), _full((SSM_W, D)), _full((D, D))],
        out_specs=[_full((1, 128)), row(D), row(D), row(W_MID), row(SSM_W),
                   colt(D), colt(SSM_W), colt(D), row(D), row(D), row(D), _full((1, D)), _full((1, SSM_W))],
        out_shape=[jax.ShapeDtypeStruct((1, 128), F32), sd(D, F32), sd(D, BF16), sd(W_MID, BF16),
                   sd(SSM_W, BF16), sdt(D), sdt(SSM_W), sdt(D), sd(D, BF16),
                   sd(D, BF16), sd(D, BF16), jax.ShapeDtypeStruct((1, D), F32), jax.ShapeDtypeStruct((1, SSM_W), F32)],
        compiler_params=_cp(("arbitrary",)),
    )(ao, proj, proj, proj, proj, yss, x, target, gate, ssm_nw, w_at, w_ss, w_ou)


DPIECES = ((D, ((D, C_Q),)),
           (W_MID, ((D, C_ZA), (D, C_GA), (D, C_GB), (SSM_W, C_ZM))),
           (XBC, ((XBC, C_XBC),)),
           (512, ((512, C_K),)),
           (128, ((128, C_DT),)))


def dproj_bwd(pieces, wcat, x, dy, norm_w, scale):
    t = x.shape[0]
    tm = min(t, 256)
    nt = t // tm
    wblocks = [blk for _, subs in DPIECES for blk in subs]
    npc, nwb = len(DPIECES), len(wblocks)

    def body(*refs):
        p_refs, w_refs = refs[:npc], refs[npc:npc + nwb]
        x_ref, dy_ref, nw_ref, sc_ref, gx_ref, dnw_ref, dsc_ref, dsh_ref, dwe_ref = refs[npc + nwb:]
        i = pl.program_id(0)

        @pl.when(i == 0)
        def _():
            for ref in (dwe_ref, dsh_ref, dnw_ref, dsc_ref):
                ref[...] = jnp.zeros_like(ref)

        dh, wi = None, 0
        for p_ref, (_, subs) in zip(p_refs, DPIECES):
            loc = 0
            for w, _ in subs:
                part = _dot_nt(p_ref[:, loc:loc + w], w_refs[wi][...])
                dh = part if dh is None else dh + part
                loc += w
                wi += 1
        xv = x_ref[...]
        r = lax.rsqrt(jnp.mean(xv * xv, axis=-1, keepdims=True) + EPS)
        xn = xv * r
        weff = nw_ref[...] * (1.0 + sc_ref[...])
        dxn = dh * weff
        gx_ref[...] = dy_ref[...] + r * (dxn - xn * jnp.mean(dxn * xn, axis=-1, keepdims=True))
        dwe_ref[...] += _csum(dh * xn)
        dsh_ref[...] += _csum(dh)

        @pl.when(i == nt - 1)
        def _():
            dwe = dwe_ref[...]
            dnw_ref[...] = dwe * (1.0 + sc_ref[...])
            dsc_ref[...] = dwe * nw_ref[...]

    vec = pl.BlockSpec((1, D), lambda i: (0, 0))
    row = pl.BlockSpec((tm, D), lambda i: (i, 0))
    return pl.pallas_call(
        body, name="dproj_bwd", grid=(nt,),
        in_specs=[pl.BlockSpec((tm, pw), lambda i: (i, 0)) for pw, _ in DPIECES]
        + [pl.BlockSpec((D, w), functools.partial(lambda i, b: (0, b), b=off // w), pipeline_mode=pl.Buffered(1))
           for w, off in wblocks]
        + [row, row, vec, vec],
        out_specs=[row, vec, vec, vec],
        out_shape=[jax.ShapeDtypeStruct((t, D), F32), jax.ShapeDtypeStruct((1, D), F32),
                   jax.ShapeDtypeStruct((1, D), F32), jax.ShapeDtypeStruct((1, D), F32)],
        scratch_shapes=[pltpu.VMEM((1, D), F32)],
        compiler_params=_cp(("arbitrary",)),
    )(*pieces, *([wcat] * nwb), x, dy, norm_w, scale)


def wgrad(at, b, name, bn, after):
    m, t = at.shape
    n = b.shape[1]
    tk = min(t, 1024)
    bm = min(m, 1024)

    def body(a_ref, b_ref, after_ref, o_ref):
        part = _dot(a_ref[...], b_ref[...])

        @pl.when(pl.program_id(2) == 0)
        def _():
            o_ref[...] = part

        @pl.when(pl.program_id(2) > 0)
        def _():
            o_ref[...] += part

    return pl.pallas_call(
        body, name=name, grid=(m // bm, n // bn, t // tk),
        in_specs=[pl.BlockSpec((bm, tk), lambda i, j, k: (i, k)), pl.BlockSpec((tk, bn), lambda i, j, k: (k, j)), ANY],
        out_specs=pl.BlockSpec((bm, bn), lambda i, j, k: (i, j)),
        out_shape=jax.ShapeDtypeStruct((m, n), F32),
        compiler_params=_cp(("parallel", "parallel", "arbitrary")),
    )(at, b, after)


SUM_TR = 256


def pair_sum(g, core, theirs, name):
    w = g.shape[2]
    nh = HROWS // SUM_TR

    def body(core_ref, a_ref, b_ref, o_ref, ob_ref):
        s = a_ref[...] + b_ref[...]
        o_ref[...] = s
        ob_ref[...] = s.astype(BF16)

    spec = pl.BlockSpec((1, SUM_TR, w), lambda d, i, c: (d, i, 0))
    return pl.pallas_call(
        body, name=name,
        out_shape=[jax.ShapeDtypeStruct((4, HROWS, w), F32), jax.ShapeDtypeStruct((4, HROWS, w), BF16)],
        grid_spec=pltpu.PrefetchScalarGridSpec(
            num_scalar_prefetch=1, grid=(4, nh),
            in_specs=[pl.BlockSpec((1, SUM_TR, w), lambda d, i, c: (d, c[0] * nh + i, 0)), spec],
            out_specs=[spec, spec]),
        compiler_params=_cp(("parallel", "parallel")))(core.reshape(1).astype(jnp.int32), g, theirs)


def chip_sum(part, chip, others, name):
    r, w = part.shape[1:]

    def body(chip_ref, a_ref, b_ref, o_ref):
        acc = a_ref[0]
        for k in range(3):
            acc = acc + b_ref[k].astype(F32)
        o_ref[...] = acc

    return pl.pallas_call(
        body, name=name, out_shape=jax.ShapeDtypeStruct((r, w), F32),
        grid_spec=pltpu.PrefetchScalarGridSpec(
            num_scalar_prefetch=1, grid=(r // SUM_TR,),
            in_specs=[pl.BlockSpec((1, SUM_TR, w), lambda i, c: (c[0], i, 0)),
                      pl.BlockSpec((3, SUM_TR, w), lambda i, c: (0, i, 0))],
            out_specs=pl.BlockSpec((SUM_TR, w), lambda i, c: (i, 0))),
        compiler_params=_cp(("parallel",)))(chip.reshape(1).astype(jnp.int32), part, others)


def sum_devices(g):
    r = g.shape[1]

    def body(g_ref, o_ref):
        acc = g_ref[0]
        for d in range(1, 8):
            acc = acc + g_ref[d]
        o_ref[...] = acc

    return pl.pallas_call(body, name="sum_devices", out_shape=jax.ShapeDtypeStruct((r, 1024), F32),
                          compiler_params=_cp())(g)


def adamw(w, g, m, v, name):
    r, c = w.shape
    tr = r
    for cand in (256, 128, 64, 32, 16, 8):
        if r % cand == 0 and r > cand:
            tr = cand
            break

    def body(w_ref, g_ref, m_ref, v_ref, d_ref, nm_ref, nv_ref):
        gv = g_ref[...]
        mn = ADAM_B1 * m_ref[...] + (1.0 - ADAM_B1) * gv
        vn = ADAM_B2 * v_ref[...] + (1.0 - ADAM_B2) * (gv * gv)
        m_hat = mn / (1.0 - ADAM_B1 ** ADAM_STEP)
        v_hat = vn / (1.0 - ADAM_B2 ** ADAM_STEP)
        d_ref[...] = -ADAM_LR * (m_hat / (jnp.sqrt(v_hat) + ADAM_EPS) + ADAM_WD * w_ref[...])
        nm_ref[...] = mn
        nv_ref[...] = vn

    spec = pl.BlockSpec((tr, c), lambda i: (i, 0))
    sd = jax.ShapeDtypeStruct((r, c), F32)
    return pl.pallas_call(body, name=name, grid=(r // tr,), in_specs=[spec] * 4, out_specs=[spec] * 3,
                          out_shape=[sd, sd, sd], compiler_params=_cp(("parallel",)))(w, g, m, v)


ANY = pl.BlockSpec(memory_space=pl.ANY)
VM = pl.BlockSpec(memory_space=pltpu.VMEM)
OTHER_CHIPS = ((1, 0), (0, 1), (1, 1))


def _pos():
    return lax.axis_index("x"), lax.axis_index("y"), lax.axis_index("c")


def _flip(v, bit):
    return 1 - v if bit else v


def _rcopy(src, dst, ssem, rsem, peer):
    return pltpu.make_async_remote_copy(src_ref=src, dst_ref=dst, send_sem=ssem, recv_sem=rsem,
                                        device_id=peer, device_id_type=MESH)


def allgather_small(p, name):
    r = p.shape[0]

    def body(in_ref, out_ref, ssem, rsem, lsem):
        x, y, c = _pos()
        me = 4 * x + 2 * y + c
        loc = pltpu.make_async_copy(in_ref, out_ref.at[me], lsem)
        loc.start()
        sends = []
        peers = []
        for k in range(1, 8):
            px, py, pc = _flip(x, (k >> 2) & 1), _flip(y, (k >> 1) & 1), _flip(c, k & 1)
            peers.append((px, py, pc))
            cp = _rcopy(in_ref, out_ref.at[me], ssem.at[k - 1], rsem.at[k - 1], (px, py, pc))
            cp.start()
            sends.append(cp)
        for k in range(1, 8):
            px, py, pc = peers[k - 1]
            _rcopy(in_ref, out_ref.at[4 * px + 2 * py + pc], ssem.at[k - 1], rsem.at[k - 1], (px, py, pc)).wait_recv()
        for cp in sends:
            cp.wait_send()
        loc.wait()

    return pl.pallas_call(
        body, name=name, out_shape=jax.ShapeDtypeStruct((8, r, 1024), F32),
        in_specs=[VM], out_specs=VM,
        scratch_shapes=[pltpu.SemaphoreType.DMA((7,)), pltpu.SemaphoreType.DMA((7,)), pltpu.SemaphoreType.DMA],
    )(p)


def gather_weights(w_in_b, mod_sh):
    def body(wi_ref, m_ref, gi_ref, mo_ref, ssem, rsem, lsem):
        x, y, c = _pos()
        chip = 2 * x + y
        mine = pl.ds(pl.multiple_of(c * HROWS, 16), HROWS)
        other = pl.ds(pl.multiple_of((1 - c) * HROWS, 16), HROWS)
        sib = (x, y, 1 - c)
        pairs = ((wi_ref, gi_ref),)
        loc_m = pltpu.make_async_copy(m_ref, mo_ref.at[chip], lsem)
        loc_m.start()
        sends = []
        for k, (fx, fy) in enumerate(OTHER_CHIPS):
            peer = (_flip(x, fx), _flip(y, fy), c)
            for a, (w_ref, g_ref) in enumerate(pairs):
                cw = _rcopy(w_ref.at[mine], g_ref.at[chip, mine], ssem.at[6 * a + k], rsem.at[6 * a + k], peer)
                cw.start()
                sends.append(cw)
            cm = _rcopy(m_ref, mo_ref.at[chip], ssem.at[12 + k], rsem.at[12 + k], peer)
            cm.start()
            sends.append(cm)
        for k, (fx, fy) in enumerate(OTHER_CHIPS):
            px, py = _flip(x, fx), _flip(y, fy)
            for a, (w_ref, g_ref) in enumerate(pairs):
                got = g_ref.at[2 * px + py, mine]
                _rcopy(w_ref.at[mine], got, ssem.at[6 * a + k], rsem.at[6 * a + k], (px, py, c)).wait_recv()
                fw = _rcopy(got, got, ssem.at[6 * a + 3 + k], rsem.at[6 * a + 3 + k], sib)
                fw.start()
                sends.append(fw)
        for k, (fx, fy) in enumerate(OTHER_CHIPS):
            px, py = _flip(x, fx), _flip(y, fy)
            for a, (w_ref, g_ref) in enumerate(pairs):
                land = g_ref.at[2 * px + py, other]
                _rcopy(land, land, ssem.at[6 * a + 3 + k], rsem.at[6 * a + 3 + k], sib).wait_recv()
            _rcopy(m_ref, mo_ref.at[2 * px + py], ssem.at[12 + k], rsem.at[12 + k], (px, py, c)).wait_recv()
        for cp in sends:
            cp.wait_send()
        loc_m.wait()

    return pl.pallas_call(
        body, name="gather_weights",
        out_shape=[jax.ShapeDtypeStruct((4, D, SH_IN), BF16), jax.ShapeDtypeStruct((4, 8, 768), F32)],
        in_specs=[ANY, VM], out_specs=[ANY, VM],
        scratch_shapes=[pltpu.SemaphoreType.DMA((15,)), pltpu.SemaphoreType.DMA((15,)), pltpu.SemaphoreType.DMA],
    )(w_in_b, mod_sh)


def pair_exchange(g):
    def body(g_ref, r_ref, ssem, rsem):
        x, y, c = _pos()
        other = pl.ds(pl.multiple_of((1 - c) * HROWS, 8), HROWS)
        cp = _rcopy(g_ref.at[:, other, :], r_ref, ssem, rsem, (x, y, 1 - c))
        cp.start()
        cp.wait()

    return pl.pallas_call(
        body, name="pair_exchange", out_shape=jax.ShapeDtypeStruct((4, HROWS, g.shape[2]), F32),
        in_specs=[ANY], out_specs=ANY,
        scratch_shapes=[pltpu.SemaphoreType.DMA, pltpu.SemaphoreType.DMA],
    )(g)


HBM = pl.BlockSpec(memory_space=pltpu.HBM)
SEM = pl.BlockSpec(memory_space=pltpu.SEMAPHORE)
DATAFLOW = pltpu.SideEffectType.DATAFLOW_SIDE_EFFECTING


def split_start(name, make_copies, srcs, lands, nsem, after):
    arrays = [*srcs, *lands]
    n, ns = len(arrays), len(srcs)

    def body(*refs):
        for cp in make_copies(refs[:ns], refs[ns:n], refs[n + 1], refs[n + 2])[0]:
            cp.start()
        refs[-1][...] = jnp.zeros_like(refs[-1])

    res = pl.pallas_call(
        body, name=name,
        out_shape=(pltpu.SemaphoreType.DMA((nsem,)), pltpu.SemaphoreType.DMA((nsem,)),
                   *[pltpu.HBM(a.shape, a.dtype) for a in arrays], jax.ShapeDtypeStruct((8, 128), F32)),
        in_specs=(HBM,) * n + (ANY,), out_specs=(SEM, SEM) + (HBM,) * n + (VM,),
        input_output_aliases={i: 2 + i for i in range(n)},
        compiler_params=pltpu.CompilerParams(has_side_effects=DATAFLOW),
    )(*[pltpu.with_memory_space_constraint(a, pltpu.HBM) for a in arrays], after)
    return res[0], res[1], list(res[2:2 + n]), res[-1]


def split_wait(name, make_copies, ssem, rsem, arrays, ns, after):
    n = len(arrays)

    def body(*refs):
        sends, recvs = make_copies(refs[:ns], refs[ns:n], refs[n], refs[n + 1])
        for cp in sends:
            cp.wait_send()
        for cp in recvs:
            cp.wait_recv()

    return pl.pallas_call(
        body, name=name, out_shape=tuple(pltpu.HBM(a.shape, a.dtype) for a in arrays),
        in_specs=(HBM,) * n + (SEM, SEM, ANY), out_specs=(HBM,) * n,
        input_output_aliases={i: i for i in range(n)},
        compiler_params=pltpu.CompilerParams(has_side_effects=DATAFLOW),
    )(*arrays, ssem, rsem, after)


def _chip_copies(srcs, lands, ssem, rsem):
    x, y, c = _pos()
    copies = []
    for k, (fx, fy) in enumerate(OTHER_CHIPS):
        px, py = _flip(x, fx), _flip(y, fy)
        for a, (p_ref, l_ref) in enumerate(zip(srcs, lands)):
            copies.append(_rcopy(p_ref.at[2 * px + py], l_ref.at[k], ssem.at[3 * a + k], rsem.at[3 * a + k], (px, py, c)))
    return copies, copies


def _pair_copies(srcs, lands, ssem, rsem):
    x, y, c = _pos()
    other = pl.ds(pl.multiple_of((1 - c) * HROWS, 8), HROWS)
    copies = [_rcopy(srcs[0].at[:, other, :], lands[0], ssem.at[0], rsem.at[0], (x, y, 1 - c))]
    return copies, copies


def _rest_copies(srcs, lands, ssem, rsem):
    x, y, c = _pos()
    chip = 2 * x + y
    mine = pl.ds(pl.multiple_of(c * HROWS, 16), HROWS)
    sends, recvs = [], []
    for k, (fx, fy) in enumerate(OTHER_CHIPS):
        px, py = _flip(x, fx), _flip(y, fy)
        for t in range(2):
            rows_t = pl.ds(t * HROWS, HROWS)
            sends.append(_rcopy(srcs[0].at[mine], lands[0].at[chip, mine], ssem.at[2 * k + t], rsem.at[2 * k + c],
                                (px, py, t)))
            recvs.append(_rcopy(srcs[0].at[rows_t], lands[0].at[2 * px + py, rows_t], ssem.at[2 * k + t],
                                rsem.at[2 * k + t], (px, py, t)))
    return sends, recvs


def pair_swap(red_in, red_rest):
    def body(ai_ref, ar_ref, oi_ref, or_ref, ssem, rsem):
        x, y, c = _pos()
        cps = [_rcopy(a_ref, o_ref, ssem.at[a], rsem.at[a], (x, y, 1 - c))
               for a, (a_ref, o_ref) in enumerate(((ai_ref, oi_ref), (ar_ref, or_ref)))]
        for cp in cps:
            cp.start()
        for cp in cps:
            cp.wait()

    return pl.pallas_call(
        body, name="pair_swap",
        out_shape=[jax.ShapeDtypeStruct((HROWS, SH_IN), F32), jax.ShapeDtypeStruct((HROWS, D), F32)],
        in_specs=[ANY, ANY], out_specs=[ANY, ANY],
        scratch_shapes=[pltpu.SemaphoreType.DMA((2,)), pltpu.SemaphoreType.DMA((2,))],
    )(red_in, red_rest)


def _flat(v, width=1024):
    v = v.reshape(-1)
    n = -(-v.shape[0] // width) * width
    return jnp.pad(v, (0, n - v.shape[0]))


def _rows(parts, rows):
    flat = jnp.concatenate(parts)
    return jnp.pad(flat, (0, rows * 1024 - flat.shape[0])).reshape(rows, 1024)


def _pack_small(b_ada, norm_w, conv_b, ssm_norm_w, q_norm_w, k_norm_w, sinks, dt_bias, a_log, d_skip, rel_bias,
                extra=None, tail=(), rows=16):
    misc = [q_norm_w, k_norm_w, sinks, dt_bias, a_log, d_skip] + ([] if extra is None else [extra])
    parts = [_flat(b_ada), _flat(norm_w), _flat(conv_b), _flat(ssm_norm_w)] + [_flat(v, 128) for v in misc]
    parts.append(jnp.zeros(((8 - len(misc)) * 128,), F32))
    parts.append(_flat(rel_bias))
    parts.append(jnp.zeros((5 * 1024,), F32))
    return _rows(parts + [_flat(v) for v in tail], rows)


def _unpack_small(p):
    misc = p[9]
    return dict(b_ada=p[0:3].reshape(1, 3072), norm_w=p[3:4], conv_b=p[4:7].reshape(1, 3072),
                ssm_norm_w=p[7:9].reshape(1, 2048), q_norm_w=misc[None, 0:64], k_norm_w=misc[None, 128:192],
                sinks=misc[None, 256:272], dt_bias=misc[None, 384:416], a_log=misc[None, 512:544],
                d_skip=misc[None, 640:672], rel_bias=p[10, :512].reshape(32, 16), extra=misc[768])


SMALL = ("b_ada", "norm_w", "conv_b", "ssm_norm_w", "q_norm_w", "k_norm_w", "sinks", "dt_bias", "a_log", "d_skip",
         "rel_bias")
WEIGHTS = ("w_ada", "b_ada", "norm_w", "w_in", "q_norm_w", "k_norm_w", "rel_bias", "sinks", "conv_w", "conv_b",
           "dt_bias", "a_log", "d_skip", "ssm_norm_w", "w_attn_proj", "w_ssm_proj", "w_out")
IN_COLS = ((0, 1024, C_Q), (1024, 256, C_K), (1280, 256, C_V), (1536, 1024, C_ZA), (2560, 2048, C_ZM),
           (4608, 3072, C_XBC), (7680, 32, C_DT), (7712, 1024, C_GA), (8736, 1024, C_GB))


def _to_cat(shards):
    parts, pos = [], 0
    for o, n, cnew in sorted(IN_COLS, key=lambda e: e[2]):
        assert cnew == pos
        c0 = o
        while c0 < o + n:
            i = c0 // SH_IN
            c1 = min(o + n, (i + 1) * SH_IN)
            parts.append(shards[i][:, c0 - i * SH_IN:c1 - i * SH_IN])
            c0 = c1
        pos += n
    parts.append(jnp.zeros((D, NP - pos), shards.dtype))
    return jnp.concatenate(parts, axis=1)


def _from_cat(dw_pieces):
    starts = [subs[0][1] for _, subs in DPIECES]

    def cols(c0, c1):
        p = max(q for q in range(len(starts)) if starts[q] <= c0)
        return dw_pieces[p][:, c0 - starts[p]:c1 - starts[p]]

    shards = []
    for i in range(4):
        lo, hi = i * SH_IN, (i + 1) * SH_IN
        parts = []
        for o, n, cnew in IN_COLS:
            a, b = max(o, lo), min(o + n, hi)
            if a < b:
                parts.append(cols(cnew + a - o, cnew + b - o))
        shards.append(jnp.concatenate(parts, axis=1))
    return jnp.stack(shards)


def kernel(x, c, w_ada, b_ada, norm_w, w_in, q_norm_w, k_norm_w, rel_bias, sinks, conv_w, conv_b, dt_bias, a_log, d_skip, ssm_norm_w, w_attn_proj, w_ssm_proj, w_out, loss_target, m_w_ada, m_b_ada, m_norm_w, m_w_in, m_q_norm_w, m_k_norm_w, m_rel_bias, m_sinks, m_conv_w, m_conv_b, m_dt_bias, m_a_log, m_d_skip, m_ssm_norm_w, m_w_attn_proj, m_w_ssm_proj, m_w_out, v_w_ada, v_b_ada, v_norm_w, v_w_in, v_q_norm_w, v_k_norm_w, v_rel_bias, v_sinks, v_conv_w, v_conv_b, v_dt_bias, v_a_log, v_d_skip, v_ssm_norm_w, v_w_attn_proj, v_w_ssm_proj, v_w_out):
    args = dict(locals())
    xi, yi, ci = lax.axis_index("x"), lax.axis_index("y"), lax.axis_index("c")
    chip = 2 * xi + yi
    me = 4 * xi + 2 * yi + ci
    x2 = x[0]
    tgt = loss_target[0]

    pay = _rows([c.reshape(-1), conv_w[0].reshape(-1)], 8)
    g0 = allgather_small(pay, "gather_cond")
    c_all = g0[:, 0, :]
    conv_w_full = g0[0::2, 1:4, :].reshape(4, CONV_K, 768).transpose(1, 0, 2).reshape(CONV_K, XBC)

    b_ada_sh = lax.dynamic_slice(b_ada, (0, chip * 768), (1, 768))
    mod_sh = ada_mod(c_all, w_ada[0], b_ada_sh)

    w_in_b = w_in[0].astype(BF16)
    w_rest_b = jnp.concatenate([w_attn_proj[0], w_ssm_proj[0], w_out[0]], axis=0).astype(BF16)
    wg_in, modg = gather_weights(w_in_b, mod_sh)
    wg_in = lax.dynamic_update_slice(wg_in, w_in_b[None], (chip, 0, 0))
    rs_sem, rr_sem, rest_thru, rest_tok = split_start("gather_rest_start", _rest_copies, [w_rest_b],
                                                      [lax.empty((4, D, D), BF16)], 6, modg)
    mod = lax.dynamic_slice(modg, (0, me, 0), (4, 1, 768)).reshape(1, 3 * D)
    shift, scale, gate = mod[:, :D], mod[:, D:2 * D] + rest_tok[:1, :1], mod[:, 2 * D:]
    wcat = _to_cat(wg_in)

    pad128 = lambda v: jnp.pad(v, ((0, 0), (0, 128 - v.shape[1])))
    dtb_p, alog_p, dsk_p = pad128(dt_bias), pad128(a_log), pad128(d_skip)
    bucket = _bucket_table()

    proj, dt_raw, h_t = norm_proj(x2, norm_w, scale, shift, wcat)
    biasm = bias_expand(rel_bias, sinks, bucket)
    ao = attn_fwd(proj, biasm, q_norm_w, k_norm_w)
    act, dsl = conv_fwd(proj, conv_w_full, conv_b)
    yss, sprev = ssd_fwd(act, dt_raw, dtb_p, alog_p, dsk_p)

    w_rest_b, wg_rest = split_wait("gather_rest_wait", _rest_copies, rs_sem, rr_sem, rest_thru, 1, yss)
    wg_rest = lax.dynamic_update_slice(wg_rest, w_rest_b[None], (chip, 0, 0))
    w_at = wg_rest[:, :R_AT].reshape(D, D)
    w_ss = wg_rest[:, R_AT:R_AT + R_SS].reshape(SSM_W, D)
    w_ou = wg_rest[:, R_AT + R_SS:].reshape(D, D)
    (loss_p, dy, dao, dmid, dyss, ua_t, yn_t, mg_t, dya, dyb, dout, dgate, dssm_nw) = tail(
        proj, ao, yss, x2, tgt, gate, ssm_norm_w, w_at, w_ss, w_ou)

    dq, dkv, dqw, dkw, dacc = attn_bwd(proj, dao, biasm, q_norm_w, k_norm_w)
    dbias = bias_reduce(dacc, bucket)
    drb = dbias[:, :NBUCKET].T
    dsk = dbias[:, NBUCKET].reshape(1, HQ)
    dact, ddt, ddtb, dalog, ddskip = ssd_bwd(act, dt_raw, dyss, sprev, dtb_p, alog_p, dsk_p)
    dxbc, dconv_w, dconv_b = conv_bwd(proj, dact, dsl, conv_w_full)

    dproj = (dq, dmid, dxbc, dkv, ddt)
    dwcat = [wgrad(h_t, piece, "dw_in_%d" % p, min(piece.shape[1], 1024), rest_tok) for p, piece in enumerate(dproj)]

    g_in = _from_cat(dwcat)
    ps_sem, pr_sem, pair_thru, pair_tok = split_start("pair_in_start", _pair_copies, [g_in],
                                                      [lax.empty((4, HROWS, SH_IN), F32)], 1, loss_p)
    dw_at = wgrad(ua_t, dya, "dw_attn", 512, pair_tok)
    dw_ss = wgrad(yn_t, dyb, "dw_ssm", 512, pair_tok)
    dw_ou = wgrad(mg_t, dout, "dw_out", 512, pair_tok)
    g_rest = jnp.concatenate([dw_at.reshape(4, R_AT, D), dw_ss.reshape(4, R_SS, D), dw_ou.reshape(4, R_OU, D)], axis=1)
    sib_rest = pair_exchange(g_rest)
    g_in, sib_in = split_wait("pair_in_wait", _pair_copies, ps_sem, pr_sem, pair_thru, 1, sib_rest)
    part_in, pb_in = pair_sum(g_in, ci, sib_in, "pair_sum_in")
    part_rest, pb_rest = pair_sum(g_rest, ci, sib_rest, "pair_sum_rest")
    cs_sem, cr_sem, chip_thru, token = split_start(
        "chip_exchange_start", _chip_copies, [pb_in, pb_rest],
        [lax.empty((3, HROWS, SH_IN), BF16), lax.empty((3, HROWS, D), BF16)], 6, part_rest)
    grad_x, dnorm_w, dscale, dshift = dproj_bwd(dproj, wcat, x2, dy, norm_w, scale + token[:1, :1])
    _, _, oth_in, oth_rest = split_wait("chip_exchange_wait", _chip_copies, cs_sem, cr_sem, chip_thru, 2, dshift)
    red_in = chip_sum(part_in, chip, oth_in, "chip_sum_in")
    red_rest = chip_sum(part_rest, chip, oth_rest, "chip_sum_rest")
    recv_in, recv_rest = pair_swap(red_in, red_rest)
    both = lambda mine, theirs: jnp.concatenate([jnp.where(ci == 0, mine, theirs), jnp.where(ci == 0, theirs, mine)],
                                                axis=0)
    g_shard_in = both(red_in, recv_in)
    g_shard_rest = both(red_rest, recv_rest)

    dmod = jnp.concatenate([dshift, dscale, dgate], axis=1)
    gsmall = _pack_small(dmod, dnorm_w, dconv_b, dssm_nw, dqw, dkw, dsk[:, :HQ], ddtb[:, :SH], dalog[:, :SH],
                         ddskip[:, :SH], drb, extra=loss_p[:, :1], tail=(dconv_w,), rows=32)
    gall = allgather_small(gsmall, "gather_small_grads")
    ssum = sum_devices(gall)
    gs = _unpack_small(ssum[:16])
    loss = gs["extra"]
    dconv_w_sh = lax.dynamic_slice(ssum[16:28].reshape(CONV_K, XBC), (0, chip * 768), (CONV_K, 768))
    dmod_all = gall[:, 0:3, :].reshape(8, 3 * D)
    dw_ada = ada_grad(c_all, lax.dynamic_slice(dmod_all, (0, chip * 768), (8, 768)))

    grads = dict(gs)
    grads["w_ada"] = dw_ada
    grads["w_in"] = g_shard_in
    grads["w_attn_proj"] = g_shard_rest[:R_AT]
    grads["w_ssm_proj"] = g_shard_rest[R_AT:R_AT + R_SS]
    grads["w_out"] = g_shard_rest[R_AT + R_SS:]
    grads["conv_w"] = dconv_w_sh

    delta, new_m, new_v = {}, {}, {}
    for n in ("w_ada", "w_in", "conv_w", "w_attn_proj", "w_ssm_proj", "w_out"):
        delta[n], new_m[n], new_v[n] = adamw(args[n][0], grads[n], args["m_" + n][0], args["v_" + n][0], "adamw_" + n)
    ws = _pack_small(*[args[n] for n in SMALL])
    ms = _pack_small(*[args["m_" + n] for n in SMALL])
    vs = _pack_small(*[args["v_" + n] for n in SMALL])
    d_s, m_s, v_s = adamw(ws, ssum[:16], ms, vs, "adamw_small")
    d_s, m_s, v_s = _unpack_small(d_s), _unpack_small(m_s), _unpack_small(v_s)
    for n in SMALL:
        delta[n], new_m[n], new_v[n] = d_s[n], m_s[n], v_s[n]

    def shaped(n, a):
        return a.reshape(args[n].shape)

    outs = [loss, grad_x[None]]
    for table in (grads, delta, new_m, new_v):
        outs += [shaped(n, table[n]) for n in WEIGHTS]
    return tuple(outs)
```

```python
import functools
import math

import jax
import jax.numpy as jnp
from jax import lax
from jax.experimental import pallas as pl
from jax.experimental.pallas import tpu as pltpu

F32 = jnp.float32
BF16 = jnp.bfloat16
MESH = pl.DeviceIdType.MESH

D = 1024
HQ, HKV, GRP, DH = 16, 4, 4, 64
BLK = 128
NBUCKET, MAXDIST = 32, 128
SSM_W, SH, SG, SR, SP, SN = 2048, 32, 4, 8, 64, 128
CONV_K = 4
XBC = SSM_W + 2 * SG * SN
IN_W = 9760
EPS = 1e-6
NEG = -1e30
SCALE = DH ** -0.5

C_Q, C_ZA, C_GA, C_GB, C_ZM, C_XBC, C_K, C_V, C_DT = 0, 1024, 2048, 3072, 4096, 6144, 9216, 9472, 9728
NP = 9984
TN = 1664
W_MID = C_XBC - C_ZA

SH_IN = IN_W // 4
R_AT, R_SS, R_OU = 256, 512, 256
HROWS = D // 2

ADAM_LR, ADAM_B1, ADAM_B2, ADAM_EPS, ADAM_WD, ADAM_STEP = 0.001, 0.9, 0.999, 1e-08, 0.01, 10

VMEM_LIMIT = 56 * 1024 * 1024


def _cp(sem=None):
    if sem is None:
        return pltpu.CompilerParams(vmem_limit_bytes=VMEM_LIMIT)
    return pltpu.CompilerParams(dimension_semantics=sem, vmem_limit_bytes=VMEM_LIMIT)


def _sig(x):
    return 0.5 * jnp.tanh(0.5 * x) + 0.5


def _dot(a, b):
    return jnp.dot(a, b, preferred_element_type=F32)


def _dot_nt(a, b):
    return lax.dot_general(a, b, (((1,), (1,)), ((), ())), preferred_element_type=F32)


def _dot_tn(a, b):
    return lax.dot_general(a, b, (((0,), (0,)), ((), ())), preferred_element_type=F32)


def _rsum(x):
    return jnp.sum(x, axis=-1, keepdims=True)


def _csum(x):
    return jnp.sum(x, axis=0, keepdims=True)


def _asum(x):
    return _csum(_rsum(x))


def _full(shape):
    nd = len(shape)
    return pl.BlockSpec(shape, lambda *_: (0,) * nd)


def ada_mod(c_all, w_ada_sh, b_ada_sh):
    def body(c_ref, w_ref, b_ref, o_ref):
        cv = c_ref[...]
        s = cv * _sig(cv)
        o_ref[...] = jnp.dot(s, w_ref[...], preferred_element_type=F32,
                             precision=lax.Precision.HIGHEST) + b_ref[...]

    n = w_ada_sh.shape[1]
    return pl.pallas_call(body, name="ada_mod", out_shape=jax.ShapeDtypeStruct((8, n), F32),
                          compiler_params=_cp())(c_all, w_ada_sh, b_ada_sh)


def ada_grad(c_all, dmod_sh):
    def body(c_ref, d_ref, o_ref):
        cv = c_ref[...]
        s = cv * _sig(cv)
        o_ref[...] = lax.dot_general(s, d_ref[...], (((0,), (0,)), ((), ())), preferred_element_type=F32,
                                     precision=lax.Precision.HIGHEST)

    n = dmod_sh.shape[1]
    return pl.pallas_call(body, name="ada_grad", out_shape=jax.ShapeDtypeStruct((D, n), F32),
                          compiler_params=_cp())(c_all, dmod_sh)


def norm_proj(x, norm_w, scale, shift, wcat):
    t = x.shape[0]
    tm = min(t, 1024)

    def body(x_ref, nw_ref, sc_ref, sh_ref, w_ref, p_ref, dt_ref, ht_ref, hs):
        @pl.when(pl.program_id(1) == 0)
        def _():
            xv = x_ref[...]
            r = lax.rsqrt(jnp.mean(xv * xv, axis=-1, keepdims=True) + EPS)
            h = (xv * r) * nw_ref[...]
            h = h * (1.0 + sc_ref[...]) + sh_ref[...]
            hs[...] = h.astype(BF16)
            ht_ref[...] = h.T.astype(BF16)

        p = _dot(hs[...], w_ref[...])
        p_ref[...] = p.astype(BF16)

        @pl.when(pl.program_id(1) == C_DT // TN)
        def _():
            dt_ref[...] = p[:, C_DT % TN:C_DT % TN + 128]

    vec = pl.BlockSpec((1, D), lambda i, j: (0, 0))
    return pl.pallas_call(
        body, name="norm_proj", grid=(t // tm, NP // TN),
        in_specs=[pl.BlockSpec((tm, D), lambda i, j: (i, 0)), vec, vec, vec,
                  pl.BlockSpec((D, TN), lambda i, j: (0, j))],
        out_specs=[pl.BlockSpec((tm, TN), lambda i, j: (i, j)), pl.BlockSpec((tm, 128), lambda i, j: (i, 0)),
                   pl.BlockSpec((D, tm), lambda i, j: (0, i))],
        out_shape=[jax.ShapeDtypeStruct((t, NP), BF16), jax.ShapeDtypeStruct((t, 128), F32),
                   jax.ShapeDtypeStruct((D, t), BF16)],
        scratch_shapes=[pltpu.VMEM((tm, D), BF16)],
        compiler_params=_cp(("parallel", "arbitrary")),
    )(x, norm_w, scale, shift, wcat)


def _bucket_table():
    qi = jnp.arange(BLK)[:, None]
    kj = jnp.arange(2 * BLK)[None, :]
    dist = qi + BLK - kj
    n = jnp.maximum(dist, 0)
    max_exact = NBUCKET // 2
    nf = jnp.maximum(n, 1).astype(F32)
    large = max_exact + (jnp.log(nf / max_exact) / math.log(MAXDIST / max_exact)
                         * (NBUCKET - max_exact)).astype(jnp.int32)
    large = jnp.minimum(large, NBUCKET - 1)
    bucket = jnp.where(n < max_exact, n, large).astype(jnp.int32)
    valid = (dist >= 0) & (dist < BLK)
    return jnp.where(valid, bucket, -1)


def bias_expand(rel_bias, sinks, bucket):
    def body(rb_ref, sk_ref, bk_ref, o_ref):
        hd = pl.program_id(0)
        bk = bk_ref[...]
        col = lax.broadcasted_iota(jnp.int32, (BLK, 2 * BLK), 1)

        def step(b, acc):
            return jnp.where(bk == b, rb_ref[b, hd], acc)

        acc = lax.fori_loop(0, NBUCKET, step, jnp.full((BLK, 2 * BLK), NEG, F32))
        acc = jnp.where(col == 0, sk_ref[0, hd], acc)
        o_ref[1, 0] = acc
        o_ref[0, 0] = jnp.where(jnp.logical_and(col > 0, col < BLK), NEG, acc)

    smem = pl.BlockSpec(memory_space=pltpu.SMEM)
    return pl.pallas_call(
        body, name="bias_expand", grid=(HQ,),
        in_specs=[smem, smem, _full((BLK, 2 * BLK))],
        out_specs=pl.BlockSpec((2, 1, BLK, 2 * BLK), lambda h: (0, h, 0, 0)),
        out_shape=jax.ShapeDtypeStruct((2, HQ, BLK, 2 * BLK), F32),
        compiler_params=_cp(("arbitrary",)),
    )(rel_bias, sinks, bucket)


def bias_reduce(dacc, bucket):
    col = jnp.arange(BLK * 2 * BLK, dtype=jnp.int32) % (2 * BLK)
    lane = jnp.arange(128, dtype=jnp.int32)[None, :]
    member = (bucket.reshape(-1)[:, None] == lane) | ((col[:, None] == 0) & (lane == NBUCKET))

    def body(d_ref, m_ref, o_ref):
        mm = m_ref[...]
        o_ref[...] = sum(_dot(part, mm) for part in _split3(d_ref[...]))

    return pl.pallas_call(body, name="bias_reduce", out_shape=jax.ShapeDtypeStruct((HQ, 128), F32),
                          compiler_params=_cp())(dacc.reshape(HQ, BLK * 2 * BLK), member.astype(BF16))


GQ = GRP * BLK


def _stack_heads(x, nh):
    return jnp.concatenate([x[:, DH * h:DH * (h + 1)] for h in range(nh)], axis=0)


def _unstack(xs, nh):
    rows = xs.shape[0] // nh
    return jnp.concatenate([xs[rows * h:rows * (h + 1)] for h in range(nh)], axis=1)


def _rms(x):
    return lax.rsqrt(jnp.mean(x * x, axis=-1, keepdims=True) + EPS)


def _stack_q(q, qw):
    qs = _stack_heads(q, HQ)
    r = _rms(qs)
    qhat = qs * r
    return qhat * qw, qhat, r


def _band_first(shape):
    return (lax.broadcasted_iota(jnp.int32, shape, 0) & (2 * BLK - 1)) == 0


def _stack_kv(kp, kc, vp, vc, kw):
    ks = _stack_heads(jnp.concatenate([kp, kc], axis=0), HKV)
    r = _rms(ks)
    khat = ks * r
    first = _band_first(ks.shape)
    kn = jnp.where(first, 0.0, khat * kw)
    v2 = jnp.where(first, 0.0, _stack_heads(jnp.concatenate([vp, vc], axis=0), HKV)).astype(BF16)
    return kn, khat, r, v2


def _softmax_rows(s):
    p = jnp.exp(s - jnp.max(s, axis=-1, keepdims=True))
    return p * (1.0 / _rsum(p))


def attn_fwd(proj, biasm, q_norm_w, k_norm_w):
    t = proj.shape[0]
    nb = t // BLK

    def body(q_ref, kc_ref, kp_ref, vc_ref, vp_ref, bm_ref, qw_ref, kw_ref, o_ref):
        f = lambda ref: ref[...].astype(F32)
        qn = _stack_q(f(q_ref), qw_ref[...])[0].astype(BF16)
        kn, _, _, v2 = _stack_kv(f(kp_ref), f(kc_ref), f(vp_ref), f(vc_ref), kw_ref[...])
        knb = kn.astype(BF16)
        s = jnp.concatenate([_dot_nt(qn[GQ * j:GQ * (j + 1)], knb[2 * BLK * j:2 * BLK * (j + 1)])
                             for j in range(HKV)], axis=0)
        pr = _softmax_rows(s * SCALE + bm_ref[0].reshape(HQ * BLK, 2 * BLK)).astype(BF16)
        o = jnp.concatenate([_dot(pr[GQ * j:GQ * (j + 1)], v2[2 * BLK * j:2 * BLK * (j + 1)])
                             for j in range(HKV)], axis=0)
        o_ref[...] = _unstack(o, HQ).astype(BF16)

    kblk, vblk = C_K // 256, C_V // 256
    prev = lambda n: jnp.maximum(n - 1, 0)
    return pl.pallas_call(
        body, name="attn_fwd", grid=(nb,),
        in_specs=[pl.BlockSpec((BLK, D), lambda n: (n, 0)),
                  pl.BlockSpec((BLK, 256), lambda n: (n, kblk)),
                  pl.BlockSpec((BLK, 256), lambda n: (prev(n), kblk)),
                  pl.BlockSpec((BLK, 256), lambda n: (n, vblk)),
                  pl.BlockSpec((BLK, 256), lambda n: (prev(n), vblk)),
                  pl.BlockSpec((1, HQ, BLK, 2 * BLK), lambda n: (jnp.minimum(n, 1), 0, 0, 0)),
                  _full((1, DH)), _full((1, DH))],
        out_specs=pl.BlockSpec((BLK, D), lambda n: (n, 0)),
        out_shape=jax.ShapeDtypeStruct((t, D), BF16),
        compiler_params=_cp(("parallel",)),
    )(proj, proj, proj, proj, proj, biasm, q_norm_w, k_norm_w)


def attn_bwd(proj, dao, biasm, q_norm_w, k_norm_w):
    t = proj.shape[0]
    nb = t // BLK
    kb = 2 * BLK

    def body(q_ref, kc_ref, kp_ref, vc_ref, vp_ref, do_ref, bm_ref, qw_ref, kw_ref,
             dq_ref, dkv_ref, dqw_ref, dkw_ref, dacc_ref, ck, cv, pk, pv, nk, nv):
        n = pl.program_id(0)

        @pl.when(n == 0)
        def _():
            for ref in (dqw_ref, dkw_ref, dacc_ref, ck, cv):
                ref[...] = jnp.zeros_like(ref)

        qw = qw_ref[...]
        kw = kw_ref[...]
        f = lambda ref: ref[...].astype(F32)
        kn, khat, rk, v2 = _stack_kv(f(kp_ref), f(kc_ref), f(vp_ref), f(vc_ref), kw)
        grp = lambda a, j: a[GQ * j:GQ * (j + 1)]
        band = lambda a, j: a[kb * j:kb * (j + 1)]

        @pl.when(n < nb)
        def _():
            qn, qhat, rq = _stack_q(f(q_ref), qw)
            qnb = qn.astype(BF16)
            knb = kn.astype(BF16)
            dos = _stack_heads(f(do_ref), HQ).astype(BF16)
            s = jnp.concatenate([_dot_nt(grp(qnb, j), band(knb, j)) for j in range(HKV)], axis=0)
            pr = _softmax_rows(s * SCALE + bm_ref[0].reshape(HQ * BLK, kb))
            dp = jnp.concatenate([_dot_nt(grp(dos, j), band(v2, j)) for j in range(HKV)], axis=0)
            ds = pr * (dp - _rsum(pr * dp))
            dacc_ref[...] += ds.reshape(HQ, BLK, kb)
            dsb = ds.astype(BF16)
            prb = pr.astype(BF16)
            dqn = jnp.concatenate([_dot(grp(dsb, j), band(knb, j)) for j in range(HKV)], axis=0) * SCALE
            dqhat = dqn * qw
            dq = rq * (dqhat - qhat * jnp.mean(dqhat * qhat, axis=-1, keepdims=True))
            dq_ref[...] = _unstack(dq, HQ).astype(BF16)
            dqw_ref[...] += _csum(dqn * qhat)
            first = _band_first((kb, DH))
            for j in range(HKV):
                rows = slice(BLK * j, BLK * (j + 1))
                dkn = jnp.where(first, 0.0, _dot_tn(grp(dsb, j), grp(qnb, j)) * SCALE)
                dvj = jnp.where(first, 0.0, _dot_tn(grp(prb, j), grp(dos, j)))
                pk[rows, :] = dkn[:BLK]
                nk[rows, :] = dkn[BLK:]
                pv[rows, :] = dvj[:BLK]
                nv[rows, :] = dvj[BLK:]

        @pl.when(n == nb)
        def _():
            for ref in (pk, pv, nk, nv):
                ref[...] = jnp.zeros_like(ref)

        khp = jnp.concatenate([khat[kb * j:kb * j + BLK] for j in range(HKV)], axis=0)
        rkp = jnp.concatenate([rk[kb * j:kb * j + BLK] for j in range(HKV)], axis=0)
        dkn = ck[...] + pk[...]
        dkhat = dkn * kw
        dk = rkp * (dkhat - khp * jnp.mean(dkhat * khp, axis=-1, keepdims=True))
        dkw_ref[...] += _csum(dkn * khp)
        dkv_ref[...] = jnp.concatenate([_unstack(dk, HKV), _unstack(cv[...] + pv[...], HKV)], axis=1).astype(BF16)
        ck[...] = nk[...]
        cv[...] = nv[...]

    kblk, vblk = C_K // 256, C_V // 256
    cur = lambda n: jnp.minimum(n, nb - 1)
    prev = lambda n: jnp.maximum(n - 1, 0)
    carry = pltpu.VMEM((HKV * BLK, DH), F32)
    return pl.pallas_call(
        body, name="attn_bwd", grid=(nb + 1,),
        in_specs=[pl.BlockSpec((BLK, D), lambda n: (cur(n), 0)),
                  pl.BlockSpec((BLK, 256), lambda n: (cur(n), kblk)), pl.BlockSpec((BLK, 256), lambda n: (prev(n), kblk)),
                  pl.BlockSpec((BLK, 256), lambda n: (cur(n), vblk)), pl.BlockSpec((BLK, 256), lambda n: (prev(n), vblk)),
                  pl.BlockSpec((BLK, D), lambda n: (cur(n), 0)),
                  pl.BlockSpec((1, HQ, BLK, kb), lambda n: (jnp.minimum(n, 1), 0, 0, 0)),
                  _full((1, DH)), _full((1, DH))],
        out_specs=[pl.BlockSpec((BLK, D), lambda n: (cur(n), 0)),
                   pl.BlockSpec((BLK, 512), lambda n: (prev(n), 0)),
                   _full((1, DH)), _full((1, DH)), _full((HQ, BLK, kb))],
        out_shape=[jax.ShapeDtypeStruct((t, D), BF16), jax.ShapeDtypeStruct((t, 512), BF16),
                   jax.ShapeDtypeStruct((1, DH), F32),
                   jax.ShapeDtypeStruct((1, DH), F32), jax.ShapeDtypeStruct((HQ, BLK, kb), F32)],
        scratch_shapes=[carry] * 6,
        compiler_params=_cp(("arbitrary",)),
    )(proj, proj, proj, proj, proj, dao, biasm, q_norm_w, k_norm_w)


CONV_TM, CONV_CW, CONV_RC, HALO = 512, 1024, 32, 16


def conv_fwd(proj, conv_w, conv_b):
    t = proj.shape[0]
    tm = min(t, CONV_TM)
    c0 = C_XBC // CONV_CW

    def body(x_ref, xp_ref, w_ref, b_ref, o_ref, ds_ref):
        i = pl.program_id(1)
        w = w_ref[...]
        b = b_ref[...]
        for r in range(tm // CONV_RC):
            lo = r * CONV_RC
            if r == 0:
                head = jnp.where(i == 0, 0.0, xp_ref[...].astype(F32))
                win = jnp.concatenate([head, x_ref[0:CONV_RC, :].astype(F32)], axis=0)
            else:
                win = x_ref[lo - HALO:lo + CONV_RC, :].astype(F32)
            acc = b
            for j in range(CONV_K):
                acc = acc + w[j:j + 1] * win[HALO - 3 + j:HALO - 3 + j + CONV_RC]
            sg = _sig(acc)
            o_ref[lo:lo + CONV_RC, :] = acc * sg
            ds_ref[lo:lo + CONV_RC, :] = _dsilu(acc, sg).astype(BF16)

    rh = tm // HALO
    tile = pl.BlockSpec((tm, CONV_CW), lambda s, i: (i, s))
    return pl.pallas_call(
        body, name="conv_fwd", grid=(XBC // CONV_CW, t // tm),
        in_specs=[pl.BlockSpec((tm, CONV_CW), lambda s, i: (i, c0 + s)),
                  pl.BlockSpec((HALO, CONV_CW), lambda s, i: (jnp.maximum(i * rh - 1, 0), c0 + s)),
                  pl.BlockSpec((CONV_K, CONV_CW), lambda s, i: (0, s)), pl.BlockSpec((1, CONV_CW), lambda s, i: (0, s))],
        out_specs=[tile, tile],
        out_shape=[jax.ShapeDtypeStruct((t, XBC), F32), jax.ShapeDtypeStruct((t, XBC), BF16)],
        compiler_params=_cp(("parallel", "parallel")),
    )(proj, proj, conv_w, conv_b)


def conv_bwd(proj, dact, dsl, conv_w):
    t = proj.shape[0]
    tm = min(t, CONV_TM)
    nt = t // tm
    nr = tm // CONV_RC
    c0 = C_XBC // CONV_CW
    ext = CONV_RC + 8

    def body(x_ref, xp_ref, d_ref, dn_ref, s_ref, sn_ref, w_ref, dx_ref, dw_ref, db_ref):
        i = pl.program_id(1)

        @pl.when(i == 0)
        def _():
            dw_ref[...] = jnp.zeros_like(dw_ref)
            db_ref[...] = jnp.zeros_like(db_ref)

        w = w_ref[...]
        dws = [jnp.zeros((1, CONV_CW), F32) for _ in range(CONV_K)]
        db = jnp.zeros((1, CONV_CW), F32)
        for r in range(nr):
            lo = r * CONV_RC
            if r == 0:
                head = jnp.where(i == 0, 0.0, xp_ref[...].astype(F32))
                win = jnp.concatenate([head, x_ref[0:CONV_RC, :].astype(F32)], axis=0)
            else:
                win = x_ref[lo - HALO:lo + CONV_RC, :].astype(F32)
            if r < nr - 1:
                dext = d_ref[lo:lo + ext, :]
                sext = s_ref[lo:lo + CONV_RC + HALO, :].astype(F32)[0:ext]
            else:
                dext = jnp.concatenate([d_ref[lo:lo + CONV_RC, :], jnp.where(i == nt - 1, 0.0, dn_ref[...])], axis=0)
                sext = jnp.concatenate([s_ref[lo:lo + CONV_RC, :].astype(F32), sn_ref[...].astype(F32)], axis=0)[0:ext]
            dpre = dext * sext
            dx = jnp.zeros((CONV_RC, CONV_CW), F32)
            own = dpre[0:CONV_RC]
            for j in range(CONV_K):
                dx = dx + w[j:j + 1] * dpre[3 - j:3 - j + CONV_RC]
                dws[j] = dws[j] + _csum(own * win[HALO - 3 + j:HALO - 3 + j + CONV_RC])
            db = db + _csum(own)
            dx_ref[lo:lo + CONV_RC, :] = dx.astype(BF16)
        dw_ref[...] += jnp.concatenate(dws, axis=0)
        db_ref[...] += db

    rh = tm // HALO
    r8 = tm // 8
    nxt = lambda i, per: jnp.minimum((i + 1) * per, nt * per - 1)
    return pl.pallas_call(
        body, name="conv_bwd", grid=(XBC // CONV_CW, nt),
        in_specs=[pl.BlockSpec((tm, CONV_CW), lambda s, i: (i, c0 + s)),
                  pl.BlockSpec((HALO, CONV_CW), lambda s, i: (jnp.maximum(i * rh - 1, 0), c0 + s)),
                  pl.BlockSpec((tm, CONV_CW), lambda s, i: (i, s)),
                  pl.BlockSpec((8, CONV_CW), lambda s, i: (nxt(i, r8), s)),
                  pl.BlockSpec((tm, CONV_CW), lambda s, i: (i, s)),
                  pl.BlockSpec((HALO, CONV_CW), lambda s, i: (nxt(i, rh), s)),
                  pl.BlockSpec((CONV_K, CONV_CW), lambda s, i: (0, s))],
        out_specs=[pl.BlockSpec((tm, CONV_CW), lambda s, i: (i, s)),
                   pl.BlockSpec((CONV_K, CONV_CW), lambda s, i: (0, s)), pl.BlockSpec((1, CONV_CW), lambda s, i: (0, s))],
        out_shape=[jax.ShapeDtypeStruct((t, XBC), BF16), jax.ShapeDtypeStruct((CONV_K, XBC), F32),
                   jax.ShapeDtypeStruct((1, XBC), F32)],
        compiler_params=_cp(("parallel", "arbitrary")),
    )(proj, proj, dact, dact, dsl, dsl, conv_w)


def _split3(x):
    h = x.astype(BF16)
    r = x - h.astype(F32)
    m = r.astype(BF16)
    lo = (r - m.astype(F32)).astype(BF16)
    return h, m, lo


def _tri_mm(tri, x):
    h, m, lo = _split3(x)
    return _dot(tri, h) + _dot(tri, m) + _dot(tri, lo)


def _softplus(x):
    return jnp.maximum(x, 0.0) + jnp.log1p(jnp.exp(-jnp.abs(x)))


def _chunk_decays(dt_raw, dtb, alog):
    dtv = _softplus(dt_raw + dtb)
    a = -jnp.exp(alog)
    ri = lax.broadcasted_iota(jnp.int32, (BLK, BLK), 0)
    ci = lax.broadcasted_iota(jnp.int32, (BLK, BLK), 1)
    causal = ri >= ci
    acum = _tri_mm(causal.astype(BF16), dtv * a)
    return dtv, a, causal, acum, acum.T


NPAIR = SH // 2


def _pairs(x):
    return jnp.stack([x[:, 128 * k:128 * (k + 1)] for k in range(NPAIR)])


def _unpairs(x3):
    return jnp.concatenate([x3[k] for k in range(NPAIR)], axis=1)


def _per_head_cols(m):
    return jnp.stack([jnp.broadcast_to(m[:, h:h + 1], m.shape) for h in range(SH)])


def _pair_lanes(t):
    r = t.reshape(NPAIR, 2, t.shape[1], 128)
    lo = lax.broadcasted_iota(jnp.int32, (1, t.shape[1], 128), 2) < SP
    return jnp.where(lo, r[:, 0], r[:, 1])


class _Chunk:
    pass


def _chunk_common(dt_raw, dtb, alog, dskip):
    cm = _Chunk()
    cm.dtv, cm.a, cm.causal, acum, acum_t = _chunk_decays(dt_raw, dtb, alog)
    cm.acol = _per_head_cols(acum)
    cm.arow = jnp.stack([acum_t[h:h + 1, :] for h in range(SH)])
    cm.lam = jnp.exp(jnp.where(cm.causal[None], cm.acol - cm.arow, NEG))
    apl = _pair_lanes(cm.acol)
    alast = apl[:, BLK - 1:BLK, :]
    cm.dpl = _pair_lanes(_per_head_cols(cm.dtv))
    cm.eapl = jnp.exp(apl)
    cm.epl = jnp.exp(alast - apl)
    cm.cdpl = jnp.exp(alast)
    cm.dskpl = _pair_lanes(_per_head_cols(dskip))
    cm.lo = lax.broadcasted_iota(jnp.int32, (1, BLK, 128), 2) < SP
    return cm


def ssd_fwd(act, dt_raw, dtb_p, alog_p, dsk_p):
    t = act.shape[0]
    nc = t // BLK

    def body(xs_ref, b_ref, c_ref, dt_ref, dtb_ref, al_ref, dk_ref, y_ref, sp_ref, st):
        c = pl.program_id(0)

        @pl.when(c == 0)
        def _():
            st[...] = jnp.zeros_like(st)

        s_t = st[...]
        sp_ref[0] = s_t
        cm = _chunk_common(dt_ref[...], dtb_ref[...], al_ref[...], dk_ref[...])
        gms, cbs, bts = [], [], []
        for g in range(SG):
            bf = b_ref[:, SN * g:SN * (g + 1)]
            cb = c_ref[:, SN * g:SN * (g + 1)].astype(BF16)
            gms.append(_dot_nt(cb, bf.astype(BF16)))
            cbs.append(cb)
            bts.append(bf.T.astype(BF16))
        m = (cm.lam.reshape(SG, SR, BLK, BLK) * jnp.stack(gms)[:, None]).reshape(SH, BLK, BLK).astype(BF16)
        xs16 = _pairs(xs_ref[...])
        xdt16 = xs16 * cm.dpl
        x_lo = jnp.where(cm.lo, xdt16, 0.0).astype(BF16)
        x_hi = jnp.where(cm.lo, 0.0, xdt16).astype(BF16)
        s16 = _pairs(s_t)
        s16b = s16.astype(BF16)
        yd = jnp.stack([_dot(m[2 * k], x_lo[k]) + _dot(m[2 * k + 1], x_hi[k]) for k in range(NPAIR)])
        yo = jnp.stack([_dot(cbs[k // (NPAIR // SG)], s16b[k]) for k in range(NPAIR)])
        y_ref[...] = _unpairs(yd + yo * cm.eapl + cm.dskpl * xs16).astype(BF16)
        xe = (xdt16 * cm.epl).astype(BF16)
        st[...] = _unpairs(cm.cdpl * s16 + jnp.stack([_dot(bts[k // (NPAIR // SG)], xe[k]) for k in range(NPAIR)]))

    vec = _full((1, 128))
    return pl.pallas_call(
        body, name="ssd_fwd", grid=(nc,),
        in_specs=[pl.BlockSpec((BLK, SSM_W), lambda c: (c, 0)),
                  pl.BlockSpec((BLK, SG * SN), lambda c: (c, SSM_W // (SG * SN))),
                  pl.BlockSpec((BLK, SG * SN), lambda c: (c, SSM_W // (SG * SN) + 1)),
                  pl.BlockSpec((BLK, 128), lambda c: (c, 0)), vec, vec, vec],
        out_specs=[pl.BlockSpec((BLK, SSM_W), lambda c: (c, 0)), pl.BlockSpec((1, SN, SSM_W), lambda c: (c, 0, 0))],
        out_shape=[jax.ShapeDtypeStruct((t, SSM_W), BF16), jax.ShapeDtypeStruct((nc, SN, SSM_W), F32)],
        scratch_shapes=[pltpu.VMEM((SN, SSM_W), F32)],
        compiler_params=_cp(("arbitrary",)),
    )(act, act, act, dt_raw, dtb_p, alog_p, dsk_p)


def _head_sums(q):
    r = q.shape[1]
    lo = lax.broadcasted_iota(jnp.int32, (1, r, 128), 2) < SP
    s_lo = jnp.sum(jnp.where(lo, q, 0.0), axis=-1, keepdims=True)
    s_hi = jnp.sum(jnp.where(lo, 0.0, q), axis=-1, keepdims=True)
    lane = lax.broadcasted_iota(jnp.int32, (r, 128), 1)
    out = jnp.zeros((r, 128), F32)
    for k in range(NPAIR):
        out = jnp.where(lane == 2 * k, s_lo[k], jnp.where(lane == 2 * k + 1, s_hi[k], out))
    return out


def ssd_bwd(act, dt_raw, dy, sprev, dtb_p, alog_p, dsk_p):
    t = act.shape[0]
    nc = t // BLK

    def body(xs_ref, b_ref, c_ref, dt_ref, dy_ref, sp_ref, dtb_ref, al_ref, dk_ref,
             da_ref, ddt_ref, ddtb_ref, dal_ref, ddk_ref, dst):
        i = pl.program_id(0)

        @pl.when(i == 0)
        def _():
            dst[...] = jnp.zeros_like(dst)
            ddtb_ref[...] = jnp.zeros_like(ddtb_ref)
            dal_ref[...] = jnp.zeros_like(dal_ref)
            ddk_ref[...] = jnp.zeros_like(ddk_ref)

        dt_raw = dt_ref[...]
        dtb = dtb_ref[...]
        cm = _chunk_common(dt_raw, dtb, al_ref[...], dk_ref[...])
        ri = lax.broadcasted_iota(jnp.int32, (BLK, BLK), 0)
        ci = lax.broadcasted_iota(jnp.int32, (BLK, BLK), 1)
        lam_t = jnp.exp(jnp.where((ri <= ci)[None], cm.arow - cm.acol, NEG))
        bbs, cbs, cts, gms = [], [], [], []
        for g in range(SG):
            bf = b_ref[:, SN * g:SN * (g + 1)]
            cf = c_ref[:, SN * g:SN * (g + 1)]
            bbs.append(bf.astype(BF16))
            cbs.append(cf.astype(BF16))
            cts.append(cf.T.astype(BF16))
            gms.append(_dot_nt(bbs[g], cbs[g]))
        grp = lambda k: k // (NPAIR // SG)
        xs16 = _pairs(xs_ref[...])
        dy16 = _pairs(dy_ref[...].astype(F32))
        sp16 = _pairs(sp_ref[0])
        ds16 = _pairs(dst[...])
        xdt16 = xs16 * cm.dpl
        xdtb = xdt16.astype(BF16)
        dyh = [jnp.where(cm.lo, dy16, 0.0).astype(BF16), jnp.where(cm.lo, 0.0, dy16).astype(BF16)]
        m_t = (lam_t.reshape(SG, SR, BLK, BLK) * jnp.stack(gms)[:, None]).reshape(SH, BLK, BLK).astype(BF16)
        dxdt = jnp.stack([_dot(m_t[2 * k], dyh[0][k]) + _dot(m_t[2 * k + 1], dyh[1][k]) for k in range(NPAIR)])
        dm = jnp.stack([_dot_nt(dyh[h % 2][h // 2], xdtb[h // 2]) for h in range(SH)])
        dgl = (dm * cm.lam).reshape(SG, SR, BLK, BLK)
        dg = jnp.sum(dgl, axis=1).astype(BF16)
        w = (dgl * jnp.stack([_dot_nt(cbs[g], bbs[g]) for g in range(SG)])[:, None]).reshape(SH, BLK, BLK)
        w_rows = jnp.sum(w, axis=2, keepdims=True)
        w_cols = jnp.concatenate([jnp.sum(w, axis=1)] + [jnp.zeros((128 - SH, BLK), F32)], axis=0).T
        lane_c = lax.broadcasted_iota(jnp.int32, (BLK, 128), 1)
        da_cols = -w_cols
        for h in range(SH):
            da_cols = jnp.where(lane_c == h, da_cols + w_rows[h], da_cols)
        ds16b = ds16.astype(BF16)
        sp16b = sp16.astype(BF16)
        dxs = jnp.stack([_dot(bbs[grp(k)], ds16b[k]) for k in range(NPAIR)]) * cm.epl
        dxdt = dxdt + dxs
        dya = (dy16 * cm.eapl).astype(BF16)
        xe = (xdt16 * cm.epl).astype(BF16)
        dcs, dbs = [], []
        for g in range(SG):
            ks = range(g * (NPAIR // SG), (g + 1) * (NPAIR // SG))
            dcs.append(sum(_dot_nt(dya[k], sp16b[k]) for k in ks) + _dot(dg[g], bbs[g]))
            dbs.append(sum(_dot_nt(xe[k], ds16b[k]) for k in ks) + _dot_tn(dg[g], cbs[g]))
        dst[...] = _unpairs(cm.cdpl * ds16 + jnp.stack([_dot(cts[grp(k)], dya[k]) for k in range(NPAIR)]))
        da_ref[...] = jnp.concatenate([_unpairs(dxdt * cm.dpl + cm.dskpl * dy16)] + dbs + dcs, axis=1)
        y_off = jnp.stack([_dot(cbs[grp(k)], sp16b[k]) for k in range(NPAIR)]) * cm.eapl
        da_cols = da_cols + _head_sums(dy16 * y_off - xdt16 * dxs)
        last = _head_sums(jnp.sum(xdt16 * dxs, axis=1, keepdims=True)
                          + cm.cdpl * jnp.sum(ds16 * sp16, axis=1, keepdims=True))
        ddt = _head_sums(dxdt * xs16)
        row_i = lax.broadcasted_iota(jnp.int32, (BLK, 128), 0)
        dacum = da_cols + jnp.where(row_i == BLK - 1, last, 0.0)
        dda = _tri_mm((ri <= ci).astype(BF16), dacum)
        ddt = ddt + dda * cm.a
        dal_ref[...] += _csum(dda * cm.dtv) * cm.a
        ddt_raw = jnp.where(lane_c < SH, ddt * _sig(dt_raw + dtb), 0.0)
        ddt_ref[...] = ddt_raw.astype(BF16)
        ddtb_ref[...] += _csum(ddt_raw)
        ddk_ref[...] += _head_sums(jnp.sum(dy16 * xs16, axis=1, keepdims=True))

    rev = lambda i: nc - 1 - i
    vec = _full((1, 128))
    slab = pl.BlockSpec((BLK, SSM_W), lambda i: (rev(i), 0))
    return pl.pallas_call(
        body, name="ssd_bwd", grid=(nc,),
        in_specs=[slab,
                  pl.BlockSpec((BLK, SG * SN), lambda i: (rev(i), SSM_W // (SG * SN))),
                  pl.BlockSpec((BLK, SG * SN), lambda i: (rev(i), SSM_W // (SG * SN) + 1)),
                  pl.BlockSpec((BLK, 128), lambda i: (rev(i), 0)),
                  slab,
                  pl.BlockSpec((1, SN, SSM_W), lambda i: (rev(i), 0, 0)), vec, vec, vec],
        out_specs=[pl.BlockSpec((BLK, XBC), lambda i: (rev(i), 0)), pl.BlockSpec((BLK, 128), lambda i: (rev(i), 0)),
                   vec, vec, vec],
        out_shape=[jax.ShapeDtypeStruct((t, XBC), F32), jax.ShapeDtypeStruct((t, 128), BF16),
                   jax.ShapeDtypeStruct((1, 128), F32), jax.ShapeDtypeStruct((1, 128), F32),
                   jax.ShapeDtypeStruct((1, 128), F32)],
        scratch_shapes=[pltpu.VMEM((SN, SSM_W), F32)],
        compiler_params=_cp(("arbitrary",)),
    )(act, act, act, dt_raw, dy, sprev, dtb_p, alog_p, dsk_p)


TAIL_TM = 256


def _dsilu(z, s):
    return s * (1.0 + z * (1.0 - s))


def tail(proj, ao, yss, x, target, gate, ssm_nw, w_at, w_ss, w_ou):
    t = x.shape[0]
    tm = min(t, TAIL_TM)
    gw = SSM_W // SG

    def body(ao_ref, za_ref, ga_ref, gb_ref, zm_ref, ys_ref, x_ref, tg_ref, gt_ref, nw_ref, wa_ref, ws_ref, wo_ref,
             loss_ref, dy_ref, dao_ref, dmid_ref, dys_ref,
             ua_ref, yn_ref, mg_ref, dya_ref, dyb_ref, do_ref, dgt_ref, dnw_ref):
        i = pl.program_id(0)

        @pl.when(i == 0)
        def _():
            loss_ref[...] = jnp.zeros_like(loss_ref)
            dgt_ref[...] = jnp.zeros_like(dgt_ref)
            dnw_ref[...] = jnp.zeros_like(dnw_ref)

        ao = ao_ref[...].astype(F32)
        za = za_ref[...].astype(F32)
        sa = _sig(za)
        sila = za * sa
        ua_f = ao * sila
        ua = ua_f.astype(BF16)
        ya = _dot(ua, wa_ref[...])
        zm = zm_ref[...].astype(F32)
        sm = _sig(zm)
        silm = zm * sm
        ys = ys_ref[...].astype(F32)
        u = ys * silm
        nw = nw_ref[...]
        rs, uns = [], []
        for g in range(SG):
            ug = u[:, gw * g:gw * (g + 1)]
            r = lax.rsqrt(jnp.mean(ug * ug, axis=-1, keepdims=True) + EPS)
            rs.append(r)
            uns.append(ug * r)
        un = jnp.concatenate(uns, axis=1)
        yn_f = un * nw
        yn = yn_f.astype(BF16)
        yb = _dot(yn, ws_ref[...])
        sga = _sig(ga_ref[...].astype(F32))
        sgb = _sig(gb_ref[...].astype(F32))
        mg_f = sga * ya + sgb * yb
        mg = mg_f.astype(BF16)
        o = _dot(mg, wo_ref[...])
        gt = gt_ref[...]
        err = (x_ref[...] + gt * o) - tg_ref[...]
        lane = lax.broadcasted_iota(jnp.int32, (1, 128), 1)
        loss_ref[...] += jnp.where(lane == 0, 0.5 * _asum(_rsum(err * err) / D), 0.0)
        dy = err * (1.0 / D)
        dy_ref[...] = dy
        dgt_ref[...] += _csum(dy * o)
        do = (dy * gt).astype(BF16)
        dmg = _dot_nt(do, wo_ref[...])
        dmid_ref[:, C_GA - C_ZA:C_GB - C_ZA] = (dmg * ya * sga * (1.0 - sga)).astype(BF16)
        dmid_ref[:, C_GB - C_ZA:C_ZM - C_ZA] = (dmg * yb * sgb * (1.0 - sgb)).astype(BF16)
        dya = (dmg * sga).astype(BF16)
        dyb = (dmg * sgb).astype(BF16)
        dua = _dot_nt(dya, wa_ref[...])
        dao_ref[...] = (dua * sila).astype(BF16)
        dmid_ref[:, 0:C_GA - C_ZA] = (dua * ao * _dsilu(za, sa)).astype(BF16)
        dyn = _dot_nt(dyb, ws_ref[...])
        dnw_ref[...] += _csum(dyn * un)
        dun = dyn * nw
        dus = []
        for g in range(SG):
            gs = slice(gw * g, gw * (g + 1))
            dus.append(rs[g] * (dun[:, gs] - uns[g] * jnp.mean(dun[:, gs] * uns[g], axis=-1, keepdims=True)))
        du = jnp.concatenate(dus, axis=1)
        dys_ref[...] = (du * silm).astype(BF16)
        dmid_ref[:, C_ZM - C_ZA:] = (du * ys * _dsilu(zm, sm)).astype(BF16)
        ua_ref[...] = ua_f.T.astype(BF16)
        yn_ref[...] = yn_f.T.astype(BF16)
        mg_ref[...] = mg_f.T.astype(BF16)
        dya_ref[...] = dya
        dyb_ref[...] = dyb
        do_ref[...] = do

    row = lambda w: pl.BlockSpec((tm, w), lambda i: (i, 0))
    pcol = lambda w, c0: pl.BlockSpec((tm, w), lambda i: (i, c0 // w))
    sd = lambda w, dt: jax.ShapeDtypeStruct((t, w), dt)
    colt = lambda w: pl.BlockSpec((w, tm), lambda i: (0, i))
    sdt = lambda w: jax.ShapeDtypeStruct((w, t), BF16)
    return pl.pallas_call(
        body, name="tail", grid=(t // tm,),
        in_specs=[row(D), pcol(D, C_ZA), pcol(D, C_GA), pcol(D, C_GB), pcol(SSM_W, C_ZM), row(SSM_W), row(D), row(D),
                  _full((1, D)), _full((1, SSM_W)), _full((D, D)), _full((SSM_W, D)), _full((D, D))],
        out_specs=[_full((1, 128)), row(D), row(D), row(W_MID), row(SSM_W),
                   colt(D), colt(SSM_W), colt(D), row(D), row(D), row(D), _full((1, D)), _full((1, SSM_W))],
        out_shape=[jax.ShapeDtypeStruct((1, 128), F32), sd(D, F32), sd(D, BF16), sd(W_MID, BF16),
                   sd(SSM_W, BF16), sdt(D), sdt(SSM_W), sdt(D), sd(D, BF16),
                   sd(D, BF16), sd(D, BF16), jax.ShapeDtypeStruct((1, D), F32), jax.ShapeDtypeStruct((1, SSM_W), F32)],
        compiler_params=_cp(("arbitrary",)),
    )(ao, proj, proj, proj, proj, yss, x, target, gate, ssm_nw, w_at, w_ss, w_ou)


DPIECES = ((D, ((D, C_Q),)),
           (W_MID, ((D, C_ZA), (D, C_GA), (D, C_GB), (SSM_W, C_ZM))),
           (XBC, ((XBC, C_XBC),)),
           (512, ((512, C_K),)),
           (128, ((128, C_DT),)))


def dproj_bwd(pieces, wcat, x, dy, norm_w, scale):
    t = x.shape[0]
    tm = min(t, 256)
    nt = t // tm
    wblocks = [blk for _, subs in DPIECES for blk in subs]
    npc, nwb = len(DPIECES), len(wblocks)

    def body(*refs):
        p_refs, w_refs = refs[:npc], refs[npc:npc + nwb]
        x_ref, dy_ref, nw_ref, sc_ref, gx_ref, dnw_ref, dsc_ref, dsh_ref, dwe_ref = refs[npc + nwb:]
        i = pl.program_id(0)

        @pl.when(i == 0)
        def _():
            for ref in (dwe_ref, dsh_ref, dnw_ref, dsc_ref):
                ref[...] = jnp.zeros_like(ref)

        dh, wi = None, 0
        for p_ref, (_, subs) in zip(p_refs, DPIECES):
            loc = 0
            for w, _ in subs:
                part = _dot_nt(p_ref[:, loc:loc + w], w_refs[wi][...])
                dh = part if dh is None else dh + part
                loc += w
                wi += 1
        xv = x_ref[...]
        r = lax.rsqrt(jnp.mean(xv * xv, axis=-1, keepdims=True) + EPS)
        xn = xv * r
        weff = nw_ref[...] * (1.0 + sc_ref[...])
        dxn = dh * weff
        gx_ref[...] = dy_ref[...] + r * (dxn - xn * jnp.mean(dxn * xn, axis=-1, keepdims=True))
        dwe_ref[...] += _csum(dh * xn)
        dsh_ref[...] += _csum(dh)

        @pl.when(i == nt - 1)
        def _():
            dwe = dwe_ref[...]
            dnw_ref[...] = dwe * (1.0 + sc_ref[...])
            dsc_ref[...] = dwe * nw_ref[...]

    vec = pl.BlockSpec((1, D), lambda i: (0, 0))
    row = pl.BlockSpec((tm, D), lambda i: (i, 0))
    return pl.pallas_call(
        body, name="dproj_bwd", grid=(nt,),
        in_specs=[pl.BlockSpec((tm, pw), lambda i: (i, 0)) for pw, _ in DPIECES]
        + [pl.BlockSpec((D, w), functools.partial(lambda i, b: (0, b), b=off // w), pipeline_mode=pl.Buffered(1))
           for w, off in wblocks]
        + [row, row, vec, vec],
        out_specs=[row, vec, vec, vec],
        out_shape=[jax.ShapeDtypeStruct((t, D), F32), jax.ShapeDtypeStruct((1, D), F32),
                   jax.ShapeDtypeStruct((1, D), F32), jax.ShapeDtypeStruct((1, D), F32)],
        scratch_shapes=[pltpu.VMEM((1, D), F32)],
        compiler_params=_cp(("arbitrary",)),
    )(*pieces, *([wcat] * nwb), x, dy, norm_w, scale)


def wgrad(at, b, name, bn, after):
    m, t = at.shape
    n = b.shape[1]
    tk = min(t, 1024)
    bm = min(m, 1024)

    def body(a_ref, b_ref, after_ref, o_ref):
        part = _dot(a_ref[...], b_ref[...])

        @pl.when(pl.program_id(2) == 0)
        def _():
            o_ref[...] = part

        @pl.when(pl.program_id(2) > 0)
        def _():
            o_ref[...] += part

    return pl.pallas_call(
        body, name=name, grid=(m // bm, n // bn, t // tk),
        in_specs=[pl.BlockSpec((bm, tk), lambda i, j, k: (i, k)), pl.BlockSpec((tk, bn), lambda i, j, k: (k, j)), ANY],
        out_specs=pl.BlockSpec((bm, bn), lambda i, j, k: (i, j)),
        out_shape=jax.ShapeDtypeStruct((m, n), F32),
        compiler_params=_cp(("parallel", "parallel", "arbitrary")),
    )(at, b, after)


SUM_TR = 256


def pair_sum(g, core, theirs, name):
    w = g.shape[2]
    nh = HROWS // SUM_TR

    def body(core_ref, a_ref, b_ref, o_ref, ob_ref):
        s = a_ref[...] + b_ref[...]
        o_ref[...] = s
        ob_ref[...] = s.astype(BF16)

    spec = pl.BlockSpec((1, SUM_TR, w), lambda d, i, c: (d, i, 0))
    return pl.pallas_call(
        body, name=name,
        out_shape=[jax.ShapeDtypeStruct((4, HROWS, w), F32), jax.ShapeDtypeStruct((4, HROWS, w), BF16)],
        grid_spec=pltpu.PrefetchScalarGridSpec(
            num_scalar_prefetch=1, grid=(4, nh),
            in_specs=[pl.BlockSpec((1, SUM_TR, w), lambda d, i, c: (d, c[0] * nh + i, 0)), spec],
            out_specs=[spec, spec]),
        compiler_params=_cp(("parallel", "parallel")))(core.reshape(1).astype(jnp.int32), g, theirs)


def chip_sum(part, chip, others, name):
    r, w = part.shape[1:]

    def body(chip_ref, a_ref, b_ref, o_ref):
        acc = a_ref[0]
        for k in range(3):
            acc = acc + b_ref[k].astype(F32)
        o_ref[...] = acc

    return pl.pallas_call(
        body, name=name, out_shape=jax.ShapeDtypeStruct((r, w), F32),
        grid_spec=pltpu.PrefetchScalarGridSpec(
            num_scalar_prefetch=1, grid=(r // SUM_TR,),
            in_specs=[pl.BlockSpec((1, SUM_TR, w), lambda i, c: (c[0], i, 0)),
                      pl.BlockSpec((3, SUM_TR, w), lambda i, c: (0, i, 0))],
            out_specs=pl.BlockSpec((SUM_TR, w), lambda i, c: (i, 0))),
        compiler_params=_cp(("parallel",)))(chip.reshape(1).astype(jnp.int32), part, others)


def sum_devices(g):
    r = g.shape[1]

    def body(g_ref, o_ref):
        acc = g_ref[0]
        for d in range(1, 8):
            acc = acc + g_ref[d]
        o_ref[...] = acc

    return pl.pallas_call(body, name="sum_devices", out_shape=jax.ShapeDtypeStruct((r, 1024), F32),
                          compiler_params=_cp())(g)


def adamw(w, g, m, v, name):
    r, c = w.shape
    tr = r
    for cand in (256, 128, 64, 32, 16, 8):
        if r % cand == 0 and r > cand:
            tr = cand
            break

    def body(w_ref, g_ref, m_ref, v_ref, d_ref, nm_ref, nv_ref):
        gv = g_ref[...]
        mn = ADAM_B1 * m_ref[...] + (1.0 - ADAM_B1) * gv
        vn = ADAM_B2 * v_ref[...] + (1.0 - ADAM_B2) * (gv * gv)
        m_hat = mn / (1.0 - ADAM_B1 ** ADAM_STEP)
        v_hat = vn / (1.0 - ADAM_B2 ** ADAM_STEP)
        d_ref[...] = -ADAM_LR * (m_hat / (jnp.sqrt(v_hat) + ADAM_EPS) + ADAM_WD * w_ref[...])
        nm_ref[...] = mn
        nv_ref[...] = vn

    spec = pl.BlockSpec((tr, c), lambda i: (i, 0))
    sd = jax.ShapeDtypeStruct((r, c), F32)
    return pl.pallas_call(body, name=name, grid=(r // tr,), in_specs=[spec] * 4, out_specs=[spec] * 3,
                          out_shape=[sd, sd, sd], compiler_params=_cp(("parallel",)))(w, g, m, v)


ANY = pl.BlockSpec(memory_space=pl.ANY)
VM = pl.BlockSpec(memory_space=pltpu.VMEM)
OTHER_CHIPS = ((1, 0), (0, 1), (1, 1))


def _pos():
    return lax.axis_index("x"), lax.axis_index("y"), lax.axis_index("c")


def _flip(v, bit):
    return 1 - v if bit else v


def _rcopy(src, dst, ssem, rsem, peer):
    return pltpu.make_async_remote_copy(src_ref=src, dst_ref=dst, send_sem=ssem, recv_sem=rsem,
                                        device_id=peer, device_id_type=MESH)


def allgather_small(p, name):
    r = p.shape[0]

    def body(in_ref, out_ref, ssem, rsem, lsem):
        x, y, c = _pos()
        me = 4 * x + 2 * y + c
        loc = pltpu.make_async_copy(in_ref, out_ref.at[me], lsem)
        loc.start()
        sends = []
        peers = []
        for k in range(1, 8):
            px, py, pc = _flip(x, (k >> 2) & 1), _flip(y, (k >> 1) & 1), _flip(c, k & 1)
            peers.append((px, py, pc))
            cp = _rcopy(in_ref, out_ref.at[me], ssem.at[k - 1], rsem.at[k - 1], (px, py, pc))
            cp.start()
            sends.append(cp)
        for k in range(1, 8):
            px, py, pc = peers[k - 1]
            _rcopy(in_ref, out_ref.at[4 * px + 2 * py + pc], ssem.at[k - 1], rsem.at[k - 1], (px, py, pc)).wait_recv()
        for cp in sends:
            cp.wait_send()
        loc.wait()

    return pl.pallas_call(
        body, name=name, out_shape=jax.ShapeDtypeStruct((8, r, 1024), F32),
        in_specs=[VM], out_specs=VM,
        scratch_shapes=[pltpu.SemaphoreType.DMA((7,)), pltpu.SemaphoreType.DMA((7,)), pltpu.SemaphoreType.DMA],
    )(p)


def gather_weights(w_in_b, mod_sh):
    def body(wi_ref, m_ref, gi_ref, mo_ref, ssem, rsem, lsem):
        x, y, c = _pos()
        chip = 2 * x + y
        mine = pl.ds(pl.multiple_of(c * HROWS, 16), HROWS)
        other = pl.ds(pl.multiple_of((1 - c) * HROWS, 16), HROWS)
        sib = (x, y, 1 - c)
        pairs = ((wi_ref, gi_ref),)
        loc_m = pltpu.make_async_copy(m_ref, mo_ref.at[chip], lsem)
        loc_m.start()
        sends = []
        for k, (fx, fy) in enumerate(OTHER_CHIPS):
            peer = (_flip(x, fx), _flip(y, fy), c)
            for a, (w_ref, g_ref) in enumerate(pairs):
                cw = _rcopy(w_ref.at[mine], g_ref.at[chip, mine], ssem.at[6 * a + k], rsem.at[6 * a + k], peer)
                cw.start()
                sends.append(cw)
            cm = _rcopy(m_ref, mo_ref.at[chip], ssem.at[12 + k], rsem.at[12 + k], peer)
            cm.start()
            sends.append(cm)
        for k, (fx, fy) in enumerate(OTHER_CHIPS):
            px, py = _flip(x, fx), _flip(y, fy)
            for a, (w_ref, g_ref) in enumerate(pairs):
                got = g_ref.at[2 * px + py, mine]
                _rcopy(w_ref.at[mine], got, ssem.at[6 * a + k], rsem.at[6 * a + k], (px, py, c)).wait_recv()
                fw = _rcopy(got, got, ssem.at[6 * a + 3 + k], rsem.at[6 * a + 3 + k], sib)
                fw.start()
                sends.append(fw)
        for k, (fx, fy) in enumerate(OTHER_CHIPS):
            px, py = _flip(x, fx), _flip(y, fy)
            for a, (w_ref, g_ref) in enumerate(pairs):
                land = g_ref.at[2 * px + py, other]
                _rcopy(land, land, ssem.at[6 * a + 3 + k], rsem.at[6 * a + 3 + k], sib).wait_recv()
            _rcopy(m_ref, mo_ref.at[2 * px + py], ssem.at[12 + k], rsem.at[12 + k], (px, py, c)).wait_recv()
        for cp in sends:
            cp.wait_send()
        loc_m.wait()

    return pl.pallas_call(
        body, name="gather_weights",
        out_shape=[jax.ShapeDtypeStruct((4, D, SH_IN), BF16), jax.ShapeDtypeStruct((4, 8, 768), F32)],
        in_specs=[ANY, VM], out_specs=[ANY, VM],
        scratch_shapes=[pltpu.SemaphoreType.DMA((15,)), pltpu.SemaphoreType.DMA((15,)), pltpu.SemaphoreType.DMA],
    )(w_in_b, mod_sh)


def pair_exchange(g):
    def body(g_ref, r_ref, ssem, rsem):
        x, y, c = _pos()
        other = pl.ds(pl.multiple_of((1 - c) * HROWS, 8), HROWS)
        cp = _rcopy(g_ref.at[:, other, :], r_ref, ssem, rsem, (x, y, 1 - c))
        cp.start()
        cp.wait()

    return pl.pallas_call(
        body, name="pair_exchange", out_shape=jax.ShapeDtypeStruct((4, HROWS, g.shape[2]), F32),
        in_specs=[ANY], out_specs=ANY,
        scratch_shapes=[pltpu.SemaphoreType.DMA, pltpu.SemaphoreType.DMA],
    )(g)


HBM = pl.BlockSpec(memory_space=pltpu.HBM)
SEM = pl.BlockSpec(memory_space=pltpu.SEMAPHORE)
DATAFLOW = pltpu.SideEffectType.DATAFLOW_SIDE_EFFECTING


def split_start(name, make_copies, srcs, lands, nsem, after):
    arrays = [*srcs, *lands]
    n, ns = len(arrays), len(srcs)

    def body(*refs):
        for cp in make_copies(refs[:ns], refs[ns:n], refs[n + 1], refs[n + 2])[0]:
            cp.start()
        refs[-1][...] = jnp.zeros_like(refs[-1])

    res = pl.pallas_call(
        body, name=name,
        out_shape=(pltpu.SemaphoreType.DMA((nsem,)), pltpu.SemaphoreType.DMA((nsem,)),
                   *[pltpu.HBM(a.shape, a.dtype) for a in arrays], jax.ShapeDtypeStruct((8, 128), F32)),
        in_specs=(HBM,) * n + (ANY,), out_specs=(SEM, SEM) + (HBM,) * n + (VM,),
        input_output_aliases={i: 2 + i for i in range(n)},
        compiler_params=pltpu.CompilerParams(has_side_effects=DATAFLOW),
    )(*[pltpu.with_memory_space_constraint(a, pltpu.HBM) for a in arrays], after)
    return res[0], res[1], list(res[2:2 + n]), res[-1]


def split_wait(name, make_copies, ssem, rsem, arrays, ns, after):
    n = len(arrays)

    def body(*refs):
        sends, recvs = make_copies(refs[:ns], refs[ns:n], refs[n], refs[n + 1])
        for cp in sends:
            cp.wait_send()
        for cp in recvs:
            cp.wait_recv()

    return pl.pallas_call(
        body, name=name, out_shape=tuple(pltpu.HBM(a.shape, a.dtype) for a in arrays),
        in_specs=(HBM,) * n + (SEM, SEM, ANY), out_specs=(HBM,) * n,
        input_output_aliases={i: i for i in range(n)},
        compiler_params=pltpu.CompilerParams(has_side_effects=DATAFLOW),
    )(*arrays, ssem, rsem, after)


def _chip_copies(srcs, lands, ssem, rsem):
    x, y, c = _pos()
    copies = []
    for k, (fx, fy) in enumerate(OTHER_CHIPS):
        px, py = _flip(x, fx), _flip(y, fy)
        for a, (p_ref, l_ref) in enumerate(zip(srcs, lands)):
            copies.append(_rcopy(p_ref.at[2 * px + py], l_ref.at[k], ssem.at[3 * a + k], rsem.at[3 * a + k], (px, py, c)))
    return copies, copies


def _pair_copies(srcs, lands, ssem, rsem):
    x, y, c = _pos()
    other = pl.ds(pl.multiple_of((1 - c) * HROWS, 8), HROWS)
    copies = [_rcopy(srcs[0].at[:, other, :], lands[0], ssem.at[0], rsem.at[0], (x, y, 1 - c))]
    return copies, copies


def _rest_copies(srcs, lands, ssem, rsem):
    x, y, c = _pos()
    chip = 2 * x + y
    mine = pl.ds(pl.multiple_of(c * HROWS, 16), HROWS)
    sends, recvs = [], []
    for k, (fx, fy) in enumerate(OTHER_CHIPS):
        px, py = _flip(x, fx), _flip(y, fy)
        for t in range(2):
            rows_t = pl.ds(t * HROWS, HROWS)
            sends.append(_rcopy(srcs[0].at[mine], lands[0].at[chip, mine], ssem.at[2 * k + t], rsem.at[2 * k + c],
                                (px, py, t)))
            recvs.append(_rcopy(srcs[0].at[rows_t], lands[0].at[2 * px + py, rows_t], ssem.at[2 * k + t],
                                rsem.at[2 * k + t], (px, py, t)))
    return sends, recvs


def pair_swap(red_in, red_rest):
    def body(ai_ref, ar_ref, oi_ref, or_ref, ssem, rsem):
        x, y, c = _pos()
        cps = [_rcopy(a_ref, o_ref, ssem.at[a], rsem.at[a], (x, y, 1 - c))
               for a, (a_ref, o_ref) in enumerate(((ai_ref, oi_ref), (ar_ref, or_ref)))]
        for cp in cps:
            cp.start()
        for cp in cps:
            cp.wait()

    return pl.pallas_call(
        body, name="pair_swap",
        out_shape=[jax.ShapeDtypeStruct((HROWS, SH_IN), F32), jax.ShapeDtypeStruct((HROWS, D), F32)],
        in_specs=[ANY, ANY], out_specs=[ANY, ANY],
        scratch_shapes=[pltpu.SemaphoreType.DMA((2,)), pltpu.SemaphoreType.DMA((2,))],
    )(red_in, red_rest)


def _flat(v, width=1024):
    v = v.reshape(-1)
    n = -(-v.shape[0] // width) * width
    return jnp.pad(v, (0, n - v.shape[0]))


def _rows(parts, rows):
    flat = jnp.concatenate(parts)
    return jnp.pad(flat, (0, rows * 1024 - flat.shape[0])).reshape(rows, 1024)


def _pack_small(b_ada, norm_w, conv_b, ssm_norm_w, q_norm_w, k_norm_w, sinks, dt_bias, a_log, d_skip, rel_bias,
                extra=None, tail=(), rows=16):
    misc = [q_norm_w, k_norm_w, sinks, dt_bias, a_log, d_skip] + ([] if extra is None else [extra])
    parts = [_flat(b_ada), _flat(norm_w), _flat(conv_b), _flat(ssm_norm_w)] + [_flat(v, 128) for v in misc]
    parts.append(jnp.zeros(((8 - len(misc)) * 128,), F32))
    parts.append(_flat(rel_bias))
    parts.append(jnp.zeros((5 * 1024,), F32))
    return _rows(parts + [_flat(v) for v in tail], rows)


def _unpack_small(p):
    misc = p[9]
    return dict(b_ada=p[0:3].reshape(1, 3072), norm_w=p[3:4], conv_b=p[4:7].reshape(1, 3072),
                ssm_norm_w=p[7:9].reshape(1, 2048), q_norm_w=misc[None, 0:64], k_norm_w=misc[None, 128:192],
                sinks=misc[None, 256:272], dt_bias=misc[None, 384:416], a_log=misc[None, 512:544],
                d_skip=misc[None, 640:672], rel_bias=p[10, :512].reshape(32, 16), extra=misc[768])


SMALL = ("b_ada", "norm_w", "conv_b", "ssm_norm_w", "q_norm_w", "k_norm_w", "sinks", "dt_bias", "a_log", "d_skip",
         "rel_bias")
WEIGHTS = ("w_ada", "b_ada", "norm_w", "w_in", "q_norm_w", "k_norm_w", "rel_bias", "sinks", "conv_w", "conv_b",
           "dt_bias", "a_log", "d_skip", "ssm_norm_w", "w_attn_proj", "w_ssm_proj", "w_out")
IN_COLS = ((0, 1024, C_Q), (1024, 256, C_K), (1280, 256, C_V), (1536, 1024, C_ZA), (2560, 2048, C_ZM),
           (4608, 3072, C_XBC), (7680, 32, C_DT), (7712, 1024, C_GA), (8736, 1024, C_GB))


def _to_cat(shards):
    parts, pos = [], 0
    for o, n, cnew in sorted(IN_COLS, key=lambda e: e[2]):
        assert cnew == pos
        c0 = o
        while c0 < o + n:
            i = c0 // SH_IN
            c1 = min(o + n, (i + 1) * SH_IN)
            parts.append(shards[i][:, c0 - i * SH_IN:c1 - i * SH_IN])
            c0 = c1
        pos += n
    parts.append(jnp.zeros((D, NP - pos), shards.dtype))
    return jnp.concatenate(parts, axis=1)


def _from_cat(dw_pieces):
    starts = [subs[0][1] for _, subs in DPIECES]

    def cols(c0, c1):
        p = max(q for q in range(len(starts)) if starts[q] <= c0)
        return dw_pieces[p][:, c0 - starts[p]:c1 - starts[p]]

    shards = []
    for i in range(4):
        lo, hi = i * SH_IN, (i + 1) * SH_IN
        parts = []
        for o, n, cnew in IN_COLS:
            a, b = max(o, lo), min(o + n, hi)
            if a < b:
                parts.append(cols(cnew + a - o, cnew + b - o))
        shards.append(jnp.concatenate(parts, axis=1))
    return jnp.stack(shards)


def kernel(x, c, w_ada, b_ada, norm_w, w_in, q_norm_w, k_norm_w, rel_bias, sinks, conv_w, conv_b, dt_bias, a_log, d_skip, ssm_norm_w, w_attn_proj, w_ssm_proj, w_out, loss_target, m_w_ada, m_b_ada, m_norm_w, m_w_in, m_q_norm_w, m_k_norm_w, m_rel_bias, m_sinks, m_conv_w, m_conv_b, m_dt_bias, m_a_log, m_d_skip, m_ssm_norm_w, m_w_attn_proj, m_w_ssm_proj, m_w_out, v_w_ada, v_b_ada, v_norm_w, v_w_in, v_q_norm_w, v_k_norm_w, v_rel_bias, v_sinks, v_conv_w, v_conv_b, v_dt_bias, v_a_log, v_d_skip, v_ssm_norm_w, v_w_attn_proj, v_w_ssm_proj, v_w_out):
    args = dict(locals())
    xi, yi, ci = lax.axis_index("x"), lax.axis_index("y"), lax.axis_index("c")
    chip = 2 * xi + yi
    me = 4 * xi + 2 * yi + ci
    x2 = x[0]
    tgt = loss_target[0]

    pay = _rows([c.reshape(-1), conv_w[0].reshape(-1)], 8)
    g0 = allgather_small(pay, "gather_cond")
    c_all = g0[:, 0, :]
    conv_w_full = g0[0::2, 1:4, :].reshape(4, CONV_K, 768).transpose(1, 0, 2).reshape(CONV_K, XBC)

    b_ada_sh = lax.dynamic_slice(b_ada, (0, chip * 768), (1, 768))
    mod_sh = ada_mod(c_all, w_ada[0], b_ada_sh)

    w_in_b = w_in[0].astype(BF16)
    w_rest_b = jnp.concatenate([w_attn_proj[0], w_ssm_proj[0], w_out[0]], axis=0).astype(BF16)
    wg_in, modg = gather_weights(w_in_b, mod_sh)
    wg_in = lax.dynamic_update_slice(wg_in, w_in_b[None], (chip, 0, 0))
    rs_sem, rr_sem, rest_thru, rest_tok = split_start("gather_rest_start", _rest_copies, [w_rest_b],
                                                      [lax.empty((4, D, D), BF16)], 6, modg)
    mod = lax.dynamic_slice(modg, (0, me, 0), (4, 1, 768)).reshape(1, 3 * D)
    shift, scale, gate = mod[:, :D], mod[:, D:2 * D] + rest_tok[:1, :1], mod[:, 2 * D:]
    wcat = _to_cat(wg_in)

    pad128 = lambda v: jnp.pad(v, ((0, 0), (0, 128 - v.shape[1])))
    dtb_p, alog_p, dsk_p = pad128(dt_bias), pad128(a_log), pad128(d_skip)
    bucket = _bucket_table()

    proj, dt_raw, h_t = norm_proj(x2, norm_w, scale, shift, wcat)
    biasm = bias_expand(rel_bias, sinks, bucket)
    ao = attn_fwd(proj, biasm, q_norm_w, k_norm_w)
    act, dsl = conv_fwd(proj, conv_w_full, conv_b)
    yss, sprev = ssd_fwd(act, dt_raw, dtb_p, alog_p, dsk_p)

    w_rest_b, wg_rest = split_wait("gather_rest_wait", _rest_copies, rs_sem, rr_sem, rest_thru, 1, yss)
    wg_rest = lax.dynamic_update_slice(wg_rest, w_rest_b[None], (chip, 0, 0))
    w_at = wg_rest[:, :R_AT].reshape(D, D)
    w_ss = wg_rest[:, R_AT:R_AT + R_SS].reshape(SSM_W, D)
    w_ou = wg_rest[:, R_AT + R_SS:].reshape(D, D)
    (loss_p, dy, dao, dmid, dyss, ua_t, yn_t, mg_t, dya, dyb, dout, dgate, dssm_nw) = tail(
        proj, ao, yss, x2, tgt, gate, ssm_norm_w, w_at, w_ss, w_ou)

    dq, dkv, dqw, dkw, dacc = attn_bwd(proj, dao, biasm, q_norm_w, k_norm_w)
    dbias = bias_reduce(dacc, bucket)
    drb = dbias[:, :NBUCKET].T
    dsk = dbias[:, NBUCKET].reshape(1, HQ)
    dact, ddt, ddtb, dalog, ddskip = ssd_bwd(act, dt_raw, dyss, sprev, dtb_p, alog_p, dsk_p)
    dxbc, dconv_w, dconv_b = conv_bwd(proj, dact, dsl, conv_w_full)

    dproj = (dq, dmid, dxbc, dkv, ddt)
    dwcat = [wgrad(h_t, piece, "dw_in_%d" % p, min(piece.shape[1], 1024), rest_tok) for p, piece in enumerate(dproj)]

    g_in = _from_cat(dwcat)
    ps_sem, pr_sem, pair_thru, pair_tok = split_start("pair_in_start", _pair_copies, [g_in],
                                                      [lax.empty((4, HROWS, SH_IN), F32)], 1, loss_p)
    dw_at = wgrad(ua_t, dya, "dw_attn", 512, pair_tok)
    dw_ss = wgrad(yn_t, dyb, "dw_ssm", 512, pair_tok)
    dw_ou = wgrad(mg_t, dout, "dw_out", 512, pair_tok)
    g_rest = jnp.concatenate([dw_at.reshape(4, R_AT, D), dw_ss.reshape(4, R_SS, D), dw_ou.reshape(4, R_OU, D)], axis=1)
    sib_rest = pair_exchange(g_rest)
    g_in, sib_in = split_wait("pair_in_wait", _pair_copies, ps_sem, pr_sem, pair_thru, 1, sib_rest)
    part_in, pb_in = pair_sum(g_in, ci, sib_in, "pair_sum_in")
    part_rest, pb_rest = pair_sum(g_rest, ci, sib_rest, "pair_sum_rest")
    cs_sem, cr_sem, chip_thru, token = split_start(
        "chip_exchange_start", _chip_copies, [pb_in, pb_rest],
        [lax.empty((3, HROWS, SH_IN), BF16), lax.empty((3, HROWS, D), BF16)], 6, part_rest)
    grad_x, dnorm_w, dscale, dshift = dproj_bwd(dproj, wcat, x2, dy, norm_w, scale + token[:1, :1])
    _, _, oth_in, oth_rest = split_wait("chip_exchange_wait", _chip_copies, cs_sem, cr_sem, chip_thru, 2, dshift)
    red_in = chip_sum(part_in, chip, oth_in, "chip_sum_in")
    red_rest = chip_sum(part_rest, chip, oth_rest, "chip_sum_rest")
    recv_in, recv_rest = pair_swap(red_in, red_rest)
    both = lambda mine, theirs: jnp.concatenate([jnp.where(ci == 0, mine, theirs), jnp.where(ci == 0, theirs, mine)],
                                                axis=0)
    g_shard_in = both(red_in, recv_in)
    g_shard_rest = both(red_rest, recv_rest)

    dmod = jnp.concatenate([dshift, dscale, dgate], axis=1)
    gsmall = _pack_small(dmod, dnorm_w, dconv_b, dssm_nw, dqw, dkw, dsk[:, :HQ], ddtb[:, :SH], dalog[:, :SH],
                         ddskip[:, :SH], drb, extra=loss_p[:, :1], tail=(dconv_w,), rows=32)
    gall = allgather_small(gsmall, "gather_small_grads")
    ssum = sum_devices(gall)
    gs = _unpack_small(ssum[:16])
    loss = gs["extra"]
    dconv_w_sh = lax.dynamic_slice(ssum[16:28].reshape(CONV_K, XBC), (0, chip * 768), (CONV_K, 768))
    dmod_all = gall[:, 0:3, :].reshape(8, 3 * D)
    dw_ada = ada_grad(c_all, lax.dynamic_slice(dmod_all, (0, chip * 768), (8, 768)))

    grads = dict(gs)
    grads["w_ada"] = dw_ada
    grads["w_in"] = g_shard_in
    grads["w_attn_proj"] = g_shard_rest[:R_AT]
    grads["w_ssm_proj"] = g_shard_rest[R_AT:R_AT + R_SS]
    grads["w_out"] = g_shard_rest[R_AT + R_SS:]
    grads["conv_w"] = dconv_w_sh

    delta, new_m, new_v = {}, {}, {}
    for n in ("w_ada", "w_in", "conv_w", "w_attn_proj", "w_ssm_proj", "w_out"):
        delta[n], new_m[n], new_v[n] = adamw(args[n][0], grads[n], args["m_" + n][0], args["v_" + n][0], "adamw_" + n)
    ws = _pack_small(*[args[n] for n in SMALL])
    ms = _pack_small(*[args["m_" + n] for n in SMALL])
    vs = _pack_small(*[args["v_" + n] for n in SMALL])
    d_s, m_s, v_s = adamw(ws, ssum[:16], ms, vs, "adamw_small")
    d_s, m_s, v_s = _unpack_small(d_s), _unpack_small(m_s), _unpack_small(v_s)
    for n in SMALL:
        delta[n], new_m[n], new_v[n] = d_s[n], m_s[n], v_s[n]

    def shaped(n, a):
        return a.reshape(args[n].shape)

    outs = [loss, grad_x[None]]
    for table in (grads, delta, new_m, new_v):
        outs += [shaped(n, table[n]) for n in WEIGHTS]
    return tuple(outs)
```

```python
import functools
import math

import jax
import jax.numpy as jnp
from jax import lax
from jax.experimental import pallas as pl
from jax.experimental.pallas import tpu as pltpu

F32 = jnp.float32
BF16 = jnp.bfloat16
MESH = pl.DeviceIdType.MESH

D = 1024
HQ, HKV, GRP, DH = 16, 4, 4, 64
BLK = 128
NBUCKET, MAXDIST = 32, 128
SSM_W, SH, SG, SR, SP, SN = 2048, 32, 4, 8, 64, 128
CONV_K = 4
XBC = SSM_W + 2 * SG * SN
IN_W = 9760
EPS = 1e-6
NEG = -1e30
SCALE = DH ** -0.5

C_Q, C_ZA, C_GA, C_GB, C_ZM, C_XBC, C_K, C_V, C_DT = 0, 1024, 2048, 3072, 4096, 6144, 9216, 9472, 9728
NP = 9984
TN = 1664
W_MID = C_XBC - C_ZA

SH_IN = IN_W // 4
R_AT, R_SS, R_OU = 256, 512, 256
HROWS = D // 2

ADAM_LR, ADAM_B1, ADAM_B2, ADAM_EPS, ADAM_WD, ADAM_STEP = 0.001, 0.9, 0.999, 1e-08, 0.01, 10

VMEM_LIMIT = 56 * 1024 * 1024


def _cp(sem=None):
    if sem is None:
        return pltpu.CompilerParams(vmem_limit_bytes=VMEM_LIMIT)
    return pltpu.CompilerParams(dimension_semantics=sem, vmem_limit_bytes=VMEM_LIMIT)


def _sig(x):
    return 0.5 * jnp.tanh(0.5 * x) + 0.5


def _dot(a, b):
    return jnp.dot(a, b, preferred_element_type=F32)


def _dot_nt(a, b):
    return lax.dot_general(a, b, (((1,), (1,)), ((), ())), preferred_element_type=F32)


def _dot_tn(a, b):
    return lax.dot_general(a, b, (((0,), (0,)), ((), ())), preferred_element_type=F32)


def _rsum(x):
    return jnp.sum(x, axis=-1, keepdims=True)


def _csum(x):
    return jnp.sum(x, axis=0, keepdims=True)


def _asum(x):
    return _csum(_rsum(x))


def _full(shape):
    nd = len(shape)
    return pl.BlockSpec(shape, lambda *_: (0,) * nd)


def ada_mod(c_all, w_ada_sh, b_ada_sh):
    def body(c_ref, w_ref, b_ref, o_ref):
        cv = c_ref[...]
        s = cv * _sig(cv)
        o_ref[...] = jnp.dot(s, w_ref[...], preferred_element_type=F32,
                             precision=lax.Precision.HIGHEST) + b_ref[...]

    n = w_ada_sh.shape[1]
    return pl.pallas_call(body, name="ada_mod", out_shape=jax.ShapeDtypeStruct((8, n), F32),
                          compiler_params=_cp())(c_all, w_ada_sh, b_ada_sh)


def ada_grad(c_all, dmod_sh):
    def body(c_ref, d_ref, o_ref):
        cv = c_ref[...]
        s = cv * _sig(cv)
        o_ref[...] = lax.dot_general(s, d_ref[...], (((0,), (0,)), ((), ())), preferred_element_type=F32,
                                     precision=lax.Precision.HIGHEST)

    n = dmod_sh.shape[1]
    return pl.pallas_call(body, name="ada_grad", out_shape=jax.ShapeDtypeStruct((D, n), F32),
                          compiler_params=_cp())(c_all, dmod_sh)


def norm_proj(x, norm_w, scale, shift, wcat):
    t = x.shape[0]
    tm = min(t, 1024)

    def body(x_ref, nw_ref, sc_ref, sh_ref, w_ref, p_ref, dt_ref, ht_ref, hs):
        @pl.when(pl.program_id(1) == 0)
        def _():
            xv = x_ref[...]
            r = lax.rsqrt(jnp.mean(xv * xv, axis=-1, keepdims=True) + EPS)
            h = (xv * r) * nw_ref[...]
            h = h * (1.0 + sc_ref[...]) + sh_ref[...]
            hs[...] = h.astype(BF16)
            ht_ref[...] = h.T.astype(BF16)

        p = _dot(hs[...], w_ref[...])
        p_ref[...] = p.astype(BF16)

        @pl.when(pl.program_id(1) == C_DT // TN)
        def _():
            dt_ref[...] = p[:, C_DT % TN:C_DT % TN + 128]

    vec = pl.BlockSpec((1, D), lambda i, j: (0, 0))
    return pl.pallas_call(
        body, name="norm_proj", grid=(t // tm, NP // TN),
        in_specs=[pl.BlockSpec((tm, D), lambda i, j: (i, 0)), vec, vec, vec,
                  pl.BlockSpec((D, TN), lambda i, j: (0, j))],
        out_specs=[pl.BlockSpec((tm, TN), lambda i, j: (i, j)), pl.BlockSpec((tm, 128), lambda i, j: (i, 0)),
                   pl.BlockSpec((D, tm), lambda i, j: (0, i))],
        out_shape=[jax.ShapeDtypeStruct((t, NP), BF16), jax.ShapeDtypeStruct((t, 128), F32),
                   jax.ShapeDtypeStruct((D, t), BF16)],
        scratch_shapes=[pltpu.VMEM((tm, D), BF16)],
        compiler_params=_cp(("parallel", "arbitrary")),
    )(x, norm_w, scale, shift, wcat)


def _bucket_table():
    qi = jnp.arange(BLK)[:, None]
    kj = jnp.arange(2 * BLK)[None, :]
    dist = qi + BLK - kj
    n = jnp.maximum(dist, 0)
    max_exact = NBUCKET // 2
    nf = jnp.maximum(n, 1).astype(F32)
    large = max_exact + (jnp.log(nf / max_exact) / math.log(MAXDIST / max_exact)
                         * (NBUCKET - max_exact)).astype(jnp.int32)
    large = jnp.minimum(large, NBUCKET - 1)
    bucket = jnp.where(n < max_exact, n, large).astype(jnp.int32)
    valid = (dist >= 0) & (dist < BLK)
    return jnp.where(valid, bucket, -1)


def bias_expand(rel_bias, sinks, bucket):
    def body(rb_ref, sk_ref, bk_ref, o_ref):
        hd = pl.program_id(0)
        bk = bk_ref[...]
        col = lax.broadcasted_iota(jnp.int32, (BLK, 2 * BLK), 1)

        def step(b, acc):
            return jnp.where(bk == b, rb_ref[b, hd], acc)

        acc = lax.fori_loop(0, NBUCKET, step, jnp.full((BLK, 2 * BLK), NEG, F32))
        acc = jnp.where(col == 0, sk_ref[0, hd], acc)
        o_ref[1, 0] = acc
        o_ref[0, 0] = jnp.where(jnp.logical_and(col > 0, col < BLK), NEG, acc)

    smem = pl.BlockSpec(memory_space=pltpu.SMEM)
    return pl.pallas_call(
        body, name="bias_expand", grid=(HQ,),
        in_specs=[smem, smem, _full((BLK, 2 * BLK))],
        out_specs=pl.BlockSpec((2, 1, BLK, 2 * BLK), lambda h: (0, h, 0, 0)),
        out_shape=jax.ShapeDtypeStruct((2, HQ, BLK, 2 * BLK), F32),
        compiler_params=_cp(("arbitrary",)),
    )(rel_bias, sinks, bucket)


def bias_reduce(dacc, bucket):
    col = jnp.arange(BLK * 2 * BLK, dtype=jnp.int32) % (2 * BLK)
    lane = jnp.arange(128, dtype=jnp.int32)[None, :]
    member = (bucket.reshape(-1)[:, None] == lane) | ((col[:, None] == 0) & (lane == NBUCKET))

    def body(d_ref, m_ref, o_ref):
        mm = m_ref[...]
        o_ref[...] = sum(_dot(part, mm) for part in _split3(d_ref[...]))

    return pl.pallas_call(body, name="bias_reduce", out_shape=jax.ShapeDtypeStruct((HQ, 128), F32),
                          compiler_params=_cp())(dacc.reshape(HQ, BLK * 2 * BLK), member.astype(BF16))


GQ = GRP * BLK


def _stack_heads(x, nh):
    return jnp.concatenate([x[:, DH * h:DH * (h + 1)] for h in range(nh)], axis=0)


def _unstack(xs, nh):
    rows = xs.shape[0] // nh
    return jnp.concatenate([xs[rows * h:rows * (h + 1)] for h in range(nh)], axis=1)


def _rms(x):
    return lax.rsqrt(jnp.mean(x * x, axis=-1, keepdims=True) + EPS)


def _stack_q(q, qw):
    qs = _stack_heads(q, HQ)
    r = _rms(qs)
    qhat = qs * r
    return qhat * qw, qhat, r


def _band_first(shape):
    return (lax.broadcasted_iota(jnp.int32, shape, 0) & (2 * BLK - 1)) == 0


def _stack_kv(kp, kc, vp, vc, kw):
    ks = _stack_heads(jnp.concatenate([kp, kc], axis=0), HKV)
    r = _rms(ks)
    khat = ks * r
    first = _band_first(ks.shape)
    kn = jnp.where(first, 0.0, khat * kw)
    v2 = jnp.where(first, 0.0, _stack_heads(jnp.concatenate([vp, vc], axis=0), HKV)).astype(BF16)
    return kn, khat, r, v2


def _softmax_rows(s):
    p = jnp.exp(s - jnp.max(s, axis=-1, keepdims=True))
    return p * (1.0 / _rsum(p))


def attn_fwd(proj, biasm, q_norm_w, k_norm_w):
    t = proj.shape[0]
    nb = t // BLK

    def body(q_ref, kc_ref, kp_ref, vc_ref, vp_ref, bm_ref, qw_ref, kw_ref, o_ref):
        f = lambda ref: ref[...].astype(F32)
        qn = _stack_q(f(q_ref), qw_ref[...])[0].astype(BF16)
        kn, _, _, v2 = _stack_kv(f(kp_ref), f(kc_ref), f(vp_ref), f(vc_ref), kw_ref[...])
        knb = kn.astype(BF16)
        s = jnp.concatenate([_dot_nt(qn[GQ * j:GQ * (j + 1)], knb[2 * BLK * j:2 * BLK * (j + 1)])
                             for j in range(HKV)], axis=0)
        pr = _softmax_rows(s * SCALE + bm_ref[0].reshape(HQ * BLK, 2 * BLK)).astype(BF16)
        o = jnp.concatenate([_dot(pr[GQ * j:GQ * (j + 1)], v2[2 * BLK * j:2 * BLK * (j + 1)])
                             for j in range(HKV)], axis=0)
        o_ref[...] = _unstack(o, HQ).astype(BF16)

    kblk, vblk = C_K // 256, C_V // 256
    prev = lambda n: jnp.maximum(n - 1, 0)
    return pl.pallas_call(
        body, name="attn_fwd", grid=(nb,),
        in_specs=[pl.BlockSpec((BLK, D), lambda n: (n, 0)),
                  pl.BlockSpec((BLK, 256), lambda n: (n, kblk)),
                  pl.BlockSpec((BLK, 256), lambda n: (prev(n), kblk)),
                  pl.BlockSpec((BLK, 256), lambda n: (n, vblk)),
                  pl.BlockSpec((BLK, 256), lambda n: (prev(n), vblk)),
                  pl.BlockSpec((1, HQ, BLK, 2 * BLK), lambda n: (jnp.minimum(n, 1), 0, 0, 0)),
                  _full((1, DH)), _full((1, DH))],
        out_specs=pl.BlockSpec((BLK, D), lambda n: (n, 0)),
        out_shape=jax.ShapeDtypeStruct((t, D), BF16),
        compiler_params=_cp(("parallel",)),
    )(proj, proj, proj, proj, proj, biasm, q_norm_w, k_norm_w)


def attn_bwd(proj, dao, biasm, q_norm_w, k_norm_w):
    t = proj.shape[0]
    nb = t // BLK
    kb = 2 * BLK

    def body(q_ref, kc_ref, kp_ref, vc_ref, vp_ref, do_ref, bm_ref, qw_ref, kw_ref,
             dq_ref, dkv_ref, dqw_ref, dkw_ref, dacc_ref, ck, cv, pk, pv, nk, nv):
        n = pl.program_id(0)

        @pl.when(n == 0)
        def _():
            for ref in (dqw_ref, dkw_ref, dacc_ref, ck, cv):
                ref[...] = jnp.zeros_like(ref)

        qw = qw_ref[...]
        kw = kw_ref[...]
        f = lambda ref: ref[...].astype(F32)
        kn, khat, rk, v2 = _stack_kv(f(kp_ref), f(kc_ref), f(vp_ref), f(vc_ref), kw)
        grp = lambda a, j: a[GQ * j:GQ * (j + 1)]
        band = lambda a, j: a[kb * j:kb * (j + 1)]

        @pl.when(n < nb)
        def _():
            qn, qhat, rq = _stack_q(f(q_ref), qw)
            qnb = qn.astype(BF16)
            knb = kn.astype(BF16)
            dos = _stack_heads(f(do_ref), HQ).astype(BF16)
            s = jnp.concatenate([_dot_nt(grp(qnb, j), band(knb, j)) for j in range(HKV)], axis=0)
            pr = _softmax_rows(s * SCALE + bm_ref[0].reshape(HQ * BLK, kb))
            dp = jnp.concatenate([_dot_nt(grp(dos, j), band(v2, j)) for j in range(HKV)], axis=0)
            ds = pr * (dp - _rsum(pr * dp))
            dacc_ref[...] += ds.reshape(HQ, BLK, kb)
            dsb = ds.astype(BF16)
            prb = pr.astype(BF16)
            dqn = jnp.concatenate([_dot(grp(dsb, j), band(knb, j)) for j in range(HKV)], axis=0) * SCALE
            dqhat = dqn * qw
            dq = rq * (dqhat - qhat * jnp.mean(dqhat * qhat, axis=-1, keepdims=True))
            dq_ref[...] = _unstack(dq, HQ).astype(BF16)
            dqw_ref[...] += _csum(dqn * qhat)
            first = _band_first((kb, DH))
            for j in range(HKV):
                rows = slice(BLK * j, BLK * (j + 1))
                dkn = jnp.where(first, 0.0, _dot_tn(grp(dsb, j), grp(qnb, j)) * SCALE)
                dvj = jnp.where(first, 0.0, _dot_tn(grp(prb, j), grp(dos, j)))
                pk[rows, :] = dkn[:BLK]
                nk[rows, :] = dkn[BLK:]
                pv[rows, :] = dvj[:BLK]
                nv[rows, :] = dvj[BLK:]

        @pl.when(n == nb)
        def _():
            for ref in (pk, pv, nk, nv):
                ref[...] = jnp.zeros_like(ref)

        khp = jnp.concatenate([khat[kb * j:kb * j + BLK] for j in range(HKV)], axis=0)
        rkp = jnp.concatenate([rk[kb * j:kb * j + BLK] for j in range(HKV)], axis=0)
        dkn = ck[...] + pk[...]
        dkhat = dkn * kw
        dk = rkp * (dkhat - khp * jnp.mean(dkhat * khp, axis=-1, keepdims=True))
        dkw_ref[...] += _csum(dkn * khp)
        dkv_ref[...] = jnp.concatenate([_unstack(dk, HKV), _unstack(cv[...] + pv[...], HKV)], axis=1).astype(BF16)
        ck[...] = nk[...]
        cv[...] = nv[...]

    kblk, vblk = C_K // 256, C_V // 256
    cur = lambda n: jnp.minimum(n, nb - 1)
    prev = lambda n: jnp.maximum(n - 1, 0)
    carry = pltpu.VMEM((HKV * BLK, DH), F32)
    return pl.pallas_call(
        body, name="attn_bwd", grid=(nb + 1,),
        in_specs=[pl.BlockSpec((BLK, D), lambda n: (cur(n), 0)),
                  pl.BlockSpec((BLK, 256), lambda n: (cur(n), kblk)), pl.BlockSpec((BLK, 256), lambda n: (prev(n), kblk)),
                  pl.BlockSpec((BLK, 256), lambda n: (cur(n), vblk)), pl.BlockSpec((BLK, 256), lambda n: (prev(n), vblk)),
                  pl.BlockSpec((BLK, D), lambda n: (cur(n), 0)),
                  pl.BlockSpec((1, HQ, BLK, kb), lambda n: (jnp.minimum(n, 1), 0, 0, 0)),
                  _full((1, DH)), _full((1, DH))],
        out_specs=[pl.BlockSpec((BLK, D), lambda n: (cur(n), 0)),
                   pl.BlockSpec((BLK, 512), lambda n: (prev(n), 0)),
                   _full((1, DH)), _full((1, DH)), _full((HQ, BLK, kb))],
        out_shape=[jax.ShapeDtypeStruct((t, D), BF16), jax.ShapeDtypeStruct((t, 512), BF16),
                   jax.ShapeDtypeStruct((1, DH), F32),
                   jax.ShapeDtypeStruct((1, DH), F32), jax.ShapeDtypeStruct((HQ, BLK, kb), F32)],
        scratch_shapes=[carry] * 6,
        compiler_params=_cp(("arbitrary",)),
    )(proj, proj, proj, proj, proj, dao, biasm, q_norm_w, k_norm_w)


CONV_TM, CONV_CW, CONV_RC, HALO = 512, 1024, 32, 16


def conv_fwd(proj, conv_w, conv_b):
    t = proj.shape[0]
    tm = min(t, CONV_TM)
    c0 = C_XBC // CONV_CW

    def body(x_ref, xp_ref, w_ref, b_ref, o_ref, ds_ref):
        i = pl.program_id(1)
        w = w_ref[...]
        b = b_ref[...]
        for r in range(tm // CONV_RC):
            lo = r * CONV_RC
            if r == 0:
                head = jnp.where(i == 0, 0.0, xp_ref[...].astype(F32))
                win = jnp.concatenate([head, x_ref[0:CONV_RC, :].astype(F32)], axis=0)
            else:
                win = x_ref[lo - HALO:lo + CONV_RC, :].astype(F32)
            acc = b
            for j in range(CONV_K):
                acc = acc + w[j:j + 1] * win[HALO - 3 + j:HALO - 3 + j + CONV_RC]
            sg = _sig(acc)
            o_ref[lo:lo + CONV_RC, :] = acc * sg
            ds_ref[lo:lo + CONV_RC, :] = _dsilu(acc, sg).astype(BF16)

    rh = tm // HALO
    tile = pl.BlockSpec((tm, CONV_CW), lambda s, i: (i, s))
    return pl.pallas_call(
        body, name="conv_fwd", grid=(XBC // CONV_CW, t // tm),
        in_specs=[pl.BlockSpec((tm, CONV_CW), lambda s, i: (i, c0 + s)),
                  pl.BlockSpec((HALO, CONV_CW), lambda s, i: (jnp.maximum(i * rh - 1, 0), c0 + s)),
                  pl.BlockSpec((CONV_K, CONV_CW), lambda s, i: (0, s)), pl.BlockSpec((1, CONV_CW), lambda s, i: (0, s))],
        out_specs=[tile, tile],
        out_shape=[jax.ShapeDtypeStruct((t, XBC), F32), jax.ShapeDtypeStruct((t, XBC), BF16)],
        compiler_params=_cp(("parallel", "parallel")),
    )(proj, proj, conv_w, conv_b)


def conv_bwd(proj, dact, dsl, conv_w):
    t = proj.shape[0]
    tm = min(t, CONV_TM)
    nt = t // tm
    nr = tm // CONV_RC
    c0 = C_XBC // CONV_CW
    ext = CONV_RC + 8

    def body(x_ref, xp_ref, d_ref, dn_ref, s_ref, sn_ref, w_ref, dx_ref, dw_ref, db_ref):
        i = pl.program_id(1)

        @pl.when(i == 0)
        def _():
            dw_ref[...] = jnp.zeros_like(dw_ref)
            db_ref[...] = jnp.zeros_like(db_ref)

        w = w_ref[...]
        dws = [jnp.zeros((1, CONV_CW), F32) for _ in range(CONV_K)]
        db = jnp.zeros((1, CONV_CW), F32)
        for r in range(nr):
            lo = r * CONV_RC
            if r == 0:
                head = jnp.where(i == 0, 0.0, xp_ref[...].astype(F32))
                win = jnp.concatenate([head, x_ref[0:CONV_RC, :].astype(F32)], axis=0)
            else:
                win = x_ref[lo - HALO:lo + CONV_RC, :].astype(F32)
            if r < nr - 1:
                dext = d_ref[lo:lo + ext, :]
                sext = s_ref[lo:lo + CONV_RC + HALO, :].astype(F32)[0:ext]
            else:
                dext = jnp.concatenate([d_ref[lo:lo + CONV_RC, :], jnp.where(i == nt - 1, 0.0, dn_ref[...])], axis=0)
                sext = jnp.concatenate([s_ref[lo:lo + CONV_RC, :].astype(F32), sn_ref[...].astype(F32)], axis=0)[0:ext]
            dpre = dext * sext
            dx = jnp.zeros((CONV_RC, CONV_CW), F32)
            own = dpre[0:CONV_RC]
            for j in range(CONV_K):
                dx = dx + w[j:j + 1] * dpre[3 - j:3 - j + CONV_RC]
                dws[j] = dws[j] + _csum(own * win[HALO - 3 + j:HALO - 3 + j + CONV_RC])
            db = db + _csum(own)
            dx_ref[lo:lo + CONV_RC, :] = dx.astype(BF16)
        dw_ref[...] += jnp.concatenate(dws, axis=0)
        db_ref[...] += db

    rh = tm // HALO
    r8 = tm // 8
    nxt = lambda i, per: jnp.minimum((i + 1) * per, nt * per - 1)
    return pl.pallas_call(
        body, name="conv_bwd", grid=(XBC // CONV_CW, nt),
        in_specs=[pl.BlockSpec((tm, CONV_CW), lambda s, i: (i, c0 + s)),
                  pl.BlockSpec((HALO, CONV_CW), lambda s, i: (jnp.maximum(i * rh - 1, 0), c0 + s)),
                  pl.BlockSpec((tm, CONV_CW), lambda s, i: (i, s)),
                  pl.BlockSpec((8, CONV_CW), lambda s, i: (nxt(i, r8), s)),
                  pl.BlockSpec((tm, CONV_CW), lambda s, i: (i, s)),
                  pl.BlockSpec((HALO, CONV_CW), lambda s, i: (nxt(i, rh), s)),
                  pl.BlockSpec((CONV_K, CONV_CW), lambda s, i: (0, s))],
        out_specs=[pl.BlockSpec((tm, CONV_CW), lambda s, i: (i, s)),
                   pl.BlockSpec((CONV_K, CONV_CW), lambda s, i: (0, s)), pl.BlockSpec((1, CONV_CW), lambda s, i: (0, s))],
        out_shape=[jax.ShapeDtypeStruct((t, XBC), BF16), jax.ShapeDtypeStruct((CONV_K, XBC), F32),
                   jax.ShapeDtypeStruct((1, XBC), F32)],
        compiler_params=_cp(("parallel", "arbitrary")),
    )(proj, proj, dact, dact, dsl, dsl, conv_w)


def _split3(x):
    h = x.astype(BF16)
    r = x - h.astype(F32)
    m = r.astype(BF16)
    lo = (r - m.astype(F32)).astype(BF16)
    return h, m, lo


def _tri_mm(tri, x):
    h, m, lo = _split3(x)
    return _dot(tri, h) + _dot(tri, m) + _dot(tri, lo)


def _softplus(x):
    return jnp.maximum(x, 0.0) + jnp.log1p(jnp.exp(-jnp.abs(x)))


def _chunk_decays(dt_raw, dtb, alog):
    dtv = _softplus(dt_raw + dtb)
    a = -jnp.exp(alog)
    ri = lax.broadcasted_iota(jnp.int32, (BLK, BLK), 0)
    ci = lax.broadcasted_iota(jnp.int32, (BLK, BLK), 1)
    causal = ri >= ci
    acum = _tri_mm(causal.astype(BF16), dtv * a)
    return dtv, a, causal, acum, acum.T


NPAIR = SH // 2


def _pairs(x):
    return jnp.stack([x[:, 128 * k:128 * (k + 1)] for k in range(NPAIR)])


def _unpairs(x3):
    return jnp.concatenate([x3[k] for k in range(NPAIR)], axis=1)


def _per_head_cols(m):
    return jnp.stack([jnp.broadcast_to(m[:, h:h + 1], m.shape) for h in range(SH)])


def _pair_lanes(t):
    r = t.reshape(NPAIR, 2, t.shape[1], 128)
    lo = lax.broadcasted_iota(jnp.int32, (1, t.shape[1], 128), 2) < SP
    return jnp.where(lo, r[:, 0], r[:, 1])


class _Chunk:
    pass


def _chunk_common(dt_raw, dtb, alog, dskip):
    cm = _Chunk()
    cm.dtv, cm.a, cm.causal, acum, acum_t = _chunk_decays(dt_raw, dtb, alog)
    cm.acol = _per_head_cols(acum)
    cm.arow = jnp.stack([acum_t[h:h + 1, :] for h in range(SH)])
    apl = _pair_lanes(cm.acol)
    alast = apl[:, BLK - 1:BLK, :]
    cm.dpl = _pair_lanes(_per_head_cols(cm.dtv))
    cm.eapl = jnp.exp(apl)
    cm.epl = jnp.exp(alast - apl)
    cm.cdpl = jnp.exp(alast)
    cm.dskpl = _pair_lanes(_per_head_cols(dskip))
    cm.lo = lax.broadcasted_iota(jnp.int32, (1, BLK, 128), 2) < SP
    return cm


def ssd_fwd(act, dt_raw, dtb_p, alog_p, dsk_p):
    t = act.shape[0]
    nc = t // BLK

    def body(xs_ref, b_ref, c_ref, dt_ref, dtb_ref, al_ref, dk_ref, y_ref, sp_ref, st):
        c = pl.program_id(0)

        @pl.when(c == 0)
        def _():
            st[...] = jnp.zeros_like(st)

        s_t = st[...]
        sp_ref[0] = s_t
        cm = _chunk_common(dt_ref[...], dtb_ref[...], al_ref[...], dk_ref[...])
        gms, cbs, bts = [], [], []
        for g in range(SG):
            bf = b_ref[:, SN * g:SN * (g + 1)]
            cb = c_ref[:, SN * g:SN * (g + 1)].astype(BF16)
            gms.append(_dot_nt(cb, bf.astype(BF16)))
            cbs.append(cb)
            bts.append(bf.T.astype(BF16))
        lam = jnp.exp(jnp.where(cm.causal[None], cm.acol - cm.arow, NEG))
        m = (lam.reshape(SG, SR, BLK, BLK) * jnp.stack(gms)[:, None]).reshape(SH, BLK, BLK).astype(BF16)
        xs16 = _pairs(xs_ref[...])
        xdt16 = xs16 * cm.dpl
        x_lo = jnp.where(cm.lo, xdt16, 0.0).astype(BF16)
        x_hi = jnp.where(cm.lo, 0.0, xdt16).astype(BF16)
        s16 = _pairs(s_t)
        s16b = s16.astype(BF16)
        yd = jnp.stack([_dot(m[2 * k], x_lo[k]) + _dot(m[2 * k + 1], x_hi[k]) for k in range(NPAIR)])
        yo = jnp.stack([_dot(cbs[k // (NPAIR // SG)], s16b[k]) for k in range(NPAIR)])
        y_ref[...] = _unpairs(yd + yo * cm.eapl + cm.dskpl * xs16).astype(BF16)
        xe = (xdt16 * cm.epl).astype(BF16)
        st[...] = _unpairs(cm.cdpl * s16 + jnp.stack([_dot(bts[k // (NPAIR // SG)], xe[k]) for k in range(NPAIR)]))

    vec = _full((1, 128))
    return pl.pallas_call(
        body, name="ssd_fwd", grid=(nc,),
        in_specs=[pl.BlockSpec((BLK, SSM_W), lambda c: (c, 0)),
                  pl.BlockSpec((BLK, SG * SN), lambda c: (c, SSM_W // (SG * SN))),
                  pl.BlockSpec((BLK, SG * SN), lambda c: (c, SSM_W // (SG * SN) + 1)),
                  pl.BlockSpec((BLK, 128), lambda c: (c, 0)), vec, vec, vec],
        out_specs=[pl.BlockSpec((BLK, SSM_W), lambda c: (c, 0)), pl.BlockSpec((1, SN, SSM_W), lambda c: (c, 0, 0))],
        out_shape=[jax.ShapeDtypeStruct((t, SSM_W), BF16), jax.ShapeDtypeStruct((nc, SN, SSM_W), F32)],
        scratch_shapes=[pltpu.VMEM((SN, SSM_W), F32)],
        compiler_params=_cp(("arbitrary",)),
    )(act, act, act, dt_raw, dtb_p, alog_p, dsk_p)


def _head_sums(q):
    r = q.shape[1]
    lo = lax.broadcasted_iota(jnp.int32, (1, r, 128), 2) < SP
    s_lo = jnp.sum(jnp.where(lo, q, 0.0), axis=-1, keepdims=True)
    s_hi = jnp.sum(jnp.where(lo, 0.0, q), axis=-1, keepdims=True)
    lane = lax.broadcasted_iota(jnp.int32, (r, 128), 1)
    out = jnp.zeros((r, 128), F32)
    for k in range(NPAIR):
        out = jnp.where(lane == 2 * k, s_lo[k], jnp.where(lane == 2 * k + 1, s_hi[k], out))
    return out


def ssd_bwd(act, dt_raw, dy, sprev, dtb_p, alog_p, dsk_p):
    t = act.shape[0]
    nc = t // BLK

    def body(xs_ref, b_ref, c_ref, dt_ref, dy_ref, sp_ref, dtb_ref, al_ref, dk_ref,
             da_ref, ddt_ref, ddtb_ref, dal_ref, ddk_ref, dst):
        i = pl.program_id(0)

        @pl.when(i == 0)
        def _():
            dst[...] = jnp.zeros_like(dst)
            ddtb_ref[...] = jnp.zeros_like(ddtb_ref)
            dal_ref[...] = jnp.zeros_like(dal_ref)
            ddk_ref[...] = jnp.zeros_like(ddk_ref)

        dt_raw = dt_ref[...]
        dtb = dtb_ref[...]
        cm = _chunk_common(dt_raw, dtb, al_ref[...], dk_ref[...])
        ri = lax.broadcasted_iota(jnp.int32, (BLK, BLK), 0)
        ci = lax.broadcasted_iota(jnp.int32, (BLK, BLK), 1)
        lam_t = jnp.exp(jnp.where((ri <= ci)[None], cm.arow - cm.acol, NEG))
        bbs, cbs, cts, gms = [], [], [], []
        for g in range(SG):
            bf = b_ref[:, SN * g:SN * (g + 1)]
            cf = c_ref[:, SN * g:SN * (g + 1)]
            bbs.append(bf.astype(BF16))
            cbs.append(cf.astype(BF16))
            cts.append(cf.T.astype(BF16))
            gms.append(_dot_nt(bbs[g], cbs[g]))
        grp = lambda k: k // (NPAIR // SG)
        xs16 = _pairs(xs_ref[...])
        dy16 = _pairs(dy_ref[...].astype(F32))
        sp16 = _pairs(sp_ref[0])
        ds16 = _pairs(dst[...])
        xdt16 = xs16 * cm.dpl
        xdtb = xdt16.astype(BF16)
        dyh = [jnp.where(cm.lo, dy16, 0.0).astype(BF16), jnp.where(cm.lo, 0.0, dy16).astype(BF16)]
        m_t = (lam_t.reshape(SG, SR, BLK, BLK) * jnp.stack(gms)[:, None]).reshape(SH, BLK, BLK).astype(BF16)
        dxdt = jnp.stack([_dot(m_t[2 * k], dyh[0][k]) + _dot(m_t[2 * k + 1], dyh[1][k]) for k in range(NPAIR)])
        dm_t = jnp.stack([_dot_nt(xdtb[h // 2], dyh[h % 2][h // 2]) for h in range(SH)])
        dg_t = jnp.sum((dm_t * lam_t).reshape(SG, SR, BLK, BLK), axis=1).astype(BF16)
        xq16 = xdtb.astype(F32)
        xh = [jnp.where(cm.lo, xdt16, 0.0).astype(BF16), jnp.where(cm.lo, 0.0, xdt16).astype(BF16)]
        y_in = jnp.stack([_dot_tn(m_t[2 * k], xh[0][k]) + _dot_tn(m_t[2 * k + 1], xh[1][k]) for k in range(NPAIR)])
        da_diag = dy16 * y_in - xq16 * dxdt
        lane_c = lax.broadcasted_iota(jnp.int32, (BLK, 128), 1)
        ds16b = ds16.astype(BF16)
        sp16b = sp16.astype(BF16)
        dxs = jnp.stack([_dot(bbs[grp(k)], ds16b[k]) for k in range(NPAIR)]) * cm.epl
        dxdt = dxdt + dxs
        dya = (dy16 * cm.eapl).astype(BF16)
        xe = (xdt16 * cm.epl).astype(BF16)
        dcs, dbs = [], []
        for g in range(SG):
            ks = range(g * (NPAIR // SG), (g + 1) * (NPAIR // SG))
            dcs.append(sum(_dot_nt(dya[k], sp16b[k]) for k in ks) + _dot_tn(dg_t[g], bbs[g]))
            dbs.append(sum(_dot_nt(xe[k], ds16b[k]) for k in ks) + _dot(dg_t[g], cbs[g]))
        dst[...] = _unpairs(cm.cdpl * ds16 + jnp.stack([_dot(cts[grp(k)], dya[k]) for k in range(NPAIR)]))
        da_ref[...] = jnp.concatenate([_unpairs(dxdt * cm.dpl + cm.dskpl * dy16)] + dbs + dcs, axis=1)
        y_off = jnp.stack([_dot(cbs[grp(k)], sp16b[k]) for k in range(NPAIR)]) * cm.eapl
        da_cols = _head_sums(da_diag + dy16 * y_off - xdt16 * dxs)
        last = _head_sums(jnp.sum(xdt16 * dxs, axis=1, keepdims=True)
                          + cm.cdpl * jnp.sum(ds16 * sp16, axis=1, keepdims=True))
        ddt = _head_sums(dxdt * xs16)
        row_i = lax.broadcasted_iota(jnp.int32, (BLK, 128), 0)
        dacum = da_cols + jnp.where(row_i == BLK - 1, last, 0.0)
        dda = _tri_mm((ri <= ci).astype(BF16), dacum)
        ddt = ddt + dda * cm.a
        dal_ref[...] += _csum(dda * cm.dtv) * cm.a
        ddt_raw = jnp.where(lane_c < SH, ddt * _sig(dt_raw + dtb), 0.0)
        ddt_ref[...] = ddt_raw.astype(BF16)
        ddtb_ref[...] += _csum(ddt_raw)
        ddk_ref[...] += _head_sums(jnp.sum(dy16 * xs16, axis=1, keepdims=True))

    rev = lambda i: nc - 1 - i
    vec = _full((1, 128))
    slab = pl.BlockSpec((BLK, SSM_W), lambda i: (rev(i), 0))
    return pl.pallas_call(
        body, name="ssd_bwd", grid=(nc,),
        in_specs=[slab,
                  pl.BlockSpec((BLK, SG * SN), lambda i: (rev(i), SSM_W // (SG * SN))),
                  pl.BlockSpec((BLK, SG * SN), lambda i: (rev(i), SSM_W // (SG * SN) + 1)),
                  pl.BlockSpec((BLK, 128), lambda i: (rev(i), 0)),
                  slab,
                  pl.BlockSpec((1, SN, SSM_W), lambda i: (rev(i), 0, 0)), vec, vec, vec],
        out_specs=[pl.BlockSpec((BLK, XBC), lambda i: (rev(i), 0)), pl.BlockSpec((BLK, 128), lambda i: (rev(i), 0)),
                   vec, vec, vec],
        out_shape=[jax.ShapeDtypeStruct((t, XBC), F32), jax.ShapeDtypeStruct((t, 128), BF16),
                   jax.ShapeDtypeStruct((1, 128), F32), jax.ShapeDtypeStruct((1, 128), F32),
                   jax.ShapeDtypeStruct((1, 128), F32)],
        scratch_shapes=[pltpu.VMEM((SN, SSM_W), F32)],
        compiler_params=_cp(("arbitrary",)),
    )(act, act, act, dt_raw, dy, sprev, dtb_p, alog_p, dsk_p)


TAIL_TM = 256


def _dsilu(z, s):
    return s * (1.0 + z * (1.0 - s))


def tail(proj, ao, yss, x, target, gate, ssm_nw, w_at, w_ss, w_ou):
    t = x.shape[0]
    tm = min(t, TAIL_TM)
    gw = SSM_W // SG

    def body(ao_ref, za_ref, ga_ref, gb_ref, zm_ref, ys_ref, x_ref, tg_ref, gt_ref, nw_ref, wa_ref, ws_ref, wo_ref,
             loss_ref, dy_ref, dao_ref, dmid_ref, dys_ref,
             ua_ref, yn_ref, mg_ref, dya_ref, dyb_ref, do_ref, dgt_ref, dnw_ref):
        i = pl.program_id(0)

        @pl.when(i == 0)
        def _():
            loss_ref[...] = jnp.zeros_like(loss_ref)
            dgt_ref[...] = jnp.zeros_like(dgt_ref)
            dnw_ref[...] = jnp.zeros_like(dnw_ref)

        ao = ao_ref[...].astype(F32)
        za = za_ref[...].astype(F32)
        sa = _sig(za)
        sila = za * sa
        ua_f = ao * sila
        ua = ua_f.astype(BF16)
        ya = _dot(ua, wa_ref[...])
        zm = zm_ref[...].astype(F32)
        sm = _sig(zm)
        silm = zm * sm
        ys = ys_ref[...].astype(F32)
        u = ys * silm
        nw = nw_ref[...]
        rs, uns = [], []
        for g in range(SG):
            ug = u[:, gw * g:gw * (g + 1)]
            r = lax.rsqrt(jnp.mean(ug * ug, axis=-1, keepdims=True) + EPS)
            rs.append(r)
            uns.append(ug * r)
        un = jnp.concatenate(uns, axis=1)
        yn_f = un * nw
        yn = yn_f.astype(BF16)
        yb = _dot(yn, ws_ref[...])
        sga = _sig(ga_ref[...].astype(F32))
        sgb = _sig(gb_ref[...].astype(F32))
        mg_f = sga * ya + sgb * yb
        mg = mg_f.astype(BF16)
        o = _dot(mg, wo_ref[...])
        gt = gt_ref[...]
        err = (x_ref[...] + gt * o) - tg_ref[...]
        lane = lax.broadcasted_iota(jnp.int32, (1, 128), 1)
        loss_ref[...] += jnp.where(lane == 0, 0.5 * _asum(_rsum(err * err) / D), 0.0)
        dy = err * (1.0 / D)
        dy_ref[...] = dy
        dgt_ref[...] += _csum(dy * o)
        do = (dy * gt).astype(BF16)
        dmg = _dot_nt(do, wo_ref[...])
        dmid_ref[:, C_GA - C_ZA:C_GB - C_ZA] = (dmg * ya * sga * (1.0 - sga)).astype(BF16)
        dmid_ref[:, C_GB - C_ZA:C_ZM - C_ZA] = (dmg * yb * sgb * (1.0 - sgb)).astype(BF16)
        dya = (dmg * sga).astype(BF16)
        dyb = (dmg * sgb).astype(BF16)
        dua = _dot_nt(dya, wa_ref[...])
        dao_ref[...] = (dua * sila).astype(BF16)
        dmid_ref[:, 0:C_GA - C_ZA] = (dua * ao * _dsilu(za, sa)).astype(BF16)
        dyn = _dot_nt(dyb, ws_ref[...])
        dnw_ref[...] += _csum(dyn * un)
        dun = dyn * nw
        dus = []
        for g in range(SG):
            gs = slice(gw * g, gw * (g + 1))
            dus.append(rs[g] * (dun[:, gs] - uns[g] * jnp.mean(dun[:, gs] * uns[g], axis=-1, keepdims=True)))
        du = jnp.concatenate(dus, axis=1)
        dys_ref[...] = (du * silm).astype(BF16)
        dmid_ref[:, C_ZM - C_ZA:] = (du * ys * _dsilu(zm, sm)).astype(BF16)
        ua_ref[...] = ua_f.T.astype(BF16)
        yn_ref[...] = yn_f.T.astype(BF16)
        mg_ref[...] = mg_f.T.astype(BF16)
        dya_ref[...] = dya
        dyb_ref[...] = dyb
        do_ref[...] = do

    row = lambda w: pl.BlockSpec((tm, w), lambda i: (i, 0))
    pcol = lambda w, c0: pl.BlockSpec((tm, w), lambda i: (i, c0 // w))
    sd = lambda w, dt: jax.ShapeDtypeStruct((t, w), dt)
    colt = lambda w: pl.BlockSpec((w, tm), lambda i: (0, i))
    sdt = lambda w: jax.ShapeDtypeStruct((w, t), BF16)
    return pl.pallas_call(
        body, name="tail", grid=(t // tm,),
        in_specs=[row(D), pcol(D, C_ZA), pcol(D, C_GA), pcol(D, C_GB), pcol(SSM_W, C_ZM), row(SSM_W), row(D), row(D),
                  _full((1, D)), _full((1, SSM_W)), _full((D, D)), _full((SSM_W, D)), _full((D, D))],
        out_specs=[_full((1, 128)), row(D), row(D), row(W_MID), row(SSM_W),
                   colt(D), colt(SSM_W), colt(D), row(D), row(D), row(D), _full((1, D)), _full((1, SSM_W))],
        out_shape=[jax.ShapeDtypeStruct((1, 128), F32), sd(D, F32), sd(D, BF16), sd(W_MID, BF16),
                   sd(SSM_W, BF16), sdt(D), sdt(SSM_W), sdt(D), sd(D, BF16),
                   sd(D, BF16), sd(D, BF16), jax.ShapeDtypeStruct((1, D), F32), jax.ShapeDtypeStruct((1, SSM_W), F32)],
        compiler_params=_cp(("arbitrary",)),
    )(ao, proj, proj, proj, proj, yss, x, target, gate, ssm_nw, w_at, w_ss, w_ou)


DPIECES = ((D, ((D, C_Q),)),
           (W_MID, ((D, C_ZA), (D, C_GA), (D, C_GB), (SSM_W, C_ZM))),
           (XBC, ((XBC, C_XBC),)),
           (512, ((512, C_K),)),
           (128, ((128, C_DT),)))


def dproj_bwd(pieces, wcat, x, dy, norm_w, scale):
    t = x.shape[0]
    tm = min(t, 256)
    nt = t // tm
    wblocks = [blk for _, subs in DPIECES for blk in subs]
    npc, nwb = len(DPIECES), len(wblocks)

    def body(*refs):
        p_refs, w_refs = refs[:npc], refs[npc:npc + nwb]
        x_ref, dy_ref, nw_ref, sc_ref, gx_ref, dnw_ref, dsc_ref, dsh_ref, dwe_ref = refs[npc + nwb:]
        i = pl.program_id(0)

        @pl.when(i == 0)
        def _():
            for ref in (dwe_ref, dsh_ref, dnw_ref, dsc_ref):
                ref[...] = jnp.zeros_like(ref)

        dh, wi = None, 0
        for p_ref, (_, subs) in zip(p_refs, DPIECES):
            loc = 0
            for w, _ in subs:
                part = _dot_nt(p_ref[:, loc:loc + w], w_refs[wi][...])
                dh = part if dh is None else dh + part
                loc += w
                wi += 1
        xv = x_ref[...]
        r = lax.rsqrt(jnp.mean(xv * xv, axis=-1, keepdims=True) + EPS)
        xn = xv * r
        weff = nw_ref[...] * (1.0 + sc_ref[...])
        dxn = dh * weff
        gx_ref[...] = dy_ref[...] + r * (dxn - xn * jnp.mean(dxn * xn, axis=-1, keepdims=True))
        dwe_ref[...] += _csum(dh * xn)
        dsh_ref[...] += _csum(dh)

        @pl.when(i == nt - 1)
        def _():
            dwe = dwe_ref[...]
            dnw_ref[...] = dwe * (1.0 + sc_ref[...])
            dsc_ref[...] = dwe * nw_ref[...]

    vec = pl.BlockSpec((1, D), lambda i: (0, 0))
    row = pl.BlockSpec((tm, D), lambda i: (i, 0))
    return pl.pallas_call(
        body, name="dproj_bwd", grid=(nt,),
        in_specs=[pl.BlockSpec((tm, pw), lambda i: (i, 0)) for pw, _ in DPIECES]
        + [pl.BlockSpec((D, w), functools.partial(lambda i, b: (0, b), b=off // w), pipeline_mode=pl.Buffered(1))
           for w, off in wblocks]
        + [row, row, vec, vec],
        out_specs=[row, vec, vec, vec],
        out_shape=[jax.ShapeDtypeStruct((t, D), F32), jax.ShapeDtypeStruct((1, D), F32),
                   jax.ShapeDtypeStruct((1, D), F32), jax.ShapeDtypeStruct((1, D), F32)],
        scratch_shapes=[pltpu.VMEM((1, D), F32)],
        compiler_params=_cp(("arbitrary",)),
    )(*pieces, *([wcat] * nwb), x, dy, norm_w, scale)


def wgrad(at, b, name, bn, after):
    m, t = at.shape
    n = b.shape[1]
    tk = min(t, 1024)
    bm = min(m, 1024)

    def body(a_ref, b_ref, after_ref, o_ref):
        part = _dot(a_ref[...], b_ref[...])

        @pl.when(pl.program_id(2) == 0)
        def _():
            o_ref[...] = part

        @pl.when(pl.program_id(2) > 0)
        def _():
            o_ref[...] += part

    return pl.pallas_call(
        body, name=name, grid=(m // bm, n // bn, t // tk),
        in_specs=[pl.BlockSpec((bm, tk), lambda i, j, k: (i, k)), pl.BlockSpec((tk, bn), lambda i, j, k: (k, j)), ANY],
        out_specs=pl.BlockSpec((bm, bn), lambda i, j, k: (i, j)),
        out_shape=jax.ShapeDtypeStruct((m, n), F32),
        compiler_params=_cp(("parallel", "parallel", "arbitrary")),
    )(at, b, after)


SUM_TR = 256


def pair_sum(g, core, theirs, name):
    w = g.shape[2]
    nh = HROWS // SUM_TR

    def body(core_ref, a_ref, b_ref, o_ref, ob_ref):
        s = a_ref[...] + b_ref[...]
        o_ref[...] = s
        ob_ref[...] = s.astype(BF16)

    spec = pl.BlockSpec((1, SUM_TR, w), lambda d, i, c: (d, i, 0))
    return pl.pallas_call(
        body, name=name,
        out_shape=[jax.ShapeDtypeStruct((4, HROWS, w), F32), jax.ShapeDtypeStruct((4, HROWS, w), BF16)],
        grid_spec=pltpu.PrefetchScalarGridSpec(
            num_scalar_prefetch=1, grid=(4, nh),
            in_specs=[pl.BlockSpec((1, SUM_TR, w), lambda d, i, c: (d, c[0] * nh + i, 0)), spec],
            out_specs=[spec, spec]),
        compiler_params=_cp(("parallel", "parallel")))(core.reshape(1).astype(jnp.int32), g, theirs)


def chip_sum(part, chip, others, name):
    r, w = part.shape[1:]

    def body(chip_ref, a_ref, b_ref, o_ref):
        acc = a_ref[0]
        for k in range(3):
            acc = acc + b_ref[k].astype(F32)
        o_ref[...] = acc

    return pl.pallas_call(
        body, name=name, out_shape=jax.ShapeDtypeStruct((r, w), F32),
        grid_spec=pltpu.PrefetchScalarGridSpec(
            num_scalar_prefetch=1, grid=(r // SUM_TR,),
            in_specs=[pl.BlockSpec((1, SUM_TR, w), lambda i, c: (c[0], i, 0)),
                      pl.BlockSpec((3, SUM_TR, w), lambda i, c: (0, i, 0))],
            out_specs=pl.BlockSpec((SUM_TR, w), lambda i, c: (i, 0))),
        compiler_params=_cp(("parallel",)))(chip.reshape(1).astype(jnp.int32), part, others)


def sum_devices(g):
    r = g.shape[1]

    def body(g_ref, o_ref):
        acc = g_ref[0]
        for d in range(1, 8):
            acc = acc + g_ref[d]
        o_ref[...] = acc

    return pl.pallas_call(body, name="sum_devices", out_shape=jax.ShapeDtypeStruct((r, 1024), F32),
                          compiler_params=_cp())(g)


def adamw(w, g, m, v, name):
    r, c = w.shape
    tr = r
    for cand in (256, 128, 64, 32, 16, 8):
        if r % cand == 0 and r > cand:
            tr = cand
            break

    def body(w_ref, g_ref, m_ref, v_ref, d_ref, nm_ref, nv_ref):
        gv = g_ref[...]
        mn = ADAM_B1 * m_ref[...] + (1.0 - ADAM_B1) * gv
        vn = ADAM_B2 * v_ref[...] + (1.0 - ADAM_B2) * (gv * gv)
        m_hat = mn / (1.0 - ADAM_B1 ** ADAM_STEP)
        v_hat = vn / (1.0 - ADAM_B2 ** ADAM_STEP)
        d_ref[...] = -ADAM_LR * (m_hat / (jnp.sqrt(v_hat) + ADAM_EPS) + ADAM_WD * w_ref[...])
        nm_ref[...] = mn
        nv_ref[...] = vn

    spec = pl.BlockSpec((tr, c), lambda i: (i, 0))
    sd = jax.ShapeDtypeStruct((r, c), F32)
    return pl.pallas_call(body, name=name, grid=(r // tr,), in_specs=[spec] * 4, out_specs=[spec] * 3,
                          out_shape=[sd, sd, sd], compiler_params=_cp(("parallel",)))(w, g, m, v)


ANY = pl.BlockSpec(memory_space=pl.ANY)
VM = pl.BlockSpec(memory_space=pltpu.VMEM)
OTHER_CHIPS = ((1, 0), (0, 1), (1, 1))


def _pos():
    return lax.axis_index("x"), lax.axis_index("y"), lax.axis_index("c")


def _flip(v, bit):
    return 1 - v if bit else v


def _rcopy(src, dst, ssem, rsem, peer):
    return pltpu.make_async_remote_copy(src_ref=src, dst_ref=dst, send_sem=ssem, recv_sem=rsem,
                                        device_id=peer, device_id_type=MESH)


def allgather_small(p, name):
    r = p.shape[0]

    def body(in_ref, out_ref, ssem, rsem, lsem):
        x, y, c = _pos()
        me = 4 * x + 2 * y + c
        loc = pltpu.make_async_copy(in_ref, out_ref.at[me], lsem)
        loc.start()
        sends = []
        peers = []
        for k in range(1, 8):
            px, py, pc = _flip(x, (k >> 2) & 1), _flip(y, (k >> 1) & 1), _flip(c, k & 1)
            peers.append((px, py, pc))
            cp = _rcopy(in_ref, out_ref.at[me], ssem.at[k - 1], rsem.at[k - 1], (px, py, pc))
            cp.start()
            sends.append(cp)
        for k in range(1, 8):
            px, py, pc = peers[k - 1]
            _rcopy(in_ref, out_ref.at[4 * px + 2 * py + pc], ssem.at[k - 1], rsem.at[k - 1], (px, py, pc)).wait_recv()
        for cp in sends:
            cp.wait_send()
        loc.wait()

    return pl.pallas_call(
        body, name=name, out_shape=jax.ShapeDtypeStruct((8, r, 1024), F32),
        in_specs=[VM], out_specs=VM,
        scratch_shapes=[pltpu.SemaphoreType.DMA((7,)), pltpu.SemaphoreType.DMA((7,)), pltpu.SemaphoreType.DMA],
    )(p)


def gather_weights(w_in_b, mod_sh):
    def body(wi_ref, m_ref, gi_ref, mo_ref, ssem, rsem, lsem):
        x, y, c = _pos()
        chip = 2 * x + y
        mine = pl.ds(pl.multiple_of(c * HROWS, 16), HROWS)
        other = pl.ds(pl.multiple_of((1 - c) * HROWS, 16), HROWS)
        sib = (x, y, 1 - c)
        pairs = ((wi_ref, gi_ref),)
        loc_m = pltpu.make_async_copy(m_ref, mo_ref.at[chip], lsem)
        loc_m.start()
        sends = []
        for k, (fx, fy) in enumerate(OTHER_CHIPS):
            peer = (_flip(x, fx), _flip(y, fy), c)
            for a, (w_ref, g_ref) in enumerate(pairs):
                cw = _rcopy(w_ref.at[mine], g_ref.at[chip, mine], ssem.at[6 * a + k], rsem.at[6 * a + k], peer)
                cw.start()
                sends.append(cw)
            cm = _rcopy(m_ref, mo_ref.at[chip], ssem.at[12 + k], rsem.at[12 + k], peer)
            cm.start()
            sends.append(cm)
        for k, (fx, fy) in enumerate(OTHER_CHIPS):
            px, py = _flip(x, fx), _flip(y, fy)
            for a, (w_ref, g_ref) in enumerate(pairs):
                got = g_ref.at[2 * px + py, mine]
                _rcopy(w_ref.at[mine], got, ssem.at[6 * a + k], rsem.at[6 * a + k], (px, py, c)).wait_recv()
                fw = _rcopy(got, got, ssem.at[6 * a + 3 + k], rsem.at[6 * a + 3 + k], sib)
                fw.start()
                sends.append(fw)
        for k, (fx, fy) in enumerate(OTHER_CHIPS):
            px, py = _flip(x, fx), _flip(y, fy)
            for a, (w_ref, g_ref) in enumerate(pairs):
                land = g_ref.at[2 * px + py, other]
                _rcopy(land, land, ssem.at[6 * a + 3 + k], rsem.at[6 * a + 3 + k], sib).wait_recv()
            _rcopy(m_ref, mo_ref.at[2 * px + py], ssem.at[12 + k], rsem.at[12 + k], (px, py, c)).wait_recv()
        for cp in sends:
            cp.wait_send()
        loc_m.wait()

    return pl.pallas_call(
        body, name="gather_weights",
        out_shape=[jax.ShapeDtypeStruct((4, D, SH_IN), BF16), jax.ShapeDtypeStruct((4, 8, 768), F32)],
        in_specs=[ANY, VM], out_specs=[ANY, VM],
        scratch_shapes=[pltpu.SemaphoreType.DMA((15,)), pltpu.SemaphoreType.DMA((15,)), pltpu.SemaphoreType.DMA],
    )(w_in_b, mod_sh)


def pair_exchange(g):
    def body(g_ref, r_ref, ssem, rsem):
        x, y, c = _pos()
        other = pl.ds(pl.multiple_of((1 - c) * HROWS, 8), HROWS)
        cp = _rcopy(g_ref.at[:, other, :], r_ref, ssem, rsem, (x, y, 1 - c))
        cp.start()
        cp.wait()

    return pl.pallas_call(
        body, name="pair_exchange", out_shape=jax.ShapeDtypeStruct((4, HROWS, g.shape[2]), F32),
        in_specs=[ANY], out_specs=ANY,
        scratch_shapes=[pltpu.SemaphoreType.DMA, pltpu.SemaphoreType.DMA],
    )(g)


HBM = pl.BlockSpec(memory_space=pltpu.HBM)
SEM = pl.BlockSpec(memory_space=pltpu.SEMAPHORE)
DATAFLOW = pltpu.SideEffectType.DATAFLOW_SIDE_EFFECTING


def split_start(name, make_copies, srcs, lands, nsem, after):
    arrays = [*srcs, *lands]
    n, ns = len(arrays), len(srcs)

    def body(*refs):
        for cp in make_copies(refs[:ns], refs[ns:n], refs[n + 1], refs[n + 2])[0]:
            cp.start()
        refs[-1][...] = jnp.zeros_like(refs[-1])

    res = pl.pallas_call(
        body, name=name,
        out_shape=(pltpu.SemaphoreType.DMA((nsem,)), pltpu.SemaphoreType.DMA((nsem,)),
                   *[pltpu.HBM(a.shape, a.dtype) for a in arrays], jax.ShapeDtypeStruct((8, 128), F32)),
        in_specs=(HBM,) * n + (ANY,), out_specs=(SEM, SEM) + (HBM,) * n + (VM,),
        input_output_aliases={i: 2 + i for i in range(n)},
        compiler_params=pltpu.CompilerParams(has_side_effects=DATAFLOW),
    )(*[pltpu.with_memory_space_constraint(a, pltpu.HBM) for a in arrays], after)
    return res[0], res[1], list(res[2:2 + n]), res[-1]


def split_wait(name, make_copies, ssem, rsem, arrays, ns, after):
    n = len(arrays)

    def body(*refs):
        sends, recvs = make_copies(refs[:ns], refs[ns:n], refs[n], refs[n + 1])
        for cp in sends:
            cp.wait_send()
        for cp in recvs:
            cp.wait_recv()

    return pl.pallas_call(
        body, name=name, out_shape=tuple(pltpu.HBM(a.shape, a.dtype) for a in arrays),
        in_specs=(HBM,) * n + (SEM, SEM, ANY), out_specs=(HBM,) * n,
        input_output_aliases={i: i for i in range(n)},
        compiler_params=pltpu.CompilerParams(has_side_effects=DATAFLOW),
    )(*arrays, ssem, rsem, after)


def _chip_copies(srcs, lands, ssem, rsem):
    x, y, c = _pos()
    copies = []
    for k, (fx, fy) in enumerate(OTHER_CHIPS):
        px, py = _flip(x, fx), _flip(y, fy)
        for a, (p_ref, l_ref) in enumerate(zip(srcs, lands)):
            copies.append(_rcopy(p_ref.at[2 * px + py], l_ref.at[k], ssem.at[3 * a + k], rsem.at[3 * a + k], (px, py, c)))
    return copies, copies


def _pair_copies(srcs, lands, ssem, rsem):
    x, y, c = _pos()
    other = pl.ds(pl.multiple_of((1 - c) * HROWS, 8), HROWS)
    copies = [_rcopy(srcs[0].at[:, other, :], lands[0], ssem.at[0], rsem.at[0], (x, y, 1 - c))]
    return copies, copies


def _rest_copies(srcs, lands, ssem, rsem):
    x, y, c = _pos()
    chip = 2 * x + y
    mine = pl.ds(pl.multiple_of(c * HROWS, 16), HROWS)
    sends, recvs = [], []
    for k, (fx, fy) in enumerate(OTHER_CHIPS):
        px, py = _flip(x, fx), _flip(y, fy)
        for t in range(2):
            rows_t = pl.ds(t * HROWS, HROWS)
            sends.append(_rcopy(srcs[0].at[mine], lands[0].at[chip, mine], ssem.at[2 * k + t], rsem.at[2 * k + c],
                                (px, py, t)))
            recvs.append(_rcopy(srcs[0].at[rows_t], lands[0].at[2 * px + py, rows_t], ssem.at[2 * k + t],
                                rsem.at[2 * k + t], (px, py, t)))
    return sends, recvs


def pair_swap(red_in, red_rest):
    def body(ai_ref, ar_ref, oi_ref, or_ref, ssem, rsem):
        x, y, c = _pos()
        cps = [_rcopy(a_ref, o_ref, ssem.at[a], rsem.at[a], (x, y, 1 - c))
               for a, (a_ref, o_ref) in enumerate(((ai_ref, oi_ref), (ar_ref, or_ref)))]
        for cp in cps:
            cp.start()
        for cp in cps:
            cp.wait()

    return pl.pallas_call(
        body, name="pair_swap",
        out_shape=[jax.ShapeDtypeStruct((HROWS, SH_IN), F32), jax.ShapeDtypeStruct((HROWS, D), F32)],
        in_specs=[ANY, ANY], out_specs=[ANY, ANY],
        scratch_shapes=[pltpu.SemaphoreType.DMA((2,)), pltpu.SemaphoreType.DMA((2,))],
    )(red_in, red_rest)


def _flat(v, width=1024):
    v = v.reshape(-1)
    n = -(-v.shape[0] // width) * width
    return jnp.pad(v, (0, n - v.shape[0]))


def _rows(parts, rows):
    flat = jnp.concatenate(parts)
    return jnp.pad(flat, (0, rows * 1024 - flat.shape[0])).reshape(rows, 1024)


def _pack_small(b_ada, norm_w, conv_b, ssm_norm_w, q_norm_w, k_norm_w, sinks, dt_bias, a_log, d_skip, rel_bias,
                extra=None, tail=(), rows=16):
    misc = [q_norm_w, k_norm_w, sinks, dt_bias, a_log, d_skip] + ([] if extra is None else [extra])
    parts = [_flat(b_ada), _flat(norm_w), _flat(conv_b), _flat(ssm_norm_w)] + [_flat(v, 128) for v in misc]
    parts.append(jnp.zeros(((8 - len(misc)) * 128,), F32))
    parts.append(_flat(rel_bias))
    parts.append(jnp.zeros((5 * 1024,), F32))
    return _rows(parts + [_flat(v) for v in tail], rows)


def _unpack_small(p):
    misc = p[9]
    return dict(b_ada=p[0:3].reshape(1, 3072), norm_w=p[3:4], conv_b=p[4:7].reshape(1, 3072),
                ssm_norm_w=p[7:9].reshape(1, 2048), q_norm_w=misc[None, 0:64], k_norm_w=misc[None, 128:192],
                sinks=misc[None, 256:272], dt_bias=misc[None, 384:416], a_log=misc[None, 512:544],
                d_skip=misc[None, 640:672], rel_bias=p[10, :512].reshape(32, 16), extra=misc[768])


SMALL = ("b_ada", "norm_w", "conv_b", "ssm_norm_w", "q_norm_w", "k_norm_w", "sinks", "dt_bias", "a_log", "d_skip",
         "rel_bias")
WEIGHTS = ("w_ada", "b_ada", "norm_w", "w_in", "q_norm_w", "k_norm_w", "rel_bias", "sinks", "conv_w", "conv_b",
           "dt_bias", "a_log", "d_skip", "ssm_norm_w", "w_attn_proj", "w_ssm_proj", "w_out")
IN_COLS = ((0, 1024, C_Q), (1024, 256, C_K), (1280, 256, C_V), (1536, 1024, C_ZA), (2560, 2048, C_ZM),
           (4608, 3072, C_XBC), (7680, 32, C_DT), (7712, 1024, C_GA), (8736, 1024, C_GB))


def _to_cat(shards):
    parts, pos = [], 0
    for o, n, cnew in sorted(IN_COLS, key=lambda e: e[2]):
        assert cnew == pos
        c0 = o
        while c0 < o + n:
            i = c0 // SH_IN
            c1 = min(o + n, (i + 1) * SH_IN)
            parts.append(shards[i][:, c0 - i * SH_IN:c1 - i * SH_IN])
            c0 = c1
        pos += n
    parts.append(jnp.zeros((D, NP - pos), shards.dtype))
    return jnp.concatenate(parts, axis=1)


def _from_cat(dw_pieces):
    starts = [subs[0][1] for _, subs in DPIECES]

    def cols(c0, c1):
        p = max(q for q in range(len(starts)) if starts[q] <= c0)
        return dw_pieces[p][:, c0 - starts[p]:c1 - starts[p]]

    shards = []
    for i in range(4):
        lo, hi = i * SH_IN, (i + 1) * SH_IN
        parts = []
        for o, n, cnew in IN_COLS:
            a, b = max(o, lo), min(o + n, hi)
            if a < b:
                parts.append(cols(cnew + a - o, cnew + b - o))
        shards.append(jnp.concatenate(parts, axis=1))
    return jnp.stack(shards)


def kernel(x, c, w_ada, b_ada, norm_w, w_in, q_norm_w, k_norm_w, rel_bias, sinks, conv_w, conv_b, dt_bias, a_log, d_skip, ssm_norm_w, w_attn_proj, w_ssm_proj, w_out, loss_target, m_w_ada, m_b_ada, m_norm_w, m_w_in, m_q_norm_w, m_k_norm_w, m_rel_bias, m_sinks, m_conv_w, m_conv_b, m_dt_bias, m_a_log, m_d_skip, m_ssm_norm_w, m_w_attn_proj, m_w_ssm_proj, m_w_out, v_w_ada, v_b_ada, v_norm_w, v_w_in, v_q_norm_w, v_k_norm_w, v_rel_bias, v_sinks, v_conv_w, v_conv_b, v_dt_bias, v_a_log, v_d_skip, v_ssm_norm_w, v_w_attn_proj, v_w_ssm_proj, v_w_out):
    args = dict(locals())
    xi, yi, ci = lax.axis_index("x"), lax.axis_index("y"), lax.axis_index("c")
    chip = 2 * xi + yi
    me = 4 * xi + 2 * yi + ci
    x2 = x[0]
    tgt = loss_target[0]

    pay = _rows([c.reshape(-1), conv_w[0].reshape(-1)], 8)
    g0 = allgather_small(pay, "gather_cond")
    c_all = g0[:, 0, :]
    conv_w_full = g0[0::2, 1:4, :].reshape(4, CONV_K, 768).transpose(1, 0, 2).reshape(CONV_K, XBC)

    b_ada_sh = lax.dynamic_slice(b_ada, (0, chip * 768), (1, 768))
    mod_sh = ada_mod(c_all, w_ada[0], b_ada_sh)

    w_in_b = w_in[0].astype(BF16)
    w_rest_b = jnp.concatenate([w_attn_proj[0], w_ssm_proj[0], w_out[0]], axis=0).astype(BF16)
    wg_in, modg = gather_weights(w_in_b, mod_sh)
    wg_in = lax.dynamic_update_slice(wg_in, w_in_b[None], (chip, 0, 0))
    rs_sem, rr_sem, rest_thru, rest_tok = split_start("gather_rest_start", _rest_copies, [w_rest_b],
                                                      [lax.empty((4, D, D), BF16)], 6, modg)
    mod = lax.dynamic_slice(modg, (0, me, 0), (4, 1, 768)).reshape(1, 3 * D)
    shift, scale, gate = mod[:, :D], mod[:, D:2 * D] + rest_tok[:1, :1], mod[:, 2 * D:]
    wcat = _to_cat(wg_in)

    pad128 = lambda v: jnp.pad(v, ((0, 0), (0, 128 - v.shape[1])))
    dtb_p, alog_p, dsk_p = pad128(dt_bias), pad128(a_log), pad128(d_skip)
    bucket = _bucket_table()

    proj, dt_raw, h_t = norm_proj(x2, norm_w, scale, shift, wcat)
    biasm = bias_expand(rel_bias, sinks, bucket)
    ao = attn_fwd(proj, biasm, q_norm_w, k_norm_w)
    act, dsl = conv_fwd(proj, conv_w_full, conv_b)
    yss, sprev = ssd_fwd(act, dt_raw, dtb_p, alog_p, dsk_p)

    w_rest_b, wg_rest = split_wait("gather_rest_wait", _rest_copies, rs_sem, rr_sem, rest_thru, 1, yss)
    wg_rest = lax.dynamic_update_slice(wg_rest, w_rest_b[None], (chip, 0, 0))
    w_at = wg_rest[:, :R_AT].reshape(D, D)
    w_ss = wg_rest[:, R_AT:R_AT + R_SS].reshape(SSM_W, D)
    w_ou = wg_rest[:, R_AT + R_SS:].reshape(D, D)
    (loss_p, dy, dao, dmid, dyss, ua_t, yn_t, mg_t, dya, dyb, dout, dgate, dssm_nw) = tail(
        proj, ao, yss, x2, tgt, gate, ssm_norm_w, w_at, w_ss, w_ou)

    dq, dkv, dqw, dkw, dacc = attn_bwd(proj, dao, biasm, q_norm_w, k_norm_w)
    dbias = bias_reduce(dacc, bucket)
    drb = dbias[:, :NBUCKET].T
    dsk = dbias[:, NBUCKET].reshape(1, HQ)
    dact, ddt, ddtb, dalog, ddskip = ssd_bwd(act, dt_raw, dyss, sprev, dtb_p, alog_p, dsk_p)
    dxbc, dconv_w, dconv_b = conv_bwd(proj, dact, dsl, conv_w_full)

    dproj = (dq, dmid, dxbc, dkv, ddt)
    dwcat = [wgrad(h_t, piece, "dw_in_%d" % p, min(piece.shape[1], 1024), rest_tok) for p, piece in enumerate(dproj)]

    g_in = _from_cat(dwcat)
    ps_sem, pr_sem, pair_thru, pair_tok = split_start("pair_in_start", _pair_copies, [g_in],
                                                      [lax.empty((4, HROWS, SH_IN), F32)], 1, loss_p)
    dw_at = wgrad(ua_t, dya, "dw_attn", 512, pair_tok)
    dw_ss = wgrad(yn_t, dyb, "dw_ssm", 512, pair_tok)
    dw_ou = wgrad(mg_t, dout, "dw_out", 512, pair_tok)
    g_rest = jnp.concatenate([dw_at.reshape(4, R_AT, D), dw_ss.reshape(4, R_SS, D), dw_ou.reshape(4, R_OU, D)], axis=1)
    sib_rest = pair_exchange(g_rest)
    g_in, sib_in = split_wait("pair_in_wait", _pair_copies, ps_sem, pr_sem, pair_thru, 1, sib_rest)
    part_in, pb_in = pair_sum(g_in, ci, sib_in, "pair_sum_in")
    part_rest, pb_rest = pair_sum(g_rest, ci, sib_rest, "pair_sum_rest")
    cs_sem, cr_sem, chip_thru, token = split_start(
        "chip_exchange_start", _chip_copies, [pb_in, pb_rest],
        [lax.empty((3, HROWS, SH_IN), BF16), lax.empty((3, HROWS, D), BF16)], 6, part_rest)
    grad_x, dnorm_w, dscale, dshift = dproj_bwd(dproj, wcat, x2, dy, norm_w, scale + token[:1, :1])
    _, _, oth_in, oth_rest = split_wait("chip_exchange_wait", _chip_copies, cs_sem, cr_sem, chip_thru, 2, dshift)
    red_in = chip_sum(part_in, chip, oth_in, "chip_sum_in")
    red_rest = chip_sum(part_rest, chip, oth_rest, "chip_sum_rest")
    recv_in, recv_rest = pair_swap(red_in, red_rest)
    both = lambda mine, theirs: jnp.concatenate([jnp.where(ci == 0, mine, theirs), jnp.where(ci == 0, theirs, mine)],
                                                axis=0)
    g_shard_in = both(red_in, recv_in)
    g_shard_rest = both(red_rest, recv_rest)

    dmod = jnp.concatenate([dshift, dscale, dgate], axis=1)
    gsmall = _pack_small(dmod, dnorm_w, dconv_b, dssm_nw, dqw, dkw, dsk[:, :HQ], ddtb[:, :SH], dalog[:, :SH],
                         ddskip[:, :SH], drb, extra=loss_p[:, :1], tail=(dconv_w,), rows=32)
    gall = allgather_small(gsmall, "gather_small_grads")
    ssum = sum_devices(gall)
    gs = _unpack_small(ssum[:16])
    loss = gs["extra"]
    dconv_w_sh = lax.dynamic_slice(ssum[16:28].reshape(CONV_K, XBC), (0, chip * 768), (CONV_K, 768))
    dmod_all = gall[:, 0:3, :].reshape(8, 3 * D)
    dw_ada = ada_grad(c_all, lax.dynamic_slice(dmod_all, (0, chip * 768), (8, 768)))

    grads = dict(gs)
    grads["w_ada"] = dw_ada
    grads["w_in"] = g_shard_in
    grads["w_attn_proj"] = g_shard_rest[:R_AT]
    grads["w_ssm_proj"] = g_shard_rest[R_AT:R_AT + R_SS]
    grads["w_out"] = g_shard_rest[R_AT + R_SS:]
    grads["conv_w"] = dconv_w_sh

    delta, new_m, new_v = {}, {}, {}
    for n in ("w_ada", "w_in", "conv_w", "w_attn_proj", "w_ssm_proj", "w_out"):
        delta[n], new_m[n], new_v[n] = adamw(args[n][0], grads[n], args["m_" + n][0], args["v_" + n][0], "adamw_" + n)
    ws = _pack_small(*[args[n] for n in SMALL])
    ms = _pack_small(*[args["m_" + n] for n in SMALL])
    vs = _pack_small(*[args["v_" + n] for n in SMALL])
    d_s, m_s, v_s = adamw(ws, ssum[:16], ms, vs, "adamw_small")
    d_s, m_s, v_s = _unpack_small(d_s), _unpack_small(m_s), _unpack_small(v_s)
    for n in SMALL:
        delta[n], new_m[n], new_v[n] = d_s[n], m_s[n], v_s[n]

    def shaped(n, a):
        return a.reshape(args[n].shape)

    outs = [loss, grad_x[None]]
    for table in (grads, delta, new_m, new_v):
        outs += [shaped(n, table[n]) for n in WEIGHTS]
    return tuple(outs)
```

```python
import functools
import math

import jax
import jax.numpy as jnp
from jax import lax
from jax.experimental import pallas as pl
from jax.experimental.pallas import tpu as pltpu

F32 = jnp.float32
BF16 = jnp.bfloat16
MESH = pl.DeviceIdType.MESH

D = 1024
HQ, HKV, GRP, DH = 16, 4, 4, 64
BLK = 128
NBUCKET, MAXDIST = 32, 128
SSM_W, SH, SG, SR, SP, SN = 2048, 32, 4, 8, 64, 128
CONV_K = 4
XBC = SSM_W + 2 * SG * SN
IN_W = 9760
EPS = 1e-6
NEG = -1e30
SCALE = DH ** -0.5

C_Q, C_ZA, C_GA, C_GB, C_ZM, C_XBC, C_K, C_V, C_DT = 0, 1024, 2048, 3072, 4096, 6144, 9216, 9472, 9728
NP = 9984
TN = 1664
W_MID = C_XBC - C_ZA

SH_IN = IN_W // 4
R_AT, R_SS, R_OU = 256, 512, 256
HROWS = D // 2

ADAM_LR, ADAM_B1, ADAM_B2, ADAM_EPS, ADAM_WD, ADAM_STEP = 0.001, 0.9, 0.999, 1e-08, 0.01, 10

VMEM_LIMIT = 56 * 1024 * 1024


def _cp(sem=None):
    if sem is None:
        return pltpu.CompilerParams(vmem_limit_bytes=VMEM_LIMIT)
    return pltpu.CompilerParams(dimension_semantics=sem, vmem_limit_bytes=VMEM_LIMIT)


def _sig(x):
    return 0.5 * jnp.tanh(0.5 * x) + 0.5


def _dot(a, b):
    return jnp.dot(a, b, preferred_element_type=F32)


def _dot_nt(a, b):
    return lax.dot_general(a, b, (((1,), (1,)), ((), ())), preferred_element_type=F32)


def _dot_tn(a, b):
    return lax.dot_general(a, b, (((0,), (0,)), ((), ())), preferred_element_type=F32)


def _rsum(x):
    return jnp.sum(x, axis=-1, keepdims=True)


def _csum(x):
    return jnp.sum(x, axis=0, keepdims=True)


def _asum(x):
    return _csum(_rsum(x))


def _full(shape):
    nd = len(shape)
    return pl.BlockSpec(shape, lambda *_: (0,) * nd)


def ada_mod(c_all, w_ada_sh, b_ada_sh):
    def body(c_ref, w_ref, b_ref, o_ref):
        cv = c_ref[...]
        s = cv * _sig(cv)
        o_ref[...] = jnp.dot(s, w_ref[...], preferred_element_type=F32,
                             precision=lax.Precision.HIGHEST) + b_ref[...]

    n = w_ada_sh.shape[1]
    return pl.pallas_call(body, name="ada_mod", out_shape=jax.ShapeDtypeStruct((8, n), F32),
                          compiler_params=_cp())(c_all, w_ada_sh, b_ada_sh)


def ada_grad(c_all, dmod_sh):
    def body(c_ref, d_ref, o_ref):
        cv = c_ref[...]
        s = cv * _sig(cv)
        o_ref[...] = lax.dot_general(s, d_ref[...], (((0,), (0,)), ((), ())), preferred_element_type=F32,
                                     precision=lax.Precision.HIGHEST)

    n = dmod_sh.shape[1]
    return pl.pallas_call(body, name="ada_grad", out_shape=jax.ShapeDtypeStruct((D, n), F32),
                          compiler_params=_cp())(c_all, dmod_sh)


def norm_proj(x, norm_w, scale, shift, wcat):
    t = x.shape[0]
    tm = min(t, 1024)

    def body(x_ref, nw_ref, sc_ref, sh_ref, w_ref, p_ref, dt_ref, ht_ref, hs):
        @pl.when(pl.program_id(1) == 0)
        def _():
            xv = x_ref[...]
            r = lax.rsqrt(jnp.mean(xv * xv, axis=-1, keepdims=True) + EPS)
            h = (xv * r) * nw_ref[...]
            h = h * (1.0 + sc_ref[...]) + sh_ref[...]
            hs[...] = h.astype(BF16)
            ht_ref[...] = h.T.astype(BF16)

        p = _dot(hs[...], w_ref[...])
        p_ref[...] = p.astype(BF16)

        @pl.when(pl.program_id(1) == C_DT // TN)
        def _():
            dt_ref[...] = p[:, C_DT % TN:C_DT % TN + 128]

    vec = pl.BlockSpec((1, D), lambda i, j: (0, 0))
    return pl.pallas_call(
        body, name="norm_proj", grid=(t // tm, NP // TN),
        in_specs=[pl.BlockSpec((tm, D), lambda i, j: (i, 0)), vec, vec, vec,
                  pl.BlockSpec((D, TN), lambda i, j: (0, j))],
        out_specs=[pl.BlockSpec((tm, TN), lambda i, j: (i, j)), pl.BlockSpec((tm, 128), lambda i, j: (i, 0)),
                   pl.BlockSpec((D, tm), lambda i, j: (0, i))],
        out_shape=[jax.ShapeDtypeStruct((t, NP), BF16), jax.ShapeDtypeStruct((t, 128), F32),
                   jax.ShapeDtypeStruct((D, t), BF16)],
        scratch_shapes=[pltpu.VMEM((tm, D), BF16)],
        compiler_params=_cp(("parallel", "arbitrary")),
    )(x, norm_w, scale, shift, wcat)


def _bucket_table():
    qi = jnp.arange(BLK)[:, None]
    kj = jnp.arange(2 * BLK)[None, :]
    dist = qi + BLK - kj
    n = jnp.maximum(dist, 0)
    max_exact = NBUCKET // 2
    nf = jnp.maximum(n, 1).astype(F32)
    large = max_exact + (jnp.log(nf / max_exact) / math.log(MAXDIST / max_exact)
                         * (NBUCKET - max_exact)).astype(jnp.int32)
    large = jnp.minimum(large, NBUCKET - 1)
    bucket = jnp.where(n < max_exact, n, large).astype(jnp.int32)
    valid = (dist >= 0) & (dist < BLK)
    return jnp.where(valid, bucket, -1)


def bias_expand(rel_bias, sinks, bucket):
    def body(rb_ref, sk_ref, bk_ref, o_ref):
        hd = pl.program_id(0)
        bk = bk_ref[...]
        col = lax.broadcasted_iota(jnp.int32, (BLK, 2 * BLK), 1)

        def step(b, acc):
            return jnp.where(bk == b, rb_ref[b, hd], acc)

        acc = lax.fori_loop(0, NBUCKET, step, jnp.full((BLK, 2 * BLK), NEG, F32))
        acc = jnp.where(col == 0, sk_ref[0, hd], acc)
        o_ref[1, 0] = acc
        o_ref[0, 0] = jnp.where(jnp.logical_and(col > 0, col < BLK), NEG, acc)

    smem = pl.BlockSpec(memory_space=pltpu.SMEM)
    return pl.pallas_call(
        body, name="bias_expand", grid=(HQ,),
        in_specs=[smem, smem, _full((BLK, 2 * BLK))],
        out_specs=pl.BlockSpec((2, 1, BLK, 2 * BLK), lambda h: (0, h, 0, 0)),
        out_shape=jax.ShapeDtypeStruct((2, HQ, BLK, 2 * BLK), F32),
        compiler_params=_cp(("arbitrary",)),
    )(rel_bias, sinks, bucket)


def bias_reduce(dacc, bucket):
    col = jnp.arange(BLK * 2 * BLK, dtype=jnp.int32) % (2 * BLK)
    lane = jnp.arange(128, dtype=jnp.int32)[None, :]
    member = (bucket.reshape(-1)[:, None] == lane) | ((col[:, None] == 0) & (lane == NBUCKET))

    def body(d_ref, m_ref, o_ref):
        mm = m_ref[...]
        o_ref[...] = sum(_dot(part, mm) for part in _split3(d_ref[...]))

    return pl.pallas_call(body, name="bias_reduce", out_shape=jax.ShapeDtypeStruct((HQ, 128), F32),
                          compiler_params=_cp())(dacc.reshape(HQ, BLK * 2 * BLK), member.astype(BF16))


GQ = GRP * BLK


def _stack_heads(x, nh):
    return jnp.concatenate([x[:, DH * h:DH * (h + 1)] for h in range(nh)], axis=0)


def _unstack(xs, nh):
    rows = xs.shape[0] // nh
    return jnp.concatenate([xs[rows * h:rows * (h + 1)] for h in range(nh)], axis=1)


def _rms(x):
    return lax.rsqrt(jnp.mean(x * x, axis=-1, keepdims=True) + EPS)


def _stack_q(q, qw):
    qs = _stack_heads(q, HQ)
    r = _rms(qs)
    qhat = qs * r
    return qhat * qw, qhat, r


def _band_first(shape):
    return (lax.broadcasted_iota(jnp.int32, shape, 0) & (2 * BLK - 1)) == 0


def _stack_kv(kp, kc, vp, vc, kw):
    ks = _stack_heads(jnp.concatenate([kp, kc], axis=0), HKV)
    r = _rms(ks)
    khat = ks * r
    first = _band_first(ks.shape)
    kn = jnp.where(first, 0.0, khat * kw)
    v2 = jnp.where(first, 0.0, _stack_heads(jnp.concatenate([vp, vc], axis=0), HKV)).astype(BF16)
    return kn, khat, r, v2


def _softmax_rows(s):
    p = jnp.exp(s - jnp.max(s, axis=-1, keepdims=True))
    return p * (1.0 / _rsum(p))


def attn_fwd(proj, biasm, q_norm_w, k_norm_w):
    t = proj.shape[0]
    nb = t // BLK

    def body(q_ref, kc_ref, kp_ref, vc_ref, vp_ref, bm_ref, qw_ref, kw_ref, o_ref):
        f = lambda ref: ref[...].astype(F32)
        qn = _stack_q(f(q_ref), qw_ref[...])[0].astype(BF16)
        kn, _, _, v2 = _stack_kv(f(kp_ref), f(kc_ref), f(vp_ref), f(vc_ref), kw_ref[...])
        knb = kn.astype(BF16)
        s = jnp.concatenate([_dot_nt(qn[GQ * j:GQ * (j + 1)], knb[2 * BLK * j:2 * BLK * (j + 1)])
                             for j in range(HKV)], axis=0)
        pr = _softmax_rows(s * SCALE + bm_ref[0].reshape(HQ * BLK, 2 * BLK)).astype(BF16)
        o = jnp.concatenate([_dot(pr[GQ * j:GQ * (j + 1)], v2[2 * BLK * j:2 * BLK * (j + 1)])
                             for j in range(HKV)], axis=0)
        o_ref[...] = _unstack(o, HQ).astype(BF16)

    kblk, vblk = C_K // 256, C_V // 256
    prev = lambda n: jnp.maximum(n - 1, 0)
    return pl.pallas_call(
        body, name="attn_fwd", grid=(nb,),
        in_specs=[pl.BlockSpec((BLK, D), lambda n: (n, 0)),
                  pl.BlockSpec((BLK, 256), lambda n: (n, kblk)),
                  pl.BlockSpec((BLK, 256), lambda n: (prev(n), kblk)),
                  pl.BlockSpec((BLK, 256), lambda n: (n, vblk)),
                  pl.BlockSpec((BLK, 256), lambda n: (prev(n), vblk)),
                  pl.BlockSpec((1, HQ, BLK, 2 * BLK), lambda n: (jnp.minimum(n, 1), 0, 0, 0)),
                  _full((1, DH)), _full((1, DH))],
        out_specs=pl.BlockSpec((BLK, D), lambda n: (n, 0)),
        out_shape=jax.ShapeDtypeStruct((t, D), BF16),
        compiler_params=_cp(("parallel",)),
    )(proj, proj, proj, proj, proj, biasm, q_norm_w, k_norm_w)


def attn_bwd(proj, dao, biasm, q_norm_w, k_norm_w):
    t = proj.shape[0]
    nb = t // BLK
    kb = 2 * BLK

    def body(q_ref, kc_ref, kp_ref, vc_ref, vp_ref, do_ref, bm_ref, qw_ref, kw_ref,
             dq_ref, dkv_ref, dqw_ref, dkw_ref, dacc_ref, ck, cv, pk, pv, nk, nv):
        n = pl.program_id(0)

        @pl.when(n == 0)
        def _():
            for ref in (dqw_ref, dkw_ref, dacc_ref, ck, cv):
                ref[...] = jnp.zeros_like(ref)

        qw = qw_ref[...]
        kw = kw_ref[...]
        f = lambda ref: ref[...].astype(F32)
        kn, khat, rk, v2 = _stack_kv(f(kp_ref), f(kc_ref), f(vp_ref), f(vc_ref), kw)
        grp = lambda a, j: a[GQ * j:GQ * (j + 1)]
        band = lambda a, j: a[kb * j:kb * (j + 1)]

        @pl.when(n < nb)
        def _():
            qn, qhat, rq = _stack_q(f(q_ref), qw)
            qnb = qn.astype(BF16)
            knb = kn.astype(BF16)
            dos = _stack_heads(f(do_ref), HQ).astype(BF16)
            s = jnp.concatenate([_dot_nt(grp(qnb, j), band(knb, j)) for j in range(HKV)], axis=0)
            pr = _softmax_rows(s * SCALE + bm_ref[0].reshape(HQ * BLK, kb))
            dp = jnp.concatenate([_dot_nt(grp(dos, j), band(v2, j)) for j in range(HKV)], axis=0)
            ds = pr * (dp - _rsum(pr * dp))
            dacc_ref[...] += ds.reshape(HQ, BLK, kb)
            dsb = ds.astype(BF16)
            prb = pr.astype(BF16)
            dqn = jnp.concatenate([_dot(grp(dsb, j), band(knb, j)) for j in range(HKV)], axis=0) * SCALE
            dqhat = dqn * qw
            dq = rq * (dqhat - qhat * jnp.mean(dqhat * qhat, axis=-1, keepdims=True))
            dq_ref[...] = _unstack(dq, HQ).astype(BF16)
            dqw_ref[...] += _csum(dqn * qhat)
            first = _band_first((kb, DH))
            for j in range(HKV):
                rows = slice(BLK * j, BLK * (j + 1))
                dkn = jnp.where(first, 0.0, _dot_tn(grp(dsb, j), grp(qnb, j)) * SCALE)
                dvj = jnp.where(first, 0.0, _dot_tn(grp(prb, j), grp(dos, j)))
                pk[rows, :] = dkn[:BLK]
                nk[rows, :] = dkn[BLK:]
                pv[rows, :] = dvj[:BLK]
                nv[rows, :] = dvj[BLK:]

        @pl.when(n == nb)
        def _():
            for ref in (pk, pv, nk, nv):
                ref[...] = jnp.zeros_like(ref)

        khp = jnp.concatenate([khat[kb * j:kb * j + BLK] for j in range(HKV)], axis=0)
        rkp = jnp.concatenate([rk[kb * j:kb * j + BLK] for j in range(HKV)], axis=0)
        dkn = ck[...] + pk[...]
        dkhat = dkn * kw
        dk = rkp * (dkhat - khp * jnp.mean(dkhat * khp, axis=-1, keepdims=True))
        dkw_ref[...] += _csum(dkn * khp)
        dkv_ref[...] = jnp.concatenate([_unstack(dk, HKV), _unstack(cv[...] + pv[...], HKV)], axis=1).astype(BF16)
        ck[...] = nk[...]
        cv[...] = nv[...]

    kblk, vblk = C_K // 256, C_V // 256
    cur = lambda n: jnp.minimum(n, nb - 1)
    prev = lambda n: jnp.maximum(n - 1, 0)
    carry = pltpu.VMEM((HKV * BLK, DH), F32)
    return pl.pallas_call(
        body, name="attn_bwd", grid=(nb + 1,),
        in_specs=[pl.BlockSpec((BLK, D), lambda n: (cur(n), 0)),
                  pl.BlockSpec((BLK, 256), lambda n: (cur(n), kblk)), pl.BlockSpec((BLK, 256), lambda n: (prev(n), kblk)),
                  pl.BlockSpec((BLK, 256), lambda n: (cur(n), vblk)), pl.BlockSpec((BLK, 256), lambda n: (prev(n), vblk)),
                  pl.BlockSpec((BLK, D), lambda n: (cur(n), 0)),
                  pl.BlockSpec((1, HQ, BLK, kb), lambda n: (jnp.minimum(n, 1), 0, 0, 0)),
                  _full((1, DH)), _full((1, DH))],
        out_specs=[pl.BlockSpec((BLK, D), lambda n: (cur(n), 0)),
                   pl.BlockSpec((BLK, 512), lambda n: (prev(n), 0)),
                   _full((1, DH)), _full((1, DH)), _full((HQ, BLK, kb))],
        out_shape=[jax.ShapeDtypeStruct((t, D), BF16), jax.ShapeDtypeStruct((t, 512), BF16),
                   jax.ShapeDtypeStruct((1, DH), F32),
                   jax.ShapeDtypeStruct((1, DH), F32), jax.ShapeDtypeStruct((HQ, BLK, kb), F32)],
        scratch_shapes=[carry] * 6,
        compiler_params=_cp(("arbitrary",)),
    )(proj, proj, proj, proj, proj, dao, biasm, q_norm_w, k_norm_w)


CONV_TM, CONV_CW, CONV_RC, HALO = 512, 1024, 32, 16


def conv_fwd(proj, conv_w, conv_b):
    t = proj.shape[0]
    tm = min(t, CONV_TM)
    c0 = C_XBC // CONV_CW

    def body(x_ref, xp_ref, w_ref, b_ref, o_ref, ds_ref):
        i = pl.program_id(1)
        w = w_ref[...]
        b = b_ref[...]
        for r in range(tm // CONV_RC):
            lo = r * CONV_RC
            if r == 0:
                head = jnp.where(i == 0, 0.0, xp_ref[...].astype(F32))
                win = jnp.concatenate([head, x_ref[0:CONV_RC, :].astype(F32)], axis=0)
            else:
                win = x_ref[lo - HALO:lo + CONV_RC, :].astype(F32)
            acc = b
            for j in range(CONV_K):
                acc = acc + w[j:j + 1] * win[HALO - 3 + j:HALO - 3 + j + CONV_RC]
            sg = _sig(acc)
            o_ref[lo:lo + CONV_RC, :] = acc * sg
            ds_ref[lo:lo + CONV_RC, :] = _dsilu(acc, sg).astype(BF16)

    rh = tm // HALO
    tile = pl.BlockSpec((tm, CONV_CW), lambda s, i: (i, s))
    return pl.pallas_call(
        body, name="conv_fwd", grid=(XBC // CONV_CW, t // tm),
        in_specs=[pl.BlockSpec((tm, CONV_CW), lambda s, i: (i, c0 + s)),
                  pl.BlockSpec((HALO, CONV_CW), lambda s, i: (jnp.maximum(i * rh - 1, 0), c0 + s)),
                  pl.BlockSpec((CONV_K, CONV_CW), lambda s, i: (0, s)), pl.BlockSpec((1, CONV_CW), lambda s, i: (0, s))],
        out_specs=[tile, tile],
        out_shape=[jax.ShapeDtypeStruct((t, XBC), F32), jax.ShapeDtypeStruct((t, XBC), BF16)],
        compiler_params=_cp(("parallel", "parallel")),
    )(proj, proj, conv_w, conv_b)


def conv_bwd(proj, dact, dsl, conv_w):
    t = proj.shape[0]
    tm = min(t, CONV_TM)
    nt = t // tm
    nr = tm // CONV_RC
    c0 = C_XBC // CONV_CW
    ext = CONV_RC + 8

    def body(x_ref, xp_ref, d_ref, dn_ref, s_ref, sn_ref, w_ref, dx_ref, dw_ref, db_ref):
        i = pl.program_id(1)

        @pl.when(i == 0)
        def _():
            dw_ref[...] = jnp.zeros_like(dw_ref)
            db_ref[...] = jnp.zeros_like(db_ref)

        w = w_ref[...]
        dws = [jnp.zeros((1, CONV_CW), F32) for _ in range(CONV_K)]
        db = jnp.zeros((1, CONV_CW), F32)
        for r in range(nr):
            lo = r * CONV_RC
            if r == 0:
                head = jnp.where(i == 0, 0.0, xp_ref[...].astype(F32))
                win = jnp.concatenate([head, x_ref[0:CONV_RC, :].astype(F32)], axis=0)
            else:
                win = x_ref[lo - HALO:lo + CONV_RC, :].astype(F32)
            if r < nr - 1:
                dext = d_ref[lo:lo + ext, :]
                sext = s_ref[lo:lo + CONV_RC + HALO, :].astype(F32)[0:ext]
            else:
                dext = jnp.concatenate([d_ref[lo:lo + CONV_RC, :], jnp.where(i == nt - 1, 0.0, dn_ref[...])], axis=0)
                sext = jnp.concatenate([s_ref[lo:lo + CONV_RC, :].astype(F32), sn_ref[...].astype(F32)], axis=0)[0:ext]
            dpre = dext * sext
            dx = jnp.zeros((CONV_RC, CONV_CW), F32)
            own = dpre[0:CONV_RC]
            for j in range(CONV_K):
                dx = dx + w[j:j + 1] * dpre[3 - j:3 - j + CONV_RC]
                dws[j] = dws[j] + _csum(own * win[HALO - 3 + j:HALO - 3 + j + CONV_RC])
            db = db + _csum(own)
            dx_ref[lo:lo + CONV_RC, :] = dx.astype(BF16)
        dw_ref[...] += jnp.concatenate(dws, axis=0)
        db_ref[...] += db

    rh = tm // HALO
    r8 = tm // 8
    nxt = lambda i, per: jnp.minimum((i + 1) * per, nt * per - 1)
    return pl.pallas_call(
        body, name="conv_bwd", grid=(XBC // CONV_CW, nt),
        in_specs=[pl.BlockSpec((tm, CONV_CW), lambda s, i: (i, c0 + s)),
                  pl.BlockSpec((HALO, CONV_CW), lambda s, i: (jnp.maximum(i * rh - 1, 0), c0 + s)),
                  pl.BlockSpec((tm, CONV_CW), lambda s, i: (i, s)),
                  pl.BlockSpec((8, CONV_CW), lambda s, i: (nxt(i, r8), s)),
                  pl.BlockSpec((tm, CONV_CW), lambda s, i: (i, s)),
                  pl.BlockSpec((HALO, CONV_CW), lambda s, i: (nxt(i, rh), s)),
                  pl.BlockSpec((CONV_K, CONV_CW), lambda s, i: (0, s))],
        out_specs=[pl.BlockSpec((tm, CONV_CW), lambda s, i: (i, s)),
                   pl.BlockSpec((CONV_K, CONV_CW), lambda s, i: (0, s)), pl.BlockSpec((1, CONV_CW), lambda s, i: (0, s))],
        out_shape=[jax.ShapeDtypeStruct((t, XBC), BF16), jax.ShapeDtypeStruct((CONV_K, XBC), F32),
                   jax.ShapeDtypeStruct((1, XBC), F32)],
        compiler_params=_cp(("parallel", "arbitrary")),
    )(proj, proj, dact, dact, dsl, dsl, conv_w)


def _split3(x):
    h = x.astype(BF16)
    r = x - h.astype(F32)
    m = r.astype(BF16)
    lo = (r - m.astype(F32)).astype(BF16)
    return h, m, lo


def _tri_mm(tri, x):
    h, m, lo = _split3(x)
    return _dot(tri, h) + _dot(tri, m) + _dot(tri, lo)


def _softplus(x):
    return jnp.maximum(x, 0.0) + jnp.log1p(jnp.exp(-jnp.abs(x)))


def _chunk_decays(dt_raw, dtb, alog):
    dtv = _softplus(dt_raw + dtb)
    a = -jnp.exp(alog)
    ri = lax.broadcasted_iota(jnp.int32, (BLK, BLK), 0)
    ci = lax.broadcasted_iota(jnp.int32, (BLK, BLK), 1)
    causal = ri >= ci
    acum = _tri_mm(causal.astype(BF16), dtv * a)
    return dtv, a, causal, acum, acum.T


NPAIR = SH // 2


def _pairs(x):
    return jnp.stack([x[:, 128 * k:128 * (k + 1)] for k in range(NPAIR)])


def _unpairs(x3):
    return jnp.concatenate([x3[k] for k in range(NPAIR)], axis=1)


def _per_head_cols(m):
    return jnp.stack([jnp.broadcast_to(m[:, h:h + 1], m.shape) for h in range(SH)])


def _pair_lanes(t):
    r = t.reshape(NPAIR, 2, t.shape[1], 128)
    lo = lax.broadcasted_iota(jnp.int32, (1, t.shape[1], 128), 2) < SP
    return jnp.where(lo, r[:, 0], r[:, 1])


class _Chunk:
    pass


def _chunk_common(dt_raw, dtb, alog, dskip):
    cm = _Chunk()
    cm.dtv, cm.a, cm.causal, acum, acum_t = _chunk_decays(dt_raw, dtb, alog)
    cm.acol = _per_head_cols(acum)
    cm.arow = jnp.stack([acum_t[h:h + 1, :] for h in range(SH)])
    apl = _pair_lanes(cm.acol)
    alast = apl[:, BLK - 1:BLK, :]
    cm.dpl = _pair_lanes(_per_head_cols(cm.dtv))
    cm.eapl = jnp.exp(apl)
    cm.epl = jnp.exp(alast - apl)
    cm.cdpl = jnp.exp(alast)
    cm.dskpl = _pair_lanes(_per_head_cols(dskip))
    cm.lo = lax.broadcasted_iota(jnp.int32, (1, BLK, 128), 2) < SP
    return cm


def ssd_fwd(act, dt_raw, dtb_p, alog_p, dsk_p):
    t = act.shape[0]
    nc = t // BLK

    def body(xs_ref, b_ref, c_ref, dt_ref, dtb_ref, al_ref, dk_ref, y_ref, sp_ref, st):
        c = pl.program_id(0)

        @pl.when(c == 0)
        def _():
            st[...] = jnp.zeros_like(st)

        s_t = st[...]
        sp_ref[0] = s_t
        cm = _chunk_common(dt_ref[...], dtb_ref[...], al_ref[...], dk_ref[...])
        gms, cbs, bts = [], [], []
        for g in range(SG):
            bf = b_ref[:, SN * g:SN * (g + 1)]
            cb = c_ref[:, SN * g:SN * (g + 1)].astype(BF16)
            gms.append(_dot_nt(cb, bf.astype(BF16)))
            cbs.append(cb)
            bts.append(bf.T.astype(BF16))
        lam = jnp.exp(jnp.where(cm.causal[None], cm.acol - cm.arow, NEG))
        m = (lam.reshape(SG, SR, BLK, BLK) * jnp.stack(gms)[:, None]).reshape(SH, BLK, BLK).astype(BF16)
        xs16 = _pairs(xs_ref[...])
        xdt16 = xs16 * cm.dpl
        x_lo = jnp.where(cm.lo, xdt16, 0.0).astype(BF16)
        x_hi = jnp.where(cm.lo, 0.0, xdt16).astype(BF16)
        s16 = _pairs(s_t)
        s16b = s16.astype(BF16)
        yd = jnp.stack([_dot(m[2 * k], x_lo[k]) + _dot(m[2 * k + 1], x_hi[k]) for k in range(NPAIR)])
        yo = jnp.stack([_dot(cbs[k // (NPAIR // SG)], s16b[k]) for k in range(NPAIR)])
        y_ref[...] = _unpairs(yd + yo * cm.eapl + cm.dskpl * xs16).astype(BF16)
        xe = (xdt16 * cm.epl).astype(BF16)
        st[...] = _unpairs(cm.cdpl * s16 + jnp.stack([_dot(bts[k // (NPAIR // SG)], xe[k]) for k in range(NPAIR)]))

    vec = _full((1, 128))
    return pl.pallas_call(
        body, name="ssd_fwd", grid=(nc,),
        in_specs=[pl.BlockSpec((BLK, SSM_W), lambda c: (c, 0)),
                  pl.BlockSpec((BLK, SG * SN), lambda c: (c, SSM_W // (SG * SN))),
                  pl.BlockSpec((BLK, SG * SN), lambda c: (c, SSM_W // (SG * SN) + 1)),
                  pl.BlockSpec((BLK, 128), lambda c: (c, 0)), vec, vec, vec],
        out_specs=[pl.BlockSpec((BLK, SSM_W), lambda c: (c, 0)), pl.BlockSpec((1, SN, SSM_W), lambda c: (c, 0, 0))],
        out_shape=[jax.ShapeDtypeStruct((t, SSM_W), BF16), jax.ShapeDtypeStruct((nc, SN, SSM_W), F32)],
        scratch_shapes=[pltpu.VMEM((SN, SSM_W), F32)],
        compiler_params=_cp(("arbitrary",)),
    )(act, act, act, dt_raw, dtb_p, alog_p, dsk_p)


def _head_sums(q):
    r = q.shape[1]
    lo = lax.broadcasted_iota(jnp.int32, (1, r, 128), 2) < SP
    s_lo = jnp.sum(jnp.where(lo, q, 0.0), axis=-1, keepdims=True)
    s_hi = jnp.sum(jnp.where(lo, 0.0, q), axis=-1, keepdims=True)
    lane = lax.broadcasted_iota(jnp.int32, (r, 128), 1)
    out = jnp.zeros((r, 128), F32)
    for k in range(NPAIR):
        out = jnp.where(lane == 2 * k, s_lo[k], jnp.where(lane == 2 * k + 1, s_hi[k], out))
    return out


def ssd_bwd(act, dt_raw, dy, sprev, dtb_p, alog_p, dsk_p):
    t = act.shape[0]
    nc = t // BLK

    def body(xs_ref, b_ref, c_ref, dt_ref, dy_ref, sp_ref, dtb_ref, al_ref, dk_ref,
             da_ref, ddt_ref, ddtb_ref, dal_ref, ddk_ref, dst):
        i = pl.program_id(0)

        @pl.when(i == 0)
        def _():
            dst[...] = jnp.zeros_like(dst)
            ddtb_ref[...] = jnp.zeros_like(ddtb_ref)
            dal_ref[...] = jnp.zeros_like(dal_ref)
            ddk_ref[...] = jnp.zeros_like(ddk_ref)

        dt_raw = dt_ref[...]
        dtb = dtb_ref[...]
        cm = _chunk_common(dt_raw, dtb, al_ref[...], dk_ref[...])
        ri = lax.broadcasted_iota(jnp.int32, (BLK, BLK), 0)
        ci = lax.broadcasted_iota(jnp.int32, (BLK, BLK), 1)
        lam_t = jnp.exp(jnp.where((ri <= ci)[None], cm.arow - cm.acol, NEG))
        bbs, cbs, cts, gms = [], [], [], []
        for g in range(SG):
            bf = b_ref[:, SN * g:SN * (g + 1)]
            cf = c_ref[:, SN * g:SN * (g + 1)]
            bbs.append(bf.astype(BF16))
            cbs.append(cf.astype(BF16))
            cts.append(cf.T.astype(BF16))
            gms.append(_dot_nt(bbs[g], cbs[g]))
        grp = lambda k: k // (NPAIR // SG)
        xs16 = _pairs(xs_ref[...])
        dy16 = _pairs(dy_ref[...].astype(F32))
        sp16 = _pairs(sp_ref[0])
        ds16 = _pairs(dst[...])
        xdt16 = xs16 * cm.dpl
        xdtb = xdt16.astype(BF16)
        dyh = [jnp.where(cm.lo, dy16, 0.0).astype(BF16), jnp.where(cm.lo, 0.0, dy16).astype(BF16)]
        m_t = (lam_t.reshape(SG, SR, BLK, BLK) * jnp.stack(gms)[:, None]).reshape(SH, BLK, BLK).astype(BF16)
        dxdt = jnp.stack([_dot(m_t[2 * k], dyh[0][k]) + _dot(m_t[2 * k + 1], dyh[1][k]) for k in range(NPAIR)])
        dm_t = jnp.stack([_dot_nt(xdtb[h // 2], dyh[h % 2][h // 2]) for h in range(SH)])
        dg_t = jnp.sum((dm_t * lam_t).reshape(SG, SR, BLK, BLK), axis=1).astype(BF16)
        xq16 = xdtb.astype(F32)
        xh = [jnp.where(cm.lo, xdt16, 0.0).astype(BF16), jnp.where(cm.lo, 0.0, xdt16).astype(BF16)]
        y_in = jnp.stack([_dot_tn(m_t[2 * k], xh[0][k]) + _dot_tn(m_t[2 * k + 1], xh[1][k]) for k in range(NPAIR)])
        da_diag = dy16 * y_in - xq16 * dxdt
        lane_c = lax.broadcasted_iota(jnp.int32, (BLK, 128), 1)
        ds16b = ds16.astype(BF16)
        sp16b = sp16.astype(BF16)
        dxs = jnp.stack([_dot(bbs[grp(k)], ds16b[k]) for k in range(NPAIR)]) * cm.epl
        dxdt = dxdt + dxs
        dya = (dy16 * cm.eapl).astype(BF16)
        xe = (xdt16 * cm.epl).astype(BF16)
        dcs, dbs = [], []
        for g in range(SG):
            ks = range(g * (NPAIR // SG), (g + 1) * (NPAIR // SG))
            dcs.append(sum(_dot_nt(dya[k], sp16b[k]) for k in ks) + _dot_tn(dg_t[g], bbs[g]))
            dbs.append(sum(_dot_nt(xe[k], ds16b[k]) for k in ks) + _dot(dg_t[g], cbs[g]))
        dst[...] = _unpairs(cm.cdpl * ds16 + jnp.stack([_dot(cts[grp(k)], dya[k]) for k in range(NPAIR)]))
        da_ref[...] = jnp.concatenate([_unpairs(dxdt * cm.dpl + cm.dskpl * dy16)] + dbs + dcs, axis=1)
        y_off = jnp.stack([_dot(cbs[grp(k)], sp16b[k]) for k in range(NPAIR)]) * cm.eapl
        da_cols = _head_sums(da_diag + dy16 * y_off - xdt16 * dxs)
        last = _head_sums(jnp.sum(xdt16 * dxs, axis=1, keepdims=True)
                          + cm.cdpl * jnp.sum(ds16 * sp16, axis=1, keepdims=True))
        ddt = _head_sums(dxdt * xs16)
        row_i = lax.broadcasted_iota(jnp.int32, (BLK, 128), 0)
        dacum = da_cols + jnp.where(row_i == BLK - 1, last, 0.0)
        dda = _tri_mm((ri <= ci).astype(BF16), dacum)
        ddt = ddt + dda * cm.a
        dal_ref[...] += _csum(dda * cm.dtv) * cm.a
        ddt_raw = jnp.where(lane_c < SH, ddt * _sig(dt_raw + dtb), 0.0)
        ddt_ref[...] = ddt_raw.astype(BF16)
        ddtb_ref[...] += _csum(ddt_raw)
        ddk_ref[...] += _head_sums(jnp.sum(dy16 * xs16, axis=1, keepdims=True))

    rev = lambda i: nc - 1 - i
    vec = _full((1, 128))
    slab = pl.BlockSpec((BLK, SSM_W), lambda i: (rev(i), 0))
    return pl.pallas_call(
        body, name="ssd_bwd", grid=(nc,),
        in_specs=[slab,
                  pl.BlockSpec((BLK, SG * SN), lambda i: (rev(i), SSM_W // (SG * SN))),
                  pl.BlockSpec((BLK, SG * SN), lambda i: (rev(i), SSM_W // (SG * SN) + 1)),
                  pl.BlockSpec((BLK, 128), lambda i: (rev(i), 0)),
                  slab,
                  pl.BlockSpec((1, SN, SSM_W), lambda i: (rev(i), 0, 0)), vec, vec, vec],
        out_specs=[pl.BlockSpec((BLK, XBC), lambda i: (rev(i), 0)), pl.BlockSpec((BLK, 128), lambda i: (rev(i), 0)),
                   vec, vec, vec],
        out_shape=[jax.ShapeDtypeStruct((t, XBC), F32), jax.ShapeDtypeStruct((t, 128), BF16),
                   jax.ShapeDtypeStruct((1, 128), F32), jax.ShapeDtypeStruct((1, 128), F32),
                   jax.ShapeDtypeStruct((1, 128), F32)],
        scratch_shapes=[pltpu.VMEM((SN, SSM_W), F32)],
        compiler_params=_cp(("arbitrary",)),
    )(act, act, act, dt_raw, dy, sprev, dtb_p, alog_p, dsk_p)


TAIL_TM = 256


def _dsilu(z, s):
    return s * (1.0 + z * (1.0 - s))


def tail(proj, ao, yss, x, target, gate, ssm_nw, w_at, w_ss, w_ou):
    t = x.shape[0]
    tm = min(t, TAIL_TM)
    gw = SSM_W // SG

    def body(ao_ref, za_ref, ga_ref, gb_ref, zm_ref, ys_ref, x_ref, tg_ref, gt_ref, nw_ref, wa_ref, ws_ref, wo_ref,
             loss_ref, dy_ref, dao_ref, dmid_ref, dys_ref,
             ua_ref, yn_ref, mg_ref, dya_ref, dyb_ref, do_ref, dgt_ref, dnw_ref):
        i = pl.program_id(0)

        @pl.when(i == 0)
        def _():
            loss_ref[...] = jnp.zeros_like(loss_ref)
            dgt_ref[...] = jnp.zeros_like(dgt_ref)
            dnw_ref[...] = jnp.zeros_like(dnw_ref)

        ao = ao_ref[...].astype(F32)
        za = za_ref[...].astype(F32)
        sa = _sig(za)
        sila = za * sa
        ua_f = ao * sila
        ua = ua_f.astype(BF16)
        ya = _dot(ua, wa_ref[...])
        zm = zm_ref[...].astype(F32)
        sm = _sig(zm)
        silm = zm * sm
        ys = ys_ref[...].astype(F32)
        u = ys * silm
        nw = nw_ref[...]
        rs, uns = [], []
        for g in range(SG):
            ug = u[:, gw * g:gw * (g + 1)]
            r = lax.rsqrt(jnp.mean(ug * ug, axis=-1, keepdims=True) + EPS)
            rs.append(r)
            uns.append(ug * r)
        un = jnp.concatenate(uns, axis=1)
        yn_f = un * nw
        yn = yn_f.astype(BF16)
        yb = _dot(yn, ws_ref[...])
        sga = _sig(ga_ref[...].astype(F32))
        sgb = _sig(gb_ref[...].astype(F32))
        mg_f = sga * ya + sgb * yb
        mg = mg_f.astype(BF16)
        o = _dot(mg, wo_ref[...])
        gt = gt_ref[...]
        err = (x_ref[...] + gt * o) - tg_ref[...]
        lane = lax.broadcasted_iota(jnp.int32, (1, 128), 1)
        loss_ref[...] += jnp.where(lane == 0, 0.5 * _asum(_rsum(err * err) / D), 0.0)
        dy = err * (1.0 / D)
        dy_ref[...] = dy
        dgt_ref[...] += _csum(dy * o)
        do = (dy * gt).astype(BF16)
        dmg = _dot_nt(do, wo_ref[...])
        dmid_ref[:, C_GA - C_ZA:C_GB - C_ZA] = (dmg * ya * sga * (1.0 - sga)).astype(BF16)
        dmid_ref[:, C_GB - C_ZA:C_ZM - C_ZA] = (dmg * yb * sgb * (1.0 - sgb)).astype(BF16)
        dya = (dmg * sga).astype(BF16)
        dyb = (dmg * sgb).astype(BF16)
        dua = _dot_nt(dya, wa_ref[...])
        dao_ref[...] = (dua * sila).astype(BF16)
        dmid_ref[:, 0:C_GA - C_ZA] = (dua * ao * _dsilu(za, sa)).astype(BF16)
        dyn = _dot_nt(dyb, ws_ref[...])
        dnw_ref[...] += _csum(dyn * un)
        dun = dyn * nw
        dus = []
        for g in range(SG):
            gs = slice(gw * g, gw * (g + 1))
            dus.append(rs[g] * (dun[:, gs] - uns[g] * jnp.mean(dun[:, gs] * uns[g], axis=-1, keepdims=True)))
        du = jnp.concatenate(dus, axis=1)
        dys_ref[...] = (du * silm).astype(BF16)
        dmid_ref[:, C_ZM - C_ZA:] = (du * ys * _dsilu(zm, sm)).astype(BF16)
        ua_ref[...] = ua_f.T.astype(BF16)
        yn_ref[...] = yn_f.T.astype(BF16)
        mg_ref[...] = mg_f.T.astype(BF16)
        dya_ref[...] = dya
        dyb_ref[...] = dyb
        do_ref[...] = do

    row = lambda w: pl.BlockSpec((tm, w), lambda i: (i, 0))
    pcol = lambda w, c0: pl.BlockSpec((tm, w), lambda i: (i, c0 // w))
    sd = lambda w, dt: jax.ShapeDtypeStruct((t, w), dt)
    colt = lambda w: pl.BlockSpec((w, tm), lambda i: (0, i))
    sdt = lambda w: jax.ShapeDtypeStruct((w, t), BF16)
    return pl.pallas_call(
        body, name="tail", grid=(t // tm,),
        in_specs=[row(D), pcol(D, C_ZA), pcol(D, C_GA), pcol(D, C_GB), pcol(SSM_W, C_ZM), row(SSM_W), row(D), row(D),
                  _full((1, D)), _full((1, SSM_W)), _full((D, D)), _full((SSM_W, D)), _full((D, D))],
        out_specs=[_full((1, 128)), row(D), row(D), row(W_MID), row(SSM_W),
                   colt(D), colt(SSM_W), colt(D), row(D), row(D), row(D), _full((1, D)), _full((1, SSM_W))],
        out_shape=[jax.ShapeDtypeStruct((1, 128), F32), sd(D, F32), sd(D, BF16), sd(W_MID, BF16),
                   sd(SSM_W, BF16), sdt(D), sdt(SSM_W), sdt(D), sd(D, BF16),
                   sd(D, BF16), sd(D, BF16), jax.ShapeDtypeStruct((1, D), F32), jax.ShapeDtypeStruct((1, SSM_W), F32)],
        compiler_params=_cp(("arbitrary",)),
    )(ao, proj, proj, proj, proj, yss, x, target, gate, ssm_nw, w_at, w_ss, w_ou)


DPIECES = ((D, ((D, C_Q),)),
           (W_MID, ((D, C_ZA), (D, C_GA), (D, C_GB), (SSM_W, C_ZM))),
           (XBC, ((XBC, C_XBC),)),
           (512, ((512, C_K),)),
           (128, ((128, C_DT),)))


def dproj_bwd(pieces, wcat, x, dy, norm_w, scale):
    t = x.shape[0]
    tm = min(t, 256)
    nt = t // tm
    wblocks = [blk for _, subs in DPIECES for blk in subs]
    npc, nwb = len(DPIECES), len(wblocks)

    def body(*refs):
        p_refs, w_refs = refs[:npc], refs[npc:npc + nwb]
        x_ref, dy_ref, nw_ref, sc_ref, gx_ref, dnw_ref, dsc_ref, dsh_ref, dwe_ref = refs[npc + nwb:]
        i = pl.program_id(0)

        @pl.when(i == 0)
        def _():
            for ref in (dwe_ref, dsh_ref, dnw_ref, dsc_ref):
                ref[...] = jnp.zeros_like(ref)

        dh, wi = None, 0
        for p_ref, (_, subs) in zip(p_refs, DPIECES):
            loc = 0
            for w, _ in subs:
                part = _dot_nt(p_ref[:, loc:loc + w], w_refs[wi][...])
                dh = part if dh is None else dh + part
                loc += w
                wi += 1
        xv = x_ref[...]
        r = lax.rsqrt(jnp.mean(xv * xv, axis=-1, keepdims=True) + EPS)
        xn = xv * r
        weff = nw_ref[...] * (1.0 + sc_ref[...])
        dxn = dh * weff
        gx_ref[...] = dy_ref[...] + r * (dxn - xn * jnp.mean(dxn * xn, axis=-1, keepdims=True))
        dwe_ref[...] += _csum(dh * xn)
        dsh_ref[...] += _csum(dh)

        @pl.when(i == nt - 1)
        def _():
            dwe = dwe_ref[...]
            dnw_ref[...] = dwe * (1.0 + sc_ref[...])
            dsc_ref[...] = dwe * nw_ref[...]

    vec = pl.BlockSpec((1, D), lambda i: (0, 0))
    row = pl.BlockSpec((tm, D), lambda i: (i, 0))
    return pl.pallas_call(
        body, name="dproj_bwd", grid=(nt,),
        in_specs=[pl.BlockSpec((tm, pw), lambda i: (i, 0)) for pw, _ in DPIECES]
        + [pl.BlockSpec((D, w), functools.partial(lambda i, b: (0, b), b=off // w), pipeline_mode=pl.Buffered(1))
           for w, off in wblocks]
        + [row, row, vec, vec],
        out_specs=[row, vec, vec, vec],
        out_shape=[jax.ShapeDtypeStruct((t, D), F32), jax.ShapeDtypeStruct((1, D), F32),
                   jax.ShapeDtypeStruct((1, D), F32), jax.ShapeDtypeStruct((1, D), F32)],
        scratch_shapes=[pltpu.VMEM((1, D), F32)],
        compiler_params=_cp(("arbitrary",)),
    )(*pieces, *([wcat] * nwb), x, dy, norm_w, scale)


def wgrad(at, b, name, bn, after):
    m, t = at.shape
    n = b.shape[1]
    tk = min(t, 1024)
    bm = min(m, 1024)

    def body(a_ref, b_ref, after_ref, o_ref):
        part = _dot(a_ref[...], b_ref[...])

        @pl.when(pl.program_id(2) == 0)
        def _():
            o_ref[...] = part

        @pl.when(pl.program_id(2) > 0)
        def _():
            o_ref[...] += part

    return pl.pallas_call(
        body, name=name, grid=(m // bm, n // bn, t // tk),
        in_specs=[pl.BlockSpec((bm, tk), lambda i, j, k: (i, k)), pl.BlockSpec((tk, bn), lambda i, j, k: (k, j)), ANY],
        out_specs=pl.BlockSpec((bm, bn), lambda i, j, k: (i, j)),
        out_shape=jax.ShapeDtypeStruct((m, n), F32),
        compiler_params=_cp(("parallel", "parallel", "arbitrary")),
    )(at, b, after)


SUM_TR = 256


def pair_sum(g, core, theirs, name):
    w = g.shape[2]
    nh = HROWS // SUM_TR

    def body(core_ref, a_ref, b_ref, o_ref, ob_ref):
        s = a_ref[...] + b_ref[...]
        o_ref[...] = s
        ob_ref[...] = s.astype(BF16)

    spec = pl.BlockSpec((1, SUM_TR, w), lambda d, i, c: (d, i, 0))
    return pl.pallas_call(
        body, name=name,
        out_shape=[jax.ShapeDtypeStruct((4, HROWS, w), F32), jax.ShapeDtypeStruct((4, HROWS, w), BF16)],
        grid_spec=pltpu.PrefetchScalarGridSpec(
            num_scalar_prefetch=1, grid=(4, nh),
            in_specs=[pl.BlockSpec((1, SUM_TR, w), lambda d, i, c: (d, c[0] * nh + i, 0)), spec],
            out_specs=[spec, spec]),
        compiler_params=_cp(("parallel", "parallel")))(core.reshape(1).astype(jnp.int32), g, theirs)


def chip_sum(part, chip, others, name):
    r, w = part.shape[1:]

    def body(chip_ref, a_ref, b_ref, o_ref):
        acc = a_ref[0]
        for k in range(3):
            acc = acc + b_ref[k].astype(F32)
        o_ref[...] = acc

    return pl.pallas_call(
        body, name=name, out_shape=jax.ShapeDtypeStruct((r, w), F32),
        grid_spec=pltpu.PrefetchScalarGridSpec(
            num_scalar_prefetch=1, grid=(r // SUM_TR,),
            in_specs=[pl.BlockSpec((1, SUM_TR, w), lambda i, c: (c[0], i, 0)),
                      pl.BlockSpec((3, SUM_TR, w), lambda i, c: (0, i, 0))],
            out_specs=pl.BlockSpec((SUM_TR, w), lambda i, c: (i, 0))),
        compiler_params=_cp(("parallel",)))(chip.reshape(1).astype(jnp.int32), part, others)


def sum_devices(g):
    r = g.shape[1]

    def body(g_ref, o_ref):
        acc = g_ref[0]
        for d in range(1, 8):
            acc = acc + g_ref[d]
        o_ref[...] = acc

    return pl.pallas_call(body, name="sum_devices", out_shape=jax.ShapeDtypeStruct((r, 1024), F32),
                          compiler_params=_cp())(g)


def adamw(w, g, m, v, name):
    r, c = w.shape
    tr = r
    for cand in (256, 128, 64, 32, 16, 8):
        if r % cand == 0 and r > cand:
            tr = cand
            break

    def body(w_ref, g_ref, m_ref, v_ref, d_ref, nm_ref, nv_ref):
        gv = g_ref[...]
        mn = ADAM_B1 * m_ref[...] + (1.0 - ADAM_B1) * gv
        vn = ADAM_B2 * v_ref[...] + (1.0 - ADAM_B2) * (gv * gv)
        m_hat = mn / (1.0 - ADAM_B1 ** ADAM_STEP)
        v_hat = vn / (1.0 - ADAM_B2 ** ADAM_STEP)
        d_ref[...] = -ADAM_LR * (m_hat / (jnp.sqrt(v_hat) + ADAM_EPS) + ADAM_WD * w_ref[...])
        nm_ref[...] = mn
        nv_ref[...] = vn

    spec = pl.BlockSpec((tr, c), lambda i: (i, 0))
    sd = jax.ShapeDtypeStruct((r, c), F32)
    return pl.pallas_call(body, name=name, grid=(r // tr,), in_specs=[spec] * 4, out_specs=[spec] * 3,
                          out_shape=[sd, sd, sd], compiler_params=_cp(("parallel",)))(w, g, m, v)


def adamw_halves(w, mine, theirs, core, m, v, name):
    r, c = w.shape
    tr = 128
    nh = HROWS // tr

    def body(core_ref, w_ref, a_ref, b_ref, m_ref, v_ref, g_ref, d_ref, nm_ref, nv_ref):
        gv = jnp.where(pl.program_id(0) // nh == core_ref[0], a_ref[...], b_ref[...])
        mn = ADAM_B1 * m_ref[...] + (1.0 - ADAM_B1) * gv
        vn = ADAM_B2 * v_ref[...] + (1.0 - ADAM_B2) * (gv * gv)
        m_hat = mn / (1.0 - ADAM_B1 ** ADAM_STEP)
        v_hat = vn / (1.0 - ADAM_B2 ** ADAM_STEP)
        g_ref[...] = gv
        d_ref[...] = -ADAM_LR * (m_hat / (jnp.sqrt(v_hat) + ADAM_EPS) + ADAM_WD * w_ref[...])
        nm_ref[...] = mn
        nv_ref[...] = vn

    spec = pl.BlockSpec((tr, c), lambda i, s: (i, 0))
    half = pl.BlockSpec((tr, c), lambda i, s: (i % nh, 0))
    sd = jax.ShapeDtypeStruct((r, c), F32)
    return pl.pallas_call(
        body, name=name, out_shape=[sd, sd, sd, sd],
        grid_spec=pltpu.PrefetchScalarGridSpec(num_scalar_prefetch=1, grid=(r // tr,),
                                               in_specs=[spec, half, half, spec, spec], out_specs=[spec] * 4),
        compiler_params=_cp(("parallel",)))(core.reshape(1).astype(jnp.int32), w, mine, theirs, m, v)


ANY = pl.BlockSpec(memory_space=pl.ANY)
VM = pl.BlockSpec(memory_space=pltpu.VMEM)
OTHER_CHIPS = ((1, 0), (0, 1), (1, 1))


def _pos():
    return lax.axis_index("x"), lax.axis_index("y"), lax.axis_index("c")


def _flip(v, bit):
    return 1 - v if bit else v


def _rcopy(src, dst, ssem, rsem, peer):
    return pltpu.make_async_remote_copy(src_ref=src, dst_ref=dst, send_sem=ssem, recv_sem=rsem,
                                        device_id=peer, device_id_type=MESH)


def allgather_small(p, name):
    r = p.shape[0]

    def body(in_ref, out_ref, ssem, rsem, lsem):
        x, y, c = _pos()
        me = 4 * x + 2 * y + c
        loc = pltpu.make_async_copy(in_ref, out_ref.at[me], lsem)
        loc.start()
        sends = []
        peers = []
        for k in range(1, 8):
            px, py, pc = _flip(x, (k >> 2) & 1), _flip(y, (k >> 1) & 1), _flip(c, k & 1)
            peers.append((px, py, pc))
            cp = _rcopy(in_ref, out_ref.at[me], ssem.at[k - 1], rsem.at[k - 1], (px, py, pc))
            cp.start()
            sends.append(cp)
        for k in range(1, 8):
            px, py, pc = peers[k - 1]
            _rcopy(in_ref, out_ref.at[4 * px + 2 * py + pc], ssem.at[k - 1], rsem.at[k - 1], (px, py, pc)).wait_recv()
        for cp in sends:
            cp.wait_send()
        loc.wait()

    return pl.pallas_call(
        body, name=name, out_shape=jax.ShapeDtypeStruct((8, r, 1024), F32),
        in_specs=[VM], out_specs=VM,
        scratch_shapes=[pltpu.SemaphoreType.DMA((7,)), pltpu.SemaphoreType.DMA((7,)), pltpu.SemaphoreType.DMA],
    )(p)


def gather_weights(w_in_b, mod_sh):
    def body(wi_ref, m_ref, gi_ref, mo_ref, ssem, rsem, lsem):
        x, y, c = _pos()
        chip = 2 * x + y
        mine = pl.ds(pl.multiple_of(c * HROWS, 16), HROWS)
        other = pl.ds(pl.multiple_of((1 - c) * HROWS, 16), HROWS)
        sib = (x, y, 1 - c)
        pairs = ((wi_ref, gi_ref),)
        loc_m = pltpu.make_async_copy(m_ref, mo_ref.at[chip], lsem)
        loc_m.start()
        sends = []
        for k, (fx, fy) in enumerate(OTHER_CHIPS):
            peer = (_flip(x, fx), _flip(y, fy), c)
            for a, (w_ref, g_ref) in enumerate(pairs):
                cw = _rcopy(w_ref.at[mine], g_ref.at[chip, mine], ssem.at[6 * a + k], rsem.at[6 * a + k], peer)
                cw.start()
                sends.append(cw)
            cm = _rcopy(m_ref, mo_ref.at[chip], ssem.at[12 + k], rsem.at[12 + k], peer)
            cm.start()
            sends.append(cm)
        for k, (fx, fy) in enumerate(OTHER_CHIPS):
            px, py = _flip(x, fx), _flip(y, fy)
            for a, (w_ref, g_ref) in enumerate(pairs):
                got = g_ref.at[2 * px + py, mine]
                _rcopy(w_ref.at[mine], got, ssem.at[6 * a + k], rsem.at[6 * a + k], (px, py, c)).wait_recv()
                fw = _rcopy(got, got, ssem.at[6 * a + 3 + k], rsem.at[6 * a + 3 + k], sib)
                fw.start()
                sends.append(fw)
        for k, (fx, fy) in enumerate(OTHER_CHIPS):
            px, py = _flip(x, fx), _flip(y, fy)
            for a, (w_ref, g_ref) in enumerate(pairs):
                land = g_ref.at[2 * px + py, other]
                _rcopy(land, land, ssem.at[6 * a + 3 + k], rsem.at[6 * a + 3 + k], sib).wait_recv()
            _rcopy(m_ref, mo_ref.at[2 * px + py], ssem.at[12 + k], rsem.at[12 + k], (px, py, c)).wait_recv()
        for cp in sends:
            cp.wait_send()
        loc_m.wait()

    return pl.pallas_call(
        body, name="gather_weights",
        out_shape=[jax.ShapeDtypeStruct((4, D, SH_IN), BF16), jax.ShapeDtypeStruct((4, 8, 768), F32)],
        in_specs=[ANY, VM], out_specs=[ANY, VM],
        scratch_shapes=[pltpu.SemaphoreType.DMA((15,)), pltpu.SemaphoreType.DMA((15,)), pltpu.SemaphoreType.DMA],
    )(w_in_b, mod_sh)


def pair_exchange(g):
    def body(g_ref, r_ref, ssem, rsem):
        x, y, c = _pos()
        other = pl.ds(pl.multiple_of((1 - c) * HROWS, 8), HROWS)
        cp = _rcopy(g_ref.at[:, other, :], r_ref, ssem, rsem, (x, y, 1 - c))
        cp.start()
        cp.wait()

    return pl.pallas_call(
        body, name="pair_exchange", out_shape=jax.ShapeDtypeStruct((4, HROWS, g.shape[2]), F32),
        in_specs=[ANY], out_specs=ANY,
        scratch_shapes=[pltpu.SemaphoreType.DMA, pltpu.SemaphoreType.DMA],
    )(g)


HBM = pl.BlockSpec(memory_space=pltpu.HBM)
SEM = pl.BlockSpec(memory_space=pltpu.SEMAPHORE)
DATAFLOW = pltpu.SideEffectType.DATAFLOW_SIDE_EFFECTING


def split_start(name, make_copies, srcs, lands, nsem, after):
    arrays = [*srcs, *lands]
    n, ns = len(arrays), len(srcs)

    def body(*refs):
        for cp in make_copies(refs[:ns], refs[ns:n], refs[n + 1], refs[n + 2])[0]:
            cp.start()
        refs[-1][...] = jnp.zeros_like(refs[-1])

    res = pl.pallas_call(
        body, name=name,
        out_shape=(pltpu.SemaphoreType.DMA((nsem,)), pltpu.SemaphoreType.DMA((nsem,)),
                   *[pltpu.HBM(a.shape, a.dtype) for a in arrays], jax.ShapeDtypeStruct((8, 128), F32)),
        in_specs=(HBM,) * n + (ANY,), out_specs=(SEM, SEM) + (HBM,) * n + (VM,),
        input_output_aliases={i: 2 + i for i in range(n)},
        compiler_params=pltpu.CompilerParams(has_side_effects=DATAFLOW),
    )(*[pltpu.with_memory_space_constraint(a, pltpu.HBM) for a in arrays], after)
    return res[0], res[1], list(res[2:2 + n]), res[-1]


def split_wait(name, make_copies, ssem, rsem, arrays, ns, after):
    n = len(arrays)

    def body(*refs):
        sends, recvs = make_copies(refs[:ns], refs[ns:n], refs[n], refs[n + 1])
        for cp in sends:
            cp.wait_send()
        for cp in recvs:
            cp.wait_recv()

    return pl.pallas_call(
        body, name=name, out_shape=tuple(pltpu.HBM(a.shape, a.dtype) for a in arrays),
        in_specs=(HBM,) * n + (SEM, SEM, ANY), out_specs=(HBM,) * n,
        input_output_aliases={i: i for i in range(n)},
        compiler_params=pltpu.CompilerParams(has_side_effects=DATAFLOW),
    )(*arrays, ssem, rsem, after)


def _chip_copies(srcs, lands, ssem, rsem):
    x, y, c = _pos()
    copies = []
    for k, (fx, fy) in enumerate(OTHER_CHIPS):
        px, py = _flip(x, fx), _flip(y, fy)
        for a, (p_ref, l_ref) in enumerate(zip(srcs, lands)):
            copies.append(_rcopy(p_ref.at[2 * px + py], l_ref.at[k], ssem.at[3 * a + k], rsem.at[3 * a + k], (px, py, c)))
    return copies, copies


def _pair_copies(srcs, lands, ssem, rsem):
    x, y, c = _pos()
    other = pl.ds(pl.multiple_of((1 - c) * HROWS, 8), HROWS)
    copies = [_rcopy(srcs[0].at[:, other, :], lands[0], ssem.at[0], rsem.at[0], (x, y, 1 - c))]
    return copies, copies


def _rest_copies(srcs, lands, ssem, rsem):
    x, y, c = _pos()
    chip = 2 * x + y
    mine = pl.ds(pl.multiple_of(c * HROWS, 16), HROWS)
    sends, recvs = [], []
    for k, (fx, fy) in enumerate(OTHER_CHIPS):
        px, py = _flip(x, fx), _flip(y, fy)
        for t in range(2):
            rows_t = pl.ds(t * HROWS, HROWS)
            sends.append(_rcopy(srcs[0].at[mine], lands[0].at[chip, mine], ssem.at[2 * k + t], rsem.at[2 * k + c],
                                (px, py, t)))
            recvs.append(_rcopy(srcs[0].at[rows_t], lands[0].at[2 * px + py, rows_t], ssem.at[2 * k + t],
                                rsem.at[2 * k + t], (px, py, t)))
    return sends, recvs


def _swap_copies(srcs, lands, ssem, rsem):
    x, y, c = _pos()
    copies = [_rcopy(s_ref, l_ref, ssem.at[a], rsem.at[a], (x, y, 1 - c))
              for a, (s_ref, l_ref) in enumerate(zip(srcs, lands))]
    return copies, copies


def _flat(v, width=1024):
    v = v.reshape(-1)
    n = -(-v.shape[0] // width) * width
    return jnp.pad(v, (0, n - v.shape[0]))


def _rows(parts, rows):
    flat = jnp.concatenate(parts)
    return jnp.pad(flat, (0, rows * 1024 - flat.shape[0])).reshape(rows, 1024)


def _pack_small(b_ada, norm_w, conv_b, ssm_norm_w, q_norm_w, k_norm_w, sinks, dt_bias, a_log, d_skip, rel_bias,
                extra=None, tail=(), rows=16):
    misc = [q_norm_w, k_norm_w, sinks, dt_bias, a_log, d_skip] + ([] if extra is None else [extra])
    parts = [_flat(b_ada), _flat(norm_w), _flat(conv_b), _flat(ssm_norm_w)] + [_flat(v, 128) for v in misc]
    parts.append(jnp.zeros(((8 - len(misc)) * 128,), F32))
    parts.append(_flat(rel_bias))
    parts.append(jnp.zeros((5 * 1024,), F32))
    return _rows(parts + [_flat(v) for v in tail], rows)


def _unpack_small(p):
    misc = p[9]
    return dict(b_ada=p[0:3].reshape(1, 3072), norm_w=p[3:4], conv_b=p[4:7].reshape(1, 3072),
                ssm_norm_w=p[7:9].reshape(1, 2048), q_norm_w=misc[None, 0:64], k_norm_w=misc[None, 128:192],
                sinks=misc[None, 256:272], dt_bias=misc[None, 384:416], a_log=misc[None, 512:544],
                d_skip=misc[None, 640:672], rel_bias=p[10, :512].reshape(32, 16), extra=misc[768])


SMALL = ("b_ada", "norm_w", "conv_b", "ssm_norm_w", "q_norm_w", "k_norm_w", "sinks", "dt_bias", "a_log", "d_skip",
         "rel_bias")
WEIGHTS = ("w_ada", "b_ada", "norm_w", "w_in", "q_norm_w", "k_norm_w", "rel_bias", "sinks", "conv_w", "conv_b",
           "dt_bias", "a_log", "d_skip", "ssm_norm_w", "w_attn_proj", "w_ssm_proj", "w_out")
IN_COLS = ((0, 1024, C_Q), (1024, 256, C_K), (1280, 256, C_V), (1536, 1024, C_ZA), (2560, 2048, C_ZM),
           (4608, 3072, C_XBC), (7680, 32, C_DT), (7712, 1024, C_GA), (8736, 1024, C_GB))


def _to_cat(shards):
    parts, pos = [], 0
    for o, n, cnew in sorted(IN_COLS, key=lambda e: e[2]):
        assert cnew == pos
        c0 = o
        while c0 < o + n:
            i = c0 // SH_IN
            c1 = min(o + n, (i + 1) * SH_IN)
            parts.append(shards[i][:, c0 - i * SH_IN:c1 - i * SH_IN])
            c0 = c1
        pos += n
    parts.append(jnp.zeros((D, NP - pos), shards.dtype))
    return jnp.concatenate(parts, axis=1)


def _from_cat(dw_pieces):
    starts = [subs[0][1] for _, subs in DPIECES]

    def cols(c0, c1):
        p = max(q for q in range(len(starts)) if starts[q] <= c0)
        return dw_pieces[p][:, c0 - starts[p]:c1 - starts[p]]

    shards = []
    for i in range(4):
        lo, hi = i * SH_IN, (i + 1) * SH_IN
        parts = []
        for o, n, cnew in IN_COLS:
            a, b = max(o, lo), min(o + n, hi)
            if a < b:
                parts.append(cols(cnew + a - o, cnew + b - o))
        shards.append(jnp.concatenate(parts, axis=1))
    return jnp.stack(shards)


def kernel(x, c, w_ada, b_ada, norm_w, w_in, q_norm_w, k_norm_w, rel_bias, sinks, conv_w, conv_b, dt_bias, a_log, d_skip, ssm_norm_w, w_attn_proj, w_ssm_proj, w_out, loss_target, m_w_ada, m_b_ada, m_norm_w, m_w_in, m_q_norm_w, m_k_norm_w, m_rel_bias, m_sinks, m_conv_w, m_conv_b, m_dt_bias, m_a_log, m_d_skip, m_ssm_norm_w, m_w_attn_proj, m_w_ssm_proj, m_w_out, v_w_ada, v_b_ada, v_norm_w, v_w_in, v_q_norm_w, v_k_norm_w, v_rel_bias, v_sinks, v_conv_w, v_conv_b, v_dt_bias, v_a_log, v_d_skip, v_ssm_norm_w, v_w_attn_proj, v_w_ssm_proj, v_w_out):
    args = dict(locals())
    xi, yi, ci = lax.axis_index("x"), lax.axis_index("y"), lax.axis_index("c")
    chip = 2 * xi + yi
    me = 4 * xi + 2 * yi + ci
    x2 = x[0]
    tgt = loss_target[0]

    pay = _rows([c.reshape(-1), conv_w[0].reshape(-1)], 8)
    g0 = allgather_small(pay, "gather_cond")
    c_all = g0[:, 0, :]
    conv_w_full = g0[0::2, 1:4, :].reshape(4, CONV_K, 768).transpose(1, 0, 2).reshape(CONV_K, XBC)

    b_ada_sh = lax.dynamic_slice(b_ada, (0, chip * 768), (1, 768))
    mod_sh = ada_mod(c_all, w_ada[0], b_ada_sh)

    w_in_b = w_in[0].astype(BF16)
    w_rest_b = jnp.concatenate([w_attn_proj[0], w_ssm_proj[0], w_out[0]], axis=0).astype(BF16)
    wg_in, modg = gather_weights(w_in_b, mod_sh)
    wg_in = lax.dynamic_update_slice(wg_in, w_in_b[None], (chip, 0, 0))
    rs_sem, rr_sem, rest_thru, rest_tok = split_start("gather_rest_start", _rest_copies, [w_rest_b],
                                                      [lax.empty((4, D, D), BF16)], 6, modg)
    mod = lax.dynamic_slice(modg, (0, me, 0), (4, 1, 768)).reshape(1, 3 * D)
    shift, scale, gate = mod[:, :D], mod[:, D:2 * D] + rest_tok[:1, :1], mod[:, 2 * D:]
    wcat = _to_cat(wg_in)

    pad128 = lambda v: jnp.pad(v, ((0, 0), (0, 128 - v.shape[1])))
    dtb_p, alog_p, dsk_p = pad128(dt_bias), pad128(a_log), pad128(d_skip)
    bucket = _bucket_table()

    proj, dt_raw, h_t = norm_proj(x2, norm_w, scale, shift, wcat)
    biasm = bias_expand(rel_bias, sinks, bucket)
    ao = attn_fwd(proj, biasm, q_norm_w, k_norm_w)
    act, dsl = conv_fwd(proj, conv_w_full, conv_b)
    yss, sprev = ssd_fwd(act, dt_raw, dtb_p, alog_p, dsk_p)

    w_rest_b, wg_rest = split_wait("gather_rest_wait", _rest_copies, rs_sem, rr_sem, rest_thru, 1, yss)
    wg_rest = lax.dynamic_update_slice(wg_rest, w_rest_b[None], (chip, 0, 0))
    w_at = wg_rest[:, :R_AT].reshape(D, D)
    w_ss = wg_rest[:, R_AT:R_AT + R_SS].reshape(SSM_W, D)
    w_ou = wg_rest[:, R_AT + R_SS:].reshape(D, D)
    (loss_p, dy, dao, dmid, dyss, ua_t, yn_t, mg_t, dya, dyb, dout, dgate, dssm_nw) = tail(
        proj, ao, yss, x2, tgt, gate, ssm_norm_w, w_at, w_ss, w_ou)

    dq, dkv, dqw, dkw, dacc = attn_bwd(proj, dao, biasm, q_norm_w, k_norm_w)
    dbias = bias_reduce(dacc, bucket)
    drb = dbias[:, :NBUCKET].T
    dsk = dbias[:, NBUCKET].reshape(1, HQ)
    dact, ddt, ddtb, dalog, ddskip = ssd_bwd(act, dt_raw, dyss, sprev, dtb_p, alog_p, dsk_p)
    dxbc, dconv_w, dconv_b = conv_bwd(proj, dact, dsl, conv_w_full)

    dproj = (dq, dmid, dxbc, dkv, ddt)
    dwcat = [wgrad(h_t, piece, "dw_in_%d" % p, min(piece.shape[1], 1024), rest_tok) for p, piece in enumerate(dproj)]

    g_in = _from_cat(dwcat)
    ps_sem, pr_sem, pair_thru, pair_tok = split_start("pair_in_start", _pair_copies, [g_in],
                                                      [lax.empty((4, HROWS, SH_IN), F32)], 1, loss_p)
    dw_at = wgrad(ua_t, dya, "dw_attn", 512, pair_tok)
    dw_ss = wgrad(yn_t, dyb, "dw_ssm", 512, pair_tok)
    dw_ou = wgrad(mg_t, dout, "dw_out", 512, pair_tok)
    g_rest = jnp.concatenate([dw_at.reshape(4, R_AT, D), dw_ss.reshape(4, R_SS, D), dw_ou.reshape(4, R_OU, D)], axis=1)
    sib_rest = pair_exchange(g_rest)
    g_in, sib_in = split_wait("pair_in_wait", _pair_copies, ps_sem, pr_sem, pair_thru, 1, sib_rest)
    part_in, pb_in = pair_sum(g_in, ci, sib_in, "pair_sum_in")
    part_rest, pb_rest = pair_sum(g_rest, ci, sib_rest, "pair_sum_rest")
    cs_sem, cr_sem, chip_thru, token = split_start(
        "chip_exchange_start", _chip_copies, [pb_in, pb_rest],
        [lax.empty((3, HROWS, SH_IN), BF16), lax.empty((3, HROWS, D), BF16)], 6, part_rest)
    grad_x, dnorm_w, dscale, dshift = dproj_bwd(dproj, wcat, x2, dy, norm_w, scale + token[:1, :1])
    _, _, oth_in, oth_rest = split_wait("chip_exchange_wait", _chip_copies, cs_sem, cr_sem, chip_thru, 2, dshift)
    red_in = chip_sum(part_in, chip, oth_in, "chip_sum_in")
    red_rest = chip_sum(part_rest, chip, oth_rest, "chip_sum_rest")
    sw_ssem, sw_rsem, swap_thru, swap_tok = split_start(
        "pair_swap_start", _swap_copies, [red_in, red_rest],
        [lax.empty((HROWS, SH_IN), F32), lax.empty((HROWS, D), F32)], 2, red_rest)

    dmod = jnp.concatenate([dshift, dscale, dgate], axis=1)
    gsmall = _pack_small(dmod, dnorm_w, dconv_b, dssm_nw, dqw, dkw, dsk[:, :HQ], ddtb[:, :SH], dalog[:, :SH],
                         ddskip[:, :SH], drb, extra=loss_p[:, :1] + swap_tok[:1, :1], tail=(dconv_w,), rows=32)
    gall = allgather_small(gsmall, "gather_small_grads")
    ssum = sum_devices(gall)
    gs = _unpack_small(ssum[:16])
    loss = gs["extra"]
    dconv_w_sh = lax.dynamic_slice(ssum[16:28].reshape(CONV_K, XBC), (0, chip * 768), (CONV_K, 768))
    dmod_all = gall[:, 0:3, :].reshape(8, 3 * D)
    dw_ada = ada_grad(c_all, lax.dynamic_slice(dmod_all, (0, chip * 768), (8, 768)))

    grads = dict(gs)
    grads["w_ada"] = dw_ada
    grads["conv_w"] = dconv_w_sh

    delta, new_m, new_v = {}, {}, {}

    def step(n):
        delta[n], new_m[n], new_v[n] = adamw(args[n][0], grads[n], args["m_" + n][0], args["v_" + n][0], "adamw_" + n)

    step("w_ada")
    step("conv_w")
    ws = _pack_small(*[args[n] for n in SMALL])
    ms = _pack_small(*[args["m_" + n] for n in SMALL])
    vs = _pack_small(*[args["v_" + n] for n in SMALL])
    d_s, m_s, v_s = adamw(ws, ssum[:16], ms, vs, "adamw_small")
    red_in, red_rest, recv_in, recv_rest = split_wait("pair_swap_wait", _swap_copies, sw_ssem, sw_rsem, swap_thru, 2, d_s)
    d_s, m_s, v_s = _unpack_small(d_s), _unpack_small(m_s), _unpack_small(v_s)
    for n in SMALL:
        delta[n], new_m[n], new_v[n] = d_s[n], m_s[n], v_s[n]
    grads["w_in"], delta["w_in"], new_m["w_in"], new_v["w_in"] = adamw_halves(
        w_in[0], red_in, recv_in, ci, m_w_in[0], v_w_in[0], "adamw_w_in")
    g_shard_rest = jnp.concatenate([jnp.where(ci == 0, red_rest, recv_rest), jnp.where(ci == 0, recv_rest, red_rest)],
                                   axis=0)
    grads["w_attn_proj"] = g_shard_rest[:R_AT]
    grads["w_ssm_proj"] = g_shard_rest[R_AT:R_AT + R_SS]
    grads["w_out"] = g_shard_rest[R_AT + R_SS:]
    for n in ("w_attn_proj", "w_ssm_proj", "w_out"):
        step(n)

    def shaped(n, a):
        return a.reshape(args[n].shape)

    outs = [loss, grad_x[None]]
    for table in (grads, delta, new_m, new_v):
        outs += [shaped(n, table[n]) for n in WEIGHTS]
    return tuple(outs)
```

```python
import functools
import math

import numpy as np
import jax
import jax.numpy as jnp
from jax import lax
from jax.experimental import pallas as pl
from jax.experimental.pallas import tpu as pltpu

F32 = jnp.float32
BF16 = jnp.bfloat16
MESH = pl.DeviceIdType.MESH

D = 1024
HQ, HKV, GRP, DH = 16, 4, 4, 64
BLK = 128
NBUCKET, MAXDIST = 32, 128
SSM_W, SH, SG, SR, SP, SN = 2048, 32, 4, 8, 64, 128
CONV_K = 4
XBC = SSM_W + 2 * SG * SN
IN_W = 9760
EPS = 1e-6
NEG = -1e30
SCALE = DH ** -0.5

C_Q, C_ZA, C_GA, C_GB, C_ZM, C_XBC, C_K, C_V, C_DT = 0, 1024, 2048, 3072, 4096, 6144, 9216, 9472, 9728
NP = 9984
TN = 1664
W_MID = C_XBC - C_ZA

SH_IN = IN_W // 4
R_AT, R_SS, R_OU = 256, 512, 256
HROWS = D // 2

ADAM_LR, ADAM_B1, ADAM_B2, ADAM_EPS, ADAM_WD, ADAM_STEP = 0.001, 0.9, 0.999, 1e-08, 0.01, 10

VMEM_LIMIT = 56 * 1024 * 1024


def _cp(sem=None):
    if sem is None:
        return pltpu.CompilerParams(vmem_limit_bytes=VMEM_LIMIT)
    return pltpu.CompilerParams(dimension_semantics=sem, vmem_limit_bytes=VMEM_LIMIT)


def _sig(x):
    return 0.5 * jnp.tanh(0.5 * x) + 0.5


def _dot(a, b):
    return jnp.dot(a, b, preferred_element_type=F32)


def _dot_nt(a, b):
    return lax.dot_general(a, b, (((1,), (1,)), ((), ())), preferred_element_type=F32)


def _dot_tn(a, b):
    return lax.dot_general(a, b, (((0,), (0,)), ((), ())), preferred_element_type=F32)


def _rsum(x):
    return jnp.sum(x, axis=-1, keepdims=True)


def _csum(x):
    return jnp.sum(x, axis=0, keepdims=True)


def _asum(x):
    return _csum(_rsum(x))


def _full(shape):
    nd = len(shape)
    return pl.BlockSpec(shape, lambda *_: (0,) * nd)


def ada_mod(c_all, w_ada_sh, b_ada_sh):
    def body(c_ref, w_ref, b_ref, o_ref):
        cv = c_ref[...]
        s = cv * _sig(cv)
        o_ref[...] = jnp.dot(s, w_ref[...], preferred_element_type=F32,
                             precision=lax.Precision.HIGHEST) + b_ref[...]

    n = w_ada_sh.shape[1]
    return pl.pallas_call(body, name="ada_mod", out_shape=jax.ShapeDtypeStruct((8, n), F32),
                          compiler_params=_cp())(c_all, w_ada_sh, b_ada_sh)


def ada_grad(c_all, dmod_sh):
    def body(c_ref, d_ref, o_ref):
        cv = c_ref[...]
        s = cv * _sig(cv)
        o_ref[...] = lax.dot_general(s, d_ref[...], (((0,), (0,)), ((), ())), preferred_element_type=F32,
                                     precision=lax.Precision.HIGHEST)

    n = dmod_sh.shape[1]
    return pl.pallas_call(body, name="ada_grad", out_shape=jax.ShapeDtypeStruct((D, n), F32),
                          compiler_params=_cp())(c_all, dmod_sh)


def norm_proj(x, norm_w, scale, shift, wcat):
    t = x.shape[0]
    tm = min(t, 1024)

    def body(x_ref, nw_ref, sc_ref, sh_ref, w_ref, p_ref, dt_ref, ht_ref, hs):
        @pl.when(pl.program_id(1) == 0)
        def _():
            xv = x_ref[...]
            r = lax.rsqrt(jnp.mean(xv * xv, axis=-1, keepdims=True) + EPS)
            h = (xv * r) * nw_ref[...]
            h = h * (1.0 + sc_ref[...]) + sh_ref[...]
            hs[...] = h.astype(BF16)
            ht_ref[...] = h.T.astype(BF16)

        p = _dot(hs[...], w_ref[...])
        p_ref[...] = p.astype(BF16)

        @pl.when(pl.program_id(1) == C_DT // TN)
        def _():
            dt_ref[...] = p[:, C_DT % TN:C_DT % TN + 128]

    vec = pl.BlockSpec((1, D), lambda i, j: (0, 0))
    return pl.pallas_call(
        body, name="norm_proj", grid=(t // tm, NP // TN),
        in_specs=[pl.BlockSpec((tm, D), lambda i, j: (i, 0)), vec, vec, vec,
                  pl.BlockSpec((D, TN), lambda i, j: (0, j))],
        out_specs=[pl.BlockSpec((tm, TN), lambda i, j: (i, j)), pl.BlockSpec((tm, 128), lambda i, j: (i, 0)),
                   pl.BlockSpec((D, tm), lambda i, j: (0, i))],
        out_shape=[jax.ShapeDtypeStruct((t, NP), BF16), jax.ShapeDtypeStruct((t, 128), F32),
                   jax.ShapeDtypeStruct((D, t), BF16)],
        scratch_shapes=[pltpu.VMEM((tm, D), BF16)],
        compiler_params=_cp(("parallel", "arbitrary")),
    )(x, norm_w, scale, shift, wcat)


def _bucket_table():
    qi = np.arange(BLK)[:, None]
    kj = np.arange(2 * BLK)[None, :]
    dist = qi + BLK - kj
    n = np.maximum(dist, 0)
    max_exact = NBUCKET // 2
    nf = np.maximum(n, 1).astype(np.float32)
    large = max_exact + (np.log(nf / np.float32(max_exact)) / np.float32(math.log(MAXDIST / max_exact))
                         * np.float32(NBUCKET - max_exact)).astype(np.int32)
    large = np.minimum(large, NBUCKET - 1)
    bucket = np.where(n < max_exact, n, large).astype(np.int32)
    valid = (dist >= 0) & (dist < BLK)
    return np.where(valid, bucket, -1).astype(np.int32)


def bias_expand(rel_bias, sinks, bucket):
    def body(rb_ref, sk_ref, bk_ref, o_ref):
        bk = bk_ref[...]
        col = lax.broadcasted_iota(jnp.int32, (BLK, 2 * BLK), 1)

        def head(hd, carry):
            def step(b, acc):
                return jnp.where(bk == b, rb_ref[b, hd], acc)

            acc = lax.fori_loop(0, NBUCKET, step, jnp.full((BLK, 2 * BLK), NEG, F32))
            acc = jnp.where(col == 0, sk_ref[0, hd], acc)
            o_ref[1, hd] = acc
            o_ref[0, hd] = jnp.where(jnp.logical_and(col > 0, col < BLK), NEG, acc)
            return carry

        lax.fori_loop(0, HQ, head, 0)

    smem = pl.BlockSpec(memory_space=pltpu.SMEM)
    return pl.pallas_call(
        body, name="bias_expand", in_specs=[smem, smem, VM], out_specs=VM,
        out_shape=jax.ShapeDtypeStruct((2, HQ, BLK, 2 * BLK), F32), compiler_params=_cp(),
    )(rel_bias, sinks, jnp.asarray(bucket))


def bias_reduce(dacc, bucket):
    col = np.arange(BLK * 2 * BLK) % (2 * BLK)
    lane = np.arange(128)[None, :]
    member = (bucket.reshape(-1)[:, None] == lane) | ((col[:, None] == 0) & (lane == NBUCKET))

    def body(d_ref, m_ref, o_ref):
        mm = m_ref[...]
        o_ref[...] = sum(_dot(part, mm) for part in _split3(d_ref[...]))

    return pl.pallas_call(body, name="bias_reduce", out_shape=jax.ShapeDtypeStruct((HQ, 128), F32),
                          compiler_params=_cp())(dacc.reshape(HQ, BLK * 2 * BLK), jnp.asarray(member, BF16))


GQ = GRP * BLK


def _stack_heads(x, nh):
    return jnp.concatenate([x[:, DH * h:DH * (h + 1)] for h in range(nh)], axis=0)


def _unstack(xs, nh):
    rows = xs.shape[0] // nh
    return jnp.concatenate([xs[rows * h:rows * (h + 1)] for h in range(nh)], axis=1)


def _rms(x):
    return lax.rsqrt(jnp.mean(x * x, axis=-1, keepdims=True) + EPS)


def _stack_q(q, qw):
    qs = _stack_heads(q, HQ)
    r = _rms(qs)
    qhat = qs * r
    return qhat * qw, qhat, r


def _band_first(shape):
    return (lax.broadcasted_iota(jnp.int32, shape, 0) & (2 * BLK - 1)) == 0


def _stack_kv(kp, kc, vp, vc, kw):
    ks = _stack_heads(jnp.concatenate([kp, kc], axis=0), HKV)
    r = _rms(ks)
    khat = ks * r
    first = _band_first(ks.shape)
    kn = jnp.where(first, 0.0, khat * kw)
    v2 = jnp.where(first, 0.0, _stack_heads(jnp.concatenate([vp, vc], axis=0), HKV)).astype(BF16)
    return kn, khat, r, v2


def _softmax_rows(s):
    p = jnp.exp(s - jnp.max(s, axis=-1, keepdims=True))
    return p * (1.0 / _rsum(p))


def attn_fwd(proj, biasm, q_norm_w, k_norm_w):
    t = proj.shape[0]
    nb = t // BLK

    def body(q_ref, kc_ref, kp_ref, vc_ref, vp_ref, bm_ref, qw_ref, kw_ref, o_ref):
        f = lambda ref: ref[...].astype(F32)
        qn = _stack_q(f(q_ref), qw_ref[...])[0].astype(BF16)
        kn, _, _, v2 = _stack_kv(f(kp_ref), f(kc_ref), f(vp_ref), f(vc_ref), kw_ref[...])
        knb = kn.astype(BF16)
        s = jnp.concatenate([_dot_nt(qn[GQ * j:GQ * (j + 1)], knb[2 * BLK * j:2 * BLK * (j + 1)])
                             for j in range(HKV)], axis=0)
        pr = _softmax_rows(s * SCALE + bm_ref[0].reshape(HQ * BLK, 2 * BLK)).astype(BF16)
        o = jnp.concatenate([_dot(pr[GQ * j:GQ * (j + 1)], v2[2 * BLK * j:2 * BLK * (j + 1)])
                             for j in range(HKV)], axis=0)
        o_ref[...] = _unstack(o, HQ).astype(BF16)

    kblk, vblk = C_K // 256, C_V // 256
    prev = lambda n: jnp.maximum(n - 1, 0)
    return pl.pallas_call(
        body, name="attn_fwd", grid=(nb,),
        in_specs=[pl.BlockSpec((BLK, D), lambda n: (n, 0)),
                  pl.BlockSpec((BLK, 256), lambda n: (n, kblk)),
                  pl.BlockSpec((BLK, 256), lambda n: (prev(n), kblk)),
                  pl.BlockSpec((BLK, 256), lambda n: (n, vblk)),
                  pl.BlockSpec((BLK, 256), lambda n: (prev(n), vblk)),
                  pl.BlockSpec((1, HQ, BLK, 2 * BLK), lambda n: (jnp.minimum(n, 1), 0, 0, 0)),
                  _full((1, DH)), _full((1, DH))],
        out_specs=pl.BlockSpec((BLK, D), lambda n: (n, 0)),
        out_shape=jax.ShapeDtypeStruct((t, D), BF16),
        compiler_params=_cp(("parallel",)),
    )(proj, proj, proj, proj, proj, biasm, q_norm_w, k_norm_w)


def attn_bwd(proj, dao, biasm, q_norm_w, k_norm_w):
    t = proj.shape[0]
    nb = t // BLK
    kb = 2 * BLK

    def body(q_ref, kc_ref, kp_ref, vc_ref, vp_ref, do_ref, bm_ref, qw_ref, kw_ref,
             dq_ref, dkv_ref, dqw_ref, dkw_ref, dacc_ref, ck, cv, pk, pv, nk, nv):
        n = pl.program_id(0)

        @pl.when(n == 0)
        def _():
            for ref in (dqw_ref, dkw_ref, dacc_ref, ck, cv):
                ref[...] = jnp.zeros_like(ref)

        qw = qw_ref[...]
        kw = kw_ref[...]
        f = lambda ref: ref[...].astype(F32)
        kn, khat, rk, v2 = _stack_kv(f(kp_ref), f(kc_ref), f(vp_ref), f(vc_ref), kw)
        grp = lambda a, j: a[GQ * j:GQ * (j + 1)]
        band = lambda a, j: a[kb * j:kb * (j + 1)]

        @pl.when(n < nb)
        def _():
            qn, qhat, rq = _stack_q(f(q_ref), qw)
            qnb = qn.astype(BF16)
            knb = kn.astype(BF16)
            dos = _stack_heads(f(do_ref), HQ).astype(BF16)
            s = jnp.concatenate([_dot_nt(grp(qnb, j), band(knb, j)) for j in range(HKV)], axis=0)
            pr = _softmax_rows(s * SCALE + bm_ref[0].reshape(HQ * BLK, kb))
            dp = jnp.concatenate([_dot_nt(grp(dos, j), band(v2, j)) for j in range(HKV)], axis=0)
            ds = pr * (dp - _rsum(pr * dp))
            dacc_ref[...] += ds.reshape(HQ, BLK, kb)
            dsb = ds.astype(BF16)
            prb = pr.astype(BF16)
            dqn = jnp.concatenate([_dot(grp(dsb, j), band(knb, j)) for j in range(HKV)], axis=0) * SCALE
            dqhat = dqn * qw
            dq = rq * (dqhat - qhat * jnp.mean(dqhat * qhat, axis=-1, keepdims=True))
            dq_ref[...] = _unstack(dq, HQ).astype(BF16)
            dqw_ref[...] += _csum(dqn * qhat)
            first = _band_first((kb, DH))
            for j in range(HKV):
                rows = slice(BLK * j, BLK * (j + 1))
                dkn = jnp.where(first, 0.0, _dot_tn(grp(dsb, j), grp(qnb, j)) * SCALE)
                dvj = jnp.where(first, 0.0, _dot_tn(grp(prb, j), grp(dos, j)))
                pk[rows, :] = dkn[:BLK]
                nk[rows, :] = dkn[BLK:]
                pv[rows, :] = dvj[:BLK]
                nv[rows, :] = dvj[BLK:]

        @pl.when(n == nb)
        def _():
            for ref in (pk, pv, nk, nv):
                ref[...] = jnp.zeros_like(ref)

        khp = jnp.concatenate([khat[kb * j:kb * j + BLK] for j in range(HKV)], axis=0)
        rkp = jnp.concatenate([rk[kb * j:kb * j + BLK] for j in range(HKV)], axis=0)
        dkn = ck[...] + pk[...]
        dkhat = dkn * kw
        dk = rkp * (dkhat - khp * jnp.mean(dkhat * khp, axis=-1, keepdims=True))
        dkw_ref[...] += _csum(dkn * khp)
        dkv_ref[...] = jnp.concatenate([_unstack(dk, HKV), _unstack(cv[...] + pv[...], HKV)], axis=1).astype(BF16)
        ck[...] = nk[...]
        cv[...] = nv[...]

    kblk, vblk = C_K // 256, C_V // 256
    cur = lambda n: jnp.minimum(n, nb - 1)
    prev = lambda n: jnp.maximum(n - 1, 0)
    carry = pltpu.VMEM((HKV * BLK, DH), F32)
    return pl.pallas_call(
        body, name="attn_bwd", grid=(nb + 1,),
        in_specs=[pl.BlockSpec((BLK, D), lambda n: (cur(n), 0)),
                  pl.BlockSpec((BLK, 256), lambda n: (cur(n), kblk)), pl.BlockSpec((BLK, 256), lambda n: (prev(n), kblk)),
                  pl.BlockSpec((BLK, 256), lambda n: (cur(n), vblk)), pl.BlockSpec((BLK, 256), lambda n: (prev(n), vblk)),
                  pl.BlockSpec((BLK, D), lambda n: (cur(n), 0)),
                  pl.BlockSpec((1, HQ, BLK, kb), lambda n: (jnp.minimum(n, 1), 0, 0, 0)),
                  _full((1, DH)), _full((1, DH))],
        out_specs=[pl.BlockSpec((BLK, D), lambda n: (cur(n), 0)),
                   pl.BlockSpec((BLK, 512), lambda n: (prev(n), 0)),
                   _full((1, DH)), _full((1, DH)), _full((HQ, BLK, kb))],
        out_shape=[jax.ShapeDtypeStruct((t, D), BF16), jax.ShapeDtypeStruct((t, 512), BF16),
                   jax.ShapeDtypeStruct((1, DH), F32),
                   jax.ShapeDtypeStruct((1, DH), F32), jax.ShapeDtypeStruct((HQ, BLK, kb), F32)],
        scratch_shapes=[carry] * 6,
        compiler_params=_cp(("arbitrary",)),
    )(proj, proj, proj, proj, proj, dao, biasm, q_norm_w, k_norm_w)


CONV_TM, CONV_CW, CONV_RC, HALO = 512, 1024, 32, 16


def conv_fwd(proj, conv_w, conv_b):
    t = proj.shape[0]
    tm = min(t, CONV_TM)
    c0 = C_XBC // CONV_CW

    def body(x_ref, xp_ref, w_ref, b_ref, o_ref, ds_ref):
        i = pl.program_id(1)
        w = w_ref[...]
        b = b_ref[...]
        for r in range(tm // CONV_RC):
            lo = r * CONV_RC
            if r == 0:
                head = jnp.where(i == 0, 0.0, xp_ref[...].astype(F32))
                win = jnp.concatenate([head, x_ref[0:CONV_RC, :].astype(F32)], axis=0)
            else:
                win = x_ref[lo - HALO:lo + CONV_RC, :].astype(F32)
            acc = b
            for j in range(CONV_K):
                acc = acc + w[j:j + 1] * win[HALO - 3 + j:HALO - 3 + j + CONV_RC]
            sg = _sig(acc)
            o_ref[lo:lo + CONV_RC, :] = acc * sg
            ds_ref[lo:lo + CONV_RC, :] = _dsilu(acc, sg).astype(BF16)

    rh = tm // HALO
    tile = pl.BlockSpec((tm, CONV_CW), lambda s, i: (i, s))
    return pl.pallas_call(
        body, name="conv_fwd", grid=(XBC // CONV_CW, t // tm),
        in_specs=[pl.BlockSpec((tm, CONV_CW), lambda s, i: (i, c0 + s)),
                  pl.BlockSpec((HALO, CONV_CW), lambda s, i: (jnp.maximum(i * rh - 1, 0), c0 + s)),
                  pl.BlockSpec((CONV_K, CONV_CW), lambda s, i: (0, s)), pl.BlockSpec((1, CONV_CW), lambda s, i: (0, s))],
        out_specs=[tile, tile],
        out_shape=[jax.ShapeDtypeStruct((t, XBC), F32), jax.ShapeDtypeStruct((t, XBC), BF16)],
        compiler_params=_cp(("parallel", "parallel")),
    )(proj, proj, conv_w, conv_b)


def conv_bwd(proj, dact, dsl, conv_w):
    t = proj.shape[0]
    tm = min(t, CONV_TM)
    nt = t // tm
    nr = tm // CONV_RC
    c0 = C_XBC // CONV_CW
    ext = CONV_RC + 8

    def body(x_ref, xp_ref, d_ref, dn_ref, s_ref, sn_ref, w_ref, dx_ref, dw_ref, db_ref):
        i = pl.program_id(1)

        @pl.when(i == 0)
        def _():
            dw_ref[...] = jnp.zeros_like(dw_ref)
            db_ref[...] = jnp.zeros_like(db_ref)

        w = w_ref[...]
        dws = [jnp.zeros((1, CONV_CW), F32) for _ in range(CONV_K)]
        db = jnp.zeros((1, CONV_CW), F32)
        for r in range(nr):
            lo = r * CONV_RC
            if r == 0:
                head = jnp.where(i == 0, 0.0, xp_ref[...].astype(F32))
                win = jnp.concatenate([head, x_ref[0:CONV_RC, :].astype(F32)], axis=0)
            else:
                win = x_ref[lo - HALO:lo + CONV_RC, :].astype(F32)
            if r < nr - 1:
                dext = d_ref[lo:lo + ext, :]
                sext = s_ref[lo:lo + CONV_RC + HALO, :].astype(F32)[0:ext]
            else:
                dext = jnp.concatenate([d_ref[lo:lo + CONV_RC, :], jnp.where(i == nt - 1, 0.0, dn_ref[...])], axis=0)
                sext = jnp.concatenate([s_ref[lo:lo + CONV_RC, :].astype(F32), sn_ref[...].astype(F32)], axis=0)[0:ext]
            dpre = dext * sext
            dx = jnp.zeros((CONV_RC, CONV_CW), F32)
            own = dpre[0:CONV_RC]
            for j in range(CONV_K):
                dx = dx + w[j:j + 1] * dpre[3 - j:3 - j + CONV_RC]
                dws[j] = dws[j] + _csum(own * win[HALO - 3 + j:HALO - 3 + j + CONV_RC])
            db = db + _csum(own)
            dx_ref[lo:lo + CONV_RC, :] = dx.astype(BF16)
        dw_ref[...] += jnp.concatenate(dws, axis=0)
        db_ref[...] += db

    rh = tm // HALO
    r8 = tm // 8
    nxt = lambda i, per: jnp.minimum((i + 1) * per, nt * per - 1)
    return pl.pallas_call(
        body, name="conv_bwd", grid=(XBC // CONV_CW, nt),
        in_specs=[pl.BlockSpec((tm, CONV_CW), lambda s, i: (i, c0 + s)),
                  pl.BlockSpec((HALO, CONV_CW), lambda s, i: (jnp.maximum(i * rh - 1, 0), c0 + s)),
                  pl.BlockSpec((tm, CONV_CW), lambda s, i: (i, s)),
                  pl.BlockSpec((8, CONV_CW), lambda s, i: (nxt(i, r8), s)),
                  pl.BlockSpec((tm, CONV_CW), lambda s, i: (i, s)),
                  pl.BlockSpec((HALO, CONV_CW), lambda s, i: (nxt(i, rh), s)),
                  pl.BlockSpec((CONV_K, CONV_CW), lambda s, i: (0, s))],
        out_specs=[pl.BlockSpec((tm, CONV_CW), lambda s, i: (i, s)),
                   pl.BlockSpec((CONV_K, CONV_CW), lambda s, i: (0, s)), pl.BlockSpec((1, CONV_CW), lambda s, i: (0, s))],
        out_shape=[jax.ShapeDtypeStruct((t, XBC), BF16), jax.ShapeDtypeStruct((CONV_K, XBC), F32),
                   jax.ShapeDtypeStruct((1, XBC), F32)],
        compiler_params=_cp(("parallel", "arbitrary")),
    )(proj, proj, dact, dact, dsl, dsl, conv_w)


def _split3(x):
    h = x.astype(BF16)
    r = x - h.astype(F32)
    m = r.astype(BF16)
    lo = (r - m.astype(F32)).astype(BF16)
    return h, m, lo


def _tri_mm(tri, x):
    h, m, lo = _split3(x)
    return _dot(tri, h) + _dot(tri, m) + _dot(tri, lo)


def _softplus(x):
    return jnp.maximum(x, 0.0) + jnp.log1p(jnp.exp(-jnp.abs(x)))


def _chunk_decays(dt_raw, dtb, alog):
    dtv = _softplus(dt_raw + dtb)
    a = -jnp.exp(alog)
    ri = lax.broadcasted_iota(jnp.int32, (BLK, BLK), 0)
    ci = lax.broadcasted_iota(jnp.int32, (BLK, BLK), 1)
    causal = ri >= ci
    acum = _tri_mm(causal.astype(BF16), dtv * a)
    return dtv, a, causal, acum, acum.T


NPAIR = SH // 2


def _pairs(x):
    return jnp.stack([x[:, 128 * k:128 * (k + 1)] for k in range(NPAIR)])


def _unpairs(x3):
    return jnp.concatenate([x3[k] for k in range(NPAIR)], axis=1)


def _per_head_cols(m):
    return jnp.stack([jnp.broadcast_to(m[:, h:h + 1], m.shape) for h in range(SH)])


def _pair_lanes(t):
    r = t.reshape(NPAIR, 2, t.shape[1], 128)
    lo = lax.broadcasted_iota(jnp.int32, (1, t.shape[1], 128), 2) < SP
    return jnp.where(lo, r[:, 0], r[:, 1])


class _Chunk:
    pass


def _chunk_common(dt_raw, dtb, alog, dskip):
    cm = _Chunk()
    cm.dtv, cm.a, cm.causal, acum, acum_t = _chunk_decays(dt_raw, dtb, alog)
    cm.acol = _per_head_cols(acum)
    cm.arow = jnp.stack([acum_t[h:h + 1, :] for h in range(SH)])
    apl = _pair_lanes(cm.acol)
    alast = apl[:, BLK - 1:BLK, :]
    cm.dpl = _pair_lanes(_per_head_cols(cm.dtv))
    cm.eapl = jnp.exp(apl)
    cm.epl = jnp.exp(alast - apl)
    cm.cdpl = jnp.exp(alast)
    cm.dskpl = _pair_lanes(_per_head_cols(dskip))
    cm.lo = lax.broadcasted_iota(jnp.int32, (1, BLK, 128), 2) < SP
    return cm


def ssd_fwd(act, dt_raw, dtb_p, alog_p, dsk_p):
    t = act.shape[0]
    nc = t // BLK

    def body(xs_ref, b_ref, c_ref, dt_ref, dtb_ref, al_ref, dk_ref, y_ref, sp_ref, st):
        c = pl.program_id(0)

        @pl.when(c == 0)
        def _():
            st[...] = jnp.zeros_like(st)

        s_t = st[...]
        sp_ref[0] = s_t
        cm = _chunk_common(dt_ref[...], dtb_ref[...], al_ref[...], dk_ref[...])
        gms, cbs, bts = [], [], []
        for g in range(SG):
            bf = b_ref[:, SN * g:SN * (g + 1)]
            cb = c_ref[:, SN * g:SN * (g + 1)].astype(BF16)
            gms.append(_dot_nt(cb, bf.astype(BF16)))
            cbs.append(cb)
            bts.append(bf.T.astype(BF16))
        lam = jnp.exp(jnp.where(cm.causal[None], cm.acol - cm.arow, NEG))
        m = (lam.reshape(SG, SR, BLK, BLK) * jnp.stack(gms)[:, None]).reshape(SH, BLK, BLK).astype(BF16)
        xs16 = _pairs(xs_ref[...])
        xdt16 = xs16 * cm.dpl
        x_lo = jnp.where(cm.lo, xdt16, 0.0).astype(BF16)
        x_hi = jnp.where(cm.lo, 0.0, xdt16).astype(BF16)
        s16 = _pairs(s_t)
        s16b = s16.astype(BF16)
        yd = jnp.stack([_dot(m[2 * k], x_lo[k]) + _dot(m[2 * k + 1], x_hi[k]) for k in range(NPAIR)])
        yo = jnp.stack([_dot(cbs[k // (NPAIR // SG)], s16b[k]) for k in range(NPAIR)])
        y_ref[...] = _unpairs(yd + yo * cm.eapl + cm.dskpl * xs16).astype(BF16)
        xe = (xdt16 * cm.epl).astype(BF16)
        st[...] = _unpairs(cm.cdpl * s16 + jnp.stack([_dot(bts[k // (NPAIR // SG)], xe[k]) for k in range(NPAIR)]))

    vec = _full((1, 128))
    return pl.pallas_call(
        body, name="ssd_fwd", grid=(nc,),
        in_specs=[pl.BlockSpec((BLK, SSM_W), lambda c: (c, 0)),
                  pl.BlockSpec((BLK, SG * SN), lambda c: (c, SSM_W // (SG * SN))),
                  pl.BlockSpec((BLK, SG * SN), lambda c: (c, SSM_W // (SG * SN) + 1)),
                  pl.BlockSpec((BLK, 128), lambda c: (c, 0)), vec, vec, vec],
        out_specs=[pl.BlockSpec((BLK, SSM_W), lambda c: (c, 0)), pl.BlockSpec((1, SN, SSM_W), lambda c: (c, 0, 0))],
        out_shape=[jax.ShapeDtypeStruct((t, SSM_W), BF16), jax.ShapeDtypeStruct((nc, SN, SSM_W), F32)],
        scratch_shapes=[pltpu.VMEM((SN, SSM_W), F32)],
        compiler_params=_cp(("arbitrary",)),
    )(act, act, act, dt_raw, dtb_p, alog_p, dsk_p)


def _head_sums(q):
    r = q.shape[1]
    lo = lax.broadcasted_iota(jnp.int32, (1, r, 128), 2) < SP
    s_lo = jnp.sum(jnp.where(lo, q, 0.0), axis=-1, keepdims=True)
    s_hi = jnp.sum(jnp.where(lo, 0.0, q), axis=-1, keepdims=True)
    lane = lax.broadcasted_iota(jnp.int32, (r, 128), 1)
    out = jnp.zeros((r, 128), F32)
    for k in range(NPAIR):
        out = jnp.where(lane == 2 * k, s_lo[k], jnp.where(lane == 2 * k + 1, s_hi[k], out))
    return out


def ssd_bwd(act, dt_raw, dy, sprev, dtb_p, alog_p, dsk_p):
    t = act.shape[0]
    nc = t // BLK

    def body(xs_ref, b_ref, c_ref, dt_ref, dy_ref, sp_ref, dtb_ref, al_ref, dk_ref,
             da_ref, ddt_ref, ddtb_ref, dal_ref, ddk_ref, dst):
        i = pl.program_id(0)

        @pl.when(i == 0)
        def _():
            dst[...] = jnp.zeros_like(dst)
            ddtb_ref[...] = jnp.zeros_like(ddtb_ref)
            dal_ref[...] = jnp.zeros_like(dal_ref)
            ddk_ref[...] = jnp.zeros_like(ddk_ref)

        dt_raw = dt_ref[...]
        dtb = dtb_ref[...]
        cm = _chunk_common(dt_raw, dtb, al_ref[...], dk_ref[...])
        ri = lax.broadcasted_iota(jnp.int32, (BLK, BLK), 0)
        ci = lax.broadcasted_iota(jnp.int32, (BLK, BLK), 1)
        lam_t = jnp.exp(jnp.where((ri <= ci)[None], cm.arow - cm.acol, NEG))
        bbs, cbs, cts, gms = [], [], [], []
        for g in range(SG):
            bf = b_ref[:, SN * g:SN * (g + 1)]
            cf = c_ref[:, SN * g:SN * (g + 1)]
            bbs.append(bf.astype(BF16))
            cbs.append(cf.astype(BF16))
            cts.append(cf.T.astype(BF16))
            gms.append(_dot_nt(bbs[g], cbs[g]))
        grp = lambda k: k // (NPAIR // SG)
        xs16 = _pairs(xs_ref[...])
        dy16 = _pairs(dy_ref[...].astype(F32))
        sp16 = _pairs(sp_ref[0])
        ds16 = _pairs(dst[...])
        xdt16 = xs16 * cm.dpl
        xdtb = xdt16.astype(BF16)
        dyh = [jnp.where(cm.lo, dy16, 0.0).astype(BF16), jnp.where(cm.lo, 0.0, dy16).astype(BF16)]
        m_t = (lam_t.reshape(SG, SR, BLK, BLK) * jnp.stack(gms)[:, None]).reshape(SH, BLK, BLK).astype(BF16)
        dxdt = jnp.stack([_dot(m_t[2 * k], dyh[0][k]) + _dot(m_t[2 * k + 1], dyh[1][k]) for k in range(NPAIR)])
        dm_t = jnp.stack([_dot_nt(xdtb[h // 2], dyh[h % 2][h // 2]) for h in range(SH)])
        dg_t = jnp.sum((dm_t * lam_t).reshape(SG, SR, BLK, BLK), axis=1).astype(BF16)
        xq16 = xdtb.astype(F32)
        xh = [jnp.where(cm.lo, xdt16, 0.0).astype(BF16), jnp.where(cm.lo, 0.0, xdt16).astype(BF16)]
        y_in = jnp.stack([_dot_tn(m_t[2 * k], xh[0][k]) + _dot_tn(m_t[2 * k + 1], xh[1][k]) for k in range(NPAIR)])
        da_diag = dy16 * y_in - xq16 * dxdt
        lane_c = lax.broadcasted_iota(jnp.int32, (BLK, 128), 1)
        ds16b = ds16.astype(BF16)
        sp16b = sp16.astype(BF16)
        dxs = jnp.stack([_dot(bbs[grp(k)], ds16b[k]) for k in range(NPAIR)]) * cm.epl
        dxdt = dxdt + dxs
        dya = (dy16 * cm.eapl).astype(BF16)
        xe = (xdt16 * cm.epl).astype(BF16)
        dcs, dbs = [], []
        for g in range(SG):
            ks = range(g * (NPAIR // SG), (g + 1) * (NPAIR // SG))
            dcs.append(sum(_dot_nt(dya[k], sp16b[k]) for k in ks) + _dot_tn(dg_t[g], bbs[g]))
            dbs.append(sum(_dot_nt(xe[k], ds16b[k]) for k in ks) + _dot(dg_t[g], cbs[g]))
        dst[...] = _unpairs(cm.cdpl * ds16 + jnp.stack([_dot(cts[grp(k)], dya[k]) for k in range(NPAIR)]))
        da_ref[...] = jnp.concatenate([_unpairs(dxdt * cm.dpl + cm.dskpl * dy16)] + dbs + dcs, axis=1)
        y_off = jnp.stack([_dot(cbs[grp(k)], sp16b[k]) for k in range(NPAIR)]) * cm.eapl
        da_cols = _head_sums(da_diag + dy16 * y_off - xdt16 * dxs)
        last = _head_sums(jnp.sum(xdt16 * dxs, axis=1, keepdims=True)
                          + cm.cdpl * jnp.sum(ds16 * sp16, axis=1, keepdims=True))
        ddt = _head_sums(dxdt * xs16)
        row_i = lax.broadcasted_iota(jnp.int32, (BLK, 128), 0)
        dacum = da_cols + jnp.where(row_i == BLK - 1, last, 0.0)
        dda = _tri_mm((ri <= ci).astype(BF16), dacum)
        ddt = ddt + dda * cm.a
        dal_ref[...] += _csum(dda * cm.dtv) * cm.a
        ddt_raw = jnp.where(lane_c < SH, ddt * _sig(dt_raw + dtb), 0.0)
        ddt_ref[...] = ddt_raw.astype(BF16)
        ddtb_ref[...] += _csum(ddt_raw)
        ddk_ref[...] += _head_sums(jnp.sum(dy16 * xs16, axis=1, keepdims=True))

    rev = lambda i: nc - 1 - i
    vec = _full((1, 128))
    slab = pl.BlockSpec((BLK, SSM_W), lambda i: (rev(i), 0))
    return pl.pallas_call(
        body, name="ssd_bwd", grid=(nc,),
        in_specs=[slab,
                  pl.BlockSpec((BLK, SG * SN), lambda i: (rev(i), SSM_W // (SG * SN))),
                  pl.BlockSpec((BLK, SG * SN), lambda i: (rev(i), SSM_W // (SG * SN) + 1)),
                  pl.BlockSpec((BLK, 128), lambda i: (rev(i), 0)),
                  slab,
                  pl.BlockSpec((1, SN, SSM_W), lambda i: (rev(i), 0, 0)), vec, vec, vec],
        out_specs=[pl.BlockSpec((BLK, XBC), lambda i: (rev(i), 0)), pl.BlockSpec((BLK, 128), lambda i: (rev(i), 0)),
                   vec, vec, vec],
        out_shape=[jax.ShapeDtypeStruct((t, XBC), F32), jax.ShapeDtypeStruct((t, 128), BF16),
                   jax.ShapeDtypeStruct((1, 128), F32), jax.ShapeDtypeStruct((1, 128), F32),
                   jax.ShapeDtypeStruct((1, 128), F32)],
        scratch_shapes=[pltpu.VMEM((SN, SSM_W), F32)],
        compiler_params=_cp(("arbitrary",)),
    )(act, act, act, dt_raw, dy, sprev, dtb_p, alog_p, dsk_p)


TAIL_TM = 256


def _dsilu(z, s):
    return s * (1.0 + z * (1.0 - s))


def tail(proj, ao, yss, x, target, gate, ssm_nw, w_at, w_ss, w_ou):
    t = x.shape[0]
    tm = min(t, TAIL_TM)
    gw = SSM_W // SG

    def body(ao_ref, za_ref, ga_ref, gb_ref, zm_ref, ys_ref, x_ref, tg_ref, gt_ref, nw_ref, wa_ref, ws_ref, wo_ref,
             loss_ref, dy_ref, dao_ref, dmid_ref, dys_ref,
             ua_ref, yn_ref, mg_ref, dya_ref, dyb_ref, do_ref, dgt_ref, dnw_ref):
        i = pl.program_id(0)

        @pl.when(i == 0)
        def _():
            loss_ref[...] = jnp.zeros_like(loss_ref)
            dgt_ref[...] = jnp.zeros_like(dgt_ref)
            dnw_ref[...] = jnp.zeros_like(dnw_ref)

        ao = ao_ref[...].astype(F32)
        za = za_ref[...].astype(F32)
        sa = _sig(za)
        sila = za * sa
        ua_f = ao * sila
        ua = ua_f.astype(BF16)
        ya = _dot(ua, wa_ref[...])
        zm = zm_ref[...].astype(F32)
        sm = _sig(zm)
        silm = zm * sm
        ys = ys_ref[...].astype(F32)
        u = ys * silm
        nw = nw_ref[...]
        rs, uns = [], []
        for g in range(SG):
            ug = u[:, gw * g:gw * (g + 1)]
            r = lax.rsqrt(jnp.mean(ug * ug, axis=-1, keepdims=True) + EPS)
            rs.append(r)
            uns.append(ug * r)
        un = jnp.concatenate(uns, axis=1)
        yn_f = un * nw
        yn = yn_f.astype(BF16)
        yb = _dot(yn, ws_ref[...])
        sga = _sig(ga_ref[...].astype(F32))
        sgb = _sig(gb_ref[...].astype(F32))
        mg_f = sga * ya + sgb * yb
        mg = mg_f.astype(BF16)
        o = _dot(mg, wo_ref[...])
        gt = gt_ref[...]
        err = (x_ref[...] + gt * o) - tg_ref[...]
        lane = lax.broadcasted_iota(jnp.int32, (1, 128), 1)
        loss_ref[...] += jnp.where(lane == 0, 0.5 * _asum(_rsum(err * err) / D), 0.0)
        dy = err * (1.0 / D)
        dy_ref[...] = dy
        dgt_ref[...] += _csum(dy * o)
        do = (dy * gt).astype(BF16)
        dmg = _dot_nt(do, wo_ref[...])
        dmid_ref[:, C_GA - C_ZA:C_GB - C_ZA] = (dmg * ya * sga * (1.0 - sga)).astype(BF16)
        dmid_ref[:, C_GB - C_ZA:C_ZM - C_ZA] = (dmg * yb * sgb * (1.0 - sgb)).astype(BF16)
        dya = (dmg * sga).astype(BF16)
        dyb = (dmg * sgb).astype(BF16)
        dua = _dot_nt(dya, wa_ref[...])
        dao_ref[...] = (dua * sila).astype(BF16)
        dmid_ref[:, 0:C_GA - C_ZA] = (dua * ao * _dsilu(za, sa)).astype(BF16)
        dyn = _dot_nt(dyb, ws_ref[...])
        dnw_ref[...] += _csum(dyn * un)
        dun = dyn * nw
        dus = []
        for g in range(SG):
            gs = slice(gw * g, gw * (g + 1))
            dus.append(rs[g] * (dun[:, gs] - uns[g] * jnp.mean(dun[:, gs] * uns[g], axis=-1, keepdims=True)))
        du = jnp.concatenate(dus, axis=1)
        dys_ref[...] = (du * silm).astype(BF16)
        dmid_ref[:, C_ZM - C_ZA:] = (du * ys * _dsilu(zm, sm)).astype(BF16)
        ua_ref[...] = ua_f.T.astype(BF16)
        yn_ref[...] = yn_f.T.astype(BF16)
        mg_ref[...] = mg_f.T.astype(BF16)
        dya_ref[...] = dya
        dyb_ref[...] = dyb
        do_ref[...] = do

    row = lambda w: pl.BlockSpec((tm, w), lambda i: (i, 0))
    pcol = lambda w, c0: pl.BlockSpec((tm, w), lambda i: (i, c0 // w))
    sd = lambda w, dt: jax.ShapeDtypeStruct((t, w), dt)
    colt = lambda w: pl.BlockSpec((w, tm), lambda i: (0, i))
    sdt = lambda w: jax.ShapeDtypeStruct((w, t), BF16)
    return pl.pallas_call(
        body, name="tail", grid=(t // tm,),
        in_specs=[row(D), pcol(D, C_ZA), pcol(D, C_GA), pcol(D, C_GB), pcol(SSM_W, C_ZM), row(SSM_W), row(D), row(D),
                  _full((1, D)), _full((1, SSM_W)), _full((D, D)), _full((SSM_W, D)), _full((D, D))],
        out_specs=[_full((1, 128)), row(D), row(D), row(W_MID), row(SSM_W),
                   colt(D), colt(SSM_W), colt(D), row(D), row(D), row(D), _full((1, D)), _full((1, SSM_W))],
        out_shape=[jax.ShapeDtypeStruct((1, 128), F32), sd(D, F32), sd(D, BF16), sd(W_MID, BF16),
                   sd(SSM_W, BF16), sdt(D), sdt(SSM_W), sdt(D), sd(D, BF16),
                   sd(D, BF16), sd(D, BF16), jax.ShapeDtypeStruct((1, D), F32), jax.ShapeDtypeStruct((1, SSM_W), F32)],
        compiler_params=_cp(("arbitrary",)),
    )(ao, proj, proj, proj, proj, yss, x, target, gate, ssm_nw, w_at, w_ss, w_ou)


DPIECES = ((D, ((D, C_Q),)),
           (W_MID, ((D, C_ZA), (D, C_GA), (D, C_GB), (SSM_W, C_ZM))),
           (XBC, ((XBC, C_XBC),)),
           (512, ((512, C_K),)),
           (128, ((128, C_DT),)))


def dproj_bwd(pieces, wcat, x, dy, norm_w, scale):
    t = x.shape[0]
    tm = min(t, 256)
    nt = t // tm
    wblocks = [blk for _, subs in DPIECES for blk in subs]
    npc, nwb = len(DPIECES), len(wblocks)

    def body(*refs):
        p_refs, w_refs = refs[:npc], refs[npc:npc + nwb]
        x_ref, dy_ref, nw_ref, sc_ref, gx_ref, dnw_ref, dsc_ref, dsh_ref, dwe_ref = refs[npc + nwb:]
        i = pl.program_id(0)

        @pl.when(i == 0)
        def _():
            for ref in (dwe_ref, dsh_ref, dnw_ref, dsc_ref):
                ref[...] = jnp.zeros_like(ref)

        dh, wi = None, 0
        for p_ref, (_, subs) in zip(p_refs, DPIECES):
            loc = 0
            for w, _ in subs:
                part = _dot_nt(p_ref[:, loc:loc + w], w_refs[wi][...])
                dh = part if dh is None else dh + part
                loc += w
                wi += 1
        xv = x_ref[...]
        r = lax.rsqrt(jnp.mean(xv * xv, axis=-1, keepdims=True) + EPS)
        xn = xv * r
        weff = nw_ref[...] * (1.0 + sc_ref[...])
        dxn = dh * weff
        gx_ref[...] = dy_ref[...] + r * (dxn - xn * jnp.mean(dxn * xn, axis=-1, keepdims=True))
        dwe_ref[...] += _csum(dh * xn)
        dsh_ref[...] += _csum(dh)

        @pl.when(i == nt - 1)
        def _():
            dwe = dwe_ref[...]
            dnw_ref[...] = dwe * (1.0 + sc_ref[...])
            dsc_ref[...] = dwe * nw_ref[...]

    vec = pl.BlockSpec((1, D), lambda i: (0, 0))
    row = pl.BlockSpec((tm, D), lambda i: (i, 0))
    return pl.pallas_call(
        body, name="dproj_bwd", grid=(nt,),
        in_specs=[pl.BlockSpec((tm, pw), lambda i: (i, 0)) for pw, _ in DPIECES]
        + [pl.BlockSpec((D, w), functools.partial(lambda i, b: (0, b), b=off // w), pipeline_mode=pl.Buffered(1))
           for w, off in wblocks]
        + [row, row, vec, vec],
        out_specs=[row, vec, vec, vec],
        out_shape=[jax.ShapeDtypeStruct((t, D), F32), jax.ShapeDtypeStruct((1, D), F32),
                   jax.ShapeDtypeStruct((1, D), F32), jax.ShapeDtypeStruct((1, D), F32)],
        scratch_shapes=[pltpu.VMEM((1, D), F32)],
        compiler_params=_cp(("arbitrary",)),
    )(*pieces, *([wcat] * nwb), x, dy, norm_w, scale)


def wgrad(at, b, name, bn, after):
    m, t = at.shape
    n = b.shape[1]
    tk = min(t, 1024)
    bm = min(m, 1024)

    def body(a_ref, b_ref, after_ref, o_ref):
        part = _dot(a_ref[...], b_ref[...])

        @pl.when(pl.program_id(2) == 0)
        def _():
            o_ref[...] = part

        @pl.when(pl.program_id(2) > 0)
        def _():
            o_ref[...] += part

    return pl.pallas_call(
        body, name=name, grid=(m // bm, n // bn, t // tk),
        in_specs=[pl.BlockSpec((bm, tk), lambda i, j, k: (i, k)), pl.BlockSpec((tk, bn), lambda i, j, k: (k, j)), ANY],
        out_specs=pl.BlockSpec((bm, bn), lambda i, j, k: (i, j)),
        out_shape=jax.ShapeDtypeStruct((m, n), F32),
        compiler_params=_cp(("parallel", "parallel", "arbitrary")),
    )(at, b, after)


SUM_TR = 256


def pair_sum(g, core, theirs, name):
    w = g.shape[2]
    nh = HROWS // SUM_TR

    def body(core_ref, a_ref, b_ref, o_ref, ob_ref):
        s = a_ref[...] + b_ref[...]
        o_ref[...] = s
        ob_ref[...] = s.astype(BF16)

    spec = pl.BlockSpec((1, SUM_TR, w), lambda d, i, c: (d, i, 0))
    return pl.pallas_call(
        body, name=name,
        out_shape=[jax.ShapeDtypeStruct((4, HROWS, w), F32), jax.ShapeDtypeStruct((4, HROWS, w), BF16)],
        grid_spec=pltpu.PrefetchScalarGridSpec(
            num_scalar_prefetch=1, grid=(4, nh),
            in_specs=[pl.BlockSpec((1, SUM_TR, w), lambda d, i, c: (d, c[0] * nh + i, 0)), spec],
            out_specs=[spec, spec]),
        compiler_params=_cp(("parallel", "parallel")))(core.reshape(1).astype(jnp.int32), g, theirs)


def chip_sum(part, chip, others, name):
    r, w = part.shape[1:]

    def body(chip_ref, a_ref, b_ref, o_ref):
        acc = a_ref[0]
        for k in range(3):
            acc = acc + b_ref[k].astype(F32)
        o_ref[...] = acc

    return pl.pallas_call(
        body, name=name, out_shape=jax.ShapeDtypeStruct((r, w), F32),
        grid_spec=pltpu.PrefetchScalarGridSpec(
            num_scalar_prefetch=1, grid=(r // SUM_TR,),
            in_specs=[pl.BlockSpec((1, SUM_TR, w), lambda i, c: (c[0], i, 0)),
                      pl.BlockSpec((3, SUM_TR, w), lambda i, c: (0, i, 0))],
            out_specs=pl.BlockSpec((SUM_TR, w), lambda i, c: (i, 0))),
        compiler_params=_cp(("parallel",)))(chip.reshape(1).astype(jnp.int32), part, others)


def sum_devices(g):
    r = g.shape[1]

    def body(g_ref, o_ref):
        acc = g_ref[0]
        for d in range(1, 8):
            acc = acc + g_ref[d]
        o_ref[...] = acc

    return pl.pallas_call(body, name="sum_devices", out_shape=jax.ShapeDtypeStruct((r, 1024), F32),
                          compiler_params=_cp())(g)


def adamw(w, g, m, v, name):
    r, c = w.shape
    tr = r
    for cand in (256, 128, 64, 32, 16, 8):
        if r % cand == 0 and r > cand:
            tr = cand
            break

    def body(w_ref, g_ref, m_ref, v_ref, d_ref, nm_ref, nv_ref):
        gv = g_ref[...]
        mn = ADAM_B1 * m_ref[...] + (1.0 - ADAM_B1) * gv
        vn = ADAM_B2 * v_ref[...] + (1.0 - ADAM_B2) * (gv * gv)
        m_hat = mn / (1.0 - ADAM_B1 ** ADAM_STEP)
        v_hat = vn / (1.0 - ADAM_B2 ** ADAM_STEP)
        d_ref[...] = -ADAM_LR * (m_hat / (jnp.sqrt(v_hat) + ADAM_EPS) + ADAM_WD * w_ref[...])
        nm_ref[...] = mn
        nv_ref[...] = vn

    spec = pl.BlockSpec((tr, c), lambda i: (i, 0))
    sd = jax.ShapeDtypeStruct((r, c), F32)
    return pl.pallas_call(body, name=name, grid=(r // tr,), in_specs=[spec] * 4, out_specs=[spec] * 3,
                          out_shape=[sd, sd, sd], compiler_params=_cp(("parallel",)))(w, g, m, v)


def adamw_halves(w, mine, theirs, core, m, v, name):
    r, c = w.shape
    tr = 128
    nh = HROWS // tr

    def body(core_ref, w_ref, a_ref, b_ref, m_ref, v_ref, g_ref, d_ref, nm_ref, nv_ref):
        gv = jnp.where(pl.program_id(0) // nh == core_ref[0], a_ref[...], b_ref[...])
        mn = ADAM_B1 * m_ref[...] + (1.0 - ADAM_B1) * gv
        vn = ADAM_B2 * v_ref[...] + (1.0 - ADAM_B2) * (gv * gv)
        m_hat = mn / (1.0 - ADAM_B1 ** ADAM_STEP)
        v_hat = vn / (1.0 - ADAM_B2 ** ADAM_STEP)
        g_ref[...] = gv
        d_ref[...] = -ADAM_LR * (m_hat / (jnp.sqrt(v_hat) + ADAM_EPS) + ADAM_WD * w_ref[...])
        nm_ref[...] = mn
        nv_ref[...] = vn

    spec = pl.BlockSpec((tr, c), lambda i, s: (i, 0))
    half = pl.BlockSpec((tr, c), lambda i, s: (i % nh, 0))
    sd = jax.ShapeDtypeStruct((r, c), F32)
    return pl.pallas_call(
        body, name=name, out_shape=[sd, sd, sd, sd],
        grid_spec=pltpu.PrefetchScalarGridSpec(num_scalar_prefetch=1, grid=(r // tr,),
                                               in_specs=[spec, half, half, spec, spec], out_specs=[spec] * 4),
        compiler_params=_cp(("parallel",)))(core.reshape(1).astype(jnp.int32), w, mine, theirs, m, v)


ANY = pl.BlockSpec(memory_space=pl.ANY)
VM = pl.BlockSpec(memory_space=pltpu.VMEM)
OTHER_CHIPS = ((1, 0), (0, 1), (1, 1))


def _pos():
    return lax.axis_index("x"), lax.axis_index("y"), lax.axis_index("c")


def _flip(v, bit):
    return 1 - v if bit else v


def _rcopy(src, dst, ssem, rsem, peer):
    return pltpu.make_async_remote_copy(src_ref=src, dst_ref=dst, send_sem=ssem, recv_sem=rsem,
                                        device_id=peer, device_id_type=MESH)


def allgather_small(p, name):
    r = p.shape[0]

    def body(in_ref, out_ref, ssem, rsem, lsem):
        x, y, c = _pos()
        me = 4 * x + 2 * y + c
        loc = pltpu.make_async_copy(in_ref, out_ref.at[me], lsem)
        loc.start()
        sends = []
        peers = []
        for k in range(1, 8):
            px, py, pc = _flip(x, (k >> 2) & 1), _flip(y, (k >> 1) & 1), _flip(c, k & 1)
            peers.append((px, py, pc))
            cp = _rcopy(in_ref, out_ref.at[me], ssem.at[k - 1], rsem.at[k - 1], (px, py, pc))
            cp.start()
            sends.append(cp)
        for k in range(1, 8):
            px, py, pc = peers[k - 1]
            _rcopy(in_ref, out_ref.at[4 * px + 2 * py + pc], ssem.at[k - 1], rsem.at[k - 1], (px, py, pc)).wait_recv()
        for cp in sends:
            cp.wait_send()
        loc.wait()

    return pl.pallas_call(
        body, name=name, out_shape=jax.ShapeDtypeStruct((8, r, 1024), F32),
        in_specs=[VM], out_specs=VM,
        scratch_shapes=[pltpu.SemaphoreType.DMA((7,)), pltpu.SemaphoreType.DMA((7,)), pltpu.SemaphoreType.DMA],
    )(p)


def gather_weights(w_in_b, mod_sh):
    def body(wi_ref, m_ref, gi_ref, mo_ref, ssem, rsem, lsem):
        x, y, c = _pos()
        chip = 2 * x + y
        mine = pl.ds(pl.multiple_of(c * HROWS, 16), HROWS)
        other = pl.ds(pl.multiple_of((1 - c) * HROWS, 16), HROWS)
        sib = (x, y, 1 - c)
        pairs = ((wi_ref, gi_ref),)
        loc_m = pltpu.make_async_copy(m_ref, mo_ref.at[chip], lsem)
        loc_m.start()
        sends = []
        for k, (fx, fy) in enumerate(OTHER_CHIPS):
            peer = (_flip(x, fx), _flip(y, fy), c)
            for a, (w_ref, g_ref) in enumerate(pairs):
                cw = _rcopy(w_ref.at[mine], g_ref.at[chip, mine], ssem.at[6 * a + k], rsem.at[6 * a + k], peer)
                cw.start()
                sends.append(cw)
            cm = _rcopy(m_ref, mo_ref.at[chip], ssem.at[12 + k], rsem.at[12 + k], peer)
            cm.start()
            sends.append(cm)
        for k, (fx, fy) in enumerate(OTHER_CHIPS):
            px, py = _flip(x, fx), _flip(y, fy)
            for a, (w_ref, g_ref) in enumerate(pairs):
                got = g_ref.at[2 * px + py, mine]
                _rcopy(w_ref.at[mine], got, ssem.at[6 * a + k], rsem.at[6 * a + k], (px, py, c)).wait_recv()
                fw = _rcopy(got, got, ssem.at[6 * a + 3 + k], rsem.at[6 * a + 3 + k], sib)
                fw.start()
                sends.append(fw)
        for k, (fx, fy) in enumerate(OTHER_CHIPS):
            px, py = _flip(x, fx), _flip(y, fy)
            for a, (w_ref, g_ref) in enumerate(pairs):
                land = g_ref.at[2 * px + py, other]
                _rcopy(land, land, ssem.at[6 * a + 3 + k], rsem.at[6 * a + 3 + k], sib).wait_recv()
            _rcopy(m_ref, mo_ref.at[2 * px + py], ssem.at[12 + k], rsem.at[12 + k], (px, py, c)).wait_recv()
        for cp in sends:
            cp.wait_send()
        loc_m.wait()

    return pl.pallas_call(
        body, name="gather_weights",
        out_shape=[jax.ShapeDtypeStruct((4, D, SH_IN), BF16), jax.ShapeDtypeStruct((4, 8, 768), F32)],
        in_specs=[ANY, VM], out_specs=[ANY, VM],
        scratch_shapes=[pltpu.SemaphoreType.DMA((15,)), pltpu.SemaphoreType.DMA((15,)), pltpu.SemaphoreType.DMA],
    )(w_in_b, mod_sh)


def pair_exchange(g):
    def body(g_ref, r_ref, ssem, rsem):
        x, y, c = _pos()
        other = pl.ds(pl.multiple_of((1 - c) * HROWS, 8), HROWS)
        cp = _rcopy(g_ref.at[:, other, :], r_ref, ssem, rsem, (x, y, 1 - c))
        cp.start()
        cp.wait()

    return pl.pallas_call(
        body, name="pair_exchange", out_shape=jax.ShapeDtypeStruct((4, HROWS, g.shape[2]), F32),
        in_specs=[ANY], out_specs=ANY,
        scratch_shapes=[pltpu.SemaphoreType.DMA, pltpu.SemaphoreType.DMA],
    )(g)


HBM = pl.BlockSpec(memory_space=pltpu.HBM)
SEM = pl.BlockSpec(memory_space=pltpu.SEMAPHORE)
DATAFLOW = pltpu.SideEffectType.DATAFLOW_SIDE_EFFECTING


def split_start(name, make_copies, srcs, lands, nsem, after):
    arrays = [*srcs, *lands]
    n, ns = len(arrays), len(srcs)

    def body(*refs):
        for cp in make_copies(refs[:ns], refs[ns:n], refs[n + 1], refs[n + 2])[0]:
            cp.start()
        refs[-1][...] = jnp.zeros_like(refs[-1])

    res = pl.pallas_call(
        body, name=name,
        out_shape=(pltpu.SemaphoreType.DMA((nsem,)), pltpu.SemaphoreType.DMA((nsem,)),
                   *[pltpu.HBM(a.shape, a.dtype) for a in arrays], jax.ShapeDtypeStruct((8, 128), F32)),
        in_specs=(HBM,) * n + (ANY,), out_specs=(SEM, SEM) + (HBM,) * n + (VM,),
        input_output_aliases={i: 2 + i for i in range(n)},
        compiler_params=pltpu.CompilerParams(has_side_effects=DATAFLOW),
    )(*[pltpu.with_memory_space_constraint(a, pltpu.HBM) for a in arrays], after)
    return res[0], res[1], list(res[2:2 + n]), res[-1]


def split_wait(name, make_copies, ssem, rsem, arrays, ns, after):
    n = len(arrays)

    def body(*refs):
        sends, recvs = make_copies(refs[:ns], refs[ns:n], refs[n], refs[n + 1])
        for cp in sends:
            cp.wait_send()
        for cp in recvs:
            cp.wait_recv()

    return pl.pallas_call(
        body, name=name, out_shape=tuple(pltpu.HBM(a.shape, a.dtype) for a in arrays),
        in_specs=(HBM,) * n + (SEM, SEM, ANY), out_specs=(HBM,) * n,
        input_output_aliases={i: i for i in range(n)},
        compiler_params=pltpu.CompilerParams(has_side_effects=DATAFLOW),
    )(*arrays, ssem, rsem, after)


def _chip_copies(srcs, lands, ssem, rsem):
    x, y, c = _pos()
    copies = []
    for k, (fx, fy) in enumerate(OTHER_CHIPS):
        px, py = _flip(x, fx), _flip(y, fy)
        for a, (p_ref, l_ref) in enumerate(zip(srcs, lands)):
            copies.append(_rcopy(p_ref.at[2 * px + py], l_ref.at[k], ssem.at[3 * a + k], rsem.at[3 * a + k], (px, py, c)))
    return copies, copies


def _pair_copies(srcs, lands, ssem, rsem):
    x, y, c = _pos()
    other = pl.ds(pl.multiple_of((1 - c) * HROWS, 8), HROWS)
    copies = [_rcopy(srcs[0].at[:, other, :], lands[0], ssem.at[0], rsem.at[0], (x, y, 1 - c))]
    return copies, copies


def _rest_copies(srcs, lands, ssem, rsem):
    x, y, c = _pos()
    chip = 2 * x + y
    mine = pl.ds(pl.multiple_of(c * HROWS, 16), HROWS)
    sends, recvs = [], []
    for k, (fx, fy) in enumerate(OTHER_CHIPS):
        px, py = _flip(x, fx), _flip(y, fy)
        for t in range(2):
            rows_t = pl.ds(t * HROWS, HROWS)
            sends.append(_rcopy(srcs[0].at[mine], lands[0].at[chip, mine], ssem.at[2 * k + t], rsem.at[2 * k + c],
                                (px, py, t)))
            recvs.append(_rcopy(srcs[0].at[rows_t], lands[0].at[2 * px + py, rows_t], ssem.at[2 * k + t],
                                rsem.at[2 * k + t], (px, py, t)))
    return sends, recvs


def _swap_copies(srcs, lands, ssem, rsem):
    x, y, c = _pos()
    copies = [_rcopy(s_ref, l_ref, ssem.at[a], rsem.at[a], (x, y, 1 - c))
              for a, (s_ref, l_ref) in enumerate(zip(srcs, lands))]
    return copies, copies


def _flat(v, width=1024):
    v = v.reshape(-1)
    n = -(-v.shape[0] // width) * width
    return jnp.pad(v, (0, n - v.shape[0]))


def _rows(parts, rows):
    flat = jnp.concatenate(parts)
    return jnp.pad(flat, (0, rows * 1024 - flat.shape[0])).reshape(rows, 1024)


def _pack_small(b_ada, norm_w, conv_b, ssm_norm_w, q_norm_w, k_norm_w, sinks, dt_bias, a_log, d_skip, rel_bias,
                extra=None, tail=(), rows=16):
    misc = [q_norm_w, k_norm_w, sinks, dt_bias, a_log, d_skip] + ([] if extra is None else [extra])
    parts = [_flat(b_ada), _flat(norm_w), _flat(conv_b), _flat(ssm_norm_w)] + [_flat(v, 128) for v in misc]
    parts.append(jnp.zeros(((8 - len(misc)) * 128,), F32))
    parts.append(_flat(rel_bias))
    parts.append(jnp.zeros((5 * 1024,), F32))
    return _rows(parts + [_flat(v) for v in tail], rows)


def _unpack_small(p):
    misc = p[9]
    return dict(b_ada=p[0:3].reshape(1, 3072), norm_w=p[3:4], conv_b=p[4:7].reshape(1, 3072),
                ssm_norm_w=p[7:9].reshape(1, 2048), q_norm_w=misc[None, 0:64], k_norm_w=misc[None, 128:192],
                sinks=misc[None, 256:272], dt_bias=misc[None, 384:416], a_log=misc[None, 512:544],
                d_skip=misc[None, 640:672], rel_bias=p[10, :512].reshape(32, 16), extra=misc[768])


SMALL = ("b_ada", "norm_w", "conv_b", "ssm_norm_w", "q_norm_w", "k_norm_w", "sinks", "dt_bias", "a_log", "d_skip",
         "rel_bias")
WEIGHTS = ("w_ada", "b_ada", "norm_w", "w_in", "q_norm_w", "k_norm_w", "rel_bias", "sinks", "conv_w", "conv_b",
           "dt_bias", "a_log", "d_skip", "ssm_norm_w", "w_attn_proj", "w_ssm_proj", "w_out")
IN_COLS = ((0, 1024, C_Q), (1024, 256, C_K), (1280, 256, C_V), (1536, 1024, C_ZA), (2560, 2048, C_ZM),
           (4608, 3072, C_XBC), (7680, 32, C_DT), (7712, 1024, C_GA), (8736, 1024, C_GB))


def _to_cat(shards):
    parts, pos = [], 0
    for o, n, cnew in sorted(IN_COLS, key=lambda e: e[2]):
        assert cnew == pos
        c0 = o
        while c0 < o + n:
            i = c0 // SH_IN
            c1 = min(o + n, (i + 1) * SH_IN)
            parts.append(shards[i][:, c0 - i * SH_IN:c1 - i * SH_IN])
            c0 = c1
        pos += n
    parts.append(jnp.zeros((D, NP - pos), shards.dtype))
    return jnp.concatenate(parts, axis=1)


def _from_cat(dw_pieces):
    starts = [subs[0][1] for _, subs in DPIECES]

    def cols(c0, c1):
        p = max(q for q in range(len(starts)) if starts[q] <= c0)
        return dw_pieces[p][:, c0 - starts[p]:c1 - starts[p]]

    shards = []
    for i in range(4):
        lo, hi = i * SH_IN, (i + 1) * SH_IN
        parts = []
        for o, n, cnew in IN_COLS:
            a, b = max(o, lo), min(o + n, hi)
            if a < b:
                parts.append(cols(cnew + a - o, cnew + b - o))
        shards.append(jnp.concatenate(parts, axis=1))
    return jnp.stack(shards)


def kernel(x, c, w_ada, b_ada, norm_w, w_in, q_norm_w, k_norm_w, rel_bias, sinks, conv_w, conv_b, dt_bias, a_log, d_skip, ssm_norm_w, w_attn_proj, w_ssm_proj, w_out, loss_target, m_w_ada, m_b_ada, m_norm_w, m_w_in, m_q_norm_w, m_k_norm_w, m_rel_bias, m_sinks, m_conv_w, m_conv_b, m_dt_bias, m_a_log, m_d_skip, m_ssm_norm_w, m_w_attn_proj, m_w_ssm_proj, m_w_out, v_w_ada, v_b_ada, v_norm_w, v_w_in, v_q_norm_w, v_k_norm_w, v_rel_bias, v_sinks, v_conv_w, v_conv_b, v_dt_bias, v_a_log, v_d_skip, v_ssm_norm_w, v_w_attn_proj, v_w_ssm_proj, v_w_out):
    args = dict(locals())
    xi, yi, ci = lax.axis_index("x"), lax.axis_index("y"), lax.axis_index("c")
    chip = 2 * xi + yi
    me = 4 * xi + 2 * yi + ci
    x2 = x[0]
    tgt = loss_target[0]

    pay = _rows([c.reshape(-1), conv_w[0].reshape(-1)], 8)
    g0 = allgather_small(pay, "gather_cond")
    c_all = g0[:, 0, :]
    conv_w_full = g0[0::2, 1:4, :].reshape(4, CONV_K, 768).transpose(1, 0, 2).reshape(CONV_K, XBC)

    b_ada_sh = lax.dynamic_slice(b_ada, (0, chip * 768), (1, 768))
    mod_sh = ada_mod(c_all, w_ada[0], b_ada_sh)

    w_in_b = w_in[0].astype(BF16)
    w_rest_b = jnp.concatenate([w_attn_proj[0], w_ssm_proj[0], w_out[0]], axis=0).astype(BF16)
    wg_in, modg = gather_weights(w_in_b, mod_sh)
    wg_in = lax.dynamic_update_slice(wg_in, w_in_b[None], (chip, 0, 0))
    rs_sem, rr_sem, rest_thru, rest_tok = split_start("gather_rest_start", _rest_copies, [w_rest_b],
                                                      [lax.empty((4, D, D), BF16)], 6, modg)
    mod = lax.dynamic_slice(modg, (0, me, 0), (4, 1, 768)).reshape(1, 3 * D)
    shift, scale, gate = mod[:, :D], mod[:, D:2 * D] + rest_tok[:1, :1], mod[:, 2 * D:]
    wcat = _to_cat(wg_in)

    pad128 = lambda v: jnp.pad(v, ((0, 0), (0, 128 - v.shape[1])))
    dtb_p, alog_p, dsk_p = pad128(dt_bias), pad128(a_log), pad128(d_skip)
    bucket = _bucket_table()

    proj, dt_raw, h_t = norm_proj(x2, norm_w, scale, shift, wcat)
    biasm = bias_expand(rel_bias, sinks, bucket)
    ao = attn_fwd(proj, biasm, q_norm_w, k_norm_w)
    act, dsl = conv_fwd(proj, conv_w_full, conv_b)
    yss, sprev = ssd_fwd(act, dt_raw, dtb_p, alog_p, dsk_p)

    w_rest_b, wg_rest = split_wait("gather_rest_wait", _rest_copies, rs_sem, rr_sem, rest_thru, 1, yss)
    wg_rest = lax.dynamic_update_slice(wg_rest, w_rest_b[None], (chip, 0, 0))
    w_at = wg_rest[:, :R_AT].reshape(D, D)
    w_ss = wg_rest[:, R_AT:R_AT + R_SS].reshape(SSM_W, D)
    w_ou = wg_rest[:, R_AT + R_SS:].reshape(D, D)
    (loss_p, dy, dao, dmid, dyss, ua_t, yn_t, mg_t, dya, dyb, dout, dgate, dssm_nw) = tail(
        proj, ao, yss, x2, tgt, gate, ssm_norm_w, w_at, w_ss, w_ou)

    dq, dkv, dqw, dkw, dacc = attn_bwd(proj, dao, biasm, q_norm_w, k_norm_w)
    dbias = bias_reduce(dacc, bucket)
    drb = dbias[:, :NBUCKET].T
    dsk = dbias[:, NBUCKET].reshape(1, HQ)
    dact, ddt, ddtb, dalog, ddskip = ssd_bwd(act, dt_raw, dyss, sprev, dtb_p, alog_p, dsk_p)
    dxbc, dconv_w, dconv_b = conv_bwd(proj, dact, dsl, conv_w_full)

    dproj = (dq, dmid, dxbc, dkv, ddt)
    dwcat = [wgrad(h_t, piece, "dw_in_%d" % p, min(piece.shape[1], 1024), rest_tok) for p, piece in enumerate(dproj)]

    g_in = _from_cat(dwcat)
    ps_sem, pr_sem, pair_thru, pair_tok = split_start("pair_in_start", _pair_copies, [g_in],
                                                      [lax.empty((4, HROWS, SH_IN), F32)], 1, loss_p)
    dw_at = wgrad(ua_t, dya, "dw_attn", 512, pair_tok)
    dw_ss = wgrad(yn_t, dyb, "dw_ssm", 512, pair_tok)
    dw_ou = wgrad(mg_t, dout, "dw_out", 512, pair_tok)
    g_rest = jnp.concatenate([dw_at.reshape(4, R_AT, D), dw_ss.reshape(4, R_SS, D), dw_ou.reshape(4, R_OU, D)], axis=1)
    sib_rest = pair_exchange(g_rest)
    g_in, sib_in = split_wait("pair_in_wait", _pair_copies, ps_sem, pr_sem, pair_thru, 1, sib_rest)
    part_in, pb_in = pair_sum(g_in, ci, sib_in, "pair_sum_in")
    part_rest, pb_rest = pair_sum(g_rest, ci, sib_rest, "pair_sum_rest")
    cs_sem, cr_sem, chip_thru, token = split_start(
        "chip_exchange_start", _chip_copies, [pb_in, pb_rest],
        [lax.empty((3, HROWS, SH_IN), BF16), lax.empty((3, HROWS, D), BF16)], 6, part_rest)
    grad_x, dnorm_w, dscale, dshift = dproj_bwd(dproj, wcat, x2, dy, norm_w, scale + token[:1, :1])
    _, _, oth_in, oth_rest = split_wait("chip_exchange_wait", _chip_copies, cs_sem, cr_sem, chip_thru, 2, dshift)
    red_in = chip_sum(part_in, chip, oth_in, "chip_sum_in")
    red_rest = chip_sum(part_rest, chip, oth_rest, "chip_sum_rest")
    sw_ssem, sw_rsem, swap_thru, swap_tok = split_start(
        "pair_swap_start", _swap_copies, [red_in, red_rest],
        [lax.empty((HROWS, SH_IN), F32), lax.empty((HROWS, D), F32)], 2, red_rest)

    dmod = jnp.concatenate([dshift, dscale, dgate], axis=1)
    gsmall = _pack_small(dmod, dnorm_w, dconv_b, dssm_nw, dqw, dkw, dsk[:, :HQ], ddtb[:, :SH], dalog[:, :SH],
                         ddskip[:, :SH], drb, extra=loss_p[:, :1] + swap_tok[:1, :1], tail=(dconv_w,), rows=32)
    gall = allgather_small(gsmall, "gather_small_grads")
    ssum = sum_devices(gall)
    gs = _unpack_small(ssum[:16])
    loss = gs["extra"]
    dconv_w_sh = lax.dynamic_slice(ssum[16:28].reshape(CONV_K, XBC), (0, chip * 768), (CONV_K, 768))
    dmod_all = gall[:, 0:3, :].reshape(8, 3 * D)
    dw_ada = ada_grad(c_all, lax.dynamic_slice(dmod_all, (0, chip * 768), (8, 768)))

    grads = dict(gs)
    grads["w_ada"] = dw_ada
    grads["conv_w"] = dconv_w_sh

    delta, new_m, new_v = {}, {}, {}

    def step(n):
        delta[n], new_m[n], new_v[n] = adamw(args[n][0], grads[n], args["m_" + n][0], args["v_" + n][0], "adamw_" + n)

    step("w_ada")
    step("conv_w")
    ws = _pack_small(*[args[n] for n in SMALL])
    ms = _pack_small(*[args["m_" + n] for n in SMALL])
    vs = _pack_small(*[args["v_" + n] for n in SMALL])
    d_s, m_s, v_s = adamw(ws, ssum[:16], ms, vs, "adamw_small")
    red_in, red_rest, recv_in, recv_rest = split_wait("pair_swap_wait", _swap_copies, sw_ssem, sw_rsem, swap_thru, 2, d_s)
    d_s, m_s, v_s = _unpack_small(d_s), _unpack_small(m_s), _unpack_small(v_s)
    for n in SMALL:
        delta[n], new_m[n], new_v[n] = d_s[n], m_s[n], v_s[n]
    grads["w_in"], delta["w_in"], new_m["w_in"], new_v["w_in"] = adamw_halves(
        w_in[0], red_in, recv_in, ci, m_w_in[0], v_w_in[0], "adamw_w_in")
    g_shard_rest = jnp.concatenate([jnp.where(ci == 0, red_rest, recv_rest), jnp.where(ci == 0, recv_rest, red_rest)],
                                   axis=0)
    grads["w_attn_proj"] = g_shard_rest[:R_AT]
    grads["w_ssm_proj"] = g_shard_rest[R_AT:R_AT + R_SS]
    grads["w_out"] = g_shard_rest[R_AT + R_SS:]
    for n in ("w_attn_proj", "w_ssm_proj", "w_out"):
        step(n)

    def shaped(n, a):
        return a.reshape(args[n].shape)

    outs = [loss, grad_x[None]]
    for table in (grads, delta, new_m, new_v):
        outs += [shaped(n, table[n]) for n in WEIGHTS]
    return tuple(outs)
```

```python
import functools
import math

import numpy as np
import jax
import jax.numpy as jnp
from jax import lax
from jax.experimental import pallas as pl
from jax.experimental.pallas import tpu as pltpu

F32 = jnp.float32
BF16 = jnp.bfloat16
MESH = pl.DeviceIdType.MESH

D = 1024
HQ, HKV, GRP, DH = 16, 4, 4, 64
BLK = 128
NBUCKET, MAXDIST = 32, 128
SSM_W, SH, SG, SR, SP, SN = 2048, 32, 4, 8, 64, 128
CONV_K = 4
XBC = SSM_W + 2 * SG * SN
IN_W = 9760
EPS = 1e-6
NEG = -1e30
SCALE = DH ** -0.5

C_Q, C_ZA, C_GA, C_GB, C_ZM, C_XBC, C_K, C_V, C_DT = 0, 1024, 2048, 3072, 4096, 6144, 9216, 9472, 9728
NP = 9984
TN = 1664
W_MID = C_XBC - C_ZA

SH_IN = IN_W // 4
R_AT, R_SS, R_OU = 256, 512, 256
HROWS = D // 2

ADAM_LR, ADAM_B1, ADAM_B2, ADAM_EPS, ADAM_WD, ADAM_STEP = 0.001, 0.9, 0.999, 1e-08, 0.01, 10

VMEM_LIMIT = 56 * 1024 * 1024


def _cp(sem=None):
    if sem is None:
        return pltpu.CompilerParams(vmem_limit_bytes=VMEM_LIMIT)
    return pltpu.CompilerParams(dimension_semantics=sem, vmem_limit_bytes=VMEM_LIMIT)


def _sig(x):
    return 0.5 * jnp.tanh(0.5 * x) + 0.5


def _dot(a, b):
    return jnp.dot(a, b, preferred_element_type=F32)


def _dot_nt(a, b):
    return lax.dot_general(a, b, (((1,), (1,)), ((), ())), preferred_element_type=F32)


def _dot_tn(a, b):
    return lax.dot_general(a, b, (((0,), (0,)), ((), ())), preferred_element_type=F32)


def _rsum(x):
    return jnp.sum(x, axis=-1, keepdims=True)


def _csum(x):
    return jnp.sum(x, axis=0, keepdims=True)


def _asum(x):
    return _csum(_rsum(x))


def _full(shape):
    nd = len(shape)
    return pl.BlockSpec(shape, lambda *_: (0,) * nd)


def ada_mod(c_all, w_ada_sh, b_ada_sh):
    def body(c_ref, w_ref, b_ref, o_ref):
        cv = c_ref[...]
        s = cv * _sig(cv)
        o_ref[...] = jnp.dot(s, w_ref[...], preferred_element_type=F32,
                             precision=lax.Precision.HIGHEST) + b_ref[...]

    n = w_ada_sh.shape[1]
    return pl.pallas_call(body, name="ada_mod", out_shape=jax.ShapeDtypeStruct((8, n), F32),
                          compiler_params=_cp())(c_all, w_ada_sh, b_ada_sh)


def ada_grad(c_all, dmod_sh):
    def body(c_ref, d_ref, o_ref):
        cv = c_ref[...]
        s = cv * _sig(cv)
        o_ref[...] = lax.dot_general(s, d_ref[...], (((0,), (0,)), ((), ())), preferred_element_type=F32,
                                     precision=lax.Precision.HIGHEST)

    n = dmod_sh.shape[1]
    return pl.pallas_call(body, name="ada_grad", out_shape=jax.ShapeDtypeStruct((D, n), F32),
                          compiler_params=_cp())(c_all, dmod_sh)


def norm_proj(x, norm_w, scale, shift, wcat):
    t = x.shape[0]
    tm = min(t, 1024)

    def body(x_ref, nw_ref, sc_ref, sh_ref, w_ref, p_ref, dt_ref, ht_ref, hs):
        @pl.when(pl.program_id(1) == 0)
        def _():
            xv = x_ref[...]
            r = lax.rsqrt(jnp.mean(xv * xv, axis=-1, keepdims=True) + EPS)
            h = (xv * r) * nw_ref[...]
            h = h * (1.0 + sc_ref[...]) + sh_ref[...]
            hs[...] = h.astype(BF16)
            ht_ref[...] = h.T.astype(BF16)

        p = _dot(hs[...], w_ref[...])
        p_ref[...] = p.astype(BF16)

        @pl.when(pl.program_id(1) == C_DT // TN)
        def _():
            dt_ref[...] = p[:, C_DT % TN:C_DT % TN + 128]

    vec = pl.BlockSpec((1, D), lambda i, j: (0, 0))
    return pl.pallas_call(
        body, name="norm_proj", grid=(t // tm, NP // TN),
        in_specs=[pl.BlockSpec((tm, D), lambda i, j: (i, 0)), vec, vec, vec,
                  pl.BlockSpec((D, TN), lambda i, j: (0, j))],
        out_specs=[pl.BlockSpec((tm, TN), lambda i, j: (i, j)), pl.BlockSpec((tm, 128), lambda i, j: (i, 0)),
                   pl.BlockSpec((D, tm), lambda i, j: (0, i))],
        out_shape=[jax.ShapeDtypeStruct((t, NP), BF16), jax.ShapeDtypeStruct((t, 128), F32),
                   jax.ShapeDtypeStruct((D, t), BF16)],
        scratch_shapes=[pltpu.VMEM((tm, D), BF16)],
        compiler_params=_cp(("parallel", "arbitrary")),
    )(x, norm_w, scale, shift, wcat)


def _bucket_table():
    qi = np.arange(BLK)[:, None]
    kj = np.arange(2 * BLK)[None, :]
    dist = qi + BLK - kj
    n = np.maximum(dist, 0)
    max_exact = NBUCKET // 2
    nf = np.maximum(n, 1).astype(np.float32)
    large = max_exact + (np.log(nf / np.float32(max_exact)) / np.float32(math.log(MAXDIST / max_exact))
                         * np.float32(NBUCKET - max_exact)).astype(np.int32)
    large = np.minimum(large, NBUCKET - 1)
    bucket = np.where(n < max_exact, n, large).astype(np.int32)
    valid = (dist >= 0) & (dist < BLK)
    return np.where(valid, bucket, -1).astype(np.int32)


def bias_expand(rel_bias, sinks, bucket):
    def body(rb_ref, sk_ref, bk_ref, o_ref):
        bk = bk_ref[...]
        col = lax.broadcasted_iota(jnp.int32, (BLK, 2 * BLK), 1)

        def head(hd, carry):
            def step(b, acc):
                return jnp.where(bk == b, rb_ref[b, hd], acc)

            acc = lax.fori_loop(0, NBUCKET, step, jnp.full((BLK, 2 * BLK), NEG, F32))
            acc = jnp.where(col == 0, sk_ref[0, hd], acc)
            o_ref[1, hd] = acc
            o_ref[0, hd] = jnp.where(jnp.logical_and(col > 0, col < BLK), NEG, acc)
            return carry

        lax.fori_loop(0, HQ, head, 0)

    smem = pl.BlockSpec(memory_space=pltpu.SMEM)
    return pl.pallas_call(
        body, name="bias_expand", in_specs=[smem, smem, VM], out_specs=VM,
        out_shape=jax.ShapeDtypeStruct((2, HQ, BLK, 2 * BLK), F32), compiler_params=_cp(),
    )(rel_bias, sinks, jnp.asarray(bucket))


def bias_reduce(dacc, bucket):
    col = np.arange(BLK * 2 * BLK) % (2 * BLK)
    lane = np.arange(128)[None, :]
    member = (bucket.reshape(-1)[:, None] == lane) | ((col[:, None] == 0) & (lane == NBUCKET))

    def body(d_ref, m_ref, o_ref):
        mm = m_ref[...]
        o_ref[...] = sum(_dot(part, mm) for part in _split3(d_ref[...]))

    return pl.pallas_call(body, name="bias_reduce", out_shape=jax.ShapeDtypeStruct((HQ, 128), F32),
                          compiler_params=_cp())(dacc.reshape(HQ, BLK * 2 * BLK), jnp.asarray(member, BF16))


GQ = GRP * BLK


def _stack_heads(x, nh):
    return jnp.concatenate([x[:, DH * h:DH * (h + 1)] for h in range(nh)], axis=0)


def _unstack(xs, nh):
    rows = xs.shape[0] // nh
    return jnp.concatenate([xs[rows * h:rows * (h + 1)] for h in range(nh)], axis=1)


def _rms(x):
    return lax.rsqrt(jnp.mean(x * x, axis=-1, keepdims=True) + EPS)


def _stack_q(q, qw):
    qs = _stack_heads(q, HQ)
    r = _rms(qs)
    qhat = qs * r
    return qhat * qw, qhat, r


def _band_first(shape):
    return (lax.broadcasted_iota(jnp.int32, shape, 0) & (2 * BLK - 1)) == 0


def _stack_kv(kp, kc, vp, vc, kw):
    ks = _stack_heads(jnp.concatenate([kp, kc], axis=0), HKV)
    r = _rms(ks)
    khat = ks * r
    first = _band_first(ks.shape)
    kn = jnp.where(first, 0.0, khat * kw)
    v2 = jnp.where(first, 0.0, _stack_heads(jnp.concatenate([vp, vc], axis=0), HKV)).astype(BF16)
    return kn, khat, r, v2


def _softmax_rows(s):
    p = jnp.exp(s - jnp.max(s, axis=-1, keepdims=True))
    return p * (1.0 / _rsum(p))


def attn_fwd(proj, biasm, q_norm_w, k_norm_w):
    t = proj.shape[0]
    nb = t // BLK

    def body(q_ref, kc_ref, kp_ref, vc_ref, vp_ref, bm_ref, qw_ref, kw_ref, o_ref):
        f = lambda ref: ref[...].astype(F32)
        qn = _stack_q(f(q_ref), qw_ref[...])[0].astype(BF16)
        kn, _, _, v2 = _stack_kv(f(kp_ref), f(kc_ref), f(vp_ref), f(vc_ref), kw_ref[...])
        knb = kn.astype(BF16)
        s = jnp.concatenate([_dot_nt(qn[GQ * j:GQ * (j + 1)], knb[2 * BLK * j:2 * BLK * (j + 1)])
                             for j in range(HKV)], axis=0)
        pr = _softmax_rows(s * SCALE + bm_ref[0].reshape(HQ * BLK, 2 * BLK)).astype(BF16)
        o = jnp.concatenate([_dot(pr[GQ * j:GQ * (j + 1)], v2[2 * BLK * j:2 * BLK * (j + 1)])
                             for j in range(HKV)], axis=0)
        o_ref[...] = _unstack(o, HQ).astype(BF16)

    kblk, vblk = C_K // 256, C_V // 256
    prev = lambda n: jnp.maximum(n - 1, 0)
    return pl.pallas_call(
        body, name="attn_fwd", grid=(nb,),
        in_specs=[pl.BlockSpec((BLK, D), lambda n: (n, 0)),
                  pl.BlockSpec((BLK, 256), lambda n: (n, kblk)),
                  pl.BlockSpec((BLK, 256), lambda n: (prev(n), kblk)),
                  pl.BlockSpec((BLK, 256), lambda n: (n, vblk)),
                  pl.BlockSpec((BLK, 256), lambda n: (prev(n), vblk)),
                  pl.BlockSpec((1, HQ, BLK, 2 * BLK), lambda n: (jnp.minimum(n, 1), 0, 0, 0)),
                  _full((1, DH)), _full((1, DH))],
        out_specs=pl.BlockSpec((BLK, D), lambda n: (n, 0)),
        out_shape=jax.ShapeDtypeStruct((t, D), BF16),
        compiler_params=_cp(("parallel",)),
    )(proj, proj, proj, proj, proj, biasm, q_norm_w, k_norm_w)


def attn_bwd(proj, dao, biasm, q_norm_w, k_norm_w):
    t = proj.shape[0]
    nb = t // BLK
    kb = 2 * BLK

    def body(q_ref, kc_ref, kp_ref, vc_ref, vp_ref, do_ref, bm_ref, qw_ref, kw_ref,
             dq_ref, dkv_ref, dqw_ref, dkw_ref, dacc_ref, ck, cv, pk, pv, nk, nv):
        n = pl.program_id(0)

        @pl.when(n == 0)
        def _():
            for ref in (dqw_ref, dkw_ref, dacc_ref, ck, cv):
                ref[...] = jnp.zeros_like(ref)

        qw = qw_ref[...]
        kw = kw_ref[...]
        f = lambda ref: ref[...].astype(F32)
        kn, khat, rk, v2 = _stack_kv(f(kp_ref), f(kc_ref), f(vp_ref), f(vc_ref), kw)
        grp = lambda a, j: a[GQ * j:GQ * (j + 1)]
        band = lambda a, j: a[kb * j:kb * (j + 1)]

        @pl.when(n < nb)
        def _():
            qn, qhat, rq = _stack_q(f(q_ref), qw)
            qnb = qn.astype(BF16)
            knb = kn.astype(BF16)
            dos = _stack_heads(f(do_ref), HQ).astype(BF16)
            s = jnp.concatenate([_dot_nt(grp(qnb, j), band(knb, j)) for j in range(HKV)], axis=0)
            pr = _softmax_rows(s * SCALE + bm_ref[0].reshape(HQ * BLK, kb))
            dp = jnp.concatenate([_dot_nt(grp(dos, j), band(v2, j)) for j in range(HKV)], axis=0)
            ds = pr * (dp - _rsum(pr * dp))
            dacc_ref[...] += ds.reshape(HQ, BLK, kb)
            dsb = ds.astype(BF16)
            prb = pr.astype(BF16)
            dqn = jnp.concatenate([_dot(grp(dsb, j), band(knb, j)) for j in range(HKV)], axis=0) * SCALE
            dqhat = dqn * qw
            dq = rq * (dqhat - qhat * jnp.mean(dqhat * qhat, axis=-1, keepdims=True))
            dq_ref[...] = _unstack(dq, HQ).astype(BF16)
            dqw_ref[...] += _csum(dqn * qhat)
            first = _band_first((kb, DH))
            for j in range(HKV):
                rows = slice(BLK * j, BLK * (j + 1))
                dkn = jnp.where(first, 0.0, _dot_tn(grp(dsb, j), grp(qnb, j)) * SCALE)
                dvj = jnp.where(first, 0.0, _dot_tn(grp(prb, j), grp(dos, j)))
                pk[rows, :] = dkn[:BLK]
                nk[rows, :] = dkn[BLK:]
                pv[rows, :] = dvj[:BLK]
                nv[rows, :] = dvj[BLK:]

        @pl.when(n == nb)
        def _():
            for ref in (pk, pv, nk, nv):
                ref[...] = jnp.zeros_like(ref)

        khp = jnp.concatenate([khat[kb * j:kb * j + BLK] for j in range(HKV)], axis=0)
        rkp = jnp.concatenate([rk[kb * j:kb * j + BLK] for j in range(HKV)], axis=0)
        dkn = ck[...] + pk[...]
        dkhat = dkn * kw
        dk = rkp * (dkhat - khp * jnp.mean(dkhat * khp, axis=-1, keepdims=True))
        dkw_ref[...] += _csum(dkn * khp)
        dkv_ref[...] = jnp.concatenate([_unstack(dk, HKV), _unstack(cv[...] + pv[...], HKV)], axis=1).astype(BF16)
        ck[...] = nk[...]
        cv[...] = nv[...]

    kblk, vblk = C_K // 256, C_V // 256
    cur = lambda n: jnp.minimum(n, nb - 1)
    prev = lambda n: jnp.maximum(n - 1, 0)
    carry = pltpu.VMEM((HKV * BLK, DH), F32)
    return pl.pallas_call(
        body, name="attn_bwd", grid=(nb + 1,),
        in_specs=[pl.BlockSpec((BLK, D), lambda n: (cur(n), 0)),
                  pl.BlockSpec((BLK, 256), lambda n: (cur(n), kblk)), pl.BlockSpec((BLK, 256), lambda n: (prev(n), kblk)),
                  pl.BlockSpec((BLK, 256), lambda n: (cur(n), vblk)), pl.BlockSpec((BLK, 256), lambda n: (prev(n), vblk)),
                  pl.BlockSpec((BLK, D), lambda n: (cur(n), 0)),
                  pl.BlockSpec((1, HQ, BLK, kb), lambda n: (jnp.minimum(n, 1), 0, 0, 0)),
                  _full((1, DH)), _full((1, DH))],
        out_specs=[pl.BlockSpec((BLK, D), lambda n: (cur(n), 0)),
                   pl.BlockSpec((BLK, 512), lambda n: (prev(n), 0)),
                   _full((1, DH)), _full((1, DH)), _full((HQ, BLK, kb))],
        out_shape=[jax.ShapeDtypeStruct((t, D), BF16), jax.ShapeDtypeStruct((t, 512), BF16),
                   jax.ShapeDtypeStruct((1, DH), F32),
                   jax.ShapeDtypeStruct((1, DH), F32), jax.ShapeDtypeStruct((HQ, BLK, kb), F32)],
        scratch_shapes=[carry] * 6,
        compiler_params=_cp(("arbitrary",)),
    )(proj, proj, proj, proj, proj, dao, biasm, q_norm_w, k_norm_w)


CONV_TM, CONV_CW, CONV_RC, HALO = 512, 1024, 32, 16


def conv_fwd(proj, conv_w, conv_b):
    t = proj.shape[0]
    tm = min(t, CONV_TM)
    c0 = C_XBC // CONV_CW

    def body(x_ref, xp_ref, w_ref, b_ref, o_ref, ds_ref):
        i = pl.program_id(1)
        w = w_ref[...]
        b = b_ref[...]
        for r in range(tm // CONV_RC):
            lo = r * CONV_RC
            if r == 0:
                head = jnp.where(i == 0, 0.0, xp_ref[...].astype(F32))
                win = jnp.concatenate([head, x_ref[0:CONV_RC, :].astype(F32)], axis=0)
            else:
                win = x_ref[lo - HALO:lo + CONV_RC, :].astype(F32)
            acc = b
            for j in range(CONV_K):
                acc = acc + w[j:j + 1] * win[HALO - 3 + j:HALO - 3 + j + CONV_RC]
            sg = _sig(acc)
            o_ref[lo:lo + CONV_RC, :] = acc * sg
            ds_ref[lo:lo + CONV_RC, :] = _dsilu(acc, sg).astype(BF16)

    rh = tm // HALO
    tile = pl.BlockSpec((tm, CONV_CW), lambda s, i: (i, s))
    return pl.pallas_call(
        body, name="conv_fwd", grid=(XBC // CONV_CW, t // tm),
        in_specs=[pl.BlockSpec((tm, CONV_CW), lambda s, i: (i, c0 + s)),
                  pl.BlockSpec((HALO, CONV_CW), lambda s, i: (jnp.maximum(i * rh - 1, 0), c0 + s)),
                  pl.BlockSpec((CONV_K, CONV_CW), lambda s, i: (0, s)), pl.BlockSpec((1, CONV_CW), lambda s, i: (0, s))],
        out_specs=[tile, tile],
        out_shape=[jax.ShapeDtypeStruct((t, XBC), F32), jax.ShapeDtypeStruct((t, XBC), BF16)],
        compiler_params=_cp(("parallel", "parallel")),
    )(proj, proj, conv_w, conv_b)


def conv_bwd(proj, dact, dsl, conv_w):
    t = proj.shape[0]
    tm = min(t, CONV_TM)
    nt = t // tm
    nr = tm // CONV_RC
    c0 = C_XBC // CONV_CW
    ext = CONV_RC + 8

    def body(x_ref, xp_ref, d_ref, dn_ref, s_ref, sn_ref, w_ref, dx_ref, dw_ref, db_ref):
        i = pl.program_id(1)

        @pl.when(i == 0)
        def _():
            dw_ref[...] = jnp.zeros_like(dw_ref)
            db_ref[...] = jnp.zeros_like(db_ref)

        w = w_ref[...]
        dws = [jnp.zeros((1, CONV_CW), F32) for _ in range(CONV_K)]
        db = jnp.zeros((1, CONV_CW), F32)
        for r in range(nr):
            lo = r * CONV_RC
            if r == 0:
                head = jnp.where(i == 0, 0.0, xp_ref[...].astype(F32))
                win = jnp.concatenate([head, x_ref[0:CONV_RC, :].astype(F32)], axis=0)
            else:
                win = x_ref[lo - HALO:lo + CONV_RC, :].astype(F32)
            if r < nr - 1:
                dext = d_ref[lo:lo + ext, :]
                sext = s_ref[lo:lo + CONV_RC + HALO, :].astype(F32)[0:ext]
            else:
                dext = jnp.concatenate([d_ref[lo:lo + CONV_RC, :], jnp.where(i == nt - 1, 0.0, dn_ref[...])], axis=0)
                sext = jnp.concatenate([s_ref[lo:lo + CONV_RC, :].astype(F32), sn_ref[...].astype(F32)], axis=0)[0:ext]
            dpre = dext * sext
            dx = jnp.zeros((CONV_RC, CONV_CW), F32)
            own = dpre[0:CONV_RC]
            for j in range(CONV_K):
                dx = dx + w[j:j + 1] * dpre[3 - j:3 - j + CONV_RC]
                dws[j] = dws[j] + _csum(own * win[HALO - 3 + j:HALO - 3 + j + CONV_RC])
            db = db + _csum(own)
            dx_ref[lo:lo + CONV_RC, :] = dx.astype(BF16)
        dw_ref[...] += jnp.concatenate(dws, axis=0)
        db_ref[...] += db

    rh = tm // HALO
    r8 = tm // 8
    nxt = lambda i, per: jnp.minimum((i + 1) * per, nt * per - 1)
    return pl.pallas_call(
        body, name="conv_bwd", grid=(XBC // CONV_CW, nt),
        in_specs=[pl.BlockSpec((tm, CONV_CW), lambda s, i: (i, c0 + s)),
                  pl.BlockSpec((HALO, CONV_CW), lambda s, i: (jnp.maximum(i * rh - 1, 0), c0 + s)),
                  pl.BlockSpec((tm, CONV_CW), lambda s, i: (i, s)),
                  pl.BlockSpec((8, CONV_CW), lambda s, i: (nxt(i, r8), s)),
                  pl.BlockSpec((tm, CONV_CW), lambda s, i: (i, s)),
                  pl.BlockSpec((HALO, CONV_CW), lambda s, i: (nxt(i, rh), s)),
                  pl.BlockSpec((CONV_K, CONV_CW), lambda s, i: (0, s))],
        out_specs=[pl.BlockSpec((tm, CONV_CW), lambda s, i: (i, s)),
                   pl.BlockSpec((CONV_K, CONV_CW), lambda s, i: (0, s)), pl.BlockSpec((1, CONV_CW), lambda s, i: (0, s))],
        out_shape=[jax.ShapeDtypeStruct((t, XBC), BF16), jax.ShapeDtypeStruct((CONV_K, XBC), F32),
                   jax.ShapeDtypeStruct((1, XBC), F32)],
        compiler_params=_cp(("parallel", "arbitrary")),
    )(proj, proj, dact, dact, dsl, dsl, conv_w)


def _split3(x):
    h = x.astype(BF16)
    r = x - h.astype(F32)
    m = r.astype(BF16)
    lo = (r - m.astype(F32)).astype(BF16)
    return h, m, lo


def _tri_mm(tri, x):
    h, m, lo = _split3(x)
    return _dot(tri, h) + _dot(tri, m) + _dot(tri, lo)


def _softplus(x):
    return jnp.maximum(x, 0.0) + jnp.log1p(jnp.exp(-jnp.abs(x)))


def _chunk_decays(dt_raw, dtb, alog):
    dtv = _softplus(dt_raw + dtb)
    a = -jnp.exp(alog)
    ri = lax.broadcasted_iota(jnp.int32, (BLK, BLK), 0)
    ci = lax.broadcasted_iota(jnp.int32, (BLK, BLK), 1)
    causal = ri >= ci
    acum = _tri_mm(causal.astype(BF16), dtv * a)
    return dtv, a, causal, acum, acum.T


NPAIR = SH // 2


def _pairs(x):
    return jnp.stack([x[:, 128 * k:128 * (k + 1)] for k in range(NPAIR)])


def _unpairs(x3):
    return jnp.concatenate([x3[k] for k in range(NPAIR)], axis=1)


def _per_head_cols(m):
    return jnp.stack([jnp.broadcast_to(m[:, h:h + 1], m.shape) for h in range(SH)])


def _pair_lanes(t):
    r = t.reshape(NPAIR, 2, t.shape[1], 128)
    lo = lax.broadcasted_iota(jnp.int32, (1, t.shape[1], 128), 2) < SP
    return jnp.where(lo, r[:, 0], r[:, 1])


class _Chunk:
    pass


def _chunk_common(dt_raw, dtb, alog, dskip):
    cm = _Chunk()
    cm.dtv, cm.a, cm.causal, acum, acum_t = _chunk_decays(dt_raw, dtb, alog)
    cm.acol = _per_head_cols(acum)
    cm.arow = jnp.stack([acum_t[h:h + 1, :] for h in range(SH)])
    apl = _pair_lanes(cm.acol)
    alast = apl[:, BLK - 1:BLK, :]
    cm.dpl = _pair_lanes(_per_head_cols(cm.dtv))
    cm.eapl = jnp.exp(apl)
    cm.epl = jnp.exp(alast - apl)
    cm.cdpl = jnp.exp(alast)
    cm.dskpl = _pair_lanes(_per_head_cols(dskip))
    cm.lo = lax.broadcasted_iota(jnp.int32, (1, BLK, 128), 2) < SP
    return cm


def ssd_fwd(act, dt_raw, dtb_p, alog_p, dsk_p):
    t = act.shape[0]
    nc = t // BLK

    def body(xs_ref, b_ref, c_ref, dt_ref, dtb_ref, al_ref, dk_ref, y_ref, sp_ref, st):
        c = pl.program_id(0)

        @pl.when(c == 0)
        def _():
            st[...] = jnp.zeros_like(st)

        s_t = st[...]
        sp_ref[0] = s_t
        cm = _chunk_common(dt_ref[...], dtb_ref[...], al_ref[...], dk_ref[...])
        gms, cbs, bts = [], [], []
        for g in range(SG):
            bf = b_ref[:, SN * g:SN * (g + 1)]
            cb = c_ref[:, SN * g:SN * (g + 1)].astype(BF16)
            gms.append(_dot_nt(cb, bf.astype(BF16)))
            cbs.append(cb)
            bts.append(bf.T.astype(BF16))
        lam = jnp.exp(jnp.where(cm.causal[None], cm.acol - cm.arow, NEG))
        m = (lam.reshape(SG, SR, BLK, BLK) * jnp.stack(gms)[:, None]).reshape(SH, BLK, BLK).astype(BF16)
        xs16 = _pairs(xs_ref[...])
        xdt16 = xs16 * cm.dpl
        x_lo = jnp.where(cm.lo, xdt16, 0.0).astype(BF16)
        x_hi = jnp.where(cm.lo, 0.0, xdt16).astype(BF16)
        s16 = _pairs(s_t)
        s16b = s16.astype(BF16)
        yd = jnp.stack([_dot(m[2 * k], x_lo[k]) + _dot(m[2 * k + 1], x_hi[k]) for k in range(NPAIR)])
        yo = jnp.stack([_dot(cbs[k // (NPAIR // SG)], s16b[k]) for k in range(NPAIR)])
        y_ref[...] = _unpairs(yd + yo * cm.eapl + cm.dskpl * xs16).astype(BF16)
        xe = (xdt16 * cm.epl).astype(BF16)
        st[...] = _unpairs(cm.cdpl * s16 + jnp.stack([_dot(bts[k // (NPAIR // SG)], xe[k]) for k in range(NPAIR)]))

    vec = _full((1, 128))
    return pl.pallas_call(
        body, name="ssd_fwd", grid=(nc,),
        in_specs=[pl.BlockSpec((BLK, SSM_W), lambda c: (c, 0)),
                  pl.BlockSpec((BLK, SG * SN), lambda c: (c, SSM_W // (SG * SN))),
                  pl.BlockSpec((BLK, SG * SN), lambda c: (c, SSM_W // (SG * SN) + 1)),
                  pl.BlockSpec((BLK, 128), lambda c: (c, 0)), vec, vec, vec],
        out_specs=[pl.BlockSpec((BLK, SSM_W), lambda c: (c, 0)), pl.BlockSpec((1, SN, SSM_W), lambda c: (c, 0, 0))],
        out_shape=[jax.ShapeDtypeStruct((t, SSM_W), BF16), jax.ShapeDtypeStruct((nc, SN, SSM_W), F32)],
        scratch_shapes=[pltpu.VMEM((SN, SSM_W), F32)],
        compiler_params=_cp(("arbitrary",)),
    )(act, act, act, dt_raw, dtb_p, alog_p, dsk_p)


def _head_sums(q):
    r = q.shape[1]
    lo = lax.broadcasted_iota(jnp.int32, (1, r, 128), 2) < SP
    s_lo = jnp.sum(jnp.where(lo, q, 0.0), axis=-1, keepdims=True)
    s_hi = jnp.sum(jnp.where(lo, 0.0, q), axis=-1, keepdims=True)
    lane = lax.broadcasted_iota(jnp.int32, (r, 128), 1)
    out = jnp.zeros((r, 128), F32)
    for k in range(NPAIR):
        out = jnp.where(lane == 2 * k, s_lo[k], jnp.where(lane == 2 * k + 1, s_hi[k], out))
    return out


def ssd_bwd(act, dt_raw, dy, sprev, dtb_p, alog_p, dsk_p):
    t = act.shape[0]
    nc = t // BLK

    def body(xs_ref, b_ref, c_ref, dt_ref, dy_ref, sp_ref, dtb_ref, al_ref, dk_ref,
             da_ref, ddt_ref, ddtb_ref, dal_ref, ddk_ref, dst):
        i = pl.program_id(0)

        @pl.when(i == 0)
        def _():
            dst[...] = jnp.zeros_like(dst)
            ddtb_ref[...] = jnp.zeros_like(ddtb_ref)
            dal_ref[...] = jnp.zeros_like(dal_ref)
            ddk_ref[...] = jnp.zeros_like(ddk_ref)

        dt_raw = dt_ref[...]
        dtb = dtb_ref[...]
        cm = _chunk_common(dt_raw, dtb, al_ref[...], dk_ref[...])
        ri = lax.broadcasted_iota(jnp.int32, (BLK, BLK), 0)
        ci = lax.broadcasted_iota(jnp.int32, (BLK, BLK), 1)
        lam_t = jnp.exp(jnp.where((ri <= ci)[None], cm.arow - cm.acol, NEG))
        bbs, cbs, cts, gms = [], [], [], []
        for g in range(SG):
            bf = b_ref[:, SN * g:SN * (g + 1)]
            cf = c_ref[:, SN * g:SN * (g + 1)]
            bbs.append(bf.astype(BF16))
            cbs.append(cf.astype(BF16))
            cts.append(cf.T.astype(BF16))
            gms.append(_dot_nt(bbs[g], cbs[g]))
        grp = lambda k: k // (NPAIR // SG)
        xs16 = _pairs(xs_ref[...])
        dy16 = _pairs(dy_ref[...].astype(F32))
        sp16 = _pairs(sp_ref[0])
        ds16 = _pairs(dst[...])
        xdt16 = xs16 * cm.dpl
        xdtb = xdt16.astype(BF16)
        dyh = [jnp.where(cm.lo, dy16, 0.0).astype(BF16), jnp.where(cm.lo, 0.0, dy16).astype(BF16)]
        m_t = (lam_t.reshape(SG, SR, BLK, BLK) * jnp.stack(gms)[:, None]).reshape(SH, BLK, BLK).astype(BF16)
        dxdt = jnp.stack([_dot(m_t[2 * k], dyh[0][k]) + _dot(m_t[2 * k + 1], dyh[1][k]) for k in range(NPAIR)])
        dm_t = jnp.stack([_dot_nt(xdtb[h // 2], dyh[h % 2][h // 2]) for h in range(SH)])
        dg_t = jnp.sum((dm_t * lam_t).reshape(SG, SR, BLK, BLK), axis=1).astype(BF16)
        xq16 = xdtb.astype(F32)
        xh = [jnp.where(cm.lo, xdt16, 0.0).astype(BF16), jnp.where(cm.lo, 0.0, xdt16).astype(BF16)]
        y_in = jnp.stack([_dot_tn(m_t[2 * k], xh[0][k]) + _dot_tn(m_t[2 * k + 1], xh[1][k]) for k in range(NPAIR)])
        da_diag = dy16 * y_in - xq16 * dxdt
        lane_c = lax.broadcasted_iota(jnp.int32, (BLK, 128), 1)
        ds16b = ds16.astype(BF16)
        sp16b = sp16.astype(BF16)
        dxs = jnp.stack([_dot(bbs[grp(k)], ds16b[k]) for k in range(NPAIR)]) * cm.epl
        dxdt = dxdt + dxs
        dya = (dy16 * cm.eapl).astype(BF16)
        xe = (xdt16 * cm.epl).astype(BF16)
        dcs, dbs = [], []
        for g in range(SG):
            ks = range(g * (NPAIR // SG), (g + 1) * (NPAIR // SG))
            dcs.append(sum(_dot_nt(dya[k], sp16b[k]) for k in ks) + _dot_tn(dg_t[g], bbs[g]))
            dbs.append(sum(_dot_nt(xe[k], ds16b[k]) for k in ks) + _dot(dg_t[g], cbs[g]))
        dst[...] = _unpairs(cm.cdpl * ds16 + jnp.stack([_dot(cts[grp(k)], dya[k]) for k in range(NPAIR)]))
        da_ref[...] = jnp.concatenate([_unpairs(dxdt * cm.dpl + cm.dskpl * dy16)] + dbs + dcs, axis=1)
        y_off = jnp.stack([_dot(cbs[grp(k)], sp16b[k]) for k in range(NPAIR)]) * cm.eapl
        da_cols = _head_sums(da_diag + dy16 * y_off - xdt16 * dxs)
        last = _head_sums(jnp.sum(xdt16 * dxs, axis=1, keepdims=True)
                          + cm.cdpl * jnp.sum(ds16 * sp16, axis=1, keepdims=True))
        ddt = _head_sums(dxdt * xs16)
        row_i = lax.broadcasted_iota(jnp.int32, (BLK, 128), 0)
        dacum = da_cols + jnp.where(row_i == BLK - 1, last, 0.0)
        dda = _tri_mm((ri <= ci).astype(BF16), dacum)
        ddt = ddt + dda * cm.a
        dal_ref[...] += _csum(dda * cm.dtv) * cm.a
        ddt_raw = jnp.where(lane_c < SH, ddt * _sig(dt_raw + dtb), 0.0)
        ddt_ref[...] = ddt_raw.astype(BF16)
        ddtb_ref[...] += _csum(ddt_raw)
        ddk_ref[...] += _head_sums(jnp.sum(dy16 * xs16, axis=1, keepdims=True))

    rev = lambda i: nc - 1 - i
    vec = _full((1, 128))
    slab = pl.BlockSpec((BLK, SSM_W), lambda i: (rev(i), 0))
    return pl.pallas_call(
        body, name="ssd_bwd", grid=(nc,),
        in_specs=[slab,
                  pl.BlockSpec((BLK, SG * SN), lambda i: (rev(i), SSM_W // (SG * SN))),
                  pl.BlockSpec((BLK, SG * SN), lambda i: (rev(i), SSM_W // (SG * SN) + 1)),
                  pl.BlockSpec((BLK, 128), lambda i: (rev(i), 0)),
                  slab,
                  pl.BlockSpec((1, SN, SSM_W), lambda i: (rev(i), 0, 0)), vec, vec, vec],
        out_specs=[pl.BlockSpec((BLK, XBC), lambda i: (rev(i), 0)), pl.BlockSpec((BLK, 128), lambda i: (rev(i), 0)),
                   vec, vec, vec],
        out_shape=[jax.ShapeDtypeStruct((t, XBC), F32), jax.ShapeDtypeStruct((t, 128), BF16),
                   jax.ShapeDtypeStruct((1, 128), F32), jax.ShapeDtypeStruct((1, 128), F32),
                   jax.ShapeDtypeStruct((1, 128), F32)],
        scratch_shapes=[pltpu.VMEM((SN, SSM_W), F32)],
        compiler_params=_cp(("arbitrary",)),
    )(act, act, act, dt_raw, dy, sprev, dtb_p, alog_p, dsk_p)


TAIL_TM = 256


def _dsilu(z, s):
    return s * (1.0 + z * (1.0 - s))


def tail(proj, ao, yss, x, target, gate, ssm_nw, w_at, w_ss, w_ou):
    t = x.shape[0]
    tm = min(t, TAIL_TM)
    gw = SSM_W // SG

    def body(ao_ref, za_ref, ga_ref, gb_ref, zm_ref, ys_ref, x_ref, tg_ref, gt_ref, nw_ref, wa_ref, ws_ref, wo_ref,
             loss_ref, dy_ref, dao_ref, dmid_ref, dys_ref,
             ua_ref, yn_ref, mg_ref, dya_ref, dyb_ref, do_ref, dgt_ref, dnw_ref):
        i = pl.program_id(0)

        @pl.when(i == 0)
        def _():
            loss_ref[...] = jnp.zeros_like(loss_ref)
            dgt_ref[...] = jnp.zeros_like(dgt_ref)
            dnw_ref[...] = jnp.zeros_like(dnw_ref)

        ao = ao_ref[...].astype(F32)
        za = za_ref[...].astype(F32)
        sa = _sig(za)
        sila = za * sa
        ua_f = ao * sila
        ua = ua_f.astype(BF16)
        ya = _dot(ua, wa_ref[...])
        zm = zm_ref[...].astype(F32)
        sm = _sig(zm)
        silm = zm * sm
        ys = ys_ref[...].astype(F32)
        u = ys * silm
        nw = nw_ref[...]
        rs, uns = [], []
        for g in range(SG):
            ug = u[:, gw * g:gw * (g + 1)]
            r = lax.rsqrt(jnp.mean(ug * ug, axis=-1, keepdims=True) + EPS)
            rs.append(r)
            uns.append(ug * r)
        un = jnp.concatenate(uns, axis=1)
        yn_f = un * nw
        yn = yn_f.astype(BF16)
        yb = _dot(yn, ws_ref[...])
        sga = _sig(ga_ref[...].astype(F32))
        sgb = _sig(gb_ref[...].astype(F32))
        mg_f = sga * ya + sgb * yb
        mg = mg_f.astype(BF16)
        o = _dot(mg, wo_ref[...])
        gt = gt_ref[...]
        err = (x_ref[...] + gt * o) - tg_ref[...]
        lane = lax.broadcasted_iota(jnp.int32, (1, 128), 1)
        loss_ref[...] += jnp.where(lane == 0, 0.5 * _asum(_rsum(err * err) / D), 0.0)
        dy = err * (1.0 / D)
        dy_ref[...] = dy
        dgt_ref[...] += _csum(dy * o)
        do = (dy * gt).astype(BF16)
        dmg = _dot_nt(do, wo_ref[...])
        dmid_ref[:, C_GA - C_ZA:C_GB - C_ZA] = (dmg * ya * sga * (1.0 - sga)).astype(BF16)
        dmid_ref[:, C_GB - C_ZA:C_ZM - C_ZA] = (dmg * yb * sgb * (1.0 - sgb)).astype(BF16)
        dya = (dmg * sga).astype(BF16)
        dyb = (dmg * sgb).astype(BF16)
        dua = _dot_nt(dya, wa_ref[...])
        dao_ref[...] = (dua * sila).astype(BF16)
        dmid_ref[:, 0:C_GA - C_ZA] = (dua * ao * _dsilu(za, sa)).astype(BF16)
        dyn = _dot_nt(dyb, ws_ref[...])
        dnw_ref[...] += _csum(dyn * un)
        dun = dyn * nw
        dus = []
        for g in range(SG):
            gs = slice(gw * g, gw * (g + 1))
            dus.append(rs[g] * (dun[:, gs] - uns[g] * jnp.mean(dun[:, gs] * uns[g], axis=-1, keepdims=True)))
        du = jnp.concatenate(dus, axis=1)
        dys_ref[...] = (du * silm).astype(BF16)
        dmid_ref[:, C_ZM - C_ZA:] = (du * ys * _dsilu(zm, sm)).astype(BF16)
        ua_ref[...] = ua_f.T.astype(BF16)
        yn_ref[...] = yn_f.T.astype(BF16)
        mg_ref[...] = mg_f.T.astype(BF16)
        dya_ref[...] = dya
        dyb_ref[...] = dyb
        do_ref[...] = do

    row = lambda w: pl.BlockSpec((tm, w), lambda i: (i, 0))
    pcol = lambda w, c0: pl.BlockSpec((tm, w), lambda i: (i, c0 // w))
    sd = lambda w, dt: jax.ShapeDtypeStruct((t, w), dt)
    colt = lambda w: pl.BlockSpec((w, tm), lambda i: (0, i))
    sdt = lambda w: jax.ShapeDtypeStruct((w, t), BF16)
    return pl.pallas_call(
        body, name="tail", grid=(t // tm,),
        in_specs=[row(D), pcol(D, C_ZA), pcol(D, C_GA), pcol(D, C_GB), pcol(SSM_W, C_ZM), row(SSM_W), row(D), row(D),
                  _full((1, D)), _full((1, SSM_W)), _full((D, D)), _full((SSM_W, D)), _full((D, D))],
        out_specs=[_full((1, 128)), row(D), row(D), row(W_MID), row(SSM_W),
                   colt(D), colt(SSM_W), colt(D), row(D), row(D), row(D), _full((1, D)), _full((1, SSM_W))],
        out_shape=[jax.ShapeDtypeStruct((1, 128), F32), sd(D, F32), sd(D, BF16), sd(W_MID, BF16),
                   sd(SSM_W, BF16), sdt(D), sdt(SSM_W), sdt(D), sd(D, BF16),
                   sd(D, BF16), sd(D, BF16), jax.ShapeDtypeStruct((1, D), F32), jax.ShapeDtypeStruct((1, SSM_W), F32)],
        compiler_params=_cp(("arbitrary",)),
    )(ao, proj, proj, proj, proj, yss, x, target, gate, ssm_nw, w_at, w_ss, w_ou)


DPIECES = ((D, ((D, C_Q),)),
           (W_MID, ((D, C_ZA), (D, C_GA), (D, C_GB), (SSM_W, C_ZM))),
           (XBC, ((XBC, C_XBC),)),
           (512, ((512, C_K),)),
           (128, ((128, C_DT),)))


def dproj_bwd(pieces, wcat, x, dy, norm_w, scale):
    t = x.shape[0]
    tm = min(t, 256)
    nt = t // tm
    wblocks = [blk for _, subs in DPIECES for blk in subs]
    npc, nwb = len(DPIECES), len(wblocks)

    def body(*refs):
        p_refs, w_refs = refs[:npc], refs[npc:npc + nwb]
        x_ref, dy_ref, nw_ref, sc_ref, gx_ref, dnw_ref, dsc_ref, dsh_ref, dwe_ref = refs[npc + nwb:]
        i = pl.program_id(0)

        @pl.when(i == 0)
        def _():
            for ref in (dwe_ref, dsh_ref, dnw_ref, dsc_ref):
                ref[...] = jnp.zeros_like(ref)

        dh, wi = None, 0
        for p_ref, (_, subs) in zip(p_refs, DPIECES):
            loc = 0
            for w, _ in subs:
                part = _dot_nt(p_ref[:, loc:loc + w], w_refs[wi][...])
                dh = part if dh is None else dh + part
                loc += w
                wi += 1
        xv = x_ref[...]
        r = lax.rsqrt(jnp.mean(xv * xv, axis=-1, keepdims=True) + EPS)
        xn = xv * r
        weff = nw_ref[...] * (1.0 + sc_ref[...])
        dxn = dh * weff
        gx_ref[...] = dy_ref[...] + r * (dxn - xn * jnp.mean(dxn * xn, axis=-1, keepdims=True))
        dwe_ref[...] += _csum(dh * xn)
        dsh_ref[...] += _csum(dh)

        @pl.when(i == nt - 1)
        def _():
            dwe = dwe_ref[...]
            dnw_ref[...] = dwe * (1.0 + sc_ref[...])
            dsc_ref[...] = dwe * nw_ref[...]

    vec = pl.BlockSpec((1, D), lambda i: (0, 0))
    row = pl.BlockSpec((tm, D), lambda i: (i, 0))
    return pl.pallas_call(
        body, name="dproj_bwd", grid=(nt,),
        in_specs=[pl.BlockSpec((tm, pw), lambda i: (i, 0)) for pw, _ in DPIECES]
        + [pl.BlockSpec((D, w), functools.partial(lambda i, b: (0, b), b=off // w), pipeline_mode=pl.Buffered(1))
           for w, off in wblocks]
        + [row, row, vec, vec],
        out_specs=[row, vec, vec, vec],
        out_shape=[jax.ShapeDtypeStruct((t, D), F32), jax.ShapeDtypeStruct((1, D), F32),
                   jax.ShapeDtypeStruct((1, D), F32), jax.ShapeDtypeStruct((1, D), F32)],
        scratch_shapes=[pltpu.VMEM((1, D), F32)],
        compiler_params=_cp(("arbitrary",)),
    )(*pieces, *([wcat] * nwb), x, dy, norm_w, scale)


def wgrad(at, b, name, bn, after):
    m, t = at.shape
    n = b.shape[1]
    tk = min(t, 2048)
    bm = min(m, 1024)

    def body(a_ref, b_ref, after_ref, o_ref):
        part = _dot(a_ref[...], b_ref[...])

        @pl.when(pl.program_id(2) == 0)
        def _():
            o_ref[...] = part

        @pl.when(pl.program_id(2) > 0)
        def _():
            o_ref[...] += part

    return pl.pallas_call(
        body, name=name, grid=(m // bm, n // bn, t // tk),
        in_specs=[pl.BlockSpec((bm, tk), lambda i, j, k: (i, k)), pl.BlockSpec((tk, bn), lambda i, j, k: (k, j)), ANY],
        out_specs=pl.BlockSpec((bm, bn), lambda i, j, k: (i, j)),
        out_shape=jax.ShapeDtypeStruct((m, n), F32),
        compiler_params=_cp(("parallel", "parallel", "arbitrary")),
    )(at, b, after)


SUM_TR = 256


def pair_sum(g, core, theirs, name):
    w = g.shape[2]
    nh = HROWS // SUM_TR

    def body(core_ref, a_ref, b_ref, o_ref, ob_ref):
        s = a_ref[...] + b_ref[...]
        o_ref[...] = s
        ob_ref[...] = s.astype(BF16)

    spec = pl.BlockSpec((1, SUM_TR, w), lambda d, i, c: (d, i, 0))
    return pl.pallas_call(
        body, name=name,
        out_shape=[jax.ShapeDtypeStruct((4, HROWS, w), F32), jax.ShapeDtypeStruct((4, HROWS, w), BF16)],
        grid_spec=pltpu.PrefetchScalarGridSpec(
            num_scalar_prefetch=1, grid=(4, nh),
            in_specs=[pl.BlockSpec((1, SUM_TR, w), lambda d, i, c: (d, c[0] * nh + i, 0)), spec],
            out_specs=[spec, spec]),
        compiler_params=_cp(("parallel", "parallel")))(core.reshape(1).astype(jnp.int32), g, theirs)


def chip_sum(part, chip, others, name):
    r, w = part.shape[1:]

    def body(chip_ref, a_ref, b_ref, o_ref):
        acc = a_ref[0]
        for k in range(3):
            acc = acc + b_ref[k].astype(F32)
        o_ref[...] = acc

    return pl.pallas_call(
        body, name=name, out_shape=jax.ShapeDtypeStruct((r, w), F32),
        grid_spec=pltpu.PrefetchScalarGridSpec(
            num_scalar_prefetch=1, grid=(r // SUM_TR,),
            in_specs=[pl.BlockSpec((1, SUM_TR, w), lambda i, c: (c[0], i, 0)),
                      pl.BlockSpec((3, SUM_TR, w), lambda i, c: (0, i, 0))],
            out_specs=pl.BlockSpec((SUM_TR, w), lambda i, c: (i, 0))),
        compiler_params=_cp(("parallel",)))(chip.reshape(1).astype(jnp.int32), part, others)


def sum_devices(g):
    r = g.shape[1]

    def body(g_ref, o_ref):
        acc = g_ref[0]
        for d in range(1, 8):
            acc = acc + g_ref[d]
        o_ref[...] = acc

    return pl.pallas_call(body, name="sum_devices", out_shape=jax.ShapeDtypeStruct((r, 1024), F32),
                          compiler_params=_cp())(g)


def adamw(w, g, m, v, name):
    r, c = w.shape
    tr = r
    for cand in (256, 128, 64, 32, 16, 8):
        if r % cand == 0 and r > cand:
            tr = cand
            break

    def body(w_ref, g_ref, m_ref, v_ref, d_ref, nm_ref, nv_ref):
        gv = g_ref[...]
        mn = ADAM_B1 * m_ref[...] + (1.0 - ADAM_B1) * gv
        vn = ADAM_B2 * v_ref[...] + (1.0 - ADAM_B2) * (gv * gv)
        m_hat = mn / (1.0 - ADAM_B1 ** ADAM_STEP)
        v_hat = vn / (1.0 - ADAM_B2 ** ADAM_STEP)
        d_ref[...] = -ADAM_LR * (m_hat / (jnp.sqrt(v_hat) + ADAM_EPS) + ADAM_WD * w_ref[...])
        nm_ref[...] = mn
        nv_ref[...] = vn

    spec = pl.BlockSpec((tr, c), lambda i: (i, 0))
    sd = jax.ShapeDtypeStruct((r, c), F32)
    return pl.pallas_call(body, name=name, grid=(r // tr,), in_specs=[spec] * 4, out_specs=[spec] * 3,
                          out_shape=[sd, sd, sd], compiler_params=_cp(("parallel",)))(w, g, m, v)


def adamw_halves(w, mine, theirs, core, m, v, name):
    r, c = w.shape
    tr = 128
    nh = HROWS // tr

    def body(core_ref, w_ref, a_ref, b_ref, m_ref, v_ref, g_ref, d_ref, nm_ref, nv_ref):
        gv = jnp.where(pl.program_id(0) // nh == core_ref[0], a_ref[...], b_ref[...])
        mn = ADAM_B1 * m_ref[...] + (1.0 - ADAM_B1) * gv
        vn = ADAM_B2 * v_ref[...] + (1.0 - ADAM_B2) * (gv * gv)
        m_hat = mn / (1.0 - ADAM_B1 ** ADAM_STEP)
        v_hat = vn / (1.0 - ADAM_B2 ** ADAM_STEP)
        g_ref[...] = gv
        d_ref[...] = -ADAM_LR * (m_hat / (jnp.sqrt(v_hat) + ADAM_EPS) + ADAM_WD * w_ref[...])
        nm_ref[...] = mn
        nv_ref[...] = vn

    spec = pl.BlockSpec((tr, c), lambda i, s: (i, 0))
    half = pl.BlockSpec((tr, c), lambda i, s: (i % nh, 0))
    sd = jax.ShapeDtypeStruct((r, c), F32)
    return pl.pallas_call(
        body, name=name, out_shape=[sd, sd, sd, sd],
        grid_spec=pltpu.PrefetchScalarGridSpec(num_scalar_prefetch=1, grid=(r // tr,),
                                               in_specs=[spec, half, half, spec, spec], out_specs=[spec] * 4),
        compiler_params=_cp(("parallel",)))(core.reshape(1).astype(jnp.int32), w, mine, theirs, m, v)


ANY = pl.BlockSpec(memory_space=pl.ANY)
VM = pl.BlockSpec(memory_space=pltpu.VMEM)
OTHER_CHIPS = ((1, 0), (0, 1), (1, 1))


def _pos():
    return lax.axis_index("x"), lax.axis_index("y"), lax.axis_index("c")


def _flip(v, bit):
    return 1 - v if bit else v


def _rcopy(src, dst, ssem, rsem, peer):
    return pltpu.make_async_remote_copy(src_ref=src, dst_ref=dst, send_sem=ssem, recv_sem=rsem,
                                        device_id=peer, device_id_type=MESH)


def allgather_small(p, name):
    r = p.shape[0]

    def body(in_ref, out_ref, ssem, rsem, lsem):
        x, y, c = _pos()
        me = 4 * x + 2 * y + c
        loc = pltpu.make_async_copy(in_ref, out_ref.at[me], lsem)
        loc.start()
        sends = []
        peers = []
        for k in range(1, 8):
            px, py, pc = _flip(x, (k >> 2) & 1), _flip(y, (k >> 1) & 1), _flip(c, k & 1)
            peers.append((px, py, pc))
            cp = _rcopy(in_ref, out_ref.at[me], ssem.at[k - 1], rsem.at[k - 1], (px, py, pc))
            cp.start()
            sends.append(cp)
        for k in range(1, 8):
            px, py, pc = peers[k - 1]
            _rcopy(in_ref, out_ref.at[4 * px + 2 * py + pc], ssem.at[k - 1], rsem.at[k - 1], (px, py, pc)).wait_recv()
        for cp in sends:
            cp.wait_send()
        loc.wait()

    return pl.pallas_call(
        body, name=name, out_shape=jax.ShapeDtypeStruct((8, r, 1024), F32),
        in_specs=[VM], out_specs=VM,
        scratch_shapes=[pltpu.SemaphoreType.DMA((7,)), pltpu.SemaphoreType.DMA((7,)), pltpu.SemaphoreType.DMA],
    )(p)


def gather_weights(w_in_b, mod_sh):
    def body(wi_ref, m_ref, gi_ref, mo_ref, ssem, rsem, lsem):
        x, y, c = _pos()
        chip = 2 * x + y
        mine = pl.ds(pl.multiple_of(c * HROWS, 16), HROWS)
        other = pl.ds(pl.multiple_of((1 - c) * HROWS, 16), HROWS)
        sib = (x, y, 1 - c)
        pairs = ((wi_ref, gi_ref),)
        loc_m = pltpu.make_async_copy(m_ref, mo_ref.at[chip], lsem)
        loc_m.start()
        sends = []
        for k, (fx, fy) in enumerate(OTHER_CHIPS):
            peer = (_flip(x, fx), _flip(y, fy), c)
            for a, (w_ref, g_ref) in enumerate(pairs):
                cw = _rcopy(w_ref.at[mine], g_ref.at[chip, mine], ssem.at[6 * a + k], rsem.at[6 * a + k], peer)
                cw.start()
                sends.append(cw)
            cm = _rcopy(m_ref, mo_ref.at[chip], ssem.at[12 + k], rsem.at[12 + k], peer)
            cm.start()
            sends.append(cm)
        for k, (fx, fy) in enumerate(OTHER_CHIPS):
            px, py = _flip(x, fx), _flip(y, fy)
            for a, (w_ref, g_ref) in enumerate(pairs):
                got = g_ref.at[2 * px + py, mine]
                _rcopy(w_ref.at[mine], got, ssem.at[6 * a + k], rsem.at[6 * a + k], (px, py, c)).wait_recv()
                fw = _rcopy(got, got, ssem.at[6 * a + 3 + k], rsem.at[6 * a + 3 + k], sib)
                fw.start()
                sends.append(fw)
        for k, (fx, fy) in enumerate(OTHER_CHIPS):
            px, py = _flip(x, fx), _flip(y, fy)
            for a, (w_ref, g_ref) in enumerate(pairs):
                land = g_ref.at[2 * px + py, other]
                _rcopy(land, land, ssem.at[6 * a + 3 + k], rsem.at[6 * a + 3 + k], sib).wait_recv()
            _rcopy(m_ref, mo_ref.at[2 * px + py], ssem.at[12 + k], rsem.at[12 + k], (px, py, c)).wait_recv()
        for cp in sends:
            cp.wait_send()
        loc_m.wait()

    return pl.pallas_call(
        body, name="gather_weights",
        out_shape=[jax.ShapeDtypeStruct((4, D, SH_IN), BF16), jax.ShapeDtypeStruct((4, 8, 768), F32)],
        in_specs=[ANY, VM], out_specs=[ANY, VM],
        scratch_shapes=[pltpu.SemaphoreType.DMA((15,)), pltpu.SemaphoreType.DMA((15,)), pltpu.SemaphoreType.DMA],
    )(w_in_b, mod_sh)


def pair_exchange(g):
    def body(g_ref, r_ref, ssem, rsem):
        x, y, c = _pos()
        other = pl.ds(pl.multiple_of((1 - c) * HROWS, 8), HROWS)
        cp = _rcopy(g_ref.at[:, other, :], r_ref, ssem, rsem, (x, y, 1 - c))
        cp.start()
        cp.wait()

    return pl.pallas_call(
        body, name="pair_exchange", out_shape=jax.ShapeDtypeStruct((4, HROWS, g.shape[2]), F32),
        in_specs=[ANY], out_specs=ANY,
        scratch_shapes=[pltpu.SemaphoreType.DMA, pltpu.SemaphoreType.DMA],
    )(g)


HBM = pl.BlockSpec(memory_space=pltpu.HBM)
SEM = pl.BlockSpec(memory_space=pltpu.SEMAPHORE)
DATAFLOW = pltpu.SideEffectType.DATAFLOW_SIDE_EFFECTING


def split_start(name, make_copies, srcs, lands, nsem, after):
    arrays = [*srcs, *lands]
    n, ns = len(arrays), len(srcs)

    def body(*refs):
        for cp in make_copies(refs[:ns], refs[ns:n], refs[n + 1], refs[n + 2])[0]:
            cp.start()
        refs[-1][...] = jnp.zeros_like(refs[-1])

    res = pl.pallas_call(
        body, name=name,
        out_shape=(pltpu.SemaphoreType.DMA((nsem,)), pltpu.SemaphoreType.DMA((nsem,)),
                   *[pltpu.HBM(a.shape, a.dtype) for a in arrays], jax.ShapeDtypeStruct((8, 128), F32)),
        in_specs=(HBM,) * n + (ANY,), out_specs=(SEM, SEM) + (HBM,) * n + (VM,),
        input_output_aliases={i: 2 + i for i in range(n)},
        compiler_params=pltpu.CompilerParams(has_side_effects=DATAFLOW),
    )(*[pltpu.with_memory_space_constraint(a, pltpu.HBM) for a in arrays], after)
    return res[0], res[1], list(res[2:2 + n]), res[-1]


def split_wait(name, make_copies, ssem, rsem, arrays, ns, after):
    n = len(arrays)

    def body(*refs):
        sends, recvs = make_copies(refs[:ns], refs[ns:n], refs[n], refs[n + 1])
        for cp in sends:
            cp.wait_send()
        for cp in recvs:
            cp.wait_recv()

    return pl.pallas_call(
        body, name=name, out_shape=tuple(pltpu.HBM(a.shape, a.dtype) for a in arrays),
        in_specs=(HBM,) * n + (SEM, SEM, ANY), out_specs=(HBM,) * n,
        input_output_aliases={i: i for i in range(n)},
        compiler_params=pltpu.CompilerParams(has_side_effects=DATAFLOW),
    )(*arrays, ssem, rsem, after)


def _chip_copies(srcs, lands, ssem, rsem):
    x, y, c = _pos()
    copies = []
    for k, (fx, fy) in enumerate(OTHER_CHIPS):
        px, py = _flip(x, fx), _flip(y, fy)
        for a, (p_ref, l_ref) in enumerate(zip(srcs, lands)):
            copies.append(_rcopy(p_ref.at[2 * px + py], l_ref.at[k], ssem.at[3 * a + k], rsem.at[3 * a + k], (px, py, c)))
    return copies, copies


def _pair_copies(srcs, lands, ssem, rsem):
    x, y, c = _pos()
    other = pl.ds(pl.multiple_of((1 - c) * HROWS, 8), HROWS)
    copies = [_rcopy(srcs[0].at[:, other, :], lands[0], ssem.at[0], rsem.at[0], (x, y, 1 - c))]
    return copies, copies


def _rest_copies(srcs, lands, ssem, rsem):
    x, y, c = _pos()
    chip = 2 * x + y
    mine = pl.ds(pl.multiple_of(c * HROWS, 16), HROWS)
    sends, recvs = [], []
    for k, (fx, fy) in enumerate(OTHER_CHIPS):
        px, py = _flip(x, fx), _flip(y, fy)
        for t in range(2):
            rows_t = pl.ds(t * HROWS, HROWS)
            sends.append(_rcopy(srcs[0].at[mine], lands[0].at[chip, mine], ssem.at[2 * k + t], rsem.at[2 * k + c],
                                (px, py, t)))
            recvs.append(_rcopy(srcs[0].at[rows_t], lands[0].at[2 * px + py, rows_t], ssem.at[2 * k + t],
                                rsem.at[2 * k + t], (px, py, t)))
    return sends, recvs


def _swap_copies(srcs, lands, ssem, rsem):
    x, y, c = _pos()
    copies = [_rcopy(s_ref, l_ref, ssem.at[a], rsem.at[a], (x, y, 1 - c))
              for a, (s_ref, l_ref) in enumerate(zip(srcs, lands))]
    return copies, copies


def _flat(v, width=1024):
    v = v.reshape(-1)
    n = -(-v.shape[0] // width) * width
    return jnp.pad(v, (0, n - v.shape[0]))


def _rows(parts, rows):
    flat = jnp.concatenate(parts)
    return jnp.pad(flat, (0, rows * 1024 - flat.shape[0])).reshape(rows, 1024)


def _pack_small(b_ada, norm_w, conv_b, ssm_norm_w, q_norm_w, k_norm_w, sinks, dt_bias, a_log, d_skip, rel_bias,
                extra=None, tail=(), rows=16):
    misc = [q_norm_w, k_norm_w, sinks, dt_bias, a_log, d_skip] + ([] if extra is None else [extra])
    parts = [_flat(b_ada), _flat(norm_w), _flat(conv_b), _flat(ssm_norm_w)] + [_flat(v, 128) for v in misc]
    parts.append(jnp.zeros(((8 - len(misc)) * 128,), F32))
    parts.append(_flat(rel_bias))
    parts.append(jnp.zeros((5 * 1024,), F32))
    return _rows(parts + [_flat(v) for v in tail], rows)


def _unpack_small(p):
    misc = p[9]
    return dict(b_ada=p[0:3].reshape(1, 3072), norm_w=p[3:4], conv_b=p[4:7].reshape(1, 3072),
                ssm_norm_w=p[7:9].reshape(1, 2048), q_norm_w=misc[None, 0:64], k_norm_w=misc[None, 128:192],
                sinks=misc[None, 256:272], dt_bias=misc[None, 384:416], a_log=misc[None, 512:544],
                d_skip=misc[None, 640:672], rel_bias=p[10, :512].reshape(32, 16), extra=misc[768])


SMALL = ("b_ada", "norm_w", "conv_b", "ssm_norm_w", "q_norm_w", "k_norm_w", "sinks", "dt_bias", "a_log", "d_skip",
         "rel_bias")
WEIGHTS = ("w_ada", "b_ada", "norm_w", "w_in", "q_norm_w", "k_norm_w", "rel_bias", "sinks", "conv_w", "conv_b",
           "dt_bias", "a_log", "d_skip", "ssm_norm_w", "w_attn_proj", "w_ssm_proj", "w_out")
IN_COLS = ((0, 1024, C_Q), (1024, 256, C_K), (1280, 256, C_V), (1536, 1024, C_ZA), (2560, 2048, C_ZM),
           (4608, 3072, C_XBC), (7680, 32, C_DT), (7712, 1024, C_GA), (8736, 1024, C_GB))


def _to_cat(shards):
    parts, pos = [], 0
    for o, n, cnew in sorted(IN_COLS, key=lambda e: e[2]):
        assert cnew == pos
        c0 = o
        while c0 < o + n:
            i = c0 // SH_IN
            c1 = min(o + n, (i + 1) * SH_IN)
            parts.append(shards[i][:, c0 - i * SH_IN:c1 - i * SH_IN])
            c0 = c1
        pos += n
    parts.append(jnp.zeros((D, NP - pos), shards.dtype))
    return jnp.concatenate(parts, axis=1)


def _from_cat(dw_pieces):
    starts = [subs[0][1] for _, subs in DPIECES]

    def cols(c0, c1):
        p = max(q for q in range(len(starts)) if starts[q] <= c0)
        return dw_pieces[p][:, c0 - starts[p]:c1 - starts[p]]

    shards = []
    for i in range(4):
        lo, hi = i * SH_IN, (i + 1) * SH_IN
        parts = []
        for o, n, cnew in IN_COLS:
            a, b = max(o, lo), min(o + n, hi)
            if a < b:
                parts.append(cols(cnew + a - o, cnew + b - o))
        shards.append(jnp.concatenate(parts, axis=1))
    return jnp.stack(shards)


def kernel(x, c, w_ada, b_ada, norm_w, w_in, q_norm_w, k_norm_w, rel_bias, sinks, conv_w, conv_b, dt_bias, a_log, d_skip, ssm_norm_w, w_attn_proj, w_ssm_proj, w_out, loss_target, m_w_ada, m_b_ada, m_norm_w, m_w_in, m_q_norm_w, m_k_norm_w, m_rel_bias, m_sinks, m_conv_w, m_conv_b, m_dt_bias, m_a_log, m_d_skip, m_ssm_norm_w, m_w_attn_proj, m_w_ssm_proj, m_w_out, v_w_ada, v_b_ada, v_norm_w, v_w_in, v_q_norm_w, v_k_norm_w, v_rel_bias, v_sinks, v_conv_w, v_conv_b, v_dt_bias, v_a_log, v_d_skip, v_ssm_norm_w, v_w_attn_proj, v_w_ssm_proj, v_w_out):
    args = dict(locals())
    xi, yi, ci = lax.axis_index("x"), lax.axis_index("y"), lax.axis_index("c")
    chip = 2 * xi + yi
    me = 4 * xi + 2 * yi + ci
    x2 = x[0]
    tgt = loss_target[0]

    pay = _rows([c.reshape(-1), conv_w[0].reshape(-1)], 8)
    g0 = allgather_small(pay, "gather_cond")
    c_all = g0[:, 0, :]
    conv_w_full = g0[0::2, 1:4, :].reshape(4, CONV_K, 768).transpose(1, 0, 2).reshape(CONV_K, XBC)

    b_ada_sh = lax.dynamic_slice(b_ada, (0, chip * 768), (1, 768))
    mod_sh = ada_mod(c_all, w_ada[0], b_ada_sh)

    w_in_b = w_in[0].astype(BF16)
    w_rest_b = jnp.concatenate([w_attn_proj[0], w_ssm_proj[0], w_out[0]], axis=0).astype(BF16)
    wg_in, modg = gather_weights(w_in_b, mod_sh)
    wg_in = lax.dynamic_update_slice(wg_in, w_in_b[None], (chip, 0, 0))
    rs_sem, rr_sem, rest_thru, rest_tok = split_start("gather_rest_start", _rest_copies, [w_rest_b],
                                                      [lax.empty((4, D, D), BF16)], 6, modg)
    mod = lax.dynamic_slice(modg, (0, me, 0), (4, 1, 768)).reshape(1, 3 * D)
    shift, scale, gate = mod[:, :D], mod[:, D:2 * D] + rest_tok[:1, :1], mod[:, 2 * D:]
    wcat = _to_cat(wg_in)

    pad128 = lambda v: jnp.pad(v, ((0, 0), (0, 128 - v.shape[1])))
    dtb_p, alog_p, dsk_p = pad128(dt_bias), pad128(a_log), pad128(d_skip)
    bucket = _bucket_table()

    proj, dt_raw, h_t = norm_proj(x2, norm_w, scale, shift, wcat)
    biasm = bias_expand(rel_bias, sinks, bucket)
    ao = attn_fwd(proj, biasm, q_norm_w, k_norm_w)
    act, dsl = conv_fwd(proj, conv_w_full, conv_b)
    yss, sprev = ssd_fwd(act, dt_raw, dtb_p, alog_p, dsk_p)

    w_rest_b, wg_rest = split_wait("gather_rest_wait", _rest_copies, rs_sem, rr_sem, rest_thru, 1, yss)
    wg_rest = lax.dynamic_update_slice(wg_rest, w_rest_b[None], (chip, 0, 0))
    w_at = wg_rest[:, :R_AT].reshape(D, D)
    w_ss = wg_rest[:, R_AT:R_AT + R_SS].reshape(SSM_W, D)
    w_ou = wg_rest[:, R_AT + R_SS:].reshape(D, D)
    (loss_p, dy, dao, dmid, dyss, ua_t, yn_t, mg_t, dya, dyb, dout, dgate, dssm_nw) = tail(
        proj, ao, yss, x2, tgt, gate, ssm_norm_w, w_at, w_ss, w_ou)

    dq, dkv, dqw, dkw, dacc = attn_bwd(proj, dao, biasm, q_norm_w, k_norm_w)
    dbias = bias_reduce(dacc, bucket)
    drb = dbias[:, :NBUCKET].T
    dsk = dbias[:, NBUCKET].reshape(1, HQ)
    dact, ddt, ddtb, dalog, ddskip = ssd_bwd(act, dt_raw, dyss, sprev, dtb_p, alog_p, dsk_p)
    dxbc, dconv_w, dconv_b = conv_bwd(proj, dact, dsl, conv_w_full)

    dproj = (dq, dmid, dxbc, dkv, ddt)
    dwcat = [wgrad(h_t, piece, "dw_in_%d" % p, 1280 if piece.shape[1] == W_MID else min(piece.shape[1], 1024), rest_tok)
             for p, piece in enumerate(dproj)]

    g_in = _from_cat(dwcat)
    ps_sem, pr_sem, pair_thru, pair_tok = split_start("pair_in_start", _pair_copies, [g_in],
                                                      [lax.empty((4, HROWS, SH_IN), F32)], 1, loss_p)
    dw_at = wgrad(ua_t, dya, "dw_attn", 512, pair_tok)
    dw_ss = wgrad(yn_t, dyb, "dw_ssm", 512, pair_tok)
    dw_ou = wgrad(mg_t, dout, "dw_out", 512, pair_tok)
    g_rest = jnp.concatenate([dw_at.reshape(4, R_AT, D), dw_ss.reshape(4, R_SS, D), dw_ou.reshape(4, R_OU, D)], axis=1)
    sib_rest = pair_exchange(g_rest)
    g_in, sib_in = split_wait("pair_in_wait", _pair_copies, ps_sem, pr_sem, pair_thru, 1, sib_rest)
    part_in, pb_in = pair_sum(g_in, ci, sib_in, "pair_sum_in")
    part_rest, pb_rest = pair_sum(g_rest, ci, sib_rest, "pair_sum_rest")
    cs_sem, cr_sem, chip_thru, token = split_start(
        "chip_exchange_start", _chip_copies, [pb_in, pb_rest],
        [lax.empty((3, HROWS, SH_IN), BF16), lax.empty((3, HROWS, D), BF16)], 6, part_rest)
    grad_x, dnorm_w, dscale, dshift = dproj_bwd(dproj, wcat, x2, dy, norm_w, scale + token[:1, :1])
    _, _, oth_in, oth_rest = split_wait("chip_exchange_wait", _chip_copies, cs_sem, cr_sem, chip_thru, 2, dshift)
    red_in = chip_sum(part_in, chip, oth_in, "chip_sum_in")
    red_rest = chip_sum(part_rest, chip, oth_rest, "chip_sum_rest")
    sw_ssem, sw_rsem, swap_thru, swap_tok = split_start(
        "pair_swap_start", _swap_copies, [red_in, red_rest],
        [lax.empty((HROWS, SH_IN), F32), lax.empty((HROWS, D), F32)], 2, red_rest)

    dmod = jnp.concatenate([dshift, dscale, dgate], axis=1)
    gsmall = _pack_small(dmod, dnorm_w, dconv_b, dssm_nw, dqw, dkw, dsk[:, :HQ], ddtb[:, :SH], dalog[:, :SH],
                         ddskip[:, :SH], drb, extra=loss_p[:, :1] + swap_tok[:1, :1], tail=(dconv_w,), rows=32)
    gall = allgather_small(gsmall, "gather_small_grads")
    ssum = sum_devices(gall)
    gs = _unpack_small(ssum[:16])
    loss = gs["extra"]
    dconv_w_sh = lax.dynamic_slice(ssum[16:28].reshape(CONV_K, XBC), (0, chip * 768), (CONV_K, 768))
    dmod_all = gall[:, 0:3, :].reshape(8, 3 * D)
    dw_ada = ada_grad(c_all, lax.dynamic_slice(dmod_all, (0, chip * 768), (8, 768)))

    grads = dict(gs)
    grads["w_ada"] = dw_ada
    grads["conv_w"] = dconv_w_sh

    delta, new_m, new_v = {}, {}, {}

    def step(n):
        delta[n], new_m[n], new_v[n] = adamw(args[n][0], grads[n], args["m_" + n][0], args["v_" + n][0], "adamw_" + n)

    step("w_ada")
    step("conv_w")
    ws = _pack_small(*[args[n] for n in SMALL])
    ms = _pack_small(*[args["m_" + n] for n in SMALL])
    vs = _pack_small(*[args["v_" + n] for n in SMALL])
    d_s, m_s, v_s = adamw(ws, ssum[:16], ms, vs, "adamw_small")
    red_in, red_rest, recv_in, recv_rest = split_wait("pair_swap_wait", _swap_copies, sw_ssem, sw_rsem, swap_thru, 2, d_s)
    d_s, m_s, v_s = _unpack_small(d_s), _unpack_small(m_s), _unpack_small(v_s)
    for n in SMALL:
        delta[n], new_m[n], new_v[n] = d_s[n], m_s[n], v_s[n]
    grads["w_in"], delta["w_in"], new_m["w_in"], new_v["w_in"] = adamw_halves(
        w_in[0], red_in, recv_in, ci, m_w_in[0], v_w_in[0], "adamw_w_in")
    g_shard_rest = jnp.concatenate([jnp.where(ci == 0, red_rest, recv_rest), jnp.where(ci == 0, recv_rest, red_rest)],
                                   axis=0)
    grads["w_attn_proj"] = g_shard_rest[:R_AT]
    grads["w_ssm_proj"] = g_shard_rest[R_AT:R_AT + R_SS]
    grads["w_out"] = g_shard_rest[R_AT + R_SS:]
    for n in ("w_attn_proj", "w_ssm_proj", "w_out"):
        step(n)

    def shaped(n, a):
        return a.reshape(args[n].shape)

    outs = [loss, grad_x[None]]
    for table in (grads, delta, new_m, new_v):
        outs += [shaped(n, table[n]) for n in WEIGHTS]
    return tuple(outs)
```

```python
import functools
import math

import numpy as np
import jax
import jax.numpy as jnp
from jax import lax
from jax.experimental import pallas as pl
from jax.experimental.pallas import tpu as pltpu

F32 = jnp.float32
BF16 = jnp.bfloat16
MESH = pl.DeviceIdType.MESH

D = 1024
HQ, HKV, GRP, DH = 16, 4, 4, 64
BLK = 128
NBUCKET, MAXDIST = 32, 128
SSM_W, SH, SG, SR, SP, SN = 2048, 32, 4, 8, 64, 128
CONV_K = 4
XBC = SSM_W + 2 * SG * SN
IN_W = 9760
EPS = 1e-6
NEG = -1e30
SCALE = DH ** -0.5

C_Q, C_ZA, C_GA, C_GB, C_ZM, C_XBC, C_K, C_V, C_DT = 0, 1024, 2048, 3072, 4096, 6144, 9216, 9472, 9728
NP = 9984
TN = 3328
W_MID = C_XBC - C_ZA

SH_IN = IN_W // 4
R_AT, R_SS, R_OU = 256, 512, 256
HROWS = D // 2

ADAM_LR, ADAM_B1, ADAM_B2, ADAM_EPS, ADAM_WD, ADAM_STEP = 0.001, 0.9, 0.999, 1e-08, 0.01, 10

VMEM_LIMIT = 56 * 1024 * 1024


def _cp(sem=None):
    if sem is None:
        return pltpu.CompilerParams(vmem_limit_bytes=VMEM_LIMIT)
    return pltpu.CompilerParams(dimension_semantics=sem, vmem_limit_bytes=VMEM_LIMIT)


def _sig(x):
    return 0.5 * jnp.tanh(0.5 * x) + 0.5


def _dot(a, b):
    return jnp.dot(a, b, preferred_element_type=F32)


def _dot_nt(a, b):
    return lax.dot_general(a, b, (((1,), (1,)), ((), ())), preferred_element_type=F32)


def _dot_tn(a, b):
    return lax.dot_general(a, b, (((0,), (0,)), ((), ())), preferred_element_type=F32)


def _rsum(x):
    return jnp.sum(x, axis=-1, keepdims=True)


def _csum(x):
    return jnp.sum(x, axis=0, keepdims=True)


def _asum(x):
    return _csum(_rsum(x))


def _full(shape):
    nd = len(shape)
    return pl.BlockSpec(shape, lambda *_: (0,) * nd)


def ada_mod(c_all, w_ada_sh, b_ada_sh):
    def body(c_ref, w_ref, b_ref, o_ref):
        cv = c_ref[...]
        s = cv * _sig(cv)
        o_ref[...] = jnp.dot(s, w_ref[...], preferred_element_type=F32,
                             precision=lax.Precision.HIGHEST) + b_ref[...]

    n = w_ada_sh.shape[1]
    return pl.pallas_call(body, name="ada_mod", out_shape=jax.ShapeDtypeStruct((8, n), F32),
                          compiler_params=_cp())(c_all, w_ada_sh, b_ada_sh)


def ada_grad(c_all, dmod_sh):
    def body(c_ref, d_ref, o_ref):
        cv = c_ref[...]
        s = cv * _sig(cv)
        o_ref[...] = lax.dot_general(s, d_ref[...], (((0,), (0,)), ((), ())), preferred_element_type=F32,
                                     precision=lax.Precision.HIGHEST)

    n = dmod_sh.shape[1]
    return pl.pallas_call(body, name="ada_grad", out_shape=jax.ShapeDtypeStruct((D, n), F32),
                          compiler_params=_cp())(c_all, dmod_sh)


def norm_proj(x, norm_w, scale, shift, wcat):
    t = x.shape[0]
    tm = min(t, 1024)

    def body(x_ref, nw_ref, sc_ref, sh_ref, w_ref, p_ref, dt_ref, ht_ref, hs):
        @pl.when(pl.program_id(1) == 0)
        def _():
            xv = x_ref[...]
            r = lax.rsqrt(jnp.mean(xv * xv, axis=-1, keepdims=True) + EPS)
            h = (xv * r) * nw_ref[...]
            h = h * (1.0 + sc_ref[...]) + sh_ref[...]
            hs[...] = h.astype(BF16)
            ht_ref[...] = h.T.astype(BF16)

        p = _dot(hs[...], w_ref[...])
        p_ref[...] = p.astype(BF16)

        @pl.when(pl.program_id(1) == C_DT // TN)
        def _():
            dt_ref[...] = p[:, C_DT % TN:C_DT % TN + 128]

    vec = pl.BlockSpec((1, D), lambda i, j: (0, 0))
    return pl.pallas_call(
        body, name="norm_proj", grid=(t // tm, NP // TN),
        in_specs=[pl.BlockSpec((tm, D), lambda i, j: (i, 0)), vec, vec, vec,
                  pl.BlockSpec((D, TN), lambda i, j: (0, j))],
        out_specs=[pl.BlockSpec((tm, TN), lambda i, j: (i, j)), pl.BlockSpec((tm, 128), lambda i, j: (i, 0)),
                   pl.BlockSpec((D, tm), lambda i, j: (0, i))],
        out_shape=[jax.ShapeDtypeStruct((t, NP), BF16), jax.ShapeDtypeStruct((t, 128), F32),
                   jax.ShapeDtypeStruct((D, t), BF16)],
        scratch_shapes=[pltpu.VMEM((tm, D), BF16)],
        compiler_params=_cp(("parallel", "arbitrary")),
    )(x, norm_w, scale, shift, wcat)


def _bucket_table():
    qi = np.arange(BLK)[:, None]
    kj = np.arange(2 * BLK)[None, :]
    dist = qi + BLK - kj
    n = np.maximum(dist, 0)
    max_exact = NBUCKET // 2
    nf = np.maximum(n, 1).astype(np.float32)
    large = max_exact + (np.log(nf / np.float32(max_exact)) / np.float32(math.log(MAXDIST / max_exact))
                         * np.float32(NBUCKET - max_exact)).astype(np.int32)
    large = np.minimum(large, NBUCKET - 1)
    bucket = np.where(n < max_exact, n, large).astype(np.int32)
    valid = (dist >= 0) & (dist < BLK)
    return np.where(valid, bucket, -1).astype(np.int32)


def bias_expand(rel_bias, sinks, bucket):
    def body(rb_ref, sk_ref, bk_ref, o_ref):
        bk = bk_ref[...]
        col = lax.broadcasted_iota(jnp.int32, (BLK, 2 * BLK), 1)

        def head(hd, carry):
            def step(b, acc):
                return jnp.where(bk == b, rb_ref[b, hd], acc)

            acc = lax.fori_loop(0, NBUCKET, step, jnp.full((BLK, 2 * BLK), NEG, F32))
            acc = jnp.where(col == 0, sk_ref[0, hd], acc)
            o_ref[1, hd] = acc
            o_ref[0, hd] = jnp.where(jnp.logical_and(col > 0, col < BLK), NEG, acc)
            return carry

        lax.fori_loop(0, HQ, head, 0)

    smem = pl.BlockSpec(memory_space=pltpu.SMEM)
    return pl.pallas_call(
        body, name="bias_expand", in_specs=[smem, smem, VM], out_specs=VM,
        out_shape=jax.ShapeDtypeStruct((2, HQ, BLK, 2 * BLK), F32), compiler_params=_cp(),
    )(rel_bias, sinks, jnp.asarray(bucket))


def bias_reduce(dacc, bucket):
    col = np.arange(BLK * 2 * BLK) % (2 * BLK)
    lane = np.arange(128)[None, :]
    member = (bucket.reshape(-1)[:, None] == lane) | ((col[:, None] == 0) & (lane == NBUCKET))

    def body(d_ref, m_ref, o_ref):
        mm = m_ref[...]
        o_ref[...] = sum(_dot(part, mm) for part in _split3(d_ref[...]))

    return pl.pallas_call(body, name="bias_reduce", out_shape=jax.ShapeDtypeStruct((HQ, 128), F32),
                          compiler_params=_cp())(dacc.reshape(HQ, BLK * 2 * BLK), jnp.asarray(member, BF16))


GQ = GRP * BLK


def _stack_heads(x, nh):
    return jnp.concatenate([x[:, DH * h:DH * (h + 1)] for h in range(nh)], axis=0)


def _unstack(xs, nh):
    rows = xs.shape[0] // nh
    return jnp.concatenate([xs[rows * h:rows * (h + 1)] for h in range(nh)], axis=1)


def _rms(x):
    return lax.rsqrt(jnp.mean(x * x, axis=-1, keepdims=True) + EPS)


def _stack_q(q, qw):
    qs = _stack_heads(q, HQ)
    r = _rms(qs)
    qhat = qs * r
    return qhat * qw, qhat, r


def _band_first(shape):
    return (lax.broadcasted_iota(jnp.int32, shape, 0) & (2 * BLK - 1)) == 0


def _stack_kv(kp, kc, vp, vc, kw):
    ks = _stack_heads(jnp.concatenate([kp, kc], axis=0), HKV)
    r = _rms(ks)
    khat = ks * r
    first = _band_first(ks.shape)
    kn = jnp.where(first, 0.0, khat * kw)
    v2 = jnp.where(first, 0.0, _stack_heads(jnp.concatenate([vp, vc], axis=0), HKV)).astype(BF16)
    return kn, khat, r, v2


def _softmax_rows(s):
    p = jnp.exp(s - jnp.max(s, axis=-1, keepdims=True))
    return p * (1.0 / _rsum(p))


def attn_fwd(proj, biasm, q_norm_w, k_norm_w):
    t = proj.shape[0]
    nb = t // BLK

    def body(q_ref, kc_ref, kp_ref, vc_ref, vp_ref, bm_ref, qw_ref, kw_ref, o_ref):
        f = lambda ref: ref[...].astype(F32)
        qn = _stack_q(f(q_ref), qw_ref[...])[0].astype(BF16)
        kn, _, _, v2 = _stack_kv(f(kp_ref), f(kc_ref), f(vp_ref), f(vc_ref), kw_ref[...])
        knb = kn.astype(BF16)
        s = jnp.concatenate([_dot_nt(qn[GQ * j:GQ * (j + 1)], knb[2 * BLK * j:2 * BLK * (j + 1)])
                             for j in range(HKV)], axis=0)
        pr = _softmax_rows(s * SCALE + bm_ref[0].reshape(HQ * BLK, 2 * BLK)).astype(BF16)
        o = jnp.concatenate([_dot(pr[GQ * j:GQ * (j + 1)], v2[2 * BLK * j:2 * BLK * (j + 1)])
                             for j in range(HKV)], axis=0)
        o_ref[...] = _unstack(o, HQ).astype(BF16)

    kblk, vblk = C_K // 256, C_V // 256
    prev = lambda n: jnp.maximum(n - 1, 0)
    return pl.pallas_call(
        body, name="attn_fwd", grid=(nb,),
        in_specs=[pl.BlockSpec((BLK, D), lambda n: (n, 0)),
                  pl.BlockSpec((BLK, 256), lambda n: (n, kblk)),
                  pl.BlockSpec((BLK, 256), lambda n: (prev(n), kblk)),
                  pl.BlockSpec((BLK, 256), lambda n: (n, vblk)),
                  pl.BlockSpec((BLK, 256), lambda n: (prev(n), vblk)),
                  pl.BlockSpec((1, HQ, BLK, 2 * BLK), lambda n: (jnp.minimum(n, 1), 0, 0, 0)),
                  _full((1, DH)), _full((1, DH))],
        out_specs=pl.BlockSpec((BLK, D), lambda n: (n, 0)),
        out_shape=jax.ShapeDtypeStruct((t, D), BF16),
        compiler_params=_cp(("parallel",)),
    )(proj, proj, proj, proj, proj, biasm, q_norm_w, k_norm_w)


def attn_bwd(proj, dao, biasm, q_norm_w, k_norm_w):
    t = proj.shape[0]
    nb = t // BLK
    kb = 2 * BLK

    def body(q_ref, kc_ref, kp_ref, vc_ref, vp_ref, do_ref, bm_ref, qw_ref, kw_ref,
             dq_ref, dkv_ref, dqw_ref, dkw_ref, dacc_ref, ck, cv, pk, pv, nk, nv):
        n = pl.program_id(0)

        @pl.when(n == 0)
        def _():
            for ref in (dqw_ref, dkw_ref, dacc_ref, ck, cv):
                ref[...] = jnp.zeros_like(ref)

        qw = qw_ref[...]
        kw = kw_ref[...]
        f = lambda ref: ref[...].astype(F32)
        kn, khat, rk, v2 = _stack_kv(f(kp_ref), f(kc_ref), f(vp_ref), f(vc_ref), kw)
        grp = lambda a, j: a[GQ * j:GQ * (j + 1)]
        band = lambda a, j: a[kb * j:kb * (j + 1)]

        @pl.when(n < nb)
        def _():
            qn, qhat, rq = _stack_q(f(q_ref), qw)
            qnb = qn.astype(BF16)
            knb = kn.astype(BF16)
            dos = _stack_heads(f(do_ref), HQ).astype(BF16)
            s = jnp.concatenate([_dot_nt(grp(qnb, j), band(knb, j)) for j in range(HKV)], axis=0)
            pr = _softmax_rows(s * SCALE + bm_ref[0].reshape(HQ * BLK, kb))
            dp = jnp.concatenate([_dot_nt(grp(dos, j), band(v2, j)) for j in range(HKV)], axis=0)
            ds = pr * (dp - _rsum(pr * dp))
            dacc_ref[...] += ds.reshape(HQ, BLK, kb)
            dsb = ds.astype(BF16)
            prb = pr.astype(BF16)
            dqn = jnp.concatenate([_dot(grp(dsb, j), band(knb, j)) for j in range(HKV)], axis=0) * SCALE
            dqhat = dqn * qw
            dq = rq * (dqhat - qhat * jnp.mean(dqhat * qhat, axis=-1, keepdims=True))
            dq_ref[...] = _unstack(dq, HQ).astype(BF16)
            dqw_ref[...] += _csum(dqn * qhat)
            first = _band_first((kb, DH))
            for j in range(HKV):
                rows = slice(BLK * j, BLK * (j + 1))
                dkn = jnp.where(first, 0.0, _dot_tn(grp(dsb, j), grp(qnb, j)) * SCALE)
                dvj = jnp.where(first, 0.0, _dot_tn(grp(prb, j), grp(dos, j)))
                pk[rows, :] = dkn[:BLK]
                nk[rows, :] = dkn[BLK:]
                pv[rows, :] = dvj[:BLK]
                nv[rows, :] = dvj[BLK:]

        @pl.when(n == nb)
        def _():
            for ref in (pk, pv, nk, nv):
                ref[...] = jnp.zeros_like(ref)

        khp = jnp.concatenate([khat[kb * j:kb * j + BLK] for j in range(HKV)], axis=0)
        rkp = jnp.concatenate([rk[kb * j:kb * j + BLK] for j in range(HKV)], axis=0)
        dkn = ck[...] + pk[...]
        dkhat = dkn * kw
        dk = rkp * (dkhat - khp * jnp.mean(dkhat * khp, axis=-1, keepdims=True))
        dkw_ref[...] += _csum(dkn * khp)
        dkv_ref[...] = jnp.concatenate([_unstack(dk, HKV), _unstack(cv[...] + pv[...], HKV)], axis=1).astype(BF16)
        ck[...] = nk[...]
        cv[...] = nv[...]

    kblk, vblk = C_K // 256, C_V // 256
    cur = lambda n: jnp.minimum(n, nb - 1)
    prev = lambda n: jnp.maximum(n - 1, 0)
    carry = pltpu.VMEM((HKV * BLK, DH), F32)
    return pl.pallas_call(
        body, name="attn_bwd", grid=(nb + 1,),
        in_specs=[pl.BlockSpec((BLK, D), lambda n: (cur(n), 0)),
                  pl.BlockSpec((BLK, 256), lambda n: (cur(n), kblk)), pl.BlockSpec((BLK, 256), lambda n: (prev(n), kblk)),
                  pl.BlockSpec((BLK, 256), lambda n: (cur(n), vblk)), pl.BlockSpec((BLK, 256), lambda n: (prev(n), vblk)),
                  pl.BlockSpec((BLK, D), lambda n: (cur(n), 0)),
                  pl.BlockSpec((1, HQ, BLK, kb), lambda n: (jnp.minimum(n, 1), 0, 0, 0)),
                  _full((1, DH)), _full((1, DH))],
        out_specs=[pl.BlockSpec((BLK, D), lambda n: (cur(n), 0)),
                   pl.BlockSpec((BLK, 512), lambda n: (prev(n), 0)),
                   _full((1, DH)), _full((1, DH)), _full((HQ, BLK, kb))],
        out_shape=[jax.ShapeDtypeStruct((t, D), BF16), jax.ShapeDtypeStruct((t, 512), BF16),
                   jax.ShapeDtypeStruct((1, DH), F32),
                   jax.ShapeDtypeStruct((1, DH), F32), jax.ShapeDtypeStruct((HQ, BLK, kb), F32)],
        scratch_shapes=[carry] * 6,
        compiler_params=_cp(("arbitrary",)),
    )(proj, proj, proj, proj, proj, dao, biasm, q_norm_w, k_norm_w)


CONV_TM, CONV_CW, CONV_RC, HALO = 1024, 1024, 32, 16


def conv_fwd(proj, conv_w, conv_b):
    t = proj.shape[0]
    tm = min(t, CONV_TM)
    c0 = C_XBC // CONV_CW

    def body(x_ref, xp_ref, w_ref, b_ref, o_ref, ds_ref):
        i = pl.program_id(1)
        w = w_ref[...]
        b = b_ref[...]
        for r in range(tm // CONV_RC):
            lo = r * CONV_RC
            if r == 0:
                head = jnp.where(i == 0, 0.0, xp_ref[...].astype(F32))
                win = jnp.concatenate([head, x_ref[0:CONV_RC, :].astype(F32)], axis=0)
            else:
                win = x_ref[lo - HALO:lo + CONV_RC, :].astype(F32)
            acc = b
            for j in range(CONV_K):
                acc = acc + w[j:j + 1] * win[HALO - 3 + j:HALO - 3 + j + CONV_RC]
            sg = _sig(acc)
            o_ref[lo:lo + CONV_RC, :] = acc * sg
            ds_ref[lo:lo + CONV_RC, :] = _dsilu(acc, sg).astype(BF16)

    rh = tm // HALO
    tile = pl.BlockSpec((tm, CONV_CW), lambda s, i: (i, s))
    return pl.pallas_call(
        body, name="conv_fwd", grid=(XBC // CONV_CW, t // tm),
        in_specs=[pl.BlockSpec((tm, CONV_CW), lambda s, i: (i, c0 + s)),
                  pl.BlockSpec((HALO, CONV_CW), lambda s, i: (jnp.maximum(i * rh - 1, 0), c0 + s)),
                  pl.BlockSpec((CONV_K, CONV_CW), lambda s, i: (0, s)), pl.BlockSpec((1, CONV_CW), lambda s, i: (0, s))],
        out_specs=[tile, tile],
        out_shape=[jax.ShapeDtypeStruct((t, XBC), F32), jax.ShapeDtypeStruct((t, XBC), BF16)],
        compiler_params=_cp(("parallel", "parallel")),
    )(proj, proj, conv_w, conv_b)


def conv_bwd(proj, dact, dsl, conv_w):
    t = proj.shape[0]
    tm = min(t, CONV_TM)
    nt = t // tm
    nr = tm // CONV_RC
    c0 = C_XBC // CONV_CW
    ext = CONV_RC + 8

    def body(x_ref, xp_ref, d_ref, dn_ref, s_ref, sn_ref, w_ref, dx_ref, dw_ref, db_ref):
        i = pl.program_id(1)

        @pl.when(i == 0)
        def _():
            dw_ref[...] = jnp.zeros_like(dw_ref)
            db_ref[...] = jnp.zeros_like(db_ref)

        w = w_ref[...]
        dws = [jnp.zeros((1, CONV_CW), F32) for _ in range(CONV_K)]
        db = jnp.zeros((1, CONV_CW), F32)
        for r in range(nr):
            lo = r * CONV_RC
            if r == 0:
                head = jnp.where(i == 0, 0.0, xp_ref[...].astype(F32))
                win = jnp.concatenate([head, x_ref[0:CONV_RC, :].astype(F32)], axis=0)
            else:
                win = x_ref[lo - HALO:lo + CONV_RC, :].astype(F32)
            if r < nr - 1:
                dext = d_ref[lo:lo + ext, :]
                sext = s_ref[lo:lo + CONV_RC + HALO, :].astype(F32)[0:ext]
            else:
                dext = jnp.concatenate([d_ref[lo:lo + CONV_RC, :], jnp.where(i == nt - 1, 0.0, dn_ref[...])], axis=0)
                sext = jnp.concatenate([s_ref[lo:lo + CONV_RC, :].astype(F32), sn_ref[...].astype(F32)], axis=0)[0:ext]
            dpre = dext * sext
            dx = jnp.zeros((CONV_RC, CONV_CW), F32)
            own = dpre[0:CONV_RC]
            for j in range(CONV_K):
                dx = dx + w[j:j + 1] * dpre[3 - j:3 - j + CONV_RC]
                dws[j] = dws[j] + _csum(own * win[HALO - 3 + j:HALO - 3 + j + CONV_RC])
            db = db + _csum(own)
            dx_ref[lo:lo + CONV_RC, :] = dx.astype(BF16)
        dw_ref[...] += jnp.concatenate(dws, axis=0)
        db_ref[...] += db

    rh = tm // HALO
    r8 = tm // 8
    nxt = lambda i, per: jnp.minimum((i + 1) * per, nt * per - 1)
    return pl.pallas_call(
        body, name="conv_bwd", grid=(XBC // CONV_CW, nt),
        in_specs=[pl.BlockSpec((tm, CONV_CW), lambda s, i: (i, c0 + s)),
                  pl.BlockSpec((HALO, CONV_CW), lambda s, i: (jnp.maximum(i * rh - 1, 0), c0 + s)),
                  pl.BlockSpec((tm, CONV_CW), lambda s, i: (i, s)),
                  pl.BlockSpec((8, CONV_CW), lambda s, i: (nxt(i, r8), s)),
                  pl.BlockSpec((tm, CONV_CW), lambda s, i: (i, s)),
                  pl.BlockSpec((HALO, CONV_CW), lambda s, i: (nxt(i, rh), s)),
                  pl.BlockSpec((CONV_K, CONV_CW), lambda s, i: (0, s))],
        out_specs=[pl.BlockSpec((tm, CONV_CW), lambda s, i: (i, s)),
                   pl.BlockSpec((CONV_K, CONV_CW), lambda s, i: (0, s)), pl.BlockSpec((1, CONV_CW), lambda s, i: (0, s))],
        out_shape=[jax.ShapeDtypeStruct((t, XBC), BF16), jax.ShapeDtypeStruct((CONV_K, XBC), F32),
                   jax.ShapeDtypeStruct((1, XBC), F32)],
        compiler_params=_cp(("parallel", "arbitrary")),
    )(proj, proj, dact, dact, dsl, dsl, conv_w)


def _split3(x):
    h = x.astype(BF16)
    r = x - h.astype(F32)
    m = r.astype(BF16)
    lo = (r - m.astype(F32)).astype(BF16)
    return h, m, lo


def _tri_mm(tri, x):
    h, m, lo = _split3(x)
    return _dot(tri, h) + _dot(tri, m) + _dot(tri, lo)


def _softplus(x):
    return jnp.maximum(x, 0.0) + jnp.log1p(jnp.exp(-jnp.abs(x)))


def _chunk_decays(dt_raw, dtb, alog):
    dtv = _softplus(dt_raw + dtb)
    a = -jnp.exp(alog)
    ri = lax.broadcasted_iota(jnp.int32, (BLK, BLK), 0)
    ci = lax.broadcasted_iota(jnp.int32, (BLK, BLK), 1)
    causal = ri >= ci
    acum = _tri_mm(causal.astype(BF16), dtv * a)
    return dtv, a, causal, acum, acum.T


NPAIR = SH // 2


def _pairs(x):
    return jnp.stack([x[:, 128 * k:128 * (k + 1)] for k in range(NPAIR)])


def _unpairs(x3):
    return jnp.concatenate([x3[k] for k in range(NPAIR)], axis=1)


def _per_head_cols(m):
    return jnp.stack([jnp.broadcast_to(m[:, h:h + 1], m.shape) for h in range(SH)])


def _pair_lanes(t):
    r = t.reshape(NPAIR, 2, t.shape[1], 128)
    lo = lax.broadcasted_iota(jnp.int32, (1, t.shape[1], 128), 2) < SP
    return jnp.where(lo, r[:, 0], r[:, 1])


class _Chunk:
    pass


def _chunk_common(dt_raw, dtb, alog, dskip):
    cm = _Chunk()
    cm.dtv, cm.a, cm.causal, acum, acum_t = _chunk_decays(dt_raw, dtb, alog)
    cm.acol = _per_head_cols(acum)
    cm.arow = jnp.stack([acum_t[h:h + 1, :] for h in range(SH)])
    apl = _pair_lanes(cm.acol)
    alast = apl[:, BLK - 1:BLK, :]
    cm.dpl = _pair_lanes(_per_head_cols(cm.dtv))
    cm.eapl = jnp.exp(apl)
    cm.epl = jnp.exp(alast - apl)
    cm.cdpl = jnp.exp(alast)
    cm.dskpl = _pair_lanes(_per_head_cols(dskip))
    cm.lo = lax.broadcasted_iota(jnp.int32, (1, BLK, 128), 2) < SP
    return cm


def ssd_fwd(act, dt_raw, dtb_p, alog_p, dsk_p):
    t = act.shape[0]
    nc = t // BLK

    def body(xs_ref, b_ref, c_ref, dt_ref, dtb_ref, al_ref, dk_ref, y_ref, sp_ref, st):
        c = pl.program_id(0)

        @pl.when(c == 0)
        def _():
            st[...] = jnp.zeros_like(st)

        s_t = st[...]
        sp_ref[0] = s_t
        cm = _chunk_common(dt_ref[...], dtb_ref[...], al_ref[...], dk_ref[...])
        gms, cbs, bts = [], [], []
        for g in range(SG):
            bf = b_ref[:, SN * g:SN * (g + 1)]
            cb = c_ref[:, SN * g:SN * (g + 1)].astype(BF16)
            gms.append(_dot_nt(cb, bf.astype(BF16)))
            cbs.append(cb)
            bts.append(bf.T.astype(BF16))
        lam = jnp.exp(jnp.where(cm.causal[None], cm.acol - cm.arow, NEG))
        m = (lam.reshape(SG, SR, BLK, BLK) * jnp.stack(gms)[:, None]).reshape(SH, BLK, BLK).astype(BF16)
        xs16 = _pairs(xs_ref[...])
        xdt16 = xs16 * cm.dpl
        x_lo = jnp.where(cm.lo, xdt16, 0.0).astype(BF16)
        x_hi = jnp.where(cm.lo, 0.0, xdt16).astype(BF16)
        s16 = _pairs(s_t)
        s16b = s16.astype(BF16)
        yd = jnp.stack([_dot(m[2 * k], x_lo[k]) + _dot(m[2 * k + 1], x_hi[k]) for k in range(NPAIR)])
        yo = jnp.stack([_dot(cbs[k // (NPAIR // SG)], s16b[k]) for k in range(NPAIR)])
        y_ref[...] = _unpairs(yd + yo * cm.eapl + cm.dskpl * xs16).astype(BF16)
        xe = (xdt16 * cm.epl).astype(BF16)
        st[...] = _unpairs(cm.cdpl * s16 + jnp.stack([_dot(bts[k // (NPAIR // SG)], xe[k]) for k in range(NPAIR)]))

    vec = _full((1, 128))
    return pl.pallas_call(
        body, name="ssd_fwd", grid=(nc,),
        in_specs=[pl.BlockSpec((BLK, SSM_W), lambda c: (c, 0)),
                  pl.BlockSpec((BLK, SG * SN), lambda c: (c, SSM_W // (SG * SN))),
                  pl.BlockSpec((BLK, SG * SN), lambda c: (c, SSM_W // (SG * SN) + 1)),
                  pl.BlockSpec((BLK, 128), lambda c: (c, 0)), vec, vec, vec],
        out_specs=[pl.BlockSpec((BLK, SSM_W), lambda c: (c, 0)), pl.BlockSpec((1, SN, SSM_W), lambda c: (c, 0, 0))],
        out_shape=[jax.ShapeDtypeStruct((t, SSM_W), BF16), jax.ShapeDtypeStruct((nc, SN, SSM_W), F32)],
        scratch_shapes=[pltpu.VMEM((SN, SSM_W), F32)],
        compiler_params=_cp(("arbitrary",)),
    )(act, act, act, dt_raw, dtb_p, alog_p, dsk_p)


def _head_sums(q):
    r = q.shape[1]
    lo = lax.broadcasted_iota(jnp.int32, (1, r, 128), 2) < SP
    s_lo = jnp.sum(jnp.where(lo, q, 0.0), axis=-1, keepdims=True)
    s_hi = jnp.sum(jnp.where(lo, 0.0, q), axis=-1, keepdims=True)
    lane = lax.broadcasted_iota(jnp.int32, (r, 128), 1)
    out = jnp.zeros((r, 128), F32)
    for k in range(NPAIR):
        out = jnp.where(lane == 2 * k, s_lo[k], jnp.where(lane == 2 * k + 1, s_hi[k], out))
    return out


def ssd_bwd(act, dt_raw, dy, sprev, dtb_p, alog_p, dsk_p):
    t = act.shape[0]
    nc = t // BLK

    def body(xs_ref, b_ref, c_ref, dt_ref, dy_ref, sp_ref, dtb_ref, al_ref, dk_ref,
             da_ref, ddt_ref, ddtb_ref, dal_ref, ddk_ref, dst):
        i = pl.program_id(0)

        @pl.when(i == 0)
        def _():
            dst[...] = jnp.zeros_like(dst)
            ddtb_ref[...] = jnp.zeros_like(ddtb_ref)
            dal_ref[...] = jnp.zeros_like(dal_ref)
            ddk_ref[...] = jnp.zeros_like(ddk_ref)

        dt_raw = dt_ref[...]
        dtb = dtb_ref[...]
        cm = _chunk_common(dt_raw, dtb, al_ref[...], dk_ref[...])
        ri = lax.broadcasted_iota(jnp.int32, (BLK, BLK), 0)
        ci = lax.broadcasted_iota(jnp.int32, (BLK, BLK), 1)
        lam_t = jnp.exp(jnp.where((ri <= ci)[None], cm.arow - cm.acol, NEG))
        bbs, cbs, cts, gms = [], [], [], []
        for g in range(SG):
            bf = b_ref[:, SN * g:SN * (g + 1)]
            cf = c_ref[:, SN * g:SN * (g + 1)]
            bbs.append(bf.astype(BF16))
            cbs.append(cf.astype(BF16))
            cts.append(cf.T.astype(BF16))
            gms.append(_dot_nt(bbs[g], cbs[g]))
        grp = lambda k: k // (NPAIR // SG)
        xs16 = _pairs(xs_ref[...])
        dy16 = _pairs(dy_ref[...].astype(F32))
        sp16 = _pairs(sp_ref[0])
        ds16 = _pairs(dst[...])
        xdt16 = xs16 * cm.dpl
        xdtb = xdt16.astype(BF16)
        dyh = [jnp.where(cm.lo, dy16, 0.0).astype(BF16), jnp.where(cm.lo, 0.0, dy16).astype(BF16)]
        m_t = (lam_t.reshape(SG, SR, BLK, BLK) * jnp.stack(gms)[:, None]).reshape(SH, BLK, BLK).astype(BF16)
        dxdt = jnp.stack([_dot(m_t[2 * k], dyh[0][k]) + _dot(m_t[2 * k + 1], dyh[1][k]) for k in range(NPAIR)])
        dm_t = jnp.stack([_dot_nt(xdtb[h // 2], dyh[h % 2][h // 2]) for h in range(SH)])
        dg_t = jnp.sum((dm_t * lam_t).reshape(SG, SR, BLK, BLK), axis=1).astype(BF16)
        xq16 = xdtb.astype(F32)
        xh = [jnp.where(cm.lo, xdt16, 0.0).astype(BF16), jnp.where(cm.lo, 0.0, xdt16).astype(BF16)]
        y_in = jnp.stack([_dot_tn(m_t[2 * k], xh[0][k]) + _dot_tn(m_t[2 * k + 1], xh[1][k]) for k in range(NPAIR)])
        da_diag = dy16 * y_in - xq16 * dxdt
        lane_c = lax.broadcasted_iota(jnp.int32, (BLK, 128), 1)
        ds16b = ds16.astype(BF16)
        sp16b = sp16.astype(BF16)
        dxs = jnp.stack([_dot(bbs[grp(k)], ds16b[k]) for k in range(NPAIR)]) * cm.epl
        dxdt = dxdt + dxs
        dya = (dy16 * cm.eapl).astype(BF16)
        xe = (xdt16 * cm.epl).astype(BF16)
        dcs, dbs = [], []
        for g in range(SG):
            ks = range(g * (NPAIR // SG), (g + 1) * (NPAIR // SG))
            dcs.append(sum(_dot_nt(dya[k], sp16b[k]) for k in ks) + _dot_tn(dg_t[g], bbs[g]))
            dbs.append(sum(_dot_nt(xe[k], ds16b[k]) for k in ks) + _dot(dg_t[g], cbs[g]))
        dst[...] = _unpairs(cm.cdpl * ds16 + jnp.stack([_dot(cts[grp(k)], dya[k]) for k in range(NPAIR)]))
        da_ref[...] = jnp.concatenate([_unpairs(dxdt * cm.dpl + cm.dskpl * dy16)] + dbs + dcs, axis=1)
        y_off = jnp.stack([_dot(cbs[grp(k)], sp16b[k]) for k in range(NPAIR)]) * cm.eapl
        da_cols = _head_sums(da_diag + dy16 * y_off - xdt16 * dxs)
        last = _head_sums(jnp.sum(xdt16 * dxs, axis=1, keepdims=True)
                          + cm.cdpl * jnp.sum(ds16 * sp16, axis=1, keepdims=True))
        ddt = _head_sums(dxdt * xs16)
        row_i = lax.broadcasted_iota(jnp.int32, (BLK, 128), 0)
        dacum = da_cols + jnp.where(row_i == BLK - 1, last, 0.0)
        dda = _tri_mm((ri <= ci).astype(BF16), dacum)
        ddt = ddt + dda * cm.a
        dal_ref[...] += _csum(dda * cm.dtv) * cm.a
        ddt_raw = jnp.where(lane_c < SH, ddt * _sig(dt_raw + dtb), 0.0)
        ddt_ref[...] = ddt_raw.astype(BF16)
        ddtb_ref[...] += _csum(ddt_raw)
        ddk_ref[...] += _head_sums(jnp.sum(dy16 * xs16, axis=1, keepdims=True))

    rev = lambda i: nc - 1 - i
    vec = _full((1, 128))
    slab = pl.BlockSpec((BLK, SSM_W), lambda i: (rev(i), 0))
    return pl.pallas_call(
        body, name="ssd_bwd", grid=(nc,),
        in_specs=[slab,
                  pl.BlockSpec((BLK, SG * SN), lambda i: (rev(i), SSM_W // (SG * SN))),
                  pl.BlockSpec((BLK, SG * SN), lambda i: (rev(i), SSM_W // (SG * SN) + 1)),
                  pl.BlockSpec((BLK, 128), lambda i: (rev(i), 0)),
                  slab,
                  pl.BlockSpec((1, SN, SSM_W), lambda i: (rev(i), 0, 0)), vec, vec, vec],
        out_specs=[pl.BlockSpec((BLK, XBC), lambda i: (rev(i), 0)), pl.BlockSpec((BLK, 128), lambda i: (rev(i), 0)),
                   vec, vec, vec],
        out_shape=[jax.ShapeDtypeStruct((t, XBC), F32), jax.ShapeDtypeStruct((t, 128), BF16),
                   jax.ShapeDtypeStruct((1, 128), F32), jax.ShapeDtypeStruct((1, 128), F32),
                   jax.ShapeDtypeStruct((1, 128), F32)],
        scratch_shapes=[pltpu.VMEM((SN, SSM_W), F32)],
        compiler_params=_cp(("arbitrary",)),
    )(act, act, act, dt_raw, dy, sprev, dtb_p, alog_p, dsk_p)


TAIL_TM = 256


def _dsilu(z, s):
    return s * (1.0 + z * (1.0 - s))


def tail(proj, ao, yss, x, target, gate, ssm_nw, w_at, w_ss, w_ou):
    t = x.shape[0]
    tm = min(t, TAIL_TM)
    gw = SSM_W // SG

    def body(ao_ref, za_ref, ga_ref, gb_ref, zm_ref, ys_ref, x_ref, tg_ref, gt_ref, nw_ref, wa_ref, ws_ref, wo_ref,
             loss_ref, dy_ref, dao_ref, dmid_ref, dys_ref,
             ua_ref, yn_ref, mg_ref, dya_ref, dyb_ref, do_ref, dgt_ref, dnw_ref):
        i = pl.program_id(0)

        @pl.when(i == 0)
        def _():
            loss_ref[...] = jnp.zeros_like(loss_ref)
            dgt_ref[...] = jnp.zeros_like(dgt_ref)
            dnw_ref[...] = jnp.zeros_like(dnw_ref)

        ao = ao_ref[...].astype(F32)
        za = za_ref[...].astype(F32)
        sa = _sig(za)
        sila = za * sa
        ua_f = ao * sila
        ua = ua_f.astype(BF16)
        ya = _dot(ua, wa_ref[...])
        zm = zm_ref[...].astype(F32)
        sm = _sig(zm)
        silm = zm * sm
        ys = ys_ref[...].astype(F32)
        u = ys * silm
        nw = nw_ref[...]
        rs, uns = [], []
        for g in range(SG):
            ug = u[:, gw * g:gw * (g + 1)]
            r = lax.rsqrt(jnp.mean(ug * ug, axis=-1, keepdims=True) + EPS)
            rs.append(r)
            uns.append(ug * r)
        un = jnp.concatenate(uns, axis=1)
        yn_f = un * nw
        yn = yn_f.astype(BF16)
        yb = _dot(yn, ws_ref[...])
        sga = _sig(ga_ref[...].astype(F32))
        sgb = _sig(gb_ref[...].astype(F32))
        mg_f = sga * ya + sgb * yb
        mg = mg_f.astype(BF16)
        o = _dot(mg, wo_ref[...])
        gt = gt_ref[...]
        err = (x_ref[...] + gt * o) - tg_ref[...]
        lane = lax.broadcasted_iota(jnp.int32, (1, 128), 1)
        loss_ref[...] += jnp.where(lane == 0, 0.5 * _asum(_rsum(err * err) / D), 0.0)
        dy = err * (1.0 / D)
        dy_ref[...] = dy
        dgt_ref[...] += _csum(dy * o)
        do = (dy * gt).astype(BF16)
        dmg = _dot_nt(do, wo_ref[...])
        dmid_ref[:, C_GA - C_ZA:C_GB - C_ZA] = (dmg * ya * sga * (1.0 - sga)).astype(BF16)
        dmid_ref[:, C_GB - C_ZA:C_ZM - C_ZA] = (dmg * yb * sgb * (1.0 - sgb)).astype(BF16)
        dya = (dmg * sga).astype(BF16)
        dyb = (dmg * sgb).astype(BF16)
        dua = _dot_nt(dya, wa_ref[...])
        dao_ref[...] = (dua * sila).astype(BF16)
        dmid_ref[:, 0:C_GA - C_ZA] = (dua * ao * _dsilu(za, sa)).astype(BF16)
        dyn = _dot_nt(dyb, ws_ref[...])
        dnw_ref[...] += _csum(dyn * un)
        dun = dyn * nw
        dus = []
        for g in range(SG):
            gs = slice(gw * g, gw * (g + 1))
            dus.append(rs[g] * (dun[:, gs] - uns[g] * jnp.mean(dun[:, gs] * uns[g], axis=-1, keepdims=True)))
        du = jnp.concatenate(dus, axis=1)
        dys_ref[...] = (du * silm).astype(BF16)
        dmid_ref[:, C_ZM - C_ZA:] = (du * ys * _dsilu(zm, sm)).astype(BF16)
        ua_ref[...] = ua_f.T.astype(BF16)
        yn_ref[...] = yn_f.T.astype(BF16)
        mg_ref[...] = mg_f.T.astype(BF16)
        dya_ref[...] = dya
        dyb_ref[...] = dyb
        do_ref[...] = do

    row = lambda w: pl.BlockSpec((tm, w), lambda i: (i, 0))
    pcol = lambda w, c0: pl.BlockSpec((tm, w), lambda i: (i, c0 // w))
    sd = lambda w, dt: jax.ShapeDtypeStruct((t, w), dt)
    colt = lambda w: pl.BlockSpec((w, tm), lambda i: (0, i))
    sdt = lambda w: jax.ShapeDtypeStruct((w, t), BF16)
    return pl.pallas_call(
        body, name="tail", grid=(t // tm,),
        in_specs=[row(D), pcol(D, C_ZA), pcol(D, C_GA), pcol(D, C_GB), pcol(SSM_W, C_ZM), row(SSM_W), row(D), row(D),
                  _full((1, D)), _full((1, SSM_W)), _full((D, D)), _full((SSM_W, D)), _full((D, D))],
        out_specs=[_full((1, 128)), row(D), row(D), row(W_MID), row(SSM_W),
                   colt(D), colt(SSM_W), colt(D), row(D), row(D), row(D), _full((1, D)), _full((1, SSM_W))],
        out_shape=[jax.ShapeDtypeStruct((1, 128), F32), sd(D, F32), sd(D, BF16), sd(W_MID, BF16),
                   sd(SSM_W, BF16), sdt(D), sdt(SSM_W), sdt(D), sd(D, BF16),
                   sd(D, BF16), sd(D, BF16), jax.ShapeDtypeStruct((1, D), F32), jax.ShapeDtypeStruct((1, SSM_W), F32)],
        compiler_params=_cp(("arbitrary",)),
    )(ao, proj, proj, proj, proj, yss, x, target, gate, ssm_nw, w_at, w_ss, w_ou)


DPIECES = ((D, ((D, C_Q),)),
           (W_MID, ((D, C_ZA), (D, C_GA), (D, C_GB), (SSM_W, C_ZM))),
           (XBC, ((XBC, C_XBC),)),
           (512, ((512, C_K),)),
           (128, ((128, C_DT),)))


def dproj_bwd(pieces, wcat, x, dy, norm_w, scale):
    t = x.shape[0]
    tm = min(t, 256)
    nt = t // tm
    wblocks = [blk for _, subs in DPIECES for blk in subs]
    npc, nwb = len(DPIECES), len(wblocks)

    def body(*refs):
        p_refs, w_refs = refs[:npc], refs[npc:npc + nwb]
        x_ref, dy_ref, nw_ref, sc_ref, gx_ref, dnw_ref, dsc_ref, dsh_ref, dwe_ref = refs[npc + nwb:]
        i = pl.program_id(0)

        @pl.when(i == 0)
        def _():
            for ref in (dwe_ref, dsh_ref, dnw_ref, dsc_ref):
                ref[...] = jnp.zeros_like(ref)

        dh, wi = None, 0
        for p_ref, (_, subs) in zip(p_refs, DPIECES):
            loc = 0
            for w, _ in subs:
                part = _dot_nt(p_ref[:, loc:loc + w], w_refs[wi][...])
                dh = part if dh is None else dh + part
                loc += w
                wi += 1
        xv = x_ref[...]
        r = lax.rsqrt(jnp.mean(xv * xv, axis=-1, keepdims=True) + EPS)
        xn = xv * r
        weff = nw_ref[...] * (1.0 + sc_ref[...])
        dxn = dh * weff
        gx_ref[...] = dy_ref[...] + r * (dxn - xn * jnp.mean(dxn * xn, axis=-1, keepdims=True))
        dwe_ref[...] += _csum(dh * xn)
        dsh_ref[...] += _csum(dh)

        @pl.when(i == nt - 1)
        def _():
            dwe = dwe_ref[...]
            dnw_ref[...] = dwe * (1.0 + sc_ref[...])
            dsc_ref[...] = dwe * nw_ref[...]

    vec = pl.BlockSpec((1, D), lambda i: (0, 0))
    row = pl.BlockSpec((tm, D), lambda i: (i, 0))
    return pl.pallas_call(
        body, name="dproj_bwd", grid=(nt,),
        in_specs=[pl.BlockSpec((tm, pw), lambda i: (i, 0)) for pw, _ in DPIECES]
        + [pl.BlockSpec((D, w), functools.partial(lambda i, b: (0, b), b=off // w), pipeline_mode=pl.Buffered(1))
           for w, off in wblocks]
        + [row, row, vec, vec],
        out_specs=[row, vec, vec, vec],
        out_shape=[jax.ShapeDtypeStruct((t, D), F32), jax.ShapeDtypeStruct((1, D), F32),
                   jax.ShapeDtypeStruct((1, D), F32), jax.ShapeDtypeStruct((1, D), F32)],
        scratch_shapes=[pltpu.VMEM((1, D), F32)],
        compiler_params=_cp(("arbitrary",)),
    )(*pieces, *([wcat] * nwb), x, dy, norm_w, scale)


def wgrad(at, b, name, bn, after):
    m, t = at.shape
    n = b.shape[1]
    tk = min(t, 2048)
    bm = min(m, 1024)

    def body(a_ref, b_ref, after_ref, o_ref):
        part = _dot(a_ref[...], b_ref[...])

        @pl.when(pl.program_id(2) == 0)
        def _():
            o_ref[...] = part

        @pl.when(pl.program_id(2) > 0)
        def _():
            o_ref[...] += part

    return pl.pallas_call(
        body, name=name, grid=(m // bm, n // bn, t // tk),
        in_specs=[pl.BlockSpec((bm, tk), lambda i, j, k: (i, k)), pl.BlockSpec((tk, bn), lambda i, j, k: (k, j)), ANY],
        out_specs=pl.BlockSpec((bm, bn), lambda i, j, k: (i, j)),
        out_shape=jax.ShapeDtypeStruct((m, n), F32),
        compiler_params=_cp(("parallel", "parallel", "arbitrary")),
    )(at, b, after)


SUM_TR = 256


def pair_sum(g, core, theirs, name):
    w = g.shape[2]
    nh = HROWS // SUM_TR

    def body(core_ref, a_ref, b_ref, o_ref, ob_ref):
        s = a_ref[...] + b_ref[...]
        o_ref[...] = s
        ob_ref[...] = s.astype(BF16)

    spec = pl.BlockSpec((1, SUM_TR, w), lambda d, i, c: (d, i, 0))
    return pl.pallas_call(
        body, name=name,
        out_shape=[jax.ShapeDtypeStruct((4, HROWS, w), F32), jax.ShapeDtypeStruct((4, HROWS, w), BF16)],
        grid_spec=pltpu.PrefetchScalarGridSpec(
            num_scalar_prefetch=1, grid=(4, nh),
            in_specs=[pl.BlockSpec((1, SUM_TR, w), lambda d, i, c: (d, c[0] * nh + i, 0)), spec],
            out_specs=[spec, spec]),
        compiler_params=_cp(("parallel", "parallel")))(core.reshape(1).astype(jnp.int32), g, theirs)


def chip_sum(part, chip, others, name):
    r, w = part.shape[1:]

    def body(chip_ref, a_ref, b_ref, o_ref):
        acc = a_ref[0]
        for k in range(3):
            acc = acc + b_ref[k].astype(F32)
        o_ref[...] = acc

    return pl.pallas_call(
        body, name=name, out_shape=jax.ShapeDtypeStruct((r, w), F32),
        grid_spec=pltpu.PrefetchScalarGridSpec(
            num_scalar_prefetch=1, grid=(r // SUM_TR,),
            in_specs=[pl.BlockSpec((1, SUM_TR, w), lambda i, c: (c[0], i, 0)),
                      pl.BlockSpec((3, SUM_TR, w), lambda i, c: (0, i, 0))],
            out_specs=pl.BlockSpec((SUM_TR, w), lambda i, c: (i, 0))),
        compiler_params=_cp(("parallel",)))(chip.reshape(1).astype(jnp.int32), part, others)


def sum_devices(g):
    r = g.shape[1]

    def body(g_ref, o_ref):
        acc = g_ref[0]
        for d in range(1, 8):
            acc = acc + g_ref[d]
        o_ref[...] = acc

    return pl.pallas_call(body, name="sum_devices", out_shape=jax.ShapeDtypeStruct((r, 1024), F32),
                          compiler_params=_cp())(g)


def adamw(w, g, m, v, name):
    r, c = w.shape
    tr = r
    for cand in (256, 128, 64, 32, 16, 8):
        if r % cand == 0 and r > cand:
            tr = cand
            break

    def body(w_ref, g_ref, m_ref, v_ref, d_ref, nm_ref, nv_ref):
        gv = g_ref[...]
        mn = ADAM_B1 * m_ref[...] + (1.0 - ADAM_B1) * gv
        vn = ADAM_B2 * v_ref[...] + (1.0 - ADAM_B2) * (gv * gv)
        m_hat = mn / (1.0 - ADAM_B1 ** ADAM_STEP)
        v_hat = vn / (1.0 - ADAM_B2 ** ADAM_STEP)
        d_ref[...] = -ADAM_LR * (m_hat / (jnp.sqrt(v_hat) + ADAM_EPS) + ADAM_WD * w_ref[...])
        nm_ref[...] = mn
        nv_ref[...] = vn

    spec = pl.BlockSpec((tr, c), lambda i: (i, 0))
    sd = jax.ShapeDtypeStruct((r, c), F32)
    return pl.pallas_call(body, name=name, grid=(r // tr,), in_specs=[spec] * 4, out_specs=[spec] * 3,
                          out_shape=[sd, sd, sd], compiler_params=_cp(("parallel",)))(w, g, m, v)


def adamw_halves(w, mine, theirs, core, m, v, name):
    r, c = w.shape
    tr = 128
    nh = HROWS // tr

    def body(core_ref, w_ref, a_ref, b_ref, m_ref, v_ref, g_ref, d_ref, nm_ref, nv_ref):
        gv = jnp.where(pl.program_id(0) // nh == core_ref[0], a_ref[...], b_ref[...])
        mn = ADAM_B1 * m_ref[...] + (1.0 - ADAM_B1) * gv
        vn = ADAM_B2 * v_ref[...] + (1.0 - ADAM_B2) * (gv * gv)
        m_hat = mn / (1.0 - ADAM_B1 ** ADAM_STEP)
        v_hat = vn / (1.0 - ADAM_B2 ** ADAM_STEP)
        g_ref[...] = gv
        d_ref[...] = -ADAM_LR * (m_hat / (jnp.sqrt(v_hat) + ADAM_EPS) + ADAM_WD * w_ref[...])
        nm_ref[...] = mn
        nv_ref[...] = vn

    spec = pl.BlockSpec((tr, c), lambda i, s: (i, 0))
    half = pl.BlockSpec((tr, c), lambda i, s: (i % nh, 0))
    sd = jax.ShapeDtypeStruct((r, c), F32)
    return pl.pallas_call(
        body, name=name, out_shape=[sd, sd, sd, sd],
        grid_spec=pltpu.PrefetchScalarGridSpec(num_scalar_prefetch=1, grid=(r // tr,),
                                               in_specs=[spec, half, half, spec, spec], out_specs=[spec] * 4),
        compiler_params=_cp(("parallel",)))(core.reshape(1).astype(jnp.int32), w, mine, theirs, m, v)


ANY = pl.BlockSpec(memory_space=pl.ANY)
VM = pl.BlockSpec(memory_space=pltpu.VMEM)
OTHER_CHIPS = ((1, 0), (0, 1), (1, 1))


def _pos():
    return lax.axis_index("x"), lax.axis_index("y"), lax.axis_index("c")


def _flip(v, bit):
    return 1 - v if bit else v


def _rcopy(src, dst, ssem, rsem, peer):
    return pltpu.make_async_remote_copy(src_ref=src, dst_ref=dst, send_sem=ssem, recv_sem=rsem,
                                        device_id=peer, device_id_type=MESH)


def allgather_small(p, name):
    r = p.shape[0]

    def body(in_ref, out_ref, ssem, rsem, lsem):
        x, y, c = _pos()
        me = 4 * x + 2 * y + c
        loc = pltpu.make_async_copy(in_ref, out_ref.at[me], lsem)
        loc.start()
        sends = []
        peers = []
        for k in range(1, 8):
            px, py, pc = _flip(x, (k >> 2) & 1), _flip(y, (k >> 1) & 1), _flip(c, k & 1)
            peers.append((px, py, pc))
            cp = _rcopy(in_ref, out_ref.at[me], ssem.at[k - 1], rsem.at[k - 1], (px, py, pc))
            cp.start()
            sends.append(cp)
        for k in range(1, 8):
            px, py, pc = peers[k - 1]
            _rcopy(in_ref, out_ref.at[4 * px + 2 * py + pc], ssem.at[k - 1], rsem.at[k - 1], (px, py, pc)).wait_recv()
        for cp in sends:
            cp.wait_send()
        loc.wait()

    return pl.pallas_call(
        body, name=name, out_shape=jax.ShapeDtypeStruct((8, r, 1024), F32),
        in_specs=[VM], out_specs=VM,
        scratch_shapes=[pltpu.SemaphoreType.DMA((7,)), pltpu.SemaphoreType.DMA((7,)), pltpu.SemaphoreType.DMA],
    )(p)


def gather_weights(w_in_b, mod_sh):
    def body(wi_ref, m_ref, gi_ref, mo_ref, ssem, rsem, lsem):
        x, y, c = _pos()
        chip = 2 * x + y
        mine = pl.ds(pl.multiple_of(c * HROWS, 16), HROWS)
        other = pl.ds(pl.multiple_of((1 - c) * HROWS, 16), HROWS)
        sib = (x, y, 1 - c)
        pairs = ((wi_ref, gi_ref),)
        loc_m = pltpu.make_async_copy(m_ref, mo_ref.at[chip], lsem)
        loc_m.start()
        sends = []
        for k, (fx, fy) in enumerate(OTHER_CHIPS):
            peer = (_flip(x, fx), _flip(y, fy), c)
            for a, (w_ref, g_ref) in enumerate(pairs):
                cw = _rcopy(w_ref.at[mine], g_ref.at[chip, mine], ssem.at[6 * a + k], rsem.at[6 * a + k], peer)
                cw.start()
                sends.append(cw)
            cm = _rcopy(m_ref, mo_ref.at[chip], ssem.at[12 + k], rsem.at[12 + k], peer)
            cm.start()
            sends.append(cm)
        for k, (fx, fy) in enumerate(OTHER_CHIPS):
            px, py = _flip(x, fx), _flip(y, fy)
            for a, (w_ref, g_ref) in enumerate(pairs):
                got = g_ref.at[2 * px + py, mine]
                _rcopy(w_ref.at[mine], got, ssem.at[6 * a + k], rsem.at[6 * a + k], (px, py, c)).wait_recv()
                fw = _rcopy(got, got, ssem.at[6 * a + 3 + k], rsem.at[6 * a + 3 + k], sib)
                fw.start()
                sends.append(fw)
        for k, (fx, fy) in enumerate(OTHER_CHIPS):
            px, py = _flip(x, fx), _flip(y, fy)
            for a, (w_ref, g_ref) in enumerate(pairs):
                land = g_ref.at[2 * px + py, other]
                _rcopy(land, land, ssem.at[6 * a + 3 + k], rsem.at[6 * a + 3 + k], sib).wait_recv()
            _rcopy(m_ref, mo_ref.at[2 * px + py], ssem.at[12 + k], rsem.at[12 + k], (px, py, c)).wait_recv()
        for cp in sends:
            cp.wait_send()
        loc_m.wait()

    return pl.pallas_call(
        body, name="gather_weights",
        out_shape=[jax.ShapeDtypeStruct((4, D, SH_IN), BF16), jax.ShapeDtypeStruct((4, 8, 768), F32)],
        in_specs=[ANY, VM], out_specs=[ANY, VM],
        scratch_shapes=[pltpu.SemaphoreType.DMA((15,)), pltpu.SemaphoreType.DMA((15,)), pltpu.SemaphoreType.DMA],
    )(w_in_b, mod_sh)


def pair_exchange(g):
    def body(g_ref, r_ref, ssem, rsem):
        x, y, c = _pos()
        other = pl.ds(pl.multiple_of((1 - c) * HROWS, 8), HROWS)
        cp = _rcopy(g_ref.at[:, other, :], r_ref, ssem, rsem, (x, y, 1 - c))
        cp.start()
        cp.wait()

    return pl.pallas_call(
        body, name="pair_exchange", out_shape=jax.ShapeDtypeStruct((4, HROWS, g.shape[2]), F32),
        in_specs=[ANY], out_specs=ANY,
        scratch_shapes=[pltpu.SemaphoreType.DMA, pltpu.SemaphoreType.DMA],
    )(g)


HBM = pl.BlockSpec(memory_space=pltpu.HBM)
SEM = pl.BlockSpec(memory_space=pltpu.SEMAPHORE)
DATAFLOW = pltpu.SideEffectType.DATAFLOW_SIDE_EFFECTING


def split_start(name, make_copies, srcs, lands, nsem, after):
    arrays = [*srcs, *lands]
    n, ns = len(arrays), len(srcs)

    def body(*refs):
        for cp in make_copies(refs[:ns], refs[ns:n], refs[n + 1], refs[n + 2])[0]:
            cp.start()
        refs[-1][...] = jnp.zeros_like(refs[-1])

    res = pl.pallas_call(
        body, name=name,
        out_shape=(pltpu.SemaphoreType.DMA((nsem,)), pltpu.SemaphoreType.DMA((nsem,)),
                   *[pltpu.HBM(a.shape, a.dtype) for a in arrays], jax.ShapeDtypeStruct((8, 128), F32)),
        in_specs=(HBM,) * n + (ANY,), out_specs=(SEM, SEM) + (HBM,) * n + (VM,),
        input_output_aliases={i: 2 + i for i in range(n)},
        compiler_params=pltpu.CompilerParams(has_side_effects=DATAFLOW),
    )(*[pltpu.with_memory_space_constraint(a, pltpu.HBM) for a in arrays], after)
    return res[0], res[1], list(res[2:2 + n]), res[-1]


def split_wait(name, make_copies, ssem, rsem, arrays, ns, after):
    n = len(arrays)

    def body(*refs):
        sends, recvs = make_copies(refs[:ns], refs[ns:n], refs[n], refs[n + 1])
        for cp in sends:
            cp.wait_send()
        for cp in recvs:
            cp.wait_recv()

    return pl.pallas_call(
        body, name=name, out_shape=tuple(pltpu.HBM(a.shape, a.dtype) for a in arrays),
        in_specs=(HBM,) * n + (SEM, SEM, ANY), out_specs=(HBM,) * n,
        input_output_aliases={i: i for i in range(n)},
        compiler_params=pltpu.CompilerParams(has_side_effects=DATAFLOW),
    )(*arrays, ssem, rsem, after)


def _chip_copies(srcs, lands, ssem, rsem):
    x, y, c = _pos()
    copies = []
    for k, (fx, fy) in enumerate(OTHER_CHIPS):
        px, py = _flip(x, fx), _flip(y, fy)
        for a, (p_ref, l_ref) in enumerate(zip(srcs, lands)):
            copies.append(_rcopy(p_ref.at[2 * px + py], l_ref.at[k], ssem.at[3 * a + k], rsem.at[3 * a + k], (px, py, c)))
    return copies, copies


def _pair_copies(srcs, lands, ssem, rsem):
    x, y, c = _pos()
    other = pl.ds(pl.multiple_of((1 - c) * HROWS, 8), HROWS)
    copies = [_rcopy(srcs[0].at[:, other, :], lands[0], ssem.at[0], rsem.at[0], (x, y, 1 - c))]
    return copies, copies


def _rest_copies(srcs, lands, ssem, rsem):
    x, y, c = _pos()
    chip = 2 * x + y
    mine = pl.ds(pl.multiple_of(c * HROWS, 16), HROWS)
    sends, recvs = [], []
    for k, (fx, fy) in enumerate(OTHER_CHIPS):
        px, py = _flip(x, fx), _flip(y, fy)
        for t in range(2):
            rows_t = pl.ds(t * HROWS, HROWS)
            sends.append(_rcopy(srcs[0].at[mine], lands[0].at[chip, mine], ssem.at[2 * k + t], rsem.at[2 * k + c],
                                (px, py, t)))
            recvs.append(_rcopy(srcs[0].at[rows_t], lands[0].at[2 * px + py, rows_t], ssem.at[2 * k + t],
                                rsem.at[2 * k + t], (px, py, t)))
    return sends, recvs


def _swap_copies(srcs, lands, ssem, rsem):
    x, y, c = _pos()
    copies = [_rcopy(s_ref, l_ref, ssem.at[a], rsem.at[a], (x, y, 1 - c))
              for a, (s_ref, l_ref) in enumerate(zip(srcs, lands))]
    return copies, copies


def _flat(v, width=1024):
    v = v.reshape(-1)
    n = -(-v.shape[0] // width) * width
    return jnp.pad(v, (0, n - v.shape[0]))


def _rows(parts, rows):
    flat = jnp.concatenate(parts)
    return jnp.pad(flat, (0, rows * 1024 - flat.shape[0])).reshape(rows, 1024)


def _pack_small(b_ada, norm_w, conv_b, ssm_norm_w, q_norm_w, k_norm_w, sinks, dt_bias, a_log, d_skip, rel_bias,
                extra=None, tail=(), rows=16):
    misc = [q_norm_w, k_norm_w, sinks, dt_bias, a_log, d_skip] + ([] if extra is None else [extra])
    parts = [_flat(b_ada), _flat(norm_w), _flat(conv_b), _flat(ssm_norm_w)] + [_flat(v, 128) for v in misc]
    parts.append(jnp.zeros(((8 - len(misc)) * 128,), F32))
    parts.append(_flat(rel_bias))
    parts.append(jnp.zeros((5 * 1024,), F32))
    return _rows(parts + [_flat(v) for v in tail], rows)


def _unpack_small(p):
    misc = p[9]
    return dict(b_ada=p[0:3].reshape(1, 3072), norm_w=p[3:4], conv_b=p[4:7].reshape(1, 3072),
                ssm_norm_w=p[7:9].reshape(1, 2048), q_norm_w=misc[None, 0:64], k_norm_w=misc[None, 128:192],
                sinks=misc[None, 256:272], dt_bias=misc[None, 384:416], a_log=misc[None, 512:544],
                d_skip=misc[None, 640:672], rel_bias=p[10, :512].reshape(32, 16), extra=misc[768])


SMALL = ("b_ada", "norm_w", "conv_b", "ssm_norm_w", "q_norm_w", "k_norm_w", "sinks", "dt_bias", "a_log", "d_skip",
         "rel_bias")
WEIGHTS = ("w_ada", "b_ada", "norm_w", "w_in", "q_norm_w", "k_norm_w", "rel_bias", "sinks", "conv_w", "conv_b",
           "dt_bias", "a_log", "d_skip", "ssm_norm_w", "w_attn_proj", "w_ssm_proj", "w_out")
IN_COLS = ((0, 1024, C_Q), (1024, 256, C_K), (1280, 256, C_V), (1536, 1024, C_ZA), (2560, 2048, C_ZM),
           (4608, 3072, C_XBC), (7680, 32, C_DT), (7712, 1024, C_GA), (8736, 1024, C_GB))


def _to_cat(shards):
    parts, pos = [], 0
    for o, n, cnew in sorted(IN_COLS, key=lambda e: e[2]):
        assert cnew == pos
        c0 = o
        while c0 < o + n:
            i = c0 // SH_IN
            c1 = min(o + n, (i + 1) * SH_IN)
            parts.append(shards[i][:, c0 - i * SH_IN:c1 - i * SH_IN])
            c0 = c1
        pos += n
    parts.append(jnp.zeros((D, NP - pos), shards.dtype))
    return jnp.concatenate(parts, axis=1)


def _from_cat(dw_pieces):
    starts = [subs[0][1] for _, subs in DPIECES]

    def cols(c0, c1):
        p = max(q for q in range(len(starts)) if starts[q] <= c0)
        return dw_pieces[p][:, c0 - starts[p]:c1 - starts[p]]

    shards = []
    for i in range(4):
        lo, hi = i * SH_IN, (i + 1) * SH_IN
        parts = []
        for o, n, cnew in IN_COLS:
            a, b = max(o, lo), min(o + n, hi)
            if a < b:
                parts.append(cols(cnew + a - o, cnew + b - o))
        shards.append(jnp.concatenate(parts, axis=1))
    return jnp.stack(shards)


def kernel(x, c, w_ada, b_ada, norm_w, w_in, q_norm_w, k_norm_w, rel_bias, sinks, conv_w, conv_b, dt_bias, a_log, d_skip, ssm_norm_w, w_attn_proj, w_ssm_proj, w_out, loss_target, m_w_ada, m_b_ada, m_norm_w, m_w_in, m_q_norm_w, m_k_norm_w, m_rel_bias, m_sinks, m_conv_w, m_conv_b, m_dt_bias, m_a_log, m_d_skip, m_ssm_norm_w, m_w_attn_proj, m_w_ssm_proj, m_w_out, v_w_ada, v_b_ada, v_norm_w, v_w_in, v_q_norm_w, v_k_norm_w, v_rel_bias, v_sinks, v_conv_w, v_conv_b, v_dt_bias, v_a_log, v_d_skip, v_ssm_norm_w, v_w_attn_proj, v_w_ssm_proj, v_w_out):
    args = dict(locals())
    xi, yi, ci = lax.axis_index("x"), lax.axis_index("y"), lax.axis_index("c")
    chip = 2 * xi + yi
    me = 4 * xi + 2 * yi + ci
    x2 = x[0]
    tgt = loss_target[0]

    pay = _rows([c.reshape(-1), conv_w[0].reshape(-1)], 8)
    g0 = allgather_small(pay, "gather_cond")
    c_all = g0[:, 0, :]
    conv_w_full = g0[0::2, 1:4, :].reshape(4, CONV_K, 768).transpose(1, 0, 2).reshape(CONV_K, XBC)

    b_ada_sh = lax.dynamic_slice(b_ada, (0, chip * 768), (1, 768))
    mod_sh = ada_mod(c_all, w_ada[0], b_ada_sh)

    w_in_b = w_in[0].astype(BF16)
    w_rest_b = jnp.concatenate([w_attn_proj[0], w_ssm_proj[0], w_out[0]], axis=0).astype(BF16)
    wg_in, modg = gather_weights(w_in_b, mod_sh)
    wg_in = lax.dynamic_update_slice(wg_in, w_in_b[None], (chip, 0, 0))
    rs_sem, rr_sem, rest_thru, rest_tok = split_start("gather_rest_start", _rest_copies, [w_rest_b],
                                                      [lax.empty((4, D, D), BF16)], 6, modg)
    mod = lax.dynamic_slice(modg, (0, me, 0), (4, 1, 768)).reshape(1, 3 * D)
    shift, scale, gate = mod[:, :D], mod[:, D:2 * D] + rest_tok[:1, :1], mod[:, 2 * D:]
    wcat = _to_cat(wg_in)

    pad128 = lambda v: jnp.pad(v, ((0, 0), (0, 128 - v.shape[1])))
    dtb_p, alog_p, dsk_p = pad128(dt_bias), pad128(a_log), pad128(d_skip)
    bucket = _bucket_table()

    proj, dt_raw, h_t = norm_proj(x2, norm_w, scale, shift, wcat)
    biasm = bias_expand(rel_bias, sinks, bucket)
    ao = attn_fwd(proj, biasm, q_norm_w, k_norm_w)
    act, dsl = conv_fwd(proj, conv_w_full, conv_b)
    yss, sprev = ssd_fwd(act, dt_raw, dtb_p, alog_p, dsk_p)

    w_rest_b, wg_rest = split_wait("gather_rest_wait", _rest_copies, rs_sem, rr_sem, rest_thru, 1, yss)
    wg_rest = lax.dynamic_update_slice(wg_rest, w_rest_b[None], (chip, 0, 0))
    w_at = wg_rest[:, :R_AT].reshape(D, D)
    w_ss = wg_rest[:, R_AT:R_AT + R_SS].reshape(SSM_W, D)
    w_ou = wg_rest[:, R_AT + R_SS:].reshape(D, D)
    (loss_p, dy, dao, dmid, dyss, ua_t, yn_t, mg_t, dya, dyb, dout, dgate, dssm_nw) = tail(
        proj, ao, yss, x2, tgt, gate, ssm_norm_w, w_at, w_ss, w_ou)

    dq, dkv, dqw, dkw, dacc = attn_bwd(proj, dao, biasm, q_norm_w, k_norm_w)
    dbias = bias_reduce(dacc, bucket)
    drb = dbias[:, :NBUCKET].T
    dsk = dbias[:, NBUCKET].reshape(1, HQ)
    dact, ddt, ddtb, dalog, ddskip = ssd_bwd(act, dt_raw, dyss, sprev, dtb_p, alog_p, dsk_p)
    dxbc, dconv_w, dconv_b = conv_bwd(proj, dact, dsl, conv_w_full)

    dproj = (dq, dmid, dxbc, dkv, ddt)
    dwcat = [wgrad(h_t, piece, "dw_in_%d" % p, 1280 if piece.shape[1] == W_MID else min(piece.shape[1], 1024), rest_tok)
             for p, piece in enumerate(dproj)]

    g_in = _from_cat(dwcat)
    ps_sem, pr_sem, pair_thru, pair_tok = split_start("pair_in_start", _pair_copies, [g_in],
                                                      [lax.empty((4, HROWS, SH_IN), F32)], 1, loss_p)
    dw_at = wgrad(ua_t, dya, "dw_attn", 1024, pair_tok)
    dw_ss = wgrad(yn_t, dyb, "dw_ssm", 1024, pair_tok)
    dw_ou = wgrad(mg_t, dout, "dw_out", 1024, pair_tok)
    g_rest = jnp.concatenate([dw_at.reshape(4, R_AT, D), dw_ss.reshape(4, R_SS, D), dw_ou.reshape(4, R_OU, D)], axis=1)
    sib_rest = pair_exchange(g_rest)
    g_in, sib_in = split_wait("pair_in_wait", _pair_copies, ps_sem, pr_sem, pair_thru, 1, sib_rest)
    part_in, pb_in = pair_sum(g_in, ci, sib_in, "pair_sum_in")
    part_rest, pb_rest = pair_sum(g_rest, ci, sib_rest, "pair_sum_rest")
    cs_sem, cr_sem, chip_thru, token = split_start(
        "chip_exchange_start", _chip_copies, [pb_in, pb_rest],
        [lax.empty((3, HROWS, SH_IN), BF16), lax.empty((3, HROWS, D), BF16)], 6, part_rest)
    grad_x, dnorm_w, dscale, dshift = dproj_bwd(dproj, wcat, x2, dy, norm_w, scale + token[:1, :1])
    _, _, oth_in, oth_rest = split_wait("chip_exchange_wait", _chip_copies, cs_sem, cr_sem, chip_thru, 2, dshift)
    red_in = chip_sum(part_in, chip, oth_in, "chip_sum_in")
    red_rest = chip_sum(part_rest, chip, oth_rest, "chip_sum_rest")
    sw_ssem, sw_rsem, swap_thru, swap_tok = split_start(
        "pair_swap_start", _swap_copies, [red_in, red_rest],
        [lax.empty((HROWS, SH_IN), F32), lax.empty((HROWS, D), F32)], 2, red_rest)

    dmod = jnp.concatenate([dshift, dscale, dgate], axis=1)
    gsmall = _pack_small(dmod, dnorm_w, dconv_b, dssm_nw, dqw, dkw, dsk[:, :HQ], ddtb[:, :SH], dalog[:, :SH],
                         ddskip[:, :SH], drb, extra=loss_p[:, :1] + swap_tok[:1, :1], tail=(dconv_w,), rows=32)
    gall = allgather_small(gsmall, "gather_small_grads")
    ssum = sum_devices(gall)
    gs = _unpack_small(ssum[:16])
    loss = gs["extra"]
    dconv_w_sh = lax.dynamic_slice(ssum[16:28].reshape(CONV_K, XBC), (0, chip * 768), (CONV_K, 768))
    dmod_all = gall[:, 0:3, :].reshape(8, 3 * D)
    dw_ada = ada_grad(c_all, lax.dynamic_slice(dmod_all, (0, chip * 768), (8, 768)))

    grads = dict(gs)
    grads["w_ada"] = dw_ada
    grads["conv_w"] = dconv_w_sh

    delta, new_m, new_v = {}, {}, {}

    def step(n):
        delta[n], new_m[n], new_v[n] = adamw(args[n][0], grads[n], args["m_" + n][0], args["v_" + n][0], "adamw_" + n)

    step("w_ada")
    step("conv_w")
    ws = _pack_small(*[args[n] for n in SMALL])
    ms = _pack_small(*[args["m_" + n] for n in SMALL])
    vs = _pack_small(*[args["v_" + n] for n in SMALL])
    d_s, m_s, v_s = adamw(ws, ssum[:16], ms, vs, "adamw_small")
    red_in, red_rest, recv_in, recv_rest = split_wait("pair_swap_wait", _swap_copies, sw_ssem, sw_rsem, swap_thru, 2, d_s)
    d_s, m_s, v_s = _unpack_small(d_s), _unpack_small(m_s), _unpack_small(v_s)
    for n in SMALL:
        delta[n], new_m[n], new_v[n] = d_s[n], m_s[n], v_s[n]
    grads["w_in"], delta["w_in"], new_m["w_in"], new_v["w_in"] = adamw_halves(
        w_in[0], red_in, recv_in, ci, m_w_in[0], v_w_in[0], "adamw_w_in")
    g_shard_rest = jnp.concatenate([jnp.where(ci == 0, red_rest, recv_rest), jnp.where(ci == 0, recv_rest, red_rest)],
                                   axis=0)
    grads["w_attn_proj"] = g_shard_rest[:R_AT]
    grads["w_ssm_proj"] = g_shard_rest[R_AT:R_AT + R_SS]
    grads["w_out"] = g_shard_rest[R_AT + R_SS:]
    for n in ("w_attn_proj", "w_ssm_proj", "w_out"):
        step(n)

    def shaped(n, a):
        return a.reshape(args[n].shape)

    outs = [loss, grad_x[None]]
    for table in (grads, delta, new_m, new_v):
        outs += [shaped(n, table[n]) for n in WEIGHTS]
    return tuple(outs)
```

```python
import functools
import math

import numpy as np
import jax
import jax.numpy as jnp
from jax import lax
from jax.experimental import pallas as pl
from jax.experimental.pallas import tpu as pltpu

F32 = jnp.float32
BF16 = jnp.bfloat16
MESH = pl.DeviceIdType.MESH

D = 1024
HQ, HKV, GRP, DH = 16, 4, 4, 64
BLK = 128
NBUCKET, MAXDIST = 32, 128
SSM_W, SH, SG, SR, SP, SN = 2048, 32, 4, 8, 64, 128
CONV_K = 4
XBC = SSM_W + 2 * SG * SN
IN_W = 9760
EPS = 1e-6
NEG = -1e30
SCALE = DH ** -0.5

C_Q, C_ZA, C_GA, C_GB, C_ZM, C_XBC, C_K, C_V, C_DT = 0, 1024, 2048, 3072, 4096, 6144, 9216, 9472, 9728
NP = 9984
TN = 3328
W_MID = C_XBC - C_ZA

SH_IN = IN_W // 4
R_AT, R_SS, R_OU = 256, 512, 256
HROWS = D // 2

ADAM_LR, ADAM_B1, ADAM_B2, ADAM_EPS, ADAM_WD, ADAM_STEP = 0.001, 0.9, 0.999, 1e-08, 0.01, 10

VMEM_LIMIT = 56 * 1024 * 1024


def _cp(sem=None):
    if sem is None:
        return pltpu.CompilerParams(vmem_limit_bytes=VMEM_LIMIT)
    return pltpu.CompilerParams(dimension_semantics=sem, vmem_limit_bytes=VMEM_LIMIT)


def _sig(x):
    return 0.5 * jnp.tanh(0.5 * x) + 0.5


def _dot(a, b):
    return jnp.dot(a, b, preferred_element_type=F32)


def _dot_nt(a, b):
    return lax.dot_general(a, b, (((1,), (1,)), ((), ())), preferred_element_type=F32)


def _dot_tn(a, b):
    return lax.dot_general(a, b, (((0,), (0,)), ((), ())), preferred_element_type=F32)


def _rsum(x):
    return jnp.sum(x, axis=-1, keepdims=True)


def _csum(x):
    return jnp.sum(x, axis=0, keepdims=True)


def _asum(x):
    return _csum(_rsum(x))


def _full(shape):
    nd = len(shape)
    return pl.BlockSpec(shape, lambda *_: (0,) * nd)


def ada_mod(c_all, w_ada_sh, b_ada_sh):
    def body(c_ref, w_ref, b_ref, o_ref):
        cv = c_ref[...]
        s = cv * _sig(cv)
        o_ref[...] = jnp.dot(s, w_ref[...], preferred_element_type=F32,
                             precision=lax.Precision.HIGHEST) + b_ref[...]

    n = w_ada_sh.shape[1]
    return pl.pallas_call(body, name="ada_mod", out_shape=jax.ShapeDtypeStruct((8, n), F32),
                          compiler_params=_cp())(c_all, w_ada_sh, b_ada_sh)


def ada_grad(c_all, dmod_sh):
    def body(c_ref, d_ref, o_ref):
        cv = c_ref[...]
        s = cv * _sig(cv)
        o_ref[...] = lax.dot_general(s, d_ref[...], (((0,), (0,)), ((), ())), preferred_element_type=F32,
                                     precision=lax.Precision.HIGHEST)

    n = dmod_sh.shape[1]
    return pl.pallas_call(body, name="ada_grad", out_shape=jax.ShapeDtypeStruct((D, n), F32),
                          compiler_params=_cp())(c_all, dmod_sh)


def norm_proj(x, norm_w, scale, shift, wcat):
    t = x.shape[0]
    tm = min(t, 1024)

    def body(x_ref, nw_ref, sc_ref, sh_ref, w_ref, p_ref, dt_ref, ht_ref, hs):
        @pl.when(pl.program_id(1) == 0)
        def _():
            xv = x_ref[...]
            r = lax.rsqrt(jnp.mean(xv * xv, axis=-1, keepdims=True) + EPS)
            h = (xv * r) * nw_ref[...]
            h = h * (1.0 + sc_ref[...]) + sh_ref[...]
            hs[...] = h.astype(BF16)
            ht_ref[...] = h.T.astype(BF16)

        p = _dot(hs[...], w_ref[...])
        p_ref[...] = p.astype(BF16)

        @pl.when(pl.program_id(1) == C_DT // TN)
        def _():
            dt_ref[...] = p[:, C_DT % TN:C_DT % TN + 128]

    vec = pl.BlockSpec((1, D), lambda i, j: (0, 0))
    return pl.pallas_call(
        body, name="norm_proj", grid=(t // tm, NP // TN),
        in_specs=[pl.BlockSpec((tm, D), lambda i, j: (i, 0)), vec, vec, vec,
                  pl.BlockSpec((D, TN), lambda i, j: (0, j))],
        out_specs=[pl.BlockSpec((tm, TN), lambda i, j: (i, j)), pl.BlockSpec((tm, 128), lambda i, j: (i, 0)),
                   pl.BlockSpec((D, tm), lambda i, j: (0, i))],
        out_shape=[jax.ShapeDtypeStruct((t, NP), BF16), jax.ShapeDtypeStruct((t, 128), F32),
                   jax.ShapeDtypeStruct((D, t), BF16)],
        scratch_shapes=[pltpu.VMEM((tm, D), BF16)],
        compiler_params=_cp(("parallel", "arbitrary")),
    )(x, norm_w, scale, shift, wcat)


def _bucket_table():
    qi = np.arange(BLK)[:, None]
    kj = np.arange(2 * BLK)[None, :]
    dist = qi + BLK - kj
    n = np.maximum(dist, 0)
    max_exact = NBUCKET // 2
    nf = np.maximum(n, 1).astype(np.float32)
    large = max_exact + (np.log(nf / np.float32(max_exact)) / np.float32(math.log(MAXDIST / max_exact))
                         * np.float32(NBUCKET - max_exact)).astype(np.int32)
    large = np.minimum(large, NBUCKET - 1)
    bucket = np.where(n < max_exact, n, large).astype(np.int32)
    valid = (dist >= 0) & (dist < BLK)
    return np.where(valid, bucket, -1).astype(np.int32)


def bias_expand(rel_bias, sinks, bucket):
    def body(rb_ref, sk_ref, bk_ref, o_ref):
        bk = bk_ref[...]
        col = lax.broadcasted_iota(jnp.int32, (BLK, 2 * BLK), 1)

        def head(hd, carry):
            def step(b, acc):
                return jnp.where(bk == b, rb_ref[b, hd], acc)

            acc = lax.fori_loop(0, NBUCKET, step, jnp.full((BLK, 2 * BLK), NEG, F32))
            acc = jnp.where(col == 0, sk_ref[0, hd], acc)
            o_ref[1, hd] = acc
            o_ref[0, hd] = jnp.where(jnp.logical_and(col > 0, col < BLK), NEG, acc)
            return carry

        lax.fori_loop(0, HQ, head, 0)

    smem = pl.BlockSpec(memory_space=pltpu.SMEM)
    return pl.pallas_call(
        body, name="bias_expand", in_specs=[smem, smem, VM], out_specs=VM,
        out_shape=jax.ShapeDtypeStruct((2, HQ, BLK, 2 * BLK), F32), compiler_params=_cp(),
    )(rel_bias, sinks, jnp.asarray(bucket))


def bias_reduce(dacc, bucket):
    col = np.arange(BLK * 2 * BLK) % (2 * BLK)
    lane = np.arange(128)[None, :]
    member = (bucket.reshape(-1)[:, None] == lane) | ((col[:, None] == 0) & (lane == NBUCKET))

    def body(d_ref, m_ref, o_ref):
        mm = m_ref[...]
        o_ref[...] = sum(_dot(part, mm) for part in _split3(d_ref[...]))

    return pl.pallas_call(body, name="bias_reduce", out_shape=jax.ShapeDtypeStruct((HQ, 128), F32),
                          compiler_params=_cp())(dacc.reshape(HQ, BLK * 2 * BLK), jnp.asarray(member, BF16))


GQ = GRP * BLK


def _stack_heads(x, nh):
    return jnp.concatenate([x[:, DH * h:DH * (h + 1)] for h in range(nh)], axis=0)


def _unstack(xs, nh):
    rows = xs.shape[0] // nh
    return jnp.concatenate([xs[rows * h:rows * (h + 1)] for h in range(nh)], axis=1)


def _rms(x):
    return lax.rsqrt(jnp.mean(x * x, axis=-1, keepdims=True) + EPS)


def _stack_q(q, qw):
    qs = _stack_heads(q, HQ)
    r = _rms(qs)
    qhat = qs * r
    return qhat * qw, qhat, r


def _band_first(shape):
    return (lax.broadcasted_iota(jnp.int32, shape, 0) & (2 * BLK - 1)) == 0


def _stack_kv(kp, kc, vp, vc, kw):
    ks = _stack_heads(jnp.concatenate([kp, kc], axis=0), HKV)
    r = _rms(ks)
    khat = ks * r
    first = _band_first(ks.shape)
    kn = jnp.where(first, 0.0, khat * kw)
    v2 = jnp.where(first, 0.0, _stack_heads(jnp.concatenate([vp, vc], axis=0), HKV)).astype(BF16)
    return kn, khat, r, v2


def _softmax_rows(s):
    p = jnp.exp(s - jnp.max(s, axis=-1, keepdims=True))
    return p * (1.0 / _rsum(p))


def attn_fwd(proj, biasm, q_norm_w, k_norm_w):
    t = proj.shape[0]
    nb = t // BLK

    def body(q_ref, kc_ref, kp_ref, vc_ref, vp_ref, bm_ref, qw_ref, kw_ref, o_ref):
        f = lambda ref: ref[...].astype(F32)
        qn = _stack_q(f(q_ref), qw_ref[...])[0].astype(BF16)
        kn, _, _, v2 = _stack_kv(f(kp_ref), f(kc_ref), f(vp_ref), f(vc_ref), kw_ref[...])
        knb = kn.astype(BF16)
        s = jnp.concatenate([_dot_nt(qn[GQ * j:GQ * (j + 1)], knb[2 * BLK * j:2 * BLK * (j + 1)])
                             for j in range(HKV)], axis=0)
        pr = _softmax_rows(s * SCALE + bm_ref[0].reshape(HQ * BLK, 2 * BLK)).astype(BF16)
        o = jnp.concatenate([_dot(pr[GQ * j:GQ * (j + 1)], v2[2 * BLK * j:2 * BLK * (j + 1)])
                             for j in range(HKV)], axis=0)
        o_ref[...] = _unstack(o, HQ).astype(BF16)

    kblk, vblk = C_K // 256, C_V // 256
    prev = lambda n: jnp.maximum(n - 1, 0)
    return pl.pallas_call(
        body, name="attn_fwd", grid=(nb,),
        in_specs=[pl.BlockSpec((BLK, D), lambda n: (n, 0)),
                  pl.BlockSpec((BLK, 256), lambda n: (n, kblk)),
                  pl.BlockSpec((BLK, 256), lambda n: (prev(n), kblk)),
                  pl.BlockSpec((BLK, 256), lambda n: (n, vblk)),
                  pl.BlockSpec((BLK, 256), lambda n: (prev(n), vblk)),
                  pl.BlockSpec((1, HQ, BLK, 2 * BLK), lambda n: (jnp.minimum(n, 1), 0, 0, 0)),
                  _full((1, DH)), _full((1, DH))],
        out_specs=pl.BlockSpec((BLK, D), lambda n: (n, 0)),
        out_shape=jax.ShapeDtypeStruct((t, D), BF16),
        compiler_params=_cp(("parallel",)),
    )(proj, proj, proj, proj, proj, biasm, q_norm_w, k_norm_w)


def attn_bwd(proj, dao, biasm, q_norm_w, k_norm_w):
    t = proj.shape[0]
    nb = t // BLK
    kb = 2 * BLK

    def body(q_ref, kc_ref, kp_ref, vc_ref, vp_ref, do_ref, bm_ref, qw_ref, kw_ref,
             dq_ref, dkv_ref, dqw_ref, dkw_ref, dacc_ref, ck, cv, pk, pv, nk, nv):
        n = pl.program_id(0)

        @pl.when(n == 0)
        def _():
            for ref in (dqw_ref, dkw_ref, dacc_ref, ck, cv):
                ref[...] = jnp.zeros_like(ref)

        qw = qw_ref[...]
        kw = kw_ref[...]
        f = lambda ref: ref[...].astype(F32)
        kn, khat, rk, v2 = _stack_kv(f(kp_ref), f(kc_ref), f(vp_ref), f(vc_ref), kw)
        grp = lambda a, j: a[GQ * j:GQ * (j + 1)]
        band = lambda a, j: a[kb * j:kb * (j + 1)]

        @pl.when(n < nb)
        def _():
            qn, qhat, rq = _stack_q(f(q_ref), qw)
            qnb = qn.astype(BF16)
            knb = kn.astype(BF16)
            dos = _stack_heads(f(do_ref), HQ).astype(BF16)
            s = jnp.concatenate([_dot_nt(grp(qnb, j), band(knb, j)) for j in range(HKV)], axis=0)
            pr = _softmax_rows(s * SCALE + bm_ref[0].reshape(HQ * BLK, kb))
            dp = jnp.concatenate([_dot_nt(grp(dos, j), band(v2, j)) for j in range(HKV)], axis=0)
            ds = pr * (dp - _rsum(pr * dp))
            dacc_ref[...] += ds.reshape(HQ, BLK, kb)
            dsb = ds.astype(BF16)
            prb = pr.astype(BF16)
            dqn = jnp.concatenate([_dot(grp(dsb, j), band(knb, j)) for j in range(HKV)], axis=0) * SCALE
            dqhat = dqn * qw
            dq = rq * (dqhat - qhat * jnp.mean(dqhat * qhat, axis=-1, keepdims=True))
            dq_ref[...] = _unstack(dq, HQ).astype(BF16)
            dqw_ref[...] += _csum(dqn * qhat)
            first = _band_first((kb, DH))
            for j in range(HKV):
                rows = slice(BLK * j, BLK * (j + 1))
                dkn = jnp.where(first, 0.0, _dot_tn(grp(dsb, j), grp(qnb, j)) * SCALE)
                dvj = jnp.where(first, 0.0, _dot_tn(grp(prb, j), grp(dos, j)))
                pk[rows, :] = dkn[:BLK]
                nk[rows, :] = dkn[BLK:]
                pv[rows, :] = dvj[:BLK]
                nv[rows, :] = dvj[BLK:]

        @pl.when(n == nb)
        def _():
            for ref in (pk, pv, nk, nv):
                ref[...] = jnp.zeros_like(ref)

        khp = jnp.concatenate([khat[kb * j:kb * j + BLK] for j in range(HKV)], axis=0)
        rkp = jnp.concatenate([rk[kb * j:kb * j + BLK] for j in range(HKV)], axis=0)
        dkn = ck[...] + pk[...]
        dkhat = dkn * kw
        dk = rkp * (dkhat - khp * jnp.mean(dkhat * khp, axis=-1, keepdims=True))
        dkw_ref[...] += _csum(dkn * khp)
        dkv_ref[...] = jnp.concatenate([_unstack(dk, HKV), _unstack(cv[...] + pv[...], HKV)], axis=1).astype(BF16)
        ck[...] = nk[...]
        cv[...] = nv[...]

    kblk, vblk = C_K // 256, C_V // 256
    cur = lambda n: jnp.minimum(n, nb - 1)
    prev = lambda n: jnp.maximum(n - 1, 0)
    carry = pltpu.VMEM((HKV * BLK, DH), F32)
    return pl.pallas_call(
        body, name="attn_bwd", grid=(nb + 1,),
        in_specs=[pl.BlockSpec((BLK, D), lambda n: (cur(n), 0)),
                  pl.BlockSpec((BLK, 256), lambda n: (cur(n), kblk)), pl.BlockSpec((BLK, 256), lambda n: (prev(n), kblk)),
                  pl.BlockSpec((BLK, 256), lambda n: (cur(n), vblk)), pl.BlockSpec((BLK, 256), lambda n: (prev(n), vblk)),
                  pl.BlockSpec((BLK, D), lambda n: (cur(n), 0)),
                  pl.BlockSpec((1, HQ, BLK, kb), lambda n: (jnp.minimum(n, 1), 0, 0, 0)),
                  _full((1, DH)), _full((1, DH))],
        out_specs=[pl.BlockSpec((BLK, D), lambda n: (cur(n), 0)),
                   pl.BlockSpec((BLK, 512), lambda n: (prev(n), 0)),
                   _full((1, DH)), _full((1, DH)), _full((HQ, BLK, kb))],
        out_shape=[jax.ShapeDtypeStruct((t, D), BF16), jax.ShapeDtypeStruct((t, 512), BF16),
                   jax.ShapeDtypeStruct((1, DH), F32),
                   jax.ShapeDtypeStruct((1, DH), F32), jax.ShapeDtypeStruct((HQ, BLK, kb), F32)],
        scratch_shapes=[carry] * 6,
        compiler_params=_cp(("arbitrary",)),
    )(proj, proj, proj, proj, proj, dao, biasm, q_norm_w, k_norm_w)


CONV_TM, CONV_CW, CONV_RC, HALO = 1024, 1024, 32, 16


def conv_fwd(proj, conv_w, conv_b):
    t = proj.shape[0]
    tm = min(t, CONV_TM)
    c0 = C_XBC // CONV_CW

    def body(x_ref, xp_ref, w_ref, b_ref, o_ref, ds_ref):
        i = pl.program_id(1)
        w = w_ref[...]
        b = b_ref[...]
        for r in range(tm // CONV_RC):
            lo = r * CONV_RC
            if r == 0:
                head = jnp.where(i == 0, 0.0, xp_ref[...].astype(F32))
                win = jnp.concatenate([head, x_ref[0:CONV_RC, :].astype(F32)], axis=0)
            else:
                win = x_ref[lo - HALO:lo + CONV_RC, :].astype(F32)
            acc = b
            for j in range(CONV_K):
                acc = acc + w[j:j + 1] * win[HALO - 3 + j:HALO - 3 + j + CONV_RC]
            sg = _sig(acc)
            o_ref[lo:lo + CONV_RC, :] = acc * sg
            ds_ref[lo:lo + CONV_RC, :] = _dsilu(acc, sg).astype(BF16)

    rh = tm // HALO
    tile = pl.BlockSpec((tm, CONV_CW), lambda s, i: (i, s))
    return pl.pallas_call(
        body, name="conv_fwd", grid=(XBC // CONV_CW, t // tm),
        in_specs=[pl.BlockSpec((tm, CONV_CW), lambda s, i: (i, c0 + s)),
                  pl.BlockSpec((HALO, CONV_CW), lambda s, i: (jnp.maximum(i * rh - 1, 0), c0 + s)),
                  pl.BlockSpec((CONV_K, CONV_CW), lambda s, i: (0, s)), pl.BlockSpec((1, CONV_CW), lambda s, i: (0, s))],
        out_specs=[tile, tile],
        out_shape=[jax.ShapeDtypeStruct((t, XBC), F32), jax.ShapeDtypeStruct((t, XBC), BF16)],
        compiler_params=_cp(("parallel", "parallel")),
    )(proj, proj, conv_w, conv_b)


def conv_bwd(proj, dact, dsl, conv_w):
    t = proj.shape[0]
    tm = min(t, CONV_TM)
    nt = t // tm
    nr = tm // CONV_RC
    c0 = C_XBC // CONV_CW
    ext = CONV_RC + 8

    def body(x_ref, xp_ref, d_ref, dn_ref, s_ref, sn_ref, w_ref, dx_ref, dw_ref, db_ref):
        i = pl.program_id(1)

        @pl.when(i == 0)
        def _():
            dw_ref[...] = jnp.zeros_like(dw_ref)
            db_ref[...] = jnp.zeros_like(db_ref)

        w = w_ref[...]
        dws = [jnp.zeros((1, CONV_CW), F32) for _ in range(CONV_K)]
        db = jnp.zeros((1, CONV_CW), F32)
        for r in range(nr):
            lo = r * CONV_RC
            if r == 0:
                head = jnp.where(i == 0, 0.0, xp_ref[...].astype(F32))
                win = jnp.concatenate([head, x_ref[0:CONV_RC, :].astype(F32)], axis=0)
            else:
                win = x_ref[lo - HALO:lo + CONV_RC, :].astype(F32)
            if r < nr - 1:
                dext = d_ref[lo:lo + ext, :]
                sext = s_ref[lo:lo + CONV_RC + HALO, :].astype(F32)[0:ext]
            else:
                dext = jnp.concatenate([d_ref[lo:lo + CONV_RC, :], jnp.where(i == nt - 1, 0.0, dn_ref[...])], axis=0)
                sext = jnp.concatenate([s_ref[lo:lo + CONV_RC, :].astype(F32), sn_ref[...].astype(F32)], axis=0)[0:ext]
            dpre = dext * sext
            dx = jnp.zeros((CONV_RC, CONV_CW), F32)
            own = dpre[0:CONV_RC]
            for j in range(CONV_K):
                dx = dx + w[j:j + 1] * dpre[3 - j:3 - j + CONV_RC]
                dws[j] = dws[j] + _csum(own * win[HALO - 3 + j:HALO - 3 + j + CONV_RC])
            db = db + _csum(own)
            dx_ref[lo:lo + CONV_RC, :] = dx.astype(BF16)
        dw_ref[...] += jnp.concatenate(dws, axis=0)
        db_ref[...] += db

    rh = tm // HALO
    r8 = tm // 8
    nxt = lambda i, per: jnp.minimum((i + 1) * per, nt * per - 1)
    return pl.pallas_call(
        body, name="conv_bwd", grid=(XBC // CONV_CW, nt),
        in_specs=[pl.BlockSpec((tm, CONV_CW), lambda s, i: (i, c0 + s)),
                  pl.BlockSpec((HALO, CONV_CW), lambda s, i: (jnp.maximum(i * rh - 1, 0), c0 + s)),
                  pl.BlockSpec((tm, CONV_CW), lambda s, i: (i, s)),
                  pl.BlockSpec((8, CONV_CW), lambda s, i: (nxt(i, r8), s)),
                  pl.BlockSpec((tm, CONV_CW), lambda s, i: (i, s)),
                  pl.BlockSpec((HALO, CONV_CW), lambda s, i: (nxt(i, rh), s)),
                  pl.BlockSpec((CONV_K, CONV_CW), lambda s, i: (0, s))],
        out_specs=[pl.BlockSpec((tm, CONV_CW), lambda s, i: (i, s)),
                   pl.BlockSpec((CONV_K, CONV_CW), lambda s, i: (0, s)), pl.BlockSpec((1, CONV_CW), lambda s, i: (0, s))],
        out_shape=[jax.ShapeDtypeStruct((t, XBC), BF16), jax.ShapeDtypeStruct((CONV_K, XBC), F32),
                   jax.ShapeDtypeStruct((1, XBC), F32)],
        compiler_params=_cp(("parallel", "arbitrary")),
    )(proj, proj, dact, dact, dsl, dsl, conv_w)


def _split3(x):
    h = x.astype(BF16)
    r = x - h.astype(F32)
    m = r.astype(BF16)
    lo = (r - m.astype(F32)).astype(BF16)
    return h, m, lo


def _tri_mm(tri, x):
    h, m, lo = _split3(x)
    return _dot(tri, h) + _dot(tri, m) + _dot(tri, lo)


def _softplus(x):
    return jnp.maximum(x, 0.0) + jnp.log1p(jnp.exp(-jnp.abs(x)))


def _chunk_decays(dt_raw, dtb, alog):
    dtv = _softplus(dt_raw + dtb)
    a = -jnp.exp(alog)
    ri = lax.broadcasted_iota(jnp.int32, (BLK, BLK), 0)
    ci = lax.broadcasted_iota(jnp.int32, (BLK, BLK), 1)
    causal = ri >= ci
    acum = _tri_mm(causal.astype(BF16), dtv * a)
    return dtv, a, causal, acum, acum.T


NPAIR = SH // 2


def _pairs(x):
    return jnp.stack([x[:, 128 * k:128 * (k + 1)] for k in range(NPAIR)])


def _unpairs(x3):
    return jnp.concatenate([x3[k] for k in range(NPAIR)], axis=1)


def _per_head_cols(m):
    return jnp.stack([jnp.broadcast_to(m[:, h:h + 1], m.shape) for h in range(SH)])


def _pair_lanes(t):
    r = t.reshape(NPAIR, 2, t.shape[1], 128)
    lo = lax.broadcasted_iota(jnp.int32, (1, t.shape[1], 128), 2) < SP
    return jnp.where(lo, r[:, 0], r[:, 1])


class _Chunk:
    pass


def _chunk_common(dt_raw, dtb, alog, dskip):
    cm = _Chunk()
    cm.dtv, cm.a, cm.causal, acum, acum_t = _chunk_decays(dt_raw, dtb, alog)
    cm.acol = _per_head_cols(acum)
    cm.arow = jnp.stack([acum_t[h:h + 1, :] for h in range(SH)])
    apl = _pair_lanes(cm.acol)
    alast = apl[:, BLK - 1:BLK, :]
    cm.dpl = _pair_lanes(_per_head_cols(cm.dtv))
    cm.eapl = jnp.exp(apl)
    cm.epl = jnp.exp(alast - apl)
    cm.cdpl = jnp.exp(alast)
    cm.dskpl = _pair_lanes(_per_head_cols(dskip))
    cm.lo = lax.broadcasted_iota(jnp.int32, (1, BLK, 128), 2) < SP
    return cm


def ssd_fwd(act, dt_raw, dtb_p, alog_p, dsk_p):
    t = act.shape[0]
    nc = t // BLK

    def body(xs_ref, b_ref, c_ref, dt_ref, dtb_ref, al_ref, dk_ref, y_ref, sp_ref, st):
        c = pl.program_id(0)

        @pl.when(c == 0)
        def _():
            st[...] = jnp.zeros_like(st)

        s_t = st[...]
        sp_ref[0] = s_t
        cm = _chunk_common(dt_ref[...], dtb_ref[...], al_ref[...], dk_ref[...])
        gms, cbs, bts = [], [], []
        for g in range(SG):
            bf = b_ref[:, SN * g:SN * (g + 1)]
            cb = c_ref[:, SN * g:SN * (g + 1)].astype(BF16)
            gms.append(_dot_nt(cb, bf.astype(BF16)))
            cbs.append(cb)
            bts.append(bf.T.astype(BF16))
        lam = jnp.exp(jnp.where(cm.causal[None], cm.acol - cm.arow, NEG))
        m = (lam.reshape(SG, SR, BLK, BLK) * jnp.stack(gms)[:, None]).reshape(SH, BLK, BLK).astype(BF16)
        xs16 = _pairs(xs_ref[...])
        xdt16 = xs16 * cm.dpl
        x_lo = jnp.where(cm.lo, xdt16, 0.0).astype(BF16)
        x_hi = jnp.where(cm.lo, 0.0, xdt16).astype(BF16)
        s16 = _pairs(s_t)
        s16b = s16.astype(BF16)
        yd = jnp.stack([_dot(m[2 * k], x_lo[k]) + _dot(m[2 * k + 1], x_hi[k]) for k in range(NPAIR)])
        yo = jnp.stack([_dot(cbs[k // (NPAIR // SG)], s16b[k]) for k in range(NPAIR)])
        y_ref[...] = _unpairs(yd + yo * cm.eapl + cm.dskpl * xs16).astype(BF16)
        xe = (xdt16 * cm.epl).astype(BF16)
        st[...] = _unpairs(cm.cdpl * s16 + jnp.stack([_dot(bts[k // (NPAIR // SG)], xe[k]) for k in range(NPAIR)]))

    vec = _full((1, 128))
    return pl.pallas_call(
        body, name="ssd_fwd", grid=(nc,),
        in_specs=[pl.BlockSpec((BLK, SSM_W), lambda c: (c, 0)),
                  pl.BlockSpec((BLK, SG * SN), lambda c: (c, SSM_W // (SG * SN))),
                  pl.BlockSpec((BLK, SG * SN), lambda c: (c, SSM_W // (SG * SN) + 1)),
                  pl.BlockSpec((BLK, 128), lambda c: (c, 0)), vec, vec, vec],
        out_specs=[pl.BlockSpec((BLK, SSM_W), lambda c: (c, 0)), pl.BlockSpec((1, SN, SSM_W), lambda c: (c, 0, 0))],
        out_shape=[jax.ShapeDtypeStruct((t, SSM_W), BF16), jax.ShapeDtypeStruct((nc, SN, SSM_W), F32)],
        scratch_shapes=[pltpu.VMEM((SN, SSM_W), F32)],
        compiler_params=_cp(("arbitrary",)),
    )(act, act, act, dt_raw, dtb_p, alog_p, dsk_p)


def _head_sums(q):
    r = q.shape[1]
    lo = lax.broadcasted_iota(jnp.int32, (1, r, 128), 2) < SP
    s_lo = jnp.sum(jnp.where(lo, q, 0.0), axis=-1, keepdims=True)
    s_hi = jnp.sum(jnp.where(lo, 0.0, q), axis=-1, keepdims=True)
    lane = lax.broadcasted_iota(jnp.int32, (r, 128), 1)
    out = jnp.zeros((r, 128), F32)
    for k in range(NPAIR):
        out = jnp.where(lane == 2 * k, s_lo[k], jnp.where(lane == 2 * k + 1, s_hi[k], out))
    return out


def ssd_bwd(act, dt_raw, dy, sprev, dtb_p, alog_p, dsk_p):
    t = act.shape[0]
    nc = t // BLK

    def body(xs_ref, b_ref, c_ref, dt_ref, dy_ref, sp_ref, dtb_ref, al_ref, dk_ref,
             da_ref, ddt_ref, ddtb_ref, dal_ref, ddk_ref, dst):
        i = pl.program_id(0)

        @pl.when(i == 0)
        def _():
            dst[...] = jnp.zeros_like(dst)
            ddtb_ref[...] = jnp.zeros_like(ddtb_ref)
            dal_ref[...] = jnp.zeros_like(dal_ref)
            ddk_ref[...] = jnp.zeros_like(ddk_ref)

        dt_raw = dt_ref[...]
        dtb = dtb_ref[...]
        cm = _chunk_common(dt_raw, dtb, al_ref[...], dk_ref[...])
        ri = lax.broadcasted_iota(jnp.int32, (BLK, BLK), 0)
        ci = lax.broadcasted_iota(jnp.int32, (BLK, BLK), 1)
        lam_t = jnp.exp(jnp.where((ri <= ci)[None], cm.arow - cm.acol, NEG))
        bbs, cbs, cts, gms = [], [], [], []
        for g in range(SG):
            bf = b_ref[:, SN * g:SN * (g + 1)]
            cf = c_ref[:, SN * g:SN * (g + 1)]
            bbs.append(bf.astype(BF16))
            cbs.append(cf.astype(BF16))
            cts.append(cf.T.astype(BF16))
            gms.append(_dot_nt(bbs[g], cbs[g]))
        grp = lambda k: k // (NPAIR // SG)
        xs16 = _pairs(xs_ref[...])
        dy16 = _pairs(dy_ref[...].astype(F32))
        sp16 = _pairs(sp_ref[0])
        ds16 = _pairs(dst[...])
        xdt16 = xs16 * cm.dpl
        xdtb = xdt16.astype(BF16)
        dyh = [jnp.where(cm.lo, dy16, 0.0).astype(BF16), jnp.where(cm.lo, 0.0, dy16).astype(BF16)]
        m_t = (lam_t.reshape(SG, SR, BLK, BLK) * jnp.stack(gms)[:, None]).reshape(SH, BLK, BLK).astype(BF16)
        dxdt = jnp.stack([_dot(m_t[2 * k], dyh[0][k]) + _dot(m_t[2 * k + 1], dyh[1][k]) for k in range(NPAIR)])
        dm_t = jnp.stack([_dot_nt(xdtb[h // 2], dyh[h % 2][h // 2]) for h in range(SH)])
        dg_t = jnp.sum((dm_t * lam_t).reshape(SG, SR, BLK, BLK), axis=1).astype(BF16)
        xq16 = xdtb.astype(F32)
        xh = [jnp.where(cm.lo, xdt16, 0.0).astype(BF16), jnp.where(cm.lo, 0.0, xdt16).astype(BF16)]
        y_in = jnp.stack([_dot_tn(m_t[2 * k], xh[0][k]) + _dot_tn(m_t[2 * k + 1], xh[1][k]) for k in range(NPAIR)])
        da_diag = dy16 * y_in - xq16 * dxdt
        lane_c = lax.broadcasted_iota(jnp.int32, (BLK, 128), 1)
        ds16b = ds16.astype(BF16)
        sp16b = sp16.astype(BF16)
        dxs = jnp.stack([_dot(bbs[grp(k)], ds16b[k]) for k in range(NPAIR)]) * cm.epl
        dxdt = dxdt + dxs
        dya = (dy16 * cm.eapl).astype(BF16)
        xe = (xdt16 * cm.epl).astype(BF16)
        dcs, dbs = [], []
        for g in range(SG):
            ks = range(g * (NPAIR // SG), (g + 1) * (NPAIR // SG))
            dcs.append(sum(_dot_nt(dya[k], sp16b[k]) for k in ks) + _dot_tn(dg_t[g], bbs[g]))
            dbs.append(sum(_dot_nt(xe[k], ds16b[k]) for k in ks) + _dot(dg_t[g], cbs[g]))
        dst[...] = _unpairs(cm.cdpl * ds16 + jnp.stack([_dot(cts[grp(k)], dya[k]) for k in range(NPAIR)]))
        da_ref[...] = jnp.concatenate([_unpairs(dxdt * cm.dpl + cm.dskpl * dy16)] + dbs + dcs, axis=1)
        y_off = jnp.stack([_dot(cbs[grp(k)], sp16b[k]) for k in range(NPAIR)]) * cm.eapl
        da_cols = _head_sums(da_diag + dy16 * y_off - xdt16 * dxs)
        last = _head_sums(jnp.sum(xdt16 * dxs, axis=1, keepdims=True)
                          + cm.cdpl * jnp.sum(ds16 * sp16, axis=1, keepdims=True))
        ddt = _head_sums(dxdt * xs16)
        row_i = lax.broadcasted_iota(jnp.int32, (BLK, 128), 0)
        dacum = da_cols + jnp.where(row_i == BLK - 1, last, 0.0)
        dda = _tri_mm((ri <= ci).astype(BF16), dacum)
        ddt = ddt + dda * cm.a
        dal_ref[...] += _csum(dda * cm.dtv) * cm.a
        ddt_raw = jnp.where(lane_c < SH, ddt * _sig(dt_raw + dtb), 0.0)
        ddt_ref[...] = ddt_raw.astype(BF16)
        ddtb_ref[...] += _csum(ddt_raw)
        ddk_ref[...] += _head_sums(jnp.sum(dy16 * xs16, axis=1, keepdims=True))

    rev = lambda i: nc - 1 - i
    vec = _full((1, 128))
    slab = pl.BlockSpec((BLK, SSM_W), lambda i: (rev(i), 0))
    return pl.pallas_call(
        body, name="ssd_bwd", grid=(nc,),
        in_specs=[slab,
                  pl.BlockSpec((BLK, SG * SN), lambda i: (rev(i), SSM_W // (SG * SN))),
                  pl.BlockSpec((BLK, SG * SN), lambda i: (rev(i), SSM_W // (SG * SN) + 1)),
                  pl.BlockSpec((BLK, 128), lambda i: (rev(i), 0)),
                  slab,
                  pl.BlockSpec((1, SN, SSM_W), lambda i: (rev(i), 0, 0)), vec, vec, vec],
        out_specs=[pl.BlockSpec((BLK, XBC), lambda i: (rev(i), 0)), pl.BlockSpec((BLK, 128), lambda i: (rev(i), 0)),
                   vec, vec, vec],
        out_shape=[jax.ShapeDtypeStruct((t, XBC), F32), jax.ShapeDtypeStruct((t, 128), BF16),
                   jax.ShapeDtypeStruct((1, 128), F32), jax.ShapeDtypeStruct((1, 128), F32),
                   jax.ShapeDtypeStruct((1, 128), F32)],
        scratch_shapes=[pltpu.VMEM((SN, SSM_W), F32)],
        compiler_params=_cp(("arbitrary",)),
    )(act, act, act, dt_raw, dy, sprev, dtb_p, alog_p, dsk_p)


TAIL_TM = 256


def _dsilu(z, s):
    return s * (1.0 + z * (1.0 - s))


def tail(proj, ao, yss, x, target, gate, ssm_nw, w_at, w_ss, w_ou):
    t = x.shape[0]
    tm = min(t, TAIL_TM)
    gw = SSM_W // SG

    def body(ao_ref, za_ref, ga_ref, gb_ref, zm_ref, ys_ref, x_ref, tg_ref, gt_ref, nw_ref, wa_ref, ws_ref, wo_ref,
             loss_ref, dy_ref, dao_ref, dmid_ref, dys_ref,
             ua_ref, yn_ref, mg_ref, dya_ref, dyb_ref, do_ref, dgt_ref, dnw_ref):
        i = pl.program_id(0)

        @pl.when(i == 0)
        def _():
            loss_ref[...] = jnp.zeros_like(loss_ref)
            dgt_ref[...] = jnp.zeros_like(dgt_ref)
            dnw_ref[...] = jnp.zeros_like(dnw_ref)

        ao = ao_ref[...].astype(F32)
        za = za_ref[...].astype(F32)
        sa = _sig(za)
        sila = za * sa
        ua_f = ao * sila
        ua = ua_f.astype(BF16)
        ya = _dot(ua, wa_ref[...])
        zm = zm_ref[...].astype(F32)
        sm = _sig(zm)
        silm = zm * sm
        ys = ys_ref[...].astype(F32)
        u = ys * silm
        nw = nw_ref[...]
        rs, uns = [], []
        for g in range(SG):
            ug = u[:, gw * g:gw * (g + 1)]
            r = lax.rsqrt(jnp.mean(ug * ug, axis=-1, keepdims=True) + EPS)
            rs.append(r)
            uns.append(ug * r)
        un = jnp.concatenate(uns, axis=1)
        yn_f = un * nw
        yn = yn_f.astype(BF16)
        yb = _dot(yn, ws_ref[...])
        sga = _sig(ga_ref[...].astype(F32))
        sgb = _sig(gb_ref[...].astype(F32))
        mg_f = sga * ya + sgb * yb
        mg = mg_f.astype(BF16)
        o = _dot(mg, wo_ref[...])
        gt = gt_ref[...]
        err = (x_ref[...] + gt * o) - tg_ref[...]
        lane = lax.broadcasted_iota(jnp.int32, (1, 128), 1)
        loss_ref[...] += jnp.where(lane == 0, 0.5 * _asum(_rsum(err * err) / D), 0.0)
        dy = err * (1.0 / D)
        dy_ref[...] = dy
        dgt_ref[...] += _csum(dy * o)
        do = (dy * gt).astype(BF16)
        dmg = _dot_nt(do, wo_ref[...])
        dmid_ref[:, C_GA - C_ZA:C_GB - C_ZA] = (dmg * ya * sga * (1.0 - sga)).astype(BF16)
        dmid_ref[:, C_GB - C_ZA:C_ZM - C_ZA] = (dmg * yb * sgb * (1.0 - sgb)).astype(BF16)
        dya = (dmg * sga).astype(BF16)
        dyb = (dmg * sgb).astype(BF16)
        dua = _dot_nt(dya, wa_ref[...])
        dao_ref[...] = (dua * sila).astype(BF16)
        dmid_ref[:, 0:C_GA - C_ZA] = (dua * ao * _dsilu(za, sa)).astype(BF16)
        dyn = _dot_nt(dyb, ws_ref[...])
        dnw_ref[...] += _csum(dyn * un)
        dun = dyn * nw
        dus = []
        for g in range(SG):
            gs = slice(gw * g, gw * (g + 1))
            dus.append(rs[g] * (dun[:, gs] - uns[g] * jnp.mean(dun[:, gs] * uns[g], axis=-1, keepdims=True)))
        du = jnp.concatenate(dus, axis=1)
        dys_ref[...] = (du * silm).astype(BF16)
        dmid_ref[:, C_ZM - C_ZA:] = (du * ys * _dsilu(zm, sm)).astype(BF16)
        ua_ref[...] = ua_f.T.astype(BF16)
        yn_ref[...] = yn_f.T.astype(BF16)
        mg_ref[...] = mg_f.T.astype(BF16)
        dya_ref[...] = dya
        dyb_ref[...] = dyb
        do_ref[...] = do

    row = lambda w: pl.BlockSpec((tm, w), lambda i: (i, 0))
    pcol = lambda w, c0: pl.BlockSpec((tm, w), lambda i: (i, c0 // w))
    sd = lambda w, dt: jax.ShapeDtypeStruct((t, w), dt)
    colt = lambda w: pl.BlockSpec((w, tm), lambda i: (0, i))
    sdt = lambda w: jax.ShapeDtypeStruct((w, t), BF16)
    return pl.pallas_call(
        body, name="tail", grid=(t // tm,),
        in_specs=[row(D), pcol(D, C_ZA), pcol(D, C_GA), pcol(D, C_GB), pcol(SSM_W, C_ZM), row(SSM_W), row(D), row(D),
                  _full((1, D)), _full((1, SSM_W)), _full((D, D)), _full((SSM_W, D)), _full((D, D))],
        out_specs=[_full((1, 128)), row(D), row(D), row(W_MID), row(SSM_W),
                   colt(D), colt(SSM_W), colt(D), row(D), row(D), row(D), _full((1, D)), _full((1, SSM_W))],
        out_shape=[jax.ShapeDtypeStruct((1, 128), F32), sd(D, F32), sd(D, BF16), sd(W_MID, BF16),
                   sd(SSM_W, BF16), sdt(D), sdt(SSM_W), sdt(D), sd(D, BF16),
                   sd(D, BF16), sd(D, BF16), jax.ShapeDtypeStruct((1, D), F32), jax.ShapeDtypeStruct((1, SSM_W), F32)],
        compiler_params=_cp(("arbitrary",)),
    )(ao, proj, proj, proj, proj, yss, x, target, gate, ssm_nw, w_at, w_ss, w_ou)


DPIECES = ((D, ((D, C_Q),)),
           (W_MID, ((D, C_ZA), (D, C_GA), (D, C_GB), (SSM_W, C_ZM))),
           (XBC, ((XBC, C_XBC),)),
           (512, ((512, C_K),)),
           (128, ((128, C_DT),)))


def dproj_bwd(pieces, wcat, x, dy, norm_w, scale):
    t = x.shape[0]
    tm = min(t, 256)
    nt = t // tm
    wblocks = [blk for _, subs in DPIECES for blk in subs]
    npc, nwb = len(DPIECES), len(wblocks)

    def body(*refs):
        p_refs, w_refs = refs[:npc], refs[npc:npc + nwb]
        x_ref, dy_ref, nw_ref, sc_ref, gx_ref, dnw_ref, dsc_ref, dsh_ref, dwe_ref = refs[npc + nwb:]
        i = pl.program_id(0)

        @pl.when(i == 0)
        def _():
            for ref in (dwe_ref, dsh_ref, dnw_ref, dsc_ref):
                ref[...] = jnp.zeros_like(ref)

        dh, wi = None, 0
        for p_ref, (_, subs) in zip(p_refs, DPIECES):
            loc = 0
            for w, _ in subs:
                part = _dot_nt(p_ref[:, loc:loc + w], w_refs[wi][...])
                dh = part if dh is None else dh + part
                loc += w
                wi += 1
        xv = x_ref[...]
        r = lax.rsqrt(jnp.mean(xv * xv, axis=-1, keepdims=True) + EPS)
        xn = xv * r
        weff = nw_ref[...] * (1.0 + sc_ref[...])
        dxn = dh * weff
        gx_ref[...] = dy_ref[...] + r * (dxn - xn * jnp.mean(dxn * xn, axis=-1, keepdims=True))
        dwe_ref[...] += _csum(dh * xn)
        dsh_ref[...] += _csum(dh)

        @pl.when(i == nt - 1)
        def _():
            dwe = dwe_ref[...]
            dnw_ref[...] = dwe * (1.0 + sc_ref[...])
            dsc_ref[...] = dwe * nw_ref[...]

    vec = pl.BlockSpec((1, D), lambda i: (0, 0))
    row = pl.BlockSpec((tm, D), lambda i: (i, 0))
    return pl.pallas_call(
        body, name="dproj_bwd", grid=(nt,),
        in_specs=[pl.BlockSpec((tm, pw), lambda i: (i, 0)) for pw, _ in DPIECES]
        + [pl.BlockSpec((D, w), functools.partial(lambda i, b: (0, b), b=off // w), pipeline_mode=pl.Buffered(1))
           for w, off in wblocks]
        + [row, row, vec, vec],
        out_specs=[row, vec, vec, vec],
        out_shape=[jax.ShapeDtypeStruct((t, D), F32), jax.ShapeDtypeStruct((1, D), F32),
                   jax.ShapeDtypeStruct((1, D), F32), jax.ShapeDtypeStruct((1, D), F32)],
        scratch_shapes=[pltpu.VMEM((1, D), F32)],
        compiler_params=_cp(("arbitrary",)),
    )(*pieces, *([wcat] * nwb), x, dy, norm_w, scale)


def wgrad(at, b, name, bn, after):
    m, t = at.shape
    n = b.shape[1]
    tk = min(t, 4096 if bn <= 1024 else 2048)
    bm = min(m, 1024)

    def body(a_ref, b_ref, after_ref, o_ref):
        part = _dot(a_ref[...], b_ref[...])

        @pl.when(pl.program_id(2) == 0)
        def _():
            o_ref[...] = part

        @pl.when(pl.program_id(2) > 0)
        def _():
            o_ref[...] += part

    return pl.pallas_call(
        body, name=name, grid=(m // bm, n // bn, t // tk),
        in_specs=[pl.BlockSpec((bm, tk), lambda i, j, k: (i, k)), pl.BlockSpec((tk, bn), lambda i, j, k: (k, j)), ANY],
        out_specs=pl.BlockSpec((bm, bn), lambda i, j, k: (i, j)),
        out_shape=jax.ShapeDtypeStruct((m, n), F32),
        compiler_params=_cp(("parallel", "parallel", "arbitrary")),
    )(at, b, after)


SUM_TR = 512


def pair_sum(g, core, theirs, name):
    w = g.shape[2]
    nh = HROWS // SUM_TR

    def body(core_ref, a_ref, b_ref, o_ref, ob_ref):
        s = a_ref[...] + b_ref[...]
        o_ref[...] = s
        ob_ref[...] = s.astype(BF16)

    spec = pl.BlockSpec((1, SUM_TR, w), lambda d, i, c: (d, i, 0))
    return pl.pallas_call(
        body, name=name,
        out_shape=[jax.ShapeDtypeStruct((4, HROWS, w), F32), jax.ShapeDtypeStruct((4, HROWS, w), BF16)],
        grid_spec=pltpu.PrefetchScalarGridSpec(
            num_scalar_prefetch=1, grid=(4, nh),
            in_specs=[pl.BlockSpec((1, SUM_TR, w), lambda d, i, c: (d, c[0] * nh + i, 0)), spec],
            out_specs=[spec, spec]),
        compiler_params=_cp(("parallel", "parallel")))(core.reshape(1).astype(jnp.int32), g, theirs)


def chip_sum(part, chip, others, name):
    r, w = part.shape[1:]

    def body(chip_ref, a_ref, b_ref, o_ref):
        acc = a_ref[0]
        for k in range(3):
            acc = acc + b_ref[k].astype(F32)
        o_ref[...] = acc

    return pl.pallas_call(
        body, name=name, out_shape=jax.ShapeDtypeStruct((r, w), F32),
        grid_spec=pltpu.PrefetchScalarGridSpec(
            num_scalar_prefetch=1, grid=(r // SUM_TR,),
            in_specs=[pl.BlockSpec((1, SUM_TR, w), lambda i, c: (c[0], i, 0)),
                      pl.BlockSpec((3, SUM_TR, w), lambda i, c: (0, i, 0))],
            out_specs=pl.BlockSpec((SUM_TR, w), lambda i, c: (i, 0))),
        compiler_params=_cp(("parallel",)))(chip.reshape(1).astype(jnp.int32), part, others)


def sum_devices(g):
    r = g.shape[1]

    def body(g_ref, o_ref):
        acc = g_ref[0]
        for d in range(1, 8):
            acc = acc + g_ref[d]
        o_ref[...] = acc

    return pl.pallas_call(body, name="sum_devices", out_shape=jax.ShapeDtypeStruct((r, 1024), F32),
                          compiler_params=_cp())(g)


def adamw(w, g, m, v, name):
    r, c = w.shape
    tr = r
    for cand in (256, 128, 64, 32, 16, 8):
        if r % cand == 0 and r > cand:
            tr = cand
            break

    def body(w_ref, g_ref, m_ref, v_ref, d_ref, nm_ref, nv_ref):
        gv = g_ref[...]
        mn = ADAM_B1 * m_ref[...] + (1.0 - ADAM_B1) * gv
        vn = ADAM_B2 * v_ref[...] + (1.0 - ADAM_B2) * (gv * gv)
        m_hat = mn / (1.0 - ADAM_B1 ** ADAM_STEP)
        v_hat = vn / (1.0 - ADAM_B2 ** ADAM_STEP)
        d_ref[...] = -ADAM_LR * (m_hat / (jnp.sqrt(v_hat) + ADAM_EPS) + ADAM_WD * w_ref[...])
        nm_ref[...] = mn
        nv_ref[...] = vn

    spec = pl.BlockSpec((tr, c), lambda i: (i, 0))
    sd = jax.ShapeDtypeStruct((r, c), F32)
    return pl.pallas_call(body, name=name, grid=(r // tr,), in_specs=[spec] * 4, out_specs=[spec] * 3,
                          out_shape=[sd, sd, sd], compiler_params=_cp(("parallel",)))(w, g, m, v)


def adamw_halves(w, mine, theirs, core, m, v, name):
    r, c = w.shape
    tr = 128
    nh = HROWS // tr

    def body(core_ref, w_ref, a_ref, b_ref, m_ref, v_ref, g_ref, d_ref, nm_ref, nv_ref):
        gv = jnp.where(pl.program_id(0) // nh == core_ref[0], a_ref[...], b_ref[...])
        mn = ADAM_B1 * m_ref[...] + (1.0 - ADAM_B1) * gv
        vn = ADAM_B2 * v_ref[...] + (1.0 - ADAM_B2) * (gv * gv)
        m_hat = mn / (1.0 - ADAM_B1 ** ADAM_STEP)
        v_hat = vn / (1.0 - ADAM_B2 ** ADAM_STEP)
        g_ref[...] = gv
        d_ref[...] = -ADAM_LR * (m_hat / (jnp.sqrt(v_hat) + ADAM_EPS) + ADAM_WD * w_ref[...])
        nm_ref[...] = mn
        nv_ref[...] = vn

    spec = pl.BlockSpec((tr, c), lambda i, s: (i, 0))
    half = pl.BlockSpec((tr, c), lambda i, s: (i % nh, 0))
    sd = jax.ShapeDtypeStruct((r, c), F32)
    return pl.pallas_call(
        body, name=name, out_shape=[sd, sd, sd, sd],
        grid_spec=pltpu.PrefetchScalarGridSpec(num_scalar_prefetch=1, grid=(r // tr,),
                                               in_specs=[spec, half, half, spec, spec], out_specs=[spec] * 4),
        compiler_params=_cp(("parallel",)))(core.reshape(1).astype(jnp.int32), w, mine, theirs, m, v)


ANY = pl.BlockSpec(memory_space=pl.ANY)
VM = pl.BlockSpec(memory_space=pltpu.VMEM)
OTHER_CHIPS = ((1, 0), (0, 1), (1, 1))


def _pos():
    return lax.axis_index("x"), lax.axis_index("y"), lax.axis_index("c")


def _flip(v, bit):
    return 1 - v if bit else v


def _rcopy(src, dst, ssem, rsem, peer):
    return pltpu.make_async_remote_copy(src_ref=src, dst_ref=dst, send_sem=ssem, recv_sem=rsem,
                                        device_id=peer, device_id_type=MESH)


def allgather_small(p, name):
    r = p.shape[0]

    def body(in_ref, out_ref, ssem, rsem, lsem):
        x, y, c = _pos()
        me = 4 * x + 2 * y + c
        loc = pltpu.make_async_copy(in_ref, out_ref.at[me], lsem)
        loc.start()
        sends = []
        peers = []
        for k in range(1, 8):
            px, py, pc = _flip(x, (k >> 2) & 1), _flip(y, (k >> 1) & 1), _flip(c, k & 1)
            peers.append((px, py, pc))
            cp = _rcopy(in_ref, out_ref.at[me], ssem.at[k - 1], rsem.at[k - 1], (px, py, pc))
            cp.start()
            sends.append(cp)
        for k in range(1, 8):
            px, py, pc = peers[k - 1]
            _rcopy(in_ref, out_ref.at[4 * px + 2 * py + pc], ssem.at[k - 1], rsem.at[k - 1], (px, py, pc)).wait_recv()
        for cp in sends:
            cp.wait_send()
        loc.wait()

    return pl.pallas_call(
        body, name=name, out_shape=jax.ShapeDtypeStruct((8, r, 1024), F32),
        in_specs=[VM], out_specs=VM,
        scratch_shapes=[pltpu.SemaphoreType.DMA((7,)), pltpu.SemaphoreType.DMA((7,)), pltpu.SemaphoreType.DMA],
    )(p)


def gather_weights(w_in_b, mod_sh):
    def body(wi_ref, m_ref, gi_ref, mo_ref, ssem, rsem, lsem):
        x, y, c = _pos()
        chip = 2 * x + y
        mine = pl.ds(pl.multiple_of(c * HROWS, 16), HROWS)
        other = pl.ds(pl.multiple_of((1 - c) * HROWS, 16), HROWS)
        sib = (x, y, 1 - c)
        pairs = ((wi_ref, gi_ref),)
        loc_m = pltpu.make_async_copy(m_ref, mo_ref.at[chip], lsem)
        loc_m.start()
        sends = []
        for k, (fx, fy) in enumerate(OTHER_CHIPS):
            peer = (_flip(x, fx), _flip(y, fy), c)
            for a, (w_ref, g_ref) in enumerate(pairs):
                cw = _rcopy(w_ref.at[mine], g_ref.at[chip, mine], ssem.at[6 * a + k], rsem.at[6 * a + k], peer)
                cw.start()
                sends.append(cw)
            cm = _rcopy(m_ref, mo_ref.at[chip], ssem.at[12 + k], rsem.at[12 + k], peer)
            cm.start()
            sends.append(cm)
        for k, (fx, fy) in enumerate(OTHER_CHIPS):
            px, py = _flip(x, fx), _flip(y, fy)
            for a, (w_ref, g_ref) in enumerate(pairs):
                got = g_ref.at[2 * px + py, mine]
                _rcopy(w_ref.at[mine], got, ssem.at[6 * a + k], rsem.at[6 * a + k], (px, py, c)).wait_recv()
                fw = _rcopy(got, got, ssem.at[6 * a + 3 + k], rsem.at[6 * a + 3 + k], sib)
                fw.start()
                sends.append(fw)
        for k, (fx, fy) in enumerate(OTHER_CHIPS):
            px, py = _flip(x, fx), _flip(y, fy)
            for a, (w_ref, g_ref) in enumerate(pairs):
                land = g_ref.at[2 * px + py, other]
                _rcopy(land, land, ssem.at[6 * a + 3 + k], rsem.at[6 * a + 3 + k], sib).wait_recv()
            _rcopy(m_ref, mo_ref.at[2 * px + py], ssem.at[12 + k], rsem.at[12 + k], (px, py, c)).wait_recv()
        for cp in sends:
            cp.wait_send()
        loc_m.wait()

    return pl.pallas_call(
        body, name="gather_weights",
        out_shape=[jax.ShapeDtypeStruct((4, D, SH_IN), BF16), jax.ShapeDtypeStruct((4, 8, 768), F32)],
        in_specs=[ANY, VM], out_specs=[ANY, VM],
        scratch_shapes=[pltpu.SemaphoreType.DMA((15,)), pltpu.SemaphoreType.DMA((15,)), pltpu.SemaphoreType.DMA],
    )(w_in_b, mod_sh)


def pair_exchange(g):
    def body(g_ref, r_ref, ssem, rsem):
        x, y, c = _pos()
        other = pl.ds(pl.multiple_of((1 - c) * HROWS, 8), HROWS)
        cp = _rcopy(g_ref.at[:, other, :], r_ref, ssem, rsem, (x, y, 1 - c))
        cp.start()
        cp.wait()

    return pl.pallas_call(
        body, name="pair_exchange", out_shape=jax.ShapeDtypeStruct((4, HROWS, g.shape[2]), F32),
        in_specs=[ANY], out_specs=ANY,
        scratch_shapes=[pltpu.SemaphoreType.DMA, pltpu.SemaphoreType.DMA],
    )(g)


HBM = pl.BlockSpec(memory_space=pltpu.HBM)
SEM = pl.BlockSpec(memory_space=pltpu.SEMAPHORE)
DATAFLOW = pltpu.SideEffectType.DATAFLOW_SIDE_EFFECTING


def split_start(name, make_copies, srcs, lands, nsem, after):
    arrays = [*srcs, *lands]
    n, ns = len(arrays), len(srcs)

    def body(*refs):
        for cp in make_copies(refs[:ns], refs[ns:n], refs[n + 1], refs[n + 2])[0]:
            cp.start()
        refs[-1][...] = jnp.zeros_like(refs[-1])

    res = pl.pallas_call(
        body, name=name,
        out_shape=(pltpu.SemaphoreType.DMA((nsem,)), pltpu.SemaphoreType.DMA((nsem,)),
                   *[pltpu.HBM(a.shape, a.dtype) for a in arrays], jax.ShapeDtypeStruct((8, 128), F32)),
        in_specs=(HBM,) * n + (ANY,), out_specs=(SEM, SEM) + (HBM,) * n + (VM,),
        input_output_aliases={i: 2 + i for i in range(n)},
        compiler_params=pltpu.CompilerParams(has_side_effects=DATAFLOW),
    )(*[pltpu.with_memory_space_constraint(a, pltpu.HBM) for a in arrays], after)
    return res[0], res[1], list(res[2:2 + n]), res[-1]


def split_wait(name, make_copies, ssem, rsem, arrays, ns, after):
    n = len(arrays)

    def body(*refs):
        sends, recvs = make_copies(refs[:ns], refs[ns:n], refs[n], refs[n + 1])
        for cp in sends:
            cp.wait_send()
        for cp in recvs:
            cp.wait_recv()

    return pl.pallas_call(
        body, name=name, out_shape=tuple(pltpu.HBM(a.shape, a.dtype) for a in arrays),
        in_specs=(HBM,) * n + (SEM, SEM, ANY), out_specs=(HBM,) * n,
        input_output_aliases={i: i for i in range(n)},
        compiler_params=pltpu.CompilerParams(has_side_effects=DATAFLOW),
    )(*arrays, ssem, rsem, after)


def _chip_copies(srcs, lands, ssem, rsem):
    x, y, c = _pos()
    copies = []
    for k, (fx, fy) in enumerate(OTHER_CHIPS):
        px, py = _flip(x, fx), _flip(y, fy)
        for a, (p_ref, l_ref) in enumerate(zip(srcs, lands)):
            copies.append(_rcopy(p_ref.at[2 * px + py], l_ref.at[k], ssem.at[3 * a + k], rsem.at[3 * a + k], (px, py, c)))
    return copies, copies


def _pair_copies(srcs, lands, ssem, rsem):
    x, y, c = _pos()
    other = pl.ds(pl.multiple_of((1 - c) * HROWS, 8), HROWS)
    copies = [_rcopy(srcs[0].at[:, other, :], lands[0], ssem.at[0], rsem.at[0], (x, y, 1 - c))]
    return copies, copies


def _rest_copies(srcs, lands, ssem, rsem):
    x, y, c = _pos()
    chip = 2 * x + y
    mine = pl.ds(pl.multiple_of(c * HROWS, 16), HROWS)
    sends, recvs = [], []
    for k, (fx, fy) in enumerate(OTHER_CHIPS):
        px, py = _flip(x, fx), _flip(y, fy)
        for t in range(2):
            rows_t = pl.ds(t * HROWS, HROWS)
            sends.append(_rcopy(srcs[0].at[mine], lands[0].at[chip, mine], ssem.at[2 * k + t], rsem.at[2 * k + c],
                                (px, py, t)))
            recvs.append(_rcopy(srcs[0].at[rows_t], lands[0].at[2 * px + py, rows_t], ssem.at[2 * k + t],
                                rsem.at[2 * k + t], (px, py, t)))
    return sends, recvs


def _swap_copies(srcs, lands, ssem, rsem):
    x, y, c = _pos()
    copies = [_rcopy(s_ref, l_ref, ssem.at[a], rsem.at[a], (x, y, 1 - c))
              for a, (s_ref, l_ref) in enumerate(zip(srcs, lands))]
    return copies, copies


def _flat(v, width=1024):
    v = v.reshape(-1)
    n = -(-v.shape[0] // width) * width
    return jnp.pad(v, (0, n - v.shape[0]))


def _rows(parts, rows):
    flat = jnp.concatenate(parts)
    return jnp.pad(flat, (0, rows * 1024 - flat.shape[0])).reshape(rows, 1024)


def _pack_small(b_ada, norm_w, conv_b, ssm_norm_w, q_norm_w, k_norm_w, sinks, dt_bias, a_log, d_skip, rel_bias,
                extra=None, tail=(), rows=16):
    misc = [q_norm_w, k_norm_w, sinks, dt_bias, a_log, d_skip] + ([] if extra is None else [extra])
    parts = [_flat(b_ada), _flat(norm_w), _flat(conv_b), _flat(ssm_norm_w)] + [_flat(v, 128) for v in misc]
    parts.append(jnp.zeros(((8 - len(misc)) * 128,), F32))
    parts.append(_flat(rel_bias))
    parts.append(jnp.zeros((5 * 1024,), F32))
    return _rows(parts + [_flat(v) for v in tail], rows)


def _unpack_small(p):
    misc = p[9]
    return dict(b_ada=p[0:3].reshape(1, 3072), norm_w=p[3:4], conv_b=p[4:7].reshape(1, 3072),
                ssm_norm_w=p[7:9].reshape(1, 2048), q_norm_w=misc[None, 0:64], k_norm_w=misc[None, 128:192],
                sinks=misc[None, 256:272], dt_bias=misc[None, 384:416], a_log=misc[None, 512:544],
                d_skip=misc[None, 640:672], rel_bias=p[10, :512].reshape(32, 16), extra=misc[768])


SMALL = ("b_ada", "norm_w", "conv_b", "ssm_norm_w", "q_norm_w", "k_norm_w", "sinks", "dt_bias", "a_log", "d_skip",
         "rel_bias")
WEIGHTS = ("w_ada", "b_ada", "norm_w", "w_in", "q_norm_w", "k_norm_w", "rel_bias", "sinks", "conv_w", "conv_b",
           "dt_bias", "a_log", "d_skip", "ssm_norm_w", "w_attn_proj", "w_ssm_proj", "w_out")
IN_COLS = ((0, 1024, C_Q), (1024, 256, C_K), (1280, 256, C_V), (1536, 1024, C_ZA), (2560, 2048, C_ZM),
           (4608, 3072, C_XBC), (7680, 32, C_DT), (7712, 1024, C_GA), (8736, 1024, C_GB))


def _to_cat(shards):
    parts, pos = [], 0
    for o, n, cnew in sorted(IN_COLS, key=lambda e: e[2]):
        assert cnew == pos
        c0 = o
        while c0 < o + n:
            i = c0 // SH_IN
            c1 = min(o + n, (i + 1) * SH_IN)
            parts.append(shards[i][:, c0 - i * SH_IN:c1 - i * SH_IN])
            c0 = c1
        pos += n
    parts.append(jnp.zeros((D, NP - pos), shards.dtype))
    return jnp.concatenate(parts, axis=1)


def _from_cat(dw_pieces):
    starts = [subs[0][1] for _, subs in DPIECES]

    def cols(c0, c1):
        p = max(q for q in range(len(starts)) if starts[q] <= c0)
        return dw_pieces[p][:, c0 - starts[p]:c1 - starts[p]]

    shards = []
    for i in range(4):
        lo, hi = i * SH_IN, (i + 1) * SH_IN
        parts = []
        for o, n, cnew in IN_COLS:
            a, b = max(o, lo), min(o + n, hi)
            if a < b:
                parts.append(cols(cnew + a - o, cnew + b - o))
        shards.append(jnp.concatenate(parts, axis=1))
    return jnp.stack(shards)


def kernel(x, c, w_ada, b_ada, norm_w, w_in, q_norm_w, k_norm_w, rel_bias, sinks, conv_w, conv_b, dt_bias, a_log, d_skip, ssm_norm_w, w_attn_proj, w_ssm_proj, w_out, loss_target, m_w_ada, m_b_ada, m_norm_w, m_w_in, m_q_norm_w, m_k_norm_w, m_rel_bias, m_sinks, m_conv_w, m_conv_b, m_dt_bias, m_a_log, m_d_skip, m_ssm_norm_w, m_w_attn_proj, m_w_ssm_proj, m_w_out, v_w_ada, v_b_ada, v_norm_w, v_w_in, v_q_norm_w, v_k_norm_w, v_rel_bias, v_sinks, v_conv_w, v_conv_b, v_dt_bias, v_a_log, v_d_skip, v_ssm_norm_w, v_w_attn_proj, v_w_ssm_proj, v_w_out):
    args = dict(locals())
    xi, yi, ci = lax.axis_index("x"), lax.axis_index("y"), lax.axis_index("c")
    chip = 2 * xi + yi
    me = 4 * xi + 2 * yi + ci
    x2 = x[0]
    tgt = loss_target[0]

    pay = _rows([c.reshape(-1), conv_w[0].reshape(-1)], 8)
    g0 = allgather_small(pay, "gather_cond")
    c_all = g0[:, 0, :]
    conv_w_full = g0[0::2, 1:4, :].reshape(4, CONV_K, 768).transpose(1, 0, 2).reshape(CONV_K, XBC)

    b_ada_sh = lax.dynamic_slice(b_ada, (0, chip * 768), (1, 768))
    mod_sh = ada_mod(c_all, w_ada[0], b_ada_sh)

    w_in_b = w_in[0].astype(BF16)
    w_rest_b = jnp.concatenate([w_attn_proj[0], w_ssm_proj[0], w_out[0]], axis=0).astype(BF16)
    wg_in, modg = gather_weights(w_in_b, mod_sh)
    wg_in = lax.dynamic_update_slice(wg_in, w_in_b[None], (chip, 0, 0))
    rs_sem, rr_sem, rest_thru, rest_tok = split_start("gather_rest_start", _rest_copies, [w_rest_b],
                                                      [lax.empty((4, D, D), BF16)], 6, modg)
    mod = lax.dynamic_slice(modg, (0, me, 0), (4, 1, 768)).reshape(1, 3 * D)
    shift, scale, gate = mod[:, :D], mod[:, D:2 * D] + rest_tok[:1, :1], mod[:, 2 * D:]
    wcat = _to_cat(wg_in)

    pad128 = lambda v: jnp.pad(v, ((0, 0), (0, 128 - v.shape[1])))
    dtb_p, alog_p, dsk_p = pad128(dt_bias), pad128(a_log), pad128(d_skip)
    bucket = _bucket_table()

    proj, dt_raw, h_t = norm_proj(x2, norm_w, scale, shift, wcat)
    biasm = bias_expand(rel_bias, sinks, bucket)
    ao = attn_fwd(proj, biasm, q_norm_w, k_norm_w)
    act, dsl = conv_fwd(proj, conv_w_full, conv_b)
    yss, sprev = ssd_fwd(act, dt_raw, dtb_p, alog_p, dsk_p)

    w_rest_b, wg_rest = split_wait("gather_rest_wait", _rest_copies, rs_sem, rr_sem, rest_thru, 1, yss)
    wg_rest = lax.dynamic_update_slice(wg_rest, w_rest_b[None], (chip, 0, 0))
    w_at = wg_rest[:, :R_AT].reshape(D, D)
    w_ss = wg_rest[:, R_AT:R_AT + R_SS].reshape(SSM_W, D)
    w_ou = wg_rest[:, R_AT + R_SS:].reshape(D, D)
    (loss_p, dy, dao, dmid, dyss, ua_t, yn_t, mg_t, dya, dyb, dout, dgate, dssm_nw) = tail(
        proj, ao, yss, x2, tgt, gate, ssm_norm_w, w_at, w_ss, w_ou)

    dq, dkv, dqw, dkw, dacc = attn_bwd(proj, dao, biasm, q_norm_w, k_norm_w)
    dbias = bias_reduce(dacc, bucket)
    drb = dbias[:, :NBUCKET].T
    dsk = dbias[:, NBUCKET].reshape(1, HQ)
    dact, ddt, ddtb, dalog, ddskip = ssd_bwd(act, dt_raw, dyss, sprev, dtb_p, alog_p, dsk_p)
    dxbc, dconv_w, dconv_b = conv_bwd(proj, dact, dsl, conv_w_full)

    dproj = (dq, dmid, dxbc, dkv, ddt)
    dwcat = [wgrad(h_t, piece, "dw_in_%d" % p, 1280 if piece.shape[1] == W_MID else min(piece.shape[1], 1024), rest_tok)
             for p, piece in enumerate(dproj)]

    g_in = _from_cat(dwcat)
    ps_sem, pr_sem, pair_thru, pair_tok = split_start("pair_in_start", _pair_copies, [g_in],
                                                      [lax.empty((4, HROWS, SH_IN), F32)], 1, loss_p)
    dw_at = wgrad(ua_t, dya, "dw_attn", 1024, pair_tok)
    dw_ss = wgrad(yn_t, dyb, "dw_ssm", 1024, pair_tok)
    dw_ou = wgrad(mg_t, dout, "dw_out", 1024, pair_tok)
    g_rest = jnp.concatenate([dw_at.reshape(4, R_AT, D), dw_ss.reshape(4, R_SS, D), dw_ou.reshape(4, R_OU, D)], axis=1)
    sib_rest = pair_exchange(g_rest)
    g_in, sib_in = split_wait("pair_in_wait", _pair_copies, ps_sem, pr_sem, pair_thru, 1, sib_rest)
    part_in, pb_in = pair_sum(g_in, ci, sib_in, "pair_sum_in")
    part_rest, pb_rest = pair_sum(g_rest, ci, sib_rest, "pair_sum_rest")
    cs_sem, cr_sem, chip_thru, token = split_start(
        "chip_exchange_start", _chip_copies, [pb_in, pb_rest],
        [lax.empty((3, HROWS, SH_IN), BF16), lax.empty((3, HROWS, D), BF16)], 6, part_rest)
    grad_x, dnorm_w, dscale, dshift = dproj_bwd(dproj, wcat, x2, dy, norm_w, scale + token[:1, :1])
    _, _, oth_in, oth_rest = split_wait("chip_exchange_wait", _chip_copies, cs_sem, cr_sem, chip_thru, 2, dshift)
    red_in = chip_sum(part_in, chip, oth_in, "chip_sum_in")
    red_rest = chip_sum(part_rest, chip, oth_rest, "chip_sum_rest")
    sw_ssem, sw_rsem, swap_thru, swap_tok = split_start(
        "pair_swap_start", _swap_copies, [red_in, red_rest],
        [lax.empty((HROWS, SH_IN), F32), lax.empty((HROWS, D), F32)], 2, red_rest)

    dmod = jnp.concatenate([dshift, dscale, dgate], axis=1)
    gsmall = _pack_small(dmod, dnorm_w, dconv_b, dssm_nw, dqw, dkw, dsk[:, :HQ], ddtb[:, :SH], dalog[:, :SH],
                         ddskip[:, :SH], drb, extra=loss_p[:, :1] + swap_tok[:1, :1], tail=(dconv_w,), rows=32)
    gall = allgather_small(gsmall, "gather_small_grads")
    ssum = sum_devices(gall)
    gs = _unpack_small(ssum[:16])
    loss = gs["extra"]
    dconv_w_sh = lax.dynamic_slice(ssum[16:28].reshape(CONV_K, XBC), (0, chip * 768), (CONV_K, 768))
    dmod_all = gall[:, 0:3, :].reshape(8, 3 * D)
    dw_ada = ada_grad(c_all, lax.dynamic_slice(dmod_all, (0, chip * 768), (8, 768)))

    grads = dict(gs)
    grads["w_ada"] = dw_ada
    grads["conv_w"] = dconv_w_sh

    delta, new_m, new_v = {}, {}, {}

    def step(n):
        delta[n], new_m[n], new_v[n] = adamw(args[n][0], grads[n], args["m_" + n][0], args["v_" + n][0], "adamw_" + n)

    step("w_ada")
    step("conv_w")
    ws = _pack_small(*[args[n] for n in SMALL])
    ms = _pack_small(*[args["m_" + n] for n in SMALL])
    vs = _pack_small(*[args["v_" + n] for n in SMALL])
    d_s, m_s, v_s = adamw(ws, ssum[:16], ms, vs, "adamw_small")
    red_in, red_rest, recv_in, recv_rest = split_wait("pair_swap_wait", _swap_copies, sw_ssem, sw_rsem, swap_thru, 2, d_s)
    d_s, m_s, v_s = _unpack_small(d_s), _unpack_small(m_s), _unpack_small(v_s)
    for n in SMALL:
        delta[n], new_m[n], new_v[n] = d_s[n], m_s[n], v_s[n]
    grads["w_in"], delta["w_in"], new_m["w_in"], new_v["w_in"] = adamw_halves(
        w_in[0], red_in, recv_in, ci, m_w_in[0], v_w_in[0], "adamw_w_in")
    g_shard_rest = jnp.concatenate([jnp.where(ci == 0, red_rest, recv_rest), jnp.where(ci == 0, recv_rest, red_rest)],
                                   axis=0)
    grads["w_attn_proj"] = g_shard_rest[:R_AT]
    grads["w_ssm_proj"] = g_shard_rest[R_AT:R_AT + R_SS]
    grads["w_out"] = g_shard_rest[R_AT + R_SS:]
    for n in ("w_attn_proj", "w_ssm_proj", "w_out"):
        step(n)

    def shaped(n, a):
        return a.reshape(args[n].shape)

    outs = [loss, grad_x[None]]
    for table in (grads, delta, new_m, new_v):
        outs += [shaped(n, table[n]) for n in WEIGHTS]
    return tuple(outs)
```

```python
import functools
import math

import numpy as np
import jax
import jax.numpy as jnp
from jax import lax
from jax.experimental import pallas as pl
from jax.experimental.pallas import tpu as pltpu

F32 = jnp.float32
BF16 = jnp.bfloat16
MESH = pl.DeviceIdType.MESH

D = 1024
HQ, HKV, GRP, DH = 16, 4, 4, 64
BLK = 128
NBUCKET, MAXDIST = 32, 128
SSM_W, SH, SG, SR, SP, SN = 2048, 32, 4, 8, 64, 128
CONV_K = 4
XBC = SSM_W + 2 * SG * SN
IN_W = 9760
EPS = 1e-6
NEG = -1e30
SCALE = DH ** -0.5

C_Q, C_ZA, C_GA, C_GB, C_ZM, C_XBC, C_K, C_V, C_DT = 0, 1024, 2048, 3072, 4096, 6144, 9216, 9472, 9728
NP = 9984
TN = 3328
W_MID = C_XBC - C_ZA

SH_IN = IN_W // 4
R_AT, R_SS, R_OU = 256, 512, 256
HROWS = D // 2

ADAM_LR, ADAM_B1, ADAM_B2, ADAM_EPS, ADAM_WD, ADAM_STEP = 0.001, 0.9, 0.999, 1e-08, 0.01, 10

VMEM_LIMIT = 56 * 1024 * 1024


def _cp(sem=None):
    if sem is None:
        return pltpu.CompilerParams(vmem_limit_bytes=VMEM_LIMIT)
    return pltpu.CompilerParams(dimension_semantics=sem, vmem_limit_bytes=VMEM_LIMIT)


def _sig(x):
    return 0.5 * jnp.tanh(0.5 * x) + 0.5


def _dot(a, b):
    return jnp.dot(a, b, preferred_element_type=F32)


def _dot_nt(a, b):
    return lax.dot_general(a, b, (((1,), (1,)), ((), ())), preferred_element_type=F32)


def _dot_tn(a, b):
    return lax.dot_general(a, b, (((0,), (0,)), ((), ())), preferred_element_type=F32)


def _rsum(x):
    return jnp.sum(x, axis=-1, keepdims=True)


def _csum(x):
    return jnp.sum(x, axis=0, keepdims=True)


def _asum(x):
    return _csum(_rsum(x))


def _full(shape):
    nd = len(shape)
    return pl.BlockSpec(shape, lambda *_: (0,) * nd)


def ada_mod(c_all, w_ada_sh, b_ada_sh):
    def body(c_ref, w_ref, b_ref, o_ref):
        cv = c_ref[...]
        s = cv * _sig(cv)
        o_ref[...] = jnp.dot(s, w_ref[...], preferred_element_type=F32,
                             precision=lax.Precision.HIGHEST) + b_ref[...]

    n = w_ada_sh.shape[1]
    return pl.pallas_call(body, name="ada_mod", out_shape=jax.ShapeDtypeStruct((8, n), F32),
                          compiler_params=_cp())(c_all, w_ada_sh, b_ada_sh)


def ada_grad(c_all, dmod_sh):
    def body(c_ref, d_ref, o_ref):
        cv = c_ref[...]
        s = cv * _sig(cv)
        o_ref[...] = lax.dot_general(s, d_ref[...], (((0,), (0,)), ((), ())), preferred_element_type=F32,
                                     precision=lax.Precision.HIGHEST)

    n = dmod_sh.shape[1]
    return pl.pallas_call(body, name="ada_grad", out_shape=jax.ShapeDtypeStruct((D, n), F32),
                          compiler_params=_cp())(c_all, dmod_sh)


def norm_proj(x, norm_w, scale, shift, wcat):
    t = x.shape[0]
    tm = min(t, 1024)

    def body(x_ref, nw_ref, sc_ref, sh_ref, w_ref, p_ref, dt_ref, ht_ref, hs):
        @pl.when(pl.program_id(1) == 0)
        def _():
            xv = x_ref[...]
            r = lax.rsqrt(jnp.mean(xv * xv, axis=-1, keepdims=True) + EPS)
            h = (xv * r) * nw_ref[...]
            h = h * (1.0 + sc_ref[...]) + sh_ref[...]
            hs[...] = h.astype(BF16)
            ht_ref[...] = h.T.astype(BF16)

        p = _dot(hs[...], w_ref[...])
        p_ref[...] = p.astype(BF16)

        @pl.when(pl.program_id(1) == C_DT // TN)
        def _():
            dt_ref[...] = p[:, C_DT % TN:C_DT % TN + 128]

    vec = pl.BlockSpec((1, D), lambda i, j: (0, 0))
    return pl.pallas_call(
        body, name="norm_proj", grid=(t // tm, NP // TN),
        in_specs=[pl.BlockSpec((tm, D), lambda i, j: (i, 0)), vec, vec, vec,
                  pl.BlockSpec((D, TN), lambda i, j: (0, j))],
        out_specs=[pl.BlockSpec((tm, TN), lambda i, j: (i, j)), pl.BlockSpec((tm, 128), lambda i, j: (i, 0)),
                   pl.BlockSpec((D, tm), lambda i, j: (0, i))],
        out_shape=[jax.ShapeDtypeStruct((t, NP), BF16), jax.ShapeDtypeStruct((t, 128), F32),
                   jax.ShapeDtypeStruct((D, t), BF16)],
        scratch_shapes=[pltpu.VMEM((tm, D), BF16)],
        compiler_params=_cp(("parallel", "arbitrary")),
    )(x, norm_w, scale, shift, wcat)


def _bucket_table():
    qi = np.arange(BLK)[:, None]
    kj = np.arange(2 * BLK)[None, :]
    dist = qi + BLK - kj
    n = np.maximum(dist, 0)
    max_exact = NBUCKET // 2
    nf = np.maximum(n, 1).astype(np.float32)
    large = max_exact + (np.log(nf / np.float32(max_exact)) / np.float32(math.log(MAXDIST / max_exact))
                         * np.float32(NBUCKET - max_exact)).astype(np.int32)
    large = np.minimum(large, NBUCKET - 1)
    bucket = np.where(n < max_exact, n, large).astype(np.int32)
    valid = (dist >= 0) & (dist < BLK)
    return np.where(valid, bucket, -1).astype(np.int32)


def bias_expand(rel_bias, sinks, bucket):
    def body(rb_ref, sk_ref, bk_ref, o_ref):
        bk = bk_ref[...]
        col = lax.broadcasted_iota(jnp.int32, (BLK, 2 * BLK), 1)

        def head(hd, carry):
            def step(b, acc):
                return jnp.where(bk == b, rb_ref[b, hd], acc)

            acc = lax.fori_loop(0, NBUCKET, step, jnp.full((BLK, 2 * BLK), NEG, F32))
            acc = jnp.where(col == 0, sk_ref[0, hd], acc)
            o_ref[1, hd] = acc
            o_ref[0, hd] = jnp.where(jnp.logical_and(col > 0, col < BLK), NEG, acc)
            return carry

        lax.fori_loop(0, HQ, head, 0)

    smem = pl.BlockSpec(memory_space=pltpu.SMEM)
    return pl.pallas_call(
        body, name="bias_expand", in_specs=[smem, smem, VM], out_specs=VM,
        out_shape=jax.ShapeDtypeStruct((2, HQ, BLK, 2 * BLK), F32), compiler_params=_cp(),
    )(rel_bias, sinks, jnp.asarray(bucket))


def bias_reduce(dacc, bucket):
    col = np.arange(BLK * 2 * BLK) % (2 * BLK)
    lane = np.arange(128)[None, :]
    member = (bucket.reshape(-1)[:, None] == lane) | ((col[:, None] == 0) & (lane == NBUCKET))

    def body(d_ref, m_ref, o_ref):
        mm = m_ref[...]
        o_ref[...] = sum(_dot(part, mm) for part in _split3(d_ref[...]))

    return pl.pallas_call(body, name="bias_reduce", out_shape=jax.ShapeDtypeStruct((HQ, 128), F32),
                          compiler_params=_cp())(dacc.reshape(HQ, BLK * 2 * BLK), jnp.asarray(member, BF16))


GQ = GRP * BLK


def _stack_heads(x, nh):
    return jnp.concatenate([x[:, DH * h:DH * (h + 1)] for h in range(nh)], axis=0)


def _unstack(xs, nh):
    rows = xs.shape[0] // nh
    return jnp.concatenate([xs[rows * h:rows * (h + 1)] for h in range(nh)], axis=1)


def _rms(x):
    return lax.rsqrt(jnp.mean(x * x, axis=-1, keepdims=True) + EPS)


def _stack_q(q, qw):
    qs = _stack_heads(q, HQ)
    r = _rms(qs)
    qhat = qs * r
    return qhat * qw, qhat, r


def _band_first(shape):
    return (lax.broadcasted_iota(jnp.int32, shape, 0) & (2 * BLK - 1)) == 0


def _stack_kv(kp, kc, vp, vc, kw):
    ks = _stack_heads(jnp.concatenate([kp, kc], axis=0), HKV)
    r = _rms(ks)
    khat = ks * r
    first = _band_first(ks.shape)
    kn = jnp.where(first, 0.0, khat * kw)
    v2 = jnp.where(first, 0.0, _stack_heads(jnp.concatenate([vp, vc], axis=0), HKV)).astype(BF16)
    return kn, khat, r, v2


def _softmax_rows(s):
    p = jnp.exp(s - jnp.max(s, axis=-1, keepdims=True))
    return p * (1.0 / _rsum(p))


def attn_fwd(proj, biasm, q_norm_w, k_norm_w):
    t = proj.shape[0]
    nb = t // BLK

    def body(q_ref, kc_ref, kp_ref, vc_ref, vp_ref, bm_ref, qw_ref, kw_ref, o_ref):
        f = lambda ref: ref[...].astype(F32)
        qn = _stack_q(f(q_ref), qw_ref[...])[0].astype(BF16)
        kn, _, _, v2 = _stack_kv(f(kp_ref), f(kc_ref), f(vp_ref), f(vc_ref), kw_ref[...])
        knb = kn.astype(BF16)
        s = jnp.concatenate([_dot_nt(qn[GQ * j:GQ * (j + 1)], knb[2 * BLK * j:2 * BLK * (j + 1)])
                             for j in range(HKV)], axis=0)
        pr = _softmax_rows(s * SCALE + bm_ref[0].reshape(HQ * BLK, 2 * BLK)).astype(BF16)
        o = jnp.concatenate([_dot(pr[GQ * j:GQ * (j + 1)], v2[2 * BLK * j:2 * BLK * (j + 1)])
                             for j in range(HKV)], axis=0)
        o_ref[...] = _unstack(o, HQ).astype(BF16)

    kblk, vblk = C_K // 256, C_V // 256
    prev = lambda n: jnp.maximum(n - 1, 0)
    return pl.pallas_call(
        body, name="attn_fwd", grid=(nb,),
        in_specs=[pl.BlockSpec((BLK, D), lambda n: (n, 0)),
                  pl.BlockSpec((BLK, 256), lambda n: (n, kblk)),
                  pl.BlockSpec((BLK, 256), lambda n: (prev(n), kblk)),
                  pl.BlockSpec((BLK, 256), lambda n: (n, vblk)),
                  pl.BlockSpec((BLK, 256), lambda n: (prev(n), vblk)),
                  pl.BlockSpec((1, HQ, BLK, 2 * BLK), lambda n: (jnp.minimum(n, 1), 0, 0, 0)),
                  _full((1, DH)), _full((1, DH))],
        out_specs=pl.BlockSpec((BLK, D), lambda n: (n, 0)),
        out_shape=jax.ShapeDtypeStruct((t, D), BF16),
        compiler_params=_cp(("parallel",)),
    )(proj, proj, proj, proj, proj, biasm, q_norm_w, k_norm_w)


def attn_bwd(proj, dao, biasm, q_norm_w, k_norm_w):
    t = proj.shape[0]
    nb = t // BLK
    kb = 2 * BLK

    def body(q_ref, kc_ref, kp_ref, vc_ref, vp_ref, do_ref, bm_ref, qw_ref, kw_ref,
             dq_ref, dkv_ref, dqw_ref, dkw_ref, dacc_ref, ck, cv, pk, pv, nk, nv):
        n = pl.program_id(0)

        @pl.when(n == 0)
        def _():
            for ref in (dqw_ref, dkw_ref, dacc_ref, ck, cv):
                ref[...] = jnp.zeros_like(ref)

        qw = qw_ref[...]
        kw = kw_ref[...]
        f = lambda ref: ref[...].astype(F32)
        kn, khat, rk, v2 = _stack_kv(f(kp_ref), f(kc_ref), f(vp_ref), f(vc_ref), kw)
        grp = lambda a, j: a[GQ * j:GQ * (j + 1)]
        band = lambda a, j: a[kb * j:kb * (j + 1)]

        @pl.when(n < nb)
        def _():
            qn, qhat, rq = _stack_q(f(q_ref), qw)
            qnb = qn.astype(BF16)
            knb = kn.astype(BF16)
            dos = _stack_heads(f(do_ref), HQ).astype(BF16)
            s = jnp.concatenate([_dot_nt(grp(qnb, j), band(knb, j)) for j in range(HKV)], axis=0)
            pr = _softmax_rows(s * SCALE + bm_ref[0].reshape(HQ * BLK, kb))
            dp = jnp.concatenate([_dot_nt(grp(dos, j), band(v2, j)) for j in range(HKV)], axis=0)
            ds = pr * (dp - _rsum(pr * dp))
            dacc_ref[...] += ds.reshape(HQ, BLK, kb)
            dsb = ds.astype(BF16)
            prb = pr.astype(BF16)
            dqn = jnp.concatenate([_dot(grp(dsb, j), band(knb, j)) for j in range(HKV)], axis=0) * SCALE
            dqhat = dqn * qw
            dq = rq * (dqhat - qhat * jnp.mean(dqhat * qhat, axis=-1, keepdims=True))
            dq_ref[...] = _unstack(dq, HQ).astype(BF16)
            dqw_ref[...] += _csum(dqn * qhat)
            first = _band_first((kb, DH))
            for j in range(HKV):
                rows = slice(BLK * j, BLK * (j + 1))
                dkn = jnp.where(first, 0.0, _dot_tn(grp(dsb, j), grp(qnb, j)) * SCALE)
                dvj = jnp.where(first, 0.0, _dot_tn(grp(prb, j), grp(dos, j)))
                pk[rows, :] = dkn[:BLK]
                nk[rows, :] = dkn[BLK:]
                pv[rows, :] = dvj[:BLK]
                nv[rows, :] = dvj[BLK:]

        @pl.when(n == nb)
        def _():
            for ref in (pk, pv, nk, nv):
                ref[...] = jnp.zeros_like(ref)

        khp = jnp.concatenate([khat[kb * j:kb * j + BLK] for j in range(HKV)], axis=0)
        rkp = jnp.concatenate([rk[kb * j:kb * j + BLK] for j in range(HKV)], axis=0)
        dkn = ck[...] + pk[...]
        dkhat = dkn * kw
        dk = rkp * (dkhat - khp * jnp.mean(dkhat * khp, axis=-1, keepdims=True))
        dkw_ref[...] += _csum(dkn * khp)
        dkv_ref[...] = jnp.concatenate([_unstack(dk, HKV), _unstack(cv[...] + pv[...], HKV)], axis=1).astype(BF16)
        ck[...] = nk[...]
        cv[...] = nv[...]

    kblk, vblk = C_K // 256, C_V // 256
    cur = lambda n: jnp.minimum(n, nb - 1)
    prev = lambda n: jnp.maximum(n - 1, 0)
    carry = pltpu.VMEM((HKV * BLK, DH), F32)
    return pl.pallas_call(
        body, name="attn_bwd", grid=(nb + 1,),
        in_specs=[pl.BlockSpec((BLK, D), lambda n: (cur(n), 0)),
                  pl.BlockSpec((BLK, 256), lambda n: (cur(n), kblk)), pl.BlockSpec((BLK, 256), lambda n: (prev(n), kblk)),
                  pl.BlockSpec((BLK, 256), lambda n: (cur(n), vblk)), pl.BlockSpec((BLK, 256), lambda n: (prev(n), vblk)),
                  pl.BlockSpec((BLK, D), lambda n: (cur(n), 0)),
                  pl.BlockSpec((1, HQ, BLK, kb), lambda n: (jnp.minimum(n, 1), 0, 0, 0)),
                  _full((1, DH)), _full((1, DH))],
        out_specs=[pl.BlockSpec((BLK, D), lambda n: (cur(n), 0)),
                   pl.BlockSpec((BLK, 512), lambda n: (prev(n), 0)),
                   _full((1, DH)), _full((1, DH)), _full((HQ, BLK, kb))],
        out_shape=[jax.ShapeDtypeStruct((t, D), BF16), jax.ShapeDtypeStruct((t, 512), BF16),
                   jax.ShapeDtypeStruct((1, DH), F32),
                   jax.ShapeDtypeStruct((1, DH), F32), jax.ShapeDtypeStruct((HQ, BLK, kb), F32)],
        scratch_shapes=[carry] * 6,
        compiler_params=_cp(("arbitrary",)),
    )(proj, proj, proj, proj, proj, dao, biasm, q_norm_w, k_norm_w)


CONV_TM, CONV_CW, CONV_RC, HALO = 1024, 1024, 32, 16


def conv_fwd(proj, conv_w, conv_b):
    t = proj.shape[0]
    tm = min(t, CONV_TM)
    c0 = C_XBC // CONV_CW

    def body(x_ref, xp_ref, w_ref, b_ref, o_ref, ds_ref):
        i = pl.program_id(1)
        w = w_ref[...]
        b = b_ref[...]
        for r in range(tm // CONV_RC):
            lo = r * CONV_RC
            if r == 0:
                head = jnp.where(i == 0, 0.0, xp_ref[...].astype(F32))
                win = jnp.concatenate([head, x_ref[0:CONV_RC, :].astype(F32)], axis=0)
            else:
                win = x_ref[lo - HALO:lo + CONV_RC, :].astype(F32)
            acc = b
            for j in range(CONV_K):
                acc = acc + w[j:j + 1] * win[HALO - 3 + j:HALO - 3 + j + CONV_RC]
            sg = _sig(acc)
            o_ref[lo:lo + CONV_RC, :] = acc * sg
            ds_ref[lo:lo + CONV_RC, :] = _dsilu(acc, sg).astype(BF16)

    rh = tm // HALO
    tile = pl.BlockSpec((tm, CONV_CW), lambda s, i: (i, s))
    return pl.pallas_call(
        body, name="conv_fwd", grid=(XBC // CONV_CW, t // tm),
        in_specs=[pl.BlockSpec((tm, CONV_CW), lambda s, i: (i, c0 + s)),
                  pl.BlockSpec((HALO, CONV_CW), lambda s, i: (jnp.maximum(i * rh - 1, 0), c0 + s)),
                  pl.BlockSpec((CONV_K, CONV_CW), lambda s, i: (0, s)), pl.BlockSpec((1, CONV_CW), lambda s, i: (0, s))],
        out_specs=[tile, tile],
        out_shape=[jax.ShapeDtypeStruct((t, XBC), F32), jax.ShapeDtypeStruct((t, XBC), BF16)],
        compiler_params=_cp(("parallel", "parallel")),
    )(proj, proj, conv_w, conv_b)


def conv_bwd(proj, dact, dsl, conv_w):
    t = proj.shape[0]
    tm = min(t, CONV_TM)
    nt = t // tm
    nr = tm // CONV_RC
    c0 = C_XBC // CONV_CW
    ext = CONV_RC + 8

    def body(x_ref, xp_ref, d_ref, dn_ref, s_ref, sn_ref, w_ref, dx_ref, dw_ref, db_ref):
        i = pl.program_id(1)

        @pl.when(i == 0)
        def _():
            dw_ref[...] = jnp.zeros_like(dw_ref)
            db_ref[...] = jnp.zeros_like(db_ref)

        w = w_ref[...]
        dws = [jnp.zeros((1, CONV_CW), F32) for _ in range(CONV_K)]
        db = jnp.zeros((1, CONV_CW), F32)
        for r in range(nr):
            lo = r * CONV_RC
            if r == 0:
                head = jnp.where(i == 0, 0.0, xp_ref[...].astype(F32))
                win = jnp.concatenate([head, x_ref[0:CONV_RC, :].astype(F32)], axis=0)
            else:
                win = x_ref[lo - HALO:lo + CONV_RC, :].astype(F32)
            if r < nr - 1:
                dext = d_ref[lo:lo + ext, :]
                sext = s_ref[lo:lo + CONV_RC + HALO, :].astype(F32)[0:ext]
            else:
                dext = jnp.concatenate([d_ref[lo:lo + CONV_RC, :], jnp.where(i == nt - 1, 0.0, dn_ref[...])], axis=0)
                sext = jnp.concatenate([s_ref[lo:lo + CONV_RC, :].astype(F32), sn_ref[...].astype(F32)], axis=0)[0:ext]
            dpre = dext * sext
            dx = jnp.zeros((CONV_RC, CONV_CW), F32)
            own = dpre[0:CONV_RC]
            for j in range(CONV_K):
                dx = dx + w[j:j + 1] * dpre[3 - j:3 - j + CONV_RC]
                dws[j] = dws[j] + _csum(own * win[HALO - 3 + j:HALO - 3 + j + CONV_RC])
            db = db + _csum(own)
            dx_ref[lo:lo + CONV_RC, :] = dx.astype(BF16)
        dw_ref[...] += jnp.concatenate(dws, axis=0)
        db_ref[...] += db

    rh = tm // HALO
    r8 = tm // 8
    nxt = lambda i, per: jnp.minimum((i + 1) * per, nt * per - 1)
    return pl.pallas_call(
        body, name="conv_bwd", grid=(XBC // CONV_CW, nt),
        in_specs=[pl.BlockSpec((tm, CONV_CW), lambda s, i: (i, c0 + s)),
                  pl.BlockSpec((HALO, CONV_CW), lambda s, i: (jnp.maximum(i * rh - 1, 0), c0 + s)),
                  pl.BlockSpec((tm, CONV_CW), lambda s, i: (i, s)),
                  pl.BlockSpec((8, CONV_CW), lambda s, i: (nxt(i, r8), s)),
                  pl.BlockSpec((tm, CONV_CW), lambda s, i: (i, s)),
                  pl.BlockSpec((HALO, CONV_CW), lambda s, i: (nxt(i, rh), s)),
                  pl.BlockSpec((CONV_K, CONV_CW), lambda s, i: (0, s))],
        out_specs=[pl.BlockSpec((tm, CONV_CW), lambda s, i: (i, s)),
                   pl.BlockSpec((CONV_K, CONV_CW), lambda s, i: (0, s)), pl.BlockSpec((1, CONV_CW), lambda s, i: (0, s))],
        out_shape=[jax.ShapeDtypeStruct((t, XBC), BF16), jax.ShapeDtypeStruct((CONV_K, XBC), F32),
                   jax.ShapeDtypeStruct((1, XBC), F32)],
        compiler_params=_cp(("parallel", "arbitrary")),
    )(proj, proj, dact, dact, dsl, dsl, conv_w)


def _split3(x):
    h = x.astype(BF16)
    r = x - h.astype(F32)
    m = r.astype(BF16)
    lo = (r - m.astype(F32)).astype(BF16)
    return h, m, lo


def _tri_mm(tri, x):
    h, m, lo = _split3(x)
    return _dot(tri, h) + _dot(tri, m) + _dot(tri, lo)


def _softplus(x):
    return jnp.maximum(x, 0.0) + jnp.log1p(jnp.exp(-jnp.abs(x)))


def _chunk_decays(dt_raw, dtb, alog):
    dtv = _softplus(dt_raw + dtb)
    a = -jnp.exp(alog)
    ri = lax.broadcasted_iota(jnp.int32, (BLK, BLK), 0)
    ci = lax.broadcasted_iota(jnp.int32, (BLK, BLK), 1)
    causal = ri >= ci
    acum = _tri_mm(causal.astype(BF16), dtv * a)
    return dtv, a, causal, acum, acum.T


NPAIR = SH // 2


def _pairs(x):
    return jnp.stack([x[:, 128 * k:128 * (k + 1)] for k in range(NPAIR)])


def _unpairs(x3):
    return jnp.concatenate([x3[k] for k in range(NPAIR)], axis=1)


def _per_head_cols(m):
    return jnp.stack([jnp.broadcast_to(m[:, h:h + 1], m.shape) for h in range(SH)])


def _pair_lanes(t):
    r = t.reshape(NPAIR, 2, t.shape[1], 128)
    lo = lax.broadcasted_iota(jnp.int32, (1, t.shape[1], 128), 2) < SP
    return jnp.where(lo, r[:, 0], r[:, 1])


class _Chunk:
    pass


def _chunk_common(dt_raw, dtb, alog, dskip):
    cm = _Chunk()
    cm.dtv, cm.a, cm.causal, acum, acum_t = _chunk_decays(dt_raw, dtb, alog)
    cm.acol = _per_head_cols(acum)
    cm.arow = jnp.stack([acum_t[h:h + 1, :] for h in range(SH)])
    apl = _pair_lanes(cm.acol)
    alast = apl[:, BLK - 1:BLK, :]
    cm.dpl = _pair_lanes(_per_head_cols(cm.dtv))
    cm.eapl = jnp.exp(apl)
    cm.epl = jnp.exp(alast - apl)
    cm.cdpl = jnp.exp(alast)
    cm.dskpl = _pair_lanes(_per_head_cols(dskip))
    cm.lo = lax.broadcasted_iota(jnp.int32, (1, BLK, 128), 2) < SP
    return cm


def ssd_fwd(act, dt_raw, dtb_p, alog_p, dsk_p):
    t = act.shape[0]
    nc = t // BLK

    def body(xs_ref, b_ref, c_ref, dt_ref, dtb_ref, al_ref, dk_ref, y_ref, sp_ref, st):
        c = pl.program_id(0)

        @pl.when(c == 0)
        def _():
            st[...] = jnp.zeros_like(st)

        s_t = st[...]
        sp_ref[0] = s_t
        cm = _chunk_common(dt_ref[...], dtb_ref[...], al_ref[...], dk_ref[...])
        gms, cbs, bts = [], [], []
        for g in range(SG):
            bf = b_ref[:, SN * g:SN * (g + 1)]
            cb = c_ref[:, SN * g:SN * (g + 1)].astype(BF16)
            gms.append(_dot_nt(cb, bf.astype(BF16)))
            cbs.append(cb)
            bts.append(bf.T.astype(BF16))
        lam = jnp.exp(jnp.where(cm.causal[None], cm.acol - cm.arow, NEG))
        m = (lam.reshape(SG, SR, BLK, BLK) * jnp.stack(gms)[:, None]).reshape(SH, BLK, BLK).astype(BF16)
        xs16 = _pairs(xs_ref[...])
        xdt16 = xs16 * cm.dpl
        x_lo = jnp.where(cm.lo, xdt16, 0.0).astype(BF16)
        x_hi = jnp.where(cm.lo, 0.0, xdt16).astype(BF16)
        s16 = _pairs(s_t)
        s16b = s16.astype(BF16)
        yd = jnp.stack([_dot(m[2 * k], x_lo[k]) + _dot(m[2 * k + 1], x_hi[k]) for k in range(NPAIR)])
        yo = jnp.stack([_dot(cbs[k // (NPAIR // SG)], s16b[k]) for k in range(NPAIR)])
        y_ref[...] = _unpairs(yd + yo * cm.eapl + cm.dskpl * xs16).astype(BF16)
        xe = (xdt16 * cm.epl).astype(BF16)
        st[...] = _unpairs(cm.cdpl * s16 + jnp.stack([_dot(bts[k // (NPAIR // SG)], xe[k]) for k in range(NPAIR)]))

    vec = _full((1, 128))
    return pl.pallas_call(
        body, name="ssd_fwd", grid=(nc,),
        in_specs=[pl.BlockSpec((BLK, SSM_W), lambda c: (c, 0)),
                  pl.BlockSpec((BLK, SG * SN), lambda c: (c, SSM_W // (SG * SN))),
                  pl.BlockSpec((BLK, SG * SN), lambda c: (c, SSM_W // (SG * SN) + 1)),
                  pl.BlockSpec((BLK, 128), lambda c: (c, 0)), vec, vec, vec],
        out_specs=[pl.BlockSpec((BLK, SSM_W), lambda c: (c, 0)), pl.BlockSpec((1, SN, SSM_W), lambda c: (c, 0, 0))],
        out_shape=[jax.ShapeDtypeStruct((t, SSM_W), BF16), jax.ShapeDtypeStruct((nc, SN, SSM_W), F32)],
        scratch_shapes=[pltpu.VMEM((SN, SSM_W), F32)],
        compiler_params=_cp(("arbitrary",)),
    )(act, act, act, dt_raw, dtb_p, alog_p, dsk_p)


def _head_sums(q):
    r = q.shape[1]
    lo = lax.broadcasted_iota(jnp.int32, (1, r, 128), 2) < SP
    s_lo = jnp.sum(jnp.where(lo, q, 0.0), axis=-1, keepdims=True)
    s_hi = jnp.sum(jnp.where(lo, 0.0, q), axis=-1, keepdims=True)
    lane = lax.broadcasted_iota(jnp.int32, (r, 128), 1)
    out = jnp.zeros((r, 128), F32)
    for k in range(NPAIR):
        out = jnp.where(lane == 2 * k, s_lo[k], jnp.where(lane == 2 * k + 1, s_hi[k], out))
    return out


def ssd_bwd(act, dt_raw, dy, sprev, dtb_p, alog_p, dsk_p):
    t = act.shape[0]
    nc = t // BLK

    def body(xs_ref, b_ref, c_ref, dt_ref, dy_ref, sp_ref, dtb_ref, al_ref, dk_ref,
             da_ref, ddt_ref, ddtb_ref, dal_ref, ddk_ref, dst):
        i = pl.program_id(0)

        @pl.when(i == 0)
        def _():
            dst[...] = jnp.zeros_like(dst)
            ddtb_ref[...] = jnp.zeros_like(ddtb_ref)
            dal_ref[...] = jnp.zeros_like(dal_ref)
            ddk_ref[...] = jnp.zeros_like(ddk_ref)

        dt_raw = dt_ref[...]
        dtb = dtb_ref[...]
        cm = _chunk_common(dt_raw, dtb, al_ref[...], dk_ref[...])
        ri = lax.broadcasted_iota(jnp.int32, (BLK, BLK), 0)
        ci = lax.broadcasted_iota(jnp.int32, (BLK, BLK), 1)
        lam_t = jnp.exp(jnp.where((ri <= ci)[None], cm.arow - cm.acol, NEG))
        bbs, cbs, cts, gms = [], [], [], []
        for g in range(SG):
            bf = b_ref[:, SN * g:SN * (g + 1)]
            cf = c_ref[:, SN * g:SN * (g + 1)]
            bbs.append(bf.astype(BF16))
            cbs.append(cf.astype(BF16))
            cts.append(cf.T.astype(BF16))
            gms.append(_dot_nt(bbs[g], cbs[g]))
        grp = lambda k: k // (NPAIR // SG)
        xs16 = _pairs(xs_ref[...])
        dy16 = _pairs(dy_ref[...].astype(F32))
        sp16 = _pairs(sp_ref[0])
        ds16 = _pairs(dst[...])
        xdt16 = xs16 * cm.dpl
        xdtb = xdt16.astype(BF16)
        dyh = [jnp.where(cm.lo, dy16, 0.0).astype(BF16), jnp.where(cm.lo, 0.0, dy16).astype(BF16)]
        m_t = (lam_t.reshape(SG, SR, BLK, BLK) * jnp.stack(gms)[:, None]).reshape(SH, BLK, BLK).astype(BF16)
        dxdt = jnp.stack([_dot(m_t[2 * k], dyh[0][k]) + _dot(m_t[2 * k + 1], dyh[1][k]) for k in range(NPAIR)])
        dm_t = jnp.stack([_dot_nt(xdtb[h // 2], dyh[h % 2][h // 2]) for h in range(SH)])
        dg_t = jnp.sum((dm_t * lam_t).reshape(SG, SR, BLK, BLK), axis=1).astype(BF16)
        xq16 = xdtb.astype(F32)
        xh = [jnp.where(cm.lo, xdt16, 0.0).astype(BF16), jnp.where(cm.lo, 0.0, xdt16).astype(BF16)]
        y_in = jnp.stack([_dot_tn(m_t[2 * k], xh[0][k]) + _dot_tn(m_t[2 * k + 1], xh[1][k]) for k in range(NPAIR)])
        da_diag = dy16 * y_in - xq16 * dxdt
        lane_c = lax.broadcasted_iota(jnp.int32, (BLK, 128), 1)
        ds16b = ds16.astype(BF16)
        sp16b = sp16.astype(BF16)
        dxs = jnp.stack([_dot(bbs[grp(k)], ds16b[k]) for k in range(NPAIR)]) * cm.epl
        dxdt = dxdt + dxs
        dya = (dy16 * cm.eapl).astype(BF16)
        xe = (xdt16 * cm.epl).astype(BF16)
        dcs, dbs = [], []
        for g in range(SG):
            ks = range(g * (NPAIR // SG), (g + 1) * (NPAIR // SG))
            dcs.append(sum(_dot_nt(dya[k], sp16b[k]) for k in ks) + _dot_tn(dg_t[g], bbs[g]))
            dbs.append(sum(_dot_nt(xe[k], ds16b[k]) for k in ks) + _dot(dg_t[g], cbs[g]))
        dst[...] = _unpairs(cm.cdpl * ds16 + jnp.stack([_dot(cts[grp(k)], dya[k]) for k in range(NPAIR)]))
        da_ref[...] = jnp.concatenate([_unpairs(dxdt * cm.dpl + cm.dskpl * dy16)] + dbs + dcs, axis=1)
        y_off = jnp.stack([_dot(cbs[grp(k)], sp16b[k]) for k in range(NPAIR)]) * cm.eapl
        da_cols = _head_sums(da_diag + dy16 * y_off - xdt16 * dxs)
        last = _head_sums(jnp.sum(xdt16 * dxs, axis=1, keepdims=True)
                          + cm.cdpl * jnp.sum(ds16 * sp16, axis=1, keepdims=True))
        ddt = _head_sums(dxdt * xs16)
        row_i = lax.broadcasted_iota(jnp.int32, (BLK, 128), 0)
        dacum = da_cols + jnp.where(row_i == BLK - 1, last, 0.0)
        dda = _tri_mm((ri <= ci).astype(BF16), dacum)
        ddt = ddt + dda * cm.a
        dal_ref[...] += _csum(dda * cm.dtv) * cm.a
        ddt_raw = jnp.where(lane_c < SH, ddt * _sig(dt_raw + dtb), 0.0)
        ddt_ref[...] = ddt_raw.astype(BF16)
        ddtb_ref[...] += _csum(ddt_raw)
        ddk_ref[...] += _head_sums(jnp.sum(dy16 * xs16, axis=1, keepdims=True))

    rev = lambda i: nc - 1 - i
    vec = _full((1, 128))
    slab = pl.BlockSpec((BLK, SSM_W), lambda i: (rev(i), 0))
    return pl.pallas_call(
        body, name="ssd_bwd", grid=(nc,),
        in_specs=[slab,
                  pl.BlockSpec((BLK, SG * SN), lambda i: (rev(i), SSM_W // (SG * SN))),
                  pl.BlockSpec((BLK, SG * SN), lambda i: (rev(i), SSM_W // (SG * SN) + 1)),
                  pl.BlockSpec((BLK, 128), lambda i: (rev(i), 0)),
                  slab,
                  pl.BlockSpec((1, SN, SSM_W), lambda i: (rev(i), 0, 0)), vec, vec, vec],
        out_specs=[pl.BlockSpec((BLK, XBC), lambda i: (rev(i), 0)), pl.BlockSpec((BLK, 128), lambda i: (rev(i), 0)),
                   vec, vec, vec],
        out_shape=[jax.ShapeDtypeStruct((t, XBC), F32), jax.ShapeDtypeStruct((t, 128), BF16),
                   jax.ShapeDtypeStruct((1, 128), F32), jax.ShapeDtypeStruct((1, 128), F32),
                   jax.ShapeDtypeStruct((1, 128), F32)],
        scratch_shapes=[pltpu.VMEM((SN, SSM_W), F32)],
        compiler_params=_cp(("arbitrary",)),
    )(act, act, act, dt_raw, dy, sprev, dtb_p, alog_p, dsk_p)


TAIL_TM = 256


def _dsilu(z, s):
    return s * (1.0 + z * (1.0 - s))


def tail(proj, ao, yss, x, target, gate, ssm_nw, w_at, w_ss, w_ou):
    t = x.shape[0]
    tm = min(t, TAIL_TM)
    gw = SSM_W // SG

    def body(ao_ref, za_ref, ga_ref, gb_ref, zm_ref, ys_ref, x_ref, tg_ref, gt_ref, nw_ref, wa_ref, ws_ref, wo_ref,
             loss_ref, dy_ref, dao_ref, dmid_ref, dys_ref,
             ua_ref, yn_ref, mg_ref, dya_ref, dyb_ref, do_ref, dgt_ref, dnw_ref):
        i = pl.program_id(0)

        @pl.when(i == 0)
        def _():
            loss_ref[...] = jnp.zeros_like(loss_ref)
            dgt_ref[...] = jnp.zeros_like(dgt_ref)
            dnw_ref[...] = jnp.zeros_like(dnw_ref)

        ao = ao_ref[...].astype(F32)
        za = za_ref[...].astype(F32)
        sa = _sig(za)
        sila = za * sa
        ua_f = ao * sila
        ua = ua_f.astype(BF16)
        ya = _dot(ua, wa_ref[...])
        zm = zm_ref[...].astype(F32)
        sm = _sig(zm)
        silm = zm * sm
        ys = ys_ref[...].astype(F32)
        u = ys * silm
        nw = nw_ref[...]
        rs, uns = [], []
        for g in range(SG):
            ug = u[:, gw * g:gw * (g + 1)]
            r = lax.rsqrt(jnp.mean(ug * ug, axis=-1, keepdims=True) + EPS)
            rs.append(r)
            uns.append(ug * r)
        un = jnp.concatenate(uns, axis=1)
        yn_f = un * nw
        yn = yn_f.astype(BF16)
        yb = _dot(yn, ws_ref[...])
        sga = _sig(ga_ref[...].astype(F32))
        sgb = _sig(gb_ref[...].astype(F32))
        mg_f = sga * ya + sgb * yb
        mg = mg_f.astype(BF16)
        o = _dot(mg, wo_ref[...])
        gt = gt_ref[...]
        err = (x_ref[...] + gt * o) - tg_ref[...]
        lane = lax.broadcasted_iota(jnp.int32, (1, 128), 1)
        loss_ref[...] += jnp.where(lane == 0, 0.5 * _asum(_rsum(err * err) / D), 0.0)
        dy = err * (1.0 / D)
        dy_ref[...] = dy
        dgt_ref[...] += _csum(dy * o)
        do = (dy * gt).astype(BF16)
        dmg = _dot_nt(do, wo_ref[...])
        dmid_ref[:, C_GA - C_ZA:C_GB - C_ZA] = (dmg * ya * sga * (1.0 - sga)).astype(BF16)
        dmid_ref[:, C_GB - C_ZA:C_ZM - C_ZA] = (dmg * yb * sgb * (1.0 - sgb)).astype(BF16)
        dya = (dmg * sga).astype(BF16)
        dyb = (dmg * sgb).astype(BF16)
        dua = _dot_nt(dya, wa_ref[...])
        dao_ref[...] = (dua * sila).astype(BF16)
        dmid_ref[:, 0:C_GA - C_ZA] = (dua * ao * _dsilu(za, sa)).astype(BF16)
        dyn = _dot_nt(dyb, ws_ref[...])
        dnw_ref[...] += _csum(dyn * un)
        dun = dyn * nw
        dus = []
        for g in range(SG):
            gs = slice(gw * g, gw * (g + 1))
            dus.append(rs[g] * (dun[:, gs] - uns[g] * jnp.mean(dun[:, gs] * uns[g], axis=-1, keepdims=True)))
        du = jnp.concatenate(dus, axis=1)
        dys_ref[...] = (du * silm).astype(BF16)
        dmid_ref[:, C_ZM - C_ZA:] = (du * ys * _dsilu(zm, sm)).astype(BF16)
        ua_ref[...] = ua_f.T.astype(BF16)
        yn_ref[...] = yn_f.T.astype(BF16)
        mg_ref[...] = mg_f.T.astype(BF16)
        dya_ref[...] = dya
        dyb_ref[...] = dyb
        do_ref[...] = do

    row = lambda w: pl.BlockSpec((tm, w), lambda i: (i, 0))
    pcol = lambda w, c0: pl.BlockSpec((tm, w), lambda i: (i, c0 // w))
    sd = lambda w, dt: jax.ShapeDtypeStruct((t, w), dt)
    colt = lambda w: pl.BlockSpec((w, tm), lambda i: (0, i))
    sdt = lambda w: jax.ShapeDtypeStruct((w, t), BF16)
    return pl.pallas_call(
        body, name="tail", grid=(t // tm,),
        in_specs=[row(D), pcol(D, C_ZA), pcol(D, C_GA), pcol(D, C_GB), pcol(SSM_W, C_ZM), row(SSM_W), row(D), row(D),
                  _full((1, D)), _full((1, SSM_W)), _full((D, D)), _full((SSM_W, D)), _full((D, D))],
        out_specs=[_full((1, 128)), row(D), row(D), row(W_MID), row(SSM_W),
                   colt(D), colt(SSM_W), colt(D), row(D), row(D), row(D), _full((1, D)), _full((1, SSM_W))],
        out_shape=[jax.ShapeDtypeStruct((1, 128), F32), sd(D, F32), sd(D, BF16), sd(W_MID, BF16),
                   sd(SSM_W, BF16), sdt(D), sdt(SSM_W), sdt(D), sd(D, BF16),
                   sd(D, BF16), sd(D, BF16), jax.ShapeDtypeStruct((1, D), F32), jax.ShapeDtypeStruct((1, SSM_W), F32)],
        compiler_params=_cp(("arbitrary",)),
    )(ao, proj, proj, proj, proj, yss, x, target, gate, ssm_nw, w_at, w_ss, w_ou)


DPIECES = ((D, ((D, C_Q),)),
           (W_MID, ((D, C_ZA), (D, C_GA), (D, C_GB), (SSM_W, C_ZM))),
           (XBC, ((XBC, C_XBC),)),
           (512, ((512, C_K),)),
           (128, ((128, C_DT),)))


def dproj_bwd(pieces, wcat, x, dy, norm_w, scale):
    t = x.shape[0]
    tm = min(t, 256)
    nt = t // tm
    wblocks = [blk for _, subs in DPIECES for blk in subs]
    npc, nwb = len(DPIECES), len(wblocks)

    def body(*refs):
        p_refs, w_refs = refs[:npc], refs[npc:npc + nwb]
        x_ref, dy_ref, nw_ref, sc_ref, gx_ref, dnw_ref, dsc_ref, dsh_ref, dwe_ref = refs[npc + nwb:]
        i = pl.program_id(0)

        @pl.when(i == 0)
        def _():
            for ref in (dwe_ref, dsh_ref, dnw_ref, dsc_ref):
                ref[...] = jnp.zeros_like(ref)

        dh, wi = None, 0
        for p_ref, (_, subs) in zip(p_refs, DPIECES):
            loc = 0
            for w, _ in subs:
                part = _dot_nt(p_ref[:, loc:loc + w], w_refs[wi][...])
                dh = part if dh is None else dh + part
                loc += w
                wi += 1
        xv = x_ref[...]
        r = lax.rsqrt(jnp.mean(xv * xv, axis=-1, keepdims=True) + EPS)
        xn = xv * r
        weff = nw_ref[...] * (1.0 + sc_ref[...])
        dxn = dh * weff
        gx_ref[...] = dy_ref[...] + r * (dxn - xn * jnp.mean(dxn * xn, axis=-1, keepdims=True))
        dwe_ref[...] += _csum(dh * xn)
        dsh_ref[...] += _csum(dh)

        @pl.when(i == nt - 1)
        def _():
            dwe = dwe_ref[...]
            dnw_ref[...] = dwe * (1.0 + sc_ref[...])
            dsc_ref[...] = dwe * nw_ref[...]

    vec = pl.BlockSpec((1, D), lambda i: (0, 0))
    row = pl.BlockSpec((tm, D), lambda i: (i, 0))
    return pl.pallas_call(
        body, name="dproj_bwd", grid=(nt,),
        in_specs=[pl.BlockSpec((tm, pw), lambda i: (i, 0)) for pw, _ in DPIECES]
        + [pl.BlockSpec((D, w), functools.partial(lambda i, b: (0, b), b=off // w), pipeline_mode=pl.Buffered(1))
           for w, off in wblocks]
        + [row, row, vec, vec],
        out_specs=[row, vec, vec, vec],
        out_shape=[jax.ShapeDtypeStruct((t, D), F32), jax.ShapeDtypeStruct((1, D), F32),
                   jax.ShapeDtypeStruct((1, D), F32), jax.ShapeDtypeStruct((1, D), F32)],
        scratch_shapes=[pltpu.VMEM((1, D), F32)],
        compiler_params=_cp(("arbitrary",)),
    )(*pieces, *([wcat] * nwb), x, dy, norm_w, scale)


def wgrad(at, b, name, bn, after):
    m, t = at.shape
    n = b.shape[1]
    tk = min(t, 4096 if bn <= 1024 else 2048)
    bm = min(m, 1024)

    def body(a_ref, b_ref, after_ref, o_ref):
        part = _dot(a_ref[...], b_ref[...])

        @pl.when(pl.program_id(2) == 0)
        def _():
            o_ref[...] = part

        @pl.when(pl.program_id(2) > 0)
        def _():
            o_ref[...] += part

    return pl.pallas_call(
        body, name=name, grid=(m // bm, n // bn, t // tk),
        in_specs=[pl.BlockSpec((bm, tk), lambda i, j, k: (i, k)), pl.BlockSpec((tk, bn), lambda i, j, k: (k, j)), ANY],
        out_specs=pl.BlockSpec((bm, bn), lambda i, j, k: (i, j)),
        out_shape=jax.ShapeDtypeStruct((m, n), F32),
        compiler_params=_cp(("parallel", "parallel", "arbitrary")),
    )(at, b, after)


SUM_TR = 256


def pair_sum(g, core, theirs, name):
    w = g.shape[2]
    nh = HROWS // SUM_TR

    def body(core_ref, a_ref, b_ref, o_ref, ob_ref):
        s = a_ref[...] + b_ref[...]
        o_ref[...] = s
        ob_ref[...] = s.astype(BF16)

    spec = pl.BlockSpec((1, SUM_TR, w), lambda d, i, c: (d, i, 0))
    return pl.pallas_call(
        body, name=name,
        out_shape=[jax.ShapeDtypeStruct((4, HROWS, w), F32), jax.ShapeDtypeStruct((4, HROWS, w), BF16)],
        grid_spec=pltpu.PrefetchScalarGridSpec(
            num_scalar_prefetch=1, grid=(4, nh),
            in_specs=[pl.BlockSpec((1, SUM_TR, w), lambda d, i, c: (d, c[0] * nh + i, 0)), spec],
            out_specs=[spec, spec]),
        compiler_params=_cp(("parallel", "parallel")))(core.reshape(1).astype(jnp.int32), g, theirs)


def chip_sum(part, chip, others, name):
    r, w = part.shape[1:]

    def body(chip_ref, a_ref, b_ref, o_ref):
        acc = a_ref[0]
        for k in range(3):
            acc = acc + b_ref[k].astype(F32)
        o_ref[...] = acc

    return pl.pallas_call(
        body, name=name, out_shape=jax.ShapeDtypeStruct((r, w), F32),
        grid_spec=pltpu.PrefetchScalarGridSpec(
            num_scalar_prefetch=1, grid=(r // SUM_TR,),
            in_specs=[pl.BlockSpec((1, SUM_TR, w), lambda i, c: (c[0], i, 0)),
                      pl.BlockSpec((3, SUM_TR, w), lambda i, c: (0, i, 0))],
            out_specs=pl.BlockSpec((SUM_TR, w), lambda i, c: (i, 0))),
        compiler_params=_cp(("parallel",)))(chip.reshape(1).astype(jnp.int32), part, others)


def sum_devices(g):
    r = g.shape[1]

    def body(g_ref, o_ref):
        acc = g_ref[0]
        for d in range(1, 8):
            acc = acc + g_ref[d]
        o_ref[...] = acc

    return pl.pallas_call(body, name="sum_devices", out_shape=jax.ShapeDtypeStruct((r, 1024), F32),
                          compiler_params=_cp())(g)


def adamw(w, g, m, v, name):
    r, c = w.shape
    tr = r
    for cand in (256, 128, 64, 32, 16, 8):
        if r % cand == 0 and r > cand:
            tr = cand
            break

    def body(w_ref, g_ref, m_ref, v_ref, d_ref, nm_ref, nv_ref):
        gv = g_ref[...]
        mn = ADAM_B1 * m_ref[...] + (1.0 - ADAM_B1) * gv
        vn = ADAM_B2 * v_ref[...] + (1.0 - ADAM_B2) * (gv * gv)
        m_hat = mn / (1.0 - ADAM_B1 ** ADAM_STEP)
        v_hat = vn / (1.0 - ADAM_B2 ** ADAM_STEP)
        d_ref[...] = -ADAM_LR * (m_hat / (jnp.sqrt(v_hat) + ADAM_EPS) + ADAM_WD * w_ref[...])
        nm_ref[...] = mn
        nv_ref[...] = vn

    spec = pl.BlockSpec((tr, c), lambda i: (i, 0))
    sd = jax.ShapeDtypeStruct((r, c), F32)
    return pl.pallas_call(body, name=name, grid=(r // tr,), in_specs=[spec] * 4, out_specs=[spec] * 3,
                          out_shape=[sd, sd, sd], compiler_params=_cp(("parallel",)))(w, g, m, v)


def adamw_halves(w, mine, theirs, core, m, v, name):
    r, c = w.shape
    tr = 128
    nh = HROWS // tr

    def body(core_ref, w_ref, a_ref, b_ref, m_ref, v_ref, g_ref, d_ref, nm_ref, nv_ref):
        gv = jnp.where(pl.program_id(0) // nh == core_ref[0], a_ref[...], b_ref[...])
        mn = ADAM_B1 * m_ref[...] + (1.0 - ADAM_B1) * gv
        vn = ADAM_B2 * v_ref[...] + (1.0 - ADAM_B2) * (gv * gv)
        m_hat = mn / (1.0 - ADAM_B1 ** ADAM_STEP)
        v_hat = vn / (1.0 - ADAM_B2 ** ADAM_STEP)
        g_ref[...] = gv
        d_ref[...] = -ADAM_LR * (m_hat / (jnp.sqrt(v_hat) + ADAM_EPS) + ADAM_WD * w_ref[...])
        nm_ref[...] = mn
        nv_ref[...] = vn

    spec = pl.BlockSpec((tr, c), lambda i, s: (i, 0))
    half = pl.BlockSpec((tr, c), lambda i, s: (i % nh, 0))
    sd = jax.ShapeDtypeStruct((r, c), F32)
    return pl.pallas_call(
        body, name=name, out_shape=[sd, sd, sd, sd],
        grid_spec=pltpu.PrefetchScalarGridSpec(num_scalar_prefetch=1, grid=(r // tr,),
                                               in_specs=[spec, half, half, spec, spec], out_specs=[spec] * 4),
        compiler_params=_cp(("parallel",)))(core.reshape(1).astype(jnp.int32), w, mine, theirs, m, v)


ANY = pl.BlockSpec(memory_space=pl.ANY)
VM = pl.BlockSpec(memory_space=pltpu.VMEM)
OTHER_CHIPS = ((1, 0), (0, 1), (1, 1))


def _pos():
    return lax.axis_index("x"), lax.axis_index("y"), lax.axis_index("c")


def _flip(v, bit):
    return 1 - v if bit else v


def _rcopy(src, dst, ssem, rsem, peer):
    return pltpu.make_async_remote_copy(src_ref=src, dst_ref=dst, send_sem=ssem, recv_sem=rsem,
                                        device_id=peer, device_id_type=MESH)


def allgather_small(p, name):
    r = p.shape[0]

    def body(in_ref, out_ref, ssem, rsem, lsem):
        x, y, c = _pos()
        me = 4 * x + 2 * y + c
        loc = pltpu.make_async_copy(in_ref, out_ref.at[me], lsem)
        loc.start()
        sends = []
        peers = []
        for k in range(1, 8):
            px, py, pc = _flip(x, (k >> 2) & 1), _flip(y, (k >> 1) & 1), _flip(c, k & 1)
            peers.append((px, py, pc))
            cp = _rcopy(in_ref, out_ref.at[me], ssem.at[k - 1], rsem.at[k - 1], (px, py, pc))
            cp.start()
            sends.append(cp)
        for k in range(1, 8):
            px, py, pc = peers[k - 1]
            _rcopy(in_ref, out_ref.at[4 * px + 2 * py + pc], ssem.at[k - 1], rsem.at[k - 1], (px, py, pc)).wait_recv()
        for cp in sends:
            cp.wait_send()
        loc.wait()

    return pl.pallas_call(
        body, name=name, out_shape=jax.ShapeDtypeStruct((8, r, 1024), F32),
        in_specs=[VM], out_specs=VM,
        scratch_shapes=[pltpu.SemaphoreType.DMA((7,)), pltpu.SemaphoreType.DMA((7,)), pltpu.SemaphoreType.DMA],
    )(p)


def gather_weights(w_in_b, mod_sh):
    def body(wi_ref, m_ref, gi_ref, mo_ref, ssem, rsem, lsem):
        x, y, c = _pos()
        chip = 2 * x + y
        mine = pl.ds(pl.multiple_of(c * HROWS, 16), HROWS)
        other = pl.ds(pl.multiple_of((1 - c) * HROWS, 16), HROWS)
        sib = (x, y, 1 - c)
        pairs = ((wi_ref, gi_ref),)
        loc_m = pltpu.make_async_copy(m_ref, mo_ref.at[chip], lsem)
        loc_m.start()
        sends = []
        for k, (fx, fy) in enumerate(OTHER_CHIPS):
            peer = (_flip(x, fx), _flip(y, fy), c)
            for a, (w_ref, g_ref) in enumerate(pairs):
                cw = _rcopy(w_ref.at[mine], g_ref.at[chip, mine], ssem.at[6 * a + k], rsem.at[6 * a + k], peer)
                cw.start()
                sends.append(cw)
            cm = _rcopy(m_ref, mo_ref.at[chip], ssem.at[12 + k], rsem.at[12 + k], peer)
            cm.start()
            sends.append(cm)
        for k, (fx, fy) in enumerate(OTHER_CHIPS):
            px, py = _flip(x, fx), _flip(y, fy)
            for a, (w_ref, g_ref) in enumerate(pairs):
                got = g_ref.at[2 * px + py, mine]
                _rcopy(w_ref.at[mine], got, ssem.at[6 * a + k], rsem.at[6 * a + k], (px, py, c)).wait_recv()
                fw = _rcopy(got, got, ssem.at[6 * a + 3 + k], rsem.at[6 * a + 3 + k], sib)
                fw.start()
                sends.append(fw)
        for k, (fx, fy) in enumerate(OTHER_CHIPS):
            px, py = _flip(x, fx), _flip(y, fy)
            for a, (w_ref, g_ref) in enumerate(pairs):
                land = g_ref.at[2 * px + py, other]
                _rcopy(land, land, ssem.at[6 * a + 3 + k], rsem.at[6 * a + 3 + k], sib).wait_recv()
            _rcopy(m_ref, mo_ref.at[2 * px + py], ssem.at[12 + k], rsem.at[12 + k], (px, py, c)).wait_recv()
        for cp in sends:
            cp.wait_send()
        loc_m.wait()

    return pl.pallas_call(
        body, name="gather_weights",
        out_shape=[jax.ShapeDtypeStruct((4, D, SH_IN), BF16), jax.ShapeDtypeStruct((4, 8, 768), F32)],
        in_specs=[ANY, VM], out_specs=[ANY, VM],
        scratch_shapes=[pltpu.SemaphoreType.DMA((15,)), pltpu.SemaphoreType.DMA((15,)), pltpu.SemaphoreType.DMA],
    )(w_in_b, mod_sh)


def pair_exchange(g):
    def body(g_ref, r_ref, ssem, rsem):
        x, y, c = _pos()
        other = pl.ds(pl.multiple_of((1 - c) * HROWS, 8), HROWS)
        cp = _rcopy(g_ref.at[:, other, :], r_ref, ssem, rsem, (x, y, 1 - c))
        cp.start()
        cp.wait()

    return pl.pallas_call(
        body, name="pair_exchange", out_shape=jax.ShapeDtypeStruct((4, HROWS, g.shape[2]), F32),
        in_specs=[ANY], out_specs=ANY,
        scratch_shapes=[pltpu.SemaphoreType.DMA, pltpu.SemaphoreType.DMA],
    )(g)


HBM = pl.BlockSpec(memory_space=pltpu.HBM)
SEM = pl.BlockSpec(memory_space=pltpu.SEMAPHORE)
DATAFLOW = pltpu.SideEffectType.DATAFLOW_SIDE_EFFECTING


def split_start(name, make_copies, srcs, lands, nsem, after):
    arrays = [*srcs, *lands]
    n, ns = len(arrays), len(srcs)

    def body(*refs):
        for cp in make_copies(refs[:ns], refs[ns:n], refs[n + 1], refs[n + 2])[0]:
            cp.start()
        refs[-1][...] = jnp.zeros_like(refs[-1])

    res = pl.pallas_call(
        body, name=name,
        out_shape=(pltpu.SemaphoreType.DMA((nsem,)), pltpu.SemaphoreType.DMA((nsem,)),
                   *[pltpu.HBM(a.shape, a.dtype) for a in arrays], jax.ShapeDtypeStruct((8, 128), F32)),
        in_specs=(HBM,) * n + (ANY,), out_specs=(SEM, SEM) + (HBM,) * n + (VM,),
        input_output_aliases={i: 2 + i for i in range(n)},
        compiler_params=pltpu.CompilerParams(has_side_effects=DATAFLOW),
    )(*[pltpu.with_memory_space_constraint(a, pltpu.HBM) for a in arrays], after)
    return res[0], res[1], list(res[2:2 + n]), res[-1]


def split_wait(name, make_copies, ssem, rsem, arrays, ns, after):
    n = len(arrays)

    def body(*refs):
        sends, recvs = make_copies(refs[:ns], refs[ns:n], refs[n], refs[n + 1])
        for cp in sends:
            cp.wait_send()
        for cp in recvs:
            cp.wait_recv()

    return pl.pallas_call(
        body, name=name, out_shape=tuple(pltpu.HBM(a.shape, a.dtype) for a in arrays),
        in_specs=(HBM,) * n + (SEM, SEM, ANY), out_specs=(HBM,) * n,
        input_output_aliases={i: i for i in range(n)},
        compiler_params=pltpu.CompilerParams(has_side_effects=DATAFLOW),
    )(*arrays, ssem, rsem, after)


def _chip_copies(srcs, lands, ssem, rsem):
    x, y, c = _pos()
    copies = []
    for k, (fx, fy) in enumerate(OTHER_CHIPS):
        px, py = _flip(x, fx), _flip(y, fy)
        for a, (p_ref, l_ref) in enumerate(zip(srcs, lands)):
            copies.append(_rcopy(p_ref.at[2 * px + py], l_ref.at[k], ssem.at[3 * a + k], rsem.at[3 * a + k], (px, py, c)))
    return copies, copies


def _pair_copies(srcs, lands, ssem, rsem):
    x, y, c = _pos()
    other = pl.ds(pl.multiple_of((1 - c) * HROWS, 8), HROWS)
    copies = [_rcopy(srcs[0].at[:, other, :], lands[0], ssem.at[0], rsem.at[0], (x, y, 1 - c))]
    return copies, copies


def _rest_copies(srcs, lands, ssem, rsem):
    x, y, c = _pos()
    chip = 2 * x + y
    mine = pl.ds(pl.multiple_of(c * HROWS, 16), HROWS)
    sends, recvs = [], []
    for k, (fx, fy) in enumerate(OTHER_CHIPS):
        px, py = _flip(x, fx), _flip(y, fy)
        for t in range(2):
            rows_t = pl.ds(t * HROWS, HROWS)
            sends.append(_rcopy(srcs[0].at[mine], lands[0].at[chip, mine], ssem.at[2 * k + t], rsem.at[2 * k + c],
                                (px, py, t)))
            recvs.append(_rcopy(srcs[0].at[rows_t], lands[0].at[2 * px + py, rows_t], ssem.at[2 * k + t],
                                rsem.at[2 * k + t], (px, py, t)))
    return sends, recvs


def _swap_copies(srcs, lands, ssem, rsem):
    x, y, c = _pos()
    copies = [_rcopy(s_ref, l_ref, ssem.at[a], rsem.at[a], (x, y, 1 - c))
              for a, (s_ref, l_ref) in enumerate(zip(srcs, lands))]
    return copies, copies


def _flat(v, width=1024):
    v = v.reshape(-1)
    n = -(-v.shape[0] // width) * width
    return jnp.pad(v, (0, n - v.shape[0]))


def _rows(parts, rows):
    flat = jnp.concatenate(parts)
    return jnp.pad(flat, (0, rows * 1024 - flat.shape[0])).reshape(rows, 1024)


def _pack_small(b_ada, norm_w, conv_b, ssm_norm_w, q_norm_w, k_norm_w, sinks, dt_bias, a_log, d_skip, rel_bias,
                extra=None, tail=(), rows=16):
    misc = [q_norm_w, k_norm_w, sinks, dt_bias, a_log, d_skip] + ([] if extra is None else [extra])
    parts = [_flat(b_ada), _flat(norm_w), _flat(conv_b), _flat(ssm_norm_w)] + [_flat(v, 128) for v in misc]
    parts.append(jnp.zeros(((8 - len(misc)) * 128,), F32))
    parts.append(_flat(rel_bias))
    parts.append(jnp.zeros((5 * 1024,), F32))
    return _rows(parts + [_flat(v) for v in tail], rows)


def _unpack_small(p):
    misc = p[9]
    return dict(b_ada=p[0:3].reshape(1, 3072), norm_w=p[3:4], conv_b=p[4:7].reshape(1, 3072),
                ssm_norm_w=p[7:9].reshape(1, 2048), q_norm_w=misc[None, 0:64], k_norm_w=misc[None, 128:192],
                sinks=misc[None, 256:272], dt_bias=misc[None, 384:416], a_log=misc[None, 512:544],
                d_skip=misc[None, 640:672], rel_bias=p[10, :512].reshape(32, 16), extra=misc[768])


SMALL = ("b_ada", "norm_w", "conv_b", "ssm_norm_w", "q_norm_w", "k_norm_w", "sinks", "dt_bias", "a_log", "d_skip",
         "rel_bias")
WEIGHTS = ("w_ada", "b_ada", "norm_w", "w_in", "q_norm_w", "k_norm_w", "rel_bias", "sinks", "conv_w", "conv_b",
           "dt_bias", "a_log", "d_skip", "ssm_norm_w", "w_attn_proj", "w_ssm_proj", "w_out")
IN_COLS = ((0, 1024, C_Q), (1024, 256, C_K), (1280, 256, C_V), (1536, 1024, C_ZA), (2560, 2048, C_ZM),
           (4608, 3072, C_XBC), (7680, 32, C_DT), (7712, 1024, C_GA), (8736, 1024, C_GB))


def _to_cat(shards):
    parts, pos = [], 0
    for o, n, cnew in sorted(IN_COLS, key=lambda e: e[2]):
        assert cnew == pos
        c0 = o
        while c0 < o + n:
            i = c0 // SH_IN
            c1 = min(o + n, (i + 1) * SH_IN)
            parts.append(shards[i][:, c0 - i * SH_IN:c1 - i * SH_IN])
            c0 = c1
        pos += n
    parts.append(jnp.zeros((D, NP - pos), shards.dtype))
    return jnp.concatenate(parts, axis=1)


def _from_cat(dw_pieces):
    starts = [subs[0][1] for _, subs in DPIECES]

    def cols(c0, c1):
        p = max(q for q in range(len(starts)) if starts[q] <= c0)
        return dw_pieces[p][:, c0 - starts[p]:c1 - starts[p]]

    shards = []
    for i in range(4):
        lo, hi = i * SH_IN, (i + 1) * SH_IN
        parts = []
        for o, n, cnew in IN_COLS:
            a, b = max(o, lo), min(o + n, hi)
            if a < b:
                parts.append(cols(cnew + a - o, cnew + b - o))
        shards.append(jnp.concatenate(parts, axis=1))
    return jnp.stack(shards)


def kernel(x, c, w_ada, b_ada, norm_w, w_in, q_norm_w, k_norm_w, rel_bias, sinks, conv_w, conv_b, dt_bias, a_log, d_skip, ssm_norm_w, w_attn_proj, w_ssm_proj, w_out, loss_target, m_w_ada, m_b_ada, m_norm_w, m_w_in, m_q_norm_w, m_k_norm_w, m_rel_bias, m_sinks, m_conv_w, m_conv_b, m_dt_bias, m_a_log, m_d_skip, m_ssm_norm_w, m_w_attn_proj, m_w_ssm_proj, m_w_out, v_w_ada, v_b_ada, v_norm_w, v_w_in, v_q_norm_w, v_k_norm_w, v_rel_bias, v_sinks, v_conv_w, v_conv_b, v_dt_bias, v_a_log, v_d_skip, v_ssm_norm_w, v_w_attn_proj, v_w_ssm_proj, v_w_out):
    args = dict(locals())
    xi, yi, ci = lax.axis_index("x"), lax.axis_index("y"), lax.axis_index("c")
    chip = 2 * xi + yi
    me = 4 * xi + 2 * yi + ci
    x2 = x[0]
    tgt = loss_target[0]

    pay = _rows([c.reshape(-1), conv_w[0].reshape(-1)], 8)
    g0 = allgather_small(pay, "gather_cond")
    c_all = g0[:, 0, :]
    conv_w_full = g0[0::2, 1:4, :].reshape(4, CONV_K, 768).transpose(1, 0, 2).reshape(CONV_K, XBC)

    b_ada_sh = lax.dynamic_slice(b_ada, (0, chip * 768), (1, 768))
    mod_sh = ada_mod(c_all, w_ada[0], b_ada_sh)

    w_in_b = w_in[0].astype(BF16)
    w_rest_b = jnp.concatenate([w_attn_proj[0], w_ssm_proj[0], w_out[0]], axis=0).astype(BF16)
    wg_in, modg = gather_weights(w_in_b, mod_sh)
    wg_in = lax.dynamic_update_slice(wg_in, w_in_b[None], (chip, 0, 0))
    rs_sem, rr_sem, rest_thru, rest_tok = split_start("gather_rest_start", _rest_copies, [w_rest_b],
                                                      [lax.empty((4, D, D), BF16)], 6, modg)
    mod = lax.dynamic_slice(modg, (0, me, 0), (4, 1, 768)).reshape(1, 3 * D)
    shift, scale, gate = mod[:, :D], mod[:, D:2 * D] + rest_tok[:1, :1], mod[:, 2 * D:]
    wcat = _to_cat(wg_in)

    pad128 = lambda v: jnp.pad(v, ((0, 0), (0, 128 - v.shape[1])))
    dtb_p, alog_p, dsk_p = pad128(dt_bias), pad128(a_log), pad128(d_skip)
    bucket = _bucket_table()

    proj, dt_raw, h_t = norm_proj(x2, norm_w, scale, shift, wcat)
    biasm = bias_expand(rel_bias, sinks, bucket)
    ao = attn_fwd(proj, biasm, q_norm_w, k_norm_w)
    act, dsl = conv_fwd(proj, conv_w_full, conv_b)
    yss, sprev = ssd_fwd(act, dt_raw, dtb_p, alog_p, dsk_p)

    w_rest_b, wg_rest = split_wait("gather_rest_wait", _rest_copies, rs_sem, rr_sem, rest_thru, 1, yss)
    wg_rest = lax.dynamic_update_slice(wg_rest, w_rest_b[None], (chip, 0, 0))
    w_at = wg_rest[:, :R_AT].reshape(D, D)
    w_ss = wg_rest[:, R_AT:R_AT + R_SS].reshape(SSM_W, D)
    w_ou = wg_rest[:, R_AT + R_SS:].reshape(D, D)
    (loss_p, dy, dao, dmid, dyss, ua_t, yn_t, mg_t, dya, dyb, dout, dgate, dssm_nw) = tail(
        proj, ao, yss, x2, tgt, gate, ssm_norm_w, w_at, w_ss, w_ou)

    dq, dkv, dqw, dkw, dacc = attn_bwd(proj, dao, biasm, q_norm_w, k_norm_w)
    dbias = bias_reduce(dacc, bucket)
    drb = dbias[:, :NBUCKET].T
    dsk = dbias[:, NBUCKET].reshape(1, HQ)
    dact, ddt, ddtb, dalog, ddskip = ssd_bwd(act, dt_raw, dyss, sprev, dtb_p, alog_p, dsk_p)
    dxbc, dconv_w, dconv_b = conv_bwd(proj, dact, dsl, conv_w_full)

    dproj = (dq, dmid, dxbc, dkv, ddt)
    dwcat = [wgrad(h_t, piece, "dw_in_%d" % p, 1280 if piece.shape[1] == W_MID else min(piece.shape[1], 1024), rest_tok)
             for p, piece in enumerate(dproj)]

    g_in = _from_cat(dwcat)
    ps_sem, pr_sem, pair_thru, pair_tok = split_start("pair_in_start", _pair_copies, [g_in],
                                                      [lax.empty((4, HROWS, SH_IN), F32)], 1, loss_p)
    dw_at = wgrad(ua_t, dya, "dw_attn", 1024, pair_tok)
    dw_ss = wgrad(yn_t, dyb, "dw_ssm", 1024, pair_tok)
    dw_ou = wgrad(mg_t, dout, "dw_out", 1024, pair_tok)
    g_rest = jnp.concatenate([dw_at.reshape(4, R_AT, D), dw_ss.reshape(4, R_SS, D), dw_ou.reshape(4, R_OU, D)], axis=1)
    sib_rest = pair_exchange(g_rest)
    g_in, sib_in = split_wait("pair_in_wait", _pair_copies, ps_sem, pr_sem, pair_thru, 1, sib_rest)
    part_in, pb_in = pair_sum(g_in, ci, sib_in, "pair_sum_in")
    part_rest, pb_rest = pair_sum(g_rest, ci, sib_rest, "pair_sum_rest")
    cs_sem, cr_sem, chip_thru, token = split_start(
        "chip_exchange_start", _chip_copies, [pb_in, pb_rest],
        [lax.empty((3, HROWS, SH_IN), BF16), lax.empty((3, HROWS, D), BF16)], 6, part_rest)
    grad_x, dnorm_w, dscale, dshift = dproj_bwd(dproj, wcat, x2, dy, norm_w, scale + token[:1, :1])
    _, _, oth_in, oth_rest = split_wait("chip_exchange_wait", _chip_copies, cs_sem, cr_sem, chip_thru, 2, dshift)
    red_in = chip_sum(part_in, chip, oth_in, "chip_sum_in")
    red_rest = chip_sum(part_rest, chip, oth_rest, "chip_sum_rest")
    sw_ssem, sw_rsem, swap_thru, swap_tok = split_start(
        "pair_swap_start", _swap_copies, [red_in, red_rest],
        [lax.empty((HROWS, SH_IN), F32), lax.empty((HROWS, D), F32)], 2, red_rest)

    dmod = jnp.concatenate([dshift, dscale, dgate], axis=1)
    gsmall = _pack_small(dmod, dnorm_w, dconv_b, dssm_nw, dqw, dkw, dsk[:, :HQ], ddtb[:, :SH], dalog[:, :SH],
                         ddskip[:, :SH], drb, extra=loss_p[:, :1] + swap_tok[:1, :1], tail=(dconv_w,), rows=32)
    gall = allgather_small(gsmall, "gather_small_grads")
    ssum = sum_devices(gall)
    gs = _unpack_small(ssum[:16])
    loss = gs["extra"]
    dconv_w_sh = lax.dynamic_slice(ssum[16:28].reshape(CONV_K, XBC), (0, chip * 768), (CONV_K, 768))
    dmod_all = gall[:, 0:3, :].reshape(8, 3 * D)
    dw_ada = ada_grad(c_all, lax.dynamic_slice(dmod_all, (0, chip * 768), (8, 768)))

    grads = dict(gs)
    grads["w_ada"] = dw_ada
    grads["conv_w"] = dconv_w_sh

    delta, new_m, new_v = {}, {}, {}

    def step(n):
        delta[n], new_m[n], new_v[n] = adamw(args[n][0], grads[n], args["m_" + n][0], args["v_" + n][0], "adamw_" + n)

    step("w_ada")
    step("conv_w")
    ws = _pack_small(*[args[n] for n in SMALL])
    ms = _pack_small(*[args["m_" + n] for n in SMALL])
    vs = _pack_small(*[args["v_" + n] for n in SMALL])
    d_s, m_s, v_s = adamw(ws, ssum[:16], ms, vs, "adamw_small")
    red_in, red_rest, recv_in, recv_rest = split_wait("pair_swap_wait", _swap_copies, sw_ssem, sw_rsem, swap_thru, 2, d_s)
    d_s, m_s, v_s = _unpack_small(d_s), _unpack_small(m_s), _unpack_small(v_s)
    for n in SMALL:
        delta[n], new_m[n], new_v[n] = d_s[n], m_s[n], v_s[n]
    grads["w_in"], delta["w_in"], new_m["w_in"], new_v["w_in"] = adamw_halves(
        w_in[0], red_in, recv_in, ci, m_w_in[0], v_w_in[0], "adamw_w_in")
    g_shard_rest = jnp.concatenate([jnp.where(ci == 0, red_rest, recv_rest), jnp.where(ci == 0, recv_rest, red_rest)],
                                   axis=0)
    grads["w_attn_proj"] = g_shard_rest[:R_AT]
    grads["w_ssm_proj"] = g_shard_rest[R_AT:R_AT + R_SS]
    grads["w_out"] = g_shard_rest[R_AT + R_SS:]
    for n in ("w_attn_proj", "w_ssm_proj", "w_out"):
        step(n)

    def shaped(n, a):
        return a.reshape(args[n].shape)

    outs = [loss, grad_x[None]]
    for table in (grads, delta, new_m, new_v):
        outs += [shaped(n, table[n]) for n in WEIGHTS]
    return tuple(outs)
```

```python
import functools
import math

import numpy as np
import jax
import jax.numpy as jnp
from jax import lax
from jax.experimental import pallas as pl
from jax.experimental.pallas import tpu as pltpu

F32 = jnp.float32
BF16 = jnp.bfloat16
MESH = pl.DeviceIdType.MESH

D = 1024
HQ, HKV, GRP, DH = 16, 4, 4, 64
BLK = 128
NBUCKET, MAXDIST = 32, 128
SSM_W, SH, SG, SR, SP, SN = 2048, 32, 4, 8, 64, 128
CONV_K = 4
XBC = SSM_W + 2 * SG * SN
IN_W = 9760
EPS = 1e-6
NEG = -1e30
SCALE = DH ** -0.5

C_Q, C_ZA, C_GA, C_GB, C_ZM, C_XBC, C_K, C_V, C_DT = 0, 1024, 2048, 3072, 4096, 6144, 9216, 9472, 9728
NP = 9984
TN = 3328
W_MID = C_XBC - C_ZA

SH_IN = IN_W // 4
R_AT, R_SS, R_OU = 256, 512, 256
HROWS = D // 2

ADAM_LR, ADAM_B1, ADAM_B2, ADAM_EPS, ADAM_WD, ADAM_STEP = 0.001, 0.9, 0.999, 1e-08, 0.01, 10

VMEM_LIMIT = 56 * 1024 * 1024


def _cp(sem=None):
    if sem is None:
        return pltpu.CompilerParams(vmem_limit_bytes=VMEM_LIMIT)
    return pltpu.CompilerParams(dimension_semantics=sem, vmem_limit_bytes=VMEM_LIMIT)


def _sig(x):
    return 0.5 * jnp.tanh(0.5 * x) + 0.5


def _dot(a, b):
    return jnp.dot(a, b, preferred_element_type=F32)


def _dot_nt(a, b):
    return lax.dot_general(a, b, (((1,), (1,)), ((), ())), preferred_element_type=F32)


def _dot_tn(a, b):
    return lax.dot_general(a, b, (((0,), (0,)), ((), ())), preferred_element_type=F32)


def _rsum(x):
    return jnp.sum(x, axis=-1, keepdims=True)


def _csum(x):
    return jnp.sum(x, axis=0, keepdims=True)


def _asum(x):
    return _csum(_rsum(x))


def _full(shape):
    nd = len(shape)
    return pl.BlockSpec(shape, lambda *_: (0,) * nd)


def ada_mod(c_all, w_ada_sh, b_ada_sh):
    def body(c_ref, w_ref, b_ref, o_ref):
        cv = c_ref[...]
        s = cv * _sig(cv)
        o_ref[...] = jnp.dot(s, w_ref[...], preferred_element_type=F32,
                             precision=lax.Precision.HIGHEST) + b_ref[...]

    n = w_ada_sh.shape[1]
    return pl.pallas_call(body, name="ada_mod", out_shape=jax.ShapeDtypeStruct((8, n), F32),
                          compiler_params=_cp())(c_all, w_ada_sh, b_ada_sh)


def ada_grad(c_all, dmod_sh):
    def body(c_ref, d_ref, o_ref):
        cv = c_ref[...]
        s = cv * _sig(cv)
        o_ref[...] = lax.dot_general(s, d_ref[...], (((0,), (0,)), ((), ())), preferred_element_type=F32,
                                     precision=lax.Precision.HIGHEST)

    n = dmod_sh.shape[1]
    return pl.pallas_call(body, name="ada_grad", out_shape=jax.ShapeDtypeStruct((D, n), F32),
                          compiler_params=_cp())(c_all, dmod_sh)


def norm_proj(x, norm_w, scale, shift, wcat):
    t = x.shape[0]
    tm = min(t, 1024)

    def body(x_ref, nw_ref, sc_ref, sh_ref, w_ref, p_ref, dt_ref, ht_ref, hs):
        @pl.when(pl.program_id(1) == 0)
        def _():
            xv = x_ref[...]
            r = lax.rsqrt(jnp.mean(xv * xv, axis=-1, keepdims=True) + EPS)
            h = (xv * r) * nw_ref[...]
            h = h * (1.0 + sc_ref[...]) + sh_ref[...]
            hs[...] = h.astype(BF16)
            ht_ref[...] = h.T.astype(BF16)

        p = _dot(hs[...], w_ref[...])
        p_ref[...] = p.astype(BF16)

        @pl.when(pl.program_id(1) == C_DT // TN)
        def _():
            dt_ref[...] = p[:, C_DT % TN:C_DT % TN + 128]

    vec = pl.BlockSpec((1, D), lambda i, j: (0, 0))
    return pl.pallas_call(
        body, name="norm_proj", grid=(t // tm, NP // TN),
        in_specs=[pl.BlockSpec((tm, D), lambda i, j: (i, 0)), vec, vec, vec,
                  pl.BlockSpec((D, TN), lambda i, j: (0, j))],
        out_specs=[pl.BlockSpec((tm, TN), lambda i, j: (i, j)), pl.BlockSpec((tm, 128), lambda i, j: (i, 0)),
                   pl.BlockSpec((D, tm), lambda i, j: (0, i))],
        out_shape=[jax.ShapeDtypeStruct((t, NP), BF16), jax.ShapeDtypeStruct((t, 128), F32),
                   jax.ShapeDtypeStruct((D, t), BF16)],
        scratch_shapes=[pltpu.VMEM((tm, D), BF16)],
        compiler_params=_cp(("parallel", "arbitrary")),
    )(x, norm_w, scale, shift, wcat)


def _bucket_table():
    qi = np.arange(BLK)[:, None]
    kj = np.arange(2 * BLK)[None, :]
    dist = qi + BLK - kj
    n = np.maximum(dist, 0)
    max_exact = NBUCKET // 2
    nf = np.maximum(n, 1).astype(np.float32)
    large = max_exact + (np.log(nf / np.float32(max_exact)) / np.float32(math.log(MAXDIST / max_exact))
                         * np.float32(NBUCKET - max_exact)).astype(np.int32)
    large = np.minimum(large, NBUCKET - 1)
    bucket = np.where(n < max_exact, n, large).astype(np.int32)
    valid = (dist >= 0) & (dist < BLK)
    return np.where(valid, bucket, -1).astype(np.int32)


def bias_expand(rel_bias, sinks, bucket):
    def body(rb_ref, sk_ref, bk_ref, o_ref):
        bk = bk_ref[...]
        col = lax.broadcasted_iota(jnp.int32, (BLK, 2 * BLK), 1)

        def head(hd, carry):
            def step(b, acc):
                return jnp.where(bk == b, rb_ref[b, hd], acc)

            acc = lax.fori_loop(0, NBUCKET, step, jnp.full((BLK, 2 * BLK), NEG, F32))
            acc = jnp.where(col == 0, sk_ref[0, hd], acc)
            o_ref[1, hd] = acc
            o_ref[0, hd] = jnp.where(jnp.logical_and(col > 0, col < BLK), NEG, acc)
            return carry

        lax.fori_loop(0, HQ, head, 0)

    smem = pl.BlockSpec(memory_space=pltpu.SMEM)
    return pl.pallas_call(
        body, name="bias_expand", in_specs=[smem, smem, VM], out_specs=VM,
        out_shape=jax.ShapeDtypeStruct((2, HQ, BLK, 2 * BLK), F32), compiler_params=_cp(),
    )(rel_bias, sinks, jnp.asarray(bucket))


def bias_reduce(dacc, bucket):
    col = np.arange(BLK * 2 * BLK) % (2 * BLK)
    lane = np.arange(128)[None, :]
    member = (bucket.reshape(-1)[:, None] == lane) | ((col[:, None] == 0) & (lane == NBUCKET))

    def body(d_ref, m_ref, o_ref):
        mm = m_ref[...]
        o_ref[...] = sum(_dot(part, mm) for part in _split3(d_ref[...]))

    return pl.pallas_call(body, name="bias_reduce", out_shape=jax.ShapeDtypeStruct((HQ, 128), F32),
                          compiler_params=_cp())(dacc.reshape(HQ, BLK * 2 * BLK), jnp.asarray(member, BF16))


GQ = GRP * BLK


def _stack_heads(x, nh):
    return jnp.concatenate([x[:, DH * h:DH * (h + 1)] for h in range(nh)], axis=0)


def _unstack(xs, nh):
    rows = xs.shape[0] // nh
    return jnp.concatenate([xs[rows * h:rows * (h + 1)] for h in range(nh)], axis=1)


def _rms(x):
    return lax.rsqrt(jnp.mean(x * x, axis=-1, keepdims=True) + EPS)


def _stack_q(q, qw):
    qs = _stack_heads(q, HQ)
    r = _rms(qs)
    qhat = qs * r
    return qhat * qw, qhat, r


def _band_first(shape):
    return (lax.broadcasted_iota(jnp.int32, shape, 0) & (2 * BLK - 1)) == 0


def _stack_kv(kp, kc, vp, vc, kw):
    ks = _stack_heads(jnp.concatenate([kp, kc], axis=0), HKV)
    r = _rms(ks)
    khat = ks * r
    first = _band_first(ks.shape)
    kn = jnp.where(first, 0.0, khat * kw)
    v2 = jnp.where(first, 0.0, _stack_heads(jnp.concatenate([vp, vc], axis=0), HKV)).astype(BF16)
    return kn, khat, r, v2


def _softmax_rows(s):
    p = jnp.exp(s - jnp.max(s, axis=-1, keepdims=True))
    return p * (1.0 / _rsum(p))


def attn_fwd(proj, biasm, q_norm_w, k_norm_w):
    t = proj.shape[0]
    nb = t // BLK

    def body(q_ref, kc_ref, kp_ref, vc_ref, vp_ref, bm_ref, qw_ref, kw_ref, o_ref):
        f = lambda ref: ref[...].astype(F32)
        qn = _stack_q(f(q_ref), qw_ref[...])[0].astype(BF16)
        kn, _, _, v2 = _stack_kv(f(kp_ref), f(kc_ref), f(vp_ref), f(vc_ref), kw_ref[...])
        knb = kn.astype(BF16)
        s = jnp.concatenate([_dot_nt(qn[GQ * j:GQ * (j + 1)], knb[2 * BLK * j:2 * BLK * (j + 1)])
                             for j in range(HKV)], axis=0)
        pr = _softmax_rows(s * SCALE + bm_ref[0].reshape(HQ * BLK, 2 * BLK)).astype(BF16)
        o = jnp.concatenate([_dot(pr[GQ * j:GQ * (j + 1)], v2[2 * BLK * j:2 * BLK * (j + 1)])
                             for j in range(HKV)], axis=0)
        o_ref[...] = _unstack(o, HQ).astype(BF16)

    kblk, vblk = C_K // 256, C_V // 256
    prev = lambda n: jnp.maximum(n - 1, 0)
    return pl.pallas_call(
        body, name="attn_fwd", grid=(nb,),
        in_specs=[pl.BlockSpec((BLK, D), lambda n: (n, 0)),
                  pl.BlockSpec((BLK, 256), lambda n: (n, kblk)),
                  pl.BlockSpec((BLK, 256), lambda n: (prev(n), kblk)),
                  pl.BlockSpec((BLK, 256), lambda n: (n, vblk)),
                  pl.BlockSpec((BLK, 256), lambda n: (prev(n), vblk)),
                  pl.BlockSpec((1, HQ, BLK, 2 * BLK), lambda n: (jnp.minimum(n, 1), 0, 0, 0)),
                  _full((1, DH)), _full((1, DH))],
        out_specs=pl.BlockSpec((BLK, D), lambda n: (n, 0)),
        out_shape=jax.ShapeDtypeStruct((t, D), BF16),
        compiler_params=_cp(("parallel",)),
    )(proj, proj, proj, proj, proj, biasm, q_norm_w, k_norm_w)


def attn_bwd(proj, dao, biasm, q_norm_w, k_norm_w):
    t = proj.shape[0]
    nb = t // BLK
    kb = 2 * BLK

    def body(q_ref, kc_ref, kp_ref, vc_ref, vp_ref, do_ref, bm_ref, qw_ref, kw_ref,
             dq_ref, dkv_ref, dqw_ref, dkw_ref, dacc_ref, ck, cv, pk, pv, nk, nv):
        n = pl.program_id(0)

        @pl.when(n == 0)
        def _():
            for ref in (dqw_ref, dkw_ref, dacc_ref, ck, cv):
                ref[...] = jnp.zeros_like(ref)

        qw = qw_ref[...]
        kw = kw_ref[...]
        f = lambda ref: ref[...].astype(F32)
        kn, khat, rk, v2 = _stack_kv(f(kp_ref), f(kc_ref), f(vp_ref), f(vc_ref), kw)
        grp = lambda a, j: a[GQ * j:GQ * (j + 1)]
        band = lambda a, j: a[kb * j:kb * (j + 1)]

        @pl.when(n < nb)
        def _():
            qn, qhat, rq = _stack_q(f(q_ref), qw)
            qnb = qn.astype(BF16)
            knb = kn.astype(BF16)
            dos = _stack_heads(f(do_ref), HQ).astype(BF16)
            s = jnp.concatenate([_dot_nt(grp(qnb, j), band(knb, j)) for j in range(HKV)], axis=0)
            pr = _softmax_rows(s * SCALE + bm_ref[0].reshape(HQ * BLK, kb))
            dp = jnp.concatenate([_dot_nt(grp(dos, j), band(v2, j)) for j in range(HKV)], axis=0)
            ds = pr * (dp - _rsum(pr * dp))
            dacc_ref[...] += ds.reshape(HQ, BLK, kb)
            dsb = ds.astype(BF16)
            prb = pr.astype(BF16)
            dqn = jnp.concatenate([_dot(grp(dsb, j), band(knb, j)) for j in range(HKV)], axis=0) * SCALE
            dqhat = dqn * qw
            dq = rq * (dqhat - qhat * jnp.mean(dqhat * qhat, axis=-1, keepdims=True))
            dq_ref[...] = _unstack(dq, HQ).astype(BF16)
            dqw_ref[...] += _csum(dqn * qhat)
            first = _band_first((kb, DH))
            for j in range(HKV):
                rows = slice(BLK * j, BLK * (j + 1))
                dkn = jnp.where(first, 0.0, _dot_tn(grp(dsb, j), grp(qnb, j)) * SCALE)
                dvj = jnp.where(first, 0.0, _dot_tn(grp(prb, j), grp(dos, j)))
                pk[rows, :] = dkn[:BLK]
                nk[rows, :] = dkn[BLK:]
                pv[rows, :] = dvj[:BLK]
                nv[rows, :] = dvj[BLK:]

        @pl.when(n == nb)
        def _():
            for ref in (pk, pv, nk, nv):
                ref[...] = jnp.zeros_like(ref)

        khp = jnp.concatenate([khat[kb * j:kb * j + BLK] for j in range(HKV)], axis=0)
        rkp = jnp.concatenate([rk[kb * j:kb * j + BLK] for j in range(HKV)], axis=0)
        dkn = ck[...] + pk[...]
        dkhat = dkn * kw
        dk = rkp * (dkhat - khp * jnp.mean(dkhat * khp, axis=-1, keepdims=True))
        dkw_ref[...] += _csum(dkn * khp)
        dkv_ref[...] = jnp.concatenate([_unstack(dk, HKV), _unstack(cv[...] + pv[...], HKV)], axis=1).astype(BF16)
        ck[...] = nk[...]
        cv[...] = nv[...]

    kblk, vblk = C_K // 256, C_V // 256
    cur = lambda n: jnp.minimum(n, nb - 1)
    prev = lambda n: jnp.maximum(n - 1, 0)
    carry = pltpu.VMEM((HKV * BLK, DH), F32)
    return pl.pallas_call(
        body, name="attn_bwd", grid=(nb + 1,),
        in_specs=[pl.BlockSpec((BLK, D), lambda n: (cur(n), 0)),
                  pl.BlockSpec((BLK, 256), lambda n: (cur(n), kblk)), pl.BlockSpec((BLK, 256), lambda n: (prev(n), kblk)),
                  pl.BlockSpec((BLK, 256), lambda n: (cur(n), vblk)), pl.BlockSpec((BLK, 256), lambda n: (prev(n), vblk)),
                  pl.BlockSpec((BLK, D), lambda n: (cur(n), 0)),
                  pl.BlockSpec((1, HQ, BLK, kb), lambda n: (jnp.minimum(n, 1), 0, 0, 0)),
                  _full((1, DH)), _full((1, DH))],
        out_specs=[pl.BlockSpec((BLK, D), lambda n: (cur(n), 0)),
                   pl.BlockSpec((BLK, 512), lambda n: (prev(n), 0)),
                   _full((1, DH)), _full((1, DH)), _full((HQ, BLK, kb))],
        out_shape=[jax.ShapeDtypeStruct((t, D), BF16), jax.ShapeDtypeStruct((t, 512), BF16),
                   jax.ShapeDtypeStruct((1, DH), F32),
                   jax.ShapeDtypeStruct((1, DH), F32), jax.ShapeDtypeStruct((HQ, BLK, kb), F32)],
        scratch_shapes=[carry] * 6,
        compiler_params=_cp(("arbitrary",)),
    )(proj, proj, proj, proj, proj, dao, biasm, q_norm_w, k_norm_w)


CONV_TM, CONV_CW, CONV_RC, HALO = 2048, 1024, 32, 16


def conv_fwd(proj, conv_w, conv_b):
    t = proj.shape[0]
    tm = min(t, CONV_TM)
    c0 = C_XBC // CONV_CW

    def body(x_ref, xp_ref, w_ref, b_ref, o_ref, ds_ref):
        i = pl.program_id(1)
        w = w_ref[...]
        b = b_ref[...]
        for r in range(tm // CONV_RC):
            lo = r * CONV_RC
            if r == 0:
                head = jnp.where(i == 0, 0.0, xp_ref[...].astype(F32))
                win = jnp.concatenate([head, x_ref[0:CONV_RC, :].astype(F32)], axis=0)
            else:
                win = x_ref[lo - HALO:lo + CONV_RC, :].astype(F32)
            acc = b
            for j in range(CONV_K):
                acc = acc + w[j:j + 1] * win[HALO - 3 + j:HALO - 3 + j + CONV_RC]
            sg = _sig(acc)
            o_ref[lo:lo + CONV_RC, :] = acc * sg
            ds_ref[lo:lo + CONV_RC, :] = _dsilu(acc, sg).astype(BF16)

    rh = tm // HALO
    tile = pl.BlockSpec((tm, CONV_CW), lambda s, i: (i, s))
    return pl.pallas_call(
        body, name="conv_fwd", grid=(XBC // CONV_CW, t // tm),
        in_specs=[pl.BlockSpec((tm, CONV_CW), lambda s, i: (i, c0 + s)),
                  pl.BlockSpec((HALO, CONV_CW), lambda s, i: (jnp.maximum(i * rh - 1, 0), c0 + s)),
                  pl.BlockSpec((CONV_K, CONV_CW), lambda s, i: (0, s)), pl.BlockSpec((1, CONV_CW), lambda s, i: (0, s))],
        out_specs=[tile, tile],
        out_shape=[jax.ShapeDtypeStruct((t, XBC), F32), jax.ShapeDtypeStruct((t, XBC), BF16)],
        compiler_params=_cp(("parallel", "parallel")),
    )(proj, proj, conv_w, conv_b)


def conv_bwd(proj, dact, dsl, conv_w):
    t = proj.shape[0]
    tm = min(t, CONV_TM)
    nt = t // tm
    nr = tm // CONV_RC
    c0 = C_XBC // CONV_CW
    ext = CONV_RC + 8

    def body(x_ref, xp_ref, d_ref, dn_ref, s_ref, sn_ref, w_ref, dx_ref, dw_ref, db_ref):
        i = pl.program_id(1)

        @pl.when(i == 0)
        def _():
            dw_ref[...] = jnp.zeros_like(dw_ref)
            db_ref[...] = jnp.zeros_like(db_ref)

        w = w_ref[...]
        dws = [jnp.zeros((1, CONV_CW), F32) for _ in range(CONV_K)]
        db = jnp.zeros((1, CONV_CW), F32)
        for r in range(nr):
            lo = r * CONV_RC
            if r == 0:
                head = jnp.where(i == 0, 0.0, xp_ref[...].astype(F32))
                win = jnp.concatenate([head, x_ref[0:CONV_RC, :].astype(F32)], axis=0)
            else:
                win = x_ref[lo - HALO:lo + CONV_RC, :].astype(F32)
            if r < nr - 1:
                dext = d_ref[lo:lo + ext, :]
                sext = s_ref[lo:lo + CONV_RC + HALO, :].astype(F32)[0:ext]
            else:
                dext = jnp.concatenate([d_ref[lo:lo + CONV_RC, :], jnp.where(i == nt - 1, 0.0, dn_ref[...])], axis=0)
                sext = jnp.concatenate([s_ref[lo:lo + CONV_RC, :].astype(F32), sn_ref[...].astype(F32)], axis=0)[0:ext]
            dpre = dext * sext
            dx = jnp.zeros((CONV_RC, CONV_CW), F32)
            own = dpre[0:CONV_RC]
            for j in range(CONV_K):
                dx = dx + w[j:j + 1] * dpre[3 - j:3 - j + CONV_RC]
                dws[j] = dws[j] + _csum(own * win[HALO - 3 + j:HALO - 3 + j + CONV_RC])
            db = db + _csum(own)
            dx_ref[lo:lo + CONV_RC, :] = dx.astype(BF16)
        dw_ref[...] += jnp.concatenate(dws, axis=0)
        db_ref[...] += db

    rh = tm // HALO
    r8 = tm // 8
    nxt = lambda i, per: jnp.minimum((i + 1) * per, nt * per - 1)
    return pl.pallas_call(
        body, name="conv_bwd", grid=(XBC // CONV_CW, nt),
        in_specs=[pl.BlockSpec((tm, CONV_CW), lambda s, i: (i, c0 + s)),
                  pl.BlockSpec((HALO, CONV_CW), lambda s, i: (jnp.maximum(i * rh - 1, 0), c0 + s)),
                  pl.BlockSpec((tm, CONV_CW), lambda s, i: (i, s)),
                  pl.BlockSpec((8, CONV_CW), lambda s, i: (nxt(i, r8), s)),
                  pl.BlockSpec((tm, CONV_CW), lambda s, i: (i, s)),
                  pl.BlockSpec((HALO, CONV_CW), lambda s, i: (nxt(i, rh), s)),
                  pl.BlockSpec((CONV_K, CONV_CW), lambda s, i: (0, s))],
        out_specs=[pl.BlockSpec((tm, CONV_CW), lambda s, i: (i, s)),
                   pl.BlockSpec((CONV_K, CONV_CW), lambda s, i: (0, s)), pl.BlockSpec((1, CONV_CW), lambda s, i: (0, s))],
        out_shape=[jax.ShapeDtypeStruct((t, XBC), BF16), jax.ShapeDtypeStruct((CONV_K, XBC), F32),
                   jax.ShapeDtypeStruct((1, XBC), F32)],
        compiler_params=_cp(("parallel", "arbitrary")),
    )(proj, proj, dact, dact, dsl, dsl, conv_w)


def _split3(x):
    h = x.astype(BF16)
    r = x - h.astype(F32)
    m = r.astype(BF16)
    lo = (r - m.astype(F32)).astype(BF16)
    return h, m, lo


def _tri_mm(tri, x):
    h, m, lo = _split3(x)
    return _dot(tri, h) + _dot(tri, m) + _dot(tri, lo)


def _softplus(x):
    return jnp.maximum(x, 0.0) + jnp.log1p(jnp.exp(-jnp.abs(x)))


def _chunk_decays(dt_raw, dtb, alog):
    dtv = _softplus(dt_raw + dtb)
    a = -jnp.exp(alog)
    ri = lax.broadcasted_iota(jnp.int32, (BLK, BLK), 0)
    ci = lax.broadcasted_iota(jnp.int32, (BLK, BLK), 1)
    causal = ri >= ci
    acum = _tri_mm(causal.astype(BF16), dtv * a)
    return dtv, a, causal, acum, acum.T


NPAIR = SH // 2


def _pairs(x):
    return jnp.stack([x[:, 128 * k:128 * (k + 1)] for k in range(NPAIR)])


def _unpairs(x3):
    return jnp.concatenate([x3[k] for k in range(NPAIR)], axis=1)


def _per_head_cols(m):
    return jnp.stack([jnp.broadcast_to(m[:, h:h + 1], m.shape) for h in range(SH)])


def _pair_lanes(t):
    r = t.reshape(NPAIR, 2, t.shape[1], 128)
    lo = lax.broadcasted_iota(jnp.int32, (1, t.shape[1], 128), 2) < SP
    return jnp.where(lo, r[:, 0], r[:, 1])


class _Chunk:
    pass


def _chunk_common(dt_raw, dtb, alog, dskip):
    cm = _Chunk()
    cm.dtv, cm.a, cm.causal, acum, acum_t = _chunk_decays(dt_raw, dtb, alog)
    cm.acol = _per_head_cols(acum)
    cm.arow = jnp.stack([acum_t[h:h + 1, :] for h in range(SH)])
    apl = _pair_lanes(cm.acol)
    alast = apl[:, BLK - 1:BLK, :]
    cm.dpl = _pair_lanes(_per_head_cols(cm.dtv))
    cm.eapl = jnp.exp(apl)
    cm.epl = jnp.exp(alast - apl)
    cm.cdpl = jnp.exp(alast)
    cm.dskpl = _pair_lanes(_per_head_cols(dskip))
    cm.lo = lax.broadcasted_iota(jnp.int32, (1, BLK, 128), 2) < SP
    return cm


def ssd_fwd(act, dt_raw, dtb_p, alog_p, dsk_p):
    t = act.shape[0]
    nc = t // BLK

    def body(xs_ref, b_ref, c_ref, dt_ref, dtb_ref, al_ref, dk_ref, y_ref, sp_ref, st):
        c = pl.program_id(0)

        @pl.when(c == 0)
        def _():
            st[...] = jnp.zeros_like(st)

        s_t = st[...]
        sp_ref[0] = s_t
        cm = _chunk_common(dt_ref[...], dtb_ref[...], al_ref[...], dk_ref[...])
        gms, cbs, bts = [], [], []
        for g in range(SG):
            bf = b_ref[:, SN * g:SN * (g + 1)]
            cb = c_ref[:, SN * g:SN * (g + 1)].astype(BF16)
            gms.append(_dot_nt(cb, bf.astype(BF16)))
            cbs.append(cb)
            bts.append(bf.T.astype(BF16))
        lam = jnp.exp(jnp.where(cm.causal[None], cm.acol - cm.arow, NEG))
        m = (lam.reshape(SG, SR, BLK, BLK) * jnp.stack(gms)[:, None]).reshape(SH, BLK, BLK).astype(BF16)
        xs16 = _pairs(xs_ref[...])
        xdt16 = xs16 * cm.dpl
        x_lo = jnp.where(cm.lo, xdt16, 0.0).astype(BF16)
        x_hi = jnp.where(cm.lo, 0.0, xdt16).astype(BF16)
        s16 = _pairs(s_t)
        s16b = s16.astype(BF16)
        yd = jnp.stack([_dot(m[2 * k], x_lo[k]) + _dot(m[2 * k + 1], x_hi[k]) for k in range(NPAIR)])
        yo = jnp.stack([_dot(cbs[k // (NPAIR // SG)], s16b[k]) for k in range(NPAIR)])
        y_ref[...] = _unpairs(yd + yo * cm.eapl + cm.dskpl * xs16).astype(BF16)
        xe = (xdt16 * cm.epl).astype(BF16)
        st[...] = _unpairs(cm.cdpl * s16 + jnp.stack([_dot(bts[k // (NPAIR // SG)], xe[k]) for k in range(NPAIR)]))

    vec = _full((1, 128))
    return pl.pallas_call(
        body, name="ssd_fwd", grid=(nc,),
        in_specs=[pl.BlockSpec((BLK, SSM_W), lambda c: (c, 0)),
                  pl.BlockSpec((BLK, SG * SN), lambda c: (c, SSM_W // (SG * SN))),
                  pl.BlockSpec((BLK, SG * SN), lambda c: (c, SSM_W // (SG * SN) + 1)),
                  pl.BlockSpec((BLK, 128), lambda c: (c, 0)), vec, vec, vec],
        out_specs=[pl.BlockSpec((BLK, SSM_W), lambda c: (c, 0)), pl.BlockSpec((1, SN, SSM_W), lambda c: (c, 0, 0))],
        out_shape=[jax.ShapeDtypeStruct((t, SSM_W), BF16), jax.ShapeDtypeStruct((nc, SN, SSM_W), F32)],
        scratch_shapes=[pltpu.VMEM((SN, SSM_W), F32)],
        compiler_params=_cp(("arbitrary",)),
    )(act, act, act, dt_raw, dtb_p, alog_p, dsk_p)


def _head_sums(q):
    r = q.shape[1]
    lo = lax.broadcasted_iota(jnp.int32, (1, r, 128), 2) < SP
    s_lo = jnp.sum(jnp.where(lo, q, 0.0), axis=-1, keepdims=True)
    s_hi = jnp.sum(jnp.where(lo, 0.0, q), axis=-1, keepdims=True)
    lane = lax.broadcasted_iota(jnp.int32, (r, 128), 1)
    out = jnp.zeros((r, 128), F32)
    for k in range(NPAIR):
        out = jnp.where(lane == 2 * k, s_lo[k], jnp.where(lane == 2 * k + 1, s_hi[k], out))
    return out


def ssd_bwd(act, dt_raw, dy, sprev, dtb_p, alog_p, dsk_p):
    t = act.shape[0]
    nc = t // BLK

    def body(xs_ref, b_ref, c_ref, dt_ref, dy_ref, sp_ref, dtb_ref, al_ref, dk_ref,
             da_ref, ddt_ref, ddtb_ref, dal_ref, ddk_ref, dst):
        i = pl.program_id(0)

        @pl.when(i == 0)
        def _():
            dst[...] = jnp.zeros_like(dst)
            ddtb_ref[...] = jnp.zeros_like(ddtb_ref)
            dal_ref[...] = jnp.zeros_like(dal_ref)
            ddk_ref[...] = jnp.zeros_like(ddk_ref)

        dt_raw = dt_ref[...]
        dtb = dtb_ref[...]
        cm = _chunk_common(dt_raw, dtb, al_ref[...], dk_ref[...])
        ri = lax.broadcasted_iota(jnp.int32, (BLK, BLK), 0)
        ci = lax.broadcasted_iota(jnp.int32, (BLK, BLK), 1)
        lam_t = jnp.exp(jnp.where((ri <= ci)[None], cm.arow - cm.acol, NEG))
        bbs, cbs, cts, gms = [], [], [], []
        for g in range(SG):
            bf = b_ref[:, SN * g:SN * (g + 1)]
            cf = c_ref[:, SN * g:SN * (g + 1)]
            bbs.append(bf.astype(BF16))
            cbs.append(cf.astype(BF16))
            cts.append(cf.T.astype(BF16))
            gms.append(_dot_nt(bbs[g], cbs[g]))
        grp = lambda k: k // (NPAIR // SG)
        xs16 = _pairs(xs_ref[...])
        dy16 = _pairs(dy_ref[...].astype(F32))
        sp16 = _pairs(sp_ref[0])
        ds16 = _pairs(dst[...])
        xdt16 = xs16 * cm.dpl
        xdtb = xdt16.astype(BF16)
        dyh = [jnp.where(cm.lo, dy16, 0.0).astype(BF16), jnp.where(cm.lo, 0.0, dy16).astype(BF16)]
        m_t = (lam_t.reshape(SG, SR, BLK, BLK) * jnp.stack(gms)[:, None]).reshape(SH, BLK, BLK).astype(BF16)
        dxdt = jnp.stack([_dot(m_t[2 * k], dyh[0][k]) + _dot(m_t[2 * k + 1], dyh[1][k]) for k in range(NPAIR)])
        dm_t = jnp.stack([_dot_nt(xdtb[h // 2], dyh[h % 2][h // 2]) for h in range(SH)])
        dg_t = jnp.sum((dm_t * lam_t).reshape(SG, SR, BLK, BLK), axis=1).astype(BF16)
        xq16 = xdtb.astype(F32)
        xh = [jnp.where(cm.lo, xdt16, 0.0).astype(BF16), jnp.where(cm.lo, 0.0, xdt16).astype(BF16)]
        y_in = jnp.stack([_dot_tn(m_t[2 * k], xh[0][k]) + _dot_tn(m_t[2 * k + 1], xh[1][k]) for k in range(NPAIR)])
        da_diag = dy16 * y_in - xq16 * dxdt
        lane_c = lax.broadcasted_iota(jnp.int32, (BLK, 128), 1)
        ds16b = ds16.astype(BF16)
        sp16b = sp16.astype(BF16)
        dxs = jnp.stack([_dot(bbs[grp(k)], ds16b[k]) for k in range(NPAIR)]) * cm.epl
        dxdt = dxdt + dxs
        dya = (dy16 * cm.eapl).astype(BF16)
        xe = (xdt16 * cm.epl).astype(BF16)
        dcs, dbs = [], []
        for g in range(SG):
            ks = range(g * (NPAIR // SG), (g + 1) * (NPAIR // SG))
            dcs.append(sum(_dot_nt(dya[k], sp16b[k]) for k in ks) + _dot_tn(dg_t[g], bbs[g]))
            dbs.append(sum(_dot_nt(xe[k], ds16b[k]) for k in ks) + _dot(dg_t[g], cbs[g]))
        dst[...] = _unpairs(cm.cdpl * ds16 + jnp.stack([_dot(cts[grp(k)], dya[k]) for k in range(NPAIR)]))
        da_ref[...] = jnp.concatenate([_unpairs(dxdt * cm.dpl + cm.dskpl * dy16)] + dbs + dcs, axis=1)
        y_off = jnp.stack([_dot(cbs[grp(k)], sp16b[k]) for k in range(NPAIR)]) * cm.eapl
        da_cols = _head_sums(da_diag + dy16 * y_off - xdt16 * dxs)
        last = _head_sums(jnp.sum(xdt16 * dxs, axis=1, keepdims=True)
                          + cm.cdpl * jnp.sum(ds16 * sp16, axis=1, keepdims=True))
        ddt = _head_sums(dxdt * xs16)
        row_i = lax.broadcasted_iota(jnp.int32, (BLK, 128), 0)
        dacum = da_cols + jnp.where(row_i == BLK - 1, last, 0.0)
        dda = _tri_mm((ri <= ci).astype(BF16), dacum)
        ddt = ddt + dda * cm.a
        dal_ref[...] += _csum(dda * cm.dtv) * cm.a
        ddt_raw = jnp.where(lane_c < SH, ddt * _sig(dt_raw + dtb), 0.0)
        ddt_ref[...] = ddt_raw.astype(BF16)
        ddtb_ref[...] += _csum(ddt_raw)
        ddk_ref[...] += _head_sums(jnp.sum(dy16 * xs16, axis=1, keepdims=True))

    rev = lambda i: nc - 1 - i
    vec = _full((1, 128))
    slab = pl.BlockSpec((BLK, SSM_W), lambda i: (rev(i), 0))
    return pl.pallas_call(
        body, name="ssd_bwd", grid=(nc,),
        in_specs=[slab,
                  pl.BlockSpec((BLK, SG * SN), lambda i: (rev(i), SSM_W // (SG * SN))),
                  pl.BlockSpec((BLK, SG * SN), lambda i: (rev(i), SSM_W // (SG * SN) + 1)),
                  pl.BlockSpec((BLK, 128), lambda i: (rev(i), 0)),
                  slab,
                  pl.BlockSpec((1, SN, SSM_W), lambda i: (rev(i), 0, 0)), vec, vec, vec],
        out_specs=[pl.BlockSpec((BLK, XBC), lambda i: (rev(i), 0)), pl.BlockSpec((BLK, 128), lambda i: (rev(i), 0)),
                   vec, vec, vec],
        out_shape=[jax.ShapeDtypeStruct((t, XBC), F32), jax.ShapeDtypeStruct((t, 128), BF16),
                   jax.ShapeDtypeStruct((1, 128), F32), jax.ShapeDtypeStruct((1, 128), F32),
                   jax.ShapeDtypeStruct((1, 128), F32)],
        scratch_shapes=[pltpu.VMEM((SN, SSM_W), F32)],
        compiler_params=_cp(("arbitrary",)),
    )(act, act, act, dt_raw, dy, sprev, dtb_p, alog_p, dsk_p)


TAIL_TM = 256


def _dsilu(z, s):
    return s * (1.0 + z * (1.0 - s))


def tail(proj, ao, yss, x, target, gate, ssm_nw, w_at, w_ss, w_ou):
    t = x.shape[0]
    tm = min(t, TAIL_TM)
    gw = SSM_W // SG

    def body(ao_ref, za_ref, ga_ref, gb_ref, zm_ref, ys_ref, x_ref, tg_ref, gt_ref, nw_ref, wa_ref, ws_ref, wo_ref,
             loss_ref, dy_ref, dao_ref, dmid_ref, dys_ref,
             ua_ref, yn_ref, mg_ref, dya_ref, dyb_ref, do_ref, dgt_ref, dnw_ref):
        i = pl.program_id(0)

        @pl.when(i == 0)
        def _():
            loss_ref[...] = jnp.zeros_like(loss_ref)
            dgt_ref[...] = jnp.zeros_like(dgt_ref)
            dnw_ref[...] = jnp.zeros_like(dnw_ref)

        ao = ao_ref[...].astype(F32)
        za = za_ref[...].astype(F32)
        sa = _sig(za)
        sila = za * sa
        ua_f = ao * sila
        ua = ua_f.astype(BF16)
        ya = _dot(ua, wa_ref[...])
        zm = zm_ref[...].astype(F32)
        sm = _sig(zm)
        silm = zm * sm
        ys = ys_ref[...].astype(F32)
        u = ys * silm
        nw = nw_ref[...]
        rs, uns = [], []
        for g in range(SG):
            ug = u[:, gw * g:gw * (g + 1)]
            r = lax.rsqrt(jnp.mean(ug * ug, axis=-1, keepdims=True) + EPS)
            rs.append(r)
            uns.append(ug * r)
        un = jnp.concatenate(uns, axis=1)
        yn_f = un * nw
        yn = yn_f.astype(BF16)
        yb = _dot(yn, ws_ref[...])
        sga = _sig(ga_ref[...].astype(F32))
        sgb = _sig(gb_ref[...].astype(F32))
        mg_f = sga * ya + sgb * yb
        mg = mg_f.astype(BF16)
        o = _dot(mg, wo_ref[...])
        gt = gt_ref[...]
        err = (x_ref[...] + gt * o) - tg_ref[...]
        lane = lax.broadcasted_iota(jnp.int32, (1, 128), 1)
        loss_ref[...] += jnp.where(lane == 0, 0.5 * _asum(_rsum(err * err) / D), 0.0)
        dy = err * (1.0 / D)
        dy_ref[...] = dy
        dgt_ref[...] += _csum(dy * o)
        do = (dy * gt).astype(BF16)
        dmg = _dot_nt(do, wo_ref[...])
        dmid_ref[:, C_GA - C_ZA:C_GB - C_ZA] = (dmg * ya * sga * (1.0 - sga)).astype(BF16)
        dmid_ref[:, C_GB - C_ZA:C_ZM - C_ZA] = (dmg * yb * sgb * (1.0 - sgb)).astype(BF16)
        dya = (dmg * sga).astype(BF16)
        dyb = (dmg * sgb).astype(BF16)
        dua = _dot_nt(dya, wa_ref[...])
        dao_ref[...] = (dua * sila).astype(BF16)
        dmid_ref[:, 0:C_GA - C_ZA] = (dua * ao * _dsilu(za, sa)).astype(BF16)
        dyn = _dot_nt(dyb, ws_ref[...])
        dnw_ref[...] += _csum(dyn * un)
        dun = dyn * nw
        dus = []
        for g in range(SG):
            gs = slice(gw * g, gw * (g + 1))
            dus.append(rs[g] * (dun[:, gs] - uns[g] * jnp.mean(dun[:, gs] * uns[g], axis=-1, keepdims=True)))
        du = jnp.concatenate(dus, axis=1)
        dys_ref[...] = (du * silm).astype(BF16)
        dmid_ref[:, C_ZM - C_ZA:] = (du * ys * _dsilu(zm, sm)).astype(BF16)
        ua_ref[...] = ua_f.T.astype(BF16)
        yn_ref[...] = yn_f.T.astype(BF16)
        mg_ref[...] = mg_f.T.astype(BF16)
        dya_ref[...] = dya
        dyb_ref[...] = dyb
        do_ref[...] = do

    row = lambda w: pl.BlockSpec((tm, w), lambda i: (i, 0))
    pcol = lambda w, c0: pl.BlockSpec((tm, w), lambda i: (i, c0 // w))
    sd = lambda w, dt: jax.ShapeDtypeStruct((t, w), dt)
    colt = lambda w: pl.BlockSpec((w, tm), lambda i: (0, i))
    sdt = lambda w: jax.ShapeDtypeStruct((w, t), BF16)
    return pl.pallas_call(
        body, name="tail", grid=(t // tm,),
        in_specs=[row(D), pcol(D, C_ZA), pcol(D, C_GA), pcol(D, C_GB), pcol(SSM_W, C_ZM), row(SSM_W), row(D), row(D),
                  _full((1, D)), _full((1, SSM_W)), _full((D, D)), _full((SSM_W, D)), _full((D, D))],
        out_specs=[_full((1, 128)), row(D), row(D), row(W_MID), row(SSM_W),
                   colt(D), colt(SSM_W), colt(D), row(D), row(D), row(D), _full((1, D)), _full((1, SSM_W))],
        out_shape=[jax.ShapeDtypeStruct((1, 128), F32), sd(D, F32), sd(D, BF16), sd(W_MID, BF16),
                   sd(SSM_W, BF16), sdt(D), sdt(SSM_W), sdt(D), sd(D, BF16),
                   sd(D, BF16), sd(D, BF16), jax.ShapeDtypeStruct((1, D), F32), jax.ShapeDtypeStruct((1, SSM_W), F32)],
        compiler_params=_cp(("arbitrary",)),
    )(ao, proj, proj, proj, proj, yss, x, target, gate, ssm_nw, w_at, w_ss, w_ou)


DPIECES = ((D, ((D, C_Q),)),
           (W_MID, ((D, C_ZA), (D, C_GA), (D, C_GB), (SSM_W, C_ZM))),
           (XBC, ((XBC, C_XBC),)),
           (512, ((512, C_K),)),
           (128, ((128, C_DT),)))


def dproj_bwd(pieces, wcat, x, dy, norm_w, scale):
    t = x.shape[0]
    tm = min(t, 256)
    nt = t // tm
    wblocks = [blk for _, subs in DPIECES for blk in subs]
    npc, nwb = len(DPIECES), len(wblocks)

    def body(*refs):
        p_refs, w_refs = refs[:npc], refs[npc:npc + nwb]
        x_ref, dy_ref, nw_ref, sc_ref, gx_ref, dnw_ref, dsc_ref, dsh_ref, dwe_ref = refs[npc + nwb:]
        i = pl.program_id(0)

        @pl.when(i == 0)
        def _():
            for ref in (dwe_ref, dsh_ref, dnw_ref, dsc_ref):
                ref[...] = jnp.zeros_like(ref)

        dh, wi = None, 0
        for p_ref, (_, subs) in zip(p_refs, DPIECES):
            loc = 0
            for w, _ in subs:
                part = _dot_nt(p_ref[:, loc:loc + w], w_refs[wi][...])
                dh = part if dh is None else dh + part
                loc += w
                wi += 1
        xv = x_ref[...]
        r = lax.rsqrt(jnp.mean(xv * xv, axis=-1, keepdims=True) + EPS)
        xn = xv * r
        weff = nw_ref[...] * (1.0 + sc_ref[...])
        dxn = dh * weff
        gx_ref[...] = dy_ref[...] + r * (dxn - xn * jnp.mean(dxn * xn, axis=-1, keepdims=True))
        dwe_ref[...] += _csum(dh * xn)
        dsh_ref[...] += _csum(dh)

        @pl.when(i == nt - 1)
        def _():
            dwe = dwe_ref[...]
            dnw_ref[...] = dwe * (1.0 + sc_ref[...])
            dsc_ref[...] = dwe * nw_ref[...]

    vec = pl.BlockSpec((1, D), lambda i: (0, 0))
    row = pl.BlockSpec((tm, D), lambda i: (i, 0))
    return pl.pallas_call(
        body, name="dproj_bwd", grid=(nt,),
        in_specs=[pl.BlockSpec((tm, pw), lambda i: (i, 0)) for pw, _ in DPIECES]
        + [pl.BlockSpec((D, w), functools.partial(lambda i, b: (0, b), b=off // w), pipeline_mode=pl.Buffered(1))
           for w, off in wblocks]
        + [row, row, vec, vec],
        out_specs=[row, vec, vec, vec],
        out_shape=[jax.ShapeDtypeStruct((t, D), F32), jax.ShapeDtypeStruct((1, D), F32),
                   jax.ShapeDtypeStruct((1, D), F32), jax.ShapeDtypeStruct((1, D), F32)],
        scratch_shapes=[pltpu.VMEM((1, D), F32)],
        compiler_params=_cp(("arbitrary",)),
    )(*pieces, *([wcat] * nwb), x, dy, norm_w, scale)


def wgrad(at, b, name, bn, after):
    m, t = at.shape
    n = b.shape[1]
    tk = min(t, 2048)
    bm = min(m, 1024)

    def body(a_ref, b_ref, after_ref, o_ref):
        part = _dot(a_ref[...], b_ref[...])

        @pl.when(pl.program_id(2) == 0)
        def _():
            o_ref[...] = part

        @pl.when(pl.program_id(2) > 0)
        def _():
            o_ref[...] += part

    return pl.pallas_call(
        body, name=name, grid=(m // bm, n // bn, t // tk),
        in_specs=[pl.BlockSpec((bm, tk), lambda i, j, k: (i, k)), pl.BlockSpec((tk, bn), lambda i, j, k: (k, j)), ANY],
        out_specs=pl.BlockSpec((bm, bn), lambda i, j, k: (i, j)),
        out_shape=jax.ShapeDtypeStruct((m, n), F32),
        compiler_params=_cp(("parallel", "parallel", "arbitrary")),
    )(at, b, after)


SUM_TR = 512


def pair_sum(g, core, theirs, name):
    w = g.shape[2]
    nh = HROWS // SUM_TR

    def body(core_ref, a_ref, b_ref, o_ref, ob_ref):
        s = a_ref[...] + b_ref[...]
        o_ref[...] = s
        ob_ref[...] = s.astype(BF16)

    spec = pl.BlockSpec((1, SUM_TR, w), lambda d, i, c: (d, i, 0))
    return pl.pallas_call(
        body, name=name,
        out_shape=[jax.ShapeDtypeStruct((4, HROWS, w), F32), jax.ShapeDtypeStruct((4, HROWS, w), BF16)],
        grid_spec=pltpu.PrefetchScalarGridSpec(
            num_scalar_prefetch=1, grid=(4, nh),
            in_specs=[pl.BlockSpec((1, SUM_TR, w), lambda d, i, c: (d, c[0] * nh + i, 0)), spec],
            out_specs=[spec, spec]),
        compiler_params=_cp(("parallel", "parallel")))(core.reshape(1).astype(jnp.int32), g, theirs)


def chip_sum(part, chip, others, name):
    r, w = part.shape[1:]

    def body(chip_ref, a_ref, b_ref, o_ref):
        acc = a_ref[0]
        for k in range(3):
            acc = acc + b_ref[k].astype(F32)
        o_ref[...] = acc

    return pl.pallas_call(
        body, name=name, out_shape=jax.ShapeDtypeStruct((r, w), F32),
        grid_spec=pltpu.PrefetchScalarGridSpec(
            num_scalar_prefetch=1, grid=(r // SUM_TR,),
            in_specs=[pl.BlockSpec((1, SUM_TR, w), lambda i, c: (c[0], i, 0)),
                      pl.BlockSpec((3, SUM_TR, w), lambda i, c: (0, i, 0))],
            out_specs=pl.BlockSpec((SUM_TR, w), lambda i, c: (i, 0))),
        compiler_params=_cp(("parallel",)))(chip.reshape(1).astype(jnp.int32), part, others)


def sum_devices(g):
    r = g.shape[1]

    def body(g_ref, o_ref):
        acc = g_ref[0]
        for d in range(1, 8):
            acc = acc + g_ref[d]
        o_ref[...] = acc

    return pl.pallas_call(body, name="sum_devices", out_shape=jax.ShapeDtypeStruct((r, 1024), F32),
                          compiler_params=_cp())(g)


def adamw(w, g, m, v, name):
    r, c = w.shape
    tr = r
    for cand in (256, 128, 64, 32, 16, 8):
        if r % cand == 0 and r > cand:
            tr = cand
            break

    def body(w_ref, g_ref, m_ref, v_ref, d_ref, nm_ref, nv_ref):
        gv = g_ref[...]
        mn = ADAM_B1 * m_ref[...] + (1.0 - ADAM_B1) * gv
        vn = ADAM_B2 * v_ref[...] + (1.0 - ADAM_B2) * (gv * gv)
        m_hat = mn / (1.0 - ADAM_B1 ** ADAM_STEP)
        v_hat = vn / (1.0 - ADAM_B2 ** ADAM_STEP)
        d_ref[...] = -ADAM_LR * (m_hat / (jnp.sqrt(v_hat) + ADAM_EPS) + ADAM_WD * w_ref[...])
        nm_ref[...] = mn
        nv_ref[...] = vn

    spec = pl.BlockSpec((tr, c), lambda i: (i, 0))
    sd = jax.ShapeDtypeStruct((r, c), F32)
    return pl.pallas_call(body, name=name, grid=(r // tr,), in_specs=[spec] * 4, out_specs=[spec] * 3,
                          out_shape=[sd, sd, sd], compiler_params=_cp(("parallel",)))(w, g, m, v)


def adamw_halves(w, mine, theirs, core, m, v, name):
    r, c = w.shape
    tr = 128
    nh = HROWS // tr

    def body(core_ref, w_ref, a_ref, b_ref, m_ref, v_ref, g_ref, d_ref, nm_ref, nv_ref):
        gv = jnp.where(pl.program_id(0) // nh == core_ref[0], a_ref[...], b_ref[...])
        mn = ADAM_B1 * m_ref[...] + (1.0 - ADAM_B1) * gv
        vn = ADAM_B2 * v_ref[...] + (1.0 - ADAM_B2) * (gv * gv)
        m_hat = mn / (1.0 - ADAM_B1 ** ADAM_STEP)
        v_hat = vn / (1.0 - ADAM_B2 ** ADAM_STEP)
        g_ref[...] = gv
        d_ref[...] = -ADAM_LR * (m_hat / (jnp.sqrt(v_hat) + ADAM_EPS) + ADAM_WD * w_ref[...])
        nm_ref[...] = mn
        nv_ref[...] = vn

    spec = pl.BlockSpec((tr, c), lambda i, s: (i, 0))
    half = pl.BlockSpec((tr, c), lambda i, s: (i % nh, 0))
    sd = jax.ShapeDtypeStruct((r, c), F32)
    return pl.pallas_call(
        body, name=name, out_shape=[sd, sd, sd, sd],
        grid_spec=pltpu.PrefetchScalarGridSpec(num_scalar_prefetch=1, grid=(r // tr,),
                                               in_specs=[spec, half, half, spec, spec], out_specs=[spec] * 4),
        compiler_params=_cp(("parallel",)))(core.reshape(1).astype(jnp.int32), w, mine, theirs, m, v)


ANY = pl.BlockSpec(memory_space=pl.ANY)
VM = pl.BlockSpec(memory_space=pltpu.VMEM)
OTHER_CHIPS = ((1, 0), (0, 1), (1, 1))


def _pos():
    return lax.axis_index("x"), lax.axis_index("y"), lax.axis_index("c")


def _flip(v, bit):
    return 1 - v if bit else v


def _rcopy(src, dst, ssem, rsem, peer):
    return pltpu.make_async_remote_copy(src_ref=src, dst_ref=dst, send_sem=ssem, recv_sem=rsem,
                                        device_id=peer, device_id_type=MESH)


def allgather_small(p, name):
    r = p.shape[0]

    def body(in_ref, out_ref, ssem, rsem, lsem):
        x, y, c = _pos()
        me = 4 * x + 2 * y + c
        loc = pltpu.make_async_copy(in_ref, out_ref.at[me], lsem)
        loc.start()
        sends = []
        peers = []
        for k in range(1, 8):
            px, py, pc = _flip(x, (k >> 2) & 1), _flip(y, (k >> 1) & 1), _flip(c, k & 1)
            peers.append((px, py, pc))
            cp = _rcopy(in_ref, out_ref.at[me], ssem.at[k - 1], rsem.at[k - 1], (px, py, pc))
            cp.start()
            sends.append(cp)
        for k in range(1, 8):
            px, py, pc = peers[k - 1]
            _rcopy(in_ref, out_ref.at[4 * px + 2 * py + pc], ssem.at[k - 1], rsem.at[k - 1], (px, py, pc)).wait_recv()
        for cp in sends:
            cp.wait_send()
        loc.wait()

    return pl.pallas_call(
        body, name=name, out_shape=jax.ShapeDtypeStruct((8, r, 1024), F32),
        in_specs=[VM], out_specs=VM,
        scratch_shapes=[pltpu.SemaphoreType.DMA((7,)), pltpu.SemaphoreType.DMA((7,)), pltpu.SemaphoreType.DMA],
    )(p)


def gather_weights(w_in_b, mod_sh):
    def body(wi_ref, m_ref, gi_ref, mo_ref, ssem, rsem, lsem):
        x, y, c = _pos()
        chip = 2 * x + y
        mine = pl.ds(pl.multiple_of(c * HROWS, 16), HROWS)
        other = pl.ds(pl.multiple_of((1 - c) * HROWS, 16), HROWS)
        sib = (x, y, 1 - c)
        pairs = ((wi_ref, gi_ref),)
        loc_m = pltpu.make_async_copy(m_ref, mo_ref.at[chip], lsem)
        loc_m.start()
        sends = []
        for k, (fx, fy) in enumerate(OTHER_CHIPS):
            peer = (_flip(x, fx), _flip(y, fy), c)
            for a, (w_ref, g_ref) in enumerate(pairs):
                cw = _rcopy(w_ref.at[mine], g_ref.at[chip, mine], ssem.at[6 * a + k], rsem.at[6 * a + k], peer)
                cw.start()
                sends.append(cw)
            cm = _rcopy(m_ref, mo_ref.at[chip], ssem.at[12 + k], rsem.at[12 + k], peer)
            cm.start()
            sends.append(cm)
        for k, (fx, fy) in enumerate(OTHER_CHIPS):
            px, py = _flip(x, fx), _flip(y, fy)
            for a, (w_ref, g_ref) in enumerate(pairs):
                got = g_ref.at[2 * px + py, mine]
                _rcopy(w_ref.at[mine], got, ssem.at[6 * a + k], rsem.at[6 * a + k], (px, py, c)).wait_recv()
                fw = _rcopy(got, got, ssem.at[6 * a + 3 + k], rsem.at[6 * a + 3 + k], sib)
                fw.start()
                sends.append(fw)
        for k, (fx, fy) in enumerate(OTHER_CHIPS):
            px, py = _flip(x, fx), _flip(y, fy)
            for a, (w_ref, g_ref) in enumerate(pairs):
                land = g_ref.at[2 * px + py, other]
                _rcopy(land, land, ssem.at[6 * a + 3 + k], rsem.at[6 * a + 3 + k], sib).wait_recv()
            _rcopy(m_ref, mo_ref.at[2 * px + py], ssem.at[12 + k], rsem.at[12 + k], (px, py, c)).wait_recv()
        for cp in sends:
            cp.wait_send()
        loc_m.wait()

    return pl.pallas_call(
        body, name="gather_weights",
        out_shape=[jax.ShapeDtypeStruct((4, D, SH_IN), BF16), jax.ShapeDtypeStruct((4, 8, 768), F32)],
        in_specs=[ANY, VM], out_specs=[ANY, VM],
        scratch_shapes=[pltpu.SemaphoreType.DMA((15,)), pltpu.SemaphoreType.DMA((15,)), pltpu.SemaphoreType.DMA],
    )(w_in_b, mod_sh)


def pair_exchange(g):
    def body(g_ref, r_ref, ssem, rsem):
        x, y, c = _pos()
        other = pl.ds(pl.multiple_of((1 - c) * HROWS, 8), HROWS)
        cp = _rcopy(g_ref.at[:, other, :], r_ref, ssem, rsem, (x, y, 1 - c))
        cp.start()
        cp.wait()

    return pl.pallas_call(
        body, name="pair_exchange", out_shape=jax.ShapeDtypeStruct((4, HROWS, g.shape[2]), F32),
        in_specs=[ANY], out_specs=ANY,
        scratch_shapes=[pltpu.SemaphoreType.DMA, pltpu.SemaphoreType.DMA],
    )(g)


HBM = pl.BlockSpec(memory_space=pltpu.HBM)
SEM = pl.BlockSpec(memory_space=pltpu.SEMAPHORE)
DATAFLOW = pltpu.SideEffectType.DATAFLOW_SIDE_EFFECTING


def split_start(name, make_copies, srcs, lands, nsem, after):
    arrays = [*srcs, *lands]
    n, ns = len(arrays), len(srcs)

    def body(*refs):
        for cp in make_copies(refs[:ns], refs[ns:n], refs[n + 1], refs[n + 2])[0]:
            cp.start()
        refs[-1][...] = jnp.zeros_like(refs[-1])

    res = pl.pallas_call(
        body, name=name,
        out_shape=(pltpu.SemaphoreType.DMA((nsem,)), pltpu.SemaphoreType.DMA((nsem,)),
                   *[pltpu.HBM(a.shape, a.dtype) for a in arrays], jax.ShapeDtypeStruct((8, 128), F32)),
        in_specs=(HBM,) * n + (ANY,), out_specs=(SEM, SEM) + (HBM,) * n + (VM,),
        input_output_aliases={i: 2 + i for i in range(n)},
        compiler_params=pltpu.CompilerParams(has_side_effects=DATAFLOW),
    )(*[pltpu.with_memory_space_constraint(a, pltpu.HBM) for a in arrays], after)
    return res[0], res[1], list(res[2:2 + n]), res[-1]


def split_wait(name, make_copies, ssem, rsem, arrays, ns, after):
    n = len(arrays)

    def body(*refs):
        sends, recvs = make_copies(refs[:ns], refs[ns:n], refs[n], refs[n + 1])
        for cp in sends:
            cp.wait_send()
        for cp in recvs:
            cp.wait_recv()

    return pl.pallas_call(
        body, name=name, out_shape=tuple(pltpu.HBM(a.shape, a.dtype) for a in arrays),
        in_specs=(HBM,) * n + (SEM, SEM, ANY), out_specs=(HBM,) * n,
        input_output_aliases={i: i for i in range(n)},
        compiler_params=pltpu.CompilerParams(has_side_effects=DATAFLOW),
    )(*arrays, ssem, rsem, after)


def _chip_copies(srcs, lands, ssem, rsem):
    x, y, c = _pos()
    copies = []
    for k, (fx, fy) in enumerate(OTHER_CHIPS):
        px, py = _flip(x, fx), _flip(y, fy)
        for a, (p_ref, l_ref) in enumerate(zip(srcs, lands)):
            copies.append(_rcopy(p_ref.at[2 * px + py], l_ref.at[k], ssem.at[3 * a + k], rsem.at[3 * a + k], (px, py, c)))
    return copies, copies


def _pair_copies(srcs, lands, ssem, rsem):
    x, y, c = _pos()
    other = pl.ds(pl.multiple_of((1 - c) * HROWS, 8), HROWS)
    copies = [_rcopy(srcs[0].at[:, other, :], lands[0], ssem.at[0], rsem.at[0], (x, y, 1 - c))]
    return copies, copies


def _rest_copies(srcs, lands, ssem, rsem):
    x, y, c = _pos()
    chip = 2 * x + y
    mine = pl.ds(pl.multiple_of(c * HROWS, 16), HROWS)
    sends, recvs = [], []
    for k, (fx, fy) in enumerate(OTHER_CHIPS):
        px, py = _flip(x, fx), _flip(y, fy)
        for t in range(2):
            rows_t = pl.ds(t * HROWS, HROWS)
            sends.append(_rcopy(srcs[0].at[mine], lands[0].at[chip, mine], ssem.at[2 * k + t], rsem.at[2 * k + c],
                                (px, py, t)))
            recvs.append(_rcopy(srcs[0].at[rows_t], lands[0].at[2 * px + py, rows_t], ssem.at[2 * k + t],
                                rsem.at[2 * k + t], (px, py, t)))
    return sends, recvs


def _swap_copies(srcs, lands, ssem, rsem):
    x, y, c = _pos()
    copies = [_rcopy(s_ref, l_ref, ssem.at[a], rsem.at[a], (x, y, 1 - c))
              for a, (s_ref, l_ref) in enumerate(zip(srcs, lands))]
    return copies, copies


def _flat(v, width=1024):
    v = v.reshape(-1)
    n = -(-v.shape[0] // width) * width
    return jnp.pad(v, (0, n - v.shape[0]))


def _rows(parts, rows):
    flat = jnp.concatenate(parts)
    return jnp.pad(flat, (0, rows * 1024 - flat.shape[0])).reshape(rows, 1024)


def _pack_small(b_ada, norm_w, conv_b, ssm_norm_w, q_norm_w, k_norm_w, sinks, dt_bias, a_log, d_skip, rel_bias,
                extra=None, tail=(), rows=16):
    misc = [q_norm_w, k_norm_w, sinks, dt_bias, a_log, d_skip] + ([] if extra is None else [extra])
    parts = [_flat(b_ada), _flat(norm_w), _flat(conv_b), _flat(ssm_norm_w)] + [_flat(v, 128) for v in misc]
    parts.append(jnp.zeros(((8 - len(misc)) * 128,), F32))
    parts.append(_flat(rel_bias))
    parts.append(jnp.zeros((5 * 1024,), F32))
    return _rows(parts + [_flat(v) for v in tail], rows)


def _unpack_small(p):
    misc = p[9]
    return dict(b_ada=p[0:3].reshape(1, 3072), norm_w=p[3:4], conv_b=p[4:7].reshape(1, 3072),
                ssm_norm_w=p[7:9].reshape(1, 2048), q_norm_w=misc[None, 0:64], k_norm_w=misc[None, 128:192],
                sinks=misc[None, 256:272], dt_bias=misc[None, 384:416], a_log=misc[None, 512:544],
                d_skip=misc[None, 640:672], rel_bias=p[10, :512].reshape(32, 16), extra=misc[768])


SMALL = ("b_ada", "norm_w", "conv_b", "ssm_norm_w", "q_norm_w", "k_norm_w", "sinks", "dt_bias", "a_log", "d_skip",
         "rel_bias")
WEIGHTS = ("w_ada", "b_ada", "norm_w", "w_in", "q_norm_w", "k_norm_w", "rel_bias", "sinks", "conv_w", "conv_b",
           "dt_bias", "a_log", "d_skip", "ssm_norm_w", "w_attn_proj", "w_ssm_proj", "w_out")
IN_COLS = ((0, 1024, C_Q), (1024, 256, C_K), (1280, 256, C_V), (1536, 1024, C_ZA), (2560, 2048, C_ZM),
           (4608, 3072, C_XBC), (7680, 32, C_DT), (7712, 1024, C_GA), (8736, 1024, C_GB))


def _to_cat(shards):
    parts, pos = [], 0
    for o, n, cnew in sorted(IN_COLS, key=lambda e: e[2]):
        assert cnew == pos
        c0 = o
        while c0 < o + n:
            i = c0 // SH_IN
            c1 = min(o + n, (i + 1) * SH_IN)
            parts.append(shards[i][:, c0 - i * SH_IN:c1 - i * SH_IN])
            c0 = c1
        pos += n
    parts.append(jnp.zeros((D, NP - pos), shards.dtype))
    return jnp.concatenate(parts, axis=1)


def _from_cat(dw_pieces):
    starts = [subs[0][1] for _, subs in DPIECES]

    def cols(c0, c1):
        p = max(q for q in range(len(starts)) if starts[q] <= c0)
        return dw_pieces[p][:, c0 - starts[p]:c1 - starts[p]]

    shards = []
    for i in range(4):
        lo, hi = i * SH_IN, (i + 1) * SH_IN
        parts = []
        for o, n, cnew in IN_COLS:
            a, b = max(o, lo), min(o + n, hi)
            if a < b:
                parts.append(cols(cnew + a - o, cnew + b - o))
        shards.append(jnp.concatenate(parts, axis=1))
    return jnp.stack(shards)


def kernel(x, c, w_ada, b_ada, norm_w, w_in, q_norm_w, k_norm_w, rel_bias, sinks, conv_w, conv_b, dt_bias, a_log, d_skip, ssm_norm_w, w_attn_proj, w_ssm_proj, w_out, loss_target, m_w_ada, m_b_ada, m_norm_w, m_w_in, m_q_norm_w, m_k_norm_w, m_rel_bias, m_sinks, m_conv_w, m_conv_b, m_dt_bias, m_a_log, m_d_skip, m_ssm_norm_w, m_w_attn_proj, m_w_ssm_proj, m_w_out, v_w_ada, v_b_ada, v_norm_w, v_w_in, v_q_norm_w, v_k_norm_w, v_rel_bias, v_sinks, v_conv_w, v_conv_b, v_dt_bias, v_a_log, v_d_skip, v_ssm_norm_w, v_w_attn_proj, v_w_ssm_proj, v_w_out):
    args = dict(locals())
    xi, yi, ci = lax.axis_index("x"), lax.axis_index("y"), lax.axis_index("c")
    chip = 2 * xi + yi
    me = 4 * xi + 2 * yi + ci
    x2 = x[0]
    tgt = loss_target[0]

    pay = _rows([c.reshape(-1), conv_w[0].reshape(-1)], 8)
    g0 = allgather_small(pay, "gather_cond")
    c_all = g0[:, 0, :]
    conv_w_full = g0[0::2, 1:4, :].reshape(4, CONV_K, 768).transpose(1, 0, 2).reshape(CONV_K, XBC)

    b_ada_sh = lax.dynamic_slice(b_ada, (0, chip * 768), (1, 768))
    mod_sh = ada_mod(c_all, w_ada[0], b_ada_sh)

    w_in_b = w_in[0].astype(BF16)
    w_rest_b = jnp.concatenate([w_attn_proj[0], w_ssm_proj[0], w_out[0]], axis=0).astype(BF16)
    wg_in, modg = gather_weights(w_in_b, mod_sh)
    wg_in = lax.dynamic_update_slice(wg_in, w_in_b[None], (chip, 0, 0))
    rs_sem, rr_sem, rest_thru, rest_tok = split_start("gather_rest_start", _rest_copies, [w_rest_b],
                                                      [lax.empty((4, D, D), BF16)], 6, modg)
    mod = lax.dynamic_slice(modg, (0, me, 0), (4, 1, 768)).reshape(1, 3 * D)
    shift, scale, gate = mod[:, :D], mod[:, D:2 * D] + rest_tok[:1, :1], mod[:, 2 * D:]
    wcat = _to_cat(wg_in)

    pad128 = lambda v: jnp.pad(v, ((0, 0), (0, 128 - v.shape[1])))
    dtb_p, alog_p, dsk_p = pad128(dt_bias), pad128(a_log), pad128(d_skip)
    bucket = _bucket_table()

    proj, dt_raw, h_t = norm_proj(x2, norm_w, scale, shift, wcat)
    biasm = bias_expand(rel_bias, sinks, bucket)
    ao = attn_fwd(proj, biasm, q_norm_w, k_norm_w)
    act, dsl = conv_fwd(proj, conv_w_full, conv_b)
    yss, sprev = ssd_fwd(act, dt_raw, dtb_p, alog_p, dsk_p)

    w_rest_b, wg_rest = split_wait("gather_rest_wait", _rest_copies, rs_sem, rr_sem, rest_thru, 1, yss)
    wg_rest = lax.dynamic_update_slice(wg_rest, w_rest_b[None], (chip, 0, 0))
    w_at = wg_rest[:, :R_AT].reshape(D, D)
    w_ss = wg_rest[:, R_AT:R_AT + R_SS].reshape(SSM_W, D)
    w_ou = wg_rest[:, R_AT + R_SS:].reshape(D, D)
    (loss_p, dy, dao, dmid, dyss, ua_t, yn_t, mg_t, dya, dyb, dout, dgate, dssm_nw) = tail(
        proj, ao, yss, x2, tgt, gate, ssm_norm_w, w_at, w_ss, w_ou)

    dq, dkv, dqw, dkw, dacc = attn_bwd(proj, dao, biasm, q_norm_w, k_norm_w)
    dbias = bias_reduce(dacc, bucket)
    drb = dbias[:, :NBUCKET].T
    dsk = dbias[:, NBUCKET].reshape(1, HQ)
    dact, ddt, ddtb, dalog, ddskip = ssd_bwd(act, dt_raw, dyss, sprev, dtb_p, alog_p, dsk_p)
    dxbc, dconv_w, dconv_b = conv_bwd(proj, dact, dsl, conv_w_full)

    dproj = (dq, dmid, dxbc, dkv, ddt)
    dwcat = [wgrad(h_t, piece, "dw_in_%d" % p, 1280 if piece.shape[1] == W_MID else min(piece.shape[1], 1024), rest_tok)
             for p, piece in enumerate(dproj)]

    g_in = _from_cat(dwcat)
    ps_sem, pr_sem, pair_thru, pair_tok = split_start("pair_in_start", _pair_copies, [g_in],
                                                      [lax.empty((4, HROWS, SH_IN), F32)], 1, loss_p)
    dw_at = wgrad(ua_t, dya, "dw_attn", 1024, pair_tok)
    dw_ss = wgrad(yn_t, dyb, "dw_ssm", 1024, pair_tok)
    dw_ou = wgrad(mg_t, dout, "dw_out", 1024, pair_tok)
    g_rest = jnp.concatenate([dw_at.reshape(4, R_AT, D), dw_ss.reshape(4, R_SS, D), dw_ou.reshape(4, R_OU, D)], axis=1)
    sib_rest = pair_exchange(g_rest)
    g_in, sib_in = split_wait("pair_in_wait", _pair_copies, ps_sem, pr_sem, pair_thru, 1, sib_rest)
    part_in, pb_in = pair_sum(g_in, ci, sib_in, "pair_sum_in")
    part_rest, pb_rest = pair_sum(g_rest, ci, sib_rest, "pair_sum_rest")
    cs_sem, cr_sem, chip_thru, token = split_start(
        "chip_exchange_start", _chip_copies, [pb_in, pb_rest],
        [lax.empty((3, HROWS, SH_IN), BF16), lax.empty((3, HROWS, D), BF16)], 6, part_rest)
    grad_x, dnorm_w, dscale, dshift = dproj_bwd(dproj, wcat, x2, dy, norm_w, scale + token[:1, :1])
    _, _, oth_in, oth_rest = split_wait("chip_exchange_wait", _chip_copies, cs_sem, cr_sem, chip_thru, 2, dshift)
    red_in = chip_sum(part_in, chip, oth_in, "chip_sum_in")
    red_rest = chip_sum(part_rest, chip, oth_rest, "chip_sum_rest")
    sw_ssem, sw_rsem, swap_thru, swap_tok = split_start(
        "pair_swap_start", _swap_copies, [red_in, red_rest],
        [lax.empty((HROWS, SH_IN), F32), lax.empty((HROWS, D), F32)], 2, red_rest)

    dmod = jnp.concatenate([dshift, dscale, dgate], axis=1)
    gsmall = _pack_small(dmod, dnorm_w, dconv_b, dssm_nw, dqw, dkw, dsk[:, :HQ], ddtb[:, :SH], dalog[:, :SH],
                         ddskip[:, :SH], drb, extra=loss_p[:, :1] + swap_tok[:1, :1], tail=(dconv_w,), rows=32)
    gall = allgather_small(gsmall, "gather_small_grads")
    ssum = sum_devices(gall)
    gs = _unpack_small(ssum[:16])
    loss = gs["extra"]
    dconv_w_sh = lax.dynamic_slice(ssum[16:28].reshape(CONV_K, XBC), (0, chip * 768), (CONV_K, 768))
    dmod_all = gall[:, 0:3, :].reshape(8, 3 * D)
    dw_ada = ada_grad(c_all, lax.dynamic_slice(dmod_all, (0, chip * 768), (8, 768)))

    grads = dict(gs)
    grads["w_ada"] = dw_ada
    grads["conv_w"] = dconv_w_sh

    delta, new_m, new_v = {}, {}, {}

    def step(n):
        delta[n], new_m[n], new_v[n] = adamw(args[n][0], grads[n], args["m_" + n][0], args["v_" + n][0], "adamw_" + n)

    step("w_ada")
    step("conv_w")
    ws = _pack_small(*[args[n] for n in SMALL])
    ms = _pack_small(*[args["m_" + n] for n in SMALL])
    vs = _pack_small(*[args["v_" + n] for n in SMALL])
    d_s, m_s, v_s = adamw(ws, ssum[:16], ms, vs, "adamw_small")
    red_in, red_rest, recv_in, recv_rest = split_wait("pair_swap_wait", _swap_copies, sw_ssem, sw_rsem, swap_thru, 2, d_s)
    d_s, m_s, v_s = _unpack_small(d_s), _unpack_small(m_s), _unpack_small(v_s)
    for n in SMALL:
        delta[n], new_m[n], new_v[n] = d_s[n], m_s[n], v_s[n]
    grads["w_in"], delta["w_in"], new_m["w_in"], new_v["w_in"] = adamw_halves(
        w_in[0], red_in, recv_in, ci, m_w_in[0], v_w_in[0], "adamw_w_in")
    g_shard_rest = jnp.concatenate([jnp.where(ci == 0, red_rest, recv_rest), jnp.where(ci == 0, recv_rest, red_rest)],
                                   axis=0)
    grads["w_attn_proj"] = g_shard_rest[:R_AT]
    grads["w_ssm_proj"] = g_shard_rest[R_AT:R_AT + R_SS]
    grads["w_out"] = g_shard_rest[R_AT + R_SS:]
    for n in ("w_attn_proj", "w_ssm_proj", "w_out"):
        step(n)

    def shaped(n, a):
        return a.reshape(args[n].shape)

    outs = [loss, grad_x[None]]
    for table in (grads, delta, new_m, new_v):
        outs += [shaped(n, table[n]) for n in WEIGHTS]
    return tuple(outs)
```

```python
import functools
import math

import numpy as np
import jax
import jax.numpy as jnp
from jax import lax
from jax.experimental import pallas as pl
from jax.experimental.pallas import tpu as pltpu

F32 = jnp.float32
BF16 = jnp.bfloat16
MESH = pl.DeviceIdType.MESH

D = 1024
HQ, HKV, GRP, DH = 16, 4, 4, 64
BLK = 128
NBUCKET, MAXDIST = 32, 128
SSM_W, SH, SG, SR, SP, SN = 2048, 32, 4, 8, 64, 128
CONV_K = 4
XBC = SSM_W + 2 * SG * SN
IN_W = 9760
EPS = 1e-6
NEG = -1e30
SCALE = DH ** -0.5

C_Q, C_ZA, C_GA, C_GB, C_ZM, C_XBC, C_K, C_V, C_DT = 0, 1024, 2048, 3072, 4096, 6144, 9216, 9472, 9728
NP = 9984
TN = 3328
W_MID = C_XBC - C_ZA

SH_IN = IN_W // 4
R_AT, R_SS, R_OU = 256, 512, 256
HROWS = D // 2

ADAM_LR, ADAM_B1, ADAM_B2, ADAM_EPS, ADAM_WD, ADAM_STEP = 0.001, 0.9, 0.999, 1e-08, 0.01, 10

VMEM_LIMIT = 56 * 1024 * 1024


def _cp(sem=None):
    if sem is None:
        return pltpu.CompilerParams(vmem_limit_bytes=VMEM_LIMIT)
    return pltpu.CompilerParams(dimension_semantics=sem, vmem_limit_bytes=VMEM_LIMIT)


def _sig(x):
    return 0.5 * jnp.tanh(0.5 * x) + 0.5


def _dot(a, b):
    return jnp.dot(a, b, preferred_element_type=F32)


def _dot_nt(a, b):
    return lax.dot_general(a, b, (((1,), (1,)), ((), ())), preferred_element_type=F32)


def _dot_tn(a, b):
    return lax.dot_general(a, b, (((0,), (0,)), ((), ())), preferred_element_type=F32)


def _rsum(x):
    return jnp.sum(x, axis=-1, keepdims=True)


def _csum(x):
    return jnp.sum(x, axis=0, keepdims=True)


def _asum(x):
    return _csum(_rsum(x))


def _full(shape):
    nd = len(shape)
    return pl.BlockSpec(shape, lambda *_: (0,) * nd)


def ada_mod(c_all, w_ada_sh, b_ada_sh):
    def body(c_ref, w_ref, b_ref, o_ref):
        cv = c_ref[...]
        s = cv * _sig(cv)
        o_ref[...] = jnp.dot(s, w_ref[...], preferred_element_type=F32,
                             precision=lax.Precision.HIGHEST) + b_ref[...]

    n = w_ada_sh.shape[1]
    return pl.pallas_call(body, name="ada_mod", out_shape=jax.ShapeDtypeStruct((8, n), F32),
                          compiler_params=_cp())(c_all, w_ada_sh, b_ada_sh)


def ada_grad(c_all, dmod_sh):
    def body(c_ref, d_ref, o_ref):
        cv = c_ref[...]
        s = cv * _sig(cv)
        o_ref[...] = lax.dot_general(s, d_ref[...], (((0,), (0,)), ((), ())), preferred_element_type=F32,
                                     precision=lax.Precision.HIGHEST)

    n = dmod_sh.shape[1]
    return pl.pallas_call(body, name="ada_grad", out_shape=jax.ShapeDtypeStruct((D, n), F32),
                          compiler_params=_cp())(c_all, dmod_sh)


def norm_proj(x, norm_w, scale, shift, wcat):
    t = x.shape[0]
    tm = min(t, 1024)

    def body(x_ref, nw_ref, sc_ref, sh_ref, w_ref, p_ref, dt_ref, ht_ref, hs):
        @pl.when(pl.program_id(1) == 0)
        def _():
            xv = x_ref[...]
            r = lax.rsqrt(jnp.mean(xv * xv, axis=-1, keepdims=True) + EPS)
            h = (xv * r) * nw_ref[...]
            h = h * (1.0 + sc_ref[...]) + sh_ref[...]
            hs[...] = h.astype(BF16)
            ht_ref[...] = h.T.astype(BF16)

        p = _dot(hs[...], w_ref[...])
        p_ref[...] = p.astype(BF16)

        @pl.when(pl.program_id(1) == C_DT // TN)
        def _():
            dt_ref[...] = p[:, C_DT % TN:C_DT % TN + 128]

    vec = pl.BlockSpec((1, D), lambda i, j: (0, 0))
    return pl.pallas_call(
        body, name="norm_proj", grid=(t // tm, NP // TN),
        in_specs=[pl.BlockSpec((tm, D), lambda i, j: (i, 0)), vec, vec, vec,
                  pl.BlockSpec((D, TN), lambda i, j: (0, j))],
        out_specs=[pl.BlockSpec((tm, TN), lambda i, j: (i, j)), pl.BlockSpec((tm, 128), lambda i, j: (i, 0)),
                   pl.BlockSpec((D, tm), lambda i, j: (0, i))],
        out_shape=[jax.ShapeDtypeStruct((t, NP), BF16), jax.ShapeDtypeStruct((t, 128), F32),
                   jax.ShapeDtypeStruct((D, t), BF16)],
        scratch_shapes=[pltpu.VMEM((tm, D), BF16)],
        compiler_params=_cp(("parallel", "arbitrary")),
    )(x, norm_w, scale, shift, wcat)


def _bucket_table():
    qi = np.arange(BLK)[:, None]
    kj = np.arange(2 * BLK)[None, :]
    dist = qi + BLK - kj
    n = np.maximum(dist, 0)
    max_exact = NBUCKET // 2
    nf = np.maximum(n, 1).astype(np.float32)
    large = max_exact + (np.log(nf / np.float32(max_exact)) / np.float32(math.log(MAXDIST / max_exact))
                         * np.float32(NBUCKET - max_exact)).astype(np.int32)
    large = np.minimum(large, NBUCKET - 1)
    bucket = np.where(n < max_exact, n, large).astype(np.int32)
    valid = (dist >= 0) & (dist < BLK)
    return np.where(valid, bucket, -1).astype(np.int32)


def bias_expand(rel_bias, sinks, bucket):
    def body(rb_ref, sk_ref, bk_ref, o_ref):
        bk = bk_ref[...]
        col = lax.broadcasted_iota(jnp.int32, (BLK, 2 * BLK), 1)

        def head(hd, carry):
            def step(b, acc):
                return jnp.where(bk == b, rb_ref[b, hd], acc)

            acc = lax.fori_loop(0, NBUCKET, step, jnp.full((BLK, 2 * BLK), NEG, F32))
            acc = jnp.where(col == 0, sk_ref[0, hd], acc)
            o_ref[1, hd] = acc
            o_ref[0, hd] = jnp.where(jnp.logical_and(col > 0, col < BLK), NEG, acc)
            return carry

        lax.fori_loop(0, HQ, head, 0)

    smem = pl.BlockSpec(memory_space=pltpu.SMEM)
    return pl.pallas_call(
        body, name="bias_expand", in_specs=[smem, smem, VM], out_specs=VM,
        out_shape=jax.ShapeDtypeStruct((2, HQ, BLK, 2 * BLK), F32), compiler_params=_cp(),
    )(rel_bias, sinks, jnp.asarray(bucket))


def bias_reduce(dacc, bucket):
    col = np.arange(BLK * 2 * BLK) % (2 * BLK)
    lane = np.arange(128)[None, :]
    member = (bucket.reshape(-1)[:, None] == lane) | ((col[:, None] == 0) & (lane == NBUCKET))

    def body(d_ref, m_ref, o_ref):
        mm = m_ref[...]
        o_ref[...] = sum(_dot(part, mm) for part in _split3(d_ref[...]))

    return pl.pallas_call(body, name="bias_reduce", out_shape=jax.ShapeDtypeStruct((HQ, 128), F32),
                          compiler_params=_cp())(dacc.reshape(HQ, BLK * 2 * BLK), jnp.asarray(member, BF16))


GQ = GRP * BLK


def _stack_heads(x, nh):
    return jnp.concatenate([x[:, DH * h:DH * (h + 1)] for h in range(nh)], axis=0)


def _unstack(xs, nh):
    rows = xs.shape[0] // nh
    return jnp.concatenate([xs[rows * h:rows * (h + 1)] for h in range(nh)], axis=1)


def _rms(x):
    return lax.rsqrt(jnp.mean(x * x, axis=-1, keepdims=True) + EPS)


def _stack_q(q, qw):
    qs = _stack_heads(q, HQ)
    r = _rms(qs)
    qhat = qs * r
    return qhat * qw, qhat, r


def _band_first(shape):
    return (lax.broadcasted_iota(jnp.int32, shape, 0) & (2 * BLK - 1)) == 0


def _stack_kv(kp, kc, vp, vc, kw):
    ks = _stack_heads(jnp.concatenate([kp, kc], axis=0), HKV)
    r = _rms(ks)
    khat = ks * r
    first = _band_first(ks.shape)
    kn = jnp.where(first, 0.0, khat * kw)
    v2 = jnp.where(first, 0.0, _stack_heads(jnp.concatenate([vp, vc], axis=0), HKV)).astype(BF16)
    return kn, khat, r, v2


def _softmax_rows(s):
    p = jnp.exp(s - jnp.max(s, axis=-1, keepdims=True))
    return p * (1.0 / _rsum(p))


def attn_fwd(proj, biasm, q_norm_w, k_norm_w):
    t = proj.shape[0]
    nb2 = t // (2 * BLK)

    def body(q_ref, kc_ref, kp_ref, vc_ref, vp_ref, bm_ref, qw_ref, kw_ref, o_ref):
        n = pl.program_id(0)
        f32 = lambda a: a.astype(F32)
        for half in range(2):
            rows = slice(BLK * half, BLK * (half + 1))
            kp = kp_ref[...] if half == 0 else kc_ref[0:BLK, :]
            vp = vp_ref[...] if half == 0 else vc_ref[0:BLK, :]
            bm = bm_ref[jnp.minimum(n, 1)] if half == 0 else bm_ref[1]
            qn = _stack_q(f32(q_ref[rows, :]), qw_ref[...])[0].astype(BF16)
            kn, _, _, v2 = _stack_kv(f32(kp), f32(kc_ref[rows, :]), f32(vp), f32(vc_ref[rows, :]), kw_ref[...])
            knb = kn.astype(BF16)
            s = jnp.concatenate([_dot_nt(qn[GQ * j:GQ * (j + 1)], knb[2 * BLK * j:2 * BLK * (j + 1)])
                                 for j in range(HKV)], axis=0)
            pr = _softmax_rows(s * SCALE + bm.reshape(HQ * BLK, 2 * BLK)).astype(BF16)
            o = jnp.concatenate([_dot(pr[GQ * j:GQ * (j + 1)], v2[2 * BLK * j:2 * BLK * (j + 1)])
                                 for j in range(HKV)], axis=0)
            o_ref[rows, :] = _unstack(o, HQ).astype(BF16)

    kblk, vblk = C_K // 256, C_V // 256
    prev = lambda n: jnp.maximum(2 * n - 1, 0)
    return pl.pallas_call(
        body, name="attn_fwd", grid=(nb2,),
        in_specs=[pl.BlockSpec((2 * BLK, D), lambda n: (n, 0)),
                  pl.BlockSpec((2 * BLK, 256), lambda n: (n, kblk)),
                  pl.BlockSpec((BLK, 256), lambda n: (prev(n), kblk)),
                  pl.BlockSpec((2 * BLK, 256), lambda n: (n, vblk)),
                  pl.BlockSpec((BLK, 256), lambda n: (prev(n), vblk)),
                  _full((2, HQ, BLK, 2 * BLK)), _full((1, DH)), _full((1, DH))],
        out_specs=pl.BlockSpec((2 * BLK, D), lambda n: (n, 0)),
        out_shape=jax.ShapeDtypeStruct((t, D), BF16),
        compiler_params=_cp(("parallel",)),
    )(proj, proj, proj, proj, proj, biasm, q_norm_w, k_norm_w)


def attn_bwd(proj, dao, biasm, q_norm_w, k_norm_w):
    t = proj.shape[0]
    nb = t // BLK
    kb = 2 * BLK

    def body(q_ref, kc_ref, kp_ref, vc_ref, vp_ref, do_ref, bm_ref, qw_ref, kw_ref,
             dq_ref, dkv_ref, dqw_ref, dkw_ref, dacc_ref, ck, cv, pk, pv, nk, nv):
        n = pl.program_id(0)

        @pl.when(n == 0)
        def _():
            for ref in (dqw_ref, dkw_ref, dacc_ref, ck, cv):
                ref[...] = jnp.zeros_like(ref)

        qw = qw_ref[...]
        kw = kw_ref[...]
        f = lambda ref: ref[...].astype(F32)
        kn, khat, rk, v2 = _stack_kv(f(kp_ref), f(kc_ref), f(vp_ref), f(vc_ref), kw)
        grp = lambda a, j: a[GQ * j:GQ * (j + 1)]
        band = lambda a, j: a[kb * j:kb * (j + 1)]

        @pl.when(n < nb)
        def _():
            qn, qhat, rq = _stack_q(f(q_ref), qw)
            qnb = qn.astype(BF16)
            knb = kn.astype(BF16)
            dos = _stack_heads(f(do_ref), HQ).astype(BF16)
            s = jnp.concatenate([_dot_nt(grp(qnb, j), band(knb, j)) for j in range(HKV)], axis=0)
            pr = _softmax_rows(s * SCALE + bm_ref[0].reshape(HQ * BLK, kb))
            dp = jnp.concatenate([_dot_nt(grp(dos, j), band(v2, j)) for j in range(HKV)], axis=0)
            ds = pr * (dp - _rsum(pr * dp))
            dacc_ref[...] += ds.reshape(HQ, BLK, kb)
            dsb = ds.astype(BF16)
            prb = pr.astype(BF16)
            dqn = jnp.concatenate([_dot(grp(dsb, j), band(knb, j)) for j in range(HKV)], axis=0) * SCALE
            dqhat = dqn * qw
            dq = rq * (dqhat - qhat * jnp.mean(dqhat * qhat, axis=-1, keepdims=True))
            dq_ref[...] = _unstack(dq, HQ).astype(BF16)
            dqw_ref[...] += _csum(dqn * qhat)
            first = _band_first((kb, DH))
            for j in range(HKV):
                rows = slice(BLK * j, BLK * (j + 1))
                dkn = jnp.where(first, 0.0, _dot_tn(grp(dsb, j), grp(qnb, j)) * SCALE)
                dvj = jnp.where(first, 0.0, _dot_tn(grp(prb, j), grp(dos, j)))
                pk[rows, :] = dkn[:BLK]
                nk[rows, :] = dkn[BLK:]
                pv[rows, :] = dvj[:BLK]
                nv[rows, :] = dvj[BLK:]

        @pl.when(n == nb)
        def _():
            for ref in (pk, pv, nk, nv):
                ref[...] = jnp.zeros_like(ref)

        khp = jnp.concatenate([khat[kb * j:kb * j + BLK] for j in range(HKV)], axis=0)
        rkp = jnp.concatenate([rk[kb * j:kb * j + BLK] for j in range(HKV)], axis=0)
        dkn = ck[...] + pk[...]
        dkhat = dkn * kw
        dk = rkp * (dkhat - khp * jnp.mean(dkhat * khp, axis=-1, keepdims=True))
        dkw_ref[...] += _csum(dkn * khp)
        dkv_ref[...] = jnp.concatenate([_unstack(dk, HKV), _unstack(cv[...] + pv[...], HKV)], axis=1).astype(BF16)
        ck[...] = nk[...]
        cv[...] = nv[...]

    kblk, vblk = C_K // 256, C_V // 256
    cur = lambda n: jnp.minimum(n, nb - 1)
    prev = lambda n: jnp.maximum(n - 1, 0)
    carry = pltpu.VMEM((HKV * BLK, DH), F32)
    return pl.pallas_call(
        body, name="attn_bwd", grid=(nb + 1,),
        in_specs=[pl.BlockSpec((BLK, D), lambda n: (cur(n), 0)),
                  pl.BlockSpec((BLK, 256), lambda n: (cur(n), kblk)), pl.BlockSpec((BLK, 256), lambda n: (prev(n), kblk)),
                  pl.BlockSpec((BLK, 256), lambda n: (cur(n), vblk)), pl.BlockSpec((BLK, 256), lambda n: (prev(n), vblk)),
                  pl.BlockSpec((BLK, D), lambda n: (cur(n), 0)),
                  pl.BlockSpec((1, HQ, BLK, kb), lambda n: (jnp.minimum(n, 1), 0, 0, 0)),
                  _full((1, DH)), _full((1, DH))],
        out_specs=[pl.BlockSpec((BLK, D), lambda n: (cur(n), 0)),
                   pl.BlockSpec((BLK, 512), lambda n: (prev(n), 0)),
                   _full((1, DH)), _full((1, DH)), _full((HQ, BLK, kb))],
        out_shape=[jax.ShapeDtypeStruct((t, D), BF16), jax.ShapeDtypeStruct((t, 512), BF16),
                   jax.ShapeDtypeStruct((1, DH), F32),
                   jax.ShapeDtypeStruct((1, DH), F32), jax.ShapeDtypeStruct((HQ, BLK, kb), F32)],
        scratch_shapes=[carry] * 6,
        compiler_params=_cp(("arbitrary",)),
    )(proj, proj, proj, proj, proj, dao, biasm, q_norm_w, k_norm_w)


CONV_TM, CONV_CW, CONV_RC, HALO = 1024, 1024, 32, 16


def conv_fwd(proj, conv_w, conv_b):
    t = proj.shape[0]
    tm = min(t, CONV_TM)
    c0 = C_XBC // CONV_CW

    def body(x_ref, xp_ref, w_ref, b_ref, o_ref, ds_ref):
        i = pl.program_id(1)
        w = w_ref[...]
        b = b_ref[...]
        for r in range(tm // CONV_RC):
            lo = r * CONV_RC
            if r == 0:
                head = jnp.where(i == 0, 0.0, xp_ref[...].astype(F32))
                win = jnp.concatenate([head, x_ref[0:CONV_RC, :].astype(F32)], axis=0)
            else:
                win = x_ref[lo - HALO:lo + CONV_RC, :].astype(F32)
            acc = b
            for j in range(CONV_K):
                acc = acc + w[j:j + 1] * win[HALO - 3 + j:HALO - 3 + j + CONV_RC]
            sg = _sig(acc)
            o_ref[lo:lo + CONV_RC, :] = acc * sg
            ds_ref[lo:lo + CONV_RC, :] = _dsilu(acc, sg).astype(BF16)

    rh = tm // HALO
    tile = pl.BlockSpec((tm, CONV_CW), lambda s, i: (i, s))
    return pl.pallas_call(
        body, name="conv_fwd", grid=(XBC // CONV_CW, t // tm),
        in_specs=[pl.BlockSpec((tm, CONV_CW), lambda s, i: (i, c0 + s)),
                  pl.BlockSpec((HALO, CONV_CW), lambda s, i: (jnp.maximum(i * rh - 1, 0), c0 + s)),
                  pl.BlockSpec((CONV_K, CONV_CW), lambda s, i: (0, s)), pl.BlockSpec((1, CONV_CW), lambda s, i: (0, s))],
        out_specs=[tile, tile],
        out_shape=[jax.ShapeDtypeStruct((t, XBC), F32), jax.ShapeDtypeStruct((t, XBC), BF16)],
        compiler_params=_cp(("parallel", "parallel")),
    )(proj, proj, conv_w, conv_b)


def conv_bwd(proj, dact, dsl, conv_w):
    t = proj.shape[0]
    tm = min(t, CONV_TM)
    nt = t // tm
    nr = tm // CONV_RC
    c0 = C_XBC // CONV_CW
    ext = CONV_RC + 8

    def body(x_ref, xp_ref, d_ref, dn_ref, s_ref, sn_ref, w_ref, dx_ref, dw_ref, db_ref):
        i = pl.program_id(1)

        @pl.when(i == 0)
        def _():
            dw_ref[...] = jnp.zeros_like(dw_ref)
            db_ref[...] = jnp.zeros_like(db_ref)

        w = w_ref[...]
        dws = [jnp.zeros((1, CONV_CW), F32) for _ in range(CONV_K)]
        db = jnp.zeros((1, CONV_CW), F32)
        for r in range(nr):
            lo = r * CONV_RC
            if r == 0:
                head = jnp.where(i == 0, 0.0, xp_ref[...].astype(F32))
                win = jnp.concatenate([head, x_ref[0:CONV_RC, :].astype(F32)], axis=0)
            else:
                win = x_ref[lo - HALO:lo + CONV_RC, :].astype(F32)
            if r < nr - 1:
                dext = d_ref[lo:lo + ext, :]
                sext = s_ref[lo:lo + CONV_RC + HALO, :].astype(F32)[0:ext]
            else:
                dext = jnp.concatenate([d_ref[lo:lo + CONV_RC, :], jnp.where(i == nt - 1, 0.0, dn_ref[...])], axis=0)
                sext = jnp.concatenate([s_ref[lo:lo + CONV_RC, :].astype(F32), sn_ref[...].astype(F32)], axis=0)[0:ext]
            dpre = dext * sext
            dx = jnp.zeros((CONV_RC, CONV_CW), F32)
            own = dpre[0:CONV_RC]
            for j in range(CONV_K):
                dx = dx + w[j:j + 1] * dpre[3 - j:3 - j + CONV_RC]
                dws[j] = dws[j] + _csum(own * win[HALO - 3 + j:HALO - 3 + j + CONV_RC])
            db = db + _csum(own)
            dx_ref[lo:lo + CONV_RC, :] = dx.astype(BF16)
        dw_ref[...] += jnp.concatenate(dws, axis=0)
        db_ref[...] += db

    rh = tm // HALO
    r8 = tm // 8
    nxt = lambda i, per: jnp.minimum((i + 1) * per, nt * per - 1)
    return pl.pallas_call(
        body, name="conv_bwd", grid=(XBC // CONV_CW, nt),
        in_specs=[pl.BlockSpec((tm, CONV_CW), lambda s, i: (i, c0 + s)),
                  pl.BlockSpec((HALO, CONV_CW), lambda s, i: (jnp.maximum(i * rh - 1, 0), c0 + s)),
                  pl.BlockSpec((tm, CONV_CW), lambda s, i: (i, s)),
                  pl.BlockSpec((8, CONV_CW), lambda s, i: (nxt(i, r8), s)),
                  pl.BlockSpec((tm, CONV_CW), lambda s, i: (i, s)),
                  pl.BlockSpec((HALO, CONV_CW), lambda s, i: (nxt(i, rh), s)),
                  pl.BlockSpec((CONV_K, CONV_CW), lambda s, i: (0, s))],
        out_specs=[pl.BlockSpec((tm, CONV_CW), lambda s, i: (i, s)),
                   pl.BlockSpec((CONV_K, CONV_CW), lambda s, i: (0, s)), pl.BlockSpec((1, CONV_CW), lambda s, i: (0, s))],
        out_shape=[jax.ShapeDtypeStruct((t, XBC), BF16), jax.ShapeDtypeStruct((CONV_K, XBC), F32),
                   jax.ShapeDtypeStruct((1, XBC), F32)],
        compiler_params=_cp(("parallel", "arbitrary")),
    )(proj, proj, dact, dact, dsl, dsl, conv_w)


def _split3(x):
    h = x.astype(BF16)
    r = x - h.astype(F32)
    m = r.astype(BF16)
    lo = (r - m.astype(F32)).astype(BF16)
    return h, m, lo


def _tri_mm(tri, x):
    h, m, lo = _split3(x)
    return _dot(tri, h) + _dot(tri, m) + _dot(tri, lo)


def _softplus(x):
    return jnp.maximum(x, 0.0) + jnp.log1p(jnp.exp(-jnp.abs(x)))


def _chunk_decays(dt_raw, dtb, alog):
    dtv = _softplus(dt_raw + dtb)
    a = -jnp.exp(alog)
    ri = lax.broadcasted_iota(jnp.int32, (BLK, BLK), 0)
    ci = lax.broadcasted_iota(jnp.int32, (BLK, BLK), 1)
    causal = ri >= ci
    acum = _tri_mm(causal.astype(BF16), dtv * a)
    return dtv, a, causal, acum, acum.T


NPAIR = SH // 2


def _pairs(x):
    return jnp.stack([x[:, 128 * k:128 * (k + 1)] for k in range(NPAIR)])


def _unpairs(x3):
    return jnp.concatenate([x3[k] for k in range(NPAIR)], axis=1)


def _per_head_cols(m):
    return jnp.stack([jnp.broadcast_to(m[:, h:h + 1], m.shape) for h in range(SH)])


def _pair_lanes(t):
    r = t.reshape(NPAIR, 2, t.shape[1], 128)
    lo = lax.broadcasted_iota(jnp.int32, (1, t.shape[1], 128), 2) < SP
    return jnp.where(lo, r[:, 0], r[:, 1])


class _Chunk:
    pass


def _chunk_common(dt_raw, dtb, alog, dskip):
    cm = _Chunk()
    cm.dtv, cm.a, cm.causal, acum, acum_t = _chunk_decays(dt_raw, dtb, alog)
    cm.acol = _per_head_cols(acum)
    cm.arow = jnp.stack([acum_t[h:h + 1, :] for h in range(SH)])
    apl = _pair_lanes(cm.acol)
    alast = apl[:, BLK - 1:BLK, :]
    cm.dpl = _pair_lanes(_per_head_cols(cm.dtv))
    cm.eapl = jnp.exp(apl)
    cm.epl = jnp.exp(alast - apl)
    cm.cdpl = jnp.exp(alast)
    cm.dskpl = _pair_lanes(_per_head_cols(dskip))
    cm.lo = lax.broadcasted_iota(jnp.int32, (1, BLK, 128), 2) < SP
    return cm


def ssd_fwd(act, dt_raw, dtb_p, alog_p, dsk_p):
    t = act.shape[0]
    nc = t // BLK

    def body(xs_ref, b_ref, c_ref, dt_ref, dtb_ref, al_ref, dk_ref, y_ref, sp_ref, st):
        c = pl.program_id(0)

        @pl.when(c == 0)
        def _():
            st[...] = jnp.zeros_like(st)

        s_t = st[...]
        sp_ref[0] = s_t
        cm = _chunk_common(dt_ref[...], dtb_ref[...], al_ref[...], dk_ref[...])
        gms, cbs, bts = [], [], []
        for g in range(SG):
            bf = b_ref[:, SN * g:SN * (g + 1)]
            cb = c_ref[:, SN * g:SN * (g + 1)].astype(BF16)
            gms.append(_dot_nt(cb, bf.astype(BF16)))
            cbs.append(cb)
            bts.append(bf.T.astype(BF16))
        lam = jnp.exp(jnp.where(cm.causal[None], cm.acol - cm.arow, NEG))
        m = (lam.reshape(SG, SR, BLK, BLK) * jnp.stack(gms)[:, None]).reshape(SH, BLK, BLK).astype(BF16)
        xs16 = _pairs(xs_ref[...])
        xdt16 = xs16 * cm.dpl
        x_lo = jnp.where(cm.lo, xdt16, 0.0).astype(BF16)
        x_hi = jnp.where(cm.lo, 0.0, xdt16).astype(BF16)
        s16 = _pairs(s_t)
        s16b = s16.astype(BF16)
        yd = jnp.stack([_dot(m[2 * k], x_lo[k]) + _dot(m[2 * k + 1], x_hi[k]) for k in range(NPAIR)])
        yo = jnp.stack([_dot(cbs[k // (NPAIR // SG)], s16b[k]) for k in range(NPAIR)])
        y_ref[...] = _unpairs(yd + yo * cm.eapl + cm.dskpl * xs16).astype(BF16)
        xe = (xdt16 * cm.epl).astype(BF16)
        st[...] = _unpairs(cm.cdpl * s16 + jnp.stack([_dot(bts[k // (NPAIR // SG)], xe[k]) for k in range(NPAIR)]))

    vec = _full((1, 128))
    return pl.pallas_call(
        body, name="ssd_fwd", grid=(nc,),
        in_specs=[pl.BlockSpec((BLK, SSM_W), lambda c: (c, 0)),
                  pl.BlockSpec((BLK, SG * SN), lambda c: (c, SSM_W // (SG * SN))),
                  pl.BlockSpec((BLK, SG * SN), lambda c: (c, SSM_W // (SG * SN) + 1)),
                  pl.BlockSpec((BLK, 128), lambda c: (c, 0)), vec, vec, vec],
        out_specs=[pl.BlockSpec((BLK, SSM_W), lambda c: (c, 0)), pl.BlockSpec((1, SN, SSM_W), lambda c: (c, 0, 0))],
        out_shape=[jax.ShapeDtypeStruct((t, SSM_W), BF16), jax.ShapeDtypeStruct((nc, SN, SSM_W), F32)],
        scratch_shapes=[pltpu.VMEM((SN, SSM_W), F32)],
        compiler_params=_cp(("arbitrary",)),
    )(act, act, act, dt_raw, dtb_p, alog_p, dsk_p)


def _head_sums(q):
    r = q.shape[1]
    lo = lax.broadcasted_iota(jnp.int32, (1, r, 128), 2) < SP
    s_lo = jnp.sum(jnp.where(lo, q, 0.0), axis=-1, keepdims=True)
    s_hi = jnp.sum(jnp.where(lo, 0.0, q), axis=-1, keepdims=True)
    lane = lax.broadcasted_iota(jnp.int32, (r, 128), 1)
    out = jnp.zeros((r, 128), F32)
    for k in range(NPAIR):
        out = jnp.where(lane == 2 * k, s_lo[k], jnp.where(lane == 2 * k + 1, s_hi[k], out))
    return out


def ssd_bwd(act, dt_raw, dy, sprev, dtb_p, alog_p, dsk_p):
    t = act.shape[0]
    nc = t // BLK

    def body(xs_ref, b_ref, c_ref, dt_ref, dy_ref, sp_ref, dtb_ref, al_ref, dk_ref,
             da_ref, ddt_ref, ddtb_ref, dal_ref, ddk_ref, dst):
        i = pl.program_id(0)

        @pl.when(i == 0)
        def _():
            dst[...] = jnp.zeros_like(dst)
            ddtb_ref[...] = jnp.zeros_like(ddtb_ref)
            dal_ref[...] = jnp.zeros_like(dal_ref)
            ddk_ref[...] = jnp.zeros_like(ddk_ref)

        dt_raw = dt_ref[...]
        dtb = dtb_ref[...]
        cm = _chunk_common(dt_raw, dtb, al_ref[...], dk_ref[...])
        ri = lax.broadcasted_iota(jnp.int32, (BLK, BLK), 0)
        ci = lax.broadcasted_iota(jnp.int32, (BLK, BLK), 1)
        lam_t = jnp.exp(jnp.where((ri <= ci)[None], cm.arow - cm.acol, NEG))
        bbs, cbs, cts, gms = [], [], [], []
        for g in range(SG):
            bf = b_ref[:, SN * g:SN * (g + 1)]
            cf = c_ref[:, SN * g:SN * (g + 1)]
            bbs.append(bf.astype(BF16))
            cbs.append(cf.astype(BF16))
            cts.append(cf.T.astype(BF16))
            gms.append(_dot_nt(bbs[g], cbs[g]))
        grp = lambda k: k // (NPAIR // SG)
        xs16 = _pairs(xs_ref[...])
        dy16 = _pairs(dy_ref[...].astype(F32))
        sp16 = _pairs(sp_ref[0])
        ds16 = _pairs(dst[...])
        xdt16 = xs16 * cm.dpl
        xdtb = xdt16.astype(BF16)
        dyh = [jnp.where(cm.lo, dy16, 0.0).astype(BF16), jnp.where(cm.lo, 0.0, dy16).astype(BF16)]
        m_t = (lam_t.reshape(SG, SR, BLK, BLK) * jnp.stack(gms)[:, None]).reshape(SH, BLK, BLK).astype(BF16)
        dxdt = jnp.stack([_dot(m_t[2 * k], dyh[0][k]) + _dot(m_t[2 * k + 1], dyh[1][k]) for k in range(NPAIR)])
        dm_t = jnp.stack([_dot_nt(xdtb[h // 2], dyh[h % 2][h // 2]) for h in range(SH)])
        dg_t = jnp.sum((dm_t * lam_t).reshape(SG, SR, BLK, BLK), axis=1).astype(BF16)
        xq16 = xdtb.astype(F32)
        xh = [jnp.where(cm.lo, xdt16, 0.0).astype(BF16), jnp.where(cm.lo, 0.0, xdt16).astype(BF16)]
        y_in = jnp.stack([_dot_tn(m_t[2 * k], xh[0][k]) + _dot_tn(m_t[2 * k + 1], xh[1][k]) for k in range(NPAIR)])
        da_diag = dy16 * y_in - xq16 * dxdt
        lane_c = lax.broadcasted_iota(jnp.int32, (BLK, 128), 1)
        ds16b = ds16.astype(BF16)
        sp16b = sp16.astype(BF16)
        dxs = jnp.stack([_dot(bbs[grp(k)], ds16b[k]) for k in range(NPAIR)]) * cm.epl
        dxdt = dxdt + dxs
        dya = (dy16 * cm.eapl).astype(BF16)
        xe = (xdt16 * cm.epl).astype(BF16)
        dcs, dbs = [], []
        for g in range(SG):
            ks = range(g * (NPAIR // SG), (g + 1) * (NPAIR // SG))
            dcs.append(sum(_dot_nt(dya[k], sp16b[k]) for k in ks) + _dot_tn(dg_t[g], bbs[g]))
            dbs.append(sum(_dot_nt(xe[k], ds16b[k]) for k in ks) + _dot(dg_t[g], cbs[g]))
        dst[...] = _unpairs(cm.cdpl * ds16 + jnp.stack([_dot(cts[grp(k)], dya[k]) for k in range(NPAIR)]))
        da_ref[...] = jnp.concatenate([_unpairs(dxdt * cm.dpl + cm.dskpl * dy16)] + dbs + dcs, axis=1)
        y_off = jnp.stack([_dot(cbs[grp(k)], sp16b[k]) for k in range(NPAIR)]) * cm.eapl
        da_cols = _head_sums(da_diag + dy16 * y_off - xdt16 * dxs)
        last = _head_sums(jnp.sum(xdt16 * dxs, axis=1, keepdims=True)
                          + cm.cdpl * jnp.sum(ds16 * sp16, axis=1, keepdims=True))
        ddt = _head_sums(dxdt * xs16)
        row_i = lax.broadcasted_iota(jnp.int32, (BLK, 128), 0)
        dacum = da_cols + jnp.where(row_i == BLK - 1, last, 0.0)
        dda = _tri_mm((ri <= ci).astype(BF16), dacum)
        ddt = ddt + dda * cm.a
        dal_ref[...] += _csum(dda * cm.dtv) * cm.a
        ddt_raw = jnp.where(lane_c < SH, ddt * _sig(dt_raw + dtb), 0.0)
        ddt_ref[...] = ddt_raw.astype(BF16)
        ddtb_ref[...] += _csum(ddt_raw)
        ddk_ref[...] += _head_sums(jnp.sum(dy16 * xs16, axis=1, keepdims=True))

    rev = lambda i: nc - 1 - i
    vec = _full((1, 128))
    slab = pl.BlockSpec((BLK, SSM_W), lambda i: (rev(i), 0))
    return pl.pallas_call(
        body, name="ssd_bwd", grid=(nc,),
        in_specs=[slab,
                  pl.BlockSpec((BLK, SG * SN), lambda i: (rev(i), SSM_W // (SG * SN))),
                  pl.BlockSpec((BLK, SG * SN), lambda i: (rev(i), SSM_W // (SG * SN) + 1)),
                  pl.BlockSpec((BLK, 128), lambda i: (rev(i), 0)),
                  slab,
                  pl.BlockSpec((1, SN, SSM_W), lambda i: (rev(i), 0, 0)), vec, vec, vec],
        out_specs=[pl.BlockSpec((BLK, XBC), lambda i: (rev(i), 0)), pl.BlockSpec((BLK, 128), lambda i: (rev(i), 0)),
                   vec, vec, vec],
        out_shape=[jax.ShapeDtypeStruct((t, XBC), F32), jax.ShapeDtypeStruct((t, 128), BF16),
                   jax.ShapeDtypeStruct((1, 128), F32), jax.ShapeDtypeStruct((1, 128), F32),
                   jax.ShapeDtypeStruct((1, 128), F32)],
        scratch_shapes=[pltpu.VMEM((SN, SSM_W), F32)],
        compiler_params=_cp(("arbitrary",)),
    )(act, act, act, dt_raw, dy, sprev, dtb_p, alog_p, dsk_p)


TAIL_TM = 256


def _dsilu(z, s):
    return s * (1.0 + z * (1.0 - s))


def tail(proj, ao, yss, x, target, gate, ssm_nw, w_at, w_ss, w_ou):
    t = x.shape[0]
    tm = min(t, TAIL_TM)
    gw = SSM_W // SG

    def body(ao_ref, za_ref, ga_ref, gb_ref, zm_ref, ys_ref, x_ref, tg_ref, gt_ref, nw_ref, wa_ref, ws_ref, wo_ref,
             loss_ref, dy_ref, dao_ref, dmid_ref, dys_ref,
             ua_ref, yn_ref, mg_ref, dya_ref, dyb_ref, do_ref, dgt_ref, dnw_ref):
        i = pl.program_id(0)

        @pl.when(i == 0)
        def _():
            loss_ref[...] = jnp.zeros_like(loss_ref)
            dgt_ref[...] = jnp.zeros_like(dgt_ref)
            dnw_ref[...] = jnp.zeros_like(dnw_ref)

        ao = ao_ref[...].astype(F32)
        za = za_ref[...].astype(F32)
        sa = _sig(za)
        sila = za * sa
        ua_f = ao * sila
        ua = ua_f.astype(BF16)
        ya = _dot(ua, wa_ref[...])
        zm = zm_ref[...].astype(F32)
        sm = _sig(zm)
        silm = zm * sm
        ys = ys_ref[...].astype(F32)
        u = ys * silm
        nw = nw_ref[...]
        rs, uns = [], []
        for g in range(SG):
            ug = u[:, gw * g:gw * (g + 1)]
            r = lax.rsqrt(jnp.mean(ug * ug, axis=-1, keepdims=True) + EPS)
            rs.append(r)
            uns.append(ug * r)
        un = jnp.concatenate(uns, axis=1)
        yn_f = un * nw
        yn = yn_f.astype(BF16)
        yb = _dot(yn, ws_ref[...])
        sga = _sig(ga_ref[...].astype(F32))
        sgb = _sig(gb_ref[...].astype(F32))
        mg_f = sga * ya + sgb * yb
        mg = mg_f.astype(BF16)
        o = _dot(mg, wo_ref[...])
        gt = gt_ref[...]
        err = (x_ref[...] + gt * o) - tg_ref[...]
        lane = lax.broadcasted_iota(jnp.int32, (1, 128), 1)
        loss_ref[...] += jnp.where(lane == 0, 0.5 * _asum(_rsum(err * err) / D), 0.0)
        dy = err * (1.0 / D)
        dy_ref[...] = dy
        dgt_ref[...] += _csum(dy * o)
        do = (dy * gt).astype(BF16)
        dmg = _dot_nt(do, wo_ref[...])
        dmid_ref[:, C_GA - C_ZA:C_GB - C_ZA] = (dmg * ya * sga * (1.0 - sga)).astype(BF16)
        dmid_ref[:, C_GB - C_ZA:C_ZM - C_ZA] = (dmg * yb * sgb * (1.0 - sgb)).astype(BF16)
        dya = (dmg * sga).astype(BF16)
        dyb = (dmg * sgb).astype(BF16)
        dua = _dot_nt(dya, wa_ref[...])
        dao_ref[...] = (dua * sila).astype(BF16)
        dmid_ref[:, 0:C_GA - C_ZA] = (dua * ao * _dsilu(za, sa)).astype(BF16)
        dyn = _dot_nt(dyb, ws_ref[...])
        dnw_ref[...] += _csum(dyn * un)
        dun = dyn * nw
        dus = []
        for g in range(SG):
            gs = slice(gw * g, gw * (g + 1))
            dus.append(rs[g] * (dun[:, gs] - uns[g] * jnp.mean(dun[:, gs] * uns[g], axis=-1, keepdims=True)))
        du = jnp.concatenate(dus, axis=1)
        dys_ref[...] = (du * silm).astype(BF16)
        dmid_ref[:, C_ZM - C_ZA:] = (du * ys * _dsilu(zm, sm)).astype(BF16)
        ua_ref[...] = ua_f.T.astype(BF16)
        yn_ref[...] = yn_f.T.astype(BF16)
        mg_ref[...] = mg_f.T.astype(BF16)
        dya_ref[...] = dya
        dyb_ref[...] = dyb
        do_ref[...] = do

    row = lambda w: pl.BlockSpec((tm, w), lambda i: (i, 0))
    pcol = lambda w, c0: pl.BlockSpec((tm, w), lambda i: (i, c0 // w))
    sd = lambda w, dt: jax.ShapeDtypeStruct((t, w), dt)
    colt = lambda w: pl.BlockSpec((w, tm), lambda i: (0, i))
    sdt = lambda w: jax.ShapeDtypeStruct((w, t), BF16)
    return pl.pallas_call(
        body, name="tail", grid=(t // tm,),
        in_specs=[row(D), pcol(D, C_ZA), pcol(D, C_GA), pcol(D, C_GB), pcol(SSM_W, C_ZM), row(SSM_W), row(D), row(D),
                  _full((1, D)), _full((1, SSM_W)), _full((D, D)), _full((SSM_W, D)), _full((D, D))],
        out_specs=[_full((1, 128)), row(D), row(D), row(W_MID), row(SSM_W),
                   colt(D), colt(SSM_W), colt(D), row(D), row(D), row(D), _full((1, D)), _full((1, SSM_W))],
        out_shape=[jax.ShapeDtypeStruct((1, 128), F32), sd(D, F32), sd(D, BF16), sd(W_MID, BF16),
                   sd(SSM_W, BF16), sdt(D), sdt(SSM_W), sdt(D), sd(D, BF16),
                   sd(D, BF16), sd(D, BF16), jax.ShapeDtypeStruct((1, D), F32), jax.ShapeDtypeStruct((1, SSM_W), F32)],
        compiler_params=_cp(("arbitrary",)),
    )(ao, proj, proj, proj, proj, yss, x, target, gate, ssm_nw, w_at, w_ss, w_ou)


DPIECES = ((D, ((D, C_Q),)),
           (W_MID, ((D, C_ZA), (D, C_GA), (D, C_GB), (SSM_W, C_ZM))),
           (XBC, ((XBC, C_XBC),)),
           (512, ((512, C_K),)),
           (128, ((128, C_DT),)))


def dproj_bwd(pieces, wcat, x, dy, norm_w, scale):
    t = x.shape[0]
    tm = min(t, 256)
    nt = t // tm
    wblocks = [blk for _, subs in DPIECES for blk in subs]
    npc, nwb = len(DPIECES), len(wblocks)

    def body(*refs):
        p_refs, w_refs = refs[:npc], refs[npc:npc + nwb]
        x_ref, dy_ref, nw_ref, sc_ref, gx_ref, dnw_ref, dsc_ref, dsh_ref, dwe_ref = refs[npc + nwb:]
        i = pl.program_id(0)

        @pl.when(i == 0)
        def _():
            for ref in (dwe_ref, dsh_ref, dnw_ref, dsc_ref):
                ref[...] = jnp.zeros_like(ref)

        dh, wi = None, 0
        for p_ref, (_, subs) in zip(p_refs, DPIECES):
            loc = 0
            for w, _ in subs:
                part = _dot_nt(p_ref[:, loc:loc + w], w_refs[wi][...])
                dh = part if dh is None else dh + part
                loc += w
                wi += 1
        xv = x_ref[...]
        r = lax.rsqrt(jnp.mean(xv * xv, axis=-1, keepdims=True) + EPS)
        xn = xv * r
        weff = nw_ref[...] * (1.0 + sc_ref[...])
        dxn = dh * weff
        gx_ref[...] = dy_ref[...] + r * (dxn - xn * jnp.mean(dxn * xn, axis=-1, keepdims=True))
        dwe_ref[...] += _csum(dh * xn)
        dsh_ref[...] += _csum(dh)

        @pl.when(i == nt - 1)
        def _():
            dwe = dwe_ref[...]
            dnw_ref[...] = dwe * (1.0 + sc_ref[...])
            dsc_ref[...] = dwe * nw_ref[...]

    vec = pl.BlockSpec((1, D), lambda i: (0, 0))
    row = pl.BlockSpec((tm, D), lambda i: (i, 0))
    return pl.pallas_call(
        body, name="dproj_bwd", grid=(nt,),
        in_specs=[pl.BlockSpec((tm, pw), lambda i: (i, 0)) for pw, _ in DPIECES]
        + [pl.BlockSpec((D, w), functools.partial(lambda i, b: (0, b), b=off // w), pipeline_mode=pl.Buffered(1))
           for w, off in wblocks]
        + [row, row, vec, vec],
        out_specs=[row, vec, vec, vec],
        out_shape=[jax.ShapeDtypeStruct((t, D), F32), jax.ShapeDtypeStruct((1, D), F32),
                   jax.ShapeDtypeStruct((1, D), F32), jax.ShapeDtypeStruct((1, D), F32)],
        scratch_shapes=[pltpu.VMEM((1, D), F32)],
        compiler_params=_cp(("arbitrary",)),
    )(*pieces, *([wcat] * nwb), x, dy, norm_w, scale)


def wgrad(at, b, name, bn, after):
    m, t = at.shape
    n = b.shape[1]
    tk = min(t, 2048)
    bm = min(m, 1024)

    def body(a_ref, b_ref, after_ref, o_ref):
        part = _dot(a_ref[...], b_ref[...])

        @pl.when(pl.program_id(2) == 0)
        def _():
            o_ref[...] = part

        @pl.when(pl.program_id(2) > 0)
        def _():
            o_ref[...] += part

    return pl.pallas_call(
        body, name=name, grid=(m // bm, n // bn, t // tk),
        in_specs=[pl.BlockSpec((bm, tk), lambda i, j, k: (i, k)), pl.BlockSpec((tk, bn), lambda i, j, k: (k, j)), ANY],
        out_specs=pl.BlockSpec((bm, bn), lambda i, j, k: (i, j)),
        out_shape=jax.ShapeDtypeStruct((m, n), F32),
        compiler_params=_cp(("parallel", "parallel", "arbitrary")),
    )(at, b, after)


SUM_TR = 256


def pair_sum(g, core, theirs, name):
    w = g.shape[2]
    nh = HROWS // SUM_TR

    def body(core_ref, a_ref, b_ref, o_ref, ob_ref):
        s = a_ref[...] + b_ref[...]
        o_ref[...] = s
        ob_ref[...] = s.astype(BF16)

    spec = pl.BlockSpec((1, SUM_TR, w), lambda d, i, c: (d, i, 0))
    return pl.pallas_call(
        body, name=name,
        out_shape=[jax.ShapeDtypeStruct((4, HROWS, w), F32), jax.ShapeDtypeStruct((4, HROWS, w), BF16)],
        grid_spec=pltpu.PrefetchScalarGridSpec(
            num_scalar_prefetch=1, grid=(4, nh),
            in_specs=[pl.BlockSpec((1, SUM_TR, w), lambda d, i, c: (d, c[0] * nh + i, 0)), spec],
            out_specs=[spec, spec]),
        compiler_params=_cp(("parallel", "parallel")))(core.reshape(1).astype(jnp.int32), g, theirs)


def chip_sum(part, chip, others, name):
    r, w = part.shape[1:]

    def body(chip_ref, a_ref, b_ref, o_ref):
        acc = a_ref[0]
        for k in range(3):
            acc = acc + b_ref[k].astype(F32)
        o_ref[...] = acc

    return pl.pallas_call(
        body, name=name, out_shape=jax.ShapeDtypeStruct((r, w), F32),
        grid_spec=pltpu.PrefetchScalarGridSpec(
            num_scalar_prefetch=1, grid=(r // SUM_TR,),
            in_specs=[pl.BlockSpec((1, SUM_TR, w), lambda i, c: (c[0], i, 0)),
                      pl.BlockSpec((3, SUM_TR, w), lambda i, c: (0, i, 0))],
            out_specs=pl.BlockSpec((SUM_TR, w), lambda i, c: (i, 0))),
        compiler_params=_cp(("parallel",)))(chip.reshape(1).astype(jnp.int32), part, others)


def sum_devices(g):
    r = g.shape[1]

    def body(g_ref, o_ref):
        acc = g_ref[0]
        for d in range(1, 8):
            acc = acc + g_ref[d]
        o_ref[...] = acc

    return pl.pallas_call(body, name="sum_devices", out_shape=jax.ShapeDtypeStruct((r, 1024), F32),
                          compiler_params=_cp())(g)


def adamw(w, g, m, v, name):
    r, c = w.shape
    tr = r
    for cand in (256, 128, 64, 32, 16, 8):
        if r % cand == 0 and r > cand:
            tr = cand
            break

    def body(w_ref, g_ref, m_ref, v_ref, d_ref, nm_ref, nv_ref):
        gv = g_ref[...]
        mn = ADAM_B1 * m_ref[...] + (1.0 - ADAM_B1) * gv
        vn = ADAM_B2 * v_ref[...] + (1.0 - ADAM_B2) * (gv * gv)
        m_hat = mn / (1.0 - ADAM_B1 ** ADAM_STEP)
        v_hat = vn / (1.0 - ADAM_B2 ** ADAM_STEP)
        d_ref[...] = -ADAM_LR * (m_hat / (jnp.sqrt(v_hat) + ADAM_EPS) + ADAM_WD * w_ref[...])
        nm_ref[...] = mn
        nv_ref[...] = vn

    spec = pl.BlockSpec((tr, c), lambda i: (i, 0))
    sd = jax.ShapeDtypeStruct((r, c), F32)
    return pl.pallas_call(body, name=name, grid=(r // tr,), in_specs=[spec] * 4, out_specs=[spec] * 3,
                          out_shape=[sd, sd, sd], compiler_params=_cp(("parallel",)))(w, g, m, v)


def adamw_halves(w, mine, theirs, core, m, v, name):
    r, c = w.shape
    tr = 128
    nh = HROWS // tr

    def body(core_ref, w_ref, a_ref, b_ref, m_ref, v_ref, g_ref, d_ref, nm_ref, nv_ref):
        gv = jnp.where(pl.program_id(0) // nh == core_ref[0], a_ref[...], b_ref[...])
        mn = ADAM_B1 * m_ref[...] + (1.0 - ADAM_B1) * gv
        vn = ADAM_B2 * v_ref[...] + (1.0 - ADAM_B2) * (gv * gv)
        m_hat = mn / (1.0 - ADAM_B1 ** ADAM_STEP)
        v_hat = vn / (1.0 - ADAM_B2 ** ADAM_STEP)
        g_ref[...] = gv
        d_ref[...] = -ADAM_LR * (m_hat / (jnp.sqrt(v_hat) + ADAM_EPS) + ADAM_WD * w_ref[...])
        nm_ref[...] = mn
        nv_ref[...] = vn

    spec = pl.BlockSpec((tr, c), lambda i, s: (i, 0))
    half = pl.BlockSpec((tr, c), lambda i, s: (i % nh, 0))
    sd = jax.ShapeDtypeStruct((r, c), F32)
    return pl.pallas_call(
        body, name=name, out_shape=[sd, sd, sd, sd],
        grid_spec=pltpu.PrefetchScalarGridSpec(num_scalar_prefetch=1, grid=(r // tr,),
                                               in_specs=[spec, half, half, spec, spec], out_specs=[spec] * 4),
        compiler_params=_cp(("parallel",)))(core.reshape(1).astype(jnp.int32), w, mine, theirs, m, v)


ANY = pl.BlockSpec(memory_space=pl.ANY)
VM = pl.BlockSpec(memory_space=pltpu.VMEM)
OTHER_CHIPS = ((1, 0), (0, 1), (1, 1))


def _pos():
    return lax.axis_index("x"), lax.axis_index("y"), lax.axis_index("c")


def _flip(v, bit):
    return 1 - v if bit else v


def _rcopy(src, dst, ssem, rsem, peer):
    return pltpu.make_async_remote_copy(src_ref=src, dst_ref=dst, send_sem=ssem, recv_sem=rsem,
                                        device_id=peer, device_id_type=MESH)


def allgather_small(p, name):
    r = p.shape[0]

    def body(in_ref, out_ref, ssem, rsem, lsem):
        x, y, c = _pos()
        me = 4 * x + 2 * y + c
        loc = pltpu.make_async_copy(in_ref, out_ref.at[me], lsem)
        loc.start()
        sends = []
        peers = []
        for k in range(1, 8):
            px, py, pc = _flip(x, (k >> 2) & 1), _flip(y, (k >> 1) & 1), _flip(c, k & 1)
            peers.append((px, py, pc))
            cp = _rcopy(in_ref, out_ref.at[me], ssem.at[k - 1], rsem.at[k - 1], (px, py, pc))
            cp.start()
            sends.append(cp)
        for k in range(1, 8):
            px, py, pc = peers[k - 1]
            _rcopy(in_ref, out_ref.at[4 * px + 2 * py + pc], ssem.at[k - 1], rsem.at[k - 1], (px, py, pc)).wait_recv()
        for cp in sends:
            cp.wait_send()
        loc.wait()

    return pl.pallas_call(
        body, name=name, out_shape=jax.ShapeDtypeStruct((8, r, 1024), F32),
        in_specs=[VM], out_specs=VM,
        scratch_shapes=[pltpu.SemaphoreType.DMA((7,)), pltpu.SemaphoreType.DMA((7,)), pltpu.SemaphoreType.DMA],
    )(p)


def gather_weights(w_in_b, mod_sh):
    def body(wi_ref, m_ref, gi_ref, mo_ref, ssem, rsem, lsem):
        x, y, c = _pos()
        chip = 2 * x + y
        mine = pl.ds(pl.multiple_of(c * HROWS, 16), HROWS)
        other = pl.ds(pl.multiple_of((1 - c) * HROWS, 16), HROWS)
        sib = (x, y, 1 - c)
        pairs = ((wi_ref, gi_ref),)
        loc_m = pltpu.make_async_copy(m_ref, mo_ref.at[chip], lsem)
        loc_m.start()
        sends = []
        for k, (fx, fy) in enumerate(OTHER_CHIPS):
            peer = (_flip(x, fx), _flip(y, fy), c)
            for a, (w_ref, g_ref) in enumerate(pairs):
                cw = _rcopy(w_ref.at[mine], g_ref.at[chip, mine], ssem.at[6 * a + k], rsem.at[6 * a + k], peer)
                cw.start()
                sends.append(cw)
            cm = _rcopy(m_ref, mo_ref.at[chip], ssem.at[12 + k], rsem.at[12 + k], peer)
            cm.start()
            sends.append(cm)
        for k, (fx, fy) in enumerate(OTHER_CHIPS):
            px, py = _flip(x, fx), _flip(y, fy)
            for a, (w_ref, g_ref) in enumerate(pairs):
                got = g_ref.at[2 * px + py, mine]
                _rcopy(w_ref.at[mine], got, ssem.at[6 * a + k], rsem.at[6 * a + k], (px, py, c)).wait_recv()
                fw = _rcopy(got, got, ssem.at[6 * a + 3 + k], rsem.at[6 * a + 3 + k], sib)
                fw.start()
                sends.append(fw)
        for k, (fx, fy) in enumerate(OTHER_CHIPS):
            px, py = _flip(x, fx), _flip(y, fy)
            for a, (w_ref, g_ref) in enumerate(pairs):
                land = g_ref.at[2 * px + py, other]
                _rcopy(land, land, ssem.at[6 * a + 3 + k], rsem.at[6 * a + 3 + k], sib).wait_recv()
            _rcopy(m_ref, mo_ref.at[2 * px + py], ssem.at[12 + k], rsem.at[12 + k], (px, py, c)).wait_recv()
        for cp in sends:
            cp.wait_send()
        loc_m.wait()

    return pl.pallas_call(
        body, name="gather_weights",
        out_shape=[jax.ShapeDtypeStruct((4, D, SH_IN), BF16), jax.ShapeDtypeStruct((4, 8, 768), F32)],
        in_specs=[ANY, VM], out_specs=[ANY, VM],
        scratch_shapes=[pltpu.SemaphoreType.DMA((15,)), pltpu.SemaphoreType.DMA((15,)), pltpu.SemaphoreType.DMA],
    )(w_in_b, mod_sh)


def pair_exchange(g):
    def body(g_ref, r_ref, ssem, rsem):
        x, y, c = _pos()
        other = pl.ds(pl.multiple_of((1 - c) * HROWS, 8), HROWS)
        cp = _rcopy(g_ref.at[:, other, :], r_ref, ssem, rsem, (x, y, 1 - c))
        cp.start()
        cp.wait()

    return pl.pallas_call(
        body, name="pair_exchange", out_shape=jax.ShapeDtypeStruct((4, HROWS, g.shape[2]), F32),
        in_specs=[ANY], out_specs=ANY,
        scratch_shapes=[pltpu.SemaphoreType.DMA, pltpu.SemaphoreType.DMA],
    )(g)


HBM = pl.BlockSpec(memory_space=pltpu.HBM)
SEM = pl.BlockSpec(memory_space=pltpu.SEMAPHORE)
DATAFLOW = pltpu.SideEffectType.DATAFLOW_SIDE_EFFECTING


def split_start(name, make_copies, srcs, lands, nsem, after):
    arrays = [*srcs, *lands]
    n, ns = len(arrays), len(srcs)

    def body(*refs):
        for cp in make_copies(refs[:ns], refs[ns:n], refs[n + 1], refs[n + 2])[0]:
            cp.start()
        refs[-1][...] = jnp.zeros_like(refs[-1])

    res = pl.pallas_call(
        body, name=name,
        out_shape=(pltpu.SemaphoreType.DMA((nsem,)), pltpu.SemaphoreType.DMA((nsem,)),
                   *[pltpu.HBM(a.shape, a.dtype) for a in arrays], jax.ShapeDtypeStruct((8, 128), F32)),
        in_specs=(HBM,) * n + (ANY,), out_specs=(SEM, SEM) + (HBM,) * n + (VM,),
        input_output_aliases={i: 2 + i for i in range(n)},
        compiler_params=pltpu.CompilerParams(has_side_effects=DATAFLOW),
    )(*[pltpu.with_memory_space_constraint(a, pltpu.HBM) for a in arrays], after)
    return res[0], res[1], list(res[2:2 + n]), res[-1]


def split_wait(name, make_copies, ssem, rsem, arrays, ns, after):
    n = len(arrays)

    def body(*refs):
        sends, recvs = make_copies(refs[:ns], refs[ns:n], refs[n], refs[n + 1])
        for cp in sends:
            cp.wait_send()
        for cp in recvs:
            cp.wait_recv()

    return pl.pallas_call(
        body, name=name, out_shape=tuple(pltpu.HBM(a.shape, a.dtype) for a in arrays),
        in_specs=(HBM,) * n + (SEM, SEM, ANY), out_specs=(HBM,) * n,
        input_output_aliases={i: i for i in range(n)},
        compiler_params=pltpu.CompilerParams(has_side_effects=DATAFLOW),
    )(*arrays, ssem, rsem, after)


def _chip_copies(srcs, lands, ssem, rsem):
    x, y, c = _pos()
    copies = []
    for k, (fx, fy) in enumerate(OTHER_CHIPS):
        px, py = _flip(x, fx), _flip(y, fy)
        for a, (p_ref, l_ref) in enumerate(zip(srcs, lands)):
            copies.append(_rcopy(p_ref.at[2 * px + py], l_ref.at[k], ssem.at[3 * a + k], rsem.at[3 * a + k], (px, py, c)))
    return copies, copies


def _pair_copies(srcs, lands, ssem, rsem):
    x, y, c = _pos()
    other = pl.ds(pl.multiple_of((1 - c) * HROWS, 8), HROWS)
    copies = [_rcopy(srcs[0].at[:, other, :], lands[0], ssem.at[0], rsem.at[0], (x, y, 1 - c))]
    return copies, copies


def _rest_copies(srcs, lands, ssem, rsem):
    x, y, c = _pos()
    chip = 2 * x + y
    mine = pl.ds(pl.multiple_of(c * HROWS, 16), HROWS)
    sends, recvs = [], []
    for k, (fx, fy) in enumerate(OTHER_CHIPS):
        px, py = _flip(x, fx), _flip(y, fy)
        for t in range(2):
            rows_t = pl.ds(t * HROWS, HROWS)
            sends.append(_rcopy(srcs[0].at[mine], lands[0].at[chip, mine], ssem.at[2 * k + t], rsem.at[2 * k + c],
                                (px, py, t)))
            recvs.append(_rcopy(srcs[0].at[rows_t], lands[0].at[2 * px + py, rows_t], ssem.at[2 * k + t],
                                rsem.at[2 * k + t], (px, py, t)))
    return sends, recvs


def _swap_copies(srcs, lands, ssem, rsem):
    x, y, c = _pos()
    copies = [_rcopy(s_ref, l_ref, ssem.at[a], rsem.at[a], (x, y, 1 - c))
              for a, (s_ref, l_ref) in enumerate(zip(srcs, lands))]
    return copies, copies


def _flat(v, width=1024):
    v = v.reshape(-1)
    n = -(-v.shape[0] // width) * width
    return jnp.pad(v, (0, n - v.shape[0]))


def _rows(parts, rows):
    flat = jnp.concatenate(parts)
    return jnp.pad(flat, (0, rows * 1024 - flat.shape[0])).reshape(rows, 1024)


def _pack_small(b_ada, norm_w, conv_b, ssm_norm_w, q_norm_w, k_norm_w, sinks, dt_bias, a_log, d_skip, rel_bias,
                extra=None, tail=(), rows=16):
    misc = [q_norm_w, k_norm_w, sinks, dt_bias, a_log, d_skip] + ([] if extra is None else [extra])
    parts = [_flat(b_ada), _flat(norm_w), _flat(conv_b), _flat(ssm_norm_w)] + [_flat(v, 128) for v in misc]
    parts.append(jnp.zeros(((8 - len(misc)) * 128,), F32))
    parts.append(_flat(rel_bias))
    parts.append(jnp.zeros((5 * 1024,), F32))
    return _rows(parts + [_flat(v) for v in tail], rows)


def _unpack_small(p):
    misc = p[9]
    return dict(b_ada=p[0:3].reshape(1, 3072), norm_w=p[3:4], conv_b=p[4:7].reshape(1, 3072),
                ssm_norm_w=p[7:9].reshape(1, 2048), q_norm_w=misc[None, 0:64], k_norm_w=misc[None, 128:192],
                sinks=misc[None, 256:272], dt_bias=misc[None, 384:416], a_log=misc[None, 512:544],
                d_skip=misc[None, 640:672], rel_bias=p[10, :512].reshape(32, 16), extra=misc[768])


SMALL = ("b_ada", "norm_w", "conv_b", "ssm_norm_w", "q_norm_w", "k_norm_w", "sinks", "dt_bias", "a_log", "d_skip",
         "rel_bias")
WEIGHTS = ("w_ada", "b_ada", "norm_w", "w_in", "q_norm_w", "k_norm_w", "rel_bias", "sinks", "conv_w", "conv_b",
           "dt_bias", "a_log", "d_skip", "ssm_norm_w", "w_attn_proj", "w_ssm_proj", "w_out")
IN_COLS = ((0, 1024, C_Q), (1024, 256, C_K), (1280, 256, C_V), (1536, 1024, C_ZA), (2560, 2048, C_ZM),
           (4608, 3072, C_XBC), (7680, 32, C_DT), (7712, 1024, C_GA), (8736, 1024, C_GB))


def _to_cat(shards):
    parts, pos = [], 0
    for o, n, cnew in sorted(IN_COLS, key=lambda e: e[2]):
        assert cnew == pos
        c0 = o
        while c0 < o + n:
            i = c0 // SH_IN
            c1 = min(o + n, (i + 1) * SH_IN)
            parts.append(shards[i][:, c0 - i * SH_IN:c1 - i * SH_IN])
            c0 = c1
        pos += n
    parts.append(jnp.zeros((D, NP - pos), shards.dtype))
    return jnp.concatenate(parts, axis=1)


def _from_cat(dw_pieces):
    starts = [subs[0][1] for _, subs in DPIECES]

    def cols(c0, c1):
        p = max(q for q in range(len(starts)) if starts[q] <= c0)
        return dw_pieces[p][:, c0 - starts[p]:c1 - starts[p]]

    shards = []
    for i in range(4):
        lo, hi = i * SH_IN, (i + 1) * SH_IN
        parts = []
        for o, n, cnew in IN_COLS:
            a, b = max(o, lo), min(o + n, hi)
            if a < b:
                parts.append(cols(cnew + a - o, cnew + b - o))
        shards.append(jnp.concatenate(parts, axis=1))
    return jnp.stack(shards)


def kernel(x, c, w_ada, b_ada, norm_w, w_in, q_norm_w, k_norm_w, rel_bias, sinks, conv_w, conv_b, dt_bias, a_log, d_skip, ssm_norm_w, w_attn_proj, w_ssm_proj, w_out, loss_target, m_w_ada, m_b_ada, m_norm_w, m_w_in, m_q_norm_w, m_k_norm_w, m_rel_bias, m_sinks, m_conv_w, m_conv_b, m_dt_bias, m_a_log, m_d_skip, m_ssm_norm_w, m_w_attn_proj, m_w_ssm_proj, m_w_out, v_w_ada, v_b_ada, v_norm_w, v_w_in, v_q_norm_w, v_k_norm_w, v_rel_bias, v_sinks, v_conv_w, v_conv_b, v_dt_bias, v_a_log, v_d_skip, v_ssm_norm_w, v_w_attn_proj, v_w_ssm_proj, v_w_out):
    args = dict(locals())
    xi, yi, ci = lax.axis_index("x"), lax.axis_index("y"), lax.axis_index("c")
    chip = 2 * xi + yi
    me = 4 * xi + 2 * yi + ci
    x2 = x[0]
    tgt = loss_target[0]

    pay = _rows([c.reshape(-1), conv_w[0].reshape(-1)], 8)
    g0 = allgather_small(pay, "gather_cond")
    c_all = g0[:, 0, :]
    conv_w_full = g0[0::2, 1:4, :].reshape(4, CONV_K, 768).transpose(1, 0, 2).reshape(CONV_K, XBC)

    b_ada_sh = lax.dynamic_slice(b_ada, (0, chip * 768), (1, 768))
    mod_sh = ada_mod(c_all, w_ada[0], b_ada_sh)

    w_in_b = w_in[0].astype(BF16)
    w_rest_b = jnp.concatenate([w_attn_proj[0], w_ssm_proj[0], w_out[0]], axis=0).astype(BF16)
    wg_in, modg = gather_weights(w_in_b, mod_sh)
    wg_in = lax.dynamic_update_slice(wg_in, w_in_b[None], (chip, 0, 0))
    rs_sem, rr_sem, rest_thru, rest_tok = split_start("gather_rest_start", _rest_copies, [w_rest_b],
                                                      [lax.empty((4, D, D), BF16)], 6, modg)
    mod = lax.dynamic_slice(modg, (0, me, 0), (4, 1, 768)).reshape(1, 3 * D)
    shift, scale, gate = mod[:, :D], mod[:, D:2 * D] + rest_tok[:1, :1], mod[:, 2 * D:]
    wcat = _to_cat(wg_in)

    pad128 = lambda v: jnp.pad(v, ((0, 0), (0, 128 - v.shape[1])))
    dtb_p, alog_p, dsk_p = pad128(dt_bias), pad128(a_log), pad128(d_skip)
    bucket = _bucket_table()

    proj, dt_raw, h_t = norm_proj(x2, norm_w, scale, shift, wcat)
    biasm = bias_expand(rel_bias, sinks, bucket)
    ao = attn_fwd(proj, biasm, q_norm_w, k_norm_w)
    act, dsl = conv_fwd(proj, conv_w_full, conv_b)
    yss, sprev = ssd_fwd(act, dt_raw, dtb_p, alog_p, dsk_p)

    w_rest_b, wg_rest = split_wait("gather_rest_wait", _rest_copies, rs_sem, rr_sem, rest_thru, 1, yss)
    wg_rest = lax.dynamic_update_slice(wg_rest, w_rest_b[None], (chip, 0, 0))
    w_at = wg_rest[:, :R_AT].reshape(D, D)
    w_ss = wg_rest[:, R_AT:R_AT + R_SS].reshape(SSM_W, D)
    w_ou = wg_rest[:, R_AT + R_SS:].reshape(D, D)
    (loss_p, dy, dao, dmid, dyss, ua_t, yn_t, mg_t, dya, dyb, dout, dgate, dssm_nw) = tail(
        proj, ao, yss, x2, tgt, gate, ssm_norm_w, w_at, w_ss, w_ou)

    dq, dkv, dqw, dkw, dacc = attn_bwd(proj, dao, biasm, q_norm_w, k_norm_w)
    dbias = bias_reduce(dacc, bucket)
    drb = dbias[:, :NBUCKET].T
    dsk = dbias[:, NBUCKET].reshape(1, HQ)
    dact, ddt, ddtb, dalog, ddskip = ssd_bwd(act, dt_raw, dyss, sprev, dtb_p, alog_p, dsk_p)
    dxbc, dconv_w, dconv_b = conv_bwd(proj, dact, dsl, conv_w_full)

    dproj = (dq, dmid, dxbc, dkv, ddt)
    dwcat = [wgrad(h_t, piece, "dw_in_%d" % p, 1280 if piece.shape[1] == W_MID else min(piece.shape[1], 1024), rest_tok)
             for p, piece in enumerate(dproj)]

    g_in = _from_cat(dwcat)
    ps_sem, pr_sem, pair_thru, pair_tok = split_start("pair_in_start", _pair_copies, [g_in],
                                                      [lax.empty((4, HROWS, SH_IN), F32)], 1, loss_p)
    dw_at = wgrad(ua_t, dya, "dw_attn", 1024, pair_tok)
    dw_ss = wgrad(yn_t, dyb, "dw_ssm", 1024, pair_tok)
    dw_ou = wgrad(mg_t, dout, "dw_out", 1024, pair_tok)
    g_rest = jnp.concatenate([dw_at.reshape(4, R_AT, D), dw_ss.reshape(4, R_SS, D), dw_ou.reshape(4, R_OU, D)], axis=1)
    sib_rest = pair_exchange(g_rest)
    g_in, sib_in = split_wait("pair_in_wait", _pair_copies, ps_sem, pr_sem, pair_thru, 1, sib_rest)
    part_in, pb_in = pair_sum(g_in, ci, sib_in, "pair_sum_in")
    part_rest, pb_rest = pair_sum(g_rest, ci, sib_rest, "pair_sum_rest")
    cs_sem, cr_sem, chip_thru, token = split_start(
        "chip_exchange_start", _chip_copies, [pb_in, pb_rest],
        [lax.empty((3, HROWS, SH_IN), BF16), lax.empty((3, HROWS, D), BF16)], 6, part_rest)
    grad_x, dnorm_w, dscale, dshift = dproj_bwd(dproj, wcat, x2, dy, norm_w, scale + token[:1, :1])
    _, _, oth_in, oth_rest = split_wait("chip_exchange_wait", _chip_copies, cs_sem, cr_sem, chip_thru, 2, dshift)
    red_in = chip_sum(part_in, chip, oth_in, "chip_sum_in")
    red_rest = chip_sum(part_rest, chip, oth_rest, "chip_sum_rest")
    sw_ssem, sw_rsem, swap_thru, swap_tok = split_start(
        "pair_swap_start", _swap_copies, [red_in, red_rest],
        [lax.empty((HROWS, SH_IN), F32), lax.empty((HROWS, D), F32)], 2, red_rest)

    dmod = jnp.concatenate([dshift, dscale, dgate], axis=1)
    gsmall = _pack_small(dmod, dnorm_w, dconv_b, dssm_nw, dqw, dkw, dsk[:, :HQ], ddtb[:, :SH], dalog[:, :SH],
                         ddskip[:, :SH], drb, extra=loss_p[:, :1] + swap_tok[:1, :1], tail=(dconv_w,), rows=32)
    gall = allgather_small(gsmall, "gather_small_grads")
    ssum = sum_devices(gall)
    gs = _unpack_small(ssum[:16])
    loss = gs["extra"]
    dconv_w_sh = lax.dynamic_slice(ssum[16:28].reshape(CONV_K, XBC), (0, chip * 768), (CONV_K, 768))
    dmod_all = gall[:, 0:3, :].reshape(8, 3 * D)
    dw_ada = ada_grad(c_all, lax.dynamic_slice(dmod_all, (0, chip * 768), (8, 768)))

    grads = dict(gs)
    grads["w_ada"] = dw_ada
    grads["conv_w"] = dconv_w_sh

    delta, new_m, new_v = {}, {}, {}

    def step(n):
        delta[n], new_m[n], new_v[n] = adamw(args[n][0], grads[n], args["m_" + n][0], args["v_" + n][0], "adamw_" + n)

    step("w_ada")
    step("conv_w")
    ws = _pack_small(*[args[n] for n in SMALL])
    ms = _pack_small(*[args["m_" + n] for n in SMALL])
    vs = _pack_small(*[args["v_" + n] for n in SMALL])
    d_s, m_s, v_s = adamw(ws, ssum[:16], ms, vs, "adamw_small")
    red_in, red_rest, recv_in, recv_rest = split_wait("pair_swap_wait", _swap_copies, sw_ssem, sw_rsem, swap_thru, 2, d_s)
    d_s, m_s, v_s = _unpack_small(d_s), _unpack_small(m_s), _unpack_small(v_s)
    for n in SMALL:
        delta[n], new_m[n], new_v[n] = d_s[n], m_s[n], v_s[n]
    grads["w_in"], delta["w_in"], new_m["w_in"], new_v["w_in"] = adamw_halves(
        w_in[0], red_in, recv_in, ci, m_w_in[0], v_w_in[0], "adamw_w_in")
    g_shard_rest = jnp.concatenate([jnp.where(ci == 0, red_rest, recv_rest), jnp.where(ci == 0, recv_rest, red_rest)],
                                   axis=0)
    grads["w_attn_proj"] = g_shard_rest[:R_AT]
    grads["w_ssm_proj"] = g_shard_rest[R_AT:R_AT + R_SS]
    grads["w_out"] = g_shard_rest[R_AT + R_SS:]
    for n in ("w_attn_proj", "w_ssm_proj", "w_out"):
        step(n)

    def shaped(n, a):
        return a.reshape(args[n].shape)

    outs = [loss, grad_x[None]]
    for table in (grads, delta, new_m, new_v):
        outs += [shaped(n, table[n]) for n in WEIGHTS]
    return tuple(outs)
```

```python
import functools
import math

import numpy as np
import jax
import jax.numpy as jnp
from jax import lax
from jax.experimental import pallas as pl
from jax.experimental.pallas import tpu as pltpu

F32 = jnp.float32
BF16 = jnp.bfloat16
MESH = pl.DeviceIdType.MESH

D = 1024
HQ, HKV, GRP, DH = 16, 4, 4, 64
BLK = 128
NBUCKET, MAXDIST = 32, 128
SSM_W, SH, SG, SR, SP, SN = 2048, 32, 4, 8, 64, 128
CONV_K = 4
XBC = SSM_W + 2 * SG * SN
IN_W = 9760
EPS = 1e-6
NEG = -1e30
SCALE = DH ** -0.5

C_Q, C_ZA, C_GA, C_GB, C_ZM, C_XBC, C_K, C_V, C_DT = 0, 1024, 2048, 3072, 4096, 6144, 9216, 9472, 9728
NP = 9984
TN = 3328
W_MID = C_XBC - C_ZA

SH_IN = IN_W // 4
R_AT, R_SS, R_OU = 256, 512, 256
HROWS = D // 2

ADAM_LR, ADAM_B1, ADAM_B2, ADAM_EPS, ADAM_WD, ADAM_STEP = 0.001, 0.9, 0.999, 1e-08, 0.01, 10

VMEM_LIMIT = 56 * 1024 * 1024


def _cp(sem=None):
    if sem is None:
        return pltpu.CompilerParams(vmem_limit_bytes=VMEM_LIMIT)
    return pltpu.CompilerParams(dimension_semantics=sem, vmem_limit_bytes=VMEM_LIMIT)


def _sig(x):
    return 0.5 * jnp.tanh(0.5 * x) + 0.5


def _dot(a, b):
    return jnp.dot(a, b, preferred_element_type=F32)


def _dot_nt(a, b):
    return lax.dot_general(a, b, (((1,), (1,)), ((), ())), preferred_element_type=F32)


def _dot_tn(a, b):
    return lax.dot_general(a, b, (((0,), (0,)), ((), ())), preferred_element_type=F32)


def _rsum(x):
    return jnp.sum(x, axis=-1, keepdims=True)


def _csum(x):
    return jnp.sum(x, axis=0, keepdims=True)


def _asum(x):
    return _csum(_rsum(x))


def _full(shape):
    nd = len(shape)
    return pl.BlockSpec(shape, lambda *_: (0,) * nd)


def ada_mod(c_all, w_ada_sh, b_ada_sh):
    def body(c_ref, w_ref, b_ref, o_ref):
        cv = c_ref[...]
        s = cv * _sig(cv)
        o_ref[...] = jnp.dot(s, w_ref[...], preferred_element_type=F32,
                             precision=lax.Precision.HIGHEST) + b_ref[...]

    n = w_ada_sh.shape[1]
    return pl.pallas_call(body, name="ada_mod", out_shape=jax.ShapeDtypeStruct((8, n), F32),
                          compiler_params=_cp())(c_all, w_ada_sh, b_ada_sh)


def ada_grad(c_all, dmod_sh):
    def body(c_ref, d_ref, o_ref):
        cv = c_ref[...]
        s = cv * _sig(cv)
        o_ref[...] = lax.dot_general(s, d_ref[...], (((0,), (0,)), ((), ())), preferred_element_type=F32,
                                     precision=lax.Precision.HIGHEST)

    n = dmod_sh.shape[1]
    return pl.pallas_call(body, name="ada_grad", out_shape=jax.ShapeDtypeStruct((D, n), F32),
                          compiler_params=_cp())(c_all, dmod_sh)


def norm_proj(x, norm_w, scale, shift, wcat):
    t = x.shape[0]
    tm = min(t, 1024)

    def body(x_ref, nw_ref, sc_ref, sh_ref, w_ref, p_ref, dt_ref, ht_ref, hs):
        @pl.when(pl.program_id(1) == 0)
        def _():
            xv = x_ref[...]
            r = lax.rsqrt(jnp.mean(xv * xv, axis=-1, keepdims=True) + EPS)
            h = (xv * r) * nw_ref[...]
            h = h * (1.0 + sc_ref[...]) + sh_ref[...]
            hs[...] = h.astype(BF16)
            ht_ref[...] = h.T.astype(BF16)

        p = _dot(hs[...], w_ref[...])
        p_ref[...] = p.astype(BF16)

        @pl.when(pl.program_id(1) == C_DT // TN)
        def _():
            dt_ref[...] = p[:, C_DT % TN:C_DT % TN + 128]

    vec = pl.BlockSpec((1, D), lambda i, j: (0, 0))
    return pl.pallas_call(
        body, name="norm_proj", grid=(t // tm, NP // TN),
        in_specs=[pl.BlockSpec((tm, D), lambda i, j: (i, 0)), vec, vec, vec,
                  pl.BlockSpec((D, TN), lambda i, j: (0, j))],
        out_specs=[pl.BlockSpec((tm, TN), lambda i, j: (i, j)), pl.BlockSpec((tm, 128), lambda i, j: (i, 0)),
                   pl.BlockSpec((D, tm), lambda i, j: (0, i))],
        out_shape=[jax.ShapeDtypeStruct((t, NP), BF16), jax.ShapeDtypeStruct((t, 128), F32),
                   jax.ShapeDtypeStruct((D, t), BF16)],
        scratch_shapes=[pltpu.VMEM((tm, D), BF16)],
        compiler_params=_cp(("parallel", "arbitrary")),
    )(x, norm_w, scale, shift, wcat)


def _bucket_table():
    qi = np.arange(BLK)[:, None]
    kj = np.arange(2 * BLK)[None, :]
    dist = qi + BLK - kj
    n = np.maximum(dist, 0)
    max_exact = NBUCKET // 2
    nf = np.maximum(n, 1).astype(np.float32)
    large = max_exact + (np.log(nf / np.float32(max_exact)) / np.float32(math.log(MAXDIST / max_exact))
                         * np.float32(NBUCKET - max_exact)).astype(np.int32)
    large = np.minimum(large, NBUCKET - 1)
    bucket = np.where(n < max_exact, n, large).astype(np.int32)
    valid = (dist >= 0) & (dist < BLK)
    return np.where(valid, bucket, -1).astype(np.int32)


def bias_expand(rel_bias, sinks, bucket):
    def body(rb_ref, sk_ref, bk_ref, o_ref):
        bk = bk_ref[...]
        col = lax.broadcasted_iota(jnp.int32, (BLK, 2 * BLK), 1)

        def head(hd, carry):
            def step(b, acc):
                return jnp.where(bk == b, rb_ref[b, hd], acc)

            acc = lax.fori_loop(0, NBUCKET, step, jnp.full((BLK, 2 * BLK), NEG, F32))
            acc = jnp.where(col == 0, sk_ref[0, hd], acc)
            o_ref[1, hd] = acc
            o_ref[0, hd] = jnp.where(jnp.logical_and(col > 0, col < BLK), NEG, acc)
            return carry

        lax.fori_loop(0, HQ, head, 0)

    smem = pl.BlockSpec(memory_space=pltpu.SMEM)
    return pl.pallas_call(
        body, name="bias_expand", in_specs=[smem, smem, VM], out_specs=VM,
        out_shape=jax.ShapeDtypeStruct((2, HQ, BLK, 2 * BLK), F32), compiler_params=_cp(),
    )(rel_bias, sinks, jnp.asarray(bucket))


def bias_reduce(dacc, bucket):
    col = np.arange(BLK * 2 * BLK) % (2 * BLK)
    lane = np.arange(128)[None, :]
    member = (bucket.reshape(-1)[:, None] == lane) | ((col[:, None] == 0) & (lane == NBUCKET))

    def body(d_ref, m_ref, o_ref):
        mm = m_ref[...]
        o_ref[...] = sum(_dot(part, mm) for part in _split3(d_ref[...]))

    return pl.pallas_call(body, name="bias_reduce", out_shape=jax.ShapeDtypeStruct((HQ, 128), F32),
                          compiler_params=_cp())(dacc.reshape(HQ, BLK * 2 * BLK), jnp.asarray(member, BF16))


GQ = GRP * BLK


def _stack_heads(x, nh):
    return jnp.concatenate([x[:, DH * h:DH * (h + 1)] for h in range(nh)], axis=0)


def _unstack(xs, nh):
    rows = xs.shape[0] // nh
    return jnp.concatenate([xs[rows * h:rows * (h + 1)] for h in range(nh)], axis=1)


def _rms(x):
    return lax.rsqrt(jnp.mean(x * x, axis=-1, keepdims=True) + EPS)


def _stack_q(q, qw):
    qs = _stack_heads(q, HQ)
    r = _rms(qs)
    qhat = qs * r
    return qhat * qw, qhat, r


def _band_first(shape):
    return (lax.broadcasted_iota(jnp.int32, shape, 0) & (2 * BLK - 1)) == 0


def _stack_kv(kp, kc, vp, vc, kw):
    ks = _stack_heads(jnp.concatenate([kp, kc], axis=0), HKV)
    r = _rms(ks)
    khat = ks * r
    first = _band_first(ks.shape)
    kn = jnp.where(first, 0.0, khat * kw)
    v2 = jnp.where(first, 0.0, _stack_heads(jnp.concatenate([vp, vc], axis=0), HKV)).astype(BF16)
    return kn, khat, r, v2


def _softmax_rows(s):
    p = jnp.exp(s - jnp.max(s, axis=-1, keepdims=True))
    return p * (1.0 / _rsum(p))


def attn_fwd(proj, biasm, q_norm_w, k_norm_w):
    t = proj.shape[0]
    nb2 = t // (2 * BLK)

    def body(q_ref, kc_ref, kp_ref, vc_ref, vp_ref, bm_ref, qw_ref, kw_ref, o_ref):
        n = pl.program_id(0)
        f32 = lambda a: a.astype(F32)
        for half in range(2):
            rows = slice(BLK * half, BLK * (half + 1))
            kp = kp_ref[...] if half == 0 else kc_ref[0:BLK, :]
            vp = vp_ref[...] if half == 0 else vc_ref[0:BLK, :]
            bm = bm_ref[jnp.minimum(n, 1)] if half == 0 else bm_ref[1]
            qn = _stack_q(f32(q_ref[rows, :]), qw_ref[...])[0].astype(BF16)
            kn, _, _, v2 = _stack_kv(f32(kp), f32(kc_ref[rows, :]), f32(vp), f32(vc_ref[rows, :]), kw_ref[...])
            knb = kn.astype(BF16)
            s = jnp.concatenate([_dot_nt(qn[GQ * j:GQ * (j + 1)], knb[2 * BLK * j:2 * BLK * (j + 1)])
                                 for j in range(HKV)], axis=0)
            pr = _softmax_rows(s * SCALE + bm.reshape(HQ * BLK, 2 * BLK)).astype(BF16)
            o = jnp.concatenate([_dot(pr[GQ * j:GQ * (j + 1)], v2[2 * BLK * j:2 * BLK * (j + 1)])
                                 for j in range(HKV)], axis=0)
            o_ref[rows, :] = _unstack(o, HQ).astype(BF16)

    kblk, vblk = C_K // 256, C_V // 256
    prev = lambda n: jnp.maximum(2 * n - 1, 0)
    return pl.pallas_call(
        body, name="attn_fwd", grid=(nb2,),
        in_specs=[pl.BlockSpec((2 * BLK, D), lambda n: (n, 0)),
                  pl.BlockSpec((2 * BLK, 256), lambda n: (n, kblk)),
                  pl.BlockSpec((BLK, 256), lambda n: (prev(n), kblk)),
                  pl.BlockSpec((2 * BLK, 256), lambda n: (n, vblk)),
                  pl.BlockSpec((BLK, 256), lambda n: (prev(n), vblk)),
                  _full((2, HQ, BLK, 2 * BLK)), _full((1, DH)), _full((1, DH))],
        out_specs=pl.BlockSpec((2 * BLK, D), lambda n: (n, 0)),
        out_shape=jax.ShapeDtypeStruct((t, D), BF16),
        compiler_params=_cp(("parallel",)),
    )(proj, proj, proj, proj, proj, biasm, q_norm_w, k_norm_w)


def attn_bwd(proj, dao, biasm, q_norm_w, k_norm_w):
    t = proj.shape[0]
    nb = t // BLK
    kb = 2 * BLK

    def body(q_ref, kc_ref, kp_ref, vc_ref, vp_ref, do_ref, bm_ref, qw_ref, kw_ref,
             dq_ref, dkv_ref, dqw_ref, dkw_ref, dacc_ref, ck, cv, pk, pv, nk, nv):
        n = pl.program_id(0)

        @pl.when(n == 0)
        def _():
            for ref in (dqw_ref, dkw_ref, dacc_ref, ck, cv):
                ref[...] = jnp.zeros_like(ref)

        qw = qw_ref[...]
        kw = kw_ref[...]
        f = lambda ref: ref[...].astype(F32)
        kn, khat, rk, v2 = _stack_kv(f(kp_ref), f(kc_ref), f(vp_ref), f(vc_ref), kw)
        grp = lambda a, j: a[GQ * j:GQ * (j + 1)]
        band = lambda a, j: a[kb * j:kb * (j + 1)]

        @pl.when(n < nb)
        def _():
            qn, qhat, rq = _stack_q(f(q_ref), qw)
            qnb = qn.astype(BF16)
            knb = kn.astype(BF16)
            dos = _stack_heads(f(do_ref), HQ).astype(BF16)
            s = jnp.concatenate([_dot_nt(grp(qnb, j), band(knb, j)) for j in range(HKV)], axis=0)
            pr = _softmax_rows(s * SCALE + bm_ref[0].reshape(HQ * BLK, kb))
            dp = jnp.concatenate([_dot_nt(grp(dos, j), band(v2, j)) for j in range(HKV)], axis=0)
            ds = pr * (dp - _rsum(pr * dp))
            dacc_ref[...] += ds.reshape(HQ, BLK, kb)
            dsb = ds.astype(BF16)
            prb = pr.astype(BF16)
            dqn = jnp.concatenate([_dot(grp(dsb, j), band(knb, j)) for j in range(HKV)], axis=0) * SCALE
            dqhat = dqn * qw
            dq = rq * (dqhat - qhat * jnp.mean(dqhat * qhat, axis=-1, keepdims=True))
            dq_ref[...] = _unstack(dq, HQ).astype(BF16)
            dqw_ref[...] += _csum(dqn * qhat)
            first = _band_first((kb, DH))
            for j in range(HKV):
                rows = slice(BLK * j, BLK * (j + 1))
                dkn = jnp.where(first, 0.0, _dot_tn(grp(dsb, j), grp(qnb, j)) * SCALE)
                dvj = jnp.where(first, 0.0, _dot_tn(grp(prb, j), grp(dos, j)))
                pk[rows, :] = dkn[:BLK]
                nk[rows, :] = dkn[BLK:]
                pv[rows, :] = dvj[:BLK]
                nv[rows, :] = dvj[BLK:]

        @pl.when(n == nb)
        def _():
            for ref in (pk, pv, nk, nv):
                ref[...] = jnp.zeros_like(ref)

        khp = jnp.concatenate([khat[kb * j:kb * j + BLK] for j in range(HKV)], axis=0)
        rkp = jnp.concatenate([rk[kb * j:kb * j + BLK] for j in range(HKV)], axis=0)
        dkn = ck[...] + pk[...]
        dkhat = dkn * kw
        dk = rkp * (dkhat - khp * jnp.mean(dkhat * khp, axis=-1, keepdims=True))
        dkw_ref[...] += _csum(dkn * khp)
        dkv_ref[...] = jnp.concatenate([_unstack(dk, HKV), _unstack(cv[...] + pv[...], HKV)], axis=1).astype(BF16)
        ck[...] = nk[...]
        cv[...] = nv[...]

    kblk, vblk = C_K // 256, C_V // 256
    cur = lambda n: jnp.minimum(n, nb - 1)
    prev = lambda n: jnp.maximum(n - 1, 0)
    carry = pltpu.VMEM((HKV * BLK, DH), F32)
    return pl.pallas_call(
        body, name="attn_bwd", grid=(nb + 1,),
        in_specs=[pl.BlockSpec((BLK, D), lambda n: (cur(n), 0)),
                  pl.BlockSpec((BLK, 256), lambda n: (cur(n), kblk)), pl.BlockSpec((BLK, 256), lambda n: (prev(n), kblk)),
                  pl.BlockSpec((BLK, 256), lambda n: (cur(n), vblk)), pl.BlockSpec((BLK, 256), lambda n: (prev(n), vblk)),
                  pl.BlockSpec((BLK, D), lambda n: (cur(n), 0)),
                  pl.BlockSpec((1, HQ, BLK, kb), lambda n: (jnp.minimum(n, 1), 0, 0, 0)),
                  _full((1, DH)), _full((1, DH))],
        out_specs=[pl.BlockSpec((BLK, D), lambda n: (cur(n), 0)),
                   pl.BlockSpec((BLK, 512), lambda n: (prev(n), 0)),
                   _full((1, DH)), _full((1, DH)), _full((HQ, BLK, kb))],
        out_shape=[jax.ShapeDtypeStruct((t, D), BF16), jax.ShapeDtypeStruct((t, 512), BF16),
                   jax.ShapeDtypeStruct((1, DH), F32),
                   jax.ShapeDtypeStruct((1, DH), F32), jax.ShapeDtypeStruct((HQ, BLK, kb), F32)],
        scratch_shapes=[carry] * 6,
        compiler_params=_cp(("arbitrary",)),
    )(proj, proj, proj, proj, proj, dao, biasm, q_norm_w, k_norm_w)


CONV_TM, CONV_CW, CONV_RC, HALO = 1024, 1024, 32, 16


def conv_fwd(proj, conv_w, conv_b):
    t = proj.shape[0]
    tm = min(t, CONV_TM)
    c0 = C_XBC // CONV_CW

    def body(x_ref, xp_ref, w_ref, b_ref, o_ref, ds_ref):
        i = pl.program_id(1)
        w = w_ref[...]
        b = b_ref[...]
        for r in range(tm // CONV_RC):
            lo = r * CONV_RC
            if r == 0:
                head = jnp.where(i == 0, 0.0, xp_ref[...].astype(F32))
                win = jnp.concatenate([head, x_ref[0:CONV_RC, :].astype(F32)], axis=0)
            else:
                win = x_ref[lo - HALO:lo + CONV_RC, :].astype(F32)
            acc = b
            for j in range(CONV_K):
                acc = acc + w[j:j + 1] * win[HALO - 3 + j:HALO - 3 + j + CONV_RC]
            sg = _sig(acc)
            o_ref[lo:lo + CONV_RC, :] = acc * sg
            ds_ref[lo:lo + CONV_RC, :] = _dsilu(acc, sg).astype(BF16)

    rh = tm // HALO
    tile = pl.BlockSpec((tm, CONV_CW), lambda s, i: (i, s))
    return pl.pallas_call(
        body, name="conv_fwd", grid=(XBC // CONV_CW, t // tm),
        in_specs=[pl.BlockSpec((tm, CONV_CW), lambda s, i: (i, c0 + s)),
                  pl.BlockSpec((HALO, CONV_CW), lambda s, i: (jnp.maximum(i * rh - 1, 0), c0 + s)),
                  pl.BlockSpec((CONV_K, CONV_CW), lambda s, i: (0, s)), pl.BlockSpec((1, CONV_CW), lambda s, i: (0, s))],
        out_specs=[tile, tile],
        out_shape=[jax.ShapeDtypeStruct((t, XBC), F32), jax.ShapeDtypeStruct((t, XBC), BF16)],
        compiler_params=_cp(("parallel", "parallel")),
    )(proj, proj, conv_w, conv_b)


def conv_bwd(proj, dact, dsl, conv_w):
    t = proj.shape[0]
    tm = min(t, CONV_TM)
    nt = t // tm
    nr = tm // CONV_RC
    c0 = C_XBC // CONV_CW
    ext = CONV_RC + 8

    def body(x_ref, xp_ref, d_ref, dn_ref, s_ref, sn_ref, w_ref, dx_ref, dw_ref, db_ref):
        i = pl.program_id(1)

        @pl.when(i == 0)
        def _():
            dw_ref[...] = jnp.zeros_like(dw_ref)
            db_ref[...] = jnp.zeros_like(db_ref)

        w = w_ref[...]
        dws = [jnp.zeros((1, CONV_CW), F32) for _ in range(CONV_K)]
        db = jnp.zeros((1, CONV_CW), F32)
        for r in range(nr):
            lo = r * CONV_RC
            if r == 0:
                head = jnp.where(i == 0, 0.0, xp_ref[...].astype(F32))
                win = jnp.concatenate([head, x_ref[0:CONV_RC, :].astype(F32)], axis=0)
            else:
                win = x_ref[lo - HALO:lo + CONV_RC, :].astype(F32)
            if r < nr - 1:
                dext = d_ref[lo:lo + ext, :]
                sext = s_ref[lo:lo + CONV_RC + HALO, :].astype(F32)[0:ext]
            else:
                dext = jnp.concatenate([d_ref[lo:lo + CONV_RC, :], jnp.where(i == nt - 1, 0.0, dn_ref[...])], axis=0)
                sext = jnp.concatenate([s_ref[lo:lo + CONV_RC, :].astype(F32), sn_ref[...].astype(F32)], axis=0)[0:ext]
            dpre = dext * sext
            dx = jnp.zeros((CONV_RC, CONV_CW), F32)
            own = dpre[0:CONV_RC]
            for j in range(CONV_K):
                dx = dx + w[j:j + 1] * dpre[3 - j:3 - j + CONV_RC]
                dws[j] = dws[j] + _csum(own * win[HALO - 3 + j:HALO - 3 + j + CONV_RC])
            db = db + _csum(own)
            dx_ref[lo:lo + CONV_RC, :] = dx.astype(BF16)
        dw_ref[...] += jnp.concatenate(dws, axis=0)
        db_ref[...] += db

    rh = tm // HALO
    r8 = tm // 8
    nxt = lambda i, per: jnp.minimum((i + 1) * per, nt * per - 1)
    return pl.pallas_call(
        body, name="conv_bwd", grid=(XBC // CONV_CW, nt),
        in_specs=[pl.BlockSpec((tm, CONV_CW), lambda s, i: (i, c0 + s)),
                  pl.BlockSpec((HALO, CONV_CW), lambda s, i: (jnp.maximum(i * rh - 1, 0), c0 + s)),
                  pl.BlockSpec((tm, CONV_CW), lambda s, i: (i, s)),
                  pl.BlockSpec((8, CONV_CW), lambda s, i: (nxt(i, r8), s)),
                  pl.BlockSpec((tm, CONV_CW), lambda s, i: (i, s)),
                  pl.BlockSpec((HALO, CONV_CW), lambda s, i: (nxt(i, rh), s)),
                  pl.BlockSpec((CONV_K, CONV_CW), lambda s, i: (0, s))],
        out_specs=[pl.BlockSpec((tm, CONV_CW), lambda s, i: (i, s)),
                   pl.BlockSpec((CONV_K, CONV_CW), lambda s, i: (0, s)), pl.BlockSpec((1, CONV_CW), lambda s, i: (0, s))],
        out_shape=[jax.ShapeDtypeStruct((t, XBC), BF16), jax.ShapeDtypeStruct((CONV_K, XBC), F32),
                   jax.ShapeDtypeStruct((1, XBC), F32)],
        compiler_params=_cp(("parallel", "arbitrary")),
    )(proj, proj, dact, dact, dsl, dsl, conv_w)


def _split3(x):
    h = x.astype(BF16)
    r = x - h.astype(F32)
    m = r.astype(BF16)
    lo = (r - m.astype(F32)).astype(BF16)
    return h, m, lo


def _tri_mm(tri, x):
    h, m, lo = _split3(x)
    return _dot(tri, h) + _dot(tri, m) + _dot(tri, lo)


def _softplus(x):
    return jnp.maximum(x, 0.0) + jnp.log1p(jnp.exp(-jnp.abs(x)))


def _chunk_decays(dt_raw, dtb, alog):
    dtv = _softplus(dt_raw + dtb)
    a = -jnp.exp(alog)
    ri = lax.broadcasted_iota(jnp.int32, (BLK, BLK), 0)
    ci = lax.broadcasted_iota(jnp.int32, (BLK, BLK), 1)
    causal = ri >= ci
    acum = _tri_mm(causal.astype(BF16), dtv * a)
    return dtv, a, causal, acum, acum.T


NPAIR = SH // 2


def _pairs(x):
    return jnp.stack([x[:, 128 * k:128 * (k + 1)] for k in range(NPAIR)])


def _unpairs(x3):
    return jnp.concatenate([x3[k] for k in range(NPAIR)], axis=1)


def _per_head_cols(m):
    return jnp.stack([jnp.broadcast_to(m[:, h:h + 1], m.shape) for h in range(SH)])


def _pair_lanes(t):
    r = t.reshape(NPAIR, 2, t.shape[1], 128)
    lo = lax.broadcasted_iota(jnp.int32, (1, t.shape[1], 128), 2) < SP
    return jnp.where(lo, r[:, 0], r[:, 1])


class _Chunk:
    pass


def _chunk_common(dt_raw, dtb, alog, dskip):
    cm = _Chunk()
    cm.dtv, cm.a, cm.causal, acum, acum_t = _chunk_decays(dt_raw, dtb, alog)
    cm.acol = _per_head_cols(acum)
    cm.arow = jnp.stack([acum_t[h:h + 1, :] for h in range(SH)])
    apl = _pair_lanes(cm.acol)
    alast = apl[:, BLK - 1:BLK, :]
    cm.dpl = _pair_lanes(_per_head_cols(cm.dtv))
    cm.eapl = jnp.exp(apl)
    cm.epl = jnp.exp(alast - apl)
    cm.cdpl = jnp.exp(alast)
    cm.dskpl = _pair_lanes(_per_head_cols(dskip))
    cm.lo = lax.broadcasted_iota(jnp.int32, (1, BLK, 128), 2) < SP
    return cm


def ssd_fwd(act, dt_raw, dtb_p, alog_p, dsk_p):
    t = act.shape[0]
    nc = t // BLK
    cps = 2

    def body(xs_ref, b_ref, c_ref, dt_ref, dtb_ref, al_ref, dk_ref, y_ref, sp_ref, st):
        c = pl.program_id(0)

        @pl.when(c == 0)
        def _():
            st[...] = jnp.zeros_like(st)

        s_t = st[...]
        for sub in range(cps):
            rows = slice(BLK * sub, BLK * (sub + 1))
            sp_ref[sub] = s_t
            cm = _chunk_common(dt_ref[rows, :], dtb_ref[...], al_ref[...], dk_ref[...])
            gms, cbs, bts = [], [], []
            for g in range(SG):
                bf = b_ref[rows, SN * g:SN * (g + 1)]
                cb = c_ref[rows, SN * g:SN * (g + 1)].astype(BF16)
                gms.append(_dot_nt(cb, bf.astype(BF16)))
                cbs.append(cb)
                bts.append(bf.T.astype(BF16))
            lam = jnp.exp(jnp.where(cm.causal[None], cm.acol - cm.arow, NEG))
            m = (lam.reshape(SG, SR, BLK, BLK) * jnp.stack(gms)[:, None]).reshape(SH, BLK, BLK).astype(BF16)
            xs16 = _pairs(xs_ref[rows, :])
            xdt16 = xs16 * cm.dpl
            x_lo = jnp.where(cm.lo, xdt16, 0.0).astype(BF16)
            x_hi = jnp.where(cm.lo, 0.0, xdt16).astype(BF16)
            s16 = _pairs(s_t)
            s16b = s16.astype(BF16)
            yd = jnp.stack([_dot(m[2 * k], x_lo[k]) + _dot(m[2 * k + 1], x_hi[k]) for k in range(NPAIR)])
            yo = jnp.stack([_dot(cbs[k // (NPAIR // SG)], s16b[k]) for k in range(NPAIR)])
            y_ref[rows, :] = _unpairs(yd + yo * cm.eapl + cm.dskpl * xs16).astype(BF16)
            xe = (xdt16 * cm.epl).astype(BF16)
            s_t = _unpairs(cm.cdpl * s16 + jnp.stack([_dot(bts[k // (NPAIR // SG)], xe[k]) for k in range(NPAIR)]))
        st[...] = s_t

    vec = _full((1, 128))
    tb = cps * BLK
    return pl.pallas_call(
        body, name="ssd_fwd", grid=(nc // cps,),
        in_specs=[pl.BlockSpec((tb, SSM_W), lambda c: (c, 0)),
                  pl.BlockSpec((tb, SG * SN), lambda c: (c, SSM_W // (SG * SN))),
                  pl.BlockSpec((tb, SG * SN), lambda c: (c, SSM_W // (SG * SN) + 1)),
                  pl.BlockSpec((tb, 128), lambda c: (c, 0)), vec, vec, vec],
        out_specs=[pl.BlockSpec((tb, SSM_W), lambda c: (c, 0)), pl.BlockSpec((cps, SN, SSM_W), lambda c: (c, 0, 0))],
        out_shape=[jax.ShapeDtypeStruct((t, SSM_W), BF16), jax.ShapeDtypeStruct((nc, SN, SSM_W), F32)],
        scratch_shapes=[pltpu.VMEM((SN, SSM_W), F32)],
        compiler_params=_cp(("arbitrary",)),
    )(act, act, act, dt_raw, dtb_p, alog_p, dsk_p)


def _head_sums(q):
    r = q.shape[1]
    lo = lax.broadcasted_iota(jnp.int32, (1, r, 128), 2) < SP
    s_lo = jnp.sum(jnp.where(lo, q, 0.0), axis=-1, keepdims=True)
    s_hi = jnp.sum(jnp.where(lo, 0.0, q), axis=-1, keepdims=True)
    lane = lax.broadcasted_iota(jnp.int32, (r, 128), 1)
    out = jnp.zeros((r, 128), F32)
    for k in range(NPAIR):
        out = jnp.where(lane == 2 * k, s_lo[k], jnp.where(lane == 2 * k + 1, s_hi[k], out))
    return out


def ssd_bwd(act, dt_raw, dy, sprev, dtb_p, alog_p, dsk_p):
    t = act.shape[0]
    nc = t // BLK

    def body(xs_ref, b_ref, c_ref, dt_ref, dy_ref, sp_ref, dtb_ref, al_ref, dk_ref,
             da_ref, ddt_ref, ddtb_ref, dal_ref, ddk_ref, dst):
        i = pl.program_id(0)

        @pl.when(i == 0)
        def _():
            dst[...] = jnp.zeros_like(dst)
            ddtb_ref[...] = jnp.zeros_like(ddtb_ref)
            dal_ref[...] = jnp.zeros_like(dal_ref)
            ddk_ref[...] = jnp.zeros_like(ddk_ref)

        dt_raw = dt_ref[...]
        dtb = dtb_ref[...]
        cm = _chunk_common(dt_raw, dtb, al_ref[...], dk_ref[...])
        ri = lax.broadcasted_iota(jnp.int32, (BLK, BLK), 0)
        ci = lax.broadcasted_iota(jnp.int32, (BLK, BLK), 1)
        lam_t = jnp.exp(jnp.where((ri <= ci)[None], cm.arow - cm.acol, NEG))
        bbs, cbs, cts, gms = [], [], [], []
        for g in range(SG):
            bf = b_ref[:, SN * g:SN * (g + 1)]
            cf = c_ref[:, SN * g:SN * (g + 1)]
            bbs.append(bf.astype(BF16))
            cbs.append(cf.astype(BF16))
            cts.append(cf.T.astype(BF16))
            gms.append(_dot_nt(bbs[g], cbs[g]))
        grp = lambda k: k // (NPAIR // SG)
        xs16 = _pairs(xs_ref[...])
        dy16 = _pairs(dy_ref[...].astype(F32))
        sp16 = _pairs(sp_ref[0])
        ds16 = _pairs(dst[...])
        xdt16 = xs16 * cm.dpl
        xdtb = xdt16.astype(BF16)
        dyh = [jnp.where(cm.lo, dy16, 0.0).astype(BF16), jnp.where(cm.lo, 0.0, dy16).astype(BF16)]
        m_t = (lam_t.reshape(SG, SR, BLK, BLK) * jnp.stack(gms)[:, None]).reshape(SH, BLK, BLK).astype(BF16)
        dxdt = jnp.stack([_dot(m_t[2 * k], dyh[0][k]) + _dot(m_t[2 * k + 1], dyh[1][k]) for k in range(NPAIR)])
        dm_t = jnp.stack([_dot_nt(xdtb[h // 2], dyh[h % 2][h // 2]) for h in range(SH)])
        dg_t = jnp.sum((dm_t * lam_t).reshape(SG, SR, BLK, BLK), axis=1).astype(BF16)
        xq16 = xdtb.astype(F32)
        xh = [jnp.where(cm.lo, xdt16, 0.0).astype(BF16), jnp.where(cm.lo, 0.0, xdt16).astype(BF16)]
        y_in = jnp.stack([_dot_tn(m_t[2 * k], xh[0][k]) + _dot_tn(m_t[2 * k + 1], xh[1][k]) for k in range(NPAIR)])
        da_diag = dy16 * y_in - xq16 * dxdt
        lane_c = lax.broadcasted_iota(jnp.int32, (BLK, 128), 1)
        ds16b = ds16.astype(BF16)
        sp16b = sp16.astype(BF16)
        dxs = jnp.stack([_dot(bbs[grp(k)], ds16b[k]) for k in range(NPAIR)]) * cm.epl
        dxdt = dxdt + dxs
        dya = (dy16 * cm.eapl).astype(BF16)
        xe = (xdt16 * cm.epl).astype(BF16)
        dcs, dbs = [], []
        for g in range(SG):
            ks = range(g * (NPAIR // SG), (g + 1) * (NPAIR // SG))
            dcs.append(sum(_dot_nt(dya[k], sp16b[k]) for k in ks) + _dot_tn(dg_t[g], bbs[g]))
            dbs.append(sum(_dot_nt(xe[k], ds16b[k]) for k in ks) + _dot(dg_t[g], cbs[g]))
        dst[...] = _unpairs(cm.cdpl * ds16 + jnp.stack([_dot(cts[grp(k)], dya[k]) for k in range(NPAIR)]))
        da_ref[...] = jnp.concatenate([_unpairs(dxdt * cm.dpl + cm.dskpl * dy16)] + dbs + dcs, axis=1)
        y_off = jnp.stack([_dot(cbs[grp(k)], sp16b[k]) for k in range(NPAIR)]) * cm.eapl
        da_cols = _head_sums(da_diag + dy16 * y_off - xdt16 * dxs)
        last = _head_sums(jnp.sum(xdt16 * dxs, axis=1, keepdims=True)
                          + cm.cdpl * jnp.sum(ds16 * sp16, axis=1, keepdims=True))
        ddt = _head_sums(dxdt * xs16)
        row_i = lax.broadcasted_iota(jnp.int32, (BLK, 128), 0)
        dacum = da_cols + jnp.where(row_i == BLK - 1, last, 0.0)
        dda = _tri_mm((ri <= ci).astype(BF16), dacum)
        ddt = ddt + dda * cm.a
        dal_ref[...] += _csum(dda * cm.dtv) * cm.a
        ddt_raw = jnp.where(lane_c < SH, ddt * _sig(dt_raw + dtb), 0.0)
        ddt_ref[...] = ddt_raw.astype(BF16)
        ddtb_ref[...] += _csum(ddt_raw)
        ddk_ref[...] += _head_sums(jnp.sum(dy16 * xs16, axis=1, keepdims=True))

    rev = lambda i: nc - 1 - i
    vec = _full((1, 128))
    slab = pl.BlockSpec((BLK, SSM_W), lambda i: (rev(i), 0))
    return pl.pallas_call(
        body, name="ssd_bwd", grid=(nc,),
        in_specs=[slab,
                  pl.BlockSpec((BLK, SG * SN), lambda i: (rev(i), SSM_W // (SG * SN))),
                  pl.BlockSpec((BLK, SG * SN), lambda i: (rev(i), SSM_W // (SG * SN) + 1)),
                  pl.BlockSpec((BLK, 128), lambda i: (rev(i), 0)),
                  slab,
                  pl.BlockSpec((1, SN, SSM_W), lambda i: (rev(i), 0, 0)), vec, vec, vec],
        out_specs=[pl.BlockSpec((BLK, XBC), lambda i: (rev(i), 0)), pl.BlockSpec((BLK, 128), lambda i: (rev(i), 0)),
                   vec, vec, vec],
        out_shape=[jax.ShapeDtypeStruct((t, XBC), F32), jax.ShapeDtypeStruct((t, 128), BF16),
                   jax.ShapeDtypeStruct((1, 128), F32), jax.ShapeDtypeStruct((1, 128), F32),
                   jax.ShapeDtypeStruct((1, 128), F32)],
        scratch_shapes=[pltpu.VMEM((SN, SSM_W), F32)],
        compiler_params=_cp(("arbitrary",)),
    )(act, act, act, dt_raw, dy, sprev, dtb_p, alog_p, dsk_p)


TAIL_TM = 256


def _dsilu(z, s):
    return s * (1.0 + z * (1.0 - s))


def tail(proj, ao, yss, x, target, gate, ssm_nw, w_at, w_ss, w_ou):
    t = x.shape[0]
    tm = min(t, TAIL_TM)
    gw = SSM_W // SG

    def body(ao_ref, za_ref, ga_ref, gb_ref, zm_ref, ys_ref, x_ref, tg_ref, gt_ref, nw_ref, wa_ref, ws_ref, wo_ref,
             loss_ref, dy_ref, dao_ref, dmid_ref, dys_ref,
             ua_ref, yn_ref, mg_ref, dya_ref, dyb_ref, do_ref, dgt_ref, dnw_ref):
        i = pl.program_id(0)

        @pl.when(i == 0)
        def _():
            loss_ref[...] = jnp.zeros_like(loss_ref)
            dgt_ref[...] = jnp.zeros_like(dgt_ref)
            dnw_ref[...] = jnp.zeros_like(dnw_ref)

        ao = ao_ref[...].astype(F32)
        za = za_ref[...].astype(F32)
        sa = _sig(za)
        sila = za * sa
        ua_f = ao * sila
        ua = ua_f.astype(BF16)
        ya = _dot(ua, wa_ref[...])
        zm = zm_ref[...].astype(F32)
        sm = _sig(zm)
        silm = zm * sm
        ys = ys_ref[...].astype(F32)
        u = ys * silm
        nw = nw_ref[...]
        rs, uns = [], []
        for g in range(SG):
            ug = u[:, gw * g:gw * (g + 1)]
            r = lax.rsqrt(jnp.mean(ug * ug, axis=-1, keepdims=True) + EPS)
            rs.append(r)
            uns.append(ug * r)
        un = jnp.concatenate(uns, axis=1)
        yn_f = un * nw
        yn = yn_f.astype(BF16)
        yb = _dot(yn, ws_ref[...])
        sga = _sig(ga_ref[...].astype(F32))
        sgb = _sig(gb_ref[...].astype(F32))
        mg_f = sga * ya + sgb * yb
        mg = mg_f.astype(BF16)
        o = _dot(mg, wo_ref[...])
        gt = gt_ref[...]
        err = (x_ref[...] + gt * o) - tg_ref[...]
        lane = lax.broadcasted_iota(jnp.int32, (1, 128), 1)
        loss_ref[...] += jnp.where(lane == 0, 0.5 * _asum(_rsum(err * err) / D), 0.0)
        dy = err * (1.0 / D)
        dy_ref[...] = dy
        dgt_ref[...] += _csum(dy * o)
        do = (dy * gt).astype(BF16)
        dmg = _dot_nt(do, wo_ref[...])
        dmid_ref[:, C_GA - C_ZA:C_GB - C_ZA] = (dmg * ya * sga * (1.0 - sga)).astype(BF16)
        dmid_ref[:, C_GB - C_ZA:C_ZM - C_ZA] = (dmg * yb * sgb * (1.0 - sgb)).astype(BF16)
        dya = (dmg * sga).astype(BF16)
        dyb = (dmg * sgb).astype(BF16)
        dua = _dot_nt(dya, wa_ref[...])
        dao_ref[...] = (dua * sila).astype(BF16)
        dmid_ref[:, 0:C_GA - C_ZA] = (dua * ao * _dsilu(za, sa)).astype(BF16)
        dyn = _dot_nt(dyb, ws_ref[...])
        dnw_ref[...] += _csum(dyn * un)
        dun = dyn * nw
        dus = []
        for g in range(SG):
            gs = slice(gw * g, gw * (g + 1))
            dus.append(rs[g] * (dun[:, gs] - uns[g] * jnp.mean(dun[:, gs] * uns[g], axis=-1, keepdims=True)))
        du = jnp.concatenate(dus, axis=1)
        dys_ref[...] = (du * silm).astype(BF16)
        dmid_ref[:, C_ZM - C_ZA:] = (du * ys * _dsilu(zm, sm)).astype(BF16)
        ua_ref[...] = ua_f.T.astype(BF16)
        yn_ref[...] = yn_f.T.astype(BF16)
        mg_ref[...] = mg_f.T.astype(BF16)
        dya_ref[...] = dya
        dyb_ref[...] = dyb
        do_ref[...] = do

    row = lambda w: pl.BlockSpec((tm, w), lambda i: (i, 0))
    pcol = lambda w, c0: pl.BlockSpec((tm, w), lambda i: (i, c0 // w))
    sd = lambda w, dt: jax.ShapeDtypeStruct((t, w), dt)
    colt = lambda w: pl.BlockSpec((w, tm), lambda i: (0, i))
    sdt = lambda w: jax.ShapeDtypeStruct((w, t), BF16)
    return pl.pallas_call(
        body, name="tail", grid=(t // tm,),
        in_specs=[row(D), pcol(D, C_ZA), pcol(D, C_GA), pcol(D, C_GB), pcol(SSM_W, C_ZM), row(SSM_W), row(D), row(D),
                  _full((1, D)), _full((1, SSM_W)), _full((D, D)), _full((SSM_W, D)), _full((D, D))],
        out_specs=[_full((1, 128)), row(D), row(D), row(W_MID), row(SSM_W),
                   colt(D), colt(SSM_W), colt(D), row(D), row(D), row(D), _full((1, D)), _full((1, SSM_W))],
        out_shape=[jax.ShapeDtypeStruct((1, 128), F32), sd(D, F32), sd(D, BF16), sd(W_MID, BF16),
                   sd(SSM_W, BF16), sdt(D), sdt(SSM_W), sdt(D), sd(D, BF16),
                   sd(D, BF16), sd(D, BF16), jax.ShapeDtypeStruct((1, D), F32), jax.ShapeDtypeStruct((1, SSM_W), F32)],
        compiler_params=_cp(("arbitrary",)),
    )(ao, proj, proj, proj, proj, yss, x, target, gate, ssm_nw, w_at, w_ss, w_ou)


DPIECES = ((D, ((D, C_Q),)),
           (W_MID, ((D, C_ZA), (D, C_GA), (D, C_GB), (SSM_W, C_ZM))),
           (XBC, ((XBC, C_XBC),)),
           (512, ((512, C_K),)),
           (128, ((128, C_DT),)))


def dproj_bwd(pieces, wcat, x, dy, norm_w, scale):
    t = x.shape[0]
    tm = min(t, 256)
    nt = t // tm
    wblocks = [blk for _, subs in DPIECES for blk in subs]
    npc, nwb = len(DPIECES), len(wblocks)

    def body(*refs):
        p_refs, w_refs = refs[:npc], refs[npc:npc + nwb]
        x_ref, dy_ref, nw_ref, sc_ref, gx_ref, dnw_ref, dsc_ref, dsh_ref, dwe_ref = refs[npc + nwb:]
        i = pl.program_id(0)

        @pl.when(i == 0)
        def _():
            for ref in (dwe_ref, dsh_ref, dnw_ref, dsc_ref):
                ref[...] = jnp.zeros_like(ref)

        dh, wi = None, 0
        for p_ref, (_, subs) in zip(p_refs, DPIECES):
            loc = 0
            for w, _ in subs:
                part = _dot_nt(p_ref[:, loc:loc + w], w_refs[wi][...])
                dh = part if dh is None else dh + part
                loc += w
                wi += 1
        xv = x_ref[...]
        r = lax.rsqrt(jnp.mean(xv * xv, axis=-1, keepdims=True) + EPS)
        xn = xv * r
        weff = nw_ref[...] * (1.0 + sc_ref[...])
        dxn = dh * weff
        gx_ref[...] = dy_ref[...] + r * (dxn - xn * jnp.mean(dxn * xn, axis=-1, keepdims=True))
        dwe_ref[...] += _csum(dh * xn)
        dsh_ref[...] += _csum(dh)

        @pl.when(i == nt - 1)
        def _():
            dwe = dwe_ref[...]
            dnw_ref[...] = dwe * (1.0 + sc_ref[...])
            dsc_ref[...] = dwe * nw_ref[...]

    vec = pl.BlockSpec((1, D), lambda i: (0, 0))
    row = pl.BlockSpec((tm, D), lambda i: (i, 0))
    return pl.pallas_call(
        body, name="dproj_bwd", grid=(nt,),
        in_specs=[pl.BlockSpec((tm, pw), lambda i: (i, 0)) for pw, _ in DPIECES]
        + [pl.BlockSpec((D, w), functools.partial(lambda i, b: (0, b), b=off // w), pipeline_mode=pl.Buffered(1))
           for w, off in wblocks]
        + [row, row, vec, vec],
        out_specs=[row, vec, vec, vec],
        out_shape=[jax.ShapeDtypeStruct((t, D), F32), jax.ShapeDtypeStruct((1, D), F32),
                   jax.ShapeDtypeStruct((1, D), F32), jax.ShapeDtypeStruct((1, D), F32)],
        scratch_shapes=[pltpu.VMEM((1, D), F32)],
        compiler_params=_cp(("arbitrary",)),
    )(*pieces, *([wcat] * nwb), x, dy, norm_w, scale)


def wgrad(at, b, name, bn, after):
    m, t = at.shape
    n = b.shape[1]
    tk = min(t, 2048)
    bm = min(m, 1024)

    def body(a_ref, b_ref, after_ref, o_ref):
        part = _dot(a_ref[...], b_ref[...])

        @pl.when(pl.program_id(2) == 0)
        def _():
            o_ref[...] = part

        @pl.when(pl.program_id(2) > 0)
        def _():
            o_ref[...] += part

    return pl.pallas_call(
        body, name=name, grid=(m // bm, n // bn, t // tk),
        in_specs=[pl.BlockSpec((bm, tk), lambda i, j, k: (i, k)), pl.BlockSpec((tk, bn), lambda i, j, k: (k, j)), ANY],
        out_specs=pl.BlockSpec((bm, bn), lambda i, j, k: (i, j)),
        out_shape=jax.ShapeDtypeStruct((m, n), F32),
        compiler_params=_cp(("parallel", "parallel", "arbitrary")),
    )(at, b, after)


SUM_TR = 256


def pair_sum(g, core, theirs, name):
    w = g.shape[2]
    nh = HROWS // SUM_TR

    def body(core_ref, a_ref, b_ref, o_ref, ob_ref):
        s = a_ref[...] + b_ref[...]
        o_ref[...] = s
        ob_ref[...] = s.astype(BF16)

    spec = pl.BlockSpec((1, SUM_TR, w), lambda d, i, c: (d, i, 0))
    return pl.pallas_call(
        body, name=name,
        out_shape=[jax.ShapeDtypeStruct((4, HROWS, w), F32), jax.ShapeDtypeStruct((4, HROWS, w), BF16)],
        grid_spec=pltpu.PrefetchScalarGridSpec(
            num_scalar_prefetch=1, grid=(4, nh),
            in_specs=[pl.BlockSpec((1, SUM_TR, w), lambda d, i, c: (d, c[0] * nh + i, 0)), spec],
            out_specs=[spec, spec]),
        compiler_params=_cp(("parallel", "parallel")))(core.reshape(1).astype(jnp.int32), g, theirs)


def chip_sum(part, chip, others, name):
    r, w = part.shape[1:]

    def body(chip_ref, a_ref, b_ref, o_ref):
        acc = a_ref[0]
        for k in range(3):
            acc = acc + b_ref[k].astype(F32)
        o_ref[...] = acc

    return pl.pallas_call(
        body, name=name, out_shape=jax.ShapeDtypeStruct((r, w), F32),
        grid_spec=pltpu.PrefetchScalarGridSpec(
            num_scalar_prefetch=1, grid=(r // SUM_TR,),
            in_specs=[pl.BlockSpec((1, SUM_TR, w), lambda i, c: (c[0], i, 0)),
                      pl.BlockSpec((3, SUM_TR, w), lambda i, c: (0, i, 0))],
            out_specs=pl.BlockSpec((SUM_TR, w), lambda i, c: (i, 0))),
        compiler_params=_cp(("parallel",)))(chip.reshape(1).astype(jnp.int32), part, others)


def sum_devices(g):
    r = g.shape[1]

    def body(g_ref, o_ref):
        acc = g_ref[0]
        for d in range(1, 8):
            acc = acc + g_ref[d]
        o_ref[...] = acc

    return pl.pallas_call(body, name="sum_devices", out_shape=jax.ShapeDtypeStruct((r, 1024), F32),
                          compiler_params=_cp())(g)


def adamw(w, g, m, v, name):
    r, c = w.shape
    tr = r
    for cand in (256, 128, 64, 32, 16, 8):
        if r % cand == 0 and r > cand:
            tr = cand
            break

    def body(w_ref, g_ref, m_ref, v_ref, d_ref, nm_ref, nv_ref):
        gv = g_ref[...]
        mn = ADAM_B1 * m_ref[...] + (1.0 - ADAM_B1) * gv
        vn = ADAM_B2 * v_ref[...] + (1.0 - ADAM_B2) * (gv * gv)
        m_hat = mn / (1.0 - ADAM_B1 ** ADAM_STEP)
        v_hat = vn / (1.0 - ADAM_B2 ** ADAM_STEP)
        d_ref[...] = -ADAM_LR * (m_hat / (jnp.sqrt(v_hat) + ADAM_EPS) + ADAM_WD * w_ref[...])
        nm_ref[...] = mn
        nv_ref[...] = vn

    spec = pl.BlockSpec((tr, c), lambda i: (i, 0))
    sd = jax.ShapeDtypeStruct((r, c), F32)
    return pl.pallas_call(body, name=name, grid=(r // tr,), in_specs=[spec] * 4, out_specs=[spec] * 3,
                          out_shape=[sd, sd, sd], compiler_params=_cp(("parallel",)))(w, g, m, v)


def adamw_halves(w, mine, theirs, core, m, v, name):
    r, c = w.shape
    tr = 128
    nh = HROWS // tr

    def body(core_ref, w_ref, a_ref, b_ref, m_ref, v_ref, g_ref, d_ref, nm_ref, nv_ref):
        gv = jnp.where(pl.program_id(0) // nh == core_ref[0], a_ref[...], b_ref[...])
        mn = ADAM_B1 * m_ref[...] + (1.0 - ADAM_B1) * gv
        vn = ADAM_B2 * v_ref[...] + (1.0 - ADAM_B2) * (gv * gv)
        m_hat = mn / (1.0 - ADAM_B1 ** ADAM_STEP)
        v_hat = vn / (1.0 - ADAM_B2 ** ADAM_STEP)
        g_ref[...] = gv
        d_ref[...] = -ADAM_LR * (m_hat / (jnp.sqrt(v_hat) + ADAM_EPS) + ADAM_WD * w_ref[...])
        nm_ref[...] = mn
        nv_ref[...] = vn

    spec = pl.BlockSpec((tr, c), lambda i, s: (i, 0))
    half = pl.BlockSpec((tr, c), lambda i, s: (i % nh, 0))
    sd = jax.ShapeDtypeStruct((r, c), F32)
    return pl.pallas_call(
        body, name=name, out_shape=[sd, sd, sd, sd],
        grid_spec=pltpu.PrefetchScalarGridSpec(num_scalar_prefetch=1, grid=(r // tr,),
                                               in_specs=[spec, half, half, spec, spec], out_specs=[spec] * 4),
        compiler_params=_cp(("parallel",)))(core.reshape(1).astype(jnp.int32), w, mine, theirs, m, v)


ANY = pl.BlockSpec(memory_space=pl.ANY)
VM = pl.BlockSpec(memory_space=pltpu.VMEM)
OTHER_CHIPS = ((1, 0), (0, 1), (1, 1))


def _pos():
    return lax.axis_index("x"), lax.axis_index("y"), lax.axis_index("c")


def _flip(v, bit):
    return 1 - v if bit else v


def _rcopy(src, dst, ssem, rsem, peer):
    return pltpu.make_async_remote_copy(src_ref=src, dst_ref=dst, send_sem=ssem, recv_sem=rsem,
                                        device_id=peer, device_id_type=MESH)


def allgather_small(p, name):
    r = p.shape[0]

    def body(in_ref, out_ref, ssem, rsem, lsem):
        x, y, c = _pos()
        me = 4 * x + 2 * y + c
        loc = pltpu.make_async_copy(in_ref, out_ref.at[me], lsem)
        loc.start()
        sends = []
        peers = []
        for k in range(1, 8):
            px, py, pc = _flip(x, (k >> 2) & 1), _flip(y, (k >> 1) & 1), _flip(c, k & 1)
            peers.append((px, py, pc))
            cp = _rcopy(in_ref, out_ref.at[me], ssem.at[k - 1], rsem.at[k - 1], (px, py, pc))
            cp.start()
            sends.append(cp)
        for k in range(1, 8):
            px, py, pc = peers[k - 1]
            _rcopy(in_ref, out_ref.at[4 * px + 2 * py + pc], ssem.at[k - 1], rsem.at[k - 1], (px, py, pc)).wait_recv()
        for cp in sends:
            cp.wait_send()
        loc.wait()

    return pl.pallas_call(
        body, name=name, out_shape=jax.ShapeDtypeStruct((8, r, 1024), F32),
        in_specs=[VM], out_specs=VM,
        scratch_shapes=[pltpu.SemaphoreType.DMA((7,)), pltpu.SemaphoreType.DMA((7,)), pltpu.SemaphoreType.DMA],
    )(p)


def gather_weights(w_in_b, mod_sh):
    def body(wi_ref, m_ref, gi_ref, mo_ref, ssem, rsem, lsem):
        x, y, c = _pos()
        chip = 2 * x + y
        mine = pl.ds(pl.multiple_of(c * HROWS, 16), HROWS)
        other = pl.ds(pl.multiple_of((1 - c) * HROWS, 16), HROWS)
        sib = (x, y, 1 - c)
        pairs = ((wi_ref, gi_ref),)
        loc_m = pltpu.make_async_copy(m_ref, mo_ref.at[chip], lsem)
        loc_m.start()
        sends = []
        for k, (fx, fy) in enumerate(OTHER_CHIPS):
            peer = (_flip(x, fx), _flip(y, fy), c)
            for a, (w_ref, g_ref) in enumerate(pairs):
                cw = _rcopy(w_ref.at[mine], g_ref.at[chip, mine], ssem.at[6 * a + k], rsem.at[6 * a + k], peer)
                cw.start()
                sends.append(cw)
            cm = _rcopy(m_ref, mo_ref.at[chip], ssem.at[12 + k], rsem.at[12 + k], peer)
            cm.start()
            sends.append(cm)
        for k, (fx, fy) in enumerate(OTHER_CHIPS):
            px, py = _flip(x, fx), _flip(y, fy)
            for a, (w_ref, g_ref) in enumerate(pairs):
                got = g_ref.at[2 * px + py, mine]
                _rcopy(w_ref.at[mine], got, ssem.at[6 * a + k], rsem.at[6 * a + k], (px, py, c)).wait_recv()
                fw = _rcopy(got, got, ssem.at[6 * a + 3 + k], rsem.at[6 * a + 3 + k], sib)
                fw.start()
                sends.append(fw)
        for k, (fx, fy) in enumerate(OTHER_CHIPS):
            px, py = _flip(x, fx), _flip(y, fy)
            for a, (w_ref, g_ref) in enumerate(pairs):
                land = g_ref.at[2 * px + py, other]
                _rcopy(land, land, ssem.at[6 * a + 3 + k], rsem.at[6 * a + 3 + k], sib).wait_recv()
            _rcopy(m_ref, mo_ref.at[2 * px + py], ssem.at[12 + k], rsem.at[12 + k], (px, py, c)).wait_recv()
        for cp in sends:
            cp.wait_send()
        loc_m.wait()

    return pl.pallas_call(
        body, name="gather_weights",
        out_shape=[jax.ShapeDtypeStruct((4, D, SH_IN), BF16), jax.ShapeDtypeStruct((4, 8, 768), F32)],
        in_specs=[ANY, VM], out_specs=[ANY, VM],
        scratch_shapes=[pltpu.SemaphoreType.DMA((15,)), pltpu.SemaphoreType.DMA((15,)), pltpu.SemaphoreType.DMA],
    )(w_in_b, mod_sh)


def pair_exchange(g):
    def body(g_ref, r_ref, ssem, rsem):
        x, y, c = _pos()
        other = pl.ds(pl.multiple_of((1 - c) * HROWS, 8), HROWS)
        cp = _rcopy(g_ref.at[:, other, :], r_ref, ssem, rsem, (x, y, 1 - c))
        cp.start()
        cp.wait()

    return pl.pallas_call(
        body, name="pair_exchange", out_shape=jax.ShapeDtypeStruct((4, HROWS, g.shape[2]), F32),
        in_specs=[ANY], out_specs=ANY,
        scratch_shapes=[pltpu.SemaphoreType.DMA, pltpu.SemaphoreType.DMA],
    )(g)


HBM = pl.BlockSpec(memory_space=pltpu.HBM)
SEM = pl.BlockSpec(memory_space=pltpu.SEMAPHORE)
DATAFLOW = pltpu.SideEffectType.DATAFLOW_SIDE_EFFECTING


def split_start(name, make_copies, srcs, lands, nsem, after):
    arrays = [*srcs, *lands]
    n, ns = len(arrays), len(srcs)

    def body(*refs):
        for cp in make_copies(refs[:ns], refs[ns:n], refs[n + 1], refs[n + 2])[0]:
            cp.start()
        refs[-1][...] = jnp.zeros_like(refs[-1])

    res = pl.pallas_call(
        body, name=name,
        out_shape=(pltpu.SemaphoreType.DMA((nsem,)), pltpu.SemaphoreType.DMA((nsem,)),
                   *[pltpu.HBM(a.shape, a.dtype) for a in arrays], jax.ShapeDtypeStruct((8, 128), F32)),
        in_specs=(HBM,) * n + (ANY,), out_specs=(SEM, SEM) + (HBM,) * n + (VM,),
        input_output_aliases={i: 2 + i for i in range(n)},
        compiler_params=pltpu.CompilerParams(has_side_effects=DATAFLOW),
    )(*[pltpu.with_memory_space_constraint(a, pltpu.HBM) for a in arrays], after)
    return res[0], res[1], list(res[2:2 + n]), res[-1]


def split_wait(name, make_copies, ssem, rsem, arrays, ns, after):
    n = len(arrays)

    def body(*refs):
        sends, recvs = make_copies(refs[:ns], refs[ns:n], refs[n], refs[n + 1])
        for cp in sends:
            cp.wait_send()
        for cp in recvs:
            cp.wait_recv()

    return pl.pallas_call(
        body, name=name, out_shape=tuple(pltpu.HBM(a.shape, a.dtype) for a in arrays),
        in_specs=(HBM,) * n + (SEM, SEM, ANY), out_specs=(HBM,) * n,
        input_output_aliases={i: i for i in range(n)},
        compiler_params=pltpu.CompilerParams(has_side_effects=DATAFLOW),
    )(*arrays, ssem, rsem, after)


def _chip_copies(srcs, lands, ssem, rsem):
    x, y, c = _pos()
    copies = []
    for k, (fx, fy) in enumerate(OTHER_CHIPS):
        px, py = _flip(x, fx), _flip(y, fy)
        for a, (p_ref, l_ref) in enumerate(zip(srcs, lands)):
            copies.append(_rcopy(p_ref.at[2 * px + py], l_ref.at[k], ssem.at[3 * a + k], rsem.at[3 * a + k], (px, py, c)))
    return copies, copies


def _pair_copies(srcs, lands, ssem, rsem):
    x, y, c = _pos()
    other = pl.ds(pl.multiple_of((1 - c) * HROWS, 8), HROWS)
    copies = [_rcopy(srcs[0].at[:, other, :], lands[0], ssem.at[0], rsem.at[0], (x, y, 1 - c))]
    return copies, copies


def _rest_copies(srcs, lands, ssem, rsem):
    x, y, c = _pos()
    chip = 2 * x + y
    mine = pl.ds(pl.multiple_of(c * HROWS, 16), HROWS)
    sends, recvs = [], []
    for k, (fx, fy) in enumerate(OTHER_CHIPS):
        px, py = _flip(x, fx), _flip(y, fy)
        for t in range(2):
            rows_t = pl.ds(t * HROWS, HROWS)
            sends.append(_rcopy(srcs[0].at[mine], lands[0].at[chip, mine], ssem.at[2 * k + t], rsem.at[2 * k + c],
                                (px, py, t)))
            recvs.append(_rcopy(srcs[0].at[rows_t], lands[0].at[2 * px + py, rows_t], ssem.at[2 * k + t],
                                rsem.at[2 * k + t], (px, py, t)))
    return sends, recvs


def _swap_copies(srcs, lands, ssem, rsem):
    x, y, c = _pos()
    copies = [_rcopy(s_ref, l_ref, ssem.at[a], rsem.at[a], (x, y, 1 - c))
              for a, (s_ref, l_ref) in enumerate(zip(srcs, lands))]
    return copies, copies


def _flat(v, width=1024):
    v = v.reshape(-1)
    n = -(-v.shape[0] // width) * width
    return jnp.pad(v, (0, n - v.shape[0]))


def _rows(parts, rows):
    flat = jnp.concatenate(parts)
    return jnp.pad(flat, (0, rows * 1024 - flat.shape[0])).reshape(rows, 1024)


def _pack_small(b_ada, norm_w, conv_b, ssm_norm_w, q_norm_w, k_norm_w, sinks, dt_bias, a_log, d_skip, rel_bias,
                extra=None, tail=(), rows=16):
    misc = [q_norm_w, k_norm_w, sinks, dt_bias, a_log, d_skip] + ([] if extra is None else [extra])
    parts = [_flat(b_ada), _flat(norm_w), _flat(conv_b), _flat(ssm_norm_w)] + [_flat(v, 128) for v in misc]
    parts.append(jnp.zeros(((8 - len(misc)) * 128,), F32))
    parts.append(_flat(rel_bias))
    parts.append(jnp.zeros((5 * 1024,), F32))
    return _rows(parts + [_flat(v) for v in tail], rows)


def _unpack_small(p):
    misc = p[9]
    return dict(b_ada=p[0:3].reshape(1, 3072), norm_w=p[3:4], conv_b=p[4:7].reshape(1, 3072),
                ssm_norm_w=p[7:9].reshape(1, 2048), q_norm_w=misc[None, 0:64], k_norm_w=misc[None, 128:192],
                sinks=misc[None, 256:272], dt_bias=misc[None, 384:416], a_log=misc[None, 512:544],
                d_skip=misc[None, 640:672], rel_bias=p[10, :512].reshape(32, 16), extra=misc[768])


SMALL = ("b_ada", "norm_w", "conv_b", "ssm_norm_w", "q_norm_w", "k_norm_w", "sinks", "dt_bias", "a_log", "d_skip",
         "rel_bias")
WEIGHTS = ("w_ada", "b_ada", "norm_w", "w_in", "q_norm_w", "k_norm_w", "rel_bias", "sinks", "conv_w", "conv_b",
           "dt_bias", "a_log", "d_skip", "ssm_norm_w", "w_attn_proj", "w_ssm_proj", "w_out")
IN_COLS = ((0, 1024, C_Q), (1024, 256, C_K), (1280, 256, C_V), (1536, 1024, C_ZA), (2560, 2048, C_ZM),
           (4608, 3072, C_XBC), (7680, 32, C_DT), (7712, 1024, C_GA), (8736, 1024, C_GB))


def _to_cat(shards):
    parts, pos = [], 0
    for o, n, cnew in sorted(IN_COLS, key=lambda e: e[2]):
        assert cnew == pos
        c0 = o
        while c0 < o + n:
            i = c0 // SH_IN
            c1 = min(o + n, (i + 1) * SH_IN)
            parts.append(shards[i][:, c0 - i * SH_IN:c1 - i * SH_IN])
            c0 = c1
        pos += n
    parts.append(jnp.zeros((D, NP - pos), shards.dtype))
    return jnp.concatenate(parts, axis=1)


def _from_cat(dw_pieces):
    starts = [subs[0][1] for _, subs in DPIECES]

    def cols(c0, c1):
        p = max(q for q in range(len(starts)) if starts[q] <= c0)
        return dw_pieces[p][:, c0 - starts[p]:c1 - starts[p]]

    shards = []
    for i in range(4):
        lo, hi = i * SH_IN, (i + 1) * SH_IN
        parts = []
        for o, n, cnew in IN_COLS:
            a, b = max(o, lo), min(o + n, hi)
            if a < b:
                parts.append(cols(cnew + a - o, cnew + b - o))
        shards.append(jnp.concatenate(parts, axis=1))
    return jnp.stack(shards)


def kernel(x, c, w_ada, b_ada, norm_w, w_in, q_norm_w, k_norm_w, rel_bias, sinks, conv_w, conv_b, dt_bias, a_log, d_skip, ssm_norm_w, w_attn_proj, w_ssm_proj, w_out, loss_target, m_w_ada, m_b_ada, m_norm_w, m_w_in, m_q_norm_w, m_k_norm_w, m_rel_bias, m_sinks, m_conv_w, m_conv_b, m_dt_bias, m_a_log, m_d_skip, m_ssm_norm_w, m_w_attn_proj, m_w_ssm_proj, m_w_out, v_w_ada, v_b_ada, v_norm_w, v_w_in, v_q_norm_w, v_k_norm_w, v_rel_bias, v_sinks, v_conv_w, v_conv_b, v_dt_bias, v_a_log, v_d_skip, v_ssm_norm_w, v_w_attn_proj, v_w_ssm_proj, v_w_out):
    args = dict(locals())
    xi, yi, ci = lax.axis_index("x"), lax.axis_index("y"), lax.axis_index("c")
    chip = 2 * xi + yi
    me = 4 * xi + 2 * yi + ci
    x2 = x[0]
    tgt = loss_target[0]

    pay = _rows([c.reshape(-1), conv_w[0].reshape(-1)], 8)
    g0 = allgather_small(pay, "gather_cond")
    c_all = g0[:, 0, :]
    conv_w_full = g0[0::2, 1:4, :].reshape(4, CONV_K, 768).transpose(1, 0, 2).reshape(CONV_K, XBC)

    b_ada_sh = lax.dynamic_slice(b_ada, (0, chip * 768), (1, 768))
    mod_sh = ada_mod(c_all, w_ada[0], b_ada_sh)

    w_in_b = w_in[0].astype(BF16)
    w_rest_b = jnp.concatenate([w_attn_proj[0], w_ssm_proj[0], w_out[0]], axis=0).astype(BF16)
    wg_in, modg = gather_weights(w_in_b, mod_sh)
    wg_in = lax.dynamic_update_slice(wg_in, w_in_b[None], (chip, 0, 0))
    rs_sem, rr_sem, rest_thru, rest_tok = split_start("gather_rest_start", _rest_copies, [w_rest_b],
                                                      [lax.empty((4, D, D), BF16)], 6, modg)
    mod = lax.dynamic_slice(modg, (0, me, 0), (4, 1, 768)).reshape(1, 3 * D)
    shift, scale, gate = mod[:, :D], mod[:, D:2 * D] + rest_tok[:1, :1], mod[:, 2 * D:]
    wcat = _to_cat(wg_in)

    pad128 = lambda v: jnp.pad(v, ((0, 0), (0, 128 - v.shape[1])))
    dtb_p, alog_p, dsk_p = pad128(dt_bias), pad128(a_log), pad128(d_skip)
    bucket = _bucket_table()

    proj, dt_raw, h_t = norm_proj(x2, norm_w, scale, shift, wcat)
    biasm = bias_expand(rel_bias, sinks, bucket)
    ao = attn_fwd(proj, biasm, q_norm_w, k_norm_w)
    act, dsl = conv_fwd(proj, conv_w_full, conv_b)
    yss, sprev = ssd_fwd(act, dt_raw, dtb_p, alog_p, dsk_p)

    w_rest_b, wg_rest = split_wait("gather_rest_wait", _rest_copies, rs_sem, rr_sem, rest_thru, 1, yss)
    wg_rest = lax.dynamic_update_slice(wg_rest, w_rest_b[None], (chip, 0, 0))
    w_at = wg_rest[:, :R_AT].reshape(D, D)
    w_ss = wg_rest[:, R_AT:R_AT + R_SS].reshape(SSM_W, D)
    w_ou = wg_rest[:, R_AT + R_SS:].reshape(D, D)
    (loss_p, dy, dao, dmid, dyss, ua_t, yn_t, mg_t, dya, dyb, dout, dgate, dssm_nw) = tail(
        proj, ao, yss, x2, tgt, gate, ssm_norm_w, w_at, w_ss, w_ou)

    dq, dkv, dqw, dkw, dacc = attn_bwd(proj, dao, biasm, q_norm_w, k_norm_w)
    dbias = bias_reduce(dacc, bucket)
    drb = dbias[:, :NBUCKET].T
    dsk = dbias[:, NBUCKET].reshape(1, HQ)
    dact, ddt, ddtb, dalog, ddskip = ssd_bwd(act, dt_raw, dyss, sprev, dtb_p, alog_p, dsk_p)
    dxbc, dconv_w, dconv_b = conv_bwd(proj, dact, dsl, conv_w_full)

    dproj = (dq, dmid, dxbc, dkv, ddt)
    dwcat = [wgrad(h_t, piece, "dw_in_%d" % p, 1280 if piece.shape[1] == W_MID else min(piece.shape[1], 1024), rest_tok)
             for p, piece in enumerate(dproj)]

    g_in = _from_cat(dwcat)
    ps_sem, pr_sem, pair_thru, pair_tok = split_start("pair_in_start", _pair_copies, [g_in],
                                                      [lax.empty((4, HROWS, SH_IN), F32)], 1, loss_p)
    dw_at = wgrad(ua_t, dya, "dw_attn", 1024, pair_tok)
    dw_ss = wgrad(yn_t, dyb, "dw_ssm", 1024, pair_tok)
    dw_ou = wgrad(mg_t, dout, "dw_out", 1024, pair_tok)
    g_rest = jnp.concatenate([dw_at.reshape(4, R_AT, D), dw_ss.reshape(4, R_SS, D), dw_ou.reshape(4, R_OU, D)], axis=1)
    sib_rest = pair_exchange(g_rest)
    g_in, sib_in = split_wait("pair_in_wait", _pair_copies, ps_sem, pr_sem, pair_thru, 1, sib_rest)
    part_in, pb_in = pair_sum(g_in, ci, sib_in, "pair_sum_in")
    part_rest, pb_rest = pair_sum(g_rest, ci, sib_rest, "pair_sum_rest")
    cs_sem, cr_sem, chip_thru, token = split_start(
        "chip_exchange_start", _chip_copies, [pb_in, pb_rest],
        [lax.empty((3, HROWS, SH_IN), BF16), lax.empty((3, HROWS, D), BF16)], 6, part_rest)
    grad_x, dnorm_w, dscale, dshift = dproj_bwd(dproj, wcat, x2, dy, norm_w, scale + token[:1, :1])
    _, _, oth_in, oth_rest = split_wait("chip_exchange_wait", _chip_copies, cs_sem, cr_sem, chip_thru, 2, dshift)
    red_in = chip_sum(part_in, chip, oth_in, "chip_sum_in")
    red_rest = chip_sum(part_rest, chip, oth_rest, "chip_sum_rest")
    sw_ssem, sw_rsem, swap_thru, swap_tok = split_start(
        "pair_swap_start", _swap_copies, [red_in, red_rest],
        [lax.empty((HROWS, SH_IN), F32), lax.empty((HROWS, D), F32)], 2, red_rest)

    dmod = jnp.concatenate([dshift, dscale, dgate], axis=1)
    gsmall = _pack_small(dmod, dnorm_w, dconv_b, dssm_nw, dqw, dkw, dsk[:, :HQ], ddtb[:, :SH], dalog[:, :SH],
                         ddskip[:, :SH], drb, extra=loss_p[:, :1] + swap_tok[:1, :1], tail=(dconv_w,), rows=32)
    gall = allgather_small(gsmall, "gather_small_grads")
    ssum = sum_devices(gall)
    gs = _unpack_small(ssum[:16])
    loss = gs["extra"]
    dconv_w_sh = lax.dynamic_slice(ssum[16:28].reshape(CONV_K, XBC), (0, chip * 768), (CONV_K, 768))
    dmod_all = gall[:, 0:3, :].reshape(8, 3 * D)
    dw_ada = ada_grad(c_all, lax.dynamic_slice(dmod_all, (0, chip * 768), (8, 768)))

    grads = dict(gs)
    grads["w_ada"] = dw_ada
    grads["conv_w"] = dconv_w_sh

    delta, new_m, new_v = {}, {}, {}

    def step(n):
        delta[n], new_m[n], new_v[n] = adamw(args[n][0], grads[n], args["m_" + n][0], args["v_" + n][0], "adamw_" + n)

    step("w_ada")
    step("conv_w")
    ws = _pack_small(*[args[n] for n in SMALL])
    ms = _pack_small(*[args["m_" + n] for n in SMALL])
    vs = _pack_small(*[args["v_" + n] for n in SMALL])
    d_s, m_s, v_s = adamw(ws, ssum[:16], ms, vs, "adamw_small")
    red_in, red_rest, recv_in, recv_rest = split_wait("pair_swap_wait", _swap_copies, sw_ssem, sw_rsem, swap_thru, 2, d_s)
    d_s, m_s, v_s = _unpack_small(d_s), _unpack_small(m_s), _unpack_small(v_s)
    for n in SMALL:
        delta[n], new_m[n], new_v[n] = d_s[n], m_s[n], v_s[n]
    grads["w_in"], delta["w_in"], new_m["w_in"], new_v["w_in"] = adamw_halves(
        w_in[0], red_in, recv_in, ci, m_w_in[0], v_w_in[0], "adamw_w_in")
    g_shard_rest = jnp.concatenate([jnp.where(ci == 0, red_rest, recv_rest), jnp.where(ci == 0, recv_rest, red_rest)],
                                   axis=0)
    grads["w_attn_proj"] = g_shard_rest[:R_AT]
    grads["w_ssm_proj"] = g_shard_rest[R_AT:R_AT + R_SS]
    grads["w_out"] = g_shard_rest[R_AT + R_SS:]
    for n in ("w_attn_proj", "w_ssm_proj", "w_out"):
        step(n)

    def shaped(n, a):
        return a.reshape(args[n].shape)

    outs = [loss, grad_x[None]]
    for table in (grads, delta, new_m, new_v):
        outs += [shaped(n, table[n]) for n in WEIGHTS]
    return tuple(outs)
```
